```python
import math
import jax, jax.numpy as jnp
from jax import lax
import numpy as np

D_MODEL = 1024
BATCH = 2
SEQ = 8192
DEPTH = 1
DEC_BATCH = 128
DEC_SEQ = 1
PAST_LEN = 16384
PAGE_SIZE = 128

HEAD_DIM = 64
MIX_WIDTH = D_MODEL
RWKV_WIDTH = MIX_WIDTH // 2
N_RWKV_HEADS = RWKV_WIDTH // HEAD_DIM
SWA_WIDTH = MIX_WIDTH // 4
N_SWA_HEADS = SWA_WIDTH // HEAD_DIM
N_SWA_KV_HEADS = N_SWA_HEADS // 2
SWA_GROUP = N_SWA_HEADS // N_SWA_KV_HEADS
SWA_KV_WIDTH = N_SWA_KV_HEADS * HEAD_DIM
MEM_WIDTH = MIX_WIDTH - RWKV_WIDTH - SWA_WIDTH
N_MEM_HEADS = MEM_WIDTH // HEAD_DIM
N_MEM = 256
WINDOW = 128
BLOCK = 128
N_BUCKETS = 32
MAX_DISTANCE = 128
DECAY_LORA = 64
AAA_LORA = 64
GATE_LORA = 128
RWKV_PROJ = 3 * RWKV_WIDTH + DECAY_LORA + AAA_LORA + GATE_LORA
SWA_PROJ = SWA_WIDTH + 2 * SWA_KV_WIDTH
IN_PROJ = RWKV_PROJ + SWA_PROJ + MEM_WIDTH
D_FF = 4 * D_MODEL
NORM_EPS = 1e-6
LNX_EPS = 64e-5
ATTN_SCALE = HEAD_DIM ** -0.5
R_SPLITS = [RWKV_WIDTH, 2 * RWKV_WIDTH, 3 * RWKV_WIDTH, 3 * RWKV_WIDTH + DECAY_LORA, 3 * RWKV_WIDTH + DECAY_LORA + AAA_LORA]

kernel_name = "hymba_rwkv7_swa_sink_memxattn_decode"


def rms_norm(x, g):
    xf = x.astype(jnp.float32)
    y = xf * lax.rsqrt(jnp.mean(xf * xf, -1, keepdims=True) + NORM_EPS)
    return (y * g.astype(jnp.float32)).astype(x.dtype)


def t5_bucket(dist):
    max_exact = N_BUCKETS // 2
    d = jnp.maximum(dist, 1).astype(jnp.float32)
    large = max_exact + (jnp.log(d / max_exact) / math.log(MAX_DISTANCE / max_exact) * (N_BUCKETS - max_exact)).astype(jnp.int32)
    large = jnp.minimum(large, N_BUCKETS - 1)
    return jnp.where(dist < max_exact, dist, large)


def rel_bias_logits(dist, rel_bias):
    b = rel_bias[t5_bucket(jnp.maximum(dist, 0))]
    return jnp.transpose(b, (2, 0, 1)).reshape(N_SWA_KV_HEADS, SWA_GROUP, dist.shape[0], dist.shape[1]).astype(jnp.float32)


def sink_softmax(s, sink):
    m = jnp.maximum(jnp.max(s, -1, keepdims=True), sink)
    e = jnp.exp(s - m)
    return e / (jnp.sum(e, -1, keepdims=True) + jnp.exp(sink - m))


def rwkv7_recurrence(s0, r, w, k, v, a, b):
    def step(s, inp):
        r_t, w_t, k_t, v_t, a_t, b_t = inp
        sa = jnp.einsum('bhvk,bhk->bhv', s, a_t)
        s = s * w_t[:, :, None, :] + sa[..., None] * b_t[:, :, None, :] + v_t[..., None] * k_t[:, :, None, :]
        return s, jnp.einsum('bhvk,bhk->bhv', s, r_t)
    xs = tuple(jnp.moveaxis(t, 1, 0) for t in (r, w, k, v, a, b))
    s_final, ys = lax.scan(step, s0, xs)
    return jnp.moveaxis(ys, 0, 1), s_final


def rwkv7_mix(p, shift0, s0, mu, w0, w_up_w, a0, w_up_a, w_up_g, k_k, k_a, r_k, lnx_w, lnx_b):
    B, T, _ = p.shape
    prev = jnp.concatenate([shift0[:, None, :].astype(p.dtype), p[:, :-1]], axis=1)
    xs = (p + (prev - p) * mu).astype(jnp.float32)
    r, k, v, wd, ad, gd = jnp.split(xs, R_SPLITS, axis=-1)
    heads = lambda t: t.reshape(B, T, N_RWKV_HEADS, HEAD_DIM)
    logw = -jnp.exp(-jax.nn.softplus(-(w0 + jnp.tanh(wd) @ w_up_w)) - 0.5)
    a = jax.nn.sigmoid(a0 + ad @ w_up_a)
    g = jax.nn.sigmoid(gd) @ w_up_g
    kk = heads(k * k_k)
    kk = kk / jnp.maximum(jnp.sqrt(jnp.sum(kk * kk, -1, keepdims=True)), 1e-12)
    k2 = heads(k * (1.0 + (a - 1.0) * k_a))
    rh, vh, ah = heads(r), heads(v), heads(a)
    y, s_new = rwkv7_recurrence(s0.astype(jnp.float32), rh, heads(jnp.exp(logw)), k2, vh, -kk, kk * ah)
    m = jnp.mean(y, -1, keepdims=True)
    var = jnp.mean(jnp.square(y - m), -1, keepdims=True)
    yn = ((y - m) * lax.rsqrt(var + LNX_EPS)).reshape(B, T, RWKV_WIDTH) * lnx_w + lnx_b
    bonus = (jnp.sum(rh * k2 * r_k, -1, keepdims=True) * vh).reshape(B, T, RWKV_WIDTH)
    out = (yn + bonus) * g
    return out.astype(p.dtype), p[:, -1], s_new.astype(s0.dtype)


def swa_prompt(q, k, v, rel_bias, sinks):
    B, T = q.shape[0], q.shape[1]
    nb = T // BLOCK
    qb = q.reshape(B, nb, BLOCK, N_SWA_KV_HEADS, SWA_GROUP, HEAD_DIM)
    kb = k.reshape(B, nb, BLOCK, N_SWA_KV_HEADS, HEAD_DIM)
    vb = v.reshape(B, nb, BLOCK, N_SWA_KV_HEADS, HEAD_DIM)
    pad = ((0, 0), (1, 0), (0, 0), (0, 0), (0, 0))
    kcat = jnp.concatenate([jnp.pad(kb[:, :-1], pad), kb], axis=2)
    vcat = jnp.concatenate([jnp.pad(vb[:, :-1], pad), vb], axis=2)
    s = jnp.einsum('bnqhgd,bnkhd->bnhgqk', qb, kcat).astype(jnp.float32) * ATTN_SCALE
    n = jnp.arange(nb)[:, None, None]
    qi = jnp.arange(BLOCK)[None, :, None]
    kj = jnp.arange(2 * BLOCK)[None, None, :]
    dist = BLOCK + qi - kj
    kpos = (n - 1) * BLOCK + kj
    mask = (dist >= 0) & (dist <= WINDOW) & (kpos >= 0)
    bias = rel_bias_logits(dist[0], rel_bias)
    s = jnp.where(mask[None, :, None, None], s + bias[None, None], -jnp.inf)
    p = sink_softmax(s, sinks.reshape(N_SWA_KV_HEADS, SWA_GROUP, 1, 1).astype(jnp.float32))
    o = jnp.einsum('bnhgqk,bnkhd->bnqhgd', p.astype(v.dtype), vcat)
    return o.reshape(B, T, SWA_WIDTH)


def swa_decode(q, k_new, v_new, k_buf, v_buf, rel_bias, sinks):
    Bd, S = q.shape[0], q.shape[1]
    kc = jnp.concatenate([k_buf.astype(k_new.dtype), k_new], axis=1)
    vc = jnp.concatenate([v_buf.astype(v_new.dtype), v_new], axis=1)
    qpos = PAST_LEN + jnp.arange(S)
    kpos = PAST_LEN - WINDOW + jnp.arange(WINDOW + S)
    dist = qpos[:, None] - kpos[None, :]
    mask = (dist >= 0) & (dist <= WINDOW) & (kpos >= 0)[None, :]
    qg = q.reshape(Bd, S, N_SWA_KV_HEADS, SWA_GROUP, HEAD_DIM)
    s = jnp.einsum('bqhgd,bkhd->bhgqk', qg, kc).astype(jnp.float32) * ATTN_SCALE
    s = jnp.where(mask, s + rel_bias_logits(dist, rel_bias)[None], -jnp.inf)
    p = sink_softmax(s, sinks.reshape(N_SWA_KV_HEADS, SWA_GROUP, 1, 1).astype(jnp.float32))
    o = jnp.einsum('bhgqk,bkhd->bqhgd', p.astype(vc.dtype), vc).reshape(Bd, S, SWA_WIDTH)
    return o, kc[:, -WINDOW:], vc[:, -WINDOW:]


def memory_kv(mem, g, w_kv, k_g):
    Bm, M, _ = mem.shape
    mk, mv = jnp.split(rms_norm(mem, g) @ w_kv, 2, axis=-1)
    mk = rms_norm(mk.reshape(Bm, M, N_MEM_HEADS, HEAD_DIM), k_g)
    return mk, mv.reshape(Bm, M, N_MEM_HEADS, HEAD_DIM)


def mem_attend(q, mk, mv):
    s = jnp.einsum('bqhd,bkhd->bhqk', q, mk.astype(q.dtype)).astype(jnp.float32) * ATTN_SCALE
    p = jax.nn.softmax(s, axis=-1)
    return jnp.einsum('bhqk,bkhd->bqhd', p.astype(q.dtype), mv.astype(q.dtype))


def trunk_layer(x, s_rwkv, s_shift, k_buf, v_buf, mem_k, mem_v, rel_bias, lp):
    B, T, _ = x.shape
    h = rms_norm(x, lp['norm1_g'])
    proj = h @ lp['w_in']
    p_rwkv, p_swa, p_mem = jnp.split(proj, [RWKV_PROJ, RWKV_PROJ + SWA_PROJ], axis=-1)
    y_r, shift_new, s_new = rwkv7_mix(p_rwkv, s_shift, s_rwkv, lp['mu_shift'], lp['w0'], lp['w_up_w'], lp['a0'],
                                      lp['w_up_a'], lp['w_up_g'], lp['k_k'], lp['k_a'], lp['r_k'], lp['lnx_w'], lp['lnx_b'])
    q_s, k_s, v_s = jnp.split(p_swa, [SWA_WIDTH, SWA_WIDTH + SWA_KV_WIDTH], axis=-1)
    q_s = rms_norm(q_s.reshape(B, T, N_SWA_HEADS, HEAD_DIM), lp['q_norm_swa'])
    k_s = rms_norm(k_s.reshape(B, T, N_SWA_KV_HEADS, HEAD_DIM), lp['k_norm_swa'])
    v_s = v_s.reshape(B, T, N_SWA_KV_HEADS, HEAD_DIM)
    if k_buf is None:
        y_s = swa_prompt(q_s, k_s, v_s, rel_bias, lp['sinks'])
        kb, vb = k_s[:, -WINDOW:], v_s[:, -WINDOW:]
    else:
        y_s, kb, vb = swa_decode(q_s, k_s, v_s, k_buf, v_buf, rel_bias, lp['sinks'])
    q_m = rms_norm(p_mem.reshape(B, T, N_MEM_HEADS, HEAD_DIM), lp['q_norm_mem'])
    y_m = mem_attend(q_m, mem_k, mem_v).reshape(B, T, MEM_WIDTH)
    mix = jnp.concatenate([y_r.astype(x.dtype), y_s.astype(x.dtype), y_m.astype(x.dtype)], axis=-1)
    x = x + mix @ lp['w_out']
    h2 = rms_norm(x, lp['norm2_g'])
    x = x + jnp.square(jax.nn.relu(h2 @ lp['w_ff1'])) @ lp['w_ff2']
    return x, s_new, shift_new, kb, vb


def setup_inputs(seed: int = 0) -> dict:
    key = jax.random.key(seed)
    ks = iter(jax.random.split(key, 48))
    nrm = lambda shape, scale: jax.random.normal(next(ks), shape, jnp.float32) * scale
    return {
        "x_prompt": nrm((BATCH, SEQ, D_MODEL), 1.0),
        "x_sample": nrm((DEC_BATCH, DEC_SEQ, D_MODEL), 1.0),
        "state_rwkv": nrm((DEPTH, DEC_BATCH, N_RWKV_HEADS, HEAD_DIM, HEAD_DIM), 0.3),
        "state_shift": nrm((DEPTH, DEC_BATCH, RWKV_PROJ), 1.0),
        "cache_swa_k": nrm((DEPTH, DEC_BATCH, WINDOW, N_SWA_KV_HEADS, HEAD_DIM), 1.0),
        "cache_swa_v": nrm((DEPTH, DEC_BATCH, WINDOW, N_SWA_KV_HEADS, HEAD_DIM), 1.0),
        "cache_mem_k": nrm((DEPTH, DEC_BATCH, N_MEM, N_MEM_HEADS, HEAD_DIM), 1.0),
        "cache_mem_v": nrm((DEPTH, DEC_BATCH, N_MEM, N_MEM_HEADS, HEAD_DIM), 1.0),
        "mem_prompt": nrm((BATCH, N_MEM, D_MODEL), 1.0),
        "rel_bias": nrm((N_BUCKETS, N_SWA_HEADS), 0.5),
        "norm1_g": 1.0 + nrm((DEPTH, D_MODEL), 0.02),
        "w_in": nrm((DEPTH, D_MODEL, IN_PROJ), D_MODEL ** -0.5),
        "mu_shift": jax.random.uniform(next(ks), (DEPTH, RWKV_PROJ), jnp.float32, 0.0, 1.0),
        "w0": nrm((DEPTH, RWKV_WIDTH), 0.5),
        "w_up_w": nrm((DEPTH, DECAY_LORA, RWKV_WIDTH), 0.1),
        "a0": nrm((DEPTH, RWKV_WIDTH), 0.5),
        "w_up_a": nrm((DEPTH, AAA_LORA, RWKV_WIDTH), AAA_LORA ** -0.5),
        "w_up_g": nrm((DEPTH, GATE_LORA, RWKV_WIDTH), GATE_LORA ** -0.5),
        "k_k": 0.85 + nrm((DEPTH, RWKV_WIDTH), 0.02),
        "k_a": 1.0 + nrm((DEPTH, RWKV_WIDTH), 0.02),
        "r_k": nrm((DEPTH, N_RWKV_HEADS, HEAD_DIM), 0.1),
        "lnx_w": 1.0 + nrm((DEPTH, RWKV_WIDTH), 0.02),
        "lnx_b": nrm((DEPTH, RWKV_WIDTH), 0.02),
        "q_norm_swa": 1.0 + nrm((DEPTH, HEAD_DIM), 0.02),
        "k_norm_swa": 1.0 + nrm((DEPTH, HEAD_DIM), 0.02),
        "sinks": nrm((DEPTH, N_SWA_HEADS), 0.5),
        "mem_norm_g": 1.0 + nrm((DEPTH, D_MODEL), 0.02),
        "w_mem_kv": nrm((DEPTH, D_MODEL, 2 * MEM_WIDTH), D_MODEL ** -0.5),
        "q_norm_mem": 1.0 + nrm((DEPTH, HEAD_DIM), 0.02),
        "k_norm_mem": 1.0 + nrm((DEPTH, HEAD_DIM), 0.02),
        "w_out": nrm((DEPTH, MIX_WIDTH, D_MODEL), MIX_WIDTH ** -0.5),
        "norm2_g": 1.0 + nrm((DEPTH, D_MODEL), 0.02),
        "w_ff1": nrm((DEPTH, D_MODEL, D_FF), D_MODEL ** -0.5),
        "w_ff2": nrm((DEPTH, D_FF, D_MODEL), D_FF ** -0.5),
    }


def reference(x_prompt, x_sample, state_rwkv, state_shift, cache_swa_k, cache_swa_v, cache_mem_k, cache_mem_v,
              mem_prompt, rel_bias, norm1_g, w_in, mu_shift, w0, w_up_w, a0, w_up_a, w_up_g, k_k, k_a, r_k,
              lnx_w, lnx_b, q_norm_swa, k_norm_swa, sinks, mem_norm_g, w_mem_kv, q_norm_mem, k_norm_mem,
              w_out, norm2_g, w_ff1, w_ff2):
    yp, ys = x_prompt, x_sample
    srp, shp, kbp, vbp, mkp, mvp = [], [], [], [], [], []
    srs, shs, kbs, vbs = [], [], [], []
    for l in range(DEPTH):
        lp = dict(norm1_g=norm1_g[l], w_in=w_in[l], mu_shift=mu_shift[l], w0=w0[l], w_up_w=w_up_w[l], a0=a0[l],
                  w_up_a=w_up_a[l], w_up_g=w_up_g[l], k_k=k_k[l], k_a=k_a[l], r_k=r_k[l], lnx_w=lnx_w[l],
                  lnx_b=lnx_b[l], q_norm_swa=q_norm_swa[l], k_norm_swa=k_norm_swa[l], sinks=sinks[l],
                  q_norm_mem=q_norm_mem[l], w_out=w_out[l], norm2_g=norm2_g[l], w_ff1=w_ff1[l], w_ff2=w_ff2[l])
        mk, mv = memory_kv(mem_prompt, mem_norm_g[l], w_mem_kv[l], k_norm_mem[l])
        s0 = jnp.zeros((BATCH, N_RWKV_HEADS, HEAD_DIM, HEAD_DIM), x_prompt.dtype)
        sh0 = jnp.zeros((BATCH, RWKV_PROJ), x_prompt.dtype)
        yp, s_new, sh_new, kb, vb = trunk_layer(yp, s0, sh0, None, None, mk, mv, rel_bias, lp)
        srp.append(s_new); shp.append(sh_new); kbp.append(kb); vbp.append(vb); mkp.append(mk); mvp.append(mv)
        ys, s_new, sh_new, kb, vb = trunk_layer(ys, state_rwkv[l], state_shift[l], cache_swa_k[l], cache_swa_v[l],
                                                cache_mem_k[l], cache_mem_v[l], rel_bias, lp)
        srs.append(s_new); shs.append(sh_new); kbs.append(kb); vbs.append(vb)
    return (yp, ys, jnp.stack(srp), jnp.stack(shp), jnp.stack(kbp), jnp.stack(vbp), jnp.stack(mkp), jnp.stack(mvp),
            jnp.stack(srs), jnp.stack(shs), jnp.stack(kbs), jnp.stack(vbs))
```

```python
import functools
import math

import numpy as np
import jax
import jax.numpy as jnp
from jax import lax
from jax.experimental import pallas as pl
from jax.experimental.pallas import tpu as pltpu

F32 = jnp.float32
BF16 = jnp.bfloat16

D_MODEL = 1024
HEAD_DIM = 64
RWKV_WIDTH = 512
N_RWKV_HEADS = 8
SWA_WIDTH = 256
N_SWA_HEADS = 4
N_SWA_KV_HEADS = 2
SWA_KV_WIDTH = 128
MEM_WIDTH = 256
N_MEM_HEADS = 4
N_MEM = 256
WINDOW = 128
BLOCK = 128
N_BUCKETS = 32
MAX_DISTANCE = 128
DECAY_LORA = 64
AAA_LORA = 64
GATE_LORA = 128
RWKV_PROJ = 3 * RWKV_WIDTH + DECAY_LORA + AAA_LORA + GATE_LORA
SWA_PROJ = SWA_WIDTH + 2 * SWA_KV_WIDTH
IN_PROJ = RWKV_PROJ + SWA_PROJ + MEM_WIDTH
D_FF = 4 * D_MODEL
NORM_EPS = 1e-6
LNX_EPS = 64e-5
ATTN_SCALE = HEAD_DIM ** -0.5
EXP_M05 = math.exp(-0.5)
NEG = -1e30

COL_R, COL_K, COL_V = 0, RWKV_WIDTH, 2 * RWKV_WIDTH
COL_WD = 3 * RWKV_WIDTH
COL_AD = COL_WD + DECAY_LORA
COL_GD = COL_AD + AAA_LORA
COL_SQ = RWKV_PROJ
COL_SK = COL_SQ + SWA_WIDTH
COL_SV = COL_SK + SWA_KV_WIDTH
COL_MQ = RWKV_PROJ + SWA_PROJ

CHUNK = 64
VMEM_LIMIT = 56 * 1024 * 1024


def _dot(a, b):
    return jnp.dot(a.astype(BF16), b.astype(BF16), preferred_element_type=F32)


def _dot_nt(a, b):
    return lax.dot_general(a.astype(BF16), b.astype(BF16), (((1,), (1,)), ((), ())),
                           preferred_element_type=F32)


def _dot_tn(a, b):
    return lax.dot_general(a.astype(BF16), b.astype(BF16), (((0,), (0,)), ((), ())),
                           preferred_element_type=F32)


def _rms(x, g):
    return x * lax.rsqrt(jnp.mean(x * x, -1, keepdims=True) + NORM_EPS) * g


def _params(*sem):
    return pltpu.CompilerParams(dimension_semantics=sem, vmem_limit_bytes=VMEM_LIMIT)


def _full(shape):
    n = len(shape)
    return pl.BlockSpec(shape, lambda *_: (0,) * n)


def _in_proj_kernel(x_ref, g_ref, w_ref, o_ref):
    h = _rms(x_ref[...], g_ref[...])
    o_ref[...] = jnp.dot(h.astype(BF16), w_ref[...], preferred_element_type=F32)


def _in_proj(x2d, g, w_bf16, tm):
    n = x2d.shape[0]
    return pl.pallas_call(
        _in_proj_kernel,
        grid=(n // tm,),
        in_specs=[pl.BlockSpec((tm, D_MODEL), lambda i: (i, 0)),
                  _full((1, D_MODEL)),
                  _full((D_MODEL, IN_PROJ))],
        out_specs=pl.BlockSpec((tm, IN_PROJ), lambda i: (i, 0)),
        out_shape=jax.ShapeDtypeStruct((n, IN_PROJ), F32),
        compiler_params=_params("arbitrary"),
        name="in_proj",
    )(x2d, g, w_bf16)


def _rwkv_features(xs, w0, a0, k_k, k_a, wupw, wupa, wupg):
    r = xs[:, COL_R:COL_R + RWKV_WIDTH]
    k = xs[:, COL_K:COL_K + RWKV_WIDTH]
    v = xs[:, COL_V:COL_V + RWKV_WIDTH]
    wd = xs[:, COL_WD:COL_WD + DECAY_LORA]
    ad = xs[:, COL_AD:COL_AD + AAA_LORA]
    gd = xs[:, COL_GD:COL_GD + GATE_LORA]
    logw = -jax.nn.sigmoid(w0 + _dot(jnp.tanh(wd), wupw)) * EXP_M05
    a_sig = jax.nn.sigmoid(a0 + _dot(ad, wupa))
    gate = _dot(jax.nn.sigmoid(gd), wupg)
    kk = k * k_k
    k2 = k * (1.0 + (a_sig - 1.0) * k_a)
    return r, k2, v, kk, a_sig, logw, gate


def _head_out(y, r, k2, v, gate, r_k, lnx_w, lnx_b):
    m = jnp.mean(y, -1, keepdims=True)
    d = y - m
    var = jnp.mean(d * d, -1, keepdims=True)
    yn = d * lax.rsqrt(var + LNX_EPS) * lnx_w + lnx_b
    bonus = jnp.sum(r * k2 * r_k, -1, keepdims=True) * v
    return (yn + bonus) * gate


def _rwkv_prompt_kernel(p_ref, mu_ref, w0_ref, a0_ref, kk_ref, ka_ref, rk_ref, lnw_ref, lnb_ref,
                        wupw_ref, wupa_ref, wupg_ref, y_ref, sout_ref, s_scr, prev_scr):
    C = CHUNK
    t = pl.program_id(1)

    @pl.when(t == 0)
    def _():
        s_scr[...] = jnp.zeros_like(s_scr)
        prev_scr[...] = jnp.zeros_like(prev_scr)

    p = p_ref[0]
    row = lax.broadcasted_iota(jnp.int32, p.shape, 0)
    prev = jnp.where(row == 0, prev_scr[...], pltpu.roll(p, 1, axis=0))
    prev_scr[...] = p[C - 1:C, :]
    xs = p + (prev - p) * mu_ref[...]
    r, k2, v, kk, a_sig, logw, gate = _rwkv_features(
        xs, w0_ref[...], a0_ref[...], kk_ref[...], ka_ref[...], wupw_ref[...], wupa_ref[...], wupg_ref[...])

    ri = lax.broadcasted_iota(jnp.int32, (C, C), 0)
    ci = lax.broadcasted_iota(jnp.int32, (C, C), 1)
    tri = jnp.where(ri >= ci, 1.0, 0.0).astype(BF16)
    l1 = logw.astype(BF16)
    rem = logw - l1.astype(F32)
    l2 = rem.astype(BF16)
    l3 = (rem - l2.astype(F32)).astype(BF16)
    cum = (jnp.dot(tri, l1, preferred_element_type=F32) + jnp.dot(tri, l2, preferred_element_type=F32)
           + jnp.dot(tri, l3, preferred_element_type=F32))
    c_last = cum[C - 1:C, :]
    e_pos = jnp.exp(cum)
    e_neg = jnp.exp(-cum)
    e_prev = jnp.exp(cum - logw)
    e_last = jnp.exp(c_last - cum)
    w_last = jnp.exp(c_last)

    r2 = lax.broadcasted_iota(jnp.int32, (C, 2 * C), 0)
    c2 = lax.broadcasted_iota(jnp.int32, (C, 2 * C), 1)
    c2m = jnp.where(c2 >= C, c2 - C, c2)
    mask_ak = jnp.logical_and(c2 >= C, c2m < r2)
    mask_r = c2m <= r2
    strict = ci < ri

    for h in range(N_RWKV_HEADS):
        sl = slice(h * HEAD_DIM, (h + 1) * HEAD_DIM)
        kk_h = kk[:, sl]
        kk_h = kk_h / jnp.maximum(jnp.sqrt(jnp.sum(kk_h * kk_h, -1, keepdims=True)), 1e-12)
        b_h = kk_h * a_sig[:, sl]
        k2_h = k2[:, sl]
        r_h = r[:, sl]
        v_h = v[:, sl]
        at = -kk_h * e_prev[:, sl]
        bt = b_h * e_neg[:, sl]
        kt = k2_h * e_neg[:, sl]
        rt = r_h * e_pos[:, sl]
        bh = b_h * e_last[:, sl]
        kh = k2_h * e_last[:, sl]
        S = s_scr[h]

        sc = _dot_nt(jnp.concatenate([at, rt], axis=0), jnp.concatenate([bt, kt], axis=0))
        top = sc[:C]
        A = jnp.where(strict, top[:, :C], 0.0)
        m_ak = jnp.where(mask_ak, top, 0.0)
        m_r = jnp.where(mask_r, sc[C:], 0.0)

        X = _dot_nt(at, S) + _dot(m_ak, jnp.concatenate([v_h, v_h], axis=0))
        X = X + _dot(A, X)
        steps = int(math.log2(C)) - 1
        for _ in range(steps):
            A = _dot(A, A)
            X = X + _dot(A, X)
        uv = jnp.concatenate([X, v_h], axis=0)
        y = _dot_nt(rt, S) + _dot(m_r, uv)
        s_scr[h] = S * w_last[:, sl] + _dot_tn(uv, jnp.concatenate([bh, kh], axis=0))

        y_ref[0, :, sl] = _head_out(y, r_h, k2_h, v_h, gate[:, sl], rk_ref[:, sl], lnw_ref[:, sl], lnb_ref[:, sl])

    @pl.when(t == pl.num_programs(1) - 1)
    def _():
        sout_ref[0] = s_scr[...]


def _rwkv_prompt(proj, mu, w0, a0, k_k, k_a, r_k, lnx_w, lnx_b, wupw, wupa, wupg):
    B, T, _ = proj.shape
    vec = lambda n: _full((1, n))
    return pl.pallas_call(
        _rwkv_prompt_kernel,
        grid=(B, T // CHUNK),
        in_specs=[pl.BlockSpec((1, CHUNK, RWKV_PROJ), lambda b, t: (b, t, 0)),
                  vec(RWKV_PROJ), vec(RWKV_WIDTH), vec(RWKV_WIDTH), vec(RWKV_WIDTH), vec(RWKV_WIDTH),
                  vec(RWKV_WIDTH), vec(RWKV_WIDTH), vec(RWKV_WIDTH),
                  _full((DECAY_LORA, RWKV_WIDTH)), _full((AAA_LORA, RWKV_WIDTH)), _full((GATE_LORA, RWKV_WIDTH))],
        out_specs=[pl.BlockSpec((1, CHUNK, RWKV_WIDTH), lambda b, t: (b, t, 0)),
                   pl.BlockSpec((1, N_RWKV_HEADS, HEAD_DIM, HEAD_DIM), lambda b, t: (b, 0, 0, 0))],
        out_shape=[jax.ShapeDtypeStruct((B, T, RWKV_WIDTH), F32),
                   jax.ShapeDtypeStruct((B, N_RWKV_HEADS, HEAD_DIM, HEAD_DIM), F32)],
        scratch_shapes=[pltpu.VMEM((N_RWKV_HEADS, HEAD_DIM, HEAD_DIM), F32),
                        pltpu.VMEM((1, RWKV_PROJ), F32)],
        compiler_params=_params("arbitrary", "arbitrary"),
        name="rwkv_prompt",
    )(proj, mu, w0, a0, k_k, k_a, r_k, lnx_w, lnx_b, wupw, wupa, wupg)


SEQ_TILE = 8


def _rwkv_step_kernel(p_ref, sh_ref, s_ref, mu_ref, w0_ref, a0_ref, kk_ref, ka_ref, rk_ref, lnw_ref, lnb_ref,
                      wupw_ref, wupa_ref, wupg_ref, y_ref, sout_ref):
    p = p_ref[...]
    xs = p + (sh_ref[...] - p) * mu_ref[...]
    r, k2, v, kk, a_sig, logw, gate = _rwkv_features(
        xs, w0_ref[...], a0_ref[...], kk_ref[...], ka_ref[...], wupw_ref[...], wupa_ref[...], wupg_ref[...])
    w = jnp.exp(logw)
    ri = lax.broadcasted_iota(jnp.int32, (HEAD_DIM, HEAD_DIM), 0)
    ci = lax.broadcasted_iota(jnp.int32, (HEAD_DIM, HEAD_DIM), 1)
    eye = ri == ci
    for h in range(N_RWKV_HEADS):
        sl = slice(h * HEAD_DIM, (h + 1) * HEAD_DIM)
        kk_h = kk[:, sl]
        kk_h = kk_h / jnp.maximum(jnp.sqrt(jnp.sum(kk_h * kk_h, -1, keepdims=True)), 1e-12)
        a_h = -kk_h
        b_h = kk_h * a_sig[:, sl]
        ys = []
        for b in range(SEQ_TILE):
            S = s_ref[b, h]
            sa = jnp.sum(S * a_h[b:b + 1], -1, keepdims=True)
            v_col = jnp.sum(jnp.where(eye, v[b:b + 1, sl], 0.0), -1, keepdims=True)
            S = S * w[b:b + 1, sl] + sa * b_h[b:b + 1] + v_col * k2[b:b + 1, sl]
            sout_ref[b, h] = S
            y_col = jnp.sum(S * r[b:b + 1, sl], -1, keepdims=True)
            ys.append(jnp.sum(jnp.where(eye, y_col, 0.0), 0, keepdims=True))
        y = jnp.concatenate(ys, axis=0)
        y_ref[:, sl] = _head_out(y, r[:, sl], k2[:, sl], v[:, sl], gate[:, sl],
                                 rk_ref[:, sl], lnw_ref[:, sl], lnb_ref[:, sl])


def _rwkv_step(proj, shift, state, mu, w0, a0, k_k, k_a, r_k, lnx_w, lnx_b, wupw, wupa, wupg):
    n = proj.shape[0]
    vec = lambda m: _full((1, m))
    st_spec = pl.BlockSpec((SEQ_TILE, N_RWKV_HEADS, HEAD_DIM, HEAD_DIM), lambda i: (i, 0, 0, 0))
    return pl.pallas_call(
        _rwkv_step_kernel,
        grid=(n // SEQ_TILE,),
        in_specs=[pl.BlockSpec((SEQ_TILE, RWKV_PROJ), lambda i: (i, 0)),
                  pl.BlockSpec((SEQ_TILE, RWKV_PROJ), lambda i: (i, 0)),
                  st_spec,
                  vec(RWKV_PROJ), vec(RWKV_WIDTH), vec(RWKV_WIDTH), vec(RWKV_WIDTH), vec(RWKV_WIDTH),
                  vec(RWKV_WIDTH), vec(RWKV_WIDTH), vec(RWKV_WIDTH),
                  _full((DECAY_LORA, RWKV_WIDTH)), _full((AAA_LORA, RWKV_WIDTH)), _full((GATE_LORA, RWKV_WIDTH))],
        out_specs=[pl.BlockSpec((SEQ_TILE, RWKV_WIDTH), lambda i: (i, 0)), st_spec],
        out_shape=[jax.ShapeDtypeStruct((n, RWKV_WIDTH), F32),
                   jax.ShapeDtypeStruct(state.shape, F32)],
        compiler_params=_params("arbitrary"),
        name="rwkv_step",
    )(proj, shift, state, mu, w0, a0, k_k, k_a, r_k, lnx_w, lnx_b, wupw, wupa, wupg)


def _t5_bucket_np(dist):
    max_exact = N_BUCKETS // 2
    d = np.maximum(dist, 1).astype(np.float32)
    large = max_exact + (np.log(d / np.float32(max_exact)) / np.float32(math.log(MAX_DISTANCE / max_exact))
                         * np.float32(N_BUCKETS - max_exact)).astype(np.int32)
    large = np.minimum(large, N_BUCKETS - 1)
    return np.where(dist < max_exact, dist, large).astype(np.int32)


def _prompt_bucket_table():
    qi = np.arange(BLOCK)[:, None]
    kj = np.arange(2 * BLOCK)[None, :]
    dist = BLOCK + qi - kj
    valid = (dist >= 0) & (dist <= WINDOW)
    return np.where(valid, _t5_bucket_np(np.maximum(dist, 0)), -1).astype(np.int32)


def _decode_bucket_table():
    dist = WINDOW - np.arange(WINDOW)
    return np.broadcast_to(_t5_bucket_np(dist)[None, :], (8, WINDOW)).astype(np.int32).copy()


def _bias_from_buckets(bkt, relb_ref, h, init):
    acc = jnp.full(bkt.shape, init, F32)
    for b in range(N_BUCKETS):
        acc = jnp.where(bkt == b, relb_ref[b, h], acc)
    return acc


def _swa_prompt_kernel(q_ref, kc_ref, kp_ref, vc_ref, vp_ref, bkt_ref, relb_ref, sink_ref, qg_ref, kg_ref,
                       y_ref, kn_ref, bias_scr):
    i = pl.program_id(1)

    @pl.when(i == 0)
    def _():
        bkt = bkt_ref[...]
        for h in range(N_SWA_HEADS):
            bias_scr[h] = _bias_from_buckets(bkt, relb_ref, h, NEG)

    q = q_ref[0]
    kc = kc_ref[0]
    kp = kp_ref[0]
    vc = vc_ref[0]
    vp = vp_ref[0]
    qg = qg_ref[...]
    kg = kg_ref[...]
    col = lax.broadcasted_iota(jnp.int32, (BLOCK, 2 * BLOCK), 1)
    pad_mask = jnp.logical_and(i == 0, col < BLOCK)
    for j in range(N_SWA_KV_HEADS):
        sl = slice(j * HEAD_DIM, (j + 1) * HEAD_DIM)
        kn_c = _rms(kc[:, sl], kg)
        kn_ref[0, :, sl] = kn_c
        kcat = jnp.concatenate([_rms(kp[:, sl], kg), kn_c], axis=0)
        vcat = jnp.concatenate([vp[:, sl], vc[:, sl]], axis=0)
        for g in range(N_SWA_HEADS // N_SWA_KV_HEADS):
            hq = j * (N_SWA_HEADS // N_SWA_KV_HEADS) + g
            qs = slice(hq * HEAD_DIM, (hq + 1) * HEAD_DIM)
            qh = _rms(q[:, qs], qg) * ATTN_SCALE
            s = _dot_nt(qh, kcat) + bias_scr[hq]
            s = jnp.where(pad_mask, NEG, s)
            sink = sink_ref[0, hq]
            m = jnp.maximum(jnp.max(s, -1, keepdims=True), sink)
            e = jnp.exp(s - m)
            den = jnp.sum(e, -1, keepdims=True) + jnp.exp(sink - m)
            y_ref[0, :, qs] = _dot(e, vcat) / den


def _swa_prompt(proj, rel_bias, sinks, qg, kg):
    B, T, _ = proj.shape
    nb = T // BLOCK
    bkt = jnp.asarray(_prompt_bucket_table())
    smem = pl.BlockSpec(memory_space=pltpu.SMEM)
    kblk, vblk = COL_SK // SWA_KV_WIDTH, COL_SV // SWA_KV_WIDTH
    return pl.pallas_call(
        _swa_prompt_kernel,
        grid=(B, nb),
        in_specs=[pl.BlockSpec((1, BLOCK, SWA_WIDTH), lambda b, i: (b, i, COL_SQ // SWA_WIDTH)),
                  pl.BlockSpec((1, BLOCK, SWA_KV_WIDTH), lambda b, i: (b, i, kblk)),
                  pl.BlockSpec((1, BLOCK, SWA_KV_WIDTH), lambda b, i: (b, jnp.maximum(i - 1, 0), kblk)),
                  pl.BlockSpec((1, BLOCK, SWA_KV_WIDTH), lambda b, i: (b, i, vblk)),
                  pl.BlockSpec((1, BLOCK, SWA_KV_WIDTH), lambda b, i: (b, jnp.maximum(i - 1, 0), vblk)),
                  _full((BLOCK, 2 * BLOCK)), smem, smem, _full((1, HEAD_DIM)), _full((1, HEAD_DIM))],
        out_specs=[pl.BlockSpec((1, BLOCK, SWA_WIDTH), lambda b, i: (b, i, 0)),
                   pl.BlockSpec((1, BLOCK, SWA_KV_WIDTH), lambda b, i: (b, 0, 0))],
        out_shape=[jax.ShapeDtypeStruct((B, T, SWA_WIDTH), F32),
                   jax.ShapeDtypeStruct((B, BLOCK, SWA_KV_WIDTH), F32)],
        scratch_shapes=[pltpu.VMEM((N_SWA_HEADS, BLOCK, 2 * BLOCK), F32)],
        compiler_params=_params("arbitrary", "arbitrary"),
        name="swa_prompt",
    )(proj, proj, proj, proj, proj, bkt, rel_bias, sinks, qg, kg)


MEM_Q_TILE = 512


def _mem_prompt_kernel(q_ref, mk_ref, mv_ref, qg_ref, y_ref):
    q = q_ref[0]
    mk = mk_ref[0]
    mv = mv_ref[0]
    qg = qg_ref[...]
    for h in range(N_MEM_HEADS):
        sl = slice(h * HEAD_DIM, (h + 1) * HEAD_DIM)
        qh = _rms(q[:, sl], qg) * ATTN_SCALE
        s = _dot_nt(qh, mk[:, sl])
        e = jnp.exp(s - jnp.max(s, -1, keepdims=True))
        y_ref[0, :, sl] = _dot(e, mv[:, sl]) / jnp.sum(e, -1, keepdims=True)


def _mem_prompt(proj, mk, mv, qg):
    B, T, _ = proj.shape
    return pl.pallas_call(
        _mem_prompt_kernel,
        grid=(B, T // MEM_Q_TILE),
        in_specs=[pl.BlockSpec((1, MEM_Q_TILE, MEM_WIDTH), lambda b, i: (b, i, COL_MQ // MEM_WIDTH)),
                  pl.BlockSpec((1, N_MEM, MEM_WIDTH), lambda b, i: (b, 0, 0)),
                  pl.BlockSpec((1, N_MEM, MEM_WIDTH), lambda b, i: (b, 0, 0)),
                  _full((1, HEAD_DIM))],
        out_specs=pl.BlockSpec((1, MEM_Q_TILE, MEM_WIDTH), lambda b, i: (b, i, 0)),
        out_shape=jax.ShapeDtypeStruct((B, T, MEM_WIDTH), F32),
        compiler_params=_params("arbitrary", "arbitrary"),
        name="mem_prompt",
    )(proj, mk, mv, qg)


def _memory_kv_kernel(mem_ref, g_ref, w_ref, kg_ref, mk_ref, mv_ref):
    kv = jnp.dot(_rms(mem_ref[0], g_ref[...]).astype(BF16), w_ref[...], preferred_element_type=F32)
    kg = kg_ref[...]
    for h in range(N_MEM_HEADS):
        sl = slice(h * HEAD_DIM, (h + 1) * HEAD_DIM)
        mk_ref[0, :, sl] = _rms(kv[:, sl], kg)
    mv_ref[0] = kv[:, MEM_WIDTH:]


def _memory_kv(mem, g, w_bf16, kg):
    B = mem.shape[0]
    blk = pl.BlockSpec((1, N_MEM, MEM_WIDTH), lambda b: (b, 0, 0))
    return pl.pallas_call(
        _memory_kv_kernel,
        grid=(B,),
        in_specs=[pl.BlockSpec((1, N_MEM, D_MODEL), lambda b: (b, 0, 0)),
                  _full((1, D_MODEL)), _full((D_MODEL, 2 * MEM_WIDTH)), _full((1, HEAD_DIM))],
        out_specs=[blk, blk],
        out_shape=[jax.ShapeDtypeStruct((B, N_MEM, MEM_WIDTH), F32)] * 2,
        compiler_params=_params("arbitrary"),
        name="memory_kv",
    )(mem, g, w_bf16, kg)


def _head_rows(x_row, n_heads):
    return jnp.concatenate([x_row[:, h * HEAD_DIM:(h + 1) * HEAD_DIM] for h in range(n_heads)], axis=0)


def _spread(xh, n_groups, rows_per_group):
    n = xh.shape[0]
    tiled = jnp.concatenate([xh] * n_groups, axis=1)
    rowi = lax.broadcasted_iota(jnp.int32, tiled.shape, 0)
    lane_grp = lax.broadcasted_iota(jnp.int32, tiled.shape, 1) // HEAD_DIM
    return jnp.where(lane_grp == rowi // rows_per_group, tiled, 0.0)


def _gather_groups(full, n_groups, rows_per_group):
    n = full.shape[0]
    rowi = lax.broadcasted_iota(jnp.int32, (n, HEAD_DIM), 0)
    out = jnp.zeros((n, HEAD_DIM), F32)
    for g in range(n_groups):
        out = out + jnp.where(rowi // rows_per_group == g, full[:, g * HEAD_DIM:(g + 1) * HEAD_DIM], 0.0)
    return out


def _decode_attn_kernel(p_ref, kbuf_ref, vbuf_ref, mk_ref, mv_ref, bkt_ref, relb_ref, sink_ref,
                        sqg_ref, skg_ref, mqg_ref, ys_ref, ym_ref, kout_ref, vout_ref):
    grp = N_SWA_HEADS // N_SWA_KV_HEADS
    hrow = lax.broadcasted_iota(jnp.int32, (8, WINDOW), 0)
    bias_w = jnp.zeros((8, WINDOW), F32)
    for h in range(N_SWA_HEADS):
        bias_w = jnp.where(hrow == h, _bias_from_buckets(bkt_ref[...], relb_ref, h, 0.0), bias_w)
    bias_w = bias_w[:N_SWA_HEADS]
    hcol = lax.broadcasted_iota(jnp.int32, (N_SWA_HEADS, 1), 0)
    bias_new = jnp.zeros((N_SWA_HEADS, 1), F32)
    sink = jnp.zeros((N_SWA_HEADS, 1), F32)
    for h in range(N_SWA_HEADS):
        bias_new = jnp.where(hcol == h, relb_ref[0, h], bias_new)
        sink = jnp.where(hcol == h, sink_ref[0, h], sink)
    rowi = lax.broadcasted_iota(jnp.int32, (WINDOW, SWA_KV_WIDTH), 0)

    for b in range(SEQ_TILE):
        prow = p_ref[b:b + 1, :]
        q = _rms(_head_rows(prow[:, COL_SQ:COL_SQ + SWA_WIDTH], N_SWA_HEADS), sqg_ref[...]) * ATTN_SCALE
        kn = _rms(_head_rows(prow[:, COL_SK:COL_SK + SWA_KV_WIDTH], N_SWA_KV_HEADS), skg_ref[...])
        kn_row = jnp.concatenate([kn[0:1], kn[1:2]], axis=1)
        vn_row = prow[:, COL_SV:COL_SV + SWA_KV_WIDTH]
        kbuf = kbuf_ref[b]
        vbuf = vbuf_ref[b]
        qm = _spread(q, N_SWA_KV_HEADS, grp)
        s = _dot_nt(qm, kbuf) + bias_w
        s_new = jnp.sum(qm * kn_row, -1, keepdims=True) + bias_new
        m = jnp.maximum(jnp.maximum(jnp.max(s, -1, keepdims=True), s_new), sink)
        e = jnp.exp(s - m)
        e_new = jnp.exp(s_new - m)
        den = jnp.sum(e, -1, keepdims=True) + e_new + jnp.exp(sink - m)
        o_full = (_dot(e, vbuf) + e_new * vn_row) / den
        o = _gather_groups(o_full, N_SWA_KV_HEADS, grp)
        for h in range(N_SWA_HEADS):
            ys_ref[b:b + 1, h * HEAD_DIM:(h + 1) * HEAD_DIM] = o[h:h + 1]
        kout_ref[b] = jnp.where(rowi == WINDOW - 1, kn_row, pltpu.roll(kbuf, WINDOW - 1, axis=0))
        vout_ref[b] = jnp.where(rowi == WINDOW - 1, vn_row, pltpu.roll(vbuf, WINDOW - 1, axis=0))
        qmem = _rms(_head_rows(prow[:, COL_MQ:COL_MQ + MEM_WIDTH], N_MEM_HEADS), mqg_ref[...]) * ATTN_SCALE
        sm = _dot_nt(_spread(qmem, N_MEM_HEADS, 1), mk_ref[b])
        em = jnp.exp(sm - jnp.max(sm, -1, keepdims=True))
        om = _gather_groups(_dot(em, mv_ref[b]), N_MEM_HEADS, 1) / jnp.sum(em, -1, keepdims=True)
        for h in range(N_MEM_HEADS):
            ym_ref[b:b + 1, h * HEAD_DIM:(h + 1) * HEAD_DIM] = om[h:h + 1]


def _decode_attn(proj, kbuf, vbuf, mk, mv, rel_bias, sinks, sqg, skg, mqg):
    n = proj.shape[0]
    bkt = jnp.asarray(_decode_bucket_table())
    smem = pl.BlockSpec(memory_space=pltpu.SMEM)
    win = pl.BlockSpec((SEQ_TILE, WINDOW, SWA_KV_WIDTH), lambda i: (i, 0, 0))
    memb = pl.BlockSpec((SEQ_TILE, N_MEM, MEM_WIDTH), lambda i: (i, 0, 0))
    return pl.pallas_call(
        _decode_attn_kernel,
        grid=(n // SEQ_TILE,),
        in_specs=[pl.BlockSpec((SEQ_TILE, IN_PROJ), lambda i: (i, 0)), win, win, memb, memb,
                  _full((8, WINDOW)), smem, smem, _full((1, HEAD_DIM)), _full((1, HEAD_DIM)), _full((1, HEAD_DIM))],
        out_specs=[pl.BlockSpec((SEQ_TILE, SWA_WIDTH), lambda i: (i, 0)),
                   pl.BlockSpec((SEQ_TILE, MEM_WIDTH), lambda i: (i, 0)), win, win],
        out_shape=[jax.ShapeDtypeStruct((n, SWA_WIDTH), F32), jax.ShapeDtypeStruct((n, MEM_WIDTH), F32),
                   jax.ShapeDtypeStruct(kbuf.shape, F32), jax.ShapeDtypeStruct(vbuf.shape, F32)],
        compiler_params=_params("arbitrary"),
        name="decode_attn",
    )(proj, kbuf, vbuf, mk, mv, bkt, rel_bias, sinks, sqg, skg, mqg)


FF_CHUNK = 1024


def _out_ffn_kernel(x_ref, yr_ref, ys_ref, ym_ref, wo_ref, g2_ref, w1_ref, w2_ref, o_ref):
    x1 = (x_ref[...]
          + jnp.dot(yr_ref[...].astype(BF16), wo_ref[0:RWKV_WIDTH, :], preferred_element_type=F32)
          + jnp.dot(ys_ref[...].astype(BF16), wo_ref[RWKV_WIDTH:RWKV_WIDTH + SWA_WIDTH, :],
                    preferred_element_type=F32)
          + jnp.dot(ym_ref[...].astype(BF16), wo_ref[RWKV_WIDTH + SWA_WIDTH:, :], preferred_element_type=F32))
    h2 = _rms(x1, g2_ref[...]).astype(BF16)
    ff = None
    for c in range(D_FF // FF_CHUNK):
        u = jnp.dot(h2, w1_ref[:, c * FF_CHUNK:(c + 1) * FF_CHUNK], preferred_element_type=F32)
        u = jnp.square(jnp.maximum(u, 0.0)).astype(BF16)
        d = jnp.dot(u, w2_ref[c * FF_CHUNK:(c + 1) * FF_CHUNK, :], preferred_element_type=F32)
        ff = d if ff is None else ff + d
    o_ref[...] = x1 + ff


def _out_ffn(x2d, yr, ys, ym, wo, g2, w1, w2, tm):
    n = x2d.shape[0]
    rows = lambda w: pl.BlockSpec((tm, w), lambda i: (i, 0))
    const = lambda shape: pl.BlockSpec(shape, lambda i: (0, 0), pipeline_mode=pl.Buffered(1))
    return pl.pallas_call(
        _out_ffn_kernel,
        grid=(n // tm,),
        in_specs=[rows(D_MODEL), rows(RWKV_WIDTH), rows(SWA_WIDTH), rows(MEM_WIDTH),
                  const((D_MODEL, D_MODEL)), _full((1, D_MODEL)), const((D_MODEL, D_FF)), const((D_FF, D_MODEL))],
        out_specs=rows(D_MODEL),
        out_shape=jax.ShapeDtypeStruct((n, D_MODEL), F32),
        compiler_params=_params("arbitrary"),
        name="out_ffn",
    )(x2d, yr, ys, ym, wo, g2, w1, w2)


def kernel(x_prompt, x_sample, state_rwkv, state_shift, cache_swa_k, cache_swa_v, cache_mem_k, cache_mem_v,
           mem_prompt, rel_bias, norm1_g, w_in, mu_shift, w0, w_up_w, a0, w_up_a, w_up_g, k_k, k_a, r_k,
           lnx_w, lnx_b, q_norm_swa, k_norm_swa, sinks, mem_norm_g, w_mem_kv, q_norm_mem, k_norm_mem,
           w_out, norm2_g, w_ff1, w_ff2):
    B, T, _ = x_prompt.shape
    Bd = x_sample.shape[0]
    l = 0
    w_in_b = w_in[l].astype(BF16)
    w_out_b = w_out[l].astype(BF16)
    w1_b = w_ff1[l].astype(BF16)
    w2_b = w_ff2[l].astype(BF16)
    rwkv_params = (mu_shift[l][None], w0[l][None], a0[l][None], k_k[l][None], k_a[l][None],
                   r_k[l].reshape(1, RWKV_WIDTH), lnx_w[l][None], lnx_b[l][None],
                   w_up_w[l].astype(BF16), w_up_a[l].astype(BF16), w_up_g[l].astype(BF16))
    sqg, skg, mqg, mkg = q_norm_swa[l][None], k_norm_swa[l][None], q_norm_mem[l][None], k_norm_mem[l][None]
    g1, g2 = norm1_g[l][None], norm2_g[l][None]

    xp = x_prompt.reshape(B * T, D_MODEL)
    proj_p = _in_proj(xp, g1, w_in_b, 512).reshape(B, T, IN_PROJ)
    mk, mv = _memory_kv(mem_prompt, mem_norm_g[l][None], w_mem_kv[l].astype(BF16), mkg)
    yr_p, s_p = _rwkv_prompt(proj_p, *rwkv_params)
    ys_p, kn_p = _swa_prompt(proj_p, rel_bias, sinks[l][None], sqg, skg)
    ym_p = _mem_prompt(proj_p, mk, mv, mqg)
    y_p = _out_ffn(xp, yr_p.reshape(B * T, RWKV_WIDTH), ys_p.reshape(B * T, SWA_WIDTH),
                   ym_p.reshape(B * T, MEM_WIDTH), w_out_b, g2, w1_b, w2_b, 512).reshape(B, T, D_MODEL)
    shift_p = proj_p[:, T - 1, :RWKV_PROJ]
    vb_p = proj_p[:, T - WINDOW:, COL_SV:COL_SV + SWA_KV_WIDTH]

    xs = x_sample.reshape(Bd, D_MODEL)
    proj_s = _in_proj(xs, g1, w_in_b, Bd)
    yr_s, s_s = _rwkv_step(proj_s[:, :RWKV_PROJ], state_shift[l], state_rwkv[l], *rwkv_params)
    ys_s, ym_s, kb_s, vb_s = _decode_attn(
        proj_s, cache_swa_k[l].reshape(Bd, WINDOW, SWA_KV_WIDTH), cache_swa_v[l].reshape(Bd, WINDOW, SWA_KV_WIDTH),
        cache_mem_k[l].reshape(Bd, N_MEM, MEM_WIDTH), cache_mem_v[l].reshape(Bd, N_MEM, MEM_WIDTH),
        rel_bias, sinks[l][None], sqg, skg, mqg)
    y_s = _out_ffn(xs, yr_s, ys_s, ym_s, w_out_b, g2, w1_b, w2_b, Bd).reshape(Bd, 1, D_MODEL)

    return (y_p, y_s,
            s_p[None], shift_p[None],
            kn_p.reshape(1, B, WINDOW, N_SWA_KV_HEADS, HEAD_DIM),
            vb_p.reshape(1, B, WINDOW, N_SWA_KV_HEADS, HEAD_DIM),
            mk.reshape(1, B, N_MEM, N_MEM_HEADS, HEAD_DIM), mv.reshape(1, B, N_MEM, N_MEM_HEADS, HEAD_DIM),
            s_s[None], proj_s[:, :RWKV_PROJ][None],
            kb_s.reshape(1, Bd, WINDOW, N_SWA_KV_HEADS, HEAD_DIM),
            vb_s.reshape(1, Bd, WINDOW, N_SWA_KV_HEADS, HEAD_DIM))
```

```python
import functools
import math

import numpy as np
import jax
import jax.numpy as jnp
from jax import lax
from jax.experimental import pallas as pl
from jax.experimental.pallas import tpu as pltpu

F32 = jnp.float32
BF16 = jnp.bfloat16

D_MODEL = 1024
HEAD_DIM = 64
RWKV_WIDTH = 512
N_RWKV_HEADS = 8
SWA_WIDTH = 256
N_SWA_HEADS = 4
N_SWA_KV_HEADS = 2
SWA_KV_WIDTH = 128
MEM_WIDTH = 256
N_MEM_HEADS = 4
N_MEM = 256
WINDOW = 128
BLOCK = 128
N_BUCKETS = 32
MAX_DISTANCE = 128
DECAY_LORA = 64
AAA_LORA = 64
GATE_LORA = 128
RWKV_PROJ = 3 * RWKV_WIDTH + DECAY_LORA + AAA_LORA + GATE_LORA
SWA_PROJ = SWA_WIDTH + 2 * SWA_KV_WIDTH
IN_PROJ = RWKV_PROJ + SWA_PROJ + MEM_WIDTH
D_FF = 4 * D_MODEL
NORM_EPS = 1e-6
LNX_EPS = 64e-5
ATTN_SCALE = HEAD_DIM ** -0.5
EXP_M05 = math.exp(-0.5)
NEG = -1e30

COL_R, COL_K, COL_V = 0, RWKV_WIDTH, 2 * RWKV_WIDTH
COL_WD = 3 * RWKV_WIDTH
COL_AD = COL_WD + DECAY_LORA
COL_GD = COL_AD + AAA_LORA
COL_SQ = RWKV_PROJ
COL_SK = COL_SQ + SWA_WIDTH
COL_SV = COL_SK + SWA_KV_WIDTH
COL_MQ = RWKV_PROJ + SWA_PROJ

CHUNK = 64
VMEM_LIMIT = 56 * 1024 * 1024


def _dot(a, b):
    return jnp.dot(a.astype(BF16), b.astype(BF16), preferred_element_type=F32)


def _dot_nt(a, b):
    return lax.dot_general(a.astype(BF16), b.astype(BF16), (((1,), (1,)), ((), ())),
                           preferred_element_type=F32)


def _dot_tn(a, b):
    return lax.dot_general(a.astype(BF16), b.astype(BF16), (((0,), (0,)), ((), ())),
                           preferred_element_type=F32)


def _rms(x, g):
    return x * lax.rsqrt(jnp.mean(x * x, -1, keepdims=True) + NORM_EPS) * g


def _params(*sem):
    return pltpu.CompilerParams(dimension_semantics=sem, vmem_limit_bytes=VMEM_LIMIT)


def _full(shape):
    n = len(shape)
    return pl.BlockSpec(shape, lambda *_: (0,) * n)


def _in_proj_kernel(x_ref, g_ref, w_ref, o_ref):
    h = _rms(x_ref[...], g_ref[...])
    o_ref[...] = jnp.dot(h.astype(BF16), w_ref[...], preferred_element_type=F32)


def _in_proj(x2d, g, w_bf16, tm):
    n = x2d.shape[0]
    return pl.pallas_call(
        _in_proj_kernel,
        grid=(n // tm,),
        in_specs=[pl.BlockSpec((tm, D_MODEL), lambda i: (i, 0)),
                  _full((1, D_MODEL)),
                  _full((D_MODEL, IN_PROJ))],
        out_specs=pl.BlockSpec((tm, IN_PROJ), lambda i: (i, 0)),
        out_shape=jax.ShapeDtypeStruct((n, IN_PROJ), F32),
        compiler_params=_params("arbitrary"),
        name="in_proj",
    )(x2d, g, w_bf16)


def _rwkv_features(xs, w0, a0, k_k, k_a, wupw, wupa, wupg):
    r = xs[:, COL_R:COL_R + RWKV_WIDTH]
    k = xs[:, COL_K:COL_K + RWKV_WIDTH]
    v = xs[:, COL_V:COL_V + RWKV_WIDTH]
    wd = xs[:, COL_WD:COL_WD + DECAY_LORA]
    ad = xs[:, COL_AD:COL_AD + AAA_LORA]
    gd = xs[:, COL_GD:COL_GD + GATE_LORA]
    logw = -jax.nn.sigmoid(w0 + _dot(jnp.tanh(wd), wupw)) * EXP_M05
    a_sig = jax.nn.sigmoid(a0 + _dot(ad, wupa))
    gate = _dot(jax.nn.sigmoid(gd), wupg)
    kk = k * k_k
    k2 = k * (1.0 + (a_sig - 1.0) * k_a)
    return r, k2, v, kk, a_sig, logw, gate


def _head_out(y, r, k2, v, gate, r_k, lnx_w, lnx_b):
    m = jnp.mean(y, -1, keepdims=True)
    d = y - m
    var = jnp.mean(d * d, -1, keepdims=True)
    yn = d * lax.rsqrt(var + LNX_EPS) * lnx_w + lnx_b
    bonus = jnp.sum(r * k2 * r_k, -1, keepdims=True) * v
    return (yn + bonus) * gate


def _rwkv_prompt_kernel(p_ref, mu_ref, w0_ref, a0_ref, kk_ref, ka_ref, rk_ref, lnw_ref, lnb_ref,
                        wupw_ref, wupa_ref, wupg_ref, y_ref, sout_ref, s_scr, prev_scr):
    C = CHUNK
    t = pl.program_id(1)

    @pl.when(t == 0)
    def _():
        s_scr[...] = jnp.zeros_like(s_scr)
        prev_scr[...] = jnp.zeros_like(prev_scr)

    p = p_ref[0]
    row = lax.broadcasted_iota(jnp.int32, p.shape, 0)
    prev = jnp.where(row == 0, prev_scr[...], pltpu.roll(p, 1, axis=0))
    prev_scr[...] = p[C - 1:C, :]
    xs = p + (prev - p) * mu_ref[...]
    r, k2, v, kk, a_sig, logw, gate = _rwkv_features(
        xs, w0_ref[...], a0_ref[...], kk_ref[...], ka_ref[...], wupw_ref[...], wupa_ref[...], wupg_ref[...])

    ri = lax.broadcasted_iota(jnp.int32, (C, C), 0)
    ci = lax.broadcasted_iota(jnp.int32, (C, C), 1)
    tri = jnp.where(ri >= ci, 1.0, 0.0).astype(BF16)
    l1 = logw.astype(BF16)
    rem = logw - l1.astype(F32)
    l2 = rem.astype(BF16)
    l3 = (rem - l2.astype(F32)).astype(BF16)
    cum = (jnp.dot(tri, l1, preferred_element_type=F32) + jnp.dot(tri, l2, preferred_element_type=F32)
           + jnp.dot(tri, l3, preferred_element_type=F32))
    c_last = cum[C - 1:C, :]
    e_pos = jnp.exp(cum)
    e_neg = jnp.exp(-cum)
    e_prev = jnp.exp(cum - logw)
    e_last = jnp.exp(c_last - cum)
    w_last = jnp.exp(c_last)

    r2 = lax.broadcasted_iota(jnp.int32, (C, 2 * C), 0)
    c2 = lax.broadcasted_iota(jnp.int32, (C, 2 * C), 1)
    c2m = jnp.where(c2 >= C, c2 - C, c2)
    mask_ak = jnp.logical_and(c2 >= C, c2m < r2)
    mask_r = c2m <= r2
    strict = ci < ri

    H = range(N_RWKV_HEADS)
    hs = [slice(h * HEAD_DIM, (h + 1) * HEAD_DIM) for h in H]
    S = [s_scr[h] for h in H]
    rk_all, lnw_all, lnb_all = rk_ref[...], lnw_ref[...], lnb_ref[...]
    kkn = []
    for h in H:
        kk_h = kk[:, hs[h]]
        kkn.append(kk_h / jnp.maximum(jnp.sqrt(jnp.sum(kk_h * kk_h, -1, keepdims=True)), 1e-12))
    b = [kkn[h] * a_sig[:, hs[h]] for h in H]
    at = [-kkn[h] * e_prev[:, hs[h]] for h in H]
    rt = [r[:, hs[h]] * e_pos[:, hs[h]] for h in H]
    lhs = [jnp.concatenate([at[h], rt[h]], axis=0) for h in H]
    rhs = [jnp.concatenate([b[h] * e_neg[:, hs[h]], k2[:, hs[h]] * e_neg[:, hs[h]]], axis=0) for h in H]
    bk = [jnp.concatenate([b[h] * e_last[:, hs[h]], k2[:, hs[h]] * e_last[:, hs[h]]], axis=0) for h in H]
    vv = [jnp.concatenate([v[:, hs[h]], v[:, hs[h]]], axis=0) for h in H]

    sc = [_dot_nt(lhs[h], rhs[h]) for h in H]
    A = [jnp.where(strict, sc[h][:C, :C], 0.0) for h in H]
    m_ak = [jnp.where(mask_ak, sc[h][:C], 0.0) for h in H]
    m_r = [jnp.where(mask_r, sc[h][C:], 0.0) for h in H]

    X0 = [_dot_nt(at[h], S[h]) for h in H]
    X1 = [_dot(m_ak[h], vv[h]) for h in H]
    X = [X0[h] + X1[h] for h in H]
    AX = [_dot(A[h], X[h]) for h in H]
    X = [X[h] + AX[h] for h in H]
    for _ in range(int(math.log2(C)) - 1):
        A = [_dot(A[h], A[h]) for h in H]
        AX = [_dot(A[h], X[h]) for h in H]
        X = [X[h] + AX[h] for h in H]
    uv = [jnp.concatenate([X[h], v[:, hs[h]]], axis=0) for h in H]
    y0 = [_dot_nt(rt[h], S[h]) for h in H]
    y1 = [_dot(m_r[h], uv[h]) for h in H]
    ds = [_dot_tn(uv[h], bk[h]) for h in H]
    for h in H:
        s_scr[h] = S[h] * w_last[:, hs[h]] + ds[h]
        y_ref[0, :, hs[h]] = _head_out(y0[h] + y1[h], r[:, hs[h]], k2[:, hs[h]], v[:, hs[h]], gate[:, hs[h]],
                                       rk_all[:, hs[h]], lnw_all[:, hs[h]], lnb_all[:, hs[h]])

    @pl.when(t == pl.num_programs(1) - 1)
    def _():
        sout_ref[0] = s_scr[...]


def _rwkv_prompt(proj, mu, w0, a0, k_k, k_a, r_k, lnx_w, lnx_b, wupw, wupa, wupg):
    B, T, _ = proj.shape
    vec = lambda n: _full((1, n))
    return pl.pallas_call(
        _rwkv_prompt_kernel,
        grid=(B, T // CHUNK),
        in_specs=[pl.BlockSpec((1, CHUNK, RWKV_PROJ), lambda b, t: (b, t, 0)),
                  vec(RWKV_PROJ), vec(RWKV_WIDTH), vec(RWKV_WIDTH), vec(RWKV_WIDTH), vec(RWKV_WIDTH),
                  vec(RWKV_WIDTH), vec(RWKV_WIDTH), vec(RWKV_WIDTH),
                  _full((DECAY_LORA, RWKV_WIDTH)), _full((AAA_LORA, RWKV_WIDTH)), _full((GATE_LORA, RWKV_WIDTH))],
        out_specs=[pl.BlockSpec((1, CHUNK, RWKV_WIDTH), lambda b, t: (b, t, 0)),
                   pl.BlockSpec((1, N_RWKV_HEADS, HEAD_DIM, HEAD_DIM), lambda b, t: (b, 0, 0, 0))],
        out_shape=[jax.ShapeDtypeStruct((B, T, RWKV_WIDTH), F32),
                   jax.ShapeDtypeStruct((B, N_RWKV_HEADS, HEAD_DIM, HEAD_DIM), F32)],
        scratch_shapes=[pltpu.VMEM((N_RWKV_HEADS, HEAD_DIM, HEAD_DIM), F32),
                        pltpu.VMEM((1, RWKV_PROJ), F32)],
        compiler_params=_params("arbitrary", "arbitrary"),
        name="rwkv_prompt",
    )(proj, mu, w0, a0, k_k, k_a, r_k, lnx_w, lnx_b, wupw, wupa, wupg)


SEQ_TILE = 8


def _rwkv_step_kernel(p_ref, sh_ref, s_ref, mu_ref, w0_ref, a0_ref, kk_ref, ka_ref, rk_ref, lnw_ref, lnb_ref,
                      wupw_ref, wupa_ref, wupg_ref, y_ref, sout_ref):
    p = p_ref[...]
    xs = p + (sh_ref[...] - p) * mu_ref[...]
    r, k2, v, kk, a_sig, logw, gate = _rwkv_features(
        xs, w0_ref[...], a0_ref[...], kk_ref[...], ka_ref[...], wupw_ref[...], wupa_ref[...], wupg_ref[...])
    w = jnp.exp(logw)
    rk_all, lnw_all, lnb_all = rk_ref[...], lnw_ref[...], lnb_ref[...]
    H = range(N_RWKV_HEADS)
    hs = [slice(h * HEAD_DIM, (h + 1) * HEAD_DIM) for h in H]
    P = [(b, h) for h in H for b in range(SEQ_TILE)]
    a_h, b_h = [], []
    for h in H:
        kk_h = kk[:, hs[h]]
        kk_h = kk_h / jnp.maximum(jnp.sqrt(jnp.sum(kk_h * kk_h, -1, keepdims=True)), 1e-12)
        a_h.append(-kk_h)
        b_h.append(kk_h * a_sig[:, hs[h]])
    row = lambda x, b, h: x[b:b + 1, hs[h]]
    S = [s_ref[b, h] for b, h in P]
    sa = [_dot_nt(a_h[h], S[i])[b:b + 1] for i, (b, h) in enumerate(P)]
    upd = [_dot_tn(jnp.concatenate([sa[i], row(v, b, h)], axis=0),
                   jnp.concatenate([b_h[h][b:b + 1], row(k2, b, h)], axis=0)) for i, (b, h) in enumerate(P)]
    S = [S[i] * row(w, b, h) + upd[i] for i, (b, h) in enumerate(P)]
    for i, (b, h) in enumerate(P):
        sout_ref[b, h] = S[i]
    y_row = [_dot_nt(r[:, hs[h]], S[i])[b:b + 1] for i, (b, h) in enumerate(P)]
    for h in H:
        y = jnp.concatenate(y_row[h * SEQ_TILE:(h + 1) * SEQ_TILE], axis=0)
        y_ref[:, hs[h]] = _head_out(y, r[:, hs[h]], k2[:, hs[h]], v[:, hs[h]], gate[:, hs[h]],
                                    rk_all[:, hs[h]], lnw_all[:, hs[h]], lnb_all[:, hs[h]])


def _rwkv_step(proj, shift, state, mu, w0, a0, k_k, k_a, r_k, lnx_w, lnx_b, wupw, wupa, wupg):
    n = proj.shape[0]
    vec = lambda m: _full((1, m))
    st_spec = pl.BlockSpec((SEQ_TILE, N_RWKV_HEADS, HEAD_DIM, HEAD_DIM), lambda i: (i, 0, 0, 0))
    return pl.pallas_call(
        _rwkv_step_kernel,
        grid=(n // SEQ_TILE,),
        in_specs=[pl.BlockSpec((SEQ_TILE, RWKV_PROJ), lambda i: (i, 0)),
                  pl.BlockSpec((SEQ_TILE, RWKV_PROJ), lambda i: (i, 0)),
                  st_spec,
                  vec(RWKV_PROJ), vec(RWKV_WIDTH), vec(RWKV_WIDTH), vec(RWKV_WIDTH), vec(RWKV_WIDTH),
                  vec(RWKV_WIDTH), vec(RWKV_WIDTH), vec(RWKV_WIDTH),
                  _full((DECAY_LORA, RWKV_WIDTH)), _full((AAA_LORA, RWKV_WIDTH)), _full((GATE_LORA, RWKV_WIDTH))],
        out_specs=[pl.BlockSpec((SEQ_TILE, RWKV_WIDTH), lambda i: (i, 0)), st_spec],
        out_shape=[jax.ShapeDtypeStruct((n, RWKV_WIDTH), F32),
                   jax.ShapeDtypeStruct(state.shape, F32)],
        compiler_params=_params("arbitrary"),
        name="rwkv_step",
    )(proj, shift, state, mu, w0, a0, k_k, k_a, r_k, lnx_w, lnx_b, wupw, wupa, wupg)


def _t5_bucket_np(dist):
    max_exact = N_BUCKETS // 2
    d = np.maximum(dist, 1).astype(np.float32)
    large = max_exact + (np.log(d / np.float32(max_exact)) / np.float32(math.log(MAX_DISTANCE / max_exact))
                         * np.float32(N_BUCKETS - max_exact)).astype(np.int32)
    large = np.minimum(large, N_BUCKETS - 1)
    return np.where(dist < max_exact, dist, large).astype(np.int32)


def _prompt_bucket_table():
    qi = np.arange(BLOCK)[:, None]
    kj = np.arange(2 * BLOCK)[None, :]
    dist = BLOCK + qi - kj
    valid = (dist >= 0) & (dist <= WINDOW)
    return np.where(valid, _t5_bucket_np(np.maximum(dist, 0)), -1).astype(np.int32)


def _decode_bucket_table():
    dist = WINDOW - np.arange(WINDOW)
    return np.broadcast_to(_t5_bucket_np(dist)[None, :], (8, WINDOW)).astype(np.int32).copy()


def _bias_from_buckets(bkt, relb_ref, h, init):
    acc = jnp.full(bkt.shape, init, F32)
    for b in range(N_BUCKETS):
        acc = jnp.where(bkt == b, relb_ref[b, h], acc)
    return acc


def _swa_prompt_kernel(q_ref, kc_ref, kp_ref, vc_ref, vp_ref, bkt_ref, relb_ref, sink_ref, qg_ref, kg_ref,
                       y_ref, kn_ref, bias_scr):
    i = pl.program_id(1)

    @pl.when(i == 0)
    def _():
        bkt = bkt_ref[...]
        for h in range(N_SWA_HEADS):
            bias_scr[h] = _bias_from_buckets(bkt, relb_ref, h, NEG)

    q = q_ref[0]
    kc = kc_ref[0]
    kp = kp_ref[0]
    vc = vc_ref[0]
    vp = vp_ref[0]
    qg = qg_ref[...]
    kg = kg_ref[...]
    col = lax.broadcasted_iota(jnp.int32, (BLOCK, 2 * BLOCK), 1)
    pad_mask = jnp.logical_and(i == 0, col < BLOCK)
    for j in range(N_SWA_KV_HEADS):
        sl = slice(j * HEAD_DIM, (j + 1) * HEAD_DIM)
        kn_c = _rms(kc[:, sl], kg)
        kn_ref[0, :, sl] = kn_c
        kcat = jnp.concatenate([_rms(kp[:, sl], kg), kn_c], axis=0)
        vcat = jnp.concatenate([vp[:, sl], vc[:, sl]], axis=0)
        for g in range(N_SWA_HEADS // N_SWA_KV_HEADS):
            hq = j * (N_SWA_HEADS // N_SWA_KV_HEADS) + g
            qs = slice(hq * HEAD_DIM, (hq + 1) * HEAD_DIM)
            qh = _rms(q[:, qs], qg) * ATTN_SCALE
            s = _dot_nt(qh, kcat) + bias_scr[hq]
            s = jnp.where(pad_mask, NEG, s)
            sink = sink_ref[0, hq]
            m = jnp.maximum(jnp.max(s, -1, keepdims=True), sink)
            e = jnp.exp(s - m)
            den = jnp.sum(e, -1, keepdims=True) + jnp.exp(sink - m)
            y_ref[0, :, qs] = _dot(e, vcat) / den


def _swa_prompt(proj, rel_bias, sinks, qg, kg):
    B, T, _ = proj.shape
    nb = T // BLOCK
    bkt = jnp.asarray(_prompt_bucket_table())
    smem = pl.BlockSpec(memory_space=pltpu.SMEM)
    kblk, vblk = COL_SK // SWA_KV_WIDTH, COL_SV // SWA_KV_WIDTH
    return pl.pallas_call(
        _swa_prompt_kernel,
        grid=(B, nb),
        in_specs=[pl.BlockSpec((1, BLOCK, SWA_WIDTH), lambda b, i: (b, i, COL_SQ // SWA_WIDTH)),
                  pl.BlockSpec((1, BLOCK, SWA_KV_WIDTH), lambda b, i: (b, i, kblk)),
                  pl.BlockSpec((1, BLOCK, SWA_KV_WIDTH), lambda b, i: (b, jnp.maximum(i - 1, 0), kblk)),
                  pl.BlockSpec((1, BLOCK, SWA_KV_WIDTH), lambda b, i: (b, i, vblk)),
                  pl.BlockSpec((1, BLOCK, SWA_KV_WIDTH), lambda b, i: (b, jnp.maximum(i - 1, 0), vblk)),
                  _full((BLOCK, 2 * BLOCK)), smem, smem, _full((1, HEAD_DIM)), _full((1, HEAD_DIM))],
        out_specs=[pl.BlockSpec((1, BLOCK, SWA_WIDTH), lambda b, i: (b, i, 0)),
                   pl.BlockSpec((1, BLOCK, SWA_KV_WIDTH), lambda b, i: (b, 0, 0))],
        out_shape=[jax.ShapeDtypeStruct((B, T, SWA_WIDTH), F32),
                   jax.ShapeDtypeStruct((B, BLOCK, SWA_KV_WIDTH), F32)],
        scratch_shapes=[pltpu.VMEM((N_SWA_HEADS, BLOCK, 2 * BLOCK), F32)],
        compiler_params=_params("arbitrary", "arbitrary"),
        name="swa_prompt",
    )(proj, proj, proj, proj, proj, bkt, rel_bias, sinks, qg, kg)


MEM_Q_TILE = 512


def _mem_prompt_kernel(q_ref, mk_ref, mv_ref, qg_ref, y_ref):
    q = q_ref[0]
    mk = mk_ref[0]
    mv = mv_ref[0]
    qg = qg_ref[...]
    for h in range(N_MEM_HEADS):
        sl = slice(h * HEAD_DIM, (h + 1) * HEAD_DIM)
        qh = _rms(q[:, sl], qg) * ATTN_SCALE
        s = _dot_nt(qh, mk[:, sl])
        e = jnp.exp(s - jnp.max(s, -1, keepdims=True))
        y_ref[0, :, sl] = _dot(e, mv[:, sl]) / jnp.sum(e, -1, keepdims=True)


def _mem_prompt(proj, mk, mv, qg):
    B, T, _ = proj.shape
    return pl.pallas_call(
        _mem_prompt_kernel,
        grid=(B, T // MEM_Q_TILE),
        in_specs=[pl.BlockSpec((1, MEM_Q_TILE, MEM_WIDTH), lambda b, i: (b, i, COL_MQ // MEM_WIDTH)),
                  pl.BlockSpec((1, N_MEM, MEM_WIDTH), lambda b, i: (b, 0, 0)),
                  pl.BlockSpec((1, N_MEM, MEM_WIDTH), lambda b, i: (b, 0, 0)),
                  _full((1, HEAD_DIM))],
        out_specs=pl.BlockSpec((1, MEM_Q_TILE, MEM_WIDTH), lambda b, i: (b, i, 0)),
        out_shape=jax.ShapeDtypeStruct((B, T, MEM_WIDTH), F32),
        compiler_params=_params("arbitrary", "arbitrary"),
        name="mem_prompt",
    )(proj, mk, mv, qg)


def _memory_kv_kernel(mem_ref, g_ref, w_ref, kg_ref, mk_ref, mv_ref):
    kv = jnp.dot(_rms(mem_ref[0], g_ref[...]).astype(BF16), w_ref[...], preferred_element_type=F32)
    kg = kg_ref[...]
    for h in range(N_MEM_HEADS):
        sl = slice(h * HEAD_DIM, (h + 1) * HEAD_DIM)
        mk_ref[0, :, sl] = _rms(kv[:, sl], kg)
    mv_ref[0] = kv[:, MEM_WIDTH:]


def _memory_kv(mem, g, w_bf16, kg):
    B = mem.shape[0]
    blk = pl.BlockSpec((1, N_MEM, MEM_WIDTH), lambda b: (b, 0, 0))
    return pl.pallas_call(
        _memory_kv_kernel,
        grid=(B,),
        in_specs=[pl.BlockSpec((1, N_MEM, D_MODEL), lambda b: (b, 0, 0)),
                  _full((1, D_MODEL)), _full((D_MODEL, 2 * MEM_WIDTH)), _full((1, HEAD_DIM))],
        out_specs=[blk, blk],
        out_shape=[jax.ShapeDtypeStruct((B, N_MEM, MEM_WIDTH), F32)] * 2,
        compiler_params=_params("arbitrary"),
        name="memory_kv",
    )(mem, g, w_bf16, kg)


def _head_rows(x_row, n_heads):
    return jnp.concatenate([x_row[:, h * HEAD_DIM:(h + 1) * HEAD_DIM] for h in range(n_heads)], axis=0)


def _spread(xh, n_groups, rows_per_group):
    n = xh.shape[0]
    tiled = jnp.concatenate([xh] * n_groups, axis=1)
    rowi = lax.broadcasted_iota(jnp.int32, tiled.shape, 0)
    lane_grp = lax.broadcasted_iota(jnp.int32, tiled.shape, 1) // HEAD_DIM
    return jnp.where(lane_grp == rowi // rows_per_group, tiled, 0.0)


def _gather_groups(full, n_groups, rows_per_group):
    n = full.shape[0]
    rowi = lax.broadcasted_iota(jnp.int32, (n, HEAD_DIM), 0)
    out = jnp.zeros((n, HEAD_DIM), F32)
    for g in range(n_groups):
        out = out + jnp.where(rowi // rows_per_group == g, full[:, g * HEAD_DIM:(g + 1) * HEAD_DIM], 0.0)
    return out


def _decode_attn_kernel(p_ref, kbuf_ref, vbuf_ref, mk_ref, mv_ref, bkt_ref, relb_ref, sink_ref,
                        sqg_ref, skg_ref, mqg_ref, ys_ref, ym_ref, kout_ref, vout_ref):
    grp = N_SWA_HEADS // N_SWA_KV_HEADS
    hrow = lax.broadcasted_iota(jnp.int32, (8, WINDOW), 0)
    bias_w = jnp.zeros((8, WINDOW), F32)
    for h in range(N_SWA_HEADS):
        bias_w = jnp.where(hrow == h, _bias_from_buckets(bkt_ref[...], relb_ref, h, 0.0), bias_w)
    bias_w = bias_w[:N_SWA_HEADS]
    hcol = lax.broadcasted_iota(jnp.int32, (N_SWA_HEADS, 1), 0)
    bias_new = jnp.zeros((N_SWA_HEADS, 1), F32)
    sink = jnp.zeros((N_SWA_HEADS, 1), F32)
    for h in range(N_SWA_HEADS):
        bias_new = jnp.where(hcol == h, relb_ref[0, h], bias_new)
        sink = jnp.where(hcol == h, sink_ref[0, h], sink)
    rowi = lax.broadcasted_iota(jnp.int32, (WINDOW, SWA_KV_WIDTH), 0)

    for b in range(SEQ_TILE):
        prow = p_ref[b:b + 1, :]
        q = _rms(_head_rows(prow[:, COL_SQ:COL_SQ + SWA_WIDTH], N_SWA_HEADS), sqg_ref[...]) * ATTN_SCALE
        kn = _rms(_head_rows(prow[:, COL_SK:COL_SK + SWA_KV_WIDTH], N_SWA_KV_HEADS), skg_ref[...])
        kn_row = jnp.concatenate([kn[0:1], kn[1:2]], axis=1)
        vn_row = prow[:, COL_SV:COL_SV + SWA_KV_WIDTH]
        kbuf = kbuf_ref[b]
        vbuf = vbuf_ref[b]
        qm = _spread(q, N_SWA_KV_HEADS, grp)
        s = _dot_nt(qm, kbuf) + bias_w
        s_new = jnp.sum(qm * kn_row, -1, keepdims=True) + bias_new
        m = jnp.maximum(jnp.maximum(jnp.max(s, -1, keepdims=True), s_new), sink)
        e = jnp.exp(s - m)
        e_new = jnp.exp(s_new - m)
        den = jnp.sum(e, -1, keepdims=True) + e_new + jnp.exp(sink - m)
        o_full = (_dot(e, vbuf) + e_new * vn_row) / den
        o = _gather_groups(o_full, N_SWA_KV_HEADS, grp)
        for h in range(N_SWA_HEADS):
            ys_ref[b:b + 1, h * HEAD_DIM:(h + 1) * HEAD_DIM] = o[h:h + 1]
        kout_ref[b] = jnp.where(rowi == WINDOW - 1, kn_row, pltpu.roll(kbuf, WINDOW - 1, axis=0))
        vout_ref[b] = jnp.where(rowi == WINDOW - 1, vn_row, pltpu.roll(vbuf, WINDOW - 1, axis=0))
        qmem = _rms(_head_rows(prow[:, COL_MQ:COL_MQ + MEM_WIDTH], N_MEM_HEADS), mqg_ref[...]) * ATTN_SCALE
        sm = _dot_nt(_spread(qmem, N_MEM_HEADS, 1), mk_ref[b])
        em = jnp.exp(sm - jnp.max(sm, -1, keepdims=True))
        om = _gather_groups(_dot(em, mv_ref[b]), N_MEM_HEADS, 1) / jnp.sum(em, -1, keepdims=True)
        for h in range(N_MEM_HEADS):
            ym_ref[b:b + 1, h * HEAD_DIM:(h + 1) * HEAD_DIM] = om[h:h + 1]


def _decode_attn(proj, kbuf, vbuf, mk, mv, rel_bias, sinks, sqg, skg, mqg):
    n = proj.shape[0]
    bkt = jnp.asarray(_decode_bucket_table())
    smem = pl.BlockSpec(memory_space=pltpu.SMEM)
    win = pl.BlockSpec((SEQ_TILE, WINDOW, SWA_KV_WIDTH), lambda i: (i, 0, 0))
    memb = pl.BlockSpec((SEQ_TILE, N_MEM, MEM_WIDTH), lambda i: (i, 0, 0))
    return pl.pallas_call(
        _decode_attn_kernel,
        grid=(n // SEQ_TILE,),
        in_specs=[pl.BlockSpec((SEQ_TILE, IN_PROJ), lambda i: (i, 0)), win, win, memb, memb,
                  _full((8, WINDOW)), smem, smem, _full((1, HEAD_DIM)), _full((1, HEAD_DIM)), _full((1, HEAD_DIM))],
        out_specs=[pl.BlockSpec((SEQ_TILE, SWA_WIDTH), lambda i: (i, 0)),
                   pl.BlockSpec((SEQ_TILE, MEM_WIDTH), lambda i: (i, 0)), win, win],
        out_shape=[jax.ShapeDtypeStruct((n, SWA_WIDTH), F32), jax.ShapeDtypeStruct((n, MEM_WIDTH), F32),
                   jax.ShapeDtypeStruct(kbuf.shape, F32), jax.ShapeDtypeStruct(vbuf.shape, F32)],
        compiler_params=_params("arbitrary"),
        name="decode_attn",
    )(proj, kbuf, vbuf, mk, mv, bkt, rel_bias, sinks, sqg, skg, mqg)


FF_CHUNK = 1024


def _out_ffn_kernel(x_ref, yr_ref, ys_ref, ym_ref, wo_ref, g2_ref, w1_ref, w2_ref, o_ref):
    x1 = (x_ref[...]
          + jnp.dot(yr_ref[...].astype(BF16), wo_ref[0:RWKV_WIDTH, :], preferred_element_type=F32)
          + jnp.dot(ys_ref[...].astype(BF16), wo_ref[RWKV_WIDTH:RWKV_WIDTH + SWA_WIDTH, :],
                    preferred_element_type=F32)
          + jnp.dot(ym_ref[...].astype(BF16), wo_ref[RWKV_WIDTH + SWA_WIDTH:, :], preferred_element_type=F32))
    h2 = _rms(x1, g2_ref[...]).astype(BF16)
    ff = None
    for c in range(D_FF // FF_CHUNK):
        u = jnp.dot(h2, w1_ref[:, c * FF_CHUNK:(c + 1) * FF_CHUNK], preferred_element_type=F32)
        u = jnp.square(jnp.maximum(u, 0.0)).astype(BF16)
        d = jnp.dot(u, w2_ref[c * FF_CHUNK:(c + 1) * FF_CHUNK, :], preferred_element_type=F32)
        ff = d if ff is None else ff + d
    o_ref[...] = x1 + ff


def _out_ffn(x2d, yr, ys, ym, wo, g2, w1, w2, tm):
    n = x2d.shape[0]
    rows = lambda w: pl.BlockSpec((tm, w), lambda i: (i, 0))
    const = lambda shape: pl.BlockSpec(shape, lambda i: (0, 0), pipeline_mode=pl.Buffered(1))
    return pl.pallas_call(
        _out_ffn_kernel,
        grid=(n // tm,),
        in_specs=[rows(D_MODEL), rows(RWKV_WIDTH), rows(SWA_WIDTH), rows(MEM_WIDTH),
                  const((D_MODEL, D_MODEL)), _full((1, D_MODEL)), const((D_MODEL, D_FF)), const((D_FF, D_MODEL))],
        out_specs=rows(D_MODEL),
        out_shape=jax.ShapeDtypeStruct((n, D_MODEL), F32),
        compiler_params=_params("arbitrary"),
        name="out_ffn",
    )(x2d, yr, ys, ym, wo, g2, w1, w2)


def kernel(x_prompt, x_sample, state_rwkv, state_shift, cache_swa_k, cache_swa_v, cache_mem_k, cache_mem_v,
           mem_prompt, rel_bias, norm1_g, w_in, mu_shift, w0, w_up_w, a0, w_up_a, w_up_g, k_k, k_a, r_k,
           lnx_w, lnx_b, q_norm_swa, k_norm_swa, sinks, mem_norm_g, w_mem_kv, q_norm_mem, k_norm_mem,
           w_out, norm2_g, w_ff1, w_ff2):
    B, T, _ = x_prompt.shape
    Bd = x_sample.shape[0]
    l = 0
    w_in_b = w_in[l].astype(BF16)
    w_out_b = w_out[l].astype(BF16)
    w1_b = w_ff1[l].astype(BF16)
    w2_b = w_ff2[l].astype(BF16)
    rwkv_params = (mu_shift[l][None], w0[l][None], a0[l][None], k_k[l][None], k_a[l][None],
                   r_k[l].reshape(1, RWKV_WIDTH), lnx_w[l][None], lnx_b[l][None],
                   w_up_w[l].astype(BF16), w_up_a[l].astype(BF16), w_up_g[l].astype(BF16))
    sqg, skg, mqg, mkg = q_norm_swa[l][None], k_norm_swa[l][None], q_norm_mem[l][None], k_norm_mem[l][None]
    g1, g2 = norm1_g[l][None], norm2_g[l][None]

    xp = x_prompt.reshape(B * T, D_MODEL)
    proj_p = _in_proj(xp, g1, w_in_b, 512).reshape(B, T, IN_PROJ)
    mk, mv = _memory_kv(mem_prompt, mem_norm_g[l][None], w_mem_kv[l].astype(BF16), mkg)
    yr_p, s_p = _rwkv_prompt(proj_p, *rwkv_params)
    ys_p, kn_p = _swa_prompt(proj_p, rel_bias, sinks[l][None], sqg, skg)
    ym_p = _mem_prompt(proj_p, mk, mv, mqg)
    y_p = _out_ffn(xp, yr_p.reshape(B * T, RWKV_WIDTH), ys_p.reshape(B * T, SWA_WIDTH),
                   ym_p.reshape(B * T, MEM_WIDTH), w_out_b, g2, w1_b, w2_b, 512).reshape(B, T, D_MODEL)
    shift_p = proj_p[:, T - 1, :RWKV_PROJ]
    vb_p = proj_p[:, T - WINDOW:, COL_SV:COL_SV + SWA_KV_WIDTH]

    xs = x_sample.reshape(Bd, D_MODEL)
    proj_s = _in_proj(xs, g1, w_in_b, Bd)
    yr_s, s_s = _rwkv_step(proj_s[:, :RWKV_PROJ], state_shift[l], state_rwkv[l], *rwkv_params)
    ys_s, ym_s, kb_s, vb_s = _decode_attn(
        proj_s, cache_swa_k[l].reshape(Bd, WINDOW, SWA_KV_WIDTH), cache_swa_v[l].reshape(Bd, WINDOW, SWA_KV_WIDTH),
        cache_mem_k[l].reshape(Bd, N_MEM, MEM_WIDTH), cache_mem_v[l].reshape(Bd, N_MEM, MEM_WIDTH),
        rel_bias, sinks[l][None], sqg, skg, mqg)
    y_s = _out_ffn(xs, yr_s, ys_s, ym_s, w_out_b, g2, w1_b, w2_b, Bd).reshape(Bd, 1, D_MODEL)

    return (y_p, y_s,
            s_p[None], shift_p[None],
            kn_p.reshape(1, B, WINDOW, N_SWA_KV_HEADS, HEAD_DIM),
            vb_p.reshape(1, B, WINDOW, N_SWA_KV_HEADS, HEAD_DIM),
            mk.reshape(1, B, N_MEM, N_MEM_HEADS, HEAD_DIM), mv.reshape(1, B, N_MEM, N_MEM_HEADS, HEAD_DIM),
            s_s[None], proj_s[:, :RWKV_PROJ][None],
            kb_s.reshape(1, Bd, WINDOW, N_SWA_KV_HEADS, HEAD_DIM),
            vb_s.reshape(1, Bd, WINDOW, N_SWA_KV_HEADS, HEAD_DIM))
```

```python
import functools
import math

import numpy as np
import jax
import jax.numpy as jnp
from jax import lax
from jax.experimental import pallas as pl
from jax.experimental.pallas import tpu as pltpu

F32 = jnp.float32
BF16 = jnp.bfloat16

D_MODEL = 1024
HEAD_DIM = 64
RWKV_WIDTH = 512
N_RWKV_HEADS = 8
SWA_WIDTH = 256
N_SWA_HEADS = 4
N_SWA_KV_HEADS = 2
SWA_KV_WIDTH = 128
MEM_WIDTH = 256
N_MEM_HEADS = 4
N_MEM = 256
WINDOW = 128
BLOCK = 128
N_BUCKETS = 32
MAX_DISTANCE = 128
DECAY_LORA = 64
AAA_LORA = 64
GATE_LORA = 128
RWKV_PROJ = 3 * RWKV_WIDTH + DECAY_LORA + AAA_LORA + GATE_LORA
SWA_PROJ = SWA_WIDTH + 2 * SWA_KV_WIDTH
IN_PROJ = RWKV_PROJ + SWA_PROJ + MEM_WIDTH
D_FF = 4 * D_MODEL
NORM_EPS = 1e-6
LNX_EPS = 64e-5
ATTN_SCALE = HEAD_DIM ** -0.5
EXP_M05 = math.exp(-0.5)
NEG = -1e30

COL_R, COL_K, COL_V = 0, RWKV_WIDTH, 2 * RWKV_WIDTH
COL_WD = 3 * RWKV_WIDTH
COL_AD = COL_WD + DECAY_LORA
COL_GD = COL_AD + AAA_LORA
COL_SQ = RWKV_PROJ
COL_SK = COL_SQ + SWA_WIDTH
COL_SV = COL_SK + SWA_KV_WIDTH
COL_MQ = RWKV_PROJ + SWA_PROJ

CHUNK = 64
VMEM_LIMIT = 56 * 1024 * 1024


def _dot(a, b):
    return jnp.dot(a.astype(BF16), b.astype(BF16), preferred_element_type=F32)


def _dot_nt(a, b):
    return lax.dot_general(a.astype(BF16), b.astype(BF16), (((1,), (1,)), ((), ())),
                           preferred_element_type=F32)


def _dot_tn(a, b):
    return lax.dot_general(a.astype(BF16), b.astype(BF16), (((0,), (0,)), ((), ())),
                           preferred_element_type=F32)


def _rms(x, g):
    return x * lax.rsqrt(jnp.mean(x * x, -1, keepdims=True) + NORM_EPS) * g


def _params(*sem):
    return pltpu.CompilerParams(dimension_semantics=sem, vmem_limit_bytes=VMEM_LIMIT)


def _full(shape):
    n = len(shape)
    return pl.BlockSpec(shape, lambda *_: (0,) * n)


def _in_proj_kernel(x_ref, g_ref, w_ref, o_ref):
    h = _rms(x_ref[...], g_ref[...])
    o_ref[...] = jnp.dot(h.astype(BF16), w_ref[...], preferred_element_type=F32)


def _in_proj(x2d, g, w_bf16, tm):
    n = x2d.shape[0]
    return pl.pallas_call(
        _in_proj_kernel,
        grid=(n // tm,),
        in_specs=[pl.BlockSpec((tm, D_MODEL), lambda i: (i, 0)),
                  _full((1, D_MODEL)),
                  _full((D_MODEL, IN_PROJ))],
        out_specs=pl.BlockSpec((tm, IN_PROJ), lambda i: (i, 0)),
        out_shape=jax.ShapeDtypeStruct((n, IN_PROJ), F32),
        compiler_params=_params("arbitrary"),
        name="in_proj",
    )(x2d, g, w_bf16)


def _rwkv_features(xs, w0, a0, k_k, k_a, wupw, wupa, wupg):
    r = xs[:, COL_R:COL_R + RWKV_WIDTH]
    k = xs[:, COL_K:COL_K + RWKV_WIDTH]
    v = xs[:, COL_V:COL_V + RWKV_WIDTH]
    wd = xs[:, COL_WD:COL_WD + DECAY_LORA]
    ad = xs[:, COL_AD:COL_AD + AAA_LORA]
    gd = xs[:, COL_GD:COL_GD + GATE_LORA]
    logw = -jax.nn.sigmoid(w0 + _dot(jnp.tanh(wd), wupw)) * EXP_M05
    a_sig = jax.nn.sigmoid(a0 + _dot(ad, wupa))
    gate = _dot(jax.nn.sigmoid(gd), wupg)
    kk = k * k_k
    k2 = k * (1.0 + (a_sig - 1.0) * k_a)
    return r, k2, v, kk, a_sig, logw, gate


def _head_out(y, r, k2, v, gate, r_k, lnx_w, lnx_b):
    m = jnp.mean(y, -1, keepdims=True)
    d = y - m
    var = jnp.mean(d * d, -1, keepdims=True)
    yn = d * lax.rsqrt(var + LNX_EPS) * lnx_w + lnx_b
    bonus = jnp.sum(r * k2 * r_k, -1, keepdims=True) * v
    return (yn + bonus) * gate


RWKV_TILE = 2 * CHUNK


def _rwkv_prompt_kernel(p_ref, mu_ref, w0_ref, a0_ref, kk_ref, ka_ref, rk_ref, lnw_ref, lnb_ref,
                        wupw_ref, wupa_ref, wupg_ref, y_ref, sout_ref, s_scr, prev_scr):
    C, TT = CHUNK, RWKV_TILE
    NC = TT // C
    t = pl.program_id(1)

    @pl.when(t == 0)
    def _():
        s_scr[...] = jnp.zeros_like(s_scr)
        prev_scr[...] = jnp.zeros_like(prev_scr)

    p = p_ref[0]
    row = lax.broadcasted_iota(jnp.int32, p.shape, 0)
    prev = jnp.where(row == 0, prev_scr[...], pltpu.roll(p, 1, axis=0))
    prev_scr[...] = p[TT - 1:TT, :]
    xs = p + (prev - p) * mu_ref[...]
    r, k2, v, kk, a_sig, logw, gate = _rwkv_features(
        xs, w0_ref[...], a0_ref[...], kk_ref[...], ka_ref[...], wupw_ref[...], wupa_ref[...], wupg_ref[...])

    ri = lax.broadcasted_iota(jnp.int32, (TT, TT), 0)
    ci = lax.broadcasted_iota(jnp.int32, (TT, TT), 1)
    tri = jnp.where(jnp.logical_and(ri >= ci, ri // C == ci // C), 1.0, 0.0).astype(BF16)
    l1 = logw.astype(BF16)
    rem = logw - l1.astype(F32)
    l2 = rem.astype(BF16)
    l3 = (rem - l2.astype(F32)).astype(BF16)
    cum = (jnp.dot(tri, l1, preferred_element_type=F32) + jnp.dot(tri, l2, preferred_element_type=F32)
           + jnp.dot(tri, l3, preferred_element_type=F32))
    c_last = jnp.concatenate([jnp.broadcast_to(cum[(c + 1) * C - 1:(c + 1) * C, :], (C, RWKV_WIDTH))
                              for c in range(NC)], axis=0)
    e_pos = jnp.exp(cum)
    e_neg = jnp.exp(-cum)
    e_prev = jnp.exp(cum - logw)
    e_last = jnp.exp(c_last - cum)

    r2 = lax.broadcasted_iota(jnp.int32, (C, 2 * C), 0)
    c2 = lax.broadcasted_iota(jnp.int32, (C, 2 * C), 1)
    c2m = jnp.where(c2 >= C, c2 - C, c2)
    mask_ak = jnp.logical_and(c2 >= C, c2m < r2)
    mask_r = c2m <= r2
    strict = lax.broadcasted_iota(jnp.int32, (C, C), 1) < lax.broadcasted_iota(jnp.int32, (C, C), 0)
    lane3 = lax.broadcasted_iota(jnp.int32, (C, 3 * HEAD_DIM), 1)
    keep_x = lane3 < 2 * HEAD_DIM
    zeros_c = jnp.zeros((C, HEAD_DIM), F32)

    H = range(N_RWKV_HEADS)
    hs = [slice(h * HEAD_DIM, (h + 1) * HEAD_DIM) for h in H]
    kkn = []
    for h in H:
        kk_h = kk[:, hs[h]]
        kkn.append(kk_h / jnp.maximum(jnp.sqrt(jnp.sum(kk_h * kk_h, -1, keepdims=True)), 1e-12))
    bb = [kkn[h] * a_sig[:, hs[h]] for h in H]
    at_f = [-kkn[h] * e_prev[:, hs[h]] for h in H]
    rt_f = [r[:, hs[h]] * e_pos[:, hs[h]] for h in H]
    bt_f = [bb[h] * e_neg[:, hs[h]] for h in H]
    kt_f = [k2[:, hs[h]] * e_neg[:, hs[h]] for h in H]
    bh_f = [bb[h] * e_last[:, hs[h]] for h in H]
    kh_f = [k2[:, hs[h]] * e_last[:, hs[h]] for h in H]

    PR = [(c, h) for c in range(NC) for h in H]
    n = len(PR)
    rows = lambda x, c: x[c * C:(c + 1) * C]
    at = [rows(at_f[h], c) for c, h in PR]
    rt = [rows(rt_f[h], c) for c, h in PR]
    vh = [rows(v[:, hs[h]], c) for c, h in PR]

    sc = [_dot_nt(jnp.concatenate([at[i], rt[i]], axis=0),
                  jnp.concatenate([rows(bt_f[h], c), rows(kt_f[h], c)], axis=0)) for i, (c, h) in enumerate(PR)]
    A = [jnp.where(strict, sc[i][:C, :C], 0.0) for i in range(n)]
    m_ak = [jnp.where(mask_ak, sc[i][:C], 0.0) for i in range(n)]
    m_r = [jnp.where(mask_r, sc[i][C:], 0.0) for i in range(n)]

    Z = [_dot(m_ak[i], jnp.concatenate([vh[i], vh[i]], axis=0)) for i in range(n)]
    W = [jnp.concatenate([at[i], Z[i], A[i]], axis=1) for i in range(n)]
    for k in range(int(math.log2(C))):
        AW = [_dot(W[i][:, 2 * HEAD_DIM:], W[i]) for i in range(n)]
        W = [jnp.where(keep_x, W[i], 0.0) + AW[i] for i in range(n)]
    PQ = [W[i][:, :2 * HEAD_DIM] for i in range(n)]
    MkG = [_dot_tn(PQ[i], rows(bh_f[h], c)) for i, (c, h) in enumerate(PR)]
    G2 = [_dot_tn(vh[i], rows(kh_f[h], c)) for i, (c, h) in enumerate(PR)]
    RY = [_dot(m_r[i], jnp.concatenate([PQ[i], jnp.concatenate([zeros_c, vh[i]], axis=1)], axis=0))
          for i in range(n)]

    S = [s_scr[h] for h in H]
    rk_all, lnw_all, lnb_all = rk_ref[...], lnw_ref[...], lnb_ref[...]
    for c in range(NC):
        idx = [c * N_RWKV_HEADS + h for h in H]
        w_last = jnp.exp(cum[(c + 1) * C - 1:(c + 1) * C, :])
        y = [_dot_nt(rt[i] + RY[i][:, :HEAD_DIM], S[h]) for h, i in zip(H, idx)]
        dS = [_dot(S[h], MkG[i][:HEAD_DIM]) for h, i in zip(H, idx)]
        S = [S[h] * w_last[:, hs[h]] + dS[h] + MkG[i][HEAD_DIM:] + G2[i] for h, i in zip(H, idx)]
        for h, i in zip(H, idx):
            yo = _head_out(y[h] + RY[i][:, HEAD_DIM:], rows(r[:, hs[h]], c), rows(k2[:, hs[h]], c), vh[i],
                           rows(gate[:, hs[h]], c), rk_all[:, hs[h]], lnw_all[:, hs[h]], lnb_all[:, hs[h]])
            y_ref[0, c * C:(c + 1) * C, hs[h]] = yo
    for h in H:
        s_scr[h] = S[h]

    @pl.when(t == pl.num_programs(1) - 1)
    def _():
        sout_ref[0] = s_scr[...]


def _rwkv_prompt(proj, mu, w0, a0, k_k, k_a, r_k, lnx_w, lnx_b, wupw, wupa, wupg):
    B, T, _ = proj.shape
    vec = lambda n: _full((1, n))
    return pl.pallas_call(
        _rwkv_prompt_kernel,
        grid=(B, T // RWKV_TILE),
        in_specs=[pl.BlockSpec((1, RWKV_TILE, RWKV_PROJ), lambda b, t: (b, t, 0)),
                  vec(RWKV_PROJ), vec(RWKV_WIDTH), vec(RWKV_WIDTH), vec(RWKV_WIDTH), vec(RWKV_WIDTH),
                  vec(RWKV_WIDTH), vec(RWKV_WIDTH), vec(RWKV_WIDTH),
                  _full((DECAY_LORA, RWKV_WIDTH)), _full((AAA_LORA, RWKV_WIDTH)), _full((GATE_LORA, RWKV_WIDTH))],
        out_specs=[pl.BlockSpec((1, RWKV_TILE, RWKV_WIDTH), lambda b, t: (b, t, 0)),
                   pl.BlockSpec((1, N_RWKV_HEADS, HEAD_DIM, HEAD_DIM), lambda b, t: (b, 0, 0, 0))],
        out_shape=[jax.ShapeDtypeStruct((B, T, RWKV_WIDTH), F32),
                   jax.ShapeDtypeStruct((B, N_RWKV_HEADS, HEAD_DIM, HEAD_DIM), F32)],
        scratch_shapes=[pltpu.VMEM((N_RWKV_HEADS, HEAD_DIM, HEAD_DIM), F32),
                        pltpu.VMEM((1, RWKV_PROJ), F32)],
        compiler_params=_params("arbitrary", "arbitrary"),
        name="rwkv_prompt",
    )(proj, mu, w0, a0, k_k, k_a, r_k, lnx_w, lnx_b, wupw, wupa, wupg)


SEQ_TILE = 8


def _rwkv_step_kernel(p_ref, sh_ref, s_ref, mu_ref, w0_ref, a0_ref, kk_ref, ka_ref, rk_ref, lnw_ref, lnb_ref,
                      wupw_ref, wupa_ref, wupg_ref, y_ref, sout_ref):
    p = p_ref[...]
    xs = p + (sh_ref[...] - p) * mu_ref[...]
    r, k2, v, kk, a_sig, logw, gate = _rwkv_features(
        xs, w0_ref[...], a0_ref[...], kk_ref[...], ka_ref[...], wupw_ref[...], wupa_ref[...], wupg_ref[...])
    w = jnp.exp(logw)
    rk_all, lnw_all, lnb_all = rk_ref[...], lnw_ref[...], lnb_ref[...]
    H = range(N_RWKV_HEADS)
    hs = [slice(h * HEAD_DIM, (h + 1) * HEAD_DIM) for h in H]
    P = [(b, h) for h in H for b in range(SEQ_TILE)]
    a_h, b_h = [], []
    for h in H:
        kk_h = kk[:, hs[h]]
        kk_h = kk_h / jnp.maximum(jnp.sqrt(jnp.sum(kk_h * kk_h, -1, keepdims=True)), 1e-12)
        a_h.append(-kk_h)
        b_h.append(kk_h * a_sig[:, hs[h]])
    row = lambda x, b, h: x[b:b + 1, hs[h]]
    S = [s_ref[b, h] for b, h in P]
    sa = [_dot_nt(a_h[h], S[i])[b:b + 1] for i, (b, h) in enumerate(P)]
    upd = [_dot_tn(jnp.concatenate([sa[i], row(v, b, h)], axis=0),
                   jnp.concatenate([b_h[h][b:b + 1], row(k2, b, h)], axis=0)) for i, (b, h) in enumerate(P)]
    S = [S[i] * row(w, b, h) + upd[i] for i, (b, h) in enumerate(P)]
    for i, (b, h) in enumerate(P):
        sout_ref[b, h] = S[i]
    y_row = [_dot_nt(r[:, hs[h]], S[i])[b:b + 1] for i, (b, h) in enumerate(P)]
    for h in H:
        y = jnp.concatenate(y_row[h * SEQ_TILE:(h + 1) * SEQ_TILE], axis=0)
        y_ref[:, hs[h]] = _head_out(y, r[:, hs[h]], k2[:, hs[h]], v[:, hs[h]], gate[:, hs[h]],
                                    rk_all[:, hs[h]], lnw_all[:, hs[h]], lnb_all[:, hs[h]])


def _rwkv_step(proj, shift, state, mu, w0, a0, k_k, k_a, r_k, lnx_w, lnx_b, wupw, wupa, wupg):
    n = proj.shape[0]
    vec = lambda m: _full((1, m))
    st_spec = pl.BlockSpec((SEQ_TILE, N_RWKV_HEADS, HEAD_DIM, HEAD_DIM), lambda i: (i, 0, 0, 0))
    return pl.pallas_call(
        _rwkv_step_kernel,
        grid=(n // SEQ_TILE,),
        in_specs=[pl.BlockSpec((SEQ_TILE, RWKV_PROJ), lambda i: (i, 0)),
                  pl.BlockSpec((SEQ_TILE, RWKV_PROJ), lambda i: (i, 0)),
                  st_spec,
                  vec(RWKV_PROJ), vec(RWKV_WIDTH), vec(RWKV_WIDTH), vec(RWKV_WIDTH), vec(RWKV_WIDTH),
                  vec(RWKV_WIDTH), vec(RWKV_WIDTH), vec(RWKV_WIDTH),
                  _full((DECAY_LORA, RWKV_WIDTH)), _full((AAA_LORA, RWKV_WIDTH)), _full((GATE_LORA, RWKV_WIDTH))],
        out_specs=[pl.BlockSpec((SEQ_TILE, RWKV_WIDTH), lambda i: (i, 0)), st_spec],
        out_shape=[jax.ShapeDtypeStruct((n, RWKV_WIDTH), F32),
                   jax.ShapeDtypeStruct(state.shape, F32)],
        compiler_params=_params("arbitrary"),
        name="rwkv_step",
    )(proj, shift, state, mu, w0, a0, k_k, k_a, r_k, lnx_w, lnx_b, wupw, wupa, wupg)


def _t5_bucket_np(dist):
    max_exact = N_BUCKETS // 2
    d = np.maximum(dist, 1).astype(np.float32)
    large = max_exact + (np.log(d / np.float32(max_exact)) / np.float32(math.log(MAX_DISTANCE / max_exact))
                         * np.float32(N_BUCKETS - max_exact)).astype(np.int32)
    large = np.minimum(large, N_BUCKETS - 1)
    return np.where(dist < max_exact, dist, large).astype(np.int32)


def _prompt_bucket_table():
    qi = np.arange(BLOCK)[:, None]
    kj = np.arange(2 * BLOCK)[None, :]
    dist = BLOCK + qi - kj
    valid = (dist >= 0) & (dist <= WINDOW)
    return np.where(valid, _t5_bucket_np(np.maximum(dist, 0)), -1).astype(np.int32)


def _decode_bucket_table():
    dist = WINDOW - np.arange(WINDOW)
    return np.broadcast_to(_t5_bucket_np(dist)[None, :], (8, WINDOW)).astype(np.int32).copy()


def _bias_from_buckets(bkt, relb_ref, h, init):
    acc = jnp.full(bkt.shape, init, F32)
    for b in range(N_BUCKETS):
        acc = jnp.where(bkt == b, relb_ref[b, h], acc)
    return acc


ATT_TILE = 2 * BLOCK


def _attn_prompt_kernel(qs_ref, qm_ref, kc_ref, kp_ref, vc_ref, vp_ref, mk_ref, mv_ref, bkt_ref, relb_ref, sink_ref,
                        sqg_ref, skg_ref, mqg_ref, ys_ref, ym_ref, kn_ref, bias_scr, mk_scr, mv_scr):
    i = pl.program_id(1)
    grp = N_SWA_HEADS // N_SWA_KV_HEADS
    ones = jnp.ones((ATT_TILE, HEAD_DIM), F32)

    @pl.when(i == 0)
    def _():
        bkt = bkt_ref[...]
        for j in range(N_SWA_KV_HEADS):
            for g in range(grp):
                bias_scr[j, g * BLOCK:(g + 1) * BLOCK, :] = _bias_from_buckets(bkt, relb_ref, j * grp + g, NEG)
        mk = mk_ref[0]
        mv = mv_ref[0]
        for h in range(N_MEM_HEADS):
            sl = slice(h * HEAD_DIM, (h + 1) * HEAD_DIM)
            mk_scr[h] = mk[:, sl].astype(BF16)
            mv_scr[h] = jnp.concatenate([mv[:, sl], ones], axis=1).astype(BF16)

    qs = qs_ref[0]
    qm = qm_ref[0]
    kc = kc_ref[0]
    kp = kp_ref[0]
    vc = vc_ref[0]
    vp = vp_ref[0]
    sqg, skg, mqg = sqg_ref[...], skg_ref[...], mqg_ref[...]
    rowi = lax.broadcasted_iota(jnp.int32, (ATT_TILE, 1), 0)
    col = lax.broadcasted_iota(jnp.int32, (ATT_TILE, 2 * BLOCK), 1)
    pad_mask = jnp.logical_and(i == 0, col < BLOCK)

    hsl = [slice(h * HEAD_DIM, (h + 1) * HEAD_DIM) for h in range(N_SWA_HEADS)]
    qn = [_rms(qs[:, hsl[h]], sqg) * ATTN_SCALE for h in range(N_SWA_HEADS)]
    kn_c = [_rms(kc[:, hsl[j]], skg) for j in range(N_SWA_KV_HEADS)]
    kn_p = [_rms(kp[:, hsl[j]], skg) for j in range(N_SWA_KV_HEADS)]
    for j in range(N_SWA_KV_HEADS):
        kn_ref[0, :, hsl[j]] = kn_c[j][BLOCK:]
    chains = [(a, j) for a in range(2) for j in range(N_SWA_KV_HEADS)]
    lhs, keys, vals, sinkcol = [], [], [], []
    for a, j in chains:
        rs = slice(a * BLOCK, (a + 1) * BLOCK)
        lhs.append(jnp.concatenate([qn[j * grp + g][rs] for g in range(grp)], axis=0))
        if a == 0:
            keys.append(jnp.concatenate([kn_p[j], kn_c[j][:BLOCK]], axis=0))
            vv = jnp.concatenate([vp[:, hsl[j]], vc[:BLOCK, hsl[j]]], axis=0)
        else:
            keys.append(kn_c[j])
            vv = vc[:, hsl[j]]
        vals.append(jnp.concatenate([vv, ones], axis=1))
        sinkcol.append(jnp.where(rowi < BLOCK, sink_ref[0, j * grp], sink_ref[0, j * grp + 1]))
    qmn = [_rms(qm[:, hsl[h]], mqg) * ATTN_SCALE for h in range(N_MEM_HEADS)]

    s_w = [_dot_nt(lhs[c], keys[c]) for c in range(len(chains))]
    s_m = [_dot_nt(qmn[h], mk_scr[h]) for h in range(N_MEM_HEADS)]
    e_w, m_w = [], []
    for c, (a, j) in enumerate(chains):
        s = s_w[c] + bias_scr[j]
        if a == 0:
            s = jnp.where(pad_mask, NEG, s)
        m = jnp.maximum(jnp.max(s, -1, keepdims=True), sinkcol[c])
        m_w.append(m)
        e_w.append(jnp.exp(s - m))
    e_m = [jnp.exp(s_m[h] - jnp.max(s_m[h], -1, keepdims=True)) for h in range(N_MEM_HEADS)]
    o_w = [_dot(e_w[c], vals[c]) for c in range(len(chains))]
    o_m = [_dot(e_m[h], mv_scr[h]) for h in range(N_MEM_HEADS)]
    for c, (a, j) in enumerate(chains):
        den = o_w[c][:, HEAD_DIM:HEAD_DIM + 1] + jnp.exp(sinkcol[c] - m_w[c])
        o = o_w[c][:, :HEAD_DIM] / den
        for g in range(grp):
            ys_ref[0, a * BLOCK:(a + 1) * BLOCK, hsl[j * grp + g]] = o[g * BLOCK:(g + 1) * BLOCK]
    for h in range(N_MEM_HEADS):
        ym_ref[0, :, hsl[h]] = o_m[h][:, :HEAD_DIM] / o_m[h][:, HEAD_DIM:HEAD_DIM + 1]


def _attn_prompt(proj, mk, mv, rel_bias, sinks, sqg, skg, mqg):
    B, T, _ = proj.shape
    bkt = jnp.asarray(_prompt_bucket_table())
    smem = pl.BlockSpec(memory_space=pltpu.SMEM)
    kblk, vblk = COL_SK // SWA_KV_WIDTH, COL_SV // SWA_KV_WIDTH
    prev = lambda b, i: (b, jnp.maximum(2 * i - 1, 0))
    memb = pl.BlockSpec((1, N_MEM, MEM_WIDTH), lambda b, i: (b, 0, 0))
    return pl.pallas_call(
        _attn_prompt_kernel,
        grid=(B, T // ATT_TILE),
        in_specs=[pl.BlockSpec((1, ATT_TILE, SWA_WIDTH), lambda b, i: (b, i, COL_SQ // SWA_WIDTH)),
                  pl.BlockSpec((1, ATT_TILE, MEM_WIDTH), lambda b, i: (b, i, COL_MQ // MEM_WIDTH)),
                  pl.BlockSpec((1, ATT_TILE, SWA_KV_WIDTH), lambda b, i: (b, i, kblk)),
                  pl.BlockSpec((1, BLOCK, SWA_KV_WIDTH), lambda b, i: prev(b, i) + (kblk,)),
                  pl.BlockSpec((1, ATT_TILE, SWA_KV_WIDTH), lambda b, i: (b, i, vblk)),
                  pl.BlockSpec((1, BLOCK, SWA_KV_WIDTH), lambda b, i: prev(b, i) + (vblk,)),
                  memb, memb,
                  _full((BLOCK, 2 * BLOCK)), smem, smem,
                  _full((1, HEAD_DIM)), _full((1, HEAD_DIM)), _full((1, HEAD_DIM))],
        out_specs=[pl.BlockSpec((1, ATT_TILE, SWA_WIDTH), lambda b, i: (b, i, 0)),
                   pl.BlockSpec((1, ATT_TILE, MEM_WIDTH), lambda b, i: (b, i, 0)),
                   pl.BlockSpec((1, BLOCK, SWA_KV_WIDTH), lambda b, i: (b, 0, 0))],
        out_shape=[jax.ShapeDtypeStruct((B, T, SWA_WIDTH), F32),
                   jax.ShapeDtypeStruct((B, T, MEM_WIDTH), F32),
                   jax.ShapeDtypeStruct((B, BLOCK, SWA_KV_WIDTH), F32)],
        scratch_shapes=[pltpu.VMEM((N_SWA_KV_HEADS, 2 * BLOCK, 2 * BLOCK), F32),
                        pltpu.VMEM((N_MEM_HEADS, N_MEM, HEAD_DIM), BF16),
                        pltpu.VMEM((N_MEM_HEADS, N_MEM, 2 * HEAD_DIM), BF16)],
        compiler_params=_params("arbitrary", "arbitrary"),
        name="attn_prompt",
    )(proj, proj, proj, proj, proj, proj, mk, mv, bkt, rel_bias, sinks, sqg, skg, mqg)


def _memory_kv_kernel(mem_ref, g_ref, w_ref, kg_ref, mk_ref, mv_ref):
    kv = jnp.dot(_rms(mem_ref[0], g_ref[...]).astype(BF16), w_ref[...], preferred_element_type=F32)
    kg = kg_ref[...]
    for h in range(N_MEM_HEADS):
        sl = slice(h * HEAD_DIM, (h + 1) * HEAD_DIM)
        mk_ref[0, :, sl] = _rms(kv[:, sl], kg)
    mv_ref[0] = kv[:, MEM_WIDTH:]


def _memory_kv(mem, g, w_bf16, kg):
    B = mem.shape[0]
    blk = pl.BlockSpec((1, N_MEM, MEM_WIDTH), lambda b: (b, 0, 0))
    return pl.pallas_call(
        _memory_kv_kernel,
        grid=(B,),
        in_specs=[pl.BlockSpec((1, N_MEM, D_MODEL), lambda b: (b, 0, 0)),
                  _full((1, D_MODEL)), _full((D_MODEL, 2 * MEM_WIDTH)), _full((1, HEAD_DIM))],
        out_specs=[blk, blk],
        out_shape=[jax.ShapeDtypeStruct((B, N_MEM, MEM_WIDTH), F32)] * 2,
        compiler_params=_params("arbitrary"),
        name="memory_kv",
    )(mem, g, w_bf16, kg)


def _head_rows(x_row, n_heads):
    return jnp.concatenate([x_row[:, h * HEAD_DIM:(h + 1) * HEAD_DIM] for h in range(n_heads)], axis=0)


def _spread(xh, n_groups, rows_per_group):
    n = xh.shape[0]
    tiled = jnp.concatenate([xh] * n_groups, axis=1)
    rowi = lax.broadcasted_iota(jnp.int32, tiled.shape, 0)
    lane_grp = lax.broadcasted_iota(jnp.int32, tiled.shape, 1) // HEAD_DIM
    return jnp.where(lane_grp == rowi // rows_per_group, tiled, 0.0)


def _gather_groups(full, n_groups, rows_per_group):
    n = full.shape[0]
    rowi = lax.broadcasted_iota(jnp.int32, (n, HEAD_DIM), 0)
    out = jnp.zeros((n, HEAD_DIM), F32)
    for g in range(n_groups):
        out = out + jnp.where(rowi // rows_per_group == g, full[:, g * HEAD_DIM:(g + 1) * HEAD_DIM], 0.0)
    return out


def _decode_attn_kernel(p_ref, kbuf_ref, vbuf_ref, mk_ref, mv_ref, bkt_ref, relb_ref, sink_ref,
                        sqg_ref, skg_ref, mqg_ref, ys_ref, ym_ref, kout_ref, vout_ref):
    grp = N_SWA_HEADS // N_SWA_KV_HEADS
    hrow = lax.broadcasted_iota(jnp.int32, (8, WINDOW), 0)
    bias_w = jnp.zeros((8, WINDOW), F32)
    for h in range(N_SWA_HEADS):
        bias_w = jnp.where(hrow == h, _bias_from_buckets(bkt_ref[...], relb_ref, h, 0.0), bias_w)
    bias_w = bias_w[:N_SWA_HEADS]
    hcol = lax.broadcasted_iota(jnp.int32, (N_SWA_HEADS, 1), 0)
    bias_new = jnp.zeros((N_SWA_HEADS, 1), F32)
    sink = jnp.zeros((N_SWA_HEADS, 1), F32)
    for h in range(N_SWA_HEADS):
        bias_new = jnp.where(hcol == h, relb_ref[0, h], bias_new)
        sink = jnp.where(hcol == h, sink_ref[0, h], sink)
    rowi = lax.broadcasted_iota(jnp.int32, (WINDOW, SWA_KV_WIDTH), 0)

    for b in range(SEQ_TILE):
        prow = p_ref[b:b + 1, :]
        q = _rms(_head_rows(prow[:, COL_SQ:COL_SQ + SWA_WIDTH], N_SWA_HEADS), sqg_ref[...]) * ATTN_SCALE
        kn = _rms(_head_rows(prow[:, COL_SK:COL_SK + SWA_KV_WIDTH], N_SWA_KV_HEADS), skg_ref[...])
        kn_row = jnp.concatenate([kn[0:1], kn[1:2]], axis=1)
        vn_row = prow[:, COL_SV:COL_SV + SWA_KV_WIDTH]
        kbuf = kbuf_ref[b]
        vbuf = vbuf_ref[b]
        qm = _spread(q, N_SWA_KV_HEADS, grp)
        s = _dot_nt(qm, kbuf) + bias_w
        s_new = jnp.sum(qm * kn_row, -1, keepdims=True) + bias_new
        m = jnp.maximum(jnp.maximum(jnp.max(s, -1, keepdims=True), s_new), sink)
        e = jnp.exp(s - m)
        e_new = jnp.exp(s_new - m)
        den = jnp.sum(e, -1, keepdims=True) + e_new + jnp.exp(sink - m)
        o_full = (_dot(e, vbuf) + e_new * vn_row) / den
        o = _gather_groups(o_full, N_SWA_KV_HEADS, grp)
        for h in range(N_SWA_HEADS):
            ys_ref[b:b + 1, h * HEAD_DIM:(h + 1) * HEAD_DIM] = o[h:h + 1]
        kout_ref[b] = jnp.where(rowi == WINDOW - 1, kn_row, pltpu.roll(kbuf, WINDOW - 1, axis=0))
        vout_ref[b] = jnp.where(rowi == WINDOW - 1, vn_row, pltpu.roll(vbuf, WINDOW - 1, axis=0))
        qmem = _rms(_head_rows(prow[:, COL_MQ:COL_MQ + MEM_WIDTH], N_MEM_HEADS), mqg_ref[...]) * ATTN_SCALE
        sm = _dot_nt(_spread(qmem, N_MEM_HEADS, 1), mk_ref[b])
        em = jnp.exp(sm - jnp.max(sm, -1, keepdims=True))
        om = _gather_groups(_dot(em, mv_ref[b]), N_MEM_HEADS, 1) / jnp.sum(em, -1, keepdims=True)
        for h in range(N_MEM_HEADS):
            ym_ref[b:b + 1, h * HEAD_DIM:(h + 1) * HEAD_DIM] = om[h:h + 1]


def _decode_attn(proj, kbuf, vbuf, mk, mv, rel_bias, sinks, sqg, skg, mqg):
    n = proj.shape[0]
    bkt = jnp.asarray(_decode_bucket_table())
    smem = pl.BlockSpec(memory_space=pltpu.SMEM)
    win = pl.BlockSpec((SEQ_TILE, WINDOW, SWA_KV_WIDTH), lambda i: (i, 0, 0))
    memb = pl.BlockSpec((SEQ_TILE, N_MEM, MEM_WIDTH), lambda i: (i, 0, 0))
    return pl.pallas_call(
        _decode_attn_kernel,
        grid=(n // SEQ_TILE,),
        in_specs=[pl.BlockSpec((SEQ_TILE, IN_PROJ), lambda i: (i, 0)), win, win, memb, memb,
                  _full((8, WINDOW)), smem, smem, _full((1, HEAD_DIM)), _full((1, HEAD_DIM)), _full((1, HEAD_DIM))],
        out_specs=[pl.BlockSpec((SEQ_TILE, SWA_WIDTH), lambda i: (i, 0)),
                   pl.BlockSpec((SEQ_TILE, MEM_WIDTH), lambda i: (i, 0)), win, win],
        out_shape=[jax.ShapeDtypeStruct((n, SWA_WIDTH), F32), jax.ShapeDtypeStruct((n, MEM_WIDTH), F32),
                   jax.ShapeDtypeStruct(kbuf.shape, F32), jax.ShapeDtypeStruct(vbuf.shape, F32)],
        compiler_params=_params("arbitrary"),
        name="decode_attn",
    )(proj, kbuf, vbuf, mk, mv, bkt, rel_bias, sinks, sqg, skg, mqg)


FF_CHUNK = 1024


def _out_ffn_kernel(x_ref, yr_ref, ys_ref, ym_ref, wo_ref, g2_ref, w1_ref, w2_ref, o_ref):
    x1 = (x_ref[...]
          + jnp.dot(yr_ref[...].astype(BF16), wo_ref[0:RWKV_WIDTH, :], preferred_element_type=F32)
          + jnp.dot(ys_ref[...].astype(BF16), wo_ref[RWKV_WIDTH:RWKV_WIDTH + SWA_WIDTH, :],
                    preferred_element_type=F32)
          + jnp.dot(ym_ref[...].astype(BF16), wo_ref[RWKV_WIDTH + SWA_WIDTH:, :], preferred_element_type=F32))
    h2 = _rms(x1, g2_ref[...]).astype(BF16)
    ff = None
    for c in range(D_FF // FF_CHUNK):
        u = jnp.dot(h2, w1_ref[:, c * FF_CHUNK:(c + 1) * FF_CHUNK], preferred_element_type=F32)
        u = jnp.square(jnp.maximum(u, 0.0)).astype(BF16)
        d = jnp.dot(u, w2_ref[c * FF_CHUNK:(c + 1) * FF_CHUNK, :], preferred_element_type=F32)
        ff = d if ff is None else ff + d
    o_ref[...] = x1 + ff


def _out_ffn(x2d, yr, ys, ym, wo, g2, w1, w2, tm):
    n = x2d.shape[0]
    rows = lambda w: pl.BlockSpec((tm, w), lambda i: (i, 0))
    const = lambda shape: pl.BlockSpec(shape, lambda i: (0, 0), pipeline_mode=pl.Buffered(1))
    return pl.pallas_call(
        _out_ffn_kernel,
        grid=(n // tm,),
        in_specs=[rows(D_MODEL), rows(RWKV_WIDTH), rows(SWA_WIDTH), rows(MEM_WIDTH),
                  const((D_MODEL, D_MODEL)), _full((1, D_MODEL)), const((D_MODEL, D_FF)), const((D_FF, D_MODEL))],
        out_specs=rows(D_MODEL),
        out_shape=jax.ShapeDtypeStruct((n, D_MODEL), F32),
        compiler_params=_params("arbitrary"),
        name="out_ffn",
    )(x2d, yr, ys, ym, wo, g2, w1, w2)


def kernel(x_prompt, x_sample, state_rwkv, state_shift, cache_swa_k, cache_swa_v, cache_mem_k, cache_mem_v,
           mem_prompt, rel_bias, norm1_g, w_in, mu_shift, w0, w_up_w, a0, w_up_a, w_up_g, k_k, k_a, r_k,
           lnx_w, lnx_b, q_norm_swa, k_norm_swa, sinks, mem_norm_g, w_mem_kv, q_norm_mem, k_norm_mem,
           w_out, norm2_g, w_ff1, w_ff2):
    B, T, _ = x_prompt.shape
    Bd = x_sample.shape[0]
    l = 0
    w_in_b = w_in[l].astype(BF16)
    w_out_b = w_out[l].astype(BF16)
    w1_b = w_ff1[l].astype(BF16)
    w2_b = w_ff2[l].astype(BF16)
    rwkv_params = (mu_shift[l][None], w0[l][None], a0[l][None], k_k[l][None], k_a[l][None],
                   r_k[l].reshape(1, RWKV_WIDTH), lnx_w[l][None], lnx_b[l][None],
                   w_up_w[l].astype(BF16), w_up_a[l].astype(BF16), w_up_g[l].astype(BF16))
    sqg, skg, mqg, mkg = q_norm_swa[l][None], k_norm_swa[l][None], q_norm_mem[l][None], k_norm_mem[l][None]
    g1, g2 = norm1_g[l][None], norm2_g[l][None]

    xp = x_prompt.reshape(B * T, D_MODEL)
    proj_p = _in_proj(xp, g1, w_in_b, 512).reshape(B, T, IN_PROJ)
    mk, mv = _memory_kv(mem_prompt, mem_norm_g[l][None], w_mem_kv[l].astype(BF16), mkg)
    yr_p, s_p = _rwkv_prompt(proj_p, *rwkv_params)
    ys_p, ym_p, kn_p = _attn_prompt(proj_p, mk, mv, rel_bias, sinks[l][None], sqg, skg, mqg)
    y_p = _out_ffn(xp, yr_p.reshape(B * T, RWKV_WIDTH), ys_p.reshape(B * T, SWA_WIDTH),
                   ym_p.reshape(B * T, MEM_WIDTH), w_out_b, g2, w1_b, w2_b, 512).reshape(B, T, D_MODEL)
    shift_p = proj_p[:, T - 1, :RWKV_PROJ]
    vb_p = proj_p[:, T - WINDOW:, COL_SV:COL_SV + SWA_KV_WIDTH]

    xs = x_sample.reshape(Bd, D_MODEL)
    proj_s = _in_proj(xs, g1, w_in_b, Bd)
    yr_s, s_s = _rwkv_step(proj_s[:, :RWKV_PROJ], state_shift[l], state_rwkv[l], *rwkv_params)
    ys_s, ym_s, kb_s, vb_s = _decode_attn(
        proj_s, cache_swa_k[l].reshape(Bd, WINDOW, SWA_KV_WIDTH), cache_swa_v[l].reshape(Bd, WINDOW, SWA_KV_WIDTH),
        cache_mem_k[l].reshape(Bd, N_MEM, MEM_WIDTH), cache_mem_v[l].reshape(Bd, N_MEM, MEM_WIDTH),
        rel_bias, sinks[l][None], sqg, skg, mqg)
    y_s = _out_ffn(xs, yr_s, ys_s, ym_s, w_out_b, g2, w1_b, w2_b, Bd).reshape(Bd, 1, D_MODEL)

    return (y_p, y_s,
            s_p[None], shift_p[None],
            kn_p.reshape(1, B, WINDOW, N_SWA_KV_HEADS, HEAD_DIM),
            vb_p.reshape(1, B, WINDOW, N_SWA_KV_HEADS, HEAD_DIM),
            mk.reshape(1, B, N_MEM, N_MEM_HEADS, HEAD_DIM), mv.reshape(1, B, N_MEM, N_MEM_HEADS, HEAD_DIM),
            s_s[None], proj_s[:, :RWKV_PROJ][None],
            kb_s.reshape(1, Bd, WINDOW, N_SWA_KV_HEADS, HEAD_DIM),
            vb_s.reshape(1, Bd, WINDOW, N_SWA_KV_HEADS, HEAD_DIM))
```

```python
import functools
import math

import numpy as np
import jax
import jax.numpy as jnp
from jax import lax
from jax.experimental import pallas as pl
from jax.experimental.pallas import tpu as pltpu

F32 = jnp.float32
BF16 = jnp.bfloat16

D_MODEL = 1024
HEAD_DIM = 64
RWKV_WIDTH = 512
N_RWKV_HEADS = 8
SWA_WIDTH = 256
N_SWA_HEADS = 4
N_SWA_KV_HEADS = 2
SWA_KV_WIDTH = 128
MEM_WIDTH = 256
N_MEM_HEADS = 4
N_MEM = 256
WINDOW = 128
BLOCK = 128
N_BUCKETS = 32
MAX_DISTANCE = 128
DECAY_LORA = 64
AAA_LORA = 64
GATE_LORA = 128
RWKV_PROJ = 3 * RWKV_WIDTH + DECAY_LORA + AAA_LORA + GATE_LORA
SWA_PROJ = SWA_WIDTH + 2 * SWA_KV_WIDTH
IN_PROJ = RWKV_PROJ + SWA_PROJ + MEM_WIDTH
D_FF = 4 * D_MODEL
NORM_EPS = 1e-6
LNX_EPS = 64e-5
ATTN_SCALE = HEAD_DIM ** -0.5
EXP_M05 = math.exp(-0.5)
NEG = -1e30

COL_R, COL_K, COL_V = 0, RWKV_WIDTH, 2 * RWKV_WIDTH
COL_WD = 3 * RWKV_WIDTH
COL_AD = COL_WD + DECAY_LORA
COL_GD = COL_AD + AAA_LORA
COL_SQ = RWKV_PROJ
COL_SK = COL_SQ + SWA_WIDTH
COL_SV = COL_SK + SWA_KV_WIDTH
COL_MQ = RWKV_PROJ + SWA_PROJ

CHUNK = 64
VMEM_LIMIT = 56 * 1024 * 1024


def _dot(a, b):
    return jnp.dot(a.astype(BF16), b.astype(BF16), preferred_element_type=F32)


def _dot_nt(a, b):
    return lax.dot_general(a.astype(BF16), b.astype(BF16), (((1,), (1,)), ((), ())),
                           preferred_element_type=F32)


def _dot_tn(a, b):
    return lax.dot_general(a.astype(BF16), b.astype(BF16), (((0,), (0,)), ((), ())),
                           preferred_element_type=F32)


def _rms(x, g):
    return x * lax.rsqrt(jnp.mean(x * x, -1, keepdims=True) + NORM_EPS) * g


def _params(*sem):
    return pltpu.CompilerParams(dimension_semantics=sem, vmem_limit_bytes=VMEM_LIMIT)


def _full(shape):
    n = len(shape)
    return pl.BlockSpec(shape, lambda *_: (0,) * n)


def _in_proj_kernel(x_ref, g_ref, w_ref, o_ref):
    h = _rms(x_ref[...], g_ref[...])
    o_ref[...] = jnp.dot(h.astype(BF16), w_ref[...], preferred_element_type=F32)


def _in_proj(x2d, g, w_bf16, tm):
    n = x2d.shape[0]
    return pl.pallas_call(
        _in_proj_kernel,
        grid=(n // tm,),
        in_specs=[pl.BlockSpec((tm, D_MODEL), lambda i: (i, 0)),
                  _full((1, D_MODEL)),
                  _full((D_MODEL, IN_PROJ))],
        out_specs=pl.BlockSpec((tm, IN_PROJ), lambda i: (i, 0)),
        out_shape=jax.ShapeDtypeStruct((n, IN_PROJ), F32),
        compiler_params=_params("arbitrary"),
        name="in_proj",
    )(x2d, g, w_bf16)


def _rwkv_features(xs, w0, a0, k_k, k_a, wupw, wupa, wupg):
    r = xs[:, COL_R:COL_R + RWKV_WIDTH]
    k = xs[:, COL_K:COL_K + RWKV_WIDTH]
    v = xs[:, COL_V:COL_V + RWKV_WIDTH]
    wd = xs[:, COL_WD:COL_WD + DECAY_LORA]
    ad = xs[:, COL_AD:COL_AD + AAA_LORA]
    gd = xs[:, COL_GD:COL_GD + GATE_LORA]
    logw = -jax.nn.sigmoid(w0 + _dot(jnp.tanh(wd), wupw)) * EXP_M05
    a_sig = jax.nn.sigmoid(a0 + _dot(ad, wupa))
    gate = _dot(jax.nn.sigmoid(gd), wupg)
    kk = k * k_k
    k2 = k * (1.0 + (a_sig - 1.0) * k_a)
    return r, k2, v, kk, a_sig, logw, gate


def _head_out(y, r, k2, v, gate, r_k, lnx_w, lnx_b):
    m = jnp.mean(y, -1, keepdims=True)
    d = y - m
    var = jnp.mean(d * d, -1, keepdims=True)
    yn = d * lax.rsqrt(var + LNX_EPS) * lnx_w + lnx_b
    bonus = jnp.sum(r * k2 * r_k, -1, keepdims=True) * v
    return (yn + bonus) * gate


RWKV_TILE = 2 * CHUNK
PAIR = 2 * HEAD_DIM
N_PAIRS = N_RWKV_HEADS // 2


def _seg_sum(x, blk):
    xb = x.astype(BF16)
    half = RWKV_WIDTH // 2
    return jnp.concatenate([jnp.dot(xb[:, :half], blk, preferred_element_type=F32),
                            jnp.dot(xb[:, half:], blk, preferred_element_type=F32)], axis=1)


def _rwkv_prompt_kernel(p_ref, mu_ref, w0_ref, a0_ref, kk_ref, ka_ref, rk_ref, lnw_ref, lnb_ref,
                        wupw_ref, wupa_ref, wupg_ref, y_ref, sout_ref, s_scr, prev_scr):
    C, TT, D = CHUNK, RWKV_TILE, HEAD_DIM
    NC = TT // C
    t = pl.program_id(1)

    @pl.when(t == 0)
    def _():
        s_scr[...] = jnp.zeros_like(s_scr)
        prev_scr[...] = jnp.zeros_like(prev_scr)

    p = p_ref[0]
    row = lax.broadcasted_iota(jnp.int32, p.shape, 0)
    prev = jnp.where(row == 0, prev_scr[...], pltpu.roll(p, 1, axis=0))
    prev_scr[...] = p[TT - 1:TT, :]
    xs = p + (prev - p) * mu_ref[...]
    r, k2, v, kk, a_sig, logw, gate = _rwkv_features(
        xs, w0_ref[...], a0_ref[...], kk_ref[...], ka_ref[...], wupw_ref[...], wupa_ref[...], wupg_ref[...])

    half = RWKV_WIDTH // 2
    bi = lax.broadcasted_iota(jnp.int32, (half, half), 0) // D
    bj = lax.broadcasted_iota(jnp.int32, (half, half), 1) // D
    blk = jnp.where(bi == bj, 1.0, 0.0).astype(BF16)
    kkn = kk / jnp.maximum(jnp.sqrt(_seg_sum(kk * kk, blk)), 1e-12)
    bb = kkn * a_sig

    ri = lax.broadcasted_iota(jnp.int32, (TT, TT), 0)
    ci = lax.broadcasted_iota(jnp.int32, (TT, TT), 1)
    tri = jnp.where(jnp.logical_and(ri >= ci, ri // C == ci // C), 1.0, 0.0).astype(BF16)
    l1 = logw.astype(BF16)
    rem = logw - l1.astype(F32)
    l2 = rem.astype(BF16)
    l3 = (rem - l2.astype(F32)).astype(BF16)
    cum = (jnp.dot(tri, l1, preferred_element_type=F32) + jnp.dot(tri, l2, preferred_element_type=F32)
           + jnp.dot(tri, l3, preferred_element_type=F32))
    c_last = jnp.concatenate([jnp.broadcast_to(cum[(c + 1) * C - 1:(c + 1) * C, :], (C, RWKV_WIDTH))
                              for c in range(NC)], axis=0)
    e_pos = jnp.exp(cum)
    e_neg = jnp.exp(-cum)
    e_prev = jnp.exp(cum - logw)
    e_last = jnp.exp(c_last - cum)

    lo_full = (lax.broadcasted_iota(jnp.int32, (TT, RWKV_WIDTH), 1) % PAIR) < D
    at_f = -kkn * e_prev
    rt_f = r * e_pos
    at_lo = jnp.where(lo_full, at_f, 0.0).astype(BF16)
    at_hi = jnp.where(lo_full, 0.0, at_f).astype(BF16)
    rt_lo = jnp.where(lo_full, rt_f, 0.0).astype(BF16)
    rt_hi = jnp.where(lo_full, 0.0, rt_f).astype(BF16)
    bt_b = (bb * e_neg).astype(BF16)
    kt_b = (k2 * e_neg).astype(BF16)
    bh_b = (bb * e_last).astype(BF16)
    kh_b = (k2 * e_last).astype(BF16)
    v_b = v.astype(BF16)

    r2 = lax.broadcasted_iota(jnp.int32, (C, PAIR), 0)
    c2 = lax.broadcasted_iota(jnp.int32, (C, PAIR), 1)
    c2m = jnp.where(c2 >= C, c2 - C, c2)
    mask_ak = jnp.logical_and(c2 >= C, c2m < r2)
    mask_r = c2m <= r2
    strict = lax.broadcasted_iota(jnp.int32, (C, C), 1) < lax.broadcasted_iota(jnp.int32, (C, C), 0)
    lo = c2 < D
    qi = lax.broadcasted_iota(jnp.int32, (PAIR, PAIR), 0) // D
    qj = lax.broadcasted_iota(jnp.int32, (PAIR, PAIR), 1) // D
    diag = qi == qj

    PR = [(c, q) for c in range(NC) for q in range(N_PAIRS)]
    n = len(PR)
    E = range(2)
    win = lambda x, c, q: x[c * C:(c + 1) * C, q * PAIR:(q + 1) * PAIR]
    sc = [_dot_nt(jnp.concatenate([win(at_lo, c, q), win(at_hi, c, q), win(rt_lo, c, q), win(rt_hi, c, q)], axis=0),
                  jnp.concatenate([win(bt_b, c, q), win(kt_b, c, q)], axis=0)) for c, q in PR]
    vr = [pltpu.roll(win(v, c, q), D, axis=1) for c, q in PR]
    vvr = [jnp.concatenate([vr[i], vr[i]], axis=0).astype(BF16) for i in range(n)]
    A = [[jnp.where(strict, sc[i][e * C:(e + 1) * C, :C], 0.0) for e in E] for i in range(n)]
    m_ak = [[jnp.where(mask_ak, sc[i][e * C:(e + 1) * C], 0.0) for e in E] for i in range(n)]
    m_r = [[jnp.where(mask_r, sc[i][(2 + e) * C:(3 + e) * C], 0.0) for e in E] for i in range(n)]

    zf = [[_dot(m_ak[i][e], vvr[i]) for e in E] for i in range(n)]
    X = [[jnp.where(lo, win(at_f, c, q), zf[i][0]), jnp.where(lo, zf[i][1], win(at_f, c, q))]
         for i, (c, q) in enumerate(PR)]
    for k in range(int(math.log2(C))):
        AW = [[_dot(A[i][e], jnp.concatenate([X[i][e], A[i][e]], axis=1)) for e in E] for i in range(n)]
        X = [[X[i][e] + AW[i][e][:, :PAIR] for e in E] for i in range(n)]
        A = [[AW[i][e][:, PAIR:] for e in E] for i in range(n)]

    S = [s_scr[q] for q in range(N_PAIRS)]
    ys = []
    for c in range(NC):
        w_last = jnp.exp(cum[(c + 1) * C - 1:(c + 1) * C, :])
        idx = [c * N_PAIRS + q for q in range(N_PAIRS)]
        mkg = [[_dot_tn(X[i][e], win(bh_b, c, q)) for e in E] for q, i in enumerate(idx)]
        vk = [_dot_tn(win(v_b, c, q), win(kh_b, c, q)) for q, i in enumerate(idx)]
        ry = [[_dot(m_r[i][0], jnp.concatenate([X[i][0], jnp.where(lo, 0.0, vr[i])], axis=0)),
               _dot(m_r[i][1], jnp.concatenate([X[i][1], jnp.where(lo, vr[i], 0.0)], axis=0))] for i in idx]
        rp = [win(rt_f, c, q) + jnp.where(lo, ry[q][0], ry[q][1]) for q in range(N_PAIRS)]
        y0 = [pltpu.roll(jnp.where(lo, ry[q][1], ry[q][0]), D, axis=1) for q in range(N_PAIRS)]
        mk = [jnp.where(diag, jnp.concatenate([mkg[q][0][:D], mkg[q][1][D:]], axis=0), 0.0) for q in range(N_PAIRS)]
        g = [jnp.where(diag, vk[q] + jnp.concatenate([mkg[q][0][D:], mkg[q][1][:D]], axis=0), 0.0)
             for q in range(N_PAIRS)]
        y = [_dot_nt(rp[q], S[q]) for q in range(N_PAIRS)]
        dS = [_dot(S[q], mk[q]) for q in range(N_PAIRS)]
        S = [S[q] * w_last[:, q * PAIR:(q + 1) * PAIR] + dS[q] + g[q] for q in range(N_PAIRS)]
        ys.append(jnp.concatenate([y[q] + y0[q] for q in range(N_PAIRS)], axis=1))
    for q in range(N_PAIRS):
        s_scr[q] = S[q]

    yf = jnp.concatenate(ys, axis=0)
    inv_d = 1.0 / D
    m = _seg_sum(yf, blk) * inv_d
    d = yf - m
    var = _seg_sum(d * d, blk) * inv_d
    yn = d * lax.rsqrt(var + LNX_EPS) * lnw_ref[...] + lnb_ref[...]
    bonus = _seg_sum(r * k2 * rk_ref[...], blk) * v
    y_ref[0] = (yn + bonus) * gate

    @pl.when(t == pl.num_programs(1) - 1)
    def _():
        for q in range(N_PAIRS):
            sout_ref[0, 2 * q] = S[q][:D, :D]
            sout_ref[0, 2 * q + 1] = S[q][D:, D:]


def _rwkv_prompt(proj, mu, w0, a0, k_k, k_a, r_k, lnx_w, lnx_b, wupw, wupa, wupg):
    B, T, _ = proj.shape
    vec = lambda n: _full((1, n))
    return pl.pallas_call(
        _rwkv_prompt_kernel,
        grid=(B, T // RWKV_TILE),
        in_specs=[pl.BlockSpec((1, RWKV_TILE, RWKV_PROJ), lambda b, t: (b, t, 0)),
                  vec(RWKV_PROJ), vec(RWKV_WIDTH), vec(RWKV_WIDTH), vec(RWKV_WIDTH), vec(RWKV_WIDTH),
                  vec(RWKV_WIDTH), vec(RWKV_WIDTH), vec(RWKV_WIDTH),
                  _full((DECAY_LORA, RWKV_WIDTH)), _full((AAA_LORA, RWKV_WIDTH)), _full((GATE_LORA, RWKV_WIDTH))],
        out_specs=[pl.BlockSpec((1, RWKV_TILE, RWKV_WIDTH), lambda b, t: (b, t, 0)),
                   pl.BlockSpec((1, N_RWKV_HEADS, HEAD_DIM, HEAD_DIM), lambda b, t: (b, 0, 0, 0))],
        out_shape=[jax.ShapeDtypeStruct((B, T, RWKV_WIDTH), F32),
                   jax.ShapeDtypeStruct((B, N_RWKV_HEADS, HEAD_DIM, HEAD_DIM), F32)],
        scratch_shapes=[pltpu.VMEM((N_PAIRS, PAIR, PAIR), F32),
                        pltpu.VMEM((1, RWKV_PROJ), F32)],
        compiler_params=_params("arbitrary", "arbitrary"),
        name="rwkv_prompt",
    )(proj, mu, w0, a0, k_k, k_a, r_k, lnx_w, lnx_b, wupw, wupa, wupg)


SEQ_TILE = 8


def _rwkv_step_kernel(p_ref, sh_ref, s_ref, mu_ref, w0_ref, a0_ref, kk_ref, ka_ref, rk_ref, lnw_ref, lnb_ref,
                      wupw_ref, wupa_ref, wupg_ref, y_ref, sout_ref):
    p = p_ref[...]
    xs = p + (sh_ref[...] - p) * mu_ref[...]
    r, k2, v, kk, a_sig, logw, gate = _rwkv_features(
        xs, w0_ref[...], a0_ref[...], kk_ref[...], ka_ref[...], wupw_ref[...], wupa_ref[...], wupg_ref[...])
    w = jnp.exp(logw)
    rk_all, lnw_all, lnb_all = rk_ref[...], lnw_ref[...], lnb_ref[...]
    H = range(N_RWKV_HEADS)
    hs = [slice(h * HEAD_DIM, (h + 1) * HEAD_DIM) for h in H]
    P = [(b, h) for h in H for b in range(SEQ_TILE)]
    a_h, b_h = [], []
    for h in H:
        kk_h = kk[:, hs[h]]
        kk_h = kk_h / jnp.maximum(jnp.sqrt(jnp.sum(kk_h * kk_h, -1, keepdims=True)), 1e-12)
        a_h.append(-kk_h)
        b_h.append(kk_h * a_sig[:, hs[h]])
    row = lambda x, b, h: x[b:b + 1, hs[h]]
    S = [s_ref[b, h] for b, h in P]
    sa = [_dot_nt(a_h[h], S[i])[b:b + 1] for i, (b, h) in enumerate(P)]
    upd = [_dot_tn(jnp.concatenate([sa[i], row(v, b, h)], axis=0),
                   jnp.concatenate([b_h[h][b:b + 1], row(k2, b, h)], axis=0)) for i, (b, h) in enumerate(P)]
    S = [S[i] * row(w, b, h) + upd[i] for i, (b, h) in enumerate(P)]
    for i, (b, h) in enumerate(P):
        sout_ref[b, h] = S[i]
    y_row = [_dot_nt(r[:, hs[h]], S[i])[b:b + 1] for i, (b, h) in enumerate(P)]
    for h in H:
        y = jnp.concatenate(y_row[h * SEQ_TILE:(h + 1) * SEQ_TILE], axis=0)
        y_ref[:, hs[h]] = _head_out(y, r[:, hs[h]], k2[:, hs[h]], v[:, hs[h]], gate[:, hs[h]],
                                    rk_all[:, hs[h]], lnw_all[:, hs[h]], lnb_all[:, hs[h]])


def _rwkv_step(proj, shift, state, mu, w0, a0, k_k, k_a, r_k, lnx_w, lnx_b, wupw, wupa, wupg):
    n = proj.shape[0]
    vec = lambda m: _full((1, m))
    st_spec = pl.BlockSpec((SEQ_TILE, N_RWKV_HEADS, HEAD_DIM, HEAD_DIM), lambda i: (i, 0, 0, 0))
    return pl.pallas_call(
        _rwkv_step_kernel,
        grid=(n // SEQ_TILE,),
        in_specs=[pl.BlockSpec((SEQ_TILE, RWKV_PROJ), lambda i: (i, 0)),
                  pl.BlockSpec((SEQ_TILE, RWKV_PROJ), lambda i: (i, 0)),
                  st_spec,
                  vec(RWKV_PROJ), vec(RWKV_WIDTH), vec(RWKV_WIDTH), vec(RWKV_WIDTH), vec(RWKV_WIDTH),
                  vec(RWKV_WIDTH), vec(RWKV_WIDTH), vec(RWKV_WIDTH),
                  _full((DECAY_LORA, RWKV_WIDTH)), _full((AAA_LORA, RWKV_WIDTH)), _full((GATE_LORA, RWKV_WIDTH))],
        out_specs=[pl.BlockSpec((SEQ_TILE, RWKV_WIDTH), lambda i: (i, 0)), st_spec],
        out_shape=[jax.ShapeDtypeStruct((n, RWKV_WIDTH), F32),
                   jax.ShapeDtypeStruct(state.shape, F32)],
        compiler_params=_params("arbitrary"),
        name="rwkv_step",
    )(proj, shift, state, mu, w0, a0, k_k, k_a, r_k, lnx_w, lnx_b, wupw, wupa, wupg)


def _t5_bucket_np(dist):
    max_exact = N_BUCKETS // 2
    d = np.maximum(dist, 1).astype(np.float32)
    large = max_exact + (np.log(d / np.float32(max_exact)) / np.float32(math.log(MAX_DISTANCE / max_exact))
                         * np.float32(N_BUCKETS - max_exact)).astype(np.int32)
    large = np.minimum(large, N_BUCKETS - 1)
    return np.where(dist < max_exact, dist, large).astype(np.int32)


def _prompt_bucket_table():
    qi = np.arange(BLOCK)[:, None]
    kj = np.arange(2 * BLOCK)[None, :]
    dist = BLOCK + qi - kj
    valid = (dist >= 0) & (dist <= WINDOW)
    return np.where(valid, _t5_bucket_np(np.maximum(dist, 0)), -1).astype(np.int32)


def _decode_bucket_table():
    dist = WINDOW - np.arange(WINDOW)
    return np.broadcast_to(_t5_bucket_np(dist)[None, :], (8, WINDOW)).astype(np.int32).copy()


def _bias_from_buckets(bkt, relb_ref, h, init):
    acc = jnp.full(bkt.shape, init, F32)
    for b in range(N_BUCKETS):
        acc = jnp.where(bkt == b, relb_ref[b, h], acc)
    return acc


ATT_TILE = 2 * BLOCK


def _attn_prompt_kernel(qs_ref, qm_ref, kc_ref, kp_ref, vc_ref, vp_ref, mk_ref, mv_ref, bkt_ref, relb_ref, sink_ref,
                        sqg_ref, skg_ref, mqg_ref, ys_ref, ym_ref, kn_ref, bias_scr, mk_scr, mv_scr):
    i = pl.program_id(1)
    grp = N_SWA_HEADS // N_SWA_KV_HEADS
    ones = jnp.ones((ATT_TILE, HEAD_DIM), F32)

    @pl.when(i == 0)
    def _():
        bkt = bkt_ref[...]
        for j in range(N_SWA_KV_HEADS):
            for g in range(grp):
                bias_scr[j, g * BLOCK:(g + 1) * BLOCK, :] = _bias_from_buckets(bkt, relb_ref, j * grp + g, NEG)
        mk = mk_ref[0]
        mv = mv_ref[0]
        for h in range(N_MEM_HEADS):
            sl = slice(h * HEAD_DIM, (h + 1) * HEAD_DIM)
            mk_scr[h] = mk[:, sl].astype(BF16)
            mv_scr[h] = jnp.concatenate([mv[:, sl], ones], axis=1).astype(BF16)

    qs = qs_ref[0]
    qm = qm_ref[0]
    kc = kc_ref[0]
    kp = kp_ref[0]
    vc = vc_ref[0]
    vp = vp_ref[0]
    sqg, skg, mqg = sqg_ref[...], skg_ref[...], mqg_ref[...]
    rowi = lax.broadcasted_iota(jnp.int32, (ATT_TILE, 1), 0)
    col = lax.broadcasted_iota(jnp.int32, (ATT_TILE, 2 * BLOCK), 1)
    pad_mask = jnp.logical_and(i == 0, col < BLOCK)

    hsl = [slice(h * HEAD_DIM, (h + 1) * HEAD_DIM) for h in range(N_SWA_HEADS)]
    qn = [_rms(qs[:, hsl[h]], sqg) * ATTN_SCALE for h in range(N_SWA_HEADS)]
    kn_c = [_rms(kc[:, hsl[j]], skg) for j in range(N_SWA_KV_HEADS)]
    kn_p = [_rms(kp[:, hsl[j]], skg) for j in range(N_SWA_KV_HEADS)]
    for j in range(N_SWA_KV_HEADS):
        kn_ref[0, :, hsl[j]] = kn_c[j][BLOCK:]
    chains = [(a, j) for a in range(2) for j in range(N_SWA_KV_HEADS)]
    lhs, keys, vals, sinkcol = [], [], [], []
    for a, j in chains:
        rs = slice(a * BLOCK, (a + 1) * BLOCK)
        lhs.append(jnp.concatenate([qn[j * grp + g][rs] for g in range(grp)], axis=0))
        if a == 0:
            keys.append(jnp.concatenate([kn_p[j], kn_c[j][:BLOCK]], axis=0))
            vv = jnp.concatenate([vp[:, hsl[j]], vc[:BLOCK, hsl[j]]], axis=0)
        else:
            keys.append(kn_c[j])
            vv = vc[:, hsl[j]]
        vals.append(jnp.concatenate([vv, ones], axis=1))
        sinkcol.append(jnp.where(rowi < BLOCK, sink_ref[0, j * grp], sink_ref[0, j * grp + 1]))
    qmn = [_rms(qm[:, hsl[h]], mqg) * ATTN_SCALE for h in range(N_MEM_HEADS)]

    s_w = [_dot_nt(lhs[c], keys[c]) for c in range(len(chains))]
    s_m = [_dot_nt(qmn[h], mk_scr[h]) for h in range(N_MEM_HEADS)]
    e_w, m_w = [], []
    for c, (a, j) in enumerate(chains):
        s = s_w[c] + bias_scr[j]
        if a == 0:
            s = jnp.where(pad_mask, NEG, s)
        m = jnp.maximum(jnp.max(s, -1, keepdims=True), sinkcol[c])
        m_w.append(m)
        e_w.append(jnp.exp(s - m))
    e_m = [jnp.exp(s_m[h] - jnp.max(s_m[h], -1, keepdims=True)) for h in range(N_MEM_HEADS)]
    o_w = [_dot(e_w[c], vals[c]) for c in range(len(chains))]
    o_m = [_dot(e_m[h], mv_scr[h]) for h in range(N_MEM_HEADS)]
    for c, (a, j) in enumerate(chains):
        den = o_w[c][:, HEAD_DIM:HEAD_DIM + 1] + jnp.exp(sinkcol[c] - m_w[c])
        o = o_w[c][:, :HEAD_DIM] / den
        for g in range(grp):
            ys_ref[0, a * BLOCK:(a + 1) * BLOCK, hsl[j * grp + g]] = o[g * BLOCK:(g + 1) * BLOCK]
    for h in range(N_MEM_HEADS):
        ym_ref[0, :, hsl[h]] = o_m[h][:, :HEAD_DIM] / o_m[h][:, HEAD_DIM:HEAD_DIM + 1]


def _attn_prompt(proj, mk, mv, rel_bias, sinks, sqg, skg, mqg):
    B, T, _ = proj.shape
    bkt = jnp.asarray(_prompt_bucket_table())
    smem = pl.BlockSpec(memory_space=pltpu.SMEM)
    kblk, vblk = COL_SK // SWA_KV_WIDTH, COL_SV // SWA_KV_WIDTH
    prev = lambda b, i: (b, jnp.maximum(2 * i - 1, 0))
    memb = pl.BlockSpec((1, N_MEM, MEM_WIDTH), lambda b, i: (b, 0, 0))
    return pl.pallas_call(
        _attn_prompt_kernel,
        grid=(B, T // ATT_TILE),
        in_specs=[pl.BlockSpec((1, ATT_TILE, SWA_WIDTH), lambda b, i: (b, i, COL_SQ // SWA_WIDTH)),
                  pl.BlockSpec((1, ATT_TILE, MEM_WIDTH), lambda b, i: (b, i, COL_MQ // MEM_WIDTH)),
                  pl.BlockSpec((1, ATT_TILE, SWA_KV_WIDTH), lambda b, i: (b, i, kblk)),
                  pl.BlockSpec((1, BLOCK, SWA_KV_WIDTH), lambda b, i: prev(b, i) + (kblk,)),
                  pl.BlockSpec((1, ATT_TILE, SWA_KV_WIDTH), lambda b, i: (b, i, vblk)),
                  pl.BlockSpec((1, BLOCK, SWA_KV_WIDTH), lambda b, i: prev(b, i) + (vblk,)),
                  memb, memb,
                  _full((BLOCK, 2 * BLOCK)), smem, smem,
                  _full((1, HEAD_DIM)), _full((1, HEAD_DIM)), _full((1, HEAD_DIM))],
        out_specs=[pl.BlockSpec((1, ATT_TILE, SWA_WIDTH), lambda b, i: (b, i, 0)),
                   pl.BlockSpec((1, ATT_TILE, MEM_WIDTH), lambda b, i: (b, i, 0)),
                   pl.BlockSpec((1, BLOCK, SWA_KV_WIDTH), lambda b, i: (b, 0, 0))],
        out_shape=[jax.ShapeDtypeStruct((B, T, SWA_WIDTH), F32),
                   jax.ShapeDtypeStruct((B, T, MEM_WIDTH), F32),
                   jax.ShapeDtypeStruct((B, BLOCK, SWA_KV_WIDTH), F32)],
        scratch_shapes=[pltpu.VMEM((N_SWA_KV_HEADS, 2 * BLOCK, 2 * BLOCK), F32),
                        pltpu.VMEM((N_MEM_HEADS, N_MEM, HEAD_DIM), BF16),
                        pltpu.VMEM((N_MEM_HEADS, N_MEM, 2 * HEAD_DIM), BF16)],
        compiler_params=_params("arbitrary", "arbitrary"),
        name="attn_prompt",
    )(proj, proj, proj, proj, proj, proj, mk, mv, bkt, rel_bias, sinks, sqg, skg, mqg)


def _memory_kv_kernel(mem_ref, g_ref, w_ref, kg_ref, mk_ref, mv_ref):
    kv = jnp.dot(_rms(mem_ref[0], g_ref[...]).astype(BF16), w_ref[...], preferred_element_type=F32)
    kg = kg_ref[...]
    for h in range(N_MEM_HEADS):
        sl = slice(h * HEAD_DIM, (h + 1) * HEAD_DIM)
        mk_ref[0, :, sl] = _rms(kv[:, sl], kg)
    mv_ref[0] = kv[:, MEM_WIDTH:]


def _memory_kv(mem, g, w_bf16, kg):
    B = mem.shape[0]
    blk = pl.BlockSpec((1, N_MEM, MEM_WIDTH), lambda b: (b, 0, 0))
    return pl.pallas_call(
        _memory_kv_kernel,
        grid=(B,),
        in_specs=[pl.BlockSpec((1, N_MEM, D_MODEL), lambda b: (b, 0, 0)),
                  _full((1, D_MODEL)), _full((D_MODEL, 2 * MEM_WIDTH)), _full((1, HEAD_DIM))],
        out_specs=[blk, blk],
        out_shape=[jax.ShapeDtypeStruct((B, N_MEM, MEM_WIDTH), F32)] * 2,
        compiler_params=_params("arbitrary"),
        name="memory_kv",
    )(mem, g, w_bf16, kg)


def _head_rows(x_row, n_heads):
    return jnp.concatenate([x_row[:, h * HEAD_DIM:(h + 1) * HEAD_DIM] for h in range(n_heads)], axis=0)


def _spread(xh, n_groups, rows_per_group):
    n = xh.shape[0]
    tiled = jnp.concatenate([xh] * n_groups, axis=1)
    rowi = lax.broadcasted_iota(jnp.int32, tiled.shape, 0)
    lane_grp = lax.broadcasted_iota(jnp.int32, tiled.shape, 1) // HEAD_DIM
    return jnp.where(lane_grp == rowi // rows_per_group, tiled, 0.0)


def _gather_groups(full, n_groups, rows_per_group):
    n = full.shape[0]
    rowi = lax.broadcasted_iota(jnp.int32, (n, HEAD_DIM), 0)
    out = jnp.zeros((n, HEAD_DIM), F32)
    for g in range(n_groups):
        out = out + jnp.where(rowi // rows_per_group == g, full[:, g * HEAD_DIM:(g + 1) * HEAD_DIM], 0.0)
    return out


def _decode_attn_kernel(p_ref, kbuf_ref, vbuf_ref, mk_ref, mv_ref, bkt_ref, relb_ref, sink_ref,
                        sqg_ref, skg_ref, mqg_ref, ys_ref, ym_ref, kout_ref, vout_ref):
    grp = N_SWA_HEADS // N_SWA_KV_HEADS
    hrow = lax.broadcasted_iota(jnp.int32, (8, WINDOW), 0)
    bias_w = jnp.zeros((8, WINDOW), F32)
    for h in range(N_SWA_HEADS):
        bias_w = jnp.where(hrow == h, _bias_from_buckets(bkt_ref[...], relb_ref, h, 0.0), bias_w)
    bias_w = bias_w[:N_SWA_HEADS]
    hcol = lax.broadcasted_iota(jnp.int32, (N_SWA_HEADS, 1), 0)
    bias_new = jnp.zeros((N_SWA_HEADS, 1), F32)
    sink = jnp.zeros((N_SWA_HEADS, 1), F32)
    for h in range(N_SWA_HEADS):
        bias_new = jnp.where(hcol == h, relb_ref[0, h], bias_new)
        sink = jnp.where(hcol == h, sink_ref[0, h], sink)
    rowi = lax.broadcasted_iota(jnp.int32, (WINDOW, SWA_KV_WIDTH), 0)

    B = range(SEQ_TILE)
    prow = [p_ref[b:b + 1, :] for b in B]
    q = [_rms(_head_rows(prow[b][:, COL_SQ:COL_SQ + SWA_WIDTH], N_SWA_HEADS), sqg_ref[...]) * ATTN_SCALE for b in B]
    kn = [_rms(_head_rows(prow[b][:, COL_SK:COL_SK + SWA_KV_WIDTH], N_SWA_KV_HEADS), skg_ref[...]) for b in B]
    kn_row = [jnp.concatenate([kn[b][0:1], kn[b][1:2]], axis=1) for b in B]
    vn_row = [prow[b][:, COL_SV:COL_SV + SWA_KV_WIDTH] for b in B]
    qm = [_spread(q[b], N_SWA_KV_HEADS, grp) for b in B]
    qmem = [_spread(_rms(_head_rows(prow[b][:, COL_MQ:COL_MQ + MEM_WIDTH], N_MEM_HEADS), mqg_ref[...]) * ATTN_SCALE,
                    N_MEM_HEADS, 1) for b in B]
    s = [_dot_nt(qm[b], kbuf_ref[b]) + bias_w for b in B]
    sm = [_dot_nt(qmem[b], mk_ref[b]) for b in B]
    s_new = [jnp.sum(qm[b] * kn_row[b], -1, keepdims=True) + bias_new for b in B]
    m = [jnp.maximum(jnp.maximum(jnp.max(s[b], -1, keepdims=True), s_new[b]), sink) for b in B]
    e = [jnp.exp(s[b] - m[b]) for b in B]
    e_new = [jnp.exp(s_new[b] - m[b]) for b in B]
    den = [jnp.sum(e[b], -1, keepdims=True) + e_new[b] + jnp.exp(sink - m[b]) for b in B]
    em = [jnp.exp(sm[b] - jnp.max(sm[b], -1, keepdims=True)) for b in B]
    ov = [_dot(e[b], vbuf_ref[b]) for b in B]
    omf = [_dot(em[b], mv_ref[b]) for b in B]
    for b in B:
        o = _gather_groups((ov[b] + e_new[b] * vn_row[b]) / den[b], N_SWA_KV_HEADS, grp)
        om = _gather_groups(omf[b], N_MEM_HEADS, 1) / jnp.sum(em[b], -1, keepdims=True)
        for h in range(N_SWA_HEADS):
            ys_ref[b:b + 1, h * HEAD_DIM:(h + 1) * HEAD_DIM] = o[h:h + 1]
        for h in range(N_MEM_HEADS):
            ym_ref[b:b + 1, h * HEAD_DIM:(h + 1) * HEAD_DIM] = om[h:h + 1]
        kout_ref[b] = jnp.where(rowi == WINDOW - 1, kn_row[b], pltpu.roll(kbuf_ref[b], WINDOW - 1, axis=0))
        vout_ref[b] = jnp.where(rowi == WINDOW - 1, vn_row[b], pltpu.roll(vbuf_ref[b], WINDOW - 1, axis=0))


def _decode_attn(proj, kbuf, vbuf, mk, mv, rel_bias, sinks, sqg, skg, mqg):
    n = proj.shape[0]
    bkt = jnp.asarray(_decode_bucket_table())
    smem = pl.BlockSpec(memory_space=pltpu.SMEM)
    win = pl.BlockSpec((SEQ_TILE, WINDOW, SWA_KV_WIDTH), lambda i: (i, 0, 0))
    memb = pl.BlockSpec((SEQ_TILE, N_MEM, MEM_WIDTH), lambda i: (i, 0, 0))
    return pl.pallas_call(
        _decode_attn_kernel,
        grid=(n // SEQ_TILE,),
        in_specs=[pl.BlockSpec((SEQ_TILE, IN_PROJ), lambda i: (i, 0)), win, win, memb, memb,
                  _full((8, WINDOW)), smem, smem, _full((1, HEAD_DIM)), _full((1, HEAD_DIM)), _full((1, HEAD_DIM))],
        out_specs=[pl.BlockSpec((SEQ_TILE, SWA_WIDTH), lambda i: (i, 0)),
                   pl.BlockSpec((SEQ_TILE, MEM_WIDTH), lambda i: (i, 0)), win, win],
        out_shape=[jax.ShapeDtypeStruct((n, SWA_WIDTH), F32), jax.ShapeDtypeStruct((n, MEM_WIDTH), F32),
                   jax.ShapeDtypeStruct(kbuf.shape, F32), jax.ShapeDtypeStruct(vbuf.shape, F32)],
        compiler_params=_params("arbitrary"),
        name="decode_attn",
    )(proj, kbuf, vbuf, mk, mv, bkt, rel_bias, sinks, sqg, skg, mqg)


FF_CHUNK = 1024


def _out_ffn_kernel(x_ref, yr_ref, ys_ref, ym_ref, wo_ref, g2_ref, w1_ref, w2_ref, o_ref):
    x1 = (x_ref[...]
          + jnp.dot(yr_ref[...].astype(BF16), wo_ref[0:RWKV_WIDTH, :], preferred_element_type=F32)
          + jnp.dot(ys_ref[...].astype(BF16), wo_ref[RWKV_WIDTH:RWKV_WIDTH + SWA_WIDTH, :],
                    preferred_element_type=F32)
          + jnp.dot(ym_ref[...].astype(BF16), wo_ref[RWKV_WIDTH + SWA_WIDTH:, :], preferred_element_type=F32))
    h2 = _rms(x1, g2_ref[...]).astype(BF16)
    ff = None
    for c in range(D_FF // FF_CHUNK):
        u = jnp.dot(h2, w1_ref[:, c * FF_CHUNK:(c + 1) * FF_CHUNK], preferred_element_type=F32)
        u = jnp.square(jnp.maximum(u, 0.0)).astype(BF16)
        d = jnp.dot(u, w2_ref[c * FF_CHUNK:(c + 1) * FF_CHUNK, :], preferred_element_type=F32)
        ff = d if ff is None else ff + d
    o_ref[...] = x1 + ff


def _out_ffn(x2d, yr, ys, ym, wo, g2, w1, w2, tm):
    n = x2d.shape[0]
    rows = lambda w: pl.BlockSpec((tm, w), lambda i: (i, 0))
    const = lambda shape: pl.BlockSpec(shape, lambda i: (0, 0), pipeline_mode=pl.Buffered(1))
    return pl.pallas_call(
        _out_ffn_kernel,
        grid=(n // tm,),
        in_specs=[rows(D_MODEL), rows(RWKV_WIDTH), rows(SWA_WIDTH), rows(MEM_WIDTH),
                  const((D_MODEL, D_MODEL)), _full((1, D_MODEL)), const((D_MODEL, D_FF)), const((D_FF, D_MODEL))],
        out_specs=rows(D_MODEL),
        out_shape=jax.ShapeDtypeStruct((n, D_MODEL), F32),
        compiler_params=_params("arbitrary"),
        name="out_ffn",
    )(x2d, yr, ys, ym, wo, g2, w1, w2)


def kernel(x_prompt, x_sample, state_rwkv, state_shift, cache_swa_k, cache_swa_v, cache_mem_k, cache_mem_v,
           mem_prompt, rel_bias, norm1_g, w_in, mu_shift, w0, w_up_w, a0, w_up_a, w_up_g, k_k, k_a, r_k,
           lnx_w, lnx_b, q_norm_swa, k_norm_swa, sinks, mem_norm_g, w_mem_kv, q_norm_mem, k_norm_mem,
           w_out, norm2_g, w_ff1, w_ff2):
    B, T, _ = x_prompt.shape
    Bd = x_sample.shape[0]
    l = 0
    w_in_b = w_in[l].astype(BF16)
    w_out_b = w_out[l].astype(BF16)
    w1_b = w_ff1[l].astype(BF16)
    w2_b = w_ff2[l].astype(BF16)
    rwkv_params = (mu_shift[l][None], w0[l][None], a0[l][None], k_k[l][None], k_a[l][None],
                   r_k[l].reshape(1, RWKV_WIDTH), lnx_w[l][None], lnx_b[l][None],
                   w_up_w[l].astype(BF16), w_up_a[l].astype(BF16), w_up_g[l].astype(BF16))
    sqg, skg, mqg, mkg = q_norm_swa[l][None], k_norm_swa[l][None], q_norm_mem[l][None], k_norm_mem[l][None]
    g1, g2 = norm1_g[l][None], norm2_g[l][None]

    xp = x_prompt.reshape(B * T, D_MODEL)
    proj_p = _in_proj(xp, g1, w_in_b, 512).reshape(B, T, IN_PROJ)
    mk, mv = _memory_kv(mem_prompt, mem_norm_g[l][None], w_mem_kv[l].astype(BF16), mkg)
    yr_p, s_p = _rwkv_prompt(proj_p, *rwkv_params)
    ys_p, ym_p, kn_p = _attn_prompt(proj_p, mk, mv, rel_bias, sinks[l][None], sqg, skg, mqg)
    y_p = _out_ffn(xp, yr_p.reshape(B * T, RWKV_WIDTH), ys_p.reshape(B * T, SWA_WIDTH),
                   ym_p.reshape(B * T, MEM_WIDTH), w_out_b, g2, w1_b, w2_b, 512).reshape(B, T, D_MODEL)
    shift_p = proj_p[:, T - 1, :RWKV_PROJ]
    vb_p = proj_p[:, T - WINDOW:, COL_SV:COL_SV + SWA_KV_WIDTH]

    xs = x_sample.reshape(Bd, D_MODEL)
    proj_s = _in_proj(xs, g1, w_in_b, Bd)
    yr_s, s_s = _rwkv_step(proj_s[:, :RWKV_PROJ], state_shift[l], state_rwkv[l], *rwkv_params)
    ys_s, ym_s, kb_s, vb_s = _decode_attn(
        proj_s, cache_swa_k[l].reshape(Bd, WINDOW, SWA_KV_WIDTH), cache_swa_v[l].reshape(Bd, WINDOW, SWA_KV_WIDTH),
        cache_mem_k[l].reshape(Bd, N_MEM, MEM_WIDTH), cache_mem_v[l].reshape(Bd, N_MEM, MEM_WIDTH),
        rel_bias, sinks[l][None], sqg, skg, mqg)
    y_s = _out_ffn(xs, yr_s, ys_s, ym_s, w_out_b, g2, w1_b, w2_b, Bd).reshape(Bd, 1, D_MODEL)

    return (y_p, y_s,
            s_p[None], shift_p[None],
            kn_p.reshape(1, B, WINDOW, N_SWA_KV_HEADS, HEAD_DIM),
            vb_p.reshape(1, B, WINDOW, N_SWA_KV_HEADS, HEAD_DIM),
            mk.reshape(1, B, N_MEM, N_MEM_HEADS, HEAD_DIM), mv.reshape(1, B, N_MEM, N_MEM_HEADS, HEAD_DIM),
            s_s[None], proj_s[:, :RWKV_PROJ][None],
            kb_s.reshape(1, Bd, WINDOW, N_SWA_KV_HEADS, HEAD_DIM),
            vb_s.reshape(1, Bd, WINDOW, N_SWA_KV_HEADS, HEAD_DIM))
```

```python
import functools
import math

import numpy as np
import jax
import jax.numpy as jnp
from jax import lax
from jax.experimental import pallas as pl
from jax.experimental.pallas import tpu as pltpu

F32 = jnp.float32
BF16 = jnp.bfloat16

D_MODEL = 1024
HEAD_DIM = 64
RWKV_WIDTH = 512
N_RWKV_HEADS = 8
SWA_WIDTH = 256
N_SWA_HEADS = 4
N_SWA_KV_HEADS = 2
SWA_KV_WIDTH = 128
MEM_WIDTH = 256
N_MEM_HEADS = 4
N_MEM = 256
WINDOW = 128
BLOCK = 128
N_BUCKETS = 32
MAX_DISTANCE = 128
DECAY_LORA = 64
AAA_LORA = 64
GATE_LORA = 128
RWKV_PROJ = 3 * RWKV_WIDTH + DECAY_LORA + AAA_LORA + GATE_LORA
SWA_PROJ = SWA_WIDTH + 2 * SWA_KV_WIDTH
IN_PROJ = RWKV_PROJ + SWA_PROJ + MEM_WIDTH
D_FF = 4 * D_MODEL
NORM_EPS = 1e-6
LNX_EPS = 64e-5
ATTN_SCALE = HEAD_DIM ** -0.5
EXP_M05 = math.exp(-0.5)
NEG = -1e30

COL_R, COL_K, COL_V = 0, RWKV_WIDTH, 2 * RWKV_WIDTH
COL_WD = 3 * RWKV_WIDTH
COL_AD = COL_WD + DECAY_LORA
COL_GD = COL_AD + AAA_LORA
COL_SQ = RWKV_PROJ
COL_SK = COL_SQ + SWA_WIDTH
COL_SV = COL_SK + SWA_KV_WIDTH
COL_MQ = RWKV_PROJ + SWA_PROJ

CHUNK = 64
VMEM_LIMIT = 56 * 1024 * 1024


def _dot(a, b):
    return jnp.dot(a.astype(BF16), b.astype(BF16), preferred_element_type=F32)


def _dot_nt(a, b):
    return lax.dot_general(a.astype(BF16), b.astype(BF16), (((1,), (1,)), ((), ())),
                           preferred_element_type=F32)


def _dot_tn(a, b):
    return lax.dot_general(a.astype(BF16), b.astype(BF16), (((0,), (0,)), ((), ())),
                           preferred_element_type=F32)


def _rms(x, g):
    return x * lax.rsqrt(jnp.mean(x * x, -1, keepdims=True) + NORM_EPS) * g


def _params(*sem):
    return pltpu.CompilerParams(dimension_semantics=sem, vmem_limit_bytes=VMEM_LIMIT)


def _full(shape):
    n = len(shape)
    return pl.BlockSpec(shape, lambda *_: (0,) * n)


def _in_proj_kernel(x_ref, g_ref, w_ref, o_ref):
    h = _rms(x_ref[...], g_ref[...])
    o_ref[...] = jnp.dot(h.astype(BF16), w_ref[...], preferred_element_type=F32)


def _in_proj(x2d, g, w_bf16, tm):
    n = x2d.shape[0]
    return pl.pallas_call(
        _in_proj_kernel,
        grid=(n // tm,),
        in_specs=[pl.BlockSpec((tm, D_MODEL), lambda i: (i, 0)),
                  _full((1, D_MODEL)),
                  _full((D_MODEL, IN_PROJ))],
        out_specs=pl.BlockSpec((tm, IN_PROJ), lambda i: (i, 0)),
        out_shape=jax.ShapeDtypeStruct((n, IN_PROJ), F32),
        compiler_params=_params("arbitrary"),
        name="in_proj",
    )(x2d, g, w_bf16)


def _rwkv_features(xs, w0, a0, k_k, k_a, wupw, wupa, wupg):
    r = xs[:, COL_R:COL_R + RWKV_WIDTH]
    k = xs[:, COL_K:COL_K + RWKV_WIDTH]
    v = xs[:, COL_V:COL_V + RWKV_WIDTH]
    wd = xs[:, COL_WD:COL_WD + DECAY_LORA]
    ad = xs[:, COL_AD:COL_AD + AAA_LORA]
    gd = xs[:, COL_GD:COL_GD + GATE_LORA]
    logw = -jax.nn.sigmoid(w0 + _dot(jnp.tanh(wd), wupw)) * EXP_M05
    a_sig = jax.nn.sigmoid(a0 + _dot(ad, wupa))
    gate = _dot(jax.nn.sigmoid(gd), wupg)
    kk = k * k_k
    k2 = k * (1.0 + (a_sig - 1.0) * k_a)
    return r, k2, v, kk, a_sig, logw, gate


def _head_out(y, r, k2, v, gate, r_k, lnx_w, lnx_b):
    m = jnp.mean(y, -1, keepdims=True)
    d = y - m
    var = jnp.mean(d * d, -1, keepdims=True)
    yn = d * lax.rsqrt(var + LNX_EPS) * lnx_w + lnx_b
    bonus = jnp.sum(r * k2 * r_k, -1, keepdims=True) * v
    return (yn + bonus) * gate


RWKV_TILE = 2 * CHUNK
PAIR = 2 * HEAD_DIM
N_PAIRS = N_RWKV_HEADS // 2


def _seg_sum(x, blk):
    xb = x.astype(BF16)
    half = RWKV_WIDTH // 2
    return jnp.concatenate([jnp.dot(xb[:, :half], blk, preferred_element_type=F32),
                            jnp.dot(xb[:, half:], blk, preferred_element_type=F32)], axis=1)


def _rwkv_prompt_kernel(p_ref, mu_ref, w0_ref, a0_ref, kk_ref, ka_ref, rk_ref, lnw_ref, lnb_ref,
                        wupw_ref, wupa_ref, wupg_ref, y_ref, sout_ref, s_scr, prev_scr):
    C, TT, D = CHUNK, RWKV_TILE, HEAD_DIM
    NC = TT // C
    t = pl.program_id(1)

    @pl.when(t == 0)
    def _():
        s_scr[...] = jnp.zeros_like(s_scr)
        prev_scr[...] = jnp.zeros_like(prev_scr)

    p = p_ref[0]
    row = lax.broadcasted_iota(jnp.int32, p.shape, 0)
    prev = jnp.where(row == 0, prev_scr[...], pltpu.roll(p, 1, axis=0))
    prev_scr[...] = p[TT - 1:TT, :]
    xs = p + (prev - p) * mu_ref[...]
    r, k2, v, kk, a_sig, logw, gate = _rwkv_features(
        xs, w0_ref[...], a0_ref[...], kk_ref[...], ka_ref[...], wupw_ref[...], wupa_ref[...], wupg_ref[...])

    half = RWKV_WIDTH // 2
    bi = lax.broadcasted_iota(jnp.int32, (half, half), 0) // D
    bj = lax.broadcasted_iota(jnp.int32, (half, half), 1) // D
    blk = jnp.where(bi == bj, 1.0, 0.0).astype(BF16)
    kkn = kk / jnp.maximum(jnp.sqrt(_seg_sum(kk * kk, blk)), 1e-12)
    bb = kkn * a_sig

    ri = lax.broadcasted_iota(jnp.int32, (TT, TT), 0)
    ci = lax.broadcasted_iota(jnp.int32, (TT, TT), 1)
    tri = jnp.where(jnp.logical_and(ri >= ci, ri // C == ci // C), 1.0, 0.0).astype(BF16)
    l1 = logw.astype(BF16)
    rem = logw - l1.astype(F32)
    l2 = rem.astype(BF16)
    l3 = (rem - l2.astype(F32)).astype(BF16)
    cum = (jnp.dot(tri, l1, preferred_element_type=F32) + jnp.dot(tri, l2, preferred_element_type=F32)
           + jnp.dot(tri, l3, preferred_element_type=F32))
    c_last = jnp.concatenate([jnp.broadcast_to(cum[(c + 1) * C - 1:(c + 1) * C, :], (C, RWKV_WIDTH))
                              for c in range(NC)], axis=0)
    e_pos = jnp.exp(cum)
    e_neg = jnp.exp(-cum)
    e_prev = jnp.exp(cum - logw)
    e_last = jnp.exp(c_last - cum)

    lo_full = (lax.broadcasted_iota(jnp.int32, (TT, RWKV_WIDTH), 1) % PAIR) < D
    at_f = -kkn * e_prev
    rt_f = r * e_pos
    at_lo = jnp.where(lo_full, at_f, 0.0).astype(BF16)
    at_hi = jnp.where(lo_full, 0.0, at_f).astype(BF16)
    rt_lo = jnp.where(lo_full, rt_f, 0.0).astype(BF16)
    rt_hi = jnp.where(lo_full, 0.0, rt_f).astype(BF16)
    bt_b = (bb * e_neg).astype(BF16)
    kt_b = (k2 * e_neg).astype(BF16)
    bh_b = (bb * e_last).astype(BF16)
    kh_b = (k2 * e_last).astype(BF16)
    v_b = v.astype(BF16)

    r2 = lax.broadcasted_iota(jnp.int32, (C, PAIR), 0)
    c2 = lax.broadcasted_iota(jnp.int32, (C, PAIR), 1)
    c2m = jnp.where(c2 >= C, c2 - C, c2)
    mask_ak = jnp.logical_and(c2 >= C, c2m < r2)
    mask_r = c2m <= r2
    strict = lax.broadcasted_iota(jnp.int32, (C, C), 1) < lax.broadcasted_iota(jnp.int32, (C, C), 0)
    lo = c2 < D
    qi = lax.broadcasted_iota(jnp.int32, (PAIR, PAIR), 0) // D
    qj = lax.broadcasted_iota(jnp.int32, (PAIR, PAIR), 1) // D
    diag = qi == qj

    PR = [(c, q) for c in range(NC) for q in range(N_PAIRS)]
    n = len(PR)
    E = range(2)
    win = lambda x, c, q: x[c * C:(c + 1) * C, q * PAIR:(q + 1) * PAIR]
    sc = [_dot_nt(jnp.concatenate([win(at_lo, c, q), win(at_hi, c, q), win(rt_lo, c, q), win(rt_hi, c, q)], axis=0),
                  jnp.concatenate([win(bt_b, c, q), win(kt_b, c, q)], axis=0)) for c, q in PR]
    vr = [pltpu.roll(win(v, c, q), D, axis=1) for c, q in PR]
    vvr = [jnp.concatenate([vr[i], vr[i]], axis=0).astype(BF16) for i in range(n)]
    A = [[jnp.where(strict, sc[i][e * C:(e + 1) * C, :C], 0.0) for e in E] for i in range(n)]
    m_ak = [[jnp.where(mask_ak, sc[i][e * C:(e + 1) * C], 0.0) for e in E] for i in range(n)]
    m_r = [[jnp.where(mask_r, sc[i][(2 + e) * C:(3 + e) * C], 0.0) for e in E] for i in range(n)]

    zf = [[_dot(m_ak[i][e], vvr[i]) for e in E] for i in range(n)]
    X = [[jnp.where(lo, win(at_f, c, q), zf[i][0]), jnp.where(lo, zf[i][1], win(at_f, c, q))]
         for i, (c, q) in enumerate(PR)]
    for k in range(int(math.log2(C))):
        AW = [[_dot(A[i][e], jnp.concatenate([X[i][e], A[i][e]], axis=1)) for e in E] for i in range(n)]
        X = [[X[i][e] + AW[i][e][:, :PAIR] for e in E] for i in range(n)]
        A = [[AW[i][e][:, PAIR:] for e in E] for i in range(n)]

    S = [s_scr[q] for q in range(N_PAIRS)]
    ys = []
    for c in range(NC):
        w_last = jnp.exp(cum[(c + 1) * C - 1:(c + 1) * C, :])
        idx = [c * N_PAIRS + q for q in range(N_PAIRS)]
        mkg = [[_dot_tn(X[i][e], win(bh_b, c, q)) for e in E] for q, i in enumerate(idx)]
        vk = [_dot_tn(win(v_b, c, q), win(kh_b, c, q)) for q, i in enumerate(idx)]
        ry = [[_dot(m_r[i][0], jnp.concatenate([X[i][0], jnp.where(lo, 0.0, vr[i])], axis=0)),
               _dot(m_r[i][1], jnp.concatenate([X[i][1], jnp.where(lo, vr[i], 0.0)], axis=0))] for i in idx]
        rp = [win(rt_f, c, q) + jnp.where(lo, ry[q][0], ry[q][1]) for q in range(N_PAIRS)]
        y0 = [pltpu.roll(jnp.where(lo, ry[q][1], ry[q][0]), D, axis=1) for q in range(N_PAIRS)]
        mk = [jnp.where(diag, jnp.concatenate([mkg[q][0][:D], mkg[q][1][D:]], axis=0), 0.0) for q in range(N_PAIRS)]
        g = [jnp.where(diag, vk[q] + jnp.concatenate([mkg[q][0][D:], mkg[q][1][:D]], axis=0), 0.0)
             for q in range(N_PAIRS)]
        y = [_dot_nt(rp[q], S[q]) for q in range(N_PAIRS)]
        dS = [_dot(S[q], mk[q]) for q in range(N_PAIRS)]
        S = [S[q] * w_last[:, q * PAIR:(q + 1) * PAIR] + dS[q] + g[q] for q in range(N_PAIRS)]
        ys.append(jnp.concatenate([y[q] + y0[q] for q in range(N_PAIRS)], axis=1))
    for q in range(N_PAIRS):
        s_scr[q] = S[q]

    yf = jnp.concatenate(ys, axis=0)
    inv_d = 1.0 / D
    m = _seg_sum(yf, blk) * inv_d
    d = yf - m
    var = _seg_sum(d * d, blk) * inv_d
    yn = d * lax.rsqrt(var + LNX_EPS) * lnw_ref[...] + lnb_ref[...]
    bonus = _seg_sum(r * k2 * rk_ref[...], blk) * v
    y_ref[0] = (yn + bonus) * gate

    @pl.when(t == pl.num_programs(1) - 1)
    def _():
        for q in range(N_PAIRS):
            sout_ref[0, 2 * q] = S[q][:D, :D]
            sout_ref[0, 2 * q + 1] = S[q][D:, D:]


def _rwkv_prompt(proj, mu, w0, a0, k_k, k_a, r_k, lnx_w, lnx_b, wupw, wupa, wupg):
    B, T, _ = proj.shape
    vec = lambda n: _full((1, n))
    return pl.pallas_call(
        _rwkv_prompt_kernel,
        grid=(B, T // RWKV_TILE),
        in_specs=[pl.BlockSpec((1, RWKV_TILE, RWKV_PROJ), lambda b, t: (b, t, 0)),
                  vec(RWKV_PROJ), vec(RWKV_WIDTH), vec(RWKV_WIDTH), vec(RWKV_WIDTH), vec(RWKV_WIDTH),
                  vec(RWKV_WIDTH), vec(RWKV_WIDTH), vec(RWKV_WIDTH),
                  _full((DECAY_LORA, RWKV_WIDTH)), _full((AAA_LORA, RWKV_WIDTH)), _full((GATE_LORA, RWKV_WIDTH))],
        out_specs=[pl.BlockSpec((1, RWKV_TILE, RWKV_WIDTH), lambda b, t: (b, t, 0)),
                   pl.BlockSpec((1, N_RWKV_HEADS, HEAD_DIM, HEAD_DIM), lambda b, t: (b, 0, 0, 0))],
        out_shape=[jax.ShapeDtypeStruct((B, T, RWKV_WIDTH), F32),
                   jax.ShapeDtypeStruct((B, N_RWKV_HEADS, HEAD_DIM, HEAD_DIM), F32)],
        scratch_shapes=[pltpu.VMEM((N_PAIRS, PAIR, PAIR), F32),
                        pltpu.VMEM((1, RWKV_PROJ), F32)],
        compiler_params=_params("arbitrary", "arbitrary"),
        name="rwkv_prompt",
    )(proj, mu, w0, a0, k_k, k_a, r_k, lnx_w, lnx_b, wupw, wupa, wupg)


SEQ_TILE = 8


def _head_blocks(width):
    bi = lax.broadcasted_iota(jnp.int32, (width, width), 0) // HEAD_DIM
    bj = lax.broadcasted_iota(jnp.int32, (width, width), 1) // HEAD_DIM
    return jnp.where(bi == bj, 1.0, 0.0).astype(BF16)


def _rwkv_step_feat_kernel(p_ref, sh_ref, mu_ref, w0_ref, a0_ref, kk_ref, ka_ref, rk_ref,
                           wupw_ref, wupa_ref, wupg_ref, vecs_ref, bonus_ref, gate_ref):
    p = p_ref[...]
    xs = p + (sh_ref[...] - p) * mu_ref[...]
    r, k2, v, kk, a_sig, logw, gate = _rwkv_features(
        xs, w0_ref[...], a0_ref[...], kk_ref[...], ka_ref[...], wupw_ref[...], wupa_ref[...], wupg_ref[...])
    blk = _head_blocks(RWKV_WIDTH // 2)
    kkn = kk / jnp.maximum(jnp.sqrt(_seg_sum(kk * kk, blk)), 1e-12)
    for i, x in enumerate((-kkn, kkn * a_sig, jnp.exp(logw), k2, r, v)):
        vecs_ref[i] = x.T
    bonus_ref[...] = _seg_sum(r * k2 * rk_ref[...], blk) * v
    gate_ref[...] = gate


def _rwkv_step_state_kernel(s_ref, vecs_ref, sout_ref, y_ref):
    S = s_ref[0]
    a, b, w, k, r, v = (vecs_ref[i] for i in range(6))
    sa = jnp.sum(S * a[None], axis=1)
    S = S * w[None] + sa[:, None, :] * b[None] + v[:, None, :] * k[None]
    sout_ref[0] = S
    y_ref[...] = jnp.sum(S * r[None], axis=1)


def _rwkv_step_out_kernel(yt_ref, bonus_ref, gate_ref, lnw_ref, lnb_ref, o_ref):
    y = yt_ref[...].T
    blk = _head_blocks(RWKV_WIDTH // 2)
    inv_d = 1.0 / HEAD_DIM
    m = _seg_sum(y, blk) * inv_d
    d = y - m
    var = _seg_sum(d * d, blk) * inv_d
    yn = d * lax.rsqrt(var + LNX_EPS) * lnw_ref[...] + lnb_ref[...]
    o_ref[...] = (yn + bonus_ref[...]) * gate_ref[...]


def _rwkv_step(proj, shift, state_t, mu, w0, a0, k_k, k_a, r_k, lnx_w, lnx_b, wupw, wupa, wupg):
    n = proj.shape[0]
    vec = lambda m: _full((1, m))
    rows = _full((n, RWKV_PROJ))
    wide = _full((n, RWKV_WIDTH))
    vecs, bonus, gate = pl.pallas_call(
        _rwkv_step_feat_kernel,
        grid=(1,),
        in_specs=[rows, rows, vec(RWKV_PROJ), vec(RWKV_WIDTH), vec(RWKV_WIDTH), vec(RWKV_WIDTH), vec(RWKV_WIDTH),
                  vec(RWKV_WIDTH),
                  _full((DECAY_LORA, RWKV_WIDTH)), _full((AAA_LORA, RWKV_WIDTH)), _full((GATE_LORA, RWKV_WIDTH))],
        out_specs=[_full((6, RWKV_WIDTH, n)), wide, wide],
        out_shape=[jax.ShapeDtypeStruct((6, RWKV_WIDTH, n), F32),
                   jax.ShapeDtypeStruct((n, RWKV_WIDTH), F32), jax.ShapeDtypeStruct((n, RWKV_WIDTH), F32)],
        compiler_params=_params("arbitrary"),
        name="rwkv_step_feat",
    )(proj, shift, mu, w0, a0, k_k, k_a, r_k, wupw, wupa, wupg)
    st_spec = pl.BlockSpec((1, HEAD_DIM, HEAD_DIM, n), lambda h: (h, 0, 0, 0))
    state_new, yt = pl.pallas_call(
        _rwkv_step_state_kernel,
        grid=(N_RWKV_HEADS,),
        in_specs=[st_spec, pl.BlockSpec((6, HEAD_DIM, n), lambda h: (0, h, 0))],
        out_specs=[st_spec, pl.BlockSpec((HEAD_DIM, n), lambda h: (h, 0))],
        out_shape=[jax.ShapeDtypeStruct(state_t.shape, F32), jax.ShapeDtypeStruct((RWKV_WIDTH, n), F32)],
        compiler_params=_params("arbitrary"),
        name="rwkv_step_state",
    )(state_t, vecs)
    y = pl.pallas_call(
        _rwkv_step_out_kernel,
        grid=(1,),
        in_specs=[_full((RWKV_WIDTH, n)), wide, wide, vec(RWKV_WIDTH), vec(RWKV_WIDTH)],
        out_specs=wide,
        out_shape=jax.ShapeDtypeStruct((n, RWKV_WIDTH), F32),
        compiler_params=_params("arbitrary"),
        name="rwkv_step_out",
    )(yt, bonus, gate, lnx_w, lnx_b)
    return y, state_new


def _t5_bucket_np(dist):
    max_exact = N_BUCKETS // 2
    d = np.maximum(dist, 1).astype(np.float32)
    large = max_exact + (np.log(d / np.float32(max_exact)) / np.float32(math.log(MAX_DISTANCE / max_exact))
                         * np.float32(N_BUCKETS - max_exact)).astype(np.int32)
    large = np.minimum(large, N_BUCKETS - 1)
    return np.where(dist < max_exact, dist, large).astype(np.int32)


def _prompt_bucket_table():
    qi = np.arange(BLOCK)[:, None]
    kj = np.arange(2 * BLOCK)[None, :]
    dist = BLOCK + qi - kj
    valid = (dist >= 0) & (dist <= WINDOW)
    return np.where(valid, _t5_bucket_np(np.maximum(dist, 0)), -1).astype(np.int32)


def _decode_bucket_table():
    dist = WINDOW - np.arange(WINDOW)
    return np.broadcast_to(_t5_bucket_np(dist)[None, :], (8, WINDOW)).astype(np.int32).copy()


def _bias_from_buckets(bkt, relb_ref, h, init):
    acc = jnp.full(bkt.shape, init, F32)
    for b in range(N_BUCKETS):
        acc = jnp.where(bkt == b, relb_ref[b, h], acc)
    return acc


ATT_TILE = 2 * BLOCK


def _attn_prompt_kernel(qs_ref, qm_ref, kc_ref, kp_ref, vc_ref, vp_ref, mk_ref, mv_ref, bkt_ref, relb_ref, sink_ref,
                        sqg_ref, skg_ref, mqg_ref, ys_ref, ym_ref, kn_ref, bias_scr, mk_scr, mv_scr):
    i = pl.program_id(1)
    grp = N_SWA_HEADS // N_SWA_KV_HEADS
    ones = jnp.ones((ATT_TILE, HEAD_DIM), F32)

    @pl.when(i == 0)
    def _():
        bkt = bkt_ref[...]
        for j in range(N_SWA_KV_HEADS):
            for g in range(grp):
                bias_scr[j, g * BLOCK:(g + 1) * BLOCK, :] = _bias_from_buckets(bkt, relb_ref, j * grp + g, NEG)
        mk = mk_ref[0]
        mv = mv_ref[0]
        for h in range(N_MEM_HEADS):
            sl = slice(h * HEAD_DIM, (h + 1) * HEAD_DIM)
            mk_scr[h] = mk[:, sl].astype(BF16)
            mv_scr[h] = jnp.concatenate([mv[:, sl], ones], axis=1).astype(BF16)

    qs = qs_ref[0]
    qm = qm_ref[0]
    kc = kc_ref[0]
    kp = kp_ref[0]
    vc = vc_ref[0]
    vp = vp_ref[0]
    sqg, skg, mqg = sqg_ref[...], skg_ref[...], mqg_ref[...]
    rowi = lax.broadcasted_iota(jnp.int32, (ATT_TILE, 1), 0)
    col = lax.broadcasted_iota(jnp.int32, (ATT_TILE, 2 * BLOCK), 1)
    pad_mask = jnp.logical_and(i == 0, col < BLOCK)

    hsl = [slice(h * HEAD_DIM, (h + 1) * HEAD_DIM) for h in range(N_SWA_HEADS)]
    qn = [_rms(qs[:, hsl[h]], sqg) * ATTN_SCALE for h in range(N_SWA_HEADS)]
    kn_c = [_rms(kc[:, hsl[j]], skg) for j in range(N_SWA_KV_HEADS)]
    kn_p = [_rms(kp[:, hsl[j]], skg) for j in range(N_SWA_KV_HEADS)]
    for j in range(N_SWA_KV_HEADS):
        kn_ref[0, :, hsl[j]] = kn_c[j][BLOCK:]
    chains = [(a, j) for a in range(2) for j in range(N_SWA_KV_HEADS)]
    lhs, keys, vals, sinkcol = [], [], [], []
    for a, j in chains:
        rs = slice(a * BLOCK, (a + 1) * BLOCK)
        lhs.append(jnp.concatenate([qn[j * grp + g][rs] for g in range(grp)], axis=0))
        if a == 0:
            keys.append(jnp.concatenate([kn_p[j], kn_c[j][:BLOCK]], axis=0))
            vv = jnp.concatenate([vp[:, hsl[j]], vc[:BLOCK, hsl[j]]], axis=0)
        else:
            keys.append(kn_c[j])
            vv = vc[:, hsl[j]]
        vals.append(jnp.concatenate([vv, ones], axis=1))
        sinkcol.append(jnp.where(rowi < BLOCK, sink_ref[0, j * grp], sink_ref[0, j * grp + 1]))
    qmn = [_rms(qm[:, hsl[h]], mqg) * ATTN_SCALE for h in range(N_MEM_HEADS)]

    s_w = [_dot_nt(lhs[c], keys[c]) for c in range(len(chains))]
    s_m = [_dot_nt(qmn[h], mk_scr[h]) for h in range(N_MEM_HEADS)]
    e_w, m_w = [], []
    for c, (a, j) in enumerate(chains):
        s = s_w[c] + bias_scr[j]
        if a == 0:
            s = jnp.where(pad_mask, NEG, s)
        m = jnp.maximum(jnp.max(s, -1, keepdims=True), sinkcol[c])
        m_w.append(m)
        e_w.append(jnp.exp(s - m))
    e_m = [jnp.exp(s_m[h] - jnp.max(s_m[h], -1, keepdims=True)) for h in range(N_MEM_HEADS)]
    o_w = [_dot(e_w[c], vals[c]) for c in range(len(chains))]
    o_m = [_dot(e_m[h], mv_scr[h]) for h in range(N_MEM_HEADS)]
    for c, (a, j) in enumerate(chains):
        den = o_w[c][:, HEAD_DIM:HEAD_DIM + 1] + jnp.exp(sinkcol[c] - m_w[c])
        o = o_w[c][:, :HEAD_DIM] / den
        for g in range(grp):
            ys_ref[0, a * BLOCK:(a + 1) * BLOCK, hsl[j * grp + g]] = o[g * BLOCK:(g + 1) * BLOCK]
    for h in range(N_MEM_HEADS):
        ym_ref[0, :, hsl[h]] = o_m[h][:, :HEAD_DIM] / o_m[h][:, HEAD_DIM:HEAD_DIM + 1]


def _attn_prompt(proj, mk, mv, rel_bias, sinks, sqg, skg, mqg):
    B, T, _ = proj.shape
    bkt = jnp.asarray(_prompt_bucket_table())
    smem = pl.BlockSpec(memory_space=pltpu.SMEM)
    kblk, vblk = COL_SK // SWA_KV_WIDTH, COL_SV // SWA_KV_WIDTH
    prev = lambda b, i: (b, jnp.maximum(2 * i - 1, 0))
    memb = pl.BlockSpec((1, N_MEM, MEM_WIDTH), lambda b, i: (b, 0, 0))
    return pl.pallas_call(
        _attn_prompt_kernel,
        grid=(B, T // ATT_TILE),
        in_specs=[pl.BlockSpec((1, ATT_TILE, SWA_WIDTH), lambda b, i: (b, i, COL_SQ // SWA_WIDTH)),
                  pl.BlockSpec((1, ATT_TILE, MEM_WIDTH), lambda b, i: (b, i, COL_MQ // MEM_WIDTH)),
                  pl.BlockSpec((1, ATT_TILE, SWA_KV_WIDTH), lambda b, i: (b, i, kblk)),
                  pl.BlockSpec((1, BLOCK, SWA_KV_WIDTH), lambda b, i: prev(b, i) + (kblk,)),
                  pl.BlockSpec((1, ATT_TILE, SWA_KV_WIDTH), lambda b, i: (b, i, vblk)),
                  pl.BlockSpec((1, BLOCK, SWA_KV_WIDTH), lambda b, i: prev(b, i) + (vblk,)),
                  memb, memb,
                  _full((BLOCK, 2 * BLOCK)), smem, smem,
                  _full((1, HEAD_DIM)), _full((1, HEAD_DIM)), _full((1, HEAD_DIM))],
        out_specs=[pl.BlockSpec((1, ATT_TILE, SWA_WIDTH), lambda b, i: (b, i, 0)),
                   pl.BlockSpec((1, ATT_TILE, MEM_WIDTH), lambda b, i: (b, i, 0)),
                   pl.BlockSpec((1, BLOCK, SWA_KV_WIDTH), lambda b, i: (b, 0, 0))],
        out_shape=[jax.ShapeDtypeStruct((B, T, SWA_WIDTH), F32),
                   jax.ShapeDtypeStruct((B, T, MEM_WIDTH), F32),
                   jax.ShapeDtypeStruct((B, BLOCK, SWA_KV_WIDTH), F32)],
        scratch_shapes=[pltpu.VMEM((N_SWA_KV_HEADS, 2 * BLOCK, 2 * BLOCK), F32),
                        pltpu.VMEM((N_MEM_HEADS, N_MEM, HEAD_DIM), BF16),
                        pltpu.VMEM((N_MEM_HEADS, N_MEM, 2 * HEAD_DIM), BF16)],
        compiler_params=_params("arbitrary", "arbitrary"),
        name="attn_prompt",
    )(proj, proj, proj, proj, proj, proj, mk, mv, bkt, rel_bias, sinks, sqg, skg, mqg)


def _memory_kv_kernel(mem_ref, g_ref, w_ref, kg_ref, mk_ref, mv_ref):
    kv = jnp.dot(_rms(mem_ref[0], g_ref[...]).astype(BF16), w_ref[...], preferred_element_type=F32)
    kg = kg_ref[...]
    for h in range(N_MEM_HEADS):
        sl = slice(h * HEAD_DIM, (h + 1) * HEAD_DIM)
        mk_ref[0, :, sl] = _rms(kv[:, sl], kg)
    mv_ref[0] = kv[:, MEM_WIDTH:]


def _memory_kv(mem, g, w_bf16, kg):
    B = mem.shape[0]
    blk = pl.BlockSpec((1, N_MEM, MEM_WIDTH), lambda b: (b, 0, 0))
    return pl.pallas_call(
        _memory_kv_kernel,
        grid=(B,),
        in_specs=[pl.BlockSpec((1, N_MEM, D_MODEL), lambda b: (b, 0, 0)),
                  _full((1, D_MODEL)), _full((D_MODEL, 2 * MEM_WIDTH)), _full((1, HEAD_DIM))],
        out_specs=[blk, blk],
        out_shape=[jax.ShapeDtypeStruct((B, N_MEM, MEM_WIDTH), F32)] * 2,
        compiler_params=_params("arbitrary"),
        name="memory_kv",
    )(mem, g, w_bf16, kg)


def _head_rows(x_row, n_heads):
    return jnp.concatenate([x_row[:, h * HEAD_DIM:(h + 1) * HEAD_DIM] for h in range(n_heads)], axis=0)


def _spread(xh, n_groups, rows_per_group):
    n = xh.shape[0]
    tiled = jnp.concatenate([xh] * n_groups, axis=1)
    rowi = lax.broadcasted_iota(jnp.int32, tiled.shape, 0)
    lane_grp = lax.broadcasted_iota(jnp.int32, tiled.shape, 1) // HEAD_DIM
    return jnp.where(lane_grp == rowi // rows_per_group, tiled, 0.0)


def _gather_groups(full, n_groups, rows_per_group):
    n = full.shape[0]
    rowi = lax.broadcasted_iota(jnp.int32, (n, HEAD_DIM), 0)
    out = jnp.zeros((n, HEAD_DIM), F32)
    for g in range(n_groups):
        out = out + jnp.where(rowi // rows_per_group == g, full[:, g * HEAD_DIM:(g + 1) * HEAD_DIM], 0.0)
    return out


def _decode_attn_kernel(p_ref, kbuf_ref, vbuf_ref, mk_ref, mv_ref, bkt_ref, relb_ref, sink_ref,
                        sqg_ref, skg_ref, mqg_ref, ys_ref, ym_ref, kout_ref, vout_ref):
    grp = N_SWA_HEADS // N_SWA_KV_HEADS
    hrow = lax.broadcasted_iota(jnp.int32, (8, WINDOW), 0)
    bias_w = jnp.zeros((8, WINDOW), F32)
    for h in range(N_SWA_HEADS):
        bias_w = jnp.where(hrow == h, _bias_from_buckets(bkt_ref[...], relb_ref, h, 0.0), bias_w)
    bias_w = bias_w[:N_SWA_HEADS]
    hcol = lax.broadcasted_iota(jnp.int32, (N_SWA_HEADS, 1), 0)
    bias_new = jnp.zeros((N_SWA_HEADS, 1), F32)
    sink = jnp.zeros((N_SWA_HEADS, 1), F32)
    for h in range(N_SWA_HEADS):
        bias_new = jnp.where(hcol == h, relb_ref[0, h], bias_new)
        sink = jnp.where(hcol == h, sink_ref[0, h], sink)
    rowi = lax.broadcasted_iota(jnp.int32, (SWA_KV_WIDTH, WINDOW), 0)
    lanei = lax.broadcasted_iota(jnp.int32, (SWA_KV_WIDTH, WINDOW), 1)
    eye = rowi == lanei

    B = range(SEQ_TILE)
    prow = [p_ref[b:b + 1, :] for b in B]
    q = [_rms(_head_rows(prow[b][:, COL_SQ:COL_SQ + SWA_WIDTH], N_SWA_HEADS), sqg_ref[...]) * ATTN_SCALE for b in B]
    kn = [_rms(_head_rows(prow[b][:, COL_SK:COL_SK + SWA_KV_WIDTH], N_SWA_KV_HEADS), skg_ref[...]) for b in B]
    kn_row = [jnp.concatenate([kn[b][0:1], kn[b][1:2]], axis=1) for b in B]
    vn_row = [prow[b][:, COL_SV:COL_SV + SWA_KV_WIDTH] for b in B]
    qm = [_spread(q[b], N_SWA_KV_HEADS, grp) for b in B]
    qmem = [_spread(_rms(_head_rows(prow[b][:, COL_MQ:COL_MQ + MEM_WIDTH], N_MEM_HEADS), mqg_ref[...]) * ATTN_SCALE,
                    N_MEM_HEADS, 1) for b in B]
    s = [_dot(qm[b], kbuf_ref[b]) + bias_w for b in B]
    sm = [_dot(qmem[b], mk_ref[b]) for b in B]
    s_new = [jnp.sum(qm[b] * kn_row[b], -1, keepdims=True) + bias_new for b in B]
    m = [jnp.maximum(jnp.maximum(jnp.max(s[b], -1, keepdims=True), s_new[b]), sink) for b in B]
    e = [jnp.exp(s[b] - m[b]) for b in B]
    e_new = [jnp.exp(s_new[b] - m[b]) for b in B]
    den = [jnp.sum(e[b], -1, keepdims=True) + e_new[b] + jnp.exp(sink - m[b]) for b in B]
    em = [jnp.exp(sm[b] - jnp.max(sm[b], -1, keepdims=True)) for b in B]
    ov = [_dot_nt(e[b], vbuf_ref[b]) for b in B]
    omf = [_dot_nt(em[b], mv_ref[b]) for b in B]
    kn_col = [jnp.sum(jnp.where(eye, kn_row[b], 0.0), -1, keepdims=True) for b in B]
    vn_col = [jnp.sum(jnp.where(eye, vn_row[b], 0.0), -1, keepdims=True) for b in B]
    for b in B:
        o = _gather_groups((ov[b] + e_new[b] * vn_row[b]) / den[b], N_SWA_KV_HEADS, grp)
        om = _gather_groups(omf[b], N_MEM_HEADS, 1) / jnp.sum(em[b], -1, keepdims=True)
        for h in range(N_SWA_HEADS):
            ys_ref[b:b + 1, h * HEAD_DIM:(h + 1) * HEAD_DIM] = o[h:h + 1]
        for h in range(N_MEM_HEADS):
            ym_ref[b:b + 1, h * HEAD_DIM:(h + 1) * HEAD_DIM] = om[h:h + 1]
        kout_ref[b] = jnp.where(lanei == WINDOW - 1, kn_col[b], pltpu.roll(kbuf_ref[b], WINDOW - 1, axis=1))
        vout_ref[b] = jnp.where(lanei == WINDOW - 1, vn_col[b], pltpu.roll(vbuf_ref[b], WINDOW - 1, axis=1))


def _decode_attn(proj, kbuf, vbuf, mk, mv, rel_bias, sinks, sqg, skg, mqg):
    n = proj.shape[0]
    bkt = jnp.asarray(_decode_bucket_table())
    smem = pl.BlockSpec(memory_space=pltpu.SMEM)
    win = pl.BlockSpec((SEQ_TILE, WINDOW, SWA_KV_WIDTH), lambda i: (i, 0, 0))
    memb = pl.BlockSpec((SEQ_TILE, N_MEM, MEM_WIDTH), lambda i: (i, 0, 0))
    return pl.pallas_call(
        _decode_attn_kernel,
        grid=(n // SEQ_TILE,),
        in_specs=[pl.BlockSpec((SEQ_TILE, IN_PROJ), lambda i: (i, 0)), win, win, memb, memb,
                  _full((8, WINDOW)), smem, smem, _full((1, HEAD_DIM)), _full((1, HEAD_DIM)), _full((1, HEAD_DIM))],
        out_specs=[pl.BlockSpec((SEQ_TILE, SWA_WIDTH), lambda i: (i, 0)),
                   pl.BlockSpec((SEQ_TILE, MEM_WIDTH), lambda i: (i, 0)), win, win],
        out_shape=[jax.ShapeDtypeStruct((n, SWA_WIDTH), F32), jax.ShapeDtypeStruct((n, MEM_WIDTH), F32),
                   jax.ShapeDtypeStruct(kbuf.shape, F32), jax.ShapeDtypeStruct(vbuf.shape, F32)],
        compiler_params=_params("arbitrary"),
        name="decode_attn",
    )(proj, kbuf, vbuf, mk, mv, bkt, rel_bias, sinks, sqg, skg, mqg)


FF_CHUNK = 1024


def _out_ffn_kernel(x_ref, yr_ref, ys_ref, ym_ref, wo_ref, g2_ref, w1_ref, w2_ref, o_ref):
    x1 = (x_ref[...]
          + jnp.dot(yr_ref[...].astype(BF16), wo_ref[0:RWKV_WIDTH, :], preferred_element_type=F32)
          + jnp.dot(ys_ref[...].astype(BF16), wo_ref[RWKV_WIDTH:RWKV_WIDTH + SWA_WIDTH, :],
                    preferred_element_type=F32)
          + jnp.dot(ym_ref[...].astype(BF16), wo_ref[RWKV_WIDTH + SWA_WIDTH:, :], preferred_element_type=F32))
    h2 = _rms(x1, g2_ref[...]).astype(BF16)
    ff = None
    for c in range(D_FF // FF_CHUNK):
        u = jnp.dot(h2, w1_ref[:, c * FF_CHUNK:(c + 1) * FF_CHUNK], preferred_element_type=F32)
        u = jnp.square(jnp.maximum(u, 0.0)).astype(BF16)
        d = jnp.dot(u, w2_ref[c * FF_CHUNK:(c + 1) * FF_CHUNK, :], preferred_element_type=F32)
        ff = d if ff is None else ff + d
    o_ref[...] = x1 + ff


def _out_ffn(x2d, yr, ys, ym, wo, g2, w1, w2, tm):
    n = x2d.shape[0]
    rows = lambda w: pl.BlockSpec((tm, w), lambda i: (i, 0))
    const = lambda shape: pl.BlockSpec(shape, lambda i: (0, 0), pipeline_mode=pl.Buffered(1))
    return pl.pallas_call(
        _out_ffn_kernel,
        grid=(n // tm,),
        in_specs=[rows(D_MODEL), rows(RWKV_WIDTH), rows(SWA_WIDTH), rows(MEM_WIDTH),
                  const((D_MODEL, D_MODEL)), _full((1, D_MODEL)), const((D_MODEL, D_FF)), const((D_FF, D_MODEL))],
        out_specs=rows(D_MODEL),
        out_shape=jax.ShapeDtypeStruct((n, D_MODEL), F32),
        compiler_params=_params("arbitrary"),
        name="out_ffn",
    )(x2d, yr, ys, ym, wo, g2, w1, w2)


def kernel(x_prompt, x_sample, state_rwkv, state_shift, cache_swa_k, cache_swa_v, cache_mem_k, cache_mem_v,
           mem_prompt, rel_bias, norm1_g, w_in, mu_shift, w0, w_up_w, a0, w_up_a, w_up_g, k_k, k_a, r_k,
           lnx_w, lnx_b, q_norm_swa, k_norm_swa, sinks, mem_norm_g, w_mem_kv, q_norm_mem, k_norm_mem,
           w_out, norm2_g, w_ff1, w_ff2):
    B, T, _ = x_prompt.shape
    Bd = x_sample.shape[0]
    l = 0
    w_in_b = w_in[l].astype(BF16)
    w_out_b = w_out[l].astype(BF16)
    w1_b = w_ff1[l].astype(BF16)
    w2_b = w_ff2[l].astype(BF16)
    rwkv_params = (mu_shift[l][None], w0[l][None], a0[l][None], k_k[l][None], k_a[l][None],
                   r_k[l].reshape(1, RWKV_WIDTH), lnx_w[l][None], lnx_b[l][None],
                   w_up_w[l].astype(BF16), w_up_a[l].astype(BF16), w_up_g[l].astype(BF16))
    sqg, skg, mqg, mkg = q_norm_swa[l][None], k_norm_swa[l][None], q_norm_mem[l][None], k_norm_mem[l][None]
    g1, g2 = norm1_g[l][None], norm2_g[l][None]

    xp = x_prompt.reshape(B * T, D_MODEL)
    proj_p = _in_proj(xp, g1, w_in_b, 512).reshape(B, T, IN_PROJ)
    mk, mv = _memory_kv(mem_prompt, mem_norm_g[l][None], w_mem_kv[l].astype(BF16), mkg)
    yr_p, s_p = _rwkv_prompt(proj_p, *rwkv_params)
    ys_p, ym_p, kn_p = _attn_prompt(proj_p, mk, mv, rel_bias, sinks[l][None], sqg, skg, mqg)
    y_p = _out_ffn(xp, yr_p.reshape(B * T, RWKV_WIDTH), ys_p.reshape(B * T, SWA_WIDTH),
                   ym_p.reshape(B * T, MEM_WIDTH), w_out_b, g2, w1_b, w2_b, 512).reshape(B, T, D_MODEL)
    shift_p = proj_p[:, T - 1, :RWKV_PROJ]
    vb_p = proj_p[:, T - WINDOW:, COL_SV:COL_SV + SWA_KV_WIDTH]

    xs = x_sample.reshape(Bd, D_MODEL)
    proj_s = _in_proj(xs, g1, w_in_b, Bd)
    yr_s, st_s = _rwkv_step(proj_s[:, :RWKV_PROJ], state_shift[l], jnp.transpose(state_rwkv[l], (1, 2, 3, 0)),
                            *rwkv_params)
    s_s = jnp.transpose(st_s, (3, 0, 1, 2))
    fmajor = lambda c: jnp.transpose(c, (0, 2, 3, 1)).reshape(Bd, c.shape[2] * HEAD_DIM, c.shape[1])
    ys_s, ym_s, kb_s, vb_s = _decode_attn(
        proj_s, fmajor(cache_swa_k[l]), fmajor(cache_swa_v[l]), fmajor(cache_mem_k[l]), fmajor(cache_mem_v[l]),
        rel_bias, sinks[l][None], sqg, skg, mqg)
    pmajor = lambda c: jnp.transpose(c.reshape(Bd, N_SWA_KV_HEADS, HEAD_DIM, WINDOW), (0, 3, 1, 2))[None]
    y_s = _out_ffn(xs, yr_s, ys_s, ym_s, w_out_b, g2, w1_b, w2_b, Bd).reshape(Bd, 1, D_MODEL)

    return (y_p, y_s,
            s_p[None], shift_p[None],
            kn_p.reshape(1, B, WINDOW, N_SWA_KV_HEADS, HEAD_DIM),
            vb_p.reshape(1, B, WINDOW, N_SWA_KV_HEADS, HEAD_DIM),
            mk.reshape(1, B, N_MEM, N_MEM_HEADS, HEAD_DIM), mv.reshape(1, B, N_MEM, N_MEM_HEADS, HEAD_DIM),
            s_s[None], proj_s[:, :RWKV_PROJ][None],
            pmajor(kb_s), pmajor(vb_s))
```

```python
import functools
import math

import numpy as np
import jax
import jax.numpy as jnp
from jax import lax
from jax.experimental import pallas as pl
from jax.experimental.pallas import tpu as pltpu

F32 = jnp.float32
BF16 = jnp.bfloat16

D_MODEL = 1024
HEAD_DIM = 64
RWKV_WIDTH = 512
N_RWKV_HEADS = 8
SWA_WIDTH = 256
N_SWA_HEADS = 4
N_SWA_KV_HEADS = 2
SWA_KV_WIDTH = 128
MEM_WIDTH = 256
N_MEM_HEADS = 4
N_MEM = 256
WINDOW = 128
BLOCK = 128
N_BUCKETS = 32
MAX_DISTANCE = 128
DECAY_LORA = 64
AAA_LORA = 64
GATE_LORA = 128
RWKV_PROJ = 3 * RWKV_WIDTH + DECAY_LORA + AAA_LORA + GATE_LORA
SWA_PROJ = SWA_WIDTH + 2 * SWA_KV_WIDTH
IN_PROJ = RWKV_PROJ + SWA_PROJ + MEM_WIDTH
D_FF = 4 * D_MODEL
NORM_EPS = 1e-6
LNX_EPS = 64e-5
ATTN_SCALE = HEAD_DIM ** -0.5
EXP_M05 = math.exp(-0.5)
NEG = -1e30

COL_R, COL_K, COL_V = 0, RWKV_WIDTH, 2 * RWKV_WIDTH
COL_WD = 3 * RWKV_WIDTH
COL_AD = COL_WD + DECAY_LORA
COL_GD = COL_AD + AAA_LORA
COL_SQ = RWKV_PROJ
COL_SK = COL_SQ + SWA_WIDTH
COL_SV = COL_SK + SWA_KV_WIDTH
COL_MQ = RWKV_PROJ + SWA_PROJ

CHUNK = 64
VMEM_LIMIT = 56 * 1024 * 1024


def _dot(a, b):
    return jnp.dot(a.astype(BF16), b.astype(BF16), preferred_element_type=F32)


def _dot_nt(a, b):
    return lax.dot_general(a.astype(BF16), b.astype(BF16), (((1,), (1,)), ((), ())),
                           preferred_element_type=F32)


def _dot_tn(a, b):
    return lax.dot_general(a.astype(BF16), b.astype(BF16), (((0,), (0,)), ((), ())),
                           preferred_element_type=F32)


def _rms(x, g):
    return x * lax.rsqrt(jnp.mean(x * x, -1, keepdims=True) + NORM_EPS) * g


def _params(*sem):
    return pltpu.CompilerParams(dimension_semantics=sem, vmem_limit_bytes=VMEM_LIMIT)


def _full(shape):
    n = len(shape)
    return pl.BlockSpec(shape, lambda *_: (0,) * n)


def _in_proj_kernel(x_ref, g_ref, w_ref, o_ref):
    h = _rms(x_ref[...], g_ref[...])
    o_ref[...] = jnp.dot(h.astype(BF16), w_ref[...], preferred_element_type=F32)


def _in_proj(x2d, g, w_bf16, tm):
    n = x2d.shape[0]
    return pl.pallas_call(
        _in_proj_kernel,
        grid=(n // tm,),
        in_specs=[pl.BlockSpec((tm, D_MODEL), lambda i: (i, 0)),
                  _full((1, D_MODEL)),
                  _full((D_MODEL, IN_PROJ))],
        out_specs=pl.BlockSpec((tm, IN_PROJ), lambda i: (i, 0)),
        out_shape=jax.ShapeDtypeStruct((n, IN_PROJ), F32),
        compiler_params=_params("arbitrary"),
        name="in_proj",
    )(x2d, g, w_bf16)


def _rwkv_features(xs, w0, a0, k_k, k_a, wupw, wupa, wupg):
    r = xs[:, COL_R:COL_R + RWKV_WIDTH]
    k = xs[:, COL_K:COL_K + RWKV_WIDTH]
    v = xs[:, COL_V:COL_V + RWKV_WIDTH]
    wd = xs[:, COL_WD:COL_WD + DECAY_LORA]
    ad = xs[:, COL_AD:COL_AD + AAA_LORA]
    gd = xs[:, COL_GD:COL_GD + GATE_LORA]
    logw = -jax.nn.sigmoid(w0 + _dot(jnp.tanh(wd), wupw)) * EXP_M05
    a_sig = jax.nn.sigmoid(a0 + _dot(ad, wupa))
    gate = _dot(jax.nn.sigmoid(gd), wupg)
    kk = k * k_k
    k2 = k * (1.0 + (a_sig - 1.0) * k_a)
    return r, k2, v, kk, a_sig, logw, gate


def _head_out(y, r, k2, v, gate, r_k, lnx_w, lnx_b):
    m = jnp.mean(y, -1, keepdims=True)
    d = y - m
    var = jnp.mean(d * d, -1, keepdims=True)
    yn = d * lax.rsqrt(var + LNX_EPS) * lnx_w + lnx_b
    bonus = jnp.sum(r * k2 * r_k, -1, keepdims=True) * v
    return (yn + bonus) * gate


RWKV_TILE = 2 * CHUNK
PAIR = 2 * HEAD_DIM
N_PAIRS = N_RWKV_HEADS // 2


def _seg_sum(x, blk):
    xb = x.astype(BF16)
    half = RWKV_WIDTH // 2
    return jnp.concatenate([jnp.dot(xb[:, :half], blk, preferred_element_type=F32),
                            jnp.dot(xb[:, half:], blk, preferred_element_type=F32)], axis=1)


def _rwkv_prompt_kernel(p_ref, mu_ref, w0_ref, a0_ref, kk_ref, ka_ref, rk_ref, lnw_ref, lnb_ref,
                        wupw_ref, wupa_ref, wupg_ref, y_ref, sout_ref, s_scr, prev_scr):
    C, TT, D = CHUNK, RWKV_TILE, HEAD_DIM
    NC = TT // C
    t = pl.program_id(1)

    @pl.when(t == 0)
    def _():
        s_scr[...] = jnp.zeros_like(s_scr)
        prev_scr[...] = jnp.zeros_like(prev_scr)

    p = p_ref[0]
    row = lax.broadcasted_iota(jnp.int32, p.shape, 0)
    prev = jnp.where(row == 0, prev_scr[...], pltpu.roll(p, 1, axis=0))
    prev_scr[...] = p[TT - 1:TT, :]
    xs = p + (prev - p) * mu_ref[...]
    r, k2, v, kk, a_sig, logw, gate = _rwkv_features(
        xs, w0_ref[...], a0_ref[...], kk_ref[...], ka_ref[...], wupw_ref[...], wupa_ref[...], wupg_ref[...])

    half = RWKV_WIDTH // 2
    bi = lax.broadcasted_iota(jnp.int32, (half, half), 0) // D
    bj = lax.broadcasted_iota(jnp.int32, (half, half), 1) // D
    blk = jnp.where(bi == bj, 1.0, 0.0).astype(BF16)
    kkn = kk / jnp.maximum(jnp.sqrt(_seg_sum(kk * kk, blk)), 1e-12)
    bb = kkn * a_sig

    ri = lax.broadcasted_iota(jnp.int32, (TT, TT), 0)
    ci = lax.broadcasted_iota(jnp.int32, (TT, TT), 1)
    tri = jnp.where(jnp.logical_and(ri >= ci, ri // C == ci // C), 1.0, 0.0).astype(BF16)
    l1 = logw.astype(BF16)
    rem = logw - l1.astype(F32)
    l2 = rem.astype(BF16)
    l3 = (rem - l2.astype(F32)).astype(BF16)
    cum = (jnp.dot(tri, l1, preferred_element_type=F32) + jnp.dot(tri, l2, preferred_element_type=F32)
           + jnp.dot(tri, l3, preferred_element_type=F32))
    c_last = jnp.concatenate([jnp.broadcast_to(cum[(c + 1) * C - 1:(c + 1) * C, :], (C, RWKV_WIDTH))
                              for c in range(NC)], axis=0)
    e_pos = jnp.exp(cum)
    e_neg = jnp.exp(-cum)
    e_prev = jnp.exp(cum - logw)
    e_last = jnp.exp(c_last - cum)

    lo_full = (lax.broadcasted_iota(jnp.int32, (TT, RWKV_WIDTH), 1) % PAIR) < D
    at_f = -kkn * e_prev
    rt_f = r * e_pos
    at_lo = jnp.where(lo_full, at_f, 0.0).astype(BF16)
    at_hi = jnp.where(lo_full, 0.0, at_f).astype(BF16)
    rt_lo = jnp.where(lo_full, rt_f, 0.0).astype(BF16)
    rt_hi = jnp.where(lo_full, 0.0, rt_f).astype(BF16)
    bt_b = (bb * e_neg).astype(BF16)
    kt_b = (k2 * e_neg).astype(BF16)
    bh_b = (bb * e_last).astype(BF16)
    kh_b = (k2 * e_last).astype(BF16)
    v_b = v.astype(BF16)

    r2 = lax.broadcasted_iota(jnp.int32, (C, PAIR), 0)
    c2 = lax.broadcasted_iota(jnp.int32, (C, PAIR), 1)
    c2m = jnp.where(c2 >= C, c2 - C, c2)
    mask_ak = jnp.logical_and(c2 >= C, c2m < r2)
    mask_r = c2m <= r2
    strict = lax.broadcasted_iota(jnp.int32, (C, C), 1) < lax.broadcasted_iota(jnp.int32, (C, C), 0)
    lo = c2 < D
    qi = lax.broadcasted_iota(jnp.int32, (PAIR, PAIR), 0) // D
    qj = lax.broadcasted_iota(jnp.int32, (PAIR, PAIR), 1) // D
    diag = qi == qj

    PR = [(c, q) for c in range(NC) for q in range(N_PAIRS)]
    n = len(PR)
    E = range(2)
    win = lambda x, c, q: x[c * C:(c + 1) * C, q * PAIR:(q + 1) * PAIR]
    sc = [_dot_nt(jnp.concatenate([win(at_lo, c, q), win(at_hi, c, q), win(rt_lo, c, q), win(rt_hi, c, q)], axis=0),
                  jnp.concatenate([win(bt_b, c, q), win(kt_b, c, q)], axis=0)) for c, q in PR]
    vr = [pltpu.roll(win(v, c, q), D, axis=1) for c, q in PR]
    vvr = [jnp.concatenate([vr[i], vr[i]], axis=0).astype(BF16) for i in range(n)]
    A = [[jnp.where(strict, sc[i][e * C:(e + 1) * C, :C], 0.0) for e in E] for i in range(n)]
    m_ak = [[jnp.where(mask_ak, sc[i][e * C:(e + 1) * C], 0.0) for e in E] for i in range(n)]
    m_r = [[jnp.where(mask_r, sc[i][(2 + e) * C:(3 + e) * C], 0.0) for e in E] for i in range(n)]

    zf = [[_dot(m_ak[i][e], vvr[i]) for e in E] for i in range(n)]
    X = [[jnp.where(lo, win(at_f, c, q), zf[i][0]), jnp.where(lo, zf[i][1], win(at_f, c, q))]
         for i, (c, q) in enumerate(PR)]
    for k in range(int(math.log2(C))):
        AW = [[_dot(A[i][e], jnp.concatenate([X[i][e], A[i][e]], axis=1)) for e in E] for i in range(n)]
        X = [[X[i][e] + AW[i][e][:, :PAIR] for e in E] for i in range(n)]
        A = [[AW[i][e][:, PAIR:] for e in E] for i in range(n)]

    S = [s_scr[q] for q in range(N_PAIRS)]
    ys = []
    for c in range(NC):
        w_last = jnp.exp(cum[(c + 1) * C - 1:(c + 1) * C, :])
        idx = [c * N_PAIRS + q for q in range(N_PAIRS)]
        mkg = [[_dot_tn(X[i][e], win(bh_b, c, q)) for e in E] for q, i in enumerate(idx)]
        vk = [_dot_tn(win(v_b, c, q), win(kh_b, c, q)) for q, i in enumerate(idx)]
        ry = [[_dot(m_r[i][0], jnp.concatenate([X[i][0], jnp.where(lo, 0.0, vr[i])], axis=0)),
               _dot(m_r[i][1], jnp.concatenate([X[i][1], jnp.where(lo, vr[i], 0.0)], axis=0))] for i in idx]
        rp = [win(rt_f, c, q) + jnp.where(lo, ry[q][0], ry[q][1]) for q in range(N_PAIRS)]
        y0 = [pltpu.roll(jnp.where(lo, ry[q][1], ry[q][0]), D, axis=1) for q in range(N_PAIRS)]
        mk = [jnp.where(diag, jnp.concatenate([mkg[q][0][:D], mkg[q][1][D:]], axis=0), 0.0) for q in range(N_PAIRS)]
        g = [jnp.where(diag, vk[q] + jnp.concatenate([mkg[q][0][D:], mkg[q][1][:D]], axis=0), 0.0)
             for q in range(N_PAIRS)]
        y = [_dot_nt(rp[q], S[q]) for q in range(N_PAIRS)]
        dS = [_dot(S[q], mk[q]) for q in range(N_PAIRS)]
        S = [S[q] * w_last[:, q * PAIR:(q + 1) * PAIR] + dS[q] + g[q] for q in range(N_PAIRS)]
        ys.append(jnp.concatenate([y[q] + y0[q] for q in range(N_PAIRS)], axis=1))
    for q in range(N_PAIRS):
        s_scr[q] = S[q]

    yf = jnp.concatenate(ys, axis=0)
    inv_d = 1.0 / D
    m = _seg_sum(yf, blk) * inv_d
    d = yf - m
    var = _seg_sum(d * d, blk) * inv_d
    yn = d * lax.rsqrt(var + LNX_EPS) * lnw_ref[...] + lnb_ref[...]
    bonus = _seg_sum(r * k2 * rk_ref[...], blk) * v
    y_ref[0] = (yn + bonus) * gate

    @pl.when(t == pl.num_programs(1) - 1)
    def _():
        for q in range(N_PAIRS):
            sout_ref[0, 2 * q] = S[q][:D, :D]
            sout_ref[0, 2 * q + 1] = S[q][D:, D:]


def _rwkv_prompt(proj, mu, w0, a0, k_k, k_a, r_k, lnx_w, lnx_b, wupw, wupa, wupg):
    B, T, _ = proj.shape
    vec = lambda n: _full((1, n))
    return pl.pallas_call(
        _rwkv_prompt_kernel,
        grid=(B, T // RWKV_TILE),
        in_specs=[pl.BlockSpec((1, RWKV_TILE, RWKV_PROJ), lambda b, t: (b, t, 0)),
                  vec(RWKV_PROJ), vec(RWKV_WIDTH), vec(RWKV_WIDTH), vec(RWKV_WIDTH), vec(RWKV_WIDTH),
                  vec(RWKV_WIDTH), vec(RWKV_WIDTH), vec(RWKV_WIDTH),
                  _full((DECAY_LORA, RWKV_WIDTH)), _full((AAA_LORA, RWKV_WIDTH)), _full((GATE_LORA, RWKV_WIDTH))],
        out_specs=[pl.BlockSpec((1, RWKV_TILE, RWKV_WIDTH), lambda b, t: (b, t, 0)),
                   pl.BlockSpec((1, N_RWKV_HEADS, HEAD_DIM, HEAD_DIM), lambda b, t: (b, 0, 0, 0))],
        out_shape=[jax.ShapeDtypeStruct((B, T, RWKV_WIDTH), F32),
                   jax.ShapeDtypeStruct((B, N_RWKV_HEADS, HEAD_DIM, HEAD_DIM), F32)],
        scratch_shapes=[pltpu.VMEM((N_PAIRS, PAIR, PAIR), F32),
                        pltpu.VMEM((1, RWKV_PROJ), F32)],
        compiler_params=_params("arbitrary", "arbitrary"),
        name="rwkv_prompt",
    )(proj, mu, w0, a0, k_k, k_a, r_k, lnx_w, lnx_b, wupw, wupa, wupg)


SEQ_TILE = 8


def _head_blocks(width):
    bi = lax.broadcasted_iota(jnp.int32, (width, width), 0) // HEAD_DIM
    bj = lax.broadcasted_iota(jnp.int32, (width, width), 1) // HEAD_DIM
    return jnp.where(bi == bj, 1.0, 0.0).astype(BF16)


def _rwkv_step_feat_kernel(p_ref, sh_ref, mu_ref, w0_ref, a0_ref, kk_ref, ka_ref, rk_ref,
                           wupw_ref, wupa_ref, wupg_ref, vecs_ref, bonus_ref, gate_ref):
    p = p_ref[...]
    xs = p + (sh_ref[...] - p) * mu_ref[...]
    r, k2, v, kk, a_sig, logw, gate = _rwkv_features(
        xs, w0_ref[...], a0_ref[...], kk_ref[...], ka_ref[...], wupw_ref[...], wupa_ref[...], wupg_ref[...])
    blk = _head_blocks(RWKV_WIDTH // 2)
    kkn = kk / jnp.maximum(jnp.sqrt(_seg_sum(kk * kk, blk)), 1e-12)
    for i, x in enumerate((-kkn, kkn * a_sig, jnp.exp(logw), k2, r, v)):
        vecs_ref[i] = x.T
    bonus_ref[...] = _seg_sum(r * k2 * rk_ref[...], blk) * v
    gate_ref[...] = gate


def _rwkv_step_state_kernel(s_ref, vecs_ref, sout_ref, y_ref):
    S = s_ref[0]
    a, b, w, k, r, v = (vecs_ref[i] for i in range(6))
    sa = jnp.sum(S * a[None], axis=1)
    S = S * w[None] + sa[:, None, :] * b[None] + v[:, None, :] * k[None]
    sout_ref[0] = S
    y_ref[...] = jnp.sum(S * r[None], axis=1)


def _rwkv_step_out_kernel(yt_ref, bonus_ref, gate_ref, lnw_ref, lnb_ref, o_ref):
    y = yt_ref[...].T
    blk = _head_blocks(RWKV_WIDTH // 2)
    inv_d = 1.0 / HEAD_DIM
    m = _seg_sum(y, blk) * inv_d
    d = y - m
    var = _seg_sum(d * d, blk) * inv_d
    yn = d * lax.rsqrt(var + LNX_EPS) * lnw_ref[...] + lnb_ref[...]
    o_ref[...] = (yn + bonus_ref[...]) * gate_ref[...]


def _rwkv_step(proj, shift, state_t, mu, w0, a0, k_k, k_a, r_k, lnx_w, lnx_b, wupw, wupa, wupg):
    n = proj.shape[0]
    vec = lambda m: _full((1, m))
    rows = _full((n, RWKV_PROJ))
    wide = _full((n, RWKV_WIDTH))
    vecs, bonus, gate = pl.pallas_call(
        _rwkv_step_feat_kernel,
        grid=(1,),
        in_specs=[rows, rows, vec(RWKV_PROJ), vec(RWKV_WIDTH), vec(RWKV_WIDTH), vec(RWKV_WIDTH), vec(RWKV_WIDTH),
                  vec(RWKV_WIDTH),
                  _full((DECAY_LORA, RWKV_WIDTH)), _full((AAA_LORA, RWKV_WIDTH)), _full((GATE_LORA, RWKV_WIDTH))],
        out_specs=[_full((6, RWKV_WIDTH, n)), wide, wide],
        out_shape=[jax.ShapeDtypeStruct((6, RWKV_WIDTH, n), F32),
                   jax.ShapeDtypeStruct((n, RWKV_WIDTH), F32), jax.ShapeDtypeStruct((n, RWKV_WIDTH), F32)],
        compiler_params=_params("arbitrary"),
        name="rwkv_step_feat",
    )(proj, shift, mu, w0, a0, k_k, k_a, r_k, wupw, wupa, wupg)
    st_spec = pl.BlockSpec((1, HEAD_DIM, HEAD_DIM, n), lambda h: (h, 0, 0, 0))
    state_new, yt = pl.pallas_call(
        _rwkv_step_state_kernel,
        grid=(N_RWKV_HEADS,),
        in_specs=[st_spec, pl.BlockSpec((6, HEAD_DIM, n), lambda h: (0, h, 0))],
        out_specs=[st_spec, pl.BlockSpec((HEAD_DIM, n), lambda h: (h, 0))],
        out_shape=[jax.ShapeDtypeStruct(state_t.shape, F32), jax.ShapeDtypeStruct((RWKV_WIDTH, n), F32)],
        compiler_params=_params("arbitrary"),
        name="rwkv_step_state",
    )(state_t, vecs)
    y = pl.pallas_call(
        _rwkv_step_out_kernel,
        grid=(1,),
        in_specs=[_full((RWKV_WIDTH, n)), wide, wide, vec(RWKV_WIDTH), vec(RWKV_WIDTH)],
        out_specs=wide,
        out_shape=jax.ShapeDtypeStruct((n, RWKV_WIDTH), F32),
        compiler_params=_params("arbitrary"),
        name="rwkv_step_out",
    )(yt, bonus, gate, lnx_w, lnx_b)
    return y, state_new


def _t5_bucket_np(dist):
    max_exact = N_BUCKETS // 2
    d = np.maximum(dist, 1).astype(np.float32)
    large = max_exact + (np.log(d / np.float32(max_exact)) / np.float32(math.log(MAX_DISTANCE / max_exact))
                         * np.float32(N_BUCKETS - max_exact)).astype(np.int32)
    large = np.minimum(large, N_BUCKETS - 1)
    return np.where(dist < max_exact, dist, large).astype(np.int32)


def _prompt_bucket_table():
    qi = np.arange(BLOCK)[:, None]
    kj = np.arange(2 * BLOCK)[None, :]
    dist = BLOCK + qi - kj
    valid = (dist >= 0) & (dist <= WINDOW)
    return np.where(valid, _t5_bucket_np(np.maximum(dist, 0)), -1).astype(np.int32)


def _decode_bucket_table():
    dist = WINDOW - np.arange(WINDOW)
    return np.broadcast_to(_t5_bucket_np(dist)[None, :], (8, WINDOW)).astype(np.int32).copy()


def _bias_from_buckets(bkt, relb_ref, h, init):
    acc = jnp.full(bkt.shape, init, F32)
    for b in range(N_BUCKETS):
        acc = jnp.where(bkt == b, relb_ref[b, h], acc)
    return acc


ATT_TILE = 4 * BLOCK


def _attn_prompt_kernel(qs_ref, qm_ref, kc_ref, kp_ref, vc_ref, vp_ref, mk_ref, mv_ref, bkt_ref, relb_ref, sink_ref,
                        sqg_ref, skg_ref, mqg_ref, ys_ref, ym_ref, kn_ref, bias_scr, mk_scr, mv_scr):
    i = pl.program_id(1)
    grp = N_SWA_HEADS // N_SWA_KV_HEADS
    ones = jnp.ones((2 * BLOCK, HEAD_DIM), F32)

    @pl.when(i == 0)
    def _():
        bkt = bkt_ref[...]
        for j in range(N_SWA_KV_HEADS):
            for g in range(grp):
                bias_scr[j, g * BLOCK:(g + 1) * BLOCK, :] = _bias_from_buckets(bkt, relb_ref, j * grp + g, NEG)
        mk = mk_ref[0]
        mv = mv_ref[0]
        for h in range(N_MEM_HEADS):
            sl = slice(h * HEAD_DIM, (h + 1) * HEAD_DIM)
            mk_scr[h] = mk[:, sl].astype(BF16)
            mv_scr[h] = jnp.concatenate([mv[:, sl], ones], axis=1).astype(BF16)

    qs = qs_ref[0]
    qm = qm_ref[0]
    kc = kc_ref[0]
    kp = kp_ref[0]
    vc = vc_ref[0]
    vp = vp_ref[0]
    sqg, skg, mqg = sqg_ref[...], skg_ref[...], mqg_ref[...]
    rowi = lax.broadcasted_iota(jnp.int32, (2 * BLOCK, 1), 0)
    col = lax.broadcasted_iota(jnp.int32, (2 * BLOCK, 2 * BLOCK), 1)
    pad_mask = jnp.logical_and(i == 0, col < BLOCK)

    def rms_heads(x, g, n_heads):
        ms = jnp.dot((x * x).astype(BF16), _head_blocks(n_heads * HEAD_DIM), preferred_element_type=F32)
        return x * lax.rsqrt(ms * (1.0 / HEAD_DIM) + NORM_EPS) * jnp.concatenate([g] * n_heads, axis=1)

    hsl = [slice(h * HEAD_DIM, (h + 1) * HEAD_DIM) for h in range(N_SWA_HEADS)]
    qs_n = rms_heads(qs, sqg, N_SWA_HEADS) * ATTN_SCALE
    qm_n = rms_heads(qm, mqg, N_MEM_HEADS) * ATTN_SCALE
    kc_n = rms_heads(kc, skg, N_SWA_KV_HEADS)
    kp_n = rms_heads(kp, skg, N_SWA_KV_HEADS)
    kn_ref[0] = kc_n[ATT_TILE - BLOCK:]
    chains = [(a, j) for a in range(ATT_TILE // BLOCK) for j in range(N_SWA_KV_HEADS)]
    lhs, keys, vals, sinkcol = [], [], [], []
    for a, j in chains:
        rs = slice(a * BLOCK, (a + 1) * BLOCK)
        lhs.append(jnp.concatenate([qs_n[rs, hsl[j * grp + g]] for g in range(grp)], axis=0))
        if a == 0:
            keys.append(jnp.concatenate([kp_n[:, hsl[j]], kc_n[:BLOCK, hsl[j]]], axis=0))
            vv = jnp.concatenate([vp[:, hsl[j]], vc[:BLOCK, hsl[j]]], axis=0)
        else:
            ks = slice((a - 1) * BLOCK, (a + 1) * BLOCK)
            keys.append(kc_n[ks, hsl[j]])
            vv = vc[ks, hsl[j]]
        vals.append(jnp.concatenate([vv, ones], axis=1))
        sinkcol.append(jnp.where(rowi < BLOCK, sink_ref[0, j * grp], sink_ref[0, j * grp + 1]))
    qmn = [qm_n[:, hsl[h]] for h in range(N_MEM_HEADS)]

    s_w = [_dot_nt(lhs[c], keys[c]) for c in range(len(chains))]
    s_m = [_dot_nt(qmn[h], mk_scr[h]) for h in range(N_MEM_HEADS)]
    e_w, m_w = [], []
    for c, (a, j) in enumerate(chains):
        s = s_w[c] + bias_scr[j]
        if a == 0:
            s = jnp.where(pad_mask, NEG, s)
        m = jnp.maximum(jnp.max(s, -1, keepdims=True), sinkcol[c])
        m_w.append(m)
        e_w.append(jnp.exp(s - m))
    e_m = [jnp.exp(s_m[h] - jnp.max(s_m[h], -1, keepdims=True)) for h in range(N_MEM_HEADS)]
    o_w = [_dot(e_w[c], vals[c]) for c in range(len(chains))]
    o_m = [_dot(e_m[h], mv_scr[h]) for h in range(N_MEM_HEADS)]
    for c, (a, j) in enumerate(chains):
        den = o_w[c][:, HEAD_DIM:HEAD_DIM + 1] + jnp.exp(sinkcol[c] - m_w[c])
        o = o_w[c][:, :HEAD_DIM] / den
        for g in range(grp):
            ys_ref[0, a * BLOCK:(a + 1) * BLOCK, hsl[j * grp + g]] = o[g * BLOCK:(g + 1) * BLOCK]
    for h in range(N_MEM_HEADS):
        ym_ref[0, :, hsl[h]] = o_m[h][:, :HEAD_DIM] / o_m[h][:, HEAD_DIM:HEAD_DIM + 1]


def _attn_prompt(proj, mk, mv, rel_bias, sinks, sqg, skg, mqg):
    B, T, _ = proj.shape
    bkt = jnp.asarray(_prompt_bucket_table())
    smem = pl.BlockSpec(memory_space=pltpu.SMEM)
    kblk, vblk = COL_SK // SWA_KV_WIDTH, COL_SV // SWA_KV_WIDTH
    prev = lambda b, i: (b, jnp.maximum((ATT_TILE // BLOCK) * i - 1, 0))
    memb = pl.BlockSpec((1, N_MEM, MEM_WIDTH), lambda b, i: (b, 0, 0))
    return pl.pallas_call(
        _attn_prompt_kernel,
        grid=(B, T // ATT_TILE),
        in_specs=[pl.BlockSpec((1, ATT_TILE, SWA_WIDTH), lambda b, i: (b, i, COL_SQ // SWA_WIDTH)),
                  pl.BlockSpec((1, ATT_TILE, MEM_WIDTH), lambda b, i: (b, i, COL_MQ // MEM_WIDTH)),
                  pl.BlockSpec((1, ATT_TILE, SWA_KV_WIDTH), lambda b, i: (b, i, kblk)),
                  pl.BlockSpec((1, BLOCK, SWA_KV_WIDTH), lambda b, i: prev(b, i) + (kblk,)),
                  pl.BlockSpec((1, ATT_TILE, SWA_KV_WIDTH), lambda b, i: (b, i, vblk)),
                  pl.BlockSpec((1, BLOCK, SWA_KV_WIDTH), lambda b, i: prev(b, i) + (vblk,)),
                  memb, memb,
                  _full((BLOCK, 2 * BLOCK)), smem, smem,
                  _full((1, HEAD_DIM)), _full((1, HEAD_DIM)), _full((1, HEAD_DIM))],
        out_specs=[pl.BlockSpec((1, ATT_TILE, SWA_WIDTH), lambda b, i: (b, i, 0)),
                   pl.BlockSpec((1, ATT_TILE, MEM_WIDTH), lambda b, i: (b, i, 0)),
                   pl.BlockSpec((1, BLOCK, SWA_KV_WIDTH), lambda b, i: (b, 0, 0))],
        out_shape=[jax.ShapeDtypeStruct((B, T, SWA_WIDTH), F32),
                   jax.ShapeDtypeStruct((B, T, MEM_WIDTH), F32),
                   jax.ShapeDtypeStruct((B, BLOCK, SWA_KV_WIDTH), F32)],
        scratch_shapes=[pltpu.VMEM((N_SWA_KV_HEADS, 2 * BLOCK, 2 * BLOCK), F32),
                        pltpu.VMEM((N_MEM_HEADS, N_MEM, HEAD_DIM), BF16),
                        pltpu.VMEM((N_MEM_HEADS, N_MEM, 2 * HEAD_DIM), BF16)],
        compiler_params=_params("arbitrary", "arbitrary"),
        name="attn_prompt",
    )(proj, proj, proj, proj, proj, proj, mk, mv, bkt, rel_bias, sinks, sqg, skg, mqg)


def _memory_kv_kernel(mem_ref, g_ref, w_ref, kg_ref, mk_ref, mv_ref):
    kv = jnp.dot(_rms(mem_ref[0], g_ref[...]).astype(BF16), w_ref[...], preferred_element_type=F32)
    kg = kg_ref[...]
    for h in range(N_MEM_HEADS):
        sl = slice(h * HEAD_DIM, (h + 1) * HEAD_DIM)
        mk_ref[0, :, sl] = _rms(kv[:, sl], kg)
    mv_ref[0] = kv[:, MEM_WIDTH:]


def _memory_kv(mem, g, w_bf16, kg):
    B = mem.shape[0]
    blk = pl.BlockSpec((1, N_MEM, MEM_WIDTH), lambda b: (b, 0, 0))
    return pl.pallas_call(
        _memory_kv_kernel,
        grid=(B,),
        in_specs=[pl.BlockSpec((1, N_MEM, D_MODEL), lambda b: (b, 0, 0)),
                  _full((1, D_MODEL)), _full((D_MODEL, 2 * MEM_WIDTH)), _full((1, HEAD_DIM))],
        out_specs=[blk, blk],
        out_shape=[jax.ShapeDtypeStruct((B, N_MEM, MEM_WIDTH), F32)] * 2,
        compiler_params=_params("arbitrary"),
        name="memory_kv",
    )(mem, g, w_bf16, kg)


def _head_rows(x_row, n_heads):
    return jnp.concatenate([x_row[:, h * HEAD_DIM:(h + 1) * HEAD_DIM] for h in range(n_heads)], axis=0)


def _spread(xh, n_groups, rows_per_group):
    n = xh.shape[0]
    tiled = jnp.concatenate([xh] * n_groups, axis=1)
    rowi = lax.broadcasted_iota(jnp.int32, tiled.shape, 0)
    lane_grp = lax.broadcasted_iota(jnp.int32, tiled.shape, 1) // HEAD_DIM
    return jnp.where(lane_grp == rowi // rows_per_group, tiled, 0.0)


def _gather_groups(full, n_groups, rows_per_group):
    n = full.shape[0]
    rowi = lax.broadcasted_iota(jnp.int32, (n, HEAD_DIM), 0)
    out = jnp.zeros((n, HEAD_DIM), F32)
    for g in range(n_groups):
        out = out + jnp.where(rowi // rows_per_group == g, full[:, g * HEAD_DIM:(g + 1) * HEAD_DIM], 0.0)
    return out


def _decode_attn_kernel(p_ref, kbuf_ref, vbuf_ref, mk_ref, mv_ref, bkt_ref, relb_ref, sink_ref,
                        sqg_ref, skg_ref, mqg_ref, ys_ref, ym_ref, kout_ref, vout_ref):
    grp = N_SWA_HEADS // N_SWA_KV_HEADS
    hrow = lax.broadcasted_iota(jnp.int32, (8, WINDOW), 0)
    bias_w = jnp.zeros((8, WINDOW), F32)
    for h in range(N_SWA_HEADS):
        bias_w = jnp.where(hrow == h, _bias_from_buckets(bkt_ref[...], relb_ref, h, 0.0), bias_w)
    bias_w = bias_w[:N_SWA_HEADS]
    hcol = lax.broadcasted_iota(jnp.int32, (N_SWA_HEADS, 1), 0)
    bias_new = jnp.zeros((N_SWA_HEADS, 1), F32)
    sink = jnp.zeros((N_SWA_HEADS, 1), F32)
    for h in range(N_SWA_HEADS):
        bias_new = jnp.where(hcol == h, relb_ref[0, h], bias_new)
        sink = jnp.where(hcol == h, sink_ref[0, h], sink)
    rowi = lax.broadcasted_iota(jnp.int32, (SWA_KV_WIDTH, WINDOW), 0)
    lanei = lax.broadcasted_iota(jnp.int32, (SWA_KV_WIDTH, WINDOW), 1)
    eye = rowi == lanei

    B = range(SEQ_TILE)
    prow = [p_ref[b:b + 1, :] for b in B]
    q = [_rms(_head_rows(prow[b][:, COL_SQ:COL_SQ + SWA_WIDTH], N_SWA_HEADS), sqg_ref[...]) * ATTN_SCALE for b in B]
    kn = [_rms(_head_rows(prow[b][:, COL_SK:COL_SK + SWA_KV_WIDTH], N_SWA_KV_HEADS), skg_ref[...]) for b in B]
    kn_row = [jnp.concatenate([kn[b][0:1], kn[b][1:2]], axis=1) for b in B]
    vn_row = [prow[b][:, COL_SV:COL_SV + SWA_KV_WIDTH] for b in B]
    qm = [_spread(q[b], N_SWA_KV_HEADS, grp) for b in B]
    qmem = [_spread(_rms(_head_rows(prow[b][:, COL_MQ:COL_MQ + MEM_WIDTH], N_MEM_HEADS), mqg_ref[...]) * ATTN_SCALE,
                    N_MEM_HEADS, 1) for b in B]
    s = [_dot(qm[b], kbuf_ref[b]) + bias_w for b in B]
    sm = [_dot(qmem[b], mk_ref[b]) for b in B]
    s_new = [jnp.sum(qm[b] * kn_row[b], -1, keepdims=True) + bias_new for b in B]
    m = [jnp.maximum(jnp.maximum(jnp.max(s[b], -1, keepdims=True), s_new[b]), sink) for b in B]
    e = [jnp.exp(s[b] - m[b]) for b in B]
    e_new = [jnp.exp(s_new[b] - m[b]) for b in B]
    den = [jnp.sum(e[b], -1, keepdims=True) + e_new[b] + jnp.exp(sink - m[b]) for b in B]
    em = [jnp.exp(sm[b] - jnp.max(sm[b], -1, keepdims=True)) for b in B]
    ov = [_dot_nt(e[b], vbuf_ref[b]) for b in B]
    omf = [_dot_nt(em[b], mv_ref[b]) for b in B]
    kn_col = [jnp.sum(jnp.where(eye, kn_row[b], 0.0), -1, keepdims=True) for b in B]
    vn_col = [jnp.sum(jnp.where(eye, vn_row[b], 0.0), -1, keepdims=True) for b in B]
    for b in B:
        o = _gather_groups((ov[b] + e_new[b] * vn_row[b]) / den[b], N_SWA_KV_HEADS, grp)
        om = _gather_groups(omf[b], N_MEM_HEADS, 1) / jnp.sum(em[b], -1, keepdims=True)
        for h in range(N_SWA_HEADS):
            ys_ref[b:b + 1, h * HEAD_DIM:(h + 1) * HEAD_DIM] = o[h:h + 1]
        for h in range(N_MEM_HEADS):
            ym_ref[b:b + 1, h * HEAD_DIM:(h + 1) * HEAD_DIM] = om[h:h + 1]
        kout_ref[b] = jnp.where(lanei == WINDOW - 1, kn_col[b], pltpu.roll(kbuf_ref[b], WINDOW - 1, axis=1))
        vout_ref[b] = jnp.where(lanei == WINDOW - 1, vn_col[b], pltpu.roll(vbuf_ref[b], WINDOW - 1, axis=1))


def _decode_attn(proj, kbuf, vbuf, mk, mv, rel_bias, sinks, sqg, skg, mqg):
    n = proj.shape[0]
    bkt = jnp.asarray(_decode_bucket_table())
    smem = pl.BlockSpec(memory_space=pltpu.SMEM)
    win = pl.BlockSpec((SEQ_TILE, WINDOW, SWA_KV_WIDTH), lambda i: (i, 0, 0))
    memb = pl.BlockSpec((SEQ_TILE, N_MEM, MEM_WIDTH), lambda i: (i, 0, 0))
    return pl.pallas_call(
        _decode_attn_kernel,
        grid=(n // SEQ_TILE,),
        in_specs=[pl.BlockSpec((SEQ_TILE, IN_PROJ), lambda i: (i, 0)), win, win, memb, memb,
                  _full((8, WINDOW)), smem, smem, _full((1, HEAD_DIM)), _full((1, HEAD_DIM)), _full((1, HEAD_DIM))],
        out_specs=[pl.BlockSpec((SEQ_TILE, SWA_WIDTH), lambda i: (i, 0)),
                   pl.BlockSpec((SEQ_TILE, MEM_WIDTH), lambda i: (i, 0)), win, win],
        out_shape=[jax.ShapeDtypeStruct((n, SWA_WIDTH), F32), jax.ShapeDtypeStruct((n, MEM_WIDTH), F32),
                   jax.ShapeDtypeStruct(kbuf.shape, F32), jax.ShapeDtypeStruct(vbuf.shape, F32)],
        compiler_params=_params("arbitrary"),
        name="decode_attn",
    )(proj, kbuf, vbuf, mk, mv, bkt, rel_bias, sinks, sqg, skg, mqg)


FF_CHUNK = 1024


def _out_ffn_kernel(x_ref, yr_ref, ys_ref, ym_ref, wo_ref, g2_ref, w1_ref, w2_ref, o_ref):
    x1 = (x_ref[...]
          + jnp.dot(yr_ref[...].astype(BF16), wo_ref[0:RWKV_WIDTH, :], preferred_element_type=F32)
          + jnp.dot(ys_ref[...].astype(BF16), wo_ref[RWKV_WIDTH:RWKV_WIDTH + SWA_WIDTH, :],
                    preferred_element_type=F32)
          + jnp.dot(ym_ref[...].astype(BF16), wo_ref[RWKV_WIDTH + SWA_WIDTH:, :], preferred_element_type=F32))
    h2 = _rms(x1, g2_ref[...]).astype(BF16)
    ff = None
    for c in range(D_FF // FF_CHUNK):
        u = jnp.dot(h2, w1_ref[:, c * FF_CHUNK:(c + 1) * FF_CHUNK], preferred_element_type=F32)
        u = jnp.square(jnp.maximum(u, 0.0)).astype(BF16)
        d = jnp.dot(u, w2_ref[c * FF_CHUNK:(c + 1) * FF_CHUNK, :], preferred_element_type=F32)
        ff = d if ff is None else ff + d
    o_ref[...] = x1 + ff


def _out_ffn(x2d, yr, ys, ym, wo, g2, w1, w2, tm):
    n = x2d.shape[0]
    rows = lambda w: pl.BlockSpec((tm, w), lambda i: (i, 0))
    const = lambda shape: pl.BlockSpec(shape, lambda i: (0, 0), pipeline_mode=pl.Buffered(1))
    return pl.pallas_call(
        _out_ffn_kernel,
        grid=(n // tm,),
        in_specs=[rows(D_MODEL), rows(RWKV_WIDTH), rows(SWA_WIDTH), rows(MEM_WIDTH),
                  const((D_MODEL, D_MODEL)), _full((1, D_MODEL)), const((D_MODEL, D_FF)), const((D_FF, D_MODEL))],
        out_specs=rows(D_MODEL),
        out_shape=jax.ShapeDtypeStruct((n, D_MODEL), F32),
        compiler_params=_params("arbitrary"),
        name="out_ffn",
    )(x2d, yr, ys, ym, wo, g2, w1, w2)


def kernel(x_prompt, x_sample, state_rwkv, state_shift, cache_swa_k, cache_swa_v, cache_mem_k, cache_mem_v,
           mem_prompt, rel_bias, norm1_g, w_in, mu_shift, w0, w_up_w, a0, w_up_a, w_up_g, k_k, k_a, r_k,
           lnx_w, lnx_b, q_norm_swa, k_norm_swa, sinks, mem_norm_g, w_mem_kv, q_norm_mem, k_norm_mem,
           w_out, norm2_g, w_ff1, w_ff2):
    B, T, _ = x_prompt.shape
    Bd = x_sample.shape[0]
    l = 0
    w_in_b = w_in[l].astype(BF16)
    w_out_b = w_out[l].astype(BF16)
    w1_b = w_ff1[l].astype(BF16)
    w2_b = w_ff2[l].astype(BF16)
    rwkv_params = (mu_shift[l][None], w0[l][None], a0[l][None], k_k[l][None], k_a[l][None],
                   r_k[l].reshape(1, RWKV_WIDTH), lnx_w[l][None], lnx_b[l][None],
                   w_up_w[l].astype(BF16), w_up_a[l].astype(BF16), w_up_g[l].astype(BF16))
    sqg, skg, mqg, mkg = q_norm_swa[l][None], k_norm_swa[l][None], q_norm_mem[l][None], k_norm_mem[l][None]
    g1, g2 = norm1_g[l][None], norm2_g[l][None]

    xp = x_prompt.reshape(B * T, D_MODEL)
    proj_p = _in_proj(xp, g1, w_in_b, 512).reshape(B, T, IN_PROJ)
    mk, mv = _memory_kv(mem_prompt, mem_norm_g[l][None], w_mem_kv[l].astype(BF16), mkg)
    yr_p, s_p = _rwkv_prompt(proj_p, *rwkv_params)
    ys_p, ym_p, kn_p = _attn_prompt(proj_p, mk, mv, rel_bias, sinks[l][None], sqg, skg, mqg)
    y_p = _out_ffn(xp, yr_p.reshape(B * T, RWKV_WIDTH), ys_p.reshape(B * T, SWA_WIDTH),
                   ym_p.reshape(B * T, MEM_WIDTH), w_out_b, g2, w1_b, w2_b, 512).reshape(B, T, D_MODEL)
    shift_p = proj_p[:, T - 1, :RWKV_PROJ]
    vb_p = proj_p[:, T - WINDOW:, COL_SV:COL_SV + SWA_KV_WIDTH]

    xs = x_sample.reshape(Bd, D_MODEL)
    proj_s = _in_proj(xs, g1, w_in_b, Bd)
    yr_s, st_s = _rwkv_step(proj_s[:, :RWKV_PROJ], state_shift[l], jnp.transpose(state_rwkv[l], (1, 2, 3, 0)),
                            *rwkv_params)
    s_s = jnp.transpose(st_s, (3, 0, 1, 2))
    fmajor = lambda c: jnp.transpose(c, (0, 2, 3, 1)).reshape(Bd, c.shape[2] * HEAD_DIM, c.shape[1])
    ys_s, ym_s, kb_s, vb_s = _decode_attn(
        proj_s, fmajor(cache_swa_k[l]), fmajor(cache_swa_v[l]), fmajor(cache_mem_k[l]), fmajor(cache_mem_v[l]),
        rel_bias, sinks[l][None], sqg, skg, mqg)
    pmajor = lambda c: jnp.transpose(c.reshape(Bd, N_SWA_KV_HEADS, HEAD_DIM, WINDOW), (0, 3, 1, 2))[None]
    y_s = _out_ffn(xs, yr_s, ys_s, ym_s, w_out_b, g2, w1_b, w2_b, Bd).reshape(Bd, 1, D_MODEL)

    return (y_p, y_s,
            s_p[None], shift_p[None],
            kn_p.reshape(1, B, WINDOW, N_SWA_KV_HEADS, HEAD_DIM),
            vb_p.reshape(1, B, WINDOW, N_SWA_KV_HEADS, HEAD_DIM),
            mk.reshape(1, B, N_MEM, N_MEM_HEADS, HEAD_DIM), mv.reshape(1, B, N_MEM, N_MEM_HEADS, HEAD_DIM),
            s_s[None], proj_s[:, :RWKV_PROJ][None],
            pmajor(kb_s), pmajor(vb_s))
```

```python
import functools
import math

import numpy as np
import jax
import jax.numpy as jnp
from jax import lax
from jax.experimental import pallas as pl
from jax.experimental.pallas import tpu as pltpu

F32 = jnp.float32
BF16 = jnp.bfloat16

D_MODEL = 1024
HEAD_DIM = 64
RWKV_WIDTH = 512
N_RWKV_HEADS = 8
SWA_WIDTH = 256
N_SWA_HEADS = 4
N_SWA_KV_HEADS = 2
SWA_KV_WIDTH = 128
MEM_WIDTH = 256
N_MEM_HEADS = 4
N_MEM = 256
WINDOW = 128
BLOCK = 128
N_BUCKETS = 32
MAX_DISTANCE = 128
DECAY_LORA = 64
AAA_LORA = 64
GATE_LORA = 128
RWKV_PROJ = 3 * RWKV_WIDTH + DECAY_LORA + AAA_LORA + GATE_LORA
SWA_PROJ = SWA_WIDTH + 2 * SWA_KV_WIDTH
IN_PROJ = RWKV_PROJ + SWA_PROJ + MEM_WIDTH
D_FF = 4 * D_MODEL
NORM_EPS = 1e-6
LNX_EPS = 64e-5
ATTN_SCALE = HEAD_DIM ** -0.5
EXP_M05 = math.exp(-0.5)
NEG = -1e30

COL_R, COL_K, COL_V = 0, RWKV_WIDTH, 2 * RWKV_WIDTH
COL_WD = 3 * RWKV_WIDTH
COL_AD = COL_WD + DECAY_LORA
COL_GD = COL_AD + AAA_LORA
COL_SQ = RWKV_PROJ
COL_SK = COL_SQ + SWA_WIDTH
COL_SV = COL_SK + SWA_KV_WIDTH
COL_MQ = RWKV_PROJ + SWA_PROJ

CHUNK = 64
VMEM_LIMIT = 56 * 1024 * 1024


def _dot(a, b):
    return jnp.dot(a.astype(BF16), b.astype(BF16), preferred_element_type=F32)


def _dot_nt(a, b):
    return lax.dot_general(a.astype(BF16), b.astype(BF16), (((1,), (1,)), ((), ())),
                           preferred_element_type=F32)


def _dot_tn(a, b):
    return lax.dot_general(a.astype(BF16), b.astype(BF16), (((0,), (0,)), ((), ())),
                           preferred_element_type=F32)


def _rms(x, g):
    return x * lax.rsqrt(jnp.mean(x * x, -1, keepdims=True) + NORM_EPS) * g


def _params(*sem):
    return pltpu.CompilerParams(dimension_semantics=sem, vmem_limit_bytes=VMEM_LIMIT)


def _full(shape):
    n = len(shape)
    return pl.BlockSpec(shape, lambda *_: (0,) * n)


def _in_proj_kernel(x_ref, g_ref, w_ref, o_ref):
    h = _rms(x_ref[...], g_ref[...])
    o_ref[...] = jnp.dot(h.astype(BF16), w_ref[...], preferred_element_type=F32)


def _in_proj(x2d, g, w_bf16, tm):
    n = x2d.shape[0]
    return pl.pallas_call(
        _in_proj_kernel,
        grid=(n // tm,),
        in_specs=[pl.BlockSpec((tm, D_MODEL), lambda i: (i, 0)),
                  _full((1, D_MODEL)),
                  _full((D_MODEL, IN_PROJ))],
        out_specs=pl.BlockSpec((tm, IN_PROJ), lambda i: (i, 0)),
        out_shape=jax.ShapeDtypeStruct((n, IN_PROJ), F32),
        compiler_params=_params("arbitrary"),
        name="in_proj",
    )(x2d, g, w_bf16)


def _rwkv_features(xs, w0, a0, k_k, k_a, wupw, wupa, wupg):
    r = xs[:, COL_R:COL_R + RWKV_WIDTH]
    k = xs[:, COL_K:COL_K + RWKV_WIDTH]
    v = xs[:, COL_V:COL_V + RWKV_WIDTH]
    wd = xs[:, COL_WD:COL_WD + DECAY_LORA]
    ad = xs[:, COL_AD:COL_AD + AAA_LORA]
    gd = xs[:, COL_GD:COL_GD + GATE_LORA]
    logw = -jax.nn.sigmoid(w0 + _dot(jnp.tanh(wd), wupw)) * EXP_M05
    a_sig = jax.nn.sigmoid(a0 + _dot(ad, wupa))
    gate = _dot(jax.nn.sigmoid(gd), wupg)
    kk = k * k_k
    k2 = k * (1.0 + (a_sig - 1.0) * k_a)
    return r, k2, v, kk, a_sig, logw, gate


def _head_out(y, r, k2, v, gate, r_k, lnx_w, lnx_b):
    m = jnp.mean(y, -1, keepdims=True)
    d = y - m
    var = jnp.mean(d * d, -1, keepdims=True)
    yn = d * lax.rsqrt(var + LNX_EPS) * lnx_w + lnx_b
    bonus = jnp.sum(r * k2 * r_k, -1, keepdims=True) * v
    return (yn + bonus) * gate


RWKV_TILE = 2 * CHUNK
PAIR = 2 * HEAD_DIM
N_PAIRS = N_RWKV_HEADS // 2


def _seg_sum(x, blk):
    xb = x.astype(BF16)
    half = RWKV_WIDTH // 2
    return jnp.concatenate([jnp.dot(xb[:, :half], blk, preferred_element_type=F32),
                            jnp.dot(xb[:, half:], blk, preferred_element_type=F32)], axis=1)


(_AT_LO, _AT_HI, _RT_LO, _RT_HI, _BT, _KT, _BH, _KH, _V, _VR, _VR_LO, _VR_HI) = range(12)
N_OPS_B = 12


def _rwkv_prompt_kernel(p_ref, mu_ref, w0_ref, a0_ref, kk_ref, ka_ref, rk_ref, lnw_ref, lnb_ref,
                        wupw_ref, wupa_ref, wupg_ref, y_ref, sout_ref, s_scr, prev_scr, opb_scr, opf_scr, wl_scr):
    C, TT, D = CHUNK, RWKV_TILE, HEAD_DIM
    NC = TT // C
    s = pl.program_id(1)
    wslot = lax.rem(s, 2)
    rslot = 1 - wslot

    @pl.when(s == 0)
    def _():
        s_scr[...] = jnp.zeros_like(s_scr)
        prev_scr[...] = jnp.zeros_like(prev_scr)
        opb_scr[...] = jnp.zeros_like(opb_scr)
        opf_scr[...] = jnp.zeros_like(opf_scr)
        wl_scr[...] = jnp.zeros_like(wl_scr)

    blk = _head_blocks(RWKV_WIDTH // 2)
    r2 = lax.broadcasted_iota(jnp.int32, (C, PAIR), 0)
    c2 = lax.broadcasted_iota(jnp.int32, (C, PAIR), 1)
    c2m = jnp.where(c2 >= C, c2 - C, c2)
    mask_ak = jnp.logical_and(c2 >= C, c2m < r2)
    mask_r = c2m <= r2
    strict = lax.broadcasted_iota(jnp.int32, (C, C), 1) < lax.broadcasted_iota(jnp.int32, (C, C), 0)
    lo = c2 < D
    qi = lax.broadcasted_iota(jnp.int32, (PAIR, PAIR), 0) // D
    qj = lax.broadcasted_iota(jnp.int32, (PAIR, PAIR), 1) // D
    diag = qi == qj
    PR = [(c, q) for c in range(NC) for q in range(N_PAIRS)]
    n = len(PR)
    E = range(2)
    opb = lambda k, c, q: opb_scr[rslot, k, c * C:(c + 1) * C, q * PAIR:(q + 1) * PAIR]

    p = p_ref[0]
    row = lax.broadcasted_iota(jnp.int32, p.shape, 0)
    prev = jnp.where(row == 0, prev_scr[...], pltpu.roll(p, 1, axis=0))
    prev_scr[...] = p[TT - 1:TT, :]
    xs = p + (prev - p) * mu_ref[...]
    r = xs[:, COL_R:COL_R + RWKV_WIDTH]
    k = xs[:, COL_K:COL_K + RWKV_WIDTH]
    v = xs[:, COL_V:COL_V + RWKV_WIDTH]
    wd_t = jnp.tanh(xs[:, COL_WD:COL_WD + DECAY_LORA]).astype(BF16)
    ad_b = xs[:, COL_AD:COL_AD + AAA_LORA].astype(BF16)
    gd_s = jax.nn.sigmoid(xs[:, COL_GD:COL_GD + GATE_LORA]).astype(BF16)

    sc = [_dot_nt(jnp.concatenate([opb(_AT_LO, c, q), opb(_AT_HI, c, q), opb(_RT_LO, c, q), opb(_RT_HI, c, q)], axis=0),
                  jnp.concatenate([opb(_BT, c, q), opb(_KT, c, q)], axis=0)) for c, q in PR]

    A = [[jnp.where(strict, sc[i][e * C:(e + 1) * C, :C], 0.0) for e in E] for i in range(n)]
    m_ak = [[jnp.where(mask_ak, sc[i][e * C:(e + 1) * C], 0.0) for e in E] for i in range(n)]
    m_r = [[jnp.where(mask_r, sc[i][(2 + e) * C:(3 + e) * C], 0.0) for e in E] for i in range(n)]
    vvr = [jnp.concatenate([opb(_VR, c, q), opb(_VR, c, q)], axis=0) for c, q in PR]
    zf = [[jnp.dot(m_ak[i][e].astype(BF16), vvr[i], preferred_element_type=F32) for e in E] for i in range(n)]
    X = [[opb(_AT_LO, c, q).astype(F32) + jnp.where(lo, 0.0, zf[i][0]),
          opb(_AT_HI, c, q).astype(F32) + jnp.where(lo, zf[i][1], 0.0)] for i, (c, q) in enumerate(PR)]

    def doubling(X, A):
        AW = [[_dot(A[i][e], jnp.concatenate([X[i][e], A[i][e]], axis=1)) for e in E] for i in range(n)]
        return ([[X[i][e] + AW[i][e][:, :PAIR] for e in E] for i in range(n)],
                [[AW[i][e][:, PAIR:] for e in E] for i in range(n)])

    n_dbl = int(math.log2(C))
    for _ in range(2):
        X, A = doubling(X, A)

    lw_pre = jnp.dot(wd_t, wupw_ref[...], preferred_element_type=F32)
    a_pre = jnp.dot(ad_b, wupa_ref[...], preferred_element_type=F32)
    gate = jnp.dot(gd_s, wupg_ref[...], preferred_element_type=F32)

    logw = -jax.nn.sigmoid(w0_ref[...] + lw_pre) * EXP_M05
    a_sig = jax.nn.sigmoid(a0_ref[...] + a_pre)
    kk = k * kk_ref[...]
    k2 = k * (1.0 + (a_sig - 1.0) * ka_ref[...])
    ri = lax.broadcasted_iota(jnp.int32, (TT, TT), 0)
    ci = lax.broadcasted_iota(jnp.int32, (TT, TT), 1)
    tri = jnp.where(jnp.logical_and(ri >= ci, ri // C == ci // C), 1.0, 0.0).astype(BF16)
    l1 = logw.astype(BF16)
    rem = logw - l1.astype(F32)
    l2 = rem.astype(BF16)
    l3 = (rem - l2.astype(F32)).astype(BF16)
    kk_sq = kk * kk
    rk2 = r * k2 * rk_ref[...]

    for _ in range(2):
        X, A = doubling(X, A)

    kk_ss = _seg_sum(kk_sq, blk)
    bonus = _seg_sum(rk2, blk) * v
    cum = (jnp.dot(tri, l1, preferred_element_type=F32) + jnp.dot(tri, l2, preferred_element_type=F32)
           + jnp.dot(tri, l3, preferred_element_type=F32))

    for _ in range(n_dbl - 4):
        X, A = doubling(X, A)

    kkn = kk / jnp.maximum(jnp.sqrt(kk_ss), 1e-12)
    bb = kkn * a_sig
    c_last = jnp.concatenate([jnp.broadcast_to(cum[(c + 1) * C - 1:(c + 1) * C, :], (C, RWKV_WIDTH))
                              for c in range(NC)], axis=0)
    e_pos = jnp.exp(cum)
    e_neg = jnp.exp(-cum)
    e_prev = jnp.exp(cum - logw)
    e_last = jnp.exp(c_last - cum)
    lo_full = (lax.broadcasted_iota(jnp.int32, (TT, RWKV_WIDTH), 1) % PAIR) < D
    at_f = -kkn * e_prev
    rt_f = r * e_pos
    vr_f = jnp.concatenate([pltpu.roll(v[:, q * PAIR:(q + 1) * PAIR], D, axis=1) for q in range(N_PAIRS)], axis=1)
    staged = {
        _AT_LO: jnp.where(lo_full, at_f, 0.0), _AT_HI: jnp.where(lo_full, 0.0, at_f),
        _RT_LO: jnp.where(lo_full, rt_f, 0.0), _RT_HI: jnp.where(lo_full, 0.0, rt_f),
        _BT: bb * e_neg, _KT: k2 * e_neg, _BH: bb * e_last, _KH: k2 * e_last,
        _V: v, _VR: vr_f, _VR_LO: jnp.where(lo_full, vr_f, 0.0), _VR_HI: jnp.where(lo_full, 0.0, vr_f),
    }
    for idx_op, val in staged.items():
        opb_scr[wslot, idx_op] = val.astype(BF16)
    opf_scr[wslot, 0] = gate
    opf_scr[wslot, 1] = bonus
    for c in range(NC):
        wl_scr[wslot, c:c + 1, :] = jnp.exp(cum[(c + 1) * C - 1:(c + 1) * C, :])

    S_old = [s_scr[q] for q in range(N_PAIRS)]
    S = S_old
    ys = []
    for c in range(NC):
        w_last = wl_scr[rslot, c:c + 1, :]
        idx = [c * N_PAIRS + q for q in range(N_PAIRS)]
        mkg = [[_dot_tn(X[i][e], opb(_BH, c, q)) for e in E] for q, i in enumerate(idx)]
        vk = [_dot_tn(opb(_V, c, q), opb(_KH, c, q)) for q, i in enumerate(idx)]
        ry = [[_dot(m_r[i][0], jnp.concatenate([X[i][0].astype(BF16), opb(_VR_HI, c, q)], axis=0)),
               _dot(m_r[i][1], jnp.concatenate([X[i][1].astype(BF16), opb(_VR_LO, c, q)], axis=0))]
              for q, i in enumerate(idx)]
        rp = [opb(_RT_LO, c, q).astype(F32) + opb(_RT_HI, c, q).astype(F32) + jnp.where(lo, ry[q][0], ry[q][1])
              for q in range(N_PAIRS)]
        y0 = [pltpu.roll(jnp.where(lo, ry[q][1], ry[q][0]), D, axis=1) for q in range(N_PAIRS)]
        mk = [jnp.where(diag, jnp.concatenate([mkg[q][0][:D], mkg[q][1][D:]], axis=0), 0.0) for q in range(N_PAIRS)]
        g = [jnp.where(diag, vk[q] + jnp.concatenate([mkg[q][0][D:], mkg[q][1][:D]], axis=0), 0.0)
             for q in range(N_PAIRS)]
        y = [_dot_nt(rp[q], S[q]) for q in range(N_PAIRS)]
        dS = [_dot(S[q], mk[q]) for q in range(N_PAIRS)]
        S = [S[q] * w_last[:, q * PAIR:(q + 1) * PAIR] + dS[q] + g[q] for q in range(N_PAIRS)]
        ys.append(jnp.concatenate([y[q] + y0[q] for q in range(N_PAIRS)], axis=1))
    valid = s > 0
    S = [jnp.where(valid, S[q], S_old[q]) for q in range(N_PAIRS)]
    for q in range(N_PAIRS):
        s_scr[q] = S[q]

    yf = jnp.concatenate(ys, axis=0)
    inv_d = 1.0 / D
    m = _seg_sum(yf, blk) * inv_d
    d = yf - m
    var = _seg_sum(d * d, blk) * inv_d
    yn = d * lax.rsqrt(var + LNX_EPS) * lnw_ref[...] + lnb_ref[...]
    y_ref[0] = (yn + opf_scr[rslot, 1]) * opf_scr[rslot, 0]

    @pl.when(s == pl.num_programs(1) - 1)
    def _():
        for q in range(N_PAIRS):
            sout_ref[0, 2 * q] = S[q][:D, :D]
            sout_ref[0, 2 * q + 1] = S[q][D:, D:]


def _rwkv_prompt(proj, mu, w0, a0, k_k, k_a, r_k, lnx_w, lnx_b, wupw, wupa, wupg):
    B, T, _ = proj.shape
    nt = T // RWKV_TILE
    vec = lambda n: _full((1, n))
    return pl.pallas_call(
        _rwkv_prompt_kernel,
        grid=(B, nt + 1),
        in_specs=[pl.BlockSpec((1, RWKV_TILE, RWKV_PROJ), lambda b, s: (b, jnp.minimum(s, nt - 1), 0)),
                  vec(RWKV_PROJ), vec(RWKV_WIDTH), vec(RWKV_WIDTH), vec(RWKV_WIDTH), vec(RWKV_WIDTH),
                  vec(RWKV_WIDTH), vec(RWKV_WIDTH), vec(RWKV_WIDTH),
                  _full((DECAY_LORA, RWKV_WIDTH)), _full((AAA_LORA, RWKV_WIDTH)), _full((GATE_LORA, RWKV_WIDTH))],
        out_specs=[pl.BlockSpec((1, RWKV_TILE, RWKV_WIDTH), lambda b, s: (b, jnp.maximum(s - 1, 0), 0)),
                   pl.BlockSpec((1, N_RWKV_HEADS, HEAD_DIM, HEAD_DIM), lambda b, s: (b, 0, 0, 0))],
        out_shape=[jax.ShapeDtypeStruct((B, T, RWKV_WIDTH), F32),
                   jax.ShapeDtypeStruct((B, N_RWKV_HEADS, HEAD_DIM, HEAD_DIM), F32)],
        scratch_shapes=[pltpu.VMEM((N_PAIRS, PAIR, PAIR), F32),
                        pltpu.VMEM((1, RWKV_PROJ), F32),
                        pltpu.VMEM((2, N_OPS_B, RWKV_TILE, RWKV_WIDTH), BF16),
                        pltpu.VMEM((2, 2, RWKV_TILE, RWKV_WIDTH), F32),
                        pltpu.VMEM((2, 8, RWKV_WIDTH), F32)],
        compiler_params=_params("arbitrary", "arbitrary"),
        name="rwkv_prompt",
    )(proj, mu, w0, a0, k_k, k_a, r_k, lnx_w, lnx_b, wupw, wupa, wupg)


SEQ_TILE = 8


def _head_blocks(width):
    bi = lax.broadcasted_iota(jnp.int32, (width, width), 0) // HEAD_DIM
    bj = lax.broadcasted_iota(jnp.int32, (width, width), 1) // HEAD_DIM
    return jnp.where(bi == bj, 1.0, 0.0).astype(BF16)


def _rwkv_step_feat_kernel(p_ref, sh_ref, mu_ref, w0_ref, a0_ref, kk_ref, ka_ref, rk_ref,
                           wupw_ref, wupa_ref, wupg_ref, vecs_ref, bonus_ref, gate_ref):
    p = p_ref[...]
    xs = p + (sh_ref[...] - p) * mu_ref[...]
    r, k2, v, kk, a_sig, logw, gate = _rwkv_features(
        xs, w0_ref[...], a0_ref[...], kk_ref[...], ka_ref[...], wupw_ref[...], wupa_ref[...], wupg_ref[...])
    blk = _head_blocks(RWKV_WIDTH // 2)
    kkn = kk / jnp.maximum(jnp.sqrt(_seg_sum(kk * kk, blk)), 1e-12)
    for i, x in enumerate((-kkn, kkn * a_sig, jnp.exp(logw), k2, r, v)):
        vecs_ref[i] = x.T
    bonus_ref[...] = _seg_sum(r * k2 * rk_ref[...], blk) * v
    gate_ref[...] = gate


def _rwkv_step_state_kernel(s_ref, vecs_ref, sout_ref, y_ref):
    S = s_ref[0]
    a, b, w, k, r, v = (vecs_ref[i] for i in range(6))
    sa = jnp.sum(S * a[None], axis=1)
    S = S * w[None] + sa[:, None, :] * b[None] + v[:, None, :] * k[None]
    sout_ref[0] = S
    y_ref[...] = jnp.sum(S * r[None], axis=1)


def _rwkv_step_out_kernel(yt_ref, bonus_ref, gate_ref, lnw_ref, lnb_ref, o_ref):
    y = yt_ref[...].T
    blk = _head_blocks(RWKV_WIDTH // 2)
    inv_d = 1.0 / HEAD_DIM
    m = _seg_sum(y, blk) * inv_d
    d = y - m
    var = _seg_sum(d * d, blk) * inv_d
    yn = d * lax.rsqrt(var + LNX_EPS) * lnw_ref[...] + lnb_ref[...]
    o_ref[...] = (yn + bonus_ref[...]) * gate_ref[...]


def _rwkv_step(proj, shift, state_t, mu, w0, a0, k_k, k_a, r_k, lnx_w, lnx_b, wupw, wupa, wupg):
    n = proj.shape[0]
    vec = lambda m: _full((1, m))
    rows = _full((n, RWKV_PROJ))
    wide = _full((n, RWKV_WIDTH))
    vecs, bonus, gate = pl.pallas_call(
        _rwkv_step_feat_kernel,
        grid=(1,),
        in_specs=[rows, rows, vec(RWKV_PROJ), vec(RWKV_WIDTH), vec(RWKV_WIDTH), vec(RWKV_WIDTH), vec(RWKV_WIDTH),
                  vec(RWKV_WIDTH),
                  _full((DECAY_LORA, RWKV_WIDTH)), _full((AAA_LORA, RWKV_WIDTH)), _full((GATE_LORA, RWKV_WIDTH))],
        out_specs=[_full((6, RWKV_WIDTH, n)), wide, wide],
        out_shape=[jax.ShapeDtypeStruct((6, RWKV_WIDTH, n), F32),
                   jax.ShapeDtypeStruct((n, RWKV_WIDTH), F32), jax.ShapeDtypeStruct((n, RWKV_WIDTH), F32)],
        compiler_params=_params("arbitrary"),
        name="rwkv_step_feat",
    )(proj, shift, mu, w0, a0, k_k, k_a, r_k, wupw, wupa, wupg)
    st_spec = pl.BlockSpec((1, HEAD_DIM, HEAD_DIM, n), lambda h: (h, 0, 0, 0))
    state_new, yt = pl.pallas_call(
        _rwkv_step_state_kernel,
        grid=(N_RWKV_HEADS,),
        in_specs=[st_spec, pl.BlockSpec((6, HEAD_DIM, n), lambda h: (0, h, 0))],
        out_specs=[st_spec, pl.BlockSpec((HEAD_DIM, n), lambda h: (h, 0))],
        out_shape=[jax.ShapeDtypeStruct(state_t.shape, F32), jax.ShapeDtypeStruct((RWKV_WIDTH, n), F32)],
        compiler_params=_params("arbitrary"),
        name="rwkv_step_state",
    )(state_t, vecs)
    y = pl.pallas_call(
        _rwkv_step_out_kernel,
        grid=(1,),
        in_specs=[_full((RWKV_WIDTH, n)), wide, wide, vec(RWKV_WIDTH), vec(RWKV_WIDTH)],
        out_specs=wide,
        out_shape=jax.ShapeDtypeStruct((n, RWKV_WIDTH), F32),
        compiler_params=_params("arbitrary"),
        name="rwkv_step_out",
    )(yt, bonus, gate, lnx_w, lnx_b)
    return y, state_new


def _t5_bucket_np(dist):
    max_exact = N_BUCKETS // 2
    d = np.maximum(dist, 1).astype(np.float32)
    large = max_exact + (np.log(d / np.float32(max_exact)) / np.float32(math.log(MAX_DISTANCE / max_exact))
                         * np.float32(N_BUCKETS - max_exact)).astype(np.int32)
    large = np.minimum(large, N_BUCKETS - 1)
    return np.where(dist < max_exact, dist, large).astype(np.int32)


def _prompt_bucket_table():
    qi = np.arange(BLOCK)[:, None]
    kj = np.arange(2 * BLOCK)[None, :]
    dist = BLOCK + qi - kj
    valid = (dist >= 0) & (dist <= WINDOW)
    return np.where(valid, _t5_bucket_np(np.maximum(dist, 0)), -1).astype(np.int32)


def _decode_bucket_table():
    dist = WINDOW - np.arange(WINDOW)
    return np.broadcast_to(_t5_bucket_np(dist)[None, :], (8, WINDOW)).astype(np.int32).copy()


def _bias_from_buckets(bkt, relb_ref, h, init):
    acc = jnp.full(bkt.shape, init, F32)
    for b in range(N_BUCKETS):
        acc = jnp.where(bkt == b, relb_ref[b, h], acc)
    return acc


ATT_TILE = 4 * BLOCK


def _attn_prompt_kernel(qs_ref, qm_ref, kc_ref, kp_ref, vc_ref, vp_ref, mk_ref, mv_ref, bkt_ref, relb_ref, sink_ref,
                        sqg_ref, skg_ref, mqg_ref, ys_ref, ym_ref, kn_ref, bias_scr, mk_scr, mv_scr):
    i = pl.program_id(1)
    grp = N_SWA_HEADS // N_SWA_KV_HEADS
    ones = jnp.ones((2 * BLOCK, HEAD_DIM), F32)

    @pl.when(i == 0)
    def _():
        bkt = bkt_ref[...]
        for j in range(N_SWA_KV_HEADS):
            for g in range(grp):
                bias_scr[j, g * BLOCK:(g + 1) * BLOCK, :] = _bias_from_buckets(bkt, relb_ref, j * grp + g, NEG)
        mk = mk_ref[0]
        mv = mv_ref[0]
        for h in range(N_MEM_HEADS):
            sl = slice(h * HEAD_DIM, (h + 1) * HEAD_DIM)
            mk_scr[h] = mk[:, sl].astype(BF16)
            mv_scr[h] = jnp.concatenate([mv[:, sl], ones], axis=1).astype(BF16)

    qs = qs_ref[0]
    qm = qm_ref[0]
    kc = kc_ref[0]
    kp = kp_ref[0]
    vc = vc_ref[0]
    vp = vp_ref[0]
    sqg, skg, mqg = sqg_ref[...], skg_ref[...], mqg_ref[...]
    rowi = lax.broadcasted_iota(jnp.int32, (2 * BLOCK, 1), 0)
    col = lax.broadcasted_iota(jnp.int32, (2 * BLOCK, 2 * BLOCK), 1)
    pad_mask = jnp.logical_and(i == 0, col < BLOCK)

    def rms_heads(x, g, n_heads):
        ms = jnp.dot((x * x).astype(BF16), _head_blocks(n_heads * HEAD_DIM), preferred_element_type=F32)
        return x * lax.rsqrt(ms * (1.0 / HEAD_DIM) + NORM_EPS) * jnp.concatenate([g] * n_heads, axis=1)

    hsl = [slice(h * HEAD_DIM, (h + 1) * HEAD_DIM) for h in range(N_SWA_HEADS)]
    qs_n = rms_heads(qs, sqg, N_SWA_HEADS) * ATTN_SCALE
    qm_n = rms_heads(qm, mqg, N_MEM_HEADS) * ATTN_SCALE
    kc_n = rms_heads(kc, skg, N_SWA_KV_HEADS)
    kp_n = rms_heads(kp, skg, N_SWA_KV_HEADS)
    kn_ref[0] = kc_n[ATT_TILE - BLOCK:]
    chains = [(a, j) for a in range(ATT_TILE // BLOCK) for j in range(N_SWA_KV_HEADS)]
    lhs, keys, vals, sinkcol = [], [], [], []
    for a, j in chains:
        rs = slice(a * BLOCK, (a + 1) * BLOCK)
        lhs.append(jnp.concatenate([qs_n[rs, hsl[j * grp + g]] for g in range(grp)], axis=0))
        if a == 0:
            keys.append(jnp.concatenate([kp_n[:, hsl[j]], kc_n[:BLOCK, hsl[j]]], axis=0))
            vv = jnp.concatenate([vp[:, hsl[j]], vc[:BLOCK, hsl[j]]], axis=0)
        else:
            ks = slice((a - 1) * BLOCK, (a + 1) * BLOCK)
            keys.append(kc_n[ks, hsl[j]])
            vv = vc[ks, hsl[j]]
        vals.append(jnp.concatenate([vv, ones], axis=1))
        sinkcol.append(jnp.where(rowi < BLOCK, sink_ref[0, j * grp], sink_ref[0, j * grp + 1]))
    qmn = [qm_n[:, hsl[h]] for h in range(N_MEM_HEADS)]

    s_w = [_dot_nt(lhs[c], keys[c]) for c in range(len(chains))]
    s_m = [_dot_nt(qmn[h], mk_scr[h]) for h in range(N_MEM_HEADS)]
    e_w, m_w = [], []
    for c, (a, j) in enumerate(chains):
        s = s_w[c] + bias_scr[j]
        if a == 0:
            s = jnp.where(pad_mask, NEG, s)
        m = jnp.maximum(jnp.max(s, -1, keepdims=True), sinkcol[c])
        m_w.append(m)
        e_w.append(jnp.exp(s - m))
    e_m = [jnp.exp(s_m[h] - jnp.max(s_m[h], -1, keepdims=True)) for h in range(N_MEM_HEADS)]
    o_w = [_dot(e_w[c], vals[c]) for c in range(len(chains))]
    o_m = [_dot(e_m[h], mv_scr[h]) for h in range(N_MEM_HEADS)]
    for c, (a, j) in enumerate(chains):
        den = o_w[c][:, HEAD_DIM:HEAD_DIM + 1] + jnp.exp(sinkcol[c] - m_w[c])
        o = o_w[c][:, :HEAD_DIM] / den
        for g in range(grp):
            ys_ref[0, a * BLOCK:(a + 1) * BLOCK, hsl[j * grp + g]] = o[g * BLOCK:(g + 1) * BLOCK]
    for h in range(N_MEM_HEADS):
        ym_ref[0, :, hsl[h]] = o_m[h][:, :HEAD_DIM] / o_m[h][:, HEAD_DIM:HEAD_DIM + 1]


def _attn_prompt(proj, mk, mv, rel_bias, sinks, sqg, skg, mqg):
    B, T, _ = proj.shape
    bkt = jnp.asarray(_prompt_bucket_table())
    smem = pl.BlockSpec(memory_space=pltpu.SMEM)
    kblk, vblk = COL_SK // SWA_KV_WIDTH, COL_SV // SWA_KV_WIDTH
    prev = lambda b, i: (b, jnp.maximum((ATT_TILE // BLOCK) * i - 1, 0))
    memb = pl.BlockSpec((1, N_MEM, MEM_WIDTH), lambda b, i: (b, 0, 0))
    return pl.pallas_call(
        _attn_prompt_kernel,
        grid=(B, T // ATT_TILE),
        in_specs=[pl.BlockSpec((1, ATT_TILE, SWA_WIDTH), lambda b, i: (b, i, COL_SQ // SWA_WIDTH)),
                  pl.BlockSpec((1, ATT_TILE, MEM_WIDTH), lambda b, i: (b, i, COL_MQ // MEM_WIDTH)),
                  pl.BlockSpec((1, ATT_TILE, SWA_KV_WIDTH), lambda b, i: (b, i, kblk)),
                  pl.BlockSpec((1, BLOCK, SWA_KV_WIDTH), lambda b, i: prev(b, i) + (kblk,)),
                  pl.BlockSpec((1, ATT_TILE, SWA_KV_WIDTH), lambda b, i: (b, i, vblk)),
                  pl.BlockSpec((1, BLOCK, SWA_KV_WIDTH), lambda b, i: prev(b, i) + (vblk,)),
                  memb, memb,
                  _full((BLOCK, 2 * BLOCK)), smem, smem,
                  _full((1, HEAD_DIM)), _full((1, HEAD_DIM)), _full((1, HEAD_DIM))],
        out_specs=[pl.BlockSpec((1, ATT_TILE, SWA_WIDTH), lambda b, i: (b, i, 0)),
                   pl.BlockSpec((1, ATT_TILE, MEM_WIDTH), lambda b, i: (b, i, 0)),
                   pl.BlockSpec((1, BLOCK, SWA_KV_WIDTH), lambda b, i: (b, 0, 0))],
        out_shape=[jax.ShapeDtypeStruct((B, T, SWA_WIDTH), F32),
                   jax.ShapeDtypeStruct((B, T, MEM_WIDTH), F32),
                   jax.ShapeDtypeStruct((B, BLOCK, SWA_KV_WIDTH), F32)],
        scratch_shapes=[pltpu.VMEM((N_SWA_KV_HEADS, 2 * BLOCK, 2 * BLOCK), F32),
                        pltpu.VMEM((N_MEM_HEADS, N_MEM, HEAD_DIM), BF16),
                        pltpu.VMEM((N_MEM_HEADS, N_MEM, 2 * HEAD_DIM), BF16)],
        compiler_params=_params("arbitrary", "arbitrary"),
        name="attn_prompt",
    )(proj, proj, proj, proj, proj, proj, mk, mv, bkt, rel_bias, sinks, sqg, skg, mqg)


def _memory_kv_kernel(mem_ref, g_ref, w_ref, kg_ref, mk_ref, mv_ref):
    kv = jnp.dot(_rms(mem_ref[0], g_ref[...]).astype(BF16), w_ref[...], preferred_element_type=F32)
    kg = kg_ref[...]
    for h in range(N_MEM_HEADS):
        sl = slice(h * HEAD_DIM, (h + 1) * HEAD_DIM)
        mk_ref[0, :, sl] = _rms(kv[:, sl], kg)
    mv_ref[0] = kv[:, MEM_WIDTH:]


def _memory_kv(mem, g, w_bf16, kg):
    B = mem.shape[0]
    blk = pl.BlockSpec((1, N_MEM, MEM_WIDTH), lambda b: (b, 0, 0))
    return pl.pallas_call(
        _memory_kv_kernel,
        grid=(B,),
        in_specs=[pl.BlockSpec((1, N_MEM, D_MODEL), lambda b: (b, 0, 0)),
                  _full((1, D_MODEL)), _full((D_MODEL, 2 * MEM_WIDTH)), _full((1, HEAD_DIM))],
        out_specs=[blk, blk],
        out_shape=[jax.ShapeDtypeStruct((B, N_MEM, MEM_WIDTH), F32)] * 2,
        compiler_params=_params("arbitrary"),
        name="memory_kv",
    )(mem, g, w_bf16, kg)


def _head_rows(x_row, n_heads):
    return jnp.concatenate([x_row[:, h * HEAD_DIM:(h + 1) * HEAD_DIM] for h in range(n_heads)], axis=0)


def _spread(xh, n_groups, rows_per_group):
    n = xh.shape[0]
    tiled = jnp.concatenate([xh] * n_groups, axis=1)
    rowi = lax.broadcasted_iota(jnp.int32, tiled.shape, 0)
    lane_grp = lax.broadcasted_iota(jnp.int32, tiled.shape, 1) // HEAD_DIM
    return jnp.where(lane_grp == rowi // rows_per_group, tiled, 0.0)


def _gather_groups(full, n_groups, rows_per_group):
    n = full.shape[0]
    rowi = lax.broadcasted_iota(jnp.int32, (n, HEAD_DIM), 0)
    out = jnp.zeros((n, HEAD_DIM), F32)
    for g in range(n_groups):
        out = out + jnp.where(rowi // rows_per_group == g, full[:, g * HEAD_DIM:(g + 1) * HEAD_DIM], 0.0)
    return out


def _decode_attn_kernel(p_ref, kbuf_ref, vbuf_ref, mk_ref, mv_ref, bkt_ref, relb_ref, sink_ref,
                        sqg_ref, skg_ref, mqg_ref, ys_ref, ym_ref, kout_ref, vout_ref):
    grp = N_SWA_HEADS // N_SWA_KV_HEADS
    hrow = lax.broadcasted_iota(jnp.int32, (8, WINDOW), 0)
    bias_w = jnp.zeros((8, WINDOW), F32)
    for h in range(N_SWA_HEADS):
        bias_w = jnp.where(hrow == h, _bias_from_buckets(bkt_ref[...], relb_ref, h, 0.0), bias_w)
    bias_w = bias_w[:N_SWA_HEADS]
    hcol = lax.broadcasted_iota(jnp.int32, (N_SWA_HEADS, 1), 0)
    bias_new = jnp.zeros((N_SWA_HEADS, 1), F32)
    sink = jnp.zeros((N_SWA_HEADS, 1), F32)
    for h in range(N_SWA_HEADS):
        bias_new = jnp.where(hcol == h, relb_ref[0, h], bias_new)
        sink = jnp.where(hcol == h, sink_ref[0, h], sink)
    rowi = lax.broadcasted_iota(jnp.int32, (SWA_KV_WIDTH, WINDOW), 0)
    lanei = lax.broadcasted_iota(jnp.int32, (SWA_KV_WIDTH, WINDOW), 1)
    eye = rowi == lanei

    B = range(SEQ_TILE)
    prow = [p_ref[b:b + 1, :] for b in B]
    q = [_rms(_head_rows(prow[b][:, COL_SQ:COL_SQ + SWA_WIDTH], N_SWA_HEADS), sqg_ref[...]) * ATTN_SCALE for b in B]
    kn = [_rms(_head_rows(prow[b][:, COL_SK:COL_SK + SWA_KV_WIDTH], N_SWA_KV_HEADS), skg_ref[...]) for b in B]
    kn_row = [jnp.concatenate([kn[b][0:1], kn[b][1:2]], axis=1) for b in B]
    vn_row = [prow[b][:, COL_SV:COL_SV + SWA_KV_WIDTH] for b in B]
    qm = [_spread(q[b], N_SWA_KV_HEADS, grp) for b in B]
    qmem = [_spread(_rms(_head_rows(prow[b][:, COL_MQ:COL_MQ + MEM_WIDTH], N_MEM_HEADS), mqg_ref[...]) * ATTN_SCALE,
                    N_MEM_HEADS, 1) for b in B]
    s = [_dot(qm[b], kbuf_ref[b]) + bias_w for b in B]
    sm = [_dot(qmem[b], mk_ref[b]) for b in B]
    s_new = [jnp.sum(qm[b] * kn_row[b], -1, keepdims=True) + bias_new for b in B]
    m = [jnp.maximum(jnp.maximum(jnp.max(s[b], -1, keepdims=True), s_new[b]), sink) for b in B]
    e = [jnp.exp(s[b] - m[b]) for b in B]
    e_new = [jnp.exp(s_new[b] - m[b]) for b in B]
    den = [jnp.sum(e[b], -1, keepdims=True) + e_new[b] + jnp.exp(sink - m[b]) for b in B]
    em = [jnp.exp(sm[b] - jnp.max(sm[b], -1, keepdims=True)) for b in B]
    ov = [_dot_nt(e[b], vbuf_ref[b]) for b in B]
    omf = [_dot_nt(em[b], mv_ref[b]) for b in B]
    kn_col = [jnp.sum(jnp.where(eye, kn_row[b], 0.0), -1, keepdims=True) for b in B]
    vn_col = [jnp.sum(jnp.where(eye, vn_row[b], 0.0), -1, keepdims=True) for b in B]
    for b in B:
        o = _gather_groups((ov[b] + e_new[b] * vn_row[b]) / den[b], N_SWA_KV_HEADS, grp)
        om = _gather_groups(omf[b], N_MEM_HEADS, 1) / jnp.sum(em[b], -1, keepdims=True)
        for h in range(N_SWA_HEADS):
            ys_ref[b:b + 1, h * HEAD_DIM:(h + 1) * HEAD_DIM] = o[h:h + 1]
        for h in range(N_MEM_HEADS):
            ym_ref[b:b + 1, h * HEAD_DIM:(h + 1) * HEAD_DIM] = om[h:h + 1]
        kout_ref[b] = jnp.where(lanei == WINDOW - 1, kn_col[b], pltpu.roll(kbuf_ref[b], WINDOW - 1, axis=1))
        vout_ref[b] = jnp.where(lanei == WINDOW - 1, vn_col[b], pltpu.roll(vbuf_ref[b], WINDOW - 1, axis=1))


def _decode_attn(proj, kbuf, vbuf, mk, mv, rel_bias, sinks, sqg, skg, mqg):
    n = proj.shape[0]
    bkt = jnp.asarray(_decode_bucket_table())
    smem = pl.BlockSpec(memory_space=pltpu.SMEM)
    win = pl.BlockSpec((SEQ_TILE, WINDOW, SWA_KV_WIDTH), lambda i: (i, 0, 0))
    memb = pl.BlockSpec((SEQ_TILE, N_MEM, MEM_WIDTH), lambda i: (i, 0, 0))
    return pl.pallas_call(
        _decode_attn_kernel,
        grid=(n // SEQ_TILE,),
        in_specs=[pl.BlockSpec((SEQ_TILE, IN_PROJ), lambda i: (i, 0)), win, win, memb, memb,
                  _full((8, WINDOW)), smem, smem, _full((1, HEAD_DIM)), _full((1, HEAD_DIM)), _full((1, HEAD_DIM))],
        out_specs=[pl.BlockSpec((SEQ_TILE, SWA_WIDTH), lambda i: (i, 0)),
                   pl.BlockSpec((SEQ_TILE, MEM_WIDTH), lambda i: (i, 0)), win, win],
        out_shape=[jax.ShapeDtypeStruct((n, SWA_WIDTH), F32), jax.ShapeDtypeStruct((n, MEM_WIDTH), F32),
                   jax.ShapeDtypeStruct(kbuf.shape, F32), jax.ShapeDtypeStruct(vbuf.shape, F32)],
        compiler_params=_params("arbitrary"),
        name="decode_attn",
    )(proj, kbuf, vbuf, mk, mv, bkt, rel_bias, sinks, sqg, skg, mqg)


FF_CHUNK = 1024


def _out_ffn_kernel(x_ref, yr_ref, ys_ref, ym_ref, wo_ref, g2_ref, w1_ref, w2_ref, o_ref):
    x1 = (x_ref[...]
          + jnp.dot(yr_ref[...].astype(BF16), wo_ref[0:RWKV_WIDTH, :], preferred_element_type=F32)
          + jnp.dot(ys_ref[...].astype(BF16), wo_ref[RWKV_WIDTH:RWKV_WIDTH + SWA_WIDTH, :],
                    preferred_element_type=F32)
          + jnp.dot(ym_ref[...].astype(BF16), wo_ref[RWKV_WIDTH + SWA_WIDTH:, :], preferred_element_type=F32))
    h2 = _rms(x1, g2_ref[...]).astype(BF16)
    ff = None
    for c in range(D_FF // FF_CHUNK):
        u = jnp.dot(h2, w1_ref[:, c * FF_CHUNK:(c + 1) * FF_CHUNK], preferred_element_type=F32)
        u = jnp.square(jnp.maximum(u, 0.0)).astype(BF16)
        d = jnp.dot(u, w2_ref[c * FF_CHUNK:(c + 1) * FF_CHUNK, :], preferred_element_type=F32)
        ff = d if ff is None else ff + d
    o_ref[...] = x1 + ff


def _out_ffn(x2d, yr, ys, ym, wo, g2, w1, w2, tm):
    n = x2d.shape[0]
    rows = lambda w: pl.BlockSpec((tm, w), lambda i: (i, 0))
    const = lambda shape: pl.BlockSpec(shape, lambda i: (0, 0), pipeline_mode=pl.Buffered(1))
    return pl.pallas_call(
        _out_ffn_kernel,
        grid=(n // tm,),
        in_specs=[rows(D_MODEL), rows(RWKV_WIDTH), rows(SWA_WIDTH), rows(MEM_WIDTH),
                  const((D_MODEL, D_MODEL)), _full((1, D_MODEL)), const((D_MODEL, D_FF)), const((D_FF, D_MODEL))],
        out_specs=rows(D_MODEL),
        out_shape=jax.ShapeDtypeStruct((n, D_MODEL), F32),
        compiler_params=_params("arbitrary"),
        name="out_ffn",
    )(x2d, yr, ys, ym, wo, g2, w1, w2)


def kernel(x_prompt, x_sample, state_rwkv, state_shift, cache_swa_k, cache_swa_v, cache_mem_k, cache_mem_v,
           mem_prompt, rel_bias, norm1_g, w_in, mu_shift, w0, w_up_w, a0, w_up_a, w_up_g, k_k, k_a, r_k,
           lnx_w, lnx_b, q_norm_swa, k_norm_swa, sinks, mem_norm_g, w_mem_kv, q_norm_mem, k_norm_mem,
           w_out, norm2_g, w_ff1, w_ff2):
    B, T, _ = x_prompt.shape
    Bd = x_sample.shape[0]
    l = 0
    w_in_b = w_in[l].astype(BF16)
    w_out_b = w_out[l].astype(BF16)
    w1_b = w_ff1[l].astype(BF16)
    w2_b = w_ff2[l].astype(BF16)
    rwkv_params = (mu_shift[l][None], w0[l][None], a0[l][None], k_k[l][None], k_a[l][None],
                   r_k[l].reshape(1, RWKV_WIDTH), lnx_w[l][None], lnx_b[l][None],
                   w_up_w[l].astype(BF16), w_up_a[l].astype(BF16), w_up_g[l].astype(BF16))
    sqg, skg, mqg, mkg = q_norm_swa[l][None], k_norm_swa[l][None], q_norm_mem[l][None], k_norm_mem[l][None]
    g1, g2 = norm1_g[l][None], norm2_g[l][None]

    xp = x_prompt.reshape(B * T, D_MODEL)
    proj_p = _in_proj(xp, g1, w_in_b, 1024).reshape(B, T, IN_PROJ)
    mk, mv = _memory_kv(mem_prompt, mem_norm_g[l][None], w_mem_kv[l].astype(BF16), mkg)
    yr_p, s_p = _rwkv_prompt(proj_p, *rwkv_params)
    ys_p, ym_p, kn_p = _attn_prompt(proj_p, mk, mv, rel_bias, sinks[l][None], sqg, skg, mqg)
    y_p = _out_ffn(xp, yr_p.reshape(B * T, RWKV_WIDTH), ys_p.reshape(B * T, SWA_WIDTH),
                   ym_p.reshape(B * T, MEM_WIDTH), w_out_b, g2, w1_b, w2_b, 512).reshape(B, T, D_MODEL)
    shift_p = proj_p[:, T - 1, :RWKV_PROJ]
    vb_p = proj_p[:, T - WINDOW:, COL_SV:COL_SV + SWA_KV_WIDTH]

    xs = x_sample.reshape(Bd, D_MODEL)
    proj_s = _in_proj(xs, g1, w_in_b, Bd)
    yr_s, st_s = _rwkv_step(proj_s[:, :RWKV_PROJ], state_shift[l], jnp.transpose(state_rwkv[l], (1, 2, 3, 0)),
                            *rwkv_params)
    s_s = jnp.transpose(st_s, (3, 0, 1, 2))
    fmajor = lambda c: jnp.transpose(c, (0, 2, 3, 1)).reshape(Bd, c.shape[2] * HEAD_DIM, c.shape[1])
    ys_s, ym_s, kb_s, vb_s = _decode_attn(
        proj_s, fmajor(cache_swa_k[l]), fmajor(cache_swa_v[l]), fmajor(cache_mem_k[l]), fmajor(cache_mem_v[l]),
        rel_bias, sinks[l][None], sqg, skg, mqg)
    pmajor = lambda c: jnp.transpose(c.reshape(Bd, N_SWA_KV_HEADS, HEAD_DIM, WINDOW), (0, 3, 1, 2))[None]
    y_s = _out_ffn(xs, yr_s, ys_s, ym_s, w_out_b, g2, w1_b, w2_b, Bd).reshape(Bd, 1, D_MODEL)

    return (y_p, y_s,
            s_p[None], shift_p[None],
            kn_p.reshape(1, B, WINDOW, N_SWA_KV_HEADS, HEAD_DIM),
            vb_p.reshape(1, B, WINDOW, N_SWA_KV_HEADS, HEAD_DIM),
            mk.reshape(1, B, N_MEM, N_MEM_HEADS, HEAD_DIM), mv.reshape(1, B, N_MEM, N_MEM_HEADS, HEAD_DIM),
            s_s[None], proj_s[:, :RWKV_PROJ][None],
            pmajor(kb_s), pmajor(vb_s))
```

```python
import math

import numpy as np
import jax
import jax.numpy as jnp
from jax import lax
from jax.experimental import pallas as pl
from jax.experimental.pallas import tpu as pltpu

F32 = jnp.float32
BF16 = jnp.bfloat16

D_MODEL = 1024
HEAD_DIM = 64
RWKV_WIDTH = 512
N_RWKV_HEADS = 8
SWA_WIDTH = 256
N_SWA_HEADS = 4
N_SWA_KV_HEADS = 2
SWA_KV_WIDTH = 128
MEM_WIDTH = 256
N_MEM_HEADS = 4
N_MEM = 256
WINDOW = 128
BLOCK = 128
N_BUCKETS = 32
MAX_DISTANCE = 128
DECAY_LORA = 64
AAA_LORA = 64
GATE_LORA = 128
RWKV_PROJ = 3 * RWKV_WIDTH + DECAY_LORA + AAA_LORA + GATE_LORA
SWA_PROJ = SWA_WIDTH + 2 * SWA_KV_WIDTH
IN_PROJ = RWKV_PROJ + SWA_PROJ + MEM_WIDTH
D_FF = 4 * D_MODEL
NORM_EPS = 1e-6
LNX_EPS = 64e-5
ATTN_SCALE = HEAD_DIM ** -0.5
EXP_M05 = math.exp(-0.5)
NEG = -1e30

COL_R, COL_K, COL_V = 0, RWKV_WIDTH, 2 * RWKV_WIDTH
COL_WD = 3 * RWKV_WIDTH
COL_AD = COL_WD + DECAY_LORA
COL_GD = COL_AD + AAA_LORA
COL_SQ = RWKV_PROJ
COL_SK = COL_SQ + SWA_WIDTH
COL_SV = COL_SK + SWA_KV_WIDTH
COL_MQ = RWKV_PROJ + SWA_PROJ

CHUNK = 64
VMEM_LIMIT = 56 * 1024 * 1024


def _dot(a, b):
    return jnp.dot(a.astype(BF16), b.astype(BF16), preferred_element_type=F32)


def _dot_nt(a, b):
    return lax.dot_general(a.astype(BF16), b.astype(BF16), (((1,), (1,)), ((), ())),
                           preferred_element_type=F32)


def _dot_tn(a, b):
    return lax.dot_general(a.astype(BF16), b.astype(BF16), (((0,), (0,)), ((), ())),
                           preferred_element_type=F32)


def _rms(x, g):
    return x * lax.rsqrt(jnp.mean(x * x, -1, keepdims=True) + NORM_EPS) * g


def _params(*sem):
    return pltpu.CompilerParams(dimension_semantics=sem, vmem_limit_bytes=VMEM_LIMIT)


def _full(shape):
    n = len(shape)
    return pl.BlockSpec(shape, lambda *_: (0,) * n)


def _head_blocks(width):
    bi = lax.broadcasted_iota(jnp.int32, (width, width), 0) // HEAD_DIM
    bj = lax.broadcasted_iota(jnp.int32, (width, width), 1) // HEAD_DIM
    return jnp.where(bi == bj, 1.0, 0.0).astype(BF16)


IN_PROJ_TILE = 1024


def _in_proj_kernel(x_ref, g_ref, w_ref, o_ref):
    h = _rms(x_ref[...], g_ref[...])
    o_ref[...] = jnp.dot(h.astype(BF16), w_ref[...], preferred_element_type=F32)


def _in_proj(x2d, g, w_bf16, tm):
    n = x2d.shape[0]
    return pl.pallas_call(
        _in_proj_kernel,
        grid=(n // tm,),
        in_specs=[pl.BlockSpec((tm, D_MODEL), lambda i: (i, 0)),
                  _full((1, D_MODEL)),
                  _full((D_MODEL, IN_PROJ))],
        out_specs=pl.BlockSpec((tm, IN_PROJ), lambda i: (i, 0)),
        out_shape=jax.ShapeDtypeStruct((n, IN_PROJ), F32),
        compiler_params=_params("arbitrary"),
        name="in_proj",
    )(x2d, g, w_bf16)


def _rwkv_features(xs, w0, a0, k_k, k_a, wupw, wupa, wupg):
    r = xs[:, COL_R:COL_R + RWKV_WIDTH]
    k = xs[:, COL_K:COL_K + RWKV_WIDTH]
    v = xs[:, COL_V:COL_V + RWKV_WIDTH]
    wd = xs[:, COL_WD:COL_WD + DECAY_LORA]
    ad = xs[:, COL_AD:COL_AD + AAA_LORA]
    gd = xs[:, COL_GD:COL_GD + GATE_LORA]
    logw = -jax.nn.sigmoid(w0 + _dot(jnp.tanh(wd), wupw)) * EXP_M05
    a_sig = jax.nn.sigmoid(a0 + _dot(ad, wupa))
    gate = _dot(jax.nn.sigmoid(gd), wupg)
    kk = k * k_k
    k2 = k * (1.0 + (a_sig - 1.0) * k_a)
    return r, k2, v, kk, a_sig, logw, gate


def _seg_sum(x, blk):
    xb = x.astype(BF16)
    half = RWKV_WIDTH // 2
    return jnp.concatenate([jnp.dot(xb[:, :half], blk, preferred_element_type=F32),
                            jnp.dot(xb[:, half:], blk, preferred_element_type=F32)], axis=1)


def _group_norm_out(y, bonus, gate, lnx_w, lnx_b, blk):
    inv_d = 1.0 / HEAD_DIM
    m = _seg_sum(y, blk) * inv_d
    d = y - m
    var = _seg_sum(d * d, blk) * inv_d
    yn = d * lax.rsqrt(var + LNX_EPS) * lnx_w + lnx_b
    return (yn + bonus) * gate


RWKV_TILE = 4 * CHUNK
PAIR = 2 * HEAD_DIM
N_PAIRS = N_RWKV_HEADS // 2


def _rwkv_prompt_kernel(p_ref, mu_ref, w0_ref, a0_ref, kk_ref, ka_ref, rk_ref, lnw_ref, lnb_ref,
                        wupw_ref, wupa_ref, wupg_ref, y_ref, sout_ref, s_scr, prev_scr):
    C, TT, D = CHUNK, RWKV_TILE, HEAD_DIM
    NC = TT // C
    t = pl.program_id(1)

    @pl.when(t == 0)
    def _():
        s_scr[...] = jnp.zeros_like(s_scr)
        prev_scr[...] = jnp.zeros_like(prev_scr)

    p = p_ref[0]
    row = lax.broadcasted_iota(jnp.int32, p.shape, 0)
    prev = jnp.where(row == 0, prev_scr[...], pltpu.roll(p, 1, axis=0))
    prev_scr[...] = p[TT - 1:TT, :]
    xs = p + (prev - p) * mu_ref[...]
    r, k2, v, kk, a_sig, logw, gate = _rwkv_features(
        xs, w0_ref[...], a0_ref[...], kk_ref[...], ka_ref[...], wupw_ref[...], wupa_ref[...], wupg_ref[...])

    blk = _head_blocks(RWKV_WIDTH // 2)
    kkn = kk / jnp.maximum(jnp.sqrt(_seg_sum(kk * kk, blk)), 1e-12)
    bb = kkn * a_sig

    ri = lax.broadcasted_iota(jnp.int32, (TT, TT), 0)
    ci = lax.broadcasted_iota(jnp.int32, (TT, TT), 1)
    tri = jnp.where(jnp.logical_and(ri >= ci, ri // C == ci // C), 1.0, 0.0).astype(BF16)
    l1 = logw.astype(BF16)
    rem = logw - l1.astype(F32)
    l2 = rem.astype(BF16)
    l3 = (rem - l2.astype(F32)).astype(BF16)
    cum = (jnp.dot(tri, l1, preferred_element_type=F32) + jnp.dot(tri, l2, preferred_element_type=F32)
           + jnp.dot(tri, l3, preferred_element_type=F32))
    c_last = jnp.concatenate([jnp.broadcast_to(cum[(c + 1) * C - 1:(c + 1) * C, :], (C, RWKV_WIDTH))
                              for c in range(NC)], axis=0)
    e_pos = jnp.exp(cum)
    e_neg = jnp.exp(-cum)
    e_prev = jnp.exp(cum - logw)
    e_last = jnp.exp(c_last - cum)

    lo_full = (lax.broadcasted_iota(jnp.int32, (TT, RWKV_WIDTH), 1) % PAIR) < D
    at_f = -kkn * e_prev
    rt_f = r * e_pos
    at_lo = jnp.where(lo_full, at_f, 0.0).astype(BF16)
    at_hi = jnp.where(lo_full, 0.0, at_f).astype(BF16)
    rt_lo = jnp.where(lo_full, rt_f, 0.0).astype(BF16)
    rt_hi = jnp.where(lo_full, 0.0, rt_f).astype(BF16)
    bt_b = (bb * e_neg).astype(BF16)
    kt_b = (k2 * e_neg).astype(BF16)
    bh_b = (bb * e_last).astype(BF16)
    kh_b = (k2 * e_last).astype(BF16)
    v_b = v.astype(BF16)

    r2 = lax.broadcasted_iota(jnp.int32, (C, PAIR), 0)
    c2 = lax.broadcasted_iota(jnp.int32, (C, PAIR), 1)
    c2m = jnp.where(c2 >= C, c2 - C, c2)
    mask_ak = jnp.logical_and(c2 >= C, c2m < r2)
    mask_r = c2m <= r2
    strict = lax.broadcasted_iota(jnp.int32, (C, C), 1) < lax.broadcasted_iota(jnp.int32, (C, C), 0)
    lo = c2 < D
    qi = lax.broadcasted_iota(jnp.int32, (PAIR, PAIR), 0) // D
    qj = lax.broadcasted_iota(jnp.int32, (PAIR, PAIR), 1) // D
    diag = qi == qj

    PR = [(c, q) for c in range(NC) for q in range(N_PAIRS)]
    n = len(PR)
    E = range(2)
    win = lambda x, c, q: x[c * C:(c + 1) * C, q * PAIR:(q + 1) * PAIR]
    sc = [_dot_nt(jnp.concatenate([win(at_lo, c, q), win(at_hi, c, q), win(rt_lo, c, q), win(rt_hi, c, q)], axis=0),
                  jnp.concatenate([win(bt_b, c, q), win(kt_b, c, q)], axis=0)) for c, q in PR]
    vr = [pltpu.roll(win(v, c, q), D, axis=1) for c, q in PR]
    vvr = [jnp.concatenate([vr[i], vr[i]], axis=0).astype(BF16) for i in range(n)]
    A = [[jnp.where(strict, sc[i][e * C:(e + 1) * C, :C], 0.0) for e in E] for i in range(n)]
    m_ak = [[jnp.where(mask_ak, sc[i][e * C:(e + 1) * C], 0.0) for e in E] for i in range(n)]
    m_r = [[jnp.where(mask_r, sc[i][(2 + e) * C:(3 + e) * C], 0.0) for e in E] for i in range(n)]

    zf = [[_dot(m_ak[i][e], vvr[i]) for e in E] for i in range(n)]
    X = [[jnp.where(lo, win(at_f, c, q), zf[i][0]), jnp.where(lo, zf[i][1], win(at_f, c, q))]
         for i, (c, q) in enumerate(PR)]
    for k in range(int(math.log2(C))):
        AW = [[_dot(A[i][e], jnp.concatenate([X[i][e], A[i][e]], axis=1)) for e in E] for i in range(n)]
        X = [[X[i][e] + AW[i][e][:, :PAIR] for e in E] for i in range(n)]
        A = [[AW[i][e][:, PAIR:] for e in E] for i in range(n)]

    S = [s_scr[q] for q in range(N_PAIRS)]
    ys = []
    for c in range(NC):
        w_last = jnp.exp(cum[(c + 1) * C - 1:(c + 1) * C, :])
        idx = [c * N_PAIRS + q for q in range(N_PAIRS)]
        mkg = [[_dot_tn(X[i][e], win(bh_b, c, q)) for e in E] for q, i in enumerate(idx)]
        vk = [_dot_tn(win(v_b, c, q), win(kh_b, c, q)) for q, i in enumerate(idx)]
        ry = [[_dot(m_r[i][0], jnp.concatenate([X[i][0], jnp.where(lo, 0.0, vr[i])], axis=0)),
               _dot(m_r[i][1], jnp.concatenate([X[i][1], jnp.where(lo, vr[i], 0.0)], axis=0))] for i in idx]
        rp = [win(rt_f, c, q) + jnp.where(lo, ry[q][0], ry[q][1]) for q in range(N_PAIRS)]
        y0 = [pltpu.roll(jnp.where(lo, ry[q][1], ry[q][0]), D, axis=1) for q in range(N_PAIRS)]
        mk = [jnp.where(diag, jnp.concatenate([mkg[q][0][:D], mkg[q][1][D:]], axis=0), 0.0) for q in range(N_PAIRS)]
        g = [jnp.where(diag, vk[q] + jnp.concatenate([mkg[q][0][D:], mkg[q][1][:D]], axis=0), 0.0)
             for q in range(N_PAIRS)]
        y = [_dot_nt(rp[q], S[q]) for q in range(N_PAIRS)]
        dS = [_dot(S[q], mk[q]) for q in range(N_PAIRS)]
        S = [S[q] * w_last[:, q * PAIR:(q + 1) * PAIR] + dS[q] + g[q] for q in range(N_PAIRS)]
        ys.append(jnp.concatenate([y[q] + y0[q] for q in range(N_PAIRS)], axis=1))
    for q in range(N_PAIRS):
        s_scr[q] = S[q]

    bonus = _seg_sum(r * k2 * rk_ref[...], blk) * v
    y_ref[0] = _group_norm_out(jnp.concatenate(ys, axis=0), bonus, gate, lnw_ref[...], lnb_ref[...], blk)

    @pl.when(t == pl.num_programs(1) - 1)
    def _():
        for q in range(N_PAIRS):
            sout_ref[0, 2 * q] = S[q][:D, :D]
            sout_ref[0, 2 * q + 1] = S[q][D:, D:]


def _rwkv_prompt(proj, mu, w0, a0, k_k, k_a, r_k, lnx_w, lnx_b, wupw, wupa, wupg):
    B, T, _ = proj.shape
    vec = lambda n: _full((1, n))
    return pl.pallas_call(
        _rwkv_prompt_kernel,
        grid=(B, T // RWKV_TILE),
        in_specs=[pl.BlockSpec((1, RWKV_TILE, RWKV_PROJ), lambda b, t: (b, t, 0)),
                  vec(RWKV_PROJ), vec(RWKV_WIDTH), vec(RWKV_WIDTH), vec(RWKV_WIDTH), vec(RWKV_WIDTH),
                  vec(RWKV_WIDTH), vec(RWKV_WIDTH), vec(RWKV_WIDTH),
                  _full((DECAY_LORA, RWKV_WIDTH)), _full((AAA_LORA, RWKV_WIDTH)), _full((GATE_LORA, RWKV_WIDTH))],
        out_specs=[pl.BlockSpec((1, RWKV_TILE, RWKV_WIDTH), lambda b, t: (b, t, 0)),
                   pl.BlockSpec((1, N_RWKV_HEADS, HEAD_DIM, HEAD_DIM), lambda b, t: (b, 0, 0, 0))],
        out_shape=[jax.ShapeDtypeStruct((B, T, RWKV_WIDTH), F32),
                   jax.ShapeDtypeStruct((B, N_RWKV_HEADS, HEAD_DIM, HEAD_DIM), F32)],
        scratch_shapes=[pltpu.VMEM((N_PAIRS, PAIR, PAIR), F32),
                        pltpu.VMEM((1, RWKV_PROJ), F32)],
        compiler_params=_params("arbitrary", "arbitrary"),
        name="rwkv_prompt",
    )(proj, mu, w0, a0, k_k, k_a, r_k, lnx_w, lnx_b, wupw, wupa, wupg)


def _rwkv_step_feat_kernel(p_ref, sh_ref, mu_ref, w0_ref, a0_ref, kk_ref, ka_ref, rk_ref,
                           wupw_ref, wupa_ref, wupg_ref, vecs_ref, bonus_ref, gate_ref):
    p = p_ref[...]
    xs = p + (sh_ref[...] - p) * mu_ref[...]
    r, k2, v, kk, a_sig, logw, gate = _rwkv_features(
        xs, w0_ref[...], a0_ref[...], kk_ref[...], ka_ref[...], wupw_ref[...], wupa_ref[...], wupg_ref[...])
    blk = _head_blocks(RWKV_WIDTH // 2)
    kkn = kk / jnp.maximum(jnp.sqrt(_seg_sum(kk * kk, blk)), 1e-12)
    for i, x in enumerate((-kkn, kkn * a_sig, jnp.exp(logw), k2, r, v)):
        vecs_ref[i] = x.T
    bonus_ref[...] = _seg_sum(r * k2 * rk_ref[...], blk) * v
    gate_ref[...] = gate


def _rwkv_step_state_kernel(s_ref, vecs_ref, sout_ref, y_ref):
    S = s_ref[0]
    a, b, w, k, r, v = (vecs_ref[i] for i in range(6))
    sa = jnp.sum(S * a[None], axis=1)
    S = S * w[None] + sa[:, None, :] * b[None] + v[:, None, :] * k[None]
    sout_ref[0] = S
    y_ref[...] = jnp.sum(S * r[None], axis=1)


def _rwkv_step_out_kernel(yt_ref, bonus_ref, gate_ref, lnw_ref, lnb_ref, o_ref):
    o_ref[...] = _group_norm_out(yt_ref[...].T, bonus_ref[...], gate_ref[...], lnw_ref[...], lnb_ref[...],
                                 _head_blocks(RWKV_WIDTH // 2))


def _rwkv_step(proj, shift, state_t, mu, w0, a0, k_k, k_a, r_k, lnx_w, lnx_b, wupw, wupa, wupg):
    n = proj.shape[0]
    vec = lambda m: _full((1, m))
    rows = _full((n, RWKV_PROJ))
    wide = _full((n, RWKV_WIDTH))
    vecs, bonus, gate = pl.pallas_call(
        _rwkv_step_feat_kernel,
        grid=(1,),
        in_specs=[rows, rows, vec(RWKV_PROJ), vec(RWKV_WIDTH), vec(RWKV_WIDTH), vec(RWKV_WIDTH), vec(RWKV_WIDTH),
                  vec(RWKV_WIDTH),
                  _full((DECAY_LORA, RWKV_WIDTH)), _full((AAA_LORA, RWKV_WIDTH)), _full((GATE_LORA, RWKV_WIDTH))],
        out_specs=[_full((6, RWKV_WIDTH, n)), wide, wide],
        out_shape=[jax.ShapeDtypeStruct((6, RWKV_WIDTH, n), F32),
                   jax.ShapeDtypeStruct((n, RWKV_WIDTH), F32), jax.ShapeDtypeStruct((n, RWKV_WIDTH), F32)],
        compiler_params=_params("arbitrary"),
        name="rwkv_step_feat",
    )(proj, shift, mu, w0, a0, k_k, k_a, r_k, wupw, wupa, wupg)
    st_spec = pl.BlockSpec((1, HEAD_DIM, HEAD_DIM, n), lambda h: (h, 0, 0, 0))
    state_new, yt = pl.pallas_call(
        _rwkv_step_state_kernel,
        grid=(N_RWKV_HEADS,),
        in_specs=[st_spec, pl.BlockSpec((6, HEAD_DIM, n), lambda h: (0, h, 0))],
        out_specs=[st_spec, pl.BlockSpec((HEAD_DIM, n), lambda h: (h, 0))],
        out_shape=[jax.ShapeDtypeStruct(state_t.shape, F32), jax.ShapeDtypeStruct((RWKV_WIDTH, n), F32)],
        compiler_params=_params("arbitrary"),
        name="rwkv_step_state",
    )(state_t, vecs)
    y = pl.pallas_call(
        _rwkv_step_out_kernel,
        grid=(1,),
        in_specs=[_full((RWKV_WIDTH, n)), wide, wide, vec(RWKV_WIDTH), vec(RWKV_WIDTH)],
        out_specs=wide,
        out_shape=jax.ShapeDtypeStruct((n, RWKV_WIDTH), F32),
        compiler_params=_params("arbitrary"),
        name="rwkv_step_out",
    )(yt, bonus, gate, lnx_w, lnx_b)
    return y, state_new


def _t5_bucket_np(dist):
    max_exact = N_BUCKETS // 2
    d = np.maximum(dist, 1).astype(np.float32)
    large = max_exact + (np.log(d / np.float32(max_exact)) / np.float32(math.log(MAX_DISTANCE / max_exact))
                         * np.float32(N_BUCKETS - max_exact)).astype(np.int32)
    large = np.minimum(large, N_BUCKETS - 1)
    return np.where(dist < max_exact, dist, large).astype(np.int32)


def _prompt_bucket_table():
    qi = np.arange(BLOCK)[:, None]
    kj = np.arange(2 * BLOCK)[None, :]
    dist = BLOCK + qi - kj
    valid = (dist >= 0) & (dist <= WINDOW)
    return np.where(valid, _t5_bucket_np(np.maximum(dist, 0)), -1).astype(np.int32)


def _decode_bucket_table():
    dist = WINDOW - np.arange(WINDOW)
    return np.broadcast_to(_t5_bucket_np(dist)[None, :], (8, WINDOW)).astype(np.int32).copy()


def _bias_from_buckets(bkt, relb_ref, h, init):
    acc = jnp.full(bkt.shape, init, F32)
    for b in range(N_BUCKETS):
        acc = jnp.where(bkt == b, relb_ref[b, h], acc)
    return acc


ATT_TILE = 4 * BLOCK


def _attn_prompt_kernel(qs_ref, qm_ref, kc_ref, kp_ref, vc_ref, vp_ref, mk_ref, mv_ref, bkt_ref, relb_ref, sink_ref,
                        sqg_ref, skg_ref, mqg_ref, ys_ref, ym_ref, kn_ref, bias_scr, mk_scr, mv_scr):
    i = pl.program_id(1)
    grp = N_SWA_HEADS // N_SWA_KV_HEADS
    ones = jnp.ones((2 * BLOCK, HEAD_DIM), F32)

    @pl.when(i == 0)
    def _():
        bkt = bkt_ref[...]
        for j in range(N_SWA_KV_HEADS):
            for g in range(grp):
                bias_scr[j, g * BLOCK:(g + 1) * BLOCK, :] = _bias_from_buckets(bkt, relb_ref, j * grp + g, NEG)
        mk = mk_ref[0]
        mv = mv_ref[0]
        for h in range(N_MEM_HEADS):
            sl = slice(h * HEAD_DIM, (h + 1) * HEAD_DIM)
            mk_scr[h] = mk[:, sl].astype(BF16)
            mv_scr[h] = jnp.concatenate([mv[:, sl], ones], axis=1).astype(BF16)

    qs = qs_ref[0]
    qm = qm_ref[0]
    kc = kc_ref[0]
    kp = kp_ref[0]
    vc = vc_ref[0]
    vp = vp_ref[0]
    sqg, skg, mqg = sqg_ref[...], skg_ref[...], mqg_ref[...]
    rowi = lax.broadcasted_iota(jnp.int32, (2 * BLOCK, 1), 0)
    col = lax.broadcasted_iota(jnp.int32, (2 * BLOCK, 2 * BLOCK), 1)
    pad_mask = jnp.logical_and(i == 0, col < BLOCK)

    def rms_heads(x, g, n_heads):
        ms = jnp.dot((x * x).astype(BF16), _head_blocks(n_heads * HEAD_DIM), preferred_element_type=F32)
        return x * lax.rsqrt(ms * (1.0 / HEAD_DIM) + NORM_EPS) * jnp.concatenate([g] * n_heads, axis=1)

    hsl = [slice(h * HEAD_DIM, (h + 1) * HEAD_DIM) for h in range(N_SWA_HEADS)]
    qs_n = rms_heads(qs, sqg, N_SWA_HEADS) * ATTN_SCALE
    qm_n = rms_heads(qm, mqg, N_MEM_HEADS) * ATTN_SCALE
    kc_n = rms_heads(kc, skg, N_SWA_KV_HEADS)
    kp_n = rms_heads(kp, skg, N_SWA_KV_HEADS)
    kn_ref[0] = kc_n[ATT_TILE - BLOCK:]
    chains = [(a, j) for a in range(ATT_TILE // BLOCK) for j in range(N_SWA_KV_HEADS)]
    lhs, keys, vals, sinkcol = [], [], [], []
    for a, j in chains:
        rs = slice(a * BLOCK, (a + 1) * BLOCK)
        lhs.append(jnp.concatenate([qs_n[rs, hsl[j * grp + g]] for g in range(grp)], axis=0))
        if a == 0:
            keys.append(jnp.concatenate([kp_n[:, hsl[j]], kc_n[:BLOCK, hsl[j]]], axis=0))
            vv = jnp.concatenate([vp[:, hsl[j]], vc[:BLOCK, hsl[j]]], axis=0)
        else:
            ks = slice((a - 1) * BLOCK, (a + 1) * BLOCK)
            keys.append(kc_n[ks, hsl[j]])
            vv = vc[ks, hsl[j]]
        vals.append(jnp.concatenate([vv, ones], axis=1))
        sinkcol.append(jnp.where(rowi < BLOCK, sink_ref[0, j * grp], sink_ref[0, j * grp + 1]))
    qmn = [qm_n[:, hsl[h]] for h in range(N_MEM_HEADS)]

    s_w = [_dot_nt(lhs[c], keys[c]) for c in range(len(chains))]
    s_m = [_dot_nt(qmn[h], mk_scr[h]) for h in range(N_MEM_HEADS)]
    e_w, m_w = [], []
    for c, (a, j) in enumerate(chains):
        s = s_w[c] + bias_scr[j]
        if a == 0:
            s = jnp.where(pad_mask, NEG, s)
        m = jnp.maximum(jnp.max(s, -1, keepdims=True), sinkcol[c])
        m_w.append(m)
        e_w.append(jnp.exp(s - m))
    e_m = [jnp.exp(s_m[h] - jnp.max(s_m[h], -1, keepdims=True)) for h in range(N_MEM_HEADS)]
    o_w = [_dot(e_w[c], vals[c]) for c in range(len(chains))]
    o_m = [_dot(e_m[h], mv_scr[h]) for h in range(N_MEM_HEADS)]
    for c, (a, j) in enumerate(chains):
        den = o_w[c][:, HEAD_DIM:HEAD_DIM + 1] + jnp.exp(sinkcol[c] - m_w[c])
        o = o_w[c][:, :HEAD_DIM] / den
        for g in range(grp):
            ys_ref[0, a * BLOCK:(a + 1) * BLOCK, hsl[j * grp + g]] = o[g * BLOCK:(g + 1) * BLOCK]
    for h in range(N_MEM_HEADS):
        ym_ref[0, :, hsl[h]] = o_m[h][:, :HEAD_DIM] / o_m[h][:, HEAD_DIM:HEAD_DIM + 1]


def _attn_prompt(proj, mk, mv, rel_bias, sinks, sqg, skg, mqg):
    B, T, _ = proj.shape
    bkt = jnp.asarray(_prompt_bucket_table())
    smem = pl.BlockSpec(memory_space=pltpu.SMEM)
    kblk, vblk = COL_SK // SWA_KV_WIDTH, COL_SV // SWA_KV_WIDTH
    prev = lambda b, i: (b, jnp.maximum((ATT_TILE // BLOCK) * i - 1, 0))
    memb = pl.BlockSpec((1, N_MEM, MEM_WIDTH), lambda b, i: (b, 0, 0))
    return pl.pallas_call(
        _attn_prompt_kernel,
        grid=(B, T // ATT_TILE),
        in_specs=[pl.BlockSpec((1, ATT_TILE, SWA_WIDTH), lambda b, i: (b, i, COL_SQ // SWA_WIDTH)),
                  pl.BlockSpec((1, ATT_TILE, MEM_WIDTH), lambda b, i: (b, i, COL_MQ // MEM_WIDTH)),
                  pl.BlockSpec((1, ATT_TILE, SWA_KV_WIDTH), lambda b, i: (b, i, kblk)),
                  pl.BlockSpec((1, BLOCK, SWA_KV_WIDTH), lambda b, i: prev(b, i) + (kblk,)),
                  pl.BlockSpec((1, ATT_TILE, SWA_KV_WIDTH), lambda b, i: (b, i, vblk)),
                  pl.BlockSpec((1, BLOCK, SWA_KV_WIDTH), lambda b, i: prev(b, i) + (vblk,)),
                  memb, memb,
                  _full((BLOCK, 2 * BLOCK)), smem, smem,
                  _full((1, HEAD_DIM)), _full((1, HEAD_DIM)), _full((1, HEAD_DIM))],
        out_specs=[pl.BlockSpec((1, ATT_TILE, SWA_WIDTH), lambda b, i: (b, i, 0)),
                   pl.BlockSpec((1, ATT_TILE, MEM_WIDTH), lambda b, i: (b, i, 0)),
                   pl.BlockSpec((1, BLOCK, SWA_KV_WIDTH), lambda b, i: (b, 0, 0))],
        out_shape=[jax.ShapeDtypeStruct((B, T, SWA_WIDTH), F32),
                   jax.ShapeDtypeStruct((B, T, MEM_WIDTH), F32),
                   jax.ShapeDtypeStruct((B, BLOCK, SWA_KV_WIDTH), F32)],
        scratch_shapes=[pltpu.VMEM((N_SWA_KV_HEADS, 2 * BLOCK, 2 * BLOCK), F32),
                        pltpu.VMEM((N_MEM_HEADS, N_MEM, HEAD_DIM), BF16),
                        pltpu.VMEM((N_MEM_HEADS, N_MEM, 2 * HEAD_DIM), BF16)],
        compiler_params=_params("arbitrary", "arbitrary"),
        name="attn_prompt",
    )(proj, proj, proj, proj, proj, proj, mk, mv, bkt, rel_bias, sinks, sqg, skg, mqg)


def _memory_kv_kernel(mem_ref, g_ref, w_ref, kg_ref, mk_ref, mv_ref):
    kv = jnp.dot(_rms(mem_ref[0], g_ref[...]).astype(BF16), w_ref[...], preferred_element_type=F32)
    kg = kg_ref[...]
    for h in range(N_MEM_HEADS):
        sl = slice(h * HEAD_DIM, (h + 1) * HEAD_DIM)
        mk_ref[0, :, sl] = _rms(kv[:, sl], kg)
    mv_ref[0] = kv[:, MEM_WIDTH:]


def _memory_kv(mem, g, w_bf16, kg):
    B = mem.shape[0]
    blk = pl.BlockSpec((1, N_MEM, MEM_WIDTH), lambda b: (b, 0, 0))
    return pl.pallas_call(
        _memory_kv_kernel,
        grid=(B,),
        in_specs=[pl.BlockSpec((1, N_MEM, D_MODEL), lambda b: (b, 0, 0)),
                  _full((1, D_MODEL)), _full((D_MODEL, 2 * MEM_WIDTH)), _full((1, HEAD_DIM))],
        out_specs=[blk, blk],
        out_shape=[jax.ShapeDtypeStruct((B, N_MEM, MEM_WIDTH), F32)] * 2,
        compiler_params=_params("arbitrary"),
        name="memory_kv",
    )(mem, g, w_bf16, kg)


SEQ_TILE = 8


def _head_rows(x_row, n_heads):
    return jnp.concatenate([x_row[:, h * HEAD_DIM:(h + 1) * HEAD_DIM] for h in range(n_heads)], axis=0)


def _spread(xh, n_groups, rows_per_group):
    tiled = jnp.concatenate([xh] * n_groups, axis=1)
    rowi = lax.broadcasted_iota(jnp.int32, tiled.shape, 0)
    lane_grp = lax.broadcasted_iota(jnp.int32, tiled.shape, 1) // HEAD_DIM
    return jnp.where(lane_grp == rowi // rows_per_group, tiled, 0.0)


def _gather_groups(full, n_groups, rows_per_group):
    n = full.shape[0]
    rowi = lax.broadcasted_iota(jnp.int32, (n, HEAD_DIM), 0)
    out = jnp.zeros((n, HEAD_DIM), F32)
    for g in range(n_groups):
        out = out + jnp.where(rowi // rows_per_group == g, full[:, g * HEAD_DIM:(g + 1) * HEAD_DIM], 0.0)
    return out


def _decode_attn_kernel(p_ref, kbuf_ref, vbuf_ref, mk_ref, mv_ref, bkt_ref, relb_ref, sink_ref,
                        sqg_ref, skg_ref, mqg_ref, ys_ref, ym_ref, kout_ref, vout_ref):
    grp = N_SWA_HEADS // N_SWA_KV_HEADS
    hrow = lax.broadcasted_iota(jnp.int32, (8, WINDOW), 0)
    bias_w = jnp.zeros((8, WINDOW), F32)
    for h in range(N_SWA_HEADS):
        bias_w = jnp.where(hrow == h, _bias_from_buckets(bkt_ref[...], relb_ref, h, 0.0), bias_w)
    bias_w = bias_w[:N_SWA_HEADS]
    hcol = lax.broadcasted_iota(jnp.int32, (N_SWA_HEADS, 1), 0)
    bias_new = jnp.zeros((N_SWA_HEADS, 1), F32)
    sink = jnp.zeros((N_SWA_HEADS, 1), F32)
    for h in range(N_SWA_HEADS):
        bias_new = jnp.where(hcol == h, relb_ref[0, h], bias_new)
        sink = jnp.where(hcol == h, sink_ref[0, h], sink)
    rowi = lax.broadcasted_iota(jnp.int32, (SWA_KV_WIDTH, WINDOW), 0)
    lanei = lax.broadcasted_iota(jnp.int32, (SWA_KV_WIDTH, WINDOW), 1)
    eye = rowi == lanei

    B = range(SEQ_TILE)
    prow = [p_ref[b:b + 1, :] for b in B]
    q = [_rms(_head_rows(prow[b][:, COL_SQ:COL_SQ + SWA_WIDTH], N_SWA_HEADS), sqg_ref[...]) * ATTN_SCALE for b in B]
    kn = [_rms(_head_rows(prow[b][:, COL_SK:COL_SK + SWA_KV_WIDTH], N_SWA_KV_HEADS), skg_ref[...]) for b in B]
    kn_row = [jnp.concatenate([kn[b][0:1], kn[b][1:2]], axis=1) for b in B]
    vn_row = [prow[b][:, COL_SV:COL_SV + SWA_KV_WIDTH] for b in B]
    qm = [_spread(q[b], N_SWA_KV_HEADS, grp) for b in B]
    qmem = [_spread(_rms(_head_rows(prow[b][:, COL_MQ:COL_MQ + MEM_WIDTH], N_MEM_HEADS), mqg_ref[...]) * ATTN_SCALE,
                    N_MEM_HEADS, 1) for b in B]
    s = [_dot(qm[b], kbuf_ref[b]) + bias_w for b in B]
    sm = [_dot(qmem[b], mk_ref[b]) for b in B]
    s_new = [jnp.sum(qm[b] * kn_row[b], -1, keepdims=True) + bias_new for b in B]
    m = [jnp.maximum(jnp.maximum(jnp.max(s[b], -1, keepdims=True), s_new[b]), sink) for b in B]
    e = [jnp.exp(s[b] - m[b]) for b in B]
    e_new = [jnp.exp(s_new[b] - m[b]) for b in B]
    den = [jnp.sum(e[b], -1, keepdims=True) + e_new[b] + jnp.exp(sink - m[b]) for b in B]
    em = [jnp.exp(sm[b] - jnp.max(sm[b], -1, keepdims=True)) for b in B]
    ov = [_dot_nt(e[b], vbuf_ref[b]) for b in B]
    omf = [_dot_nt(em[b], mv_ref[b]) for b in B]
    kn_col = [jnp.sum(jnp.where(eye, kn_row[b], 0.0), -1, keepdims=True) for b in B]
    vn_col = [jnp.sum(jnp.where(eye, vn_row[b], 0.0), -1, keepdims=True) for b in B]
    for b in B:
        o = _gather_groups((ov[b] + e_new[b] * vn_row[b]) / den[b], N_SWA_KV_HEADS, grp)
        om = _gather_groups(omf[b], N_MEM_HEADS, 1) / jnp.sum(em[b], -1, keepdims=True)
        for h in range(N_SWA_HEADS):
            ys_ref[b:b + 1, h * HEAD_DIM:(h + 1) * HEAD_DIM] = o[h:h + 1]
        for h in range(N_MEM_HEADS):
            ym_ref[b:b + 1, h * HEAD_DIM:(h + 1) * HEAD_DIM] = om[h:h + 1]
        kout_ref[b] = jnp.where(lanei == WINDOW - 1, kn_col[b], pltpu.roll(kbuf_ref[b], WINDOW - 1, axis=1))
        vout_ref[b] = jnp.where(lanei == WINDOW - 1, vn_col[b], pltpu.roll(vbuf_ref[b], WINDOW - 1, axis=1))


def _decode_attn(proj, kbuf, vbuf, mk, mv, rel_bias, sinks, sqg, skg, mqg):
    n = proj.shape[0]
    bkt = jnp.asarray(_decode_bucket_table())
    smem = pl.BlockSpec(memory_space=pltpu.SMEM)
    win = pl.BlockSpec((SEQ_TILE, WINDOW, SWA_KV_WIDTH), lambda i: (i, 0, 0))
    memb = pl.BlockSpec((SEQ_TILE, N_MEM, MEM_WIDTH), lambda i: (i, 0, 0))
    return pl.pallas_call(
        _decode_attn_kernel,
        grid=(n // SEQ_TILE,),
        in_specs=[pl.BlockSpec((SEQ_TILE, IN_PROJ), lambda i: (i, 0)), win, win, memb, memb,
                  _full((8, WINDOW)), smem, smem, _full((1, HEAD_DIM)), _full((1, HEAD_DIM)), _full((1, HEAD_DIM))],
        out_specs=[pl.BlockSpec((SEQ_TILE, SWA_WIDTH), lambda i: (i, 0)),
                   pl.BlockSpec((SEQ_TILE, MEM_WIDTH), lambda i: (i, 0)), win, win],
        out_shape=[jax.ShapeDtypeStruct((n, SWA_WIDTH), F32), jax.ShapeDtypeStruct((n, MEM_WIDTH), F32),
                   jax.ShapeDtypeStruct(kbuf.shape, F32), jax.ShapeDtypeStruct(vbuf.shape, F32)],
        compiler_params=_params("arbitrary"),
        name="decode_attn",
    )(proj, kbuf, vbuf, mk, mv, bkt, rel_bias, sinks, sqg, skg, mqg)


FF_CHUNK = 1024
OUT_FFN_TILE = 512


def _out_ffn_kernel(x_ref, yr_ref, ys_ref, ym_ref, wo_ref, g2_ref, w1_ref, w2_ref, o_ref):
    x1 = (x_ref[...]
          + jnp.dot(yr_ref[...].astype(BF16), wo_ref[0:RWKV_WIDTH, :], preferred_element_type=F32)
          + jnp.dot(ys_ref[...].astype(BF16), wo_ref[RWKV_WIDTH:RWKV_WIDTH + SWA_WIDTH, :],
                    preferred_element_type=F32)
          + jnp.dot(ym_ref[...].astype(BF16), wo_ref[RWKV_WIDTH + SWA_WIDTH:, :], preferred_element_type=F32))
    h2 = _rms(x1, g2_ref[...]).astype(BF16)
    ff = None
    for c in range(D_FF // FF_CHUNK):
        u = jnp.dot(h2, w1_ref[:, c * FF_CHUNK:(c + 1) * FF_CHUNK], preferred_element_type=F32)
        u = jnp.square(jnp.maximum(u, 0.0)).astype(BF16)
        d = jnp.dot(u, w2_ref[c * FF_CHUNK:(c + 1) * FF_CHUNK, :], preferred_element_type=F32)
        ff = d if ff is None else ff + d
    o_ref[...] = x1 + ff


def _out_ffn(x2d, yr, ys, ym, wo, g2, w1, w2, tm):
    n = x2d.shape[0]
    rows = lambda w: pl.BlockSpec((tm, w), lambda i: (i, 0))
    const = lambda shape: pl.BlockSpec(shape, lambda i: (0, 0), pipeline_mode=pl.Buffered(1))
    return pl.pallas_call(
        _out_ffn_kernel,
        grid=(n // tm,),
        in_specs=[rows(D_MODEL), rows(RWKV_WIDTH), rows(SWA_WIDTH), rows(MEM_WIDTH),
                  const((D_MODEL, D_MODEL)), _full((1, D_MODEL)), const((D_MODEL, D_FF)), const((D_FF, D_MODEL))],
        out_specs=rows(D_MODEL),
        out_shape=jax.ShapeDtypeStruct((n, D_MODEL), F32),
        compiler_params=_params("arbitrary"),
        name="out_ffn",
    )(x2d, yr, ys, ym, wo, g2, w1, w2)


def kernel(x_prompt, x_sample, state_rwkv, state_shift, cache_swa_k, cache_swa_v, cache_mem_k, cache_mem_v,
           mem_prompt, rel_bias, norm1_g, w_in, mu_shift, w0, w_up_w, a0, w_up_a, w_up_g, k_k, k_a, r_k,
           lnx_w, lnx_b, q_norm_swa, k_norm_swa, sinks, mem_norm_g, w_mem_kv, q_norm_mem, k_norm_mem,
           w_out, norm2_g, w_ff1, w_ff2):
    B, T, _ = x_prompt.shape
    Bd = x_sample.shape[0]
    l = 0
    w_in_b = w_in[l].astype(BF16)
    w_out_b = w_out[l].astype(BF16)
    w1_b = w_ff1[l].astype(BF16)
    w2_b = w_ff2[l].astype(BF16)
    rwkv_params = (mu_shift[l][None], w0[l][None], a0[l][None], k_k[l][None], k_a[l][None],
                   r_k[l].reshape(1, RWKV_WIDTH), lnx_w[l][None], lnx_b[l][None],
                   w_up_w[l].astype(BF16), w_up_a[l].astype(BF16), w_up_g[l].astype(BF16))
    sqg, skg, mqg, mkg = q_norm_swa[l][None], k_norm_swa[l][None], q_norm_mem[l][None], k_norm_mem[l][None]
    g1, g2 = norm1_g[l][None], norm2_g[l][None]

    xp = x_prompt.reshape(B * T, D_MODEL)
    proj_p = _in_proj(xp, g1, w_in_b, IN_PROJ_TILE).reshape(B, T, IN_PROJ)
    mk, mv = _memory_kv(mem_prompt, mem_norm_g[l][None], w_mem_kv[l].astype(BF16), mkg)
    yr_p, s_p = _rwkv_prompt(proj_p, *rwkv_params)
    ys_p, ym_p, kn_p = _attn_prompt(proj_p, mk, mv, rel_bias, sinks[l][None], sqg, skg, mqg)
    y_p = _out_ffn(xp, yr_p.reshape(B * T, RWKV_WIDTH), ys_p.reshape(B * T, SWA_WIDTH),
                   ym_p.reshape(B * T, MEM_WIDTH), w_out_b, g2, w1_b, w2_b, OUT_FFN_TILE).reshape(B, T, D_MODEL)
    shift_p = proj_p[:, T - 1, :RWKV_PROJ]
    vb_p = proj_p[:, T - WINDOW:, COL_SV:COL_SV + SWA_KV_WIDTH]

    xs = x_sample.reshape(Bd, D_MODEL)
    proj_s = _in_proj(xs, g1, w_in_b, Bd)
    yr_s, st_s = _rwkv_step(proj_s[:, :RWKV_PROJ], state_shift[l], jnp.transpose(state_rwkv[l], (1, 2, 3, 0)),
                            *rwkv_params)
    s_s = jnp.transpose(st_s, (3, 0, 1, 2))
    fmajor = lambda c: jnp.transpose(c, (0, 2, 3, 1)).reshape(Bd, c.shape[2] * HEAD_DIM, c.shape[1])
    ys_s, ym_s, kb_s, vb_s = _decode_attn(
        proj_s, fmajor(cache_swa_k[l]), fmajor(cache_swa_v[l]), fmajor(cache_mem_k[l]), fmajor(cache_mem_v[l]),
        rel_bias, sinks[l][None], sqg, skg, mqg)
    pmajor = lambda c: jnp.transpose(c.reshape(Bd, N_SWA_KV_HEADS, HEAD_DIM, WINDOW), (0, 3, 1, 2))[None]
    y_s = _out_ffn(xs, yr_s, ys_s, ym_s, w_out_b, g2, w1_b, w2_b, Bd).reshape(Bd, 1, D_MODEL)

    return (y_p, y_s,
            s_p[None], shift_p[None],
            kn_p.reshape(1, B, WINDOW, N_SWA_KV_HEADS, HEAD_DIM),
            vb_p.reshape(1, B, WINDOW, N_SWA_KV_HEADS, HEAD_DIM),
            mk.reshape(1, B, N_MEM, N_MEM_HEADS, HEAD_DIM), mv.reshape(1, B, N_MEM, N_MEM_HEADS, HEAD_DIM),
            s_s[None], proj_s[:, :RWKV_PROJ][None],
            pmajor(kb_s), pmajor(vb_s))
```

```python
import math

import numpy as np
import jax
import jax.numpy as jnp
from jax import lax
from jax.experimental import pallas as pl
from jax.experimental.pallas import tpu as pltpu

F32 = jnp.float32
BF16 = jnp.bfloat16

D_MODEL = 1024
HEAD_DIM = 64
RWKV_WIDTH = 512
N_RWKV_HEADS = 8
SWA_WIDTH = 256
N_SWA_HEADS = 4
N_SWA_KV_HEADS = 2
SWA_KV_WIDTH = 128
MEM_WIDTH = 256
N_MEM_HEADS = 4
N_MEM = 256
WINDOW = 128
BLOCK = 128
N_BUCKETS = 32
MAX_DISTANCE = 128
DECAY_LORA = 64
AAA_LORA = 64
GATE_LORA = 128
RWKV_PROJ = 3 * RWKV_WIDTH + DECAY_LORA + AAA_LORA + GATE_LORA
SWA_PROJ = SWA_WIDTH + 2 * SWA_KV_WIDTH
IN_PROJ = RWKV_PROJ + SWA_PROJ + MEM_WIDTH
D_FF = 4 * D_MODEL
NORM_EPS = 1e-6
LNX_EPS = 64e-5
ATTN_SCALE = HEAD_DIM ** -0.5
EXP_M05 = math.exp(-0.5)
NEG = -1e30

COL_R, COL_K, COL_V = 0, RWKV_WIDTH, 2 * RWKV_WIDTH
COL_WD = 3 * RWKV_WIDTH
COL_AD = COL_WD + DECAY_LORA
COL_GD = COL_AD + AAA_LORA
COL_SQ = RWKV_PROJ
COL_SK = COL_SQ + SWA_WIDTH
COL_SV = COL_SK + SWA_KV_WIDTH
COL_MQ = RWKV_PROJ + SWA_PROJ

CHUNK = 64
VMEM_LIMIT = 56 * 1024 * 1024


def _dot(a, b):
    return jnp.dot(a.astype(BF16), b.astype(BF16), preferred_element_type=F32)


def _dot_nt(a, b):
    return lax.dot_general(a.astype(BF16), b.astype(BF16), (((1,), (1,)), ((), ())),
                           preferred_element_type=F32)


def _dot_tn(a, b):
    return lax.dot_general(a.astype(BF16), b.astype(BF16), (((0,), (0,)), ((), ())),
                           preferred_element_type=F32)


def _rms(x, g):
    return x * lax.rsqrt(jnp.mean(x * x, -1, keepdims=True) + NORM_EPS) * g


def _params(*sem):
    return pltpu.CompilerParams(dimension_semantics=sem, vmem_limit_bytes=VMEM_LIMIT)


def _full(shape):
    n = len(shape)
    return pl.BlockSpec(shape, lambda *_: (0,) * n)


def _head_blocks(width):
    bi = lax.broadcasted_iota(jnp.int32, (width, width), 0) // HEAD_DIM
    bj = lax.broadcasted_iota(jnp.int32, (width, width), 1) // HEAD_DIM
    return jnp.where(bi == bj, 1.0, 0.0).astype(BF16)


IN_PROJ_TILE = 1024
IN_PROJ_SUB = 256


def _in_proj_kernel(x_ref, g_ref, w_ref, o_ref):
    tm = x_ref.shape[0]
    sub = min(tm, IN_PROJ_SUB)
    for j in range(tm // sub):
        rows = slice(j * sub, (j + 1) * sub)
        h = _rms(x_ref[rows, :], g_ref[...])
        o_ref[rows, :] = jnp.dot(h.astype(BF16), w_ref[...], preferred_element_type=F32)


def _in_proj(x2d, g, w_bf16, tm):
    n = x2d.shape[0]
    return pl.pallas_call(
        _in_proj_kernel,
        grid=(n // tm,),
        in_specs=[pl.BlockSpec((tm, D_MODEL), lambda i: (i, 0)),
                  _full((1, D_MODEL)),
                  _full((D_MODEL, IN_PROJ))],
        out_specs=pl.BlockSpec((tm, IN_PROJ), lambda i: (i, 0)),
        out_shape=jax.ShapeDtypeStruct((n, IN_PROJ), F32),
        compiler_params=_params("arbitrary"),
        name="in_proj",
    )(x2d, g, w_bf16)


def _rwkv_features(xs, w0, a0, k_k, k_a, wupw, wupa, wupg):
    r = xs[:, COL_R:COL_R + RWKV_WIDTH]
    k = xs[:, COL_K:COL_K + RWKV_WIDTH]
    v = xs[:, COL_V:COL_V + RWKV_WIDTH]
    wd = xs[:, COL_WD:COL_WD + DECAY_LORA]
    ad = xs[:, COL_AD:COL_AD + AAA_LORA]
    gd = xs[:, COL_GD:COL_GD + GATE_LORA]
    logw = -jax.nn.sigmoid(w0 + _dot(jnp.tanh(wd), wupw)) * EXP_M05
    a_sig = jax.nn.sigmoid(a0 + _dot(ad, wupa))
    gate = _dot(jax.nn.sigmoid(gd), wupg)
    kk = k * k_k
    k2 = k * (1.0 + (a_sig - 1.0) * k_a)
    return r, k2, v, kk, a_sig, logw, gate


def _seg_sum(x, blk):
    xb = x.astype(BF16)
    half = RWKV_WIDTH // 2
    return jnp.concatenate([jnp.dot(xb[:, :half], blk, preferred_element_type=F32),
                            jnp.dot(xb[:, half:], blk, preferred_element_type=F32)], axis=1)


def _group_norm_out(y, bonus, gate, lnx_w, lnx_b, blk):
    inv_d = 1.0 / HEAD_DIM
    m = _seg_sum(y, blk) * inv_d
    d = y - m
    var = _seg_sum(d * d, blk) * inv_d
    yn = d * lax.rsqrt(var + LNX_EPS) * lnx_w + lnx_b
    return (yn + bonus) * gate


RWKV_TILE = 4 * CHUNK
PAIR = 2 * HEAD_DIM
N_PAIRS = N_RWKV_HEADS // 2


def _rwkv_prompt_kernel(p_ref, mu_ref, w0_ref, a0_ref, kk_ref, ka_ref, rk_ref, lnw_ref, lnb_ref,
                        wupw_ref, wupa_ref, wupg_ref, y_ref, sout_ref, s_scr, prev_scr):
    C, TT, D = CHUNK, RWKV_TILE, HEAD_DIM
    NC = TT // C
    t = pl.program_id(1)

    @pl.when(t == 0)
    def _():
        s_scr[...] = jnp.zeros_like(s_scr)
        prev_scr[...] = jnp.zeros_like(prev_scr)

    p = p_ref[0]
    row = lax.broadcasted_iota(jnp.int32, p.shape, 0)
    prev = jnp.where(row == 0, prev_scr[...], pltpu.roll(p, 1, axis=0))
    prev_scr[...] = p[TT - 1:TT, :]
    xs = p + (prev - p) * mu_ref[...]
    r, k2, v, kk, a_sig, logw, gate = _rwkv_features(
        xs, w0_ref[...], a0_ref[...], kk_ref[...], ka_ref[...], wupw_ref[...], wupa_ref[...], wupg_ref[...])

    blk = _head_blocks(RWKV_WIDTH // 2)
    kkn = kk / jnp.maximum(jnp.sqrt(_seg_sum(kk * kk, blk)), 1e-12)
    bb = kkn * a_sig

    ri = lax.broadcasted_iota(jnp.int32, (TT, TT), 0)
    ci = lax.broadcasted_iota(jnp.int32, (TT, TT), 1)
    tri = jnp.where(jnp.logical_and(ri >= ci, ri // C == ci // C), 1.0, 0.0).astype(BF16)
    l1 = logw.astype(BF16)
    rem = logw - l1.astype(F32)
    l2 = rem.astype(BF16)
    l3 = (rem - l2.astype(F32)).astype(BF16)
    cum = (jnp.dot(tri, l1, preferred_element_type=F32) + jnp.dot(tri, l2, preferred_element_type=F32)
           + jnp.dot(tri, l3, preferred_element_type=F32))
    c_last = jnp.concatenate([jnp.broadcast_to(cum[(c + 1) * C - 1:(c + 1) * C, :], (C, RWKV_WIDTH))
                              for c in range(NC)], axis=0)
    e_pos = jnp.exp(cum)
    e_neg = jnp.exp(-cum)
    e_prev = jnp.exp(cum - logw)
    e_last = jnp.exp(c_last - cum)

    lo_full = (lax.broadcasted_iota(jnp.int32, (TT, RWKV_WIDTH), 1) % PAIR) < D
    at_f = -kkn * e_prev
    rt_f = r * e_pos
    at_lo = jnp.where(lo_full, at_f, 0.0).astype(BF16)
    at_hi = jnp.where(lo_full, 0.0, at_f).astype(BF16)
    rt_lo = jnp.where(lo_full, rt_f, 0.0).astype(BF16)
    rt_hi = jnp.where(lo_full, 0.0, rt_f).astype(BF16)
    bt_b = (bb * e_neg).astype(BF16)
    kt_b = (k2 * e_neg).astype(BF16)
    bh_b = (bb * e_last).astype(BF16)
    kh_b = (k2 * e_last).astype(BF16)
    v_b = v.astype(BF16)

    r2 = lax.broadcasted_iota(jnp.int32, (C, PAIR), 0)
    c2 = lax.broadcasted_iota(jnp.int32, (C, PAIR), 1)
    lo = c2 < D
    c2m = jnp.where(lo, c2, c2 - C)
    mask_a = jnp.logical_and(lo, c2 < r2)
    mask_ak = jnp.logical_and(jnp.logical_not(lo), c2m < r2)
    mask_r = c2m <= r2
    eye_hi = jnp.where(jnp.logical_and(jnp.logical_not(lo), c2m == r2), 1.0, 0.0)
    zeros_cp = jnp.zeros((C, PAIR), F32)
    qi = lax.broadcasted_iota(jnp.int32, (PAIR, PAIR), 0) // D
    qj = lax.broadcasted_iota(jnp.int32, (PAIR, PAIR), 1) // D
    diag = qi == qj

    PR = [(c, q) for c in range(NC) for q in range(N_PAIRS)]
    n = len(PR)
    E = range(2)
    win = lambda x, c, q: x[c * C:(c + 1) * C, q * PAIR:(q + 1) * PAIR]
    sc = [_dot_nt(jnp.concatenate([win(at_lo, c, q), win(at_hi, c, q), win(rt_lo, c, q), win(rt_hi, c, q)], axis=0),
                  jnp.concatenate([win(bt_b, c, q), win(kt_b, c, q)], axis=0)) for c, q in PR]
    vr = [pltpu.roll(win(v, c, q), D, axis=1) for c, q in PR]
    vvr = [jnp.concatenate([vr[i], vr[i]], axis=0).astype(BF16) for i in range(n)]
    m_ak = [[jnp.where(mask_ak, sc[i][e * C:(e + 1) * C], 0.0) for e in E] for i in range(n)]
    m_r = [[jnp.where(mask_r, sc[i][(2 + e) * C:(3 + e) * C], 0.0) for e in E] for i in range(n)]

    zf = [[_dot(m_ak[i][e], vvr[i]) for e in E] for i in range(n)]
    W = [[jnp.where(mask_a, sc[i][e * C:(e + 1) * C], eye_hi) for e in E] for i in range(n)]
    for k in range(int(math.log2(C))):
        Wb = [[W[i][e].astype(BF16) for e in E] for i in range(n)]
        AW = [[jnp.dot(Wb[i][e][:, :C], Wb[i][e], preferred_element_type=F32) for e in E] for i in range(n)]
        W = [[jnp.where(lo, 0.0, W[i][e]) + AW[i][e] for e in E] for i in range(n)]
    X = [[_dot(W[i][0], jnp.concatenate([zeros_cp, jnp.where(lo, win(at_f, c, q), zf[i][0])], axis=0)),
          _dot(W[i][1], jnp.concatenate([zeros_cp, jnp.where(lo, zf[i][1], win(at_f, c, q))], axis=0))]
         for i, (c, q) in enumerate(PR)]

    S = [s_scr[q] for q in range(N_PAIRS)]
    ys = []
    for c in range(NC):
        w_last = jnp.exp(cum[(c + 1) * C - 1:(c + 1) * C, :])
        idx = [c * N_PAIRS + q for q in range(N_PAIRS)]
        mkg = [[_dot_tn(X[i][e], win(bh_b, c, q)) for e in E] for q, i in enumerate(idx)]
        vk = [_dot_tn(win(v_b, c, q), win(kh_b, c, q)) for q, i in enumerate(idx)]
        ry = [[_dot(m_r[i][0], jnp.concatenate([X[i][0], jnp.where(lo, 0.0, vr[i])], axis=0)),
               _dot(m_r[i][1], jnp.concatenate([X[i][1], jnp.where(lo, vr[i], 0.0)], axis=0))] for i in idx]
        rp = [win(rt_f, c, q) + jnp.where(lo, ry[q][0], ry[q][1]) for q in range(N_PAIRS)]
        y0 = [pltpu.roll(jnp.where(lo, ry[q][1], ry[q][0]), D, axis=1) for q in range(N_PAIRS)]
        mk = [jnp.where(diag, jnp.concatenate([mkg[q][0][:D], mkg[q][1][D:]], axis=0), 0.0) for q in range(N_PAIRS)]
        g = [jnp.where(diag, vk[q] + jnp.concatenate([mkg[q][0][D:], mkg[q][1][:D]], axis=0), 0.0)
             for q in range(N_PAIRS)]
        y = [_dot_nt(rp[q], S[q]) for q in range(N_PAIRS)]
        dS = [_dot(S[q], mk[q]) for q in range(N_PAIRS)]
        S = [S[q] * w_last[:, q * PAIR:(q + 1) * PAIR] + dS[q] + g[q] for q in range(N_PAIRS)]
        ys.append(jnp.concatenate([y[q] + y0[q] for q in range(N_PAIRS)], axis=1))
    for q in range(N_PAIRS):
        s_scr[q] = S[q]

    bonus = _seg_sum(r * k2 * rk_ref[...], blk) * v
    y_ref[0] = _group_norm_out(jnp.concatenate(ys, axis=0), bonus, gate, lnw_ref[...], lnb_ref[...], blk)

    @pl.when(t == pl.num_programs(1) - 1)
    def _():
        for q in range(N_PAIRS):
            sout_ref[0, 2 * q] = S[q][:D, :D]
            sout_ref[0, 2 * q + 1] = S[q][D:, D:]


def _rwkv_prompt(proj, mu, w0, a0, k_k, k_a, r_k, lnx_w, lnx_b, wupw, wupa, wupg):
    B, T, _ = proj.shape
    vec = lambda n: _full((1, n))
    return pl.pallas_call(
        _rwkv_prompt_kernel,
        grid=(B, T // RWKV_TILE),
        in_specs=[pl.BlockSpec((1, RWKV_TILE, RWKV_PROJ), lambda b, t: (b, t, 0)),
                  vec(RWKV_PROJ), vec(RWKV_WIDTH), vec(RWKV_WIDTH), vec(RWKV_WIDTH), vec(RWKV_WIDTH),
                  vec(RWKV_WIDTH), vec(RWKV_WIDTH), vec(RWKV_WIDTH),
                  _full((DECAY_LORA, RWKV_WIDTH)), _full((AAA_LORA, RWKV_WIDTH)), _full((GATE_LORA, RWKV_WIDTH))],
        out_specs=[pl.BlockSpec((1, RWKV_TILE, RWKV_WIDTH), lambda b, t: (b, t, 0)),
                   pl.BlockSpec((1, N_RWKV_HEADS, HEAD_DIM, HEAD_DIM), lambda b, t: (b, 0, 0, 0))],
        out_shape=[jax.ShapeDtypeStruct((B, T, RWKV_WIDTH), F32),
                   jax.ShapeDtypeStruct((B, N_RWKV_HEADS, HEAD_DIM, HEAD_DIM), F32)],
        scratch_shapes=[pltpu.VMEM((N_PAIRS, PAIR, PAIR), F32),
                        pltpu.VMEM((1, RWKV_PROJ), F32)],
        compiler_params=_params("arbitrary", "arbitrary"),
        name="rwkv_prompt",
    )(proj, mu, w0, a0, k_k, k_a, r_k, lnx_w, lnx_b, wupw, wupa, wupg)


def _rwkv_step_feat_kernel(p_ref, sh_ref, mu_ref, w0_ref, a0_ref, kk_ref, ka_ref, rk_ref,
                           wupw_ref, wupa_ref, wupg_ref, vecs_ref, bonus_ref, gate_ref):
    p = p_ref[...]
    xs = p + (sh_ref[...] - p) * mu_ref[...]
    r, k2, v, kk, a_sig, logw, gate = _rwkv_features(
        xs, w0_ref[...], a0_ref[...], kk_ref[...], ka_ref[...], wupw_ref[...], wupa_ref[...], wupg_ref[...])
    blk = _head_blocks(RWKV_WIDTH // 2)
    kkn = kk / jnp.maximum(jnp.sqrt(_seg_sum(kk * kk, blk)), 1e-12)
    for i, x in enumerate((-kkn, kkn * a_sig, jnp.exp(logw), k2, r, v)):
        vecs_ref[i] = x.T
    bonus_ref[...] = _seg_sum(r * k2 * rk_ref[...], blk) * v
    gate_ref[...] = gate


def _rwkv_step_state_kernel(s_ref, vecs_ref, sout_ref, y_ref):
    S = s_ref[0]
    a, b, w, k, r, v = (vecs_ref[i] for i in range(6))
    sa = jnp.sum(S * a[None], axis=1)
    S = S * w[None] + sa[:, None, :] * b[None] + v[:, None, :] * k[None]
    sout_ref[0] = S
    y_ref[...] = jnp.sum(S * r[None], axis=1)


def _rwkv_step_out_kernel(yt_ref, bonus_ref, gate_ref, lnw_ref, lnb_ref, o_ref):
    o_ref[...] = _group_norm_out(yt_ref[...].T, bonus_ref[...], gate_ref[...], lnw_ref[...], lnb_ref[...],
                                 _head_blocks(RWKV_WIDTH // 2))


def _rwkv_step(proj, shift, state_t, mu, w0, a0, k_k, k_a, r_k, lnx_w, lnx_b, wupw, wupa, wupg):
    n = proj.shape[0]
    vec = lambda m: _full((1, m))
    rows = _full((n, RWKV_PROJ))
    wide = _full((n, RWKV_WIDTH))
    vecs, bonus, gate = pl.pallas_call(
        _rwkv_step_feat_kernel,
        grid=(1,),
        in_specs=[rows, rows, vec(RWKV_PROJ), vec(RWKV_WIDTH), vec(RWKV_WIDTH), vec(RWKV_WIDTH), vec(RWKV_WIDTH),
                  vec(RWKV_WIDTH),
                  _full((DECAY_LORA, RWKV_WIDTH)), _full((AAA_LORA, RWKV_WIDTH)), _full((GATE_LORA, RWKV_WIDTH))],
        out_specs=[_full((6, RWKV_WIDTH, n)), wide, wide],
        out_shape=[jax.ShapeDtypeStruct((6, RWKV_WIDTH, n), F32),
                   jax.ShapeDtypeStruct((n, RWKV_WIDTH), F32), jax.ShapeDtypeStruct((n, RWKV_WIDTH), F32)],
        compiler_params=_params("arbitrary"),
        name="rwkv_step_feat",
    )(proj, shift, mu, w0, a0, k_k, k_a, r_k, wupw, wupa, wupg)
    st_spec = pl.BlockSpec((1, HEAD_DIM, HEAD_DIM, n), lambda h: (h, 0, 0, 0))
    state_new, yt = pl.pallas_call(
        _rwkv_step_state_kernel,
        grid=(N_RWKV_HEADS,),
        in_specs=[st_spec, pl.BlockSpec((6, HEAD_DIM, n), lambda h: (0, h, 0))],
        out_specs=[st_spec, pl.BlockSpec((HEAD_DIM, n), lambda h: (h, 0))],
        out_shape=[jax.ShapeDtypeStruct(state_t.shape, F32), jax.ShapeDtypeStruct((RWKV_WIDTH, n), F32)],
        compiler_params=_params("arbitrary"),
        name="rwkv_step_state",
    )(state_t, vecs)
    y = pl.pallas_call(
        _rwkv_step_out_kernel,
        grid=(1,),
        in_specs=[_full((RWKV_WIDTH, n)), wide, wide, vec(RWKV_WIDTH), vec(RWKV_WIDTH)],
        out_specs=wide,
        out_shape=jax.ShapeDtypeStruct((n, RWKV_WIDTH), F32),
        compiler_params=_params("arbitrary"),
        name="rwkv_step_out",
    )(yt, bonus, gate, lnx_w, lnx_b)
    return y, state_new


def _t5_bucket_np(dist):
    max_exact = N_BUCKETS // 2
    d = np.maximum(dist, 1).astype(np.float32)
    large = max_exact + (np.log(d / np.float32(max_exact)) / np.float32(math.log(MAX_DISTANCE / max_exact))
                         * np.float32(N_BUCKETS - max_exact)).astype(np.int32)
    large = np.minimum(large, N_BUCKETS - 1)
    return np.where(dist < max_exact, dist, large).astype(np.int32)


def _prompt_bucket_table():
    qi = np.arange(BLOCK)[:, None]
    kj = np.arange(2 * BLOCK)[None, :]
    dist = BLOCK + qi - kj
    valid = (dist >= 0) & (dist <= WINDOW)
    return np.where(valid, _t5_bucket_np(np.maximum(dist, 0)), -1).astype(np.int32)


def _decode_bucket_table():
    dist = WINDOW - np.arange(WINDOW)
    return np.broadcast_to(_t5_bucket_np(dist)[None, :], (8, WINDOW)).astype(np.int32).copy()


def _bias_from_buckets(bkt, relb_ref, h, init):
    acc = jnp.full(bkt.shape, init, F32)
    for b in range(N_BUCKETS):
        acc = jnp.where(bkt == b, relb_ref[b, h], acc)
    return acc


ATT_TILE = 4 * BLOCK


def _attn_prompt_kernel(qs_ref, qm_ref, kc_ref, kp_ref, vc_ref, vp_ref, mk_ref, mv_ref, bkt_ref, relb_ref, sink_ref,
                        sqg_ref, skg_ref, mqg_ref, ys_ref, ym_ref, kn_ref, bias_scr, mk_scr, mv_scr):
    i = pl.program_id(1)
    grp = N_SWA_HEADS // N_SWA_KV_HEADS
    ones = jnp.ones((2 * BLOCK, HEAD_DIM), F32)

    @pl.when(i == 0)
    def _():
        bkt = bkt_ref[...]
        for j in range(N_SWA_KV_HEADS):
            for g in range(grp):
                bias_scr[j, g * BLOCK:(g + 1) * BLOCK, :] = _bias_from_buckets(bkt, relb_ref, j * grp + g, NEG)
        mk = mk_ref[0]
        mv = mv_ref[0]
        for h in range(N_MEM_HEADS):
            sl = slice(h * HEAD_DIM, (h + 1) * HEAD_DIM)
            mk_scr[h] = mk[:, sl].astype(BF16)
            mv_scr[h] = jnp.concatenate([mv[:, sl], ones], axis=1).astype(BF16)

    qs = qs_ref[0]
    qm = qm_ref[0]
    kc = kc_ref[0]
    kp = kp_ref[0]
    vc = vc_ref[0]
    vp = vp_ref[0]
    sqg, skg, mqg = sqg_ref[...], skg_ref[...], mqg_ref[...]
    rowi = lax.broadcasted_iota(jnp.int32, (2 * BLOCK, 1), 0)
    col = lax.broadcasted_iota(jnp.int32, (2 * BLOCK, 2 * BLOCK), 1)
    pad_mask = jnp.logical_and(i == 0, col < BLOCK)

    hsl = [slice(h * HEAD_DIM, (h + 1) * HEAD_DIM) for h in range(N_SWA_HEADS)]
    qs_n = _rms_heads(qs, sqg, N_SWA_HEADS) * ATTN_SCALE
    qm_n = _rms_heads(qm, mqg, N_MEM_HEADS) * ATTN_SCALE
    kc_n = _rms_heads(kc, skg, N_SWA_KV_HEADS)
    kp_n = _rms_heads(kp, skg, N_SWA_KV_HEADS)
    kn_ref[0] = kc_n[ATT_TILE - BLOCK:]
    chains = [(a, j) for a in range(ATT_TILE // BLOCK) for j in range(N_SWA_KV_HEADS)]
    lhs, keys, vals, sinkcol = [], [], [], []
    for a, j in chains:
        rs = slice(a * BLOCK, (a + 1) * BLOCK)
        lhs.append(jnp.concatenate([qs_n[rs, hsl[j * grp + g]] for g in range(grp)], axis=0))
        if a == 0:
            keys.append(jnp.concatenate([kp_n[:, hsl[j]], kc_n[:BLOCK, hsl[j]]], axis=0))
            vv = jnp.concatenate([vp[:, hsl[j]], vc[:BLOCK, hsl[j]]], axis=0)
        else:
            ks = slice((a - 1) * BLOCK, (a + 1) * BLOCK)
            keys.append(kc_n[ks, hsl[j]])
            vv = vc[ks, hsl[j]]
        vals.append(jnp.concatenate([vv, ones], axis=1))
        sinkcol.append(jnp.where(rowi < BLOCK, sink_ref[0, j * grp], sink_ref[0, j * grp + 1]))
    qmn = [qm_n[:, hsl[h]] for h in range(N_MEM_HEADS)]

    s_w = [_dot_nt(lhs[c], keys[c]) for c in range(len(chains))]
    s_m = [_dot_nt(qmn[h], mk_scr[h]) for h in range(N_MEM_HEADS)]
    e_w, m_w = [], []
    for c, (a, j) in enumerate(chains):
        s = s_w[c] + bias_scr[j]
        if a == 0:
            s = jnp.where(pad_mask, NEG, s)
        m = jnp.maximum(jnp.max(s, -1, keepdims=True), sinkcol[c])
        m_w.append(m)
        e_w.append(jnp.exp(s - m))
    e_m = [jnp.exp(s_m[h] - jnp.max(s_m[h], -1, keepdims=True)) for h in range(N_MEM_HEADS)]
    o_w = [_dot(e_w[c], vals[c]) for c in range(len(chains))]
    o_m = [_dot(e_m[h], mv_scr[h]) for h in range(N_MEM_HEADS)]
    for c, (a, j) in enumerate(chains):
        den = o_w[c][:, HEAD_DIM:HEAD_DIM + 1] + jnp.exp(sinkcol[c] - m_w[c])
        o = o_w[c][:, :HEAD_DIM] / den
        for g in range(grp):
            ys_ref[0, a * BLOCK:(a + 1) * BLOCK, hsl[j * grp + g]] = o[g * BLOCK:(g + 1) * BLOCK]
    for h in range(N_MEM_HEADS):
        ym_ref[0, :, hsl[h]] = o_m[h][:, :HEAD_DIM] / o_m[h][:, HEAD_DIM:HEAD_DIM + 1]


def _attn_prompt(proj, mk, mv, rel_bias, sinks, sqg, skg, mqg):
    B, T, _ = proj.shape
    bkt = jnp.asarray(_prompt_bucket_table())
    smem = pl.BlockSpec(memory_space=pltpu.SMEM)
    kblk, vblk = COL_SK // SWA_KV_WIDTH, COL_SV // SWA_KV_WIDTH
    prev = lambda b, i: (b, jnp.maximum((ATT_TILE // BLOCK) * i - 1, 0))
    memb = pl.BlockSpec((1, N_MEM, MEM_WIDTH), lambda b, i: (b, 0, 0))
    return pl.pallas_call(
        _attn_prompt_kernel,
        grid=(B, T // ATT_TILE),
        in_specs=[pl.BlockSpec((1, ATT_TILE, SWA_WIDTH), lambda b, i: (b, i, COL_SQ // SWA_WIDTH)),
                  pl.BlockSpec((1, ATT_TILE, MEM_WIDTH), lambda b, i: (b, i, COL_MQ // MEM_WIDTH)),
                  pl.BlockSpec((1, ATT_TILE, SWA_KV_WIDTH), lambda b, i: (b, i, kblk)),
                  pl.BlockSpec((1, BLOCK, SWA_KV_WIDTH), lambda b, i: prev(b, i) + (kblk,)),
                  pl.BlockSpec((1, ATT_TILE, SWA_KV_WIDTH), lambda b, i: (b, i, vblk)),
                  pl.BlockSpec((1, BLOCK, SWA_KV_WIDTH), lambda b, i: prev(b, i) + (vblk,)),
                  memb, memb,
                  _full((BLOCK, 2 * BLOCK)), smem, smem,
                  _full((1, HEAD_DIM)), _full((1, HEAD_DIM)), _full((1, HEAD_DIM))],
        out_specs=[pl.BlockSpec((1, ATT_TILE, SWA_WIDTH), lambda b, i: (b, i, 0)),
                   pl.BlockSpec((1, ATT_TILE, MEM_WIDTH), lambda b, i: (b, i, 0)),
                   pl.BlockSpec((1, BLOCK, SWA_KV_WIDTH), lambda b, i: (b, 0, 0))],
        out_shape=[jax.ShapeDtypeStruct((B, T, SWA_WIDTH), F32),
                   jax.ShapeDtypeStruct((B, T, MEM_WIDTH), F32),
                   jax.ShapeDtypeStruct((B, BLOCK, SWA_KV_WIDTH), F32)],
        scratch_shapes=[pltpu.VMEM((N_SWA_KV_HEADS, 2 * BLOCK, 2 * BLOCK), F32),
                        pltpu.VMEM((N_MEM_HEADS, N_MEM, HEAD_DIM), BF16),
                        pltpu.VMEM((N_MEM_HEADS, N_MEM, 2 * HEAD_DIM), BF16)],
        compiler_params=_params("arbitrary", "arbitrary"),
        name="attn_prompt",
    )(proj, proj, proj, proj, proj, proj, mk, mv, bkt, rel_bias, sinks, sqg, skg, mqg)


def _memory_kv_kernel(mem_ref, g_ref, w_ref, kg_ref, mk_ref, mv_ref):
    kv = jnp.dot(_rms(mem_ref[0], g_ref[...]).astype(BF16), w_ref[...], preferred_element_type=F32)
    kg = kg_ref[...]
    for h in range(N_MEM_HEADS):
        sl = slice(h * HEAD_DIM, (h + 1) * HEAD_DIM)
        mk_ref[0, :, sl] = _rms(kv[:, sl], kg)
    mv_ref[0] = kv[:, MEM_WIDTH:]


def _memory_kv(mem, g, w_bf16, kg):
    B = mem.shape[0]
    blk = pl.BlockSpec((1, N_MEM, MEM_WIDTH), lambda b: (b, 0, 0))
    return pl.pallas_call(
        _memory_kv_kernel,
        grid=(B,),
        in_specs=[pl.BlockSpec((1, N_MEM, D_MODEL), lambda b: (b, 0, 0)),
                  _full((1, D_MODEL)), _full((D_MODEL, 2 * MEM_WIDTH)), _full((1, HEAD_DIM))],
        out_specs=[blk, blk],
        out_shape=[jax.ShapeDtypeStruct((B, N_MEM, MEM_WIDTH), F32)] * 2,
        compiler_params=_params("arbitrary"),
        name="memory_kv",
    )(mem, g, w_bf16, kg)


SEQ_TILE = 8


def _rms_heads(x, g, n_heads):
    ms = jnp.dot((x * x).astype(BF16), _head_blocks(n_heads * HEAD_DIM), preferred_element_type=F32)
    return x * lax.rsqrt(ms * (1.0 / HEAD_DIM) + NORM_EPS) * jnp.concatenate([g] * n_heads, axis=1)


def _decode_attn_kernel(p_ref, kbuf_ref, vbuf_ref, mk_ref, mv_ref, bkt_ref, relb_ref, sink_ref,
                        sqg_ref, skg_ref, mqg_ref, ys_ref, ym_ref, kout_ref, vout_ref, tab_scr):
    grp = N_SWA_HEADS // N_SWA_KV_HEADS

    @pl.when(pl.program_id(0) == 0)
    def _():
        hrow = lax.broadcasted_iota(jnp.int32, (8, WINDOW), 0)
        lane = lax.broadcasted_iota(jnp.int32, (8, WINDOW), 1)
        bias_w = jnp.zeros((8, WINDOW), F32)
        cols = jnp.zeros((8, WINDOW), F32)
        for h in range(N_SWA_HEADS):
            bias_w = jnp.where(hrow == h, _bias_from_buckets(bkt_ref[...], relb_ref, h, 0.0), bias_w)
            cols = jnp.where(jnp.logical_and(hrow == h, lane == 0), relb_ref[0, h], cols)
            cols = jnp.where(jnp.logical_and(hrow == h, lane == 1), sink_ref[0, h], cols)
        tab_scr[0] = bias_w
        tab_scr[1] = cols

    bias_w = tab_scr[0][:N_SWA_HEADS]
    bias_new = tab_scr[1][:N_SWA_HEADS, 0:1]
    sink = tab_scr[1][:N_SWA_HEADS, 1:2]
    rowi = lax.broadcasted_iota(jnp.int32, (SWA_KV_WIDTH, WINDOW), 0)
    lanei = lax.broadcasted_iota(jnp.int32, (SWA_KV_WIDTH, WINDOW), 1)
    eye = rowi == lanei
    NH = N_SWA_HEADS
    own = (lax.broadcasted_iota(jnp.int32, (NH, NH * HEAD_DIM), 1) // HEAD_DIM
           == lax.broadcasted_iota(jnp.int32, (NH, NH * HEAD_DIM), 0))

    p = p_ref[...]
    qn = _rms_heads(p[:, COL_SQ:COL_SQ + SWA_WIDTH], sqg_ref[...], N_SWA_HEADS) * ATTN_SCALE
    kn = _rms_heads(p[:, COL_SK:COL_SK + SWA_KV_WIDTH], skg_ref[...], N_SWA_KV_HEADS)
    vn = p[:, COL_SV:COL_SV + SWA_KV_WIDTH]
    qmn = _rms_heads(p[:, COL_MQ:COL_MQ + MEM_WIDTH], mqg_ref[...], N_MEM_HEADS) * ATTN_SCALE
    rep = lambda x: jnp.concatenate([x[:, j * HEAD_DIM:(j + 1) * HEAD_DIM] for j in range(N_SWA_KV_HEADS)
                                     for _ in range(grp)], axis=1)
    kn_rep, vn_rep = rep(kn), rep(vn)

    B = range(SEQ_TILE)
    dup = lambda c: jnp.concatenate([c[j * HEAD_DIM:(j + 1) * HEAD_DIM] for j in range(N_SWA_KV_HEADS)
                                     for _ in range(grp)], axis=0)
    qd = [jnp.where(own, qn[b:b + 1, :], 0.0) for b in B]
    qmd = [jnp.where(own, qmn[b:b + 1, :], 0.0) for b in B]
    kdup = [dup(kbuf_ref[b]) for b in B]
    vdup = [dup(vbuf_ref[b]) for b in B]
    s = [_dot(qd[b], kdup[b]) + bias_w for b in B]
    sm = [_dot(qmd[b], mk_ref[b]) for b in B]
    s_new = [jnp.sum(qd[b] * kn_rep[b:b + 1, :], -1, keepdims=True) + bias_new for b in B]
    m = [jnp.maximum(jnp.maximum(jnp.max(s[b], -1, keepdims=True), s_new[b]), sink) for b in B]
    e = [jnp.exp(s[b] - m[b]) for b in B]
    e_new = [jnp.exp(s_new[b] - m[b]) for b in B]
    den = [jnp.sum(e[b], -1, keepdims=True) + e_new[b] + jnp.exp(sink - m[b]) for b in B]
    em = [jnp.exp(sm[b] - jnp.max(sm[b], -1, keepdims=True)) for b in B]
    ov = [_dot_nt(e[b], vdup[b]) for b in B]
    omf = [_dot_nt(em[b], mv_ref[b]) for b in B]
    ys_rows = [jnp.sum(jnp.where(own, (ov[b] + e_new[b] * vn_rep[b:b + 1, :]) / den[b], 0.0), 0, keepdims=True)
               for b in B]
    ym_rows = [jnp.sum(jnp.where(own, omf[b] / jnp.sum(em[b], -1, keepdims=True), 0.0), 0, keepdims=True)
               for b in B]
    ys_ref[...] = jnp.concatenate(ys_rows, axis=0)
    ym_ref[...] = jnp.concatenate(ym_rows, axis=0)
    for b in B:
        kn_col = jnp.sum(jnp.where(eye, kn[b:b + 1, :], 0.0), -1, keepdims=True)
        vn_col = jnp.sum(jnp.where(eye, vn[b:b + 1, :], 0.0), -1, keepdims=True)
        kout_ref[b] = jnp.where(lanei == WINDOW - 1, kn_col, pltpu.roll(kbuf_ref[b], WINDOW - 1, axis=1))
        vout_ref[b] = jnp.where(lanei == WINDOW - 1, vn_col, pltpu.roll(vbuf_ref[b], WINDOW - 1, axis=1))


def _decode_attn(proj, kbuf, vbuf, mk, mv, rel_bias, sinks, sqg, skg, mqg):
    n = proj.shape[0]
    bkt = jnp.asarray(_decode_bucket_table())
    smem = pl.BlockSpec(memory_space=pltpu.SMEM)
    win = pl.BlockSpec((SEQ_TILE, WINDOW, SWA_KV_WIDTH), lambda i: (i, 0, 0))
    memb = pl.BlockSpec((SEQ_TILE, N_MEM, MEM_WIDTH), lambda i: (i, 0, 0))
    return pl.pallas_call(
        _decode_attn_kernel,
        grid=(n // SEQ_TILE,),
        in_specs=[pl.BlockSpec((SEQ_TILE, IN_PROJ), lambda i: (i, 0)), win, win, memb, memb,
                  _full((8, WINDOW)), smem, smem, _full((1, HEAD_DIM)), _full((1, HEAD_DIM)), _full((1, HEAD_DIM))],
        out_specs=[pl.BlockSpec((SEQ_TILE, SWA_WIDTH), lambda i: (i, 0)),
                   pl.BlockSpec((SEQ_TILE, MEM_WIDTH), lambda i: (i, 0)), win, win],
        out_shape=[jax.ShapeDtypeStruct((n, SWA_WIDTH), F32), jax.ShapeDtypeStruct((n, MEM_WIDTH), F32),
                   jax.ShapeDtypeStruct(kbuf.shape, F32), jax.ShapeDtypeStruct(vbuf.shape, F32)],
        scratch_shapes=[pltpu.VMEM((2, 8, WINDOW), F32)],
        compiler_params=_params("arbitrary"),
        name="decode_attn",
    )(proj, kbuf, vbuf, mk, mv, bkt, rel_bias, sinks, sqg, skg, mqg)


FF_CHUNK = 1024
OUT_FFN_TILE = 512


def _out_ffn_kernel(x_ref, yr_ref, ys_ref, ym_ref, wo_ref, g2_ref, w1_ref, w2_ref, o_ref):
    x1 = (x_ref[...]
          + jnp.dot(yr_ref[...].astype(BF16), wo_ref[0:RWKV_WIDTH, :], preferred_element_type=F32)
          + jnp.dot(ys_ref[...].astype(BF16), wo_ref[RWKV_WIDTH:RWKV_WIDTH + SWA_WIDTH, :],
                    preferred_element_type=F32)
          + jnp.dot(ym_ref[...].astype(BF16), wo_ref[RWKV_WIDTH + SWA_WIDTH:, :], preferred_element_type=F32))
    h2 = _rms(x1, g2_ref[...]).astype(BF16)
    ff = None
    for c in range(D_FF // FF_CHUNK):
        u = jnp.dot(h2, w1_ref[:, c * FF_CHUNK:(c + 1) * FF_CHUNK], preferred_element_type=F32)
        u = jnp.square(jnp.maximum(u, 0.0)).astype(BF16)
        d = jnp.dot(u, w2_ref[c * FF_CHUNK:(c + 1) * FF_CHUNK, :], preferred_element_type=F32)
        ff = d if ff is None else ff + d
    o_ref[...] = x1 + ff


def _out_ffn(x2d, yr, ys, ym, wo, g2, w1, w2, tm):
    n = x2d.shape[0]
    rows = lambda w: pl.BlockSpec((tm, w), lambda i: (i, 0))
    const = lambda shape: pl.BlockSpec(shape, lambda i: (0, 0), pipeline_mode=pl.Buffered(1))
    return pl.pallas_call(
        _out_ffn_kernel,
        grid=(n // tm,),
        in_specs=[rows(D_MODEL), rows(RWKV_WIDTH), rows(SWA_WIDTH), rows(MEM_WIDTH),
                  const((D_MODEL, D_MODEL)), _full((1, D_MODEL)), const((D_MODEL, D_FF)), const((D_FF, D_MODEL))],
        out_specs=rows(D_MODEL),
        out_shape=jax.ShapeDtypeStruct((n, D_MODEL), F32),
        compiler_params=_params("arbitrary"),
        name="out_ffn",
    )(x2d, yr, ys, ym, wo, g2, w1, w2)


def kernel(x_prompt, x_sample, state_rwkv, state_shift, cache_swa_k, cache_swa_v, cache_mem_k, cache_mem_v,
           mem_prompt, rel_bias, norm1_g, w_in, mu_shift, w0, w_up_w, a0, w_up_a, w_up_g, k_k, k_a, r_k,
           lnx_w, lnx_b, q_norm_swa, k_norm_swa, sinks, mem_norm_g, w_mem_kv, q_norm_mem, k_norm_mem,
           w_out, norm2_g, w_ff1, w_ff2):
    B, T, _ = x_prompt.shape
    Bd = x_sample.shape[0]
    l = 0
    w_in_b = w_in[l].astype(BF16)
    w_out_b = w_out[l].astype(BF16)
    w1_b = w_ff1[l].astype(BF16)
    w2_b = w_ff2[l].astype(BF16)
    rwkv_params = (mu_shift[l][None], w0[l][None], a0[l][None], k_k[l][None], k_a[l][None],
                   r_k[l].reshape(1, RWKV_WIDTH), lnx_w[l][None], lnx_b[l][None],
                   w_up_w[l].astype(BF16), w_up_a[l].astype(BF16), w_up_g[l].astype(BF16))
    sqg, skg, mqg, mkg = q_norm_swa[l][None], k_norm_swa[l][None], q_norm_mem[l][None], k_norm_mem[l][None]
    g1, g2 = norm1_g[l][None], norm2_g[l][None]

    xp = x_prompt.reshape(B * T, D_MODEL)
    proj_p = _in_proj(xp, g1, w_in_b, IN_PROJ_TILE).reshape(B, T, IN_PROJ)
    mk, mv = _memory_kv(mem_prompt, mem_norm_g[l][None], w_mem_kv[l].astype(BF16), mkg)
    yr_p, s_p = _rwkv_prompt(proj_p, *rwkv_params)
    ys_p, ym_p, kn_p = _attn_prompt(proj_p, mk, mv, rel_bias, sinks[l][None], sqg, skg, mqg)
    y_p = _out_ffn(xp, yr_p.reshape(B * T, RWKV_WIDTH), ys_p.reshape(B * T, SWA_WIDTH),
                   ym_p.reshape(B * T, MEM_WIDTH), w_out_b, g2, w1_b, w2_b, OUT_FFN_TILE).reshape(B, T, D_MODEL)
    shift_p = proj_p[:, T - 1, :RWKV_PROJ]
    vb_p = proj_p[:, T - WINDOW:, COL_SV:COL_SV + SWA_KV_WIDTH]

    xs = x_sample.reshape(Bd, D_MODEL)
    proj_s = _in_proj(xs, g1, w_in_b, Bd)
    yr_s, st_s = _rwkv_step(proj_s[:, :RWKV_PROJ], state_shift[l], jnp.transpose(state_rwkv[l], (1, 2, 3, 0)),
                            *rwkv_params)
    s_s = jnp.transpose(st_s, (3, 0, 1, 2))
    fmajor = lambda c: jnp.transpose(c, (0, 2, 3, 1)).reshape(Bd, c.shape[2] * HEAD_DIM, c.shape[1])
    ys_s, ym_s, kb_s, vb_s = _decode_attn(
        proj_s, fmajor(cache_swa_k[l]), fmajor(cache_swa_v[l]), fmajor(cache_mem_k[l]), fmajor(cache_mem_v[l]),
        rel_bias, sinks[l][None], sqg, skg, mqg)
    pmajor = lambda c: jnp.transpose(c.reshape(Bd, N_SWA_KV_HEADS, HEAD_DIM, WINDOW), (0, 3, 1, 2))[None]
    y_s = _out_ffn(xs, yr_s, ys_s, ym_s, w_out_b, g2, w1_b, w2_b, Bd).reshape(Bd, 1, D_MODEL)

    return (y_p, y_s,
            s_p[None], shift_p[None],
            kn_p.reshape(1, B, WINDOW, N_SWA_KV_HEADS, HEAD_DIM),
            vb_p.reshape(1, B, WINDOW, N_SWA_KV_HEADS, HEAD_DIM),
            mk.reshape(1, B, N_MEM, N_MEM_HEADS, HEAD_DIM), mv.reshape(1, B, N_MEM, N_MEM_HEADS, HEAD_DIM),
            s_s[None], proj_s[:, :RWKV_PROJ][None],
            pmajor(kb_s), pmajor(vb_s))
```

```python
import math

import numpy as np
import jax
import jax.numpy as jnp
from jax import lax
from jax.experimental import pallas as pl
from jax.experimental.pallas import tpu as pltpu

F32 = jnp.float32
BF16 = jnp.bfloat16

D_MODEL = 1024
HEAD_DIM = 64
RWKV_WIDTH = 512
N_RWKV_HEADS = 8
SWA_WIDTH = 256
N_SWA_HEADS = 4
N_SWA_KV_HEADS = 2
SWA_KV_WIDTH = 128
MEM_WIDTH = 256
N_MEM_HEADS = 4
N_MEM = 256
WINDOW = 128
BLOCK = 128
N_BUCKETS = 32
MAX_DISTANCE = 128
DECAY_LORA = 64
AAA_LORA = 64
GATE_LORA = 128
RWKV_PROJ = 3 * RWKV_WIDTH + DECAY_LORA + AAA_LORA + GATE_LORA
SWA_PROJ = SWA_WIDTH + 2 * SWA_KV_WIDTH
IN_PROJ = RWKV_PROJ + SWA_PROJ + MEM_WIDTH
D_FF = 4 * D_MODEL
NORM_EPS = 1e-6
LNX_EPS = 64e-5
ATTN_SCALE = HEAD_DIM ** -0.5
EXP_M05 = math.exp(-0.5)
LOG2E = math.log2(math.e)
NEG = -1e30

COL_R, COL_K, COL_V = 0, RWKV_WIDTH, 2 * RWKV_WIDTH
COL_WD = 3 * RWKV_WIDTH
COL_AD = COL_WD + DECAY_LORA
COL_GD = COL_AD + AAA_LORA
COL_SQ = RWKV_PROJ
COL_SK = COL_SQ + SWA_WIDTH
COL_SV = COL_SK + SWA_KV_WIDTH
COL_MQ = RWKV_PROJ + SWA_PROJ

CHUNK = 64
VMEM_LIMIT = 56 * 1024 * 1024


def _dot(a, b):
    return jnp.dot(a.astype(BF16), b.astype(BF16), preferred_element_type=F32)


def _dot_nt(a, b):
    return lax.dot_general(a.astype(BF16), b.astype(BF16), (((1,), (1,)), ((), ())),
                           preferred_element_type=F32)


def _dot_tn(a, b):
    return lax.dot_general(a.astype(BF16), b.astype(BF16), (((0,), (0,)), ((), ())),
                           preferred_element_type=F32)


def _rms(x, g):
    return x * lax.rsqrt(jnp.mean(x * x, -1, keepdims=True) + NORM_EPS) * g


def _params(*sem):
    return pltpu.CompilerParams(dimension_semantics=sem, vmem_limit_bytes=VMEM_LIMIT)


def _full(shape):
    n = len(shape)
    return pl.BlockSpec(shape, lambda *_: (0,) * n)


def _head_blocks(width):
    bi = lax.broadcasted_iota(jnp.int32, (width, width), 0) // HEAD_DIM
    bj = lax.broadcasted_iota(jnp.int32, (width, width), 1) // HEAD_DIM
    return jnp.where(bi == bj, 1.0, 0.0).astype(BF16)


IN_PROJ_TILE = 1024
IN_PROJ_SUB = 256


def _in_proj_kernel(x_ref, g_ref, w_ref, o_ref):
    tm = x_ref.shape[0]
    sub = min(tm, IN_PROJ_SUB)
    for j in range(tm // sub):
        rows = slice(j * sub, (j + 1) * sub)
        h = _rms(x_ref[rows, :], g_ref[...])
        o_ref[rows, :] = jnp.dot(h.astype(BF16), w_ref[...], preferred_element_type=F32)


def _in_proj(x2d, g, w_bf16, tm):
    n = x2d.shape[0]
    return pl.pallas_call(
        _in_proj_kernel,
        grid=(n // tm,),
        in_specs=[pl.BlockSpec((tm, D_MODEL), lambda i: (i, 0)),
                  _full((1, D_MODEL)),
                  _full((D_MODEL, IN_PROJ))],
        out_specs=pl.BlockSpec((tm, IN_PROJ), lambda i: (i, 0)),
        out_shape=jax.ShapeDtypeStruct((n, IN_PROJ), F32),
        compiler_params=_params("arbitrary"),
        name="in_proj",
    )(x2d, g, w_bf16)


def _rwkv_features(xs, w0, a0, k_k, k_a, wupw, wupa, wupg):
    r = xs[:, COL_R:COL_R + RWKV_WIDTH]
    k = xs[:, COL_K:COL_K + RWKV_WIDTH]
    v = xs[:, COL_V:COL_V + RWKV_WIDTH]
    wd = xs[:, COL_WD:COL_WD + DECAY_LORA]
    ad = xs[:, COL_AD:COL_AD + AAA_LORA]
    gd = xs[:, COL_GD:COL_GD + GATE_LORA]
    logw = -jax.nn.sigmoid(w0 + _dot(jnp.tanh(wd), wupw)) * EXP_M05
    a_sig = jax.nn.sigmoid(a0 + _dot(ad, wupa))
    gate = _dot(jax.nn.sigmoid(gd), wupg)
    kk = k * k_k
    k2 = k * (1.0 + (a_sig - 1.0) * k_a)
    return r, k2, v, kk, a_sig, logw, gate


def _seg_sum(x, blk):
    xb = x.astype(BF16)
    half = RWKV_WIDTH // 2
    return jnp.concatenate([jnp.dot(xb[:, :half], blk, preferred_element_type=F32),
                            jnp.dot(xb[:, half:], blk, preferred_element_type=F32)], axis=1)


def _group_norm_out(y, bonus, gate, lnx_w, lnx_b, blk):
    inv_d = 1.0 / HEAD_DIM
    m = _seg_sum(y, blk) * inv_d
    d = y - m
    var = _seg_sum(d * d, blk) * inv_d
    yn = d * lax.rsqrt(var + LNX_EPS) * lnx_w + lnx_b
    return (yn + bonus) * gate


RWKV_TILE = 4 * CHUNK
PAIR = 2 * HEAD_DIM
N_PAIRS = N_RWKV_HEADS // 2


def _rwkv_prompt_kernel(p_ref, mu_ref, w0_ref, a0_ref, kk_ref, ka_ref, rk_ref, lnw_ref, lnb_ref,
                        wupw_ref, wupa_ref, wupg_ref, y_ref, sout_ref, s_scr, prev_scr):
    C, TT, D = CHUNK, RWKV_TILE, HEAD_DIM
    NC = TT // C
    t = pl.program_id(1)

    @pl.when(t == 0)
    def _():
        s_scr[...] = jnp.zeros_like(s_scr)
        prev_scr[...] = jnp.zeros_like(prev_scr)

    p = p_ref[0]
    row = lax.broadcasted_iota(jnp.int32, p.shape, 0)
    prev = jnp.where(row == 0, prev_scr[...], pltpu.roll(p, 1, axis=0))
    prev_scr[...] = p[TT - 1:TT, :]
    xs = p + (prev - p) * mu_ref[...]
    r, k2, v, kk, a_sig, logw, gate = _rwkv_features(
        xs, w0_ref[...], a0_ref[...], kk_ref[...], ka_ref[...], wupw_ref[...], wupa_ref[...], wupg_ref[...])

    blk = _head_blocks(RWKV_WIDTH // 2)
    kkn = kk / jnp.maximum(jnp.sqrt(_seg_sum(kk * kk, blk)), 1e-12)
    bb = kkn * a_sig

    ri = lax.broadcasted_iota(jnp.int32, (TT, TT), 0)
    ci = lax.broadcasted_iota(jnp.int32, (TT, TT), 1)
    tri = jnp.where(jnp.logical_and(ri >= ci, ri // C == ci // C), 1.0, 0.0).astype(BF16)
    l1 = logw.astype(BF16)
    rem = logw - l1.astype(F32)
    l2 = rem.astype(BF16)
    l3 = (rem - l2.astype(F32)).astype(BF16)
    cum = (jnp.dot(tri, l1, preferred_element_type=F32) + jnp.dot(tri, l2, preferred_element_type=F32)
           + jnp.dot(tri, l3, preferred_element_type=F32))
    c_last = jnp.concatenate([jnp.broadcast_to(cum[(c + 1) * C - 1:(c + 1) * C, :], (C, RWKV_WIDTH))
                              for c in range(NC)], axis=0)
    e_pos = jnp.exp(cum)
    e_neg = jnp.exp(-cum)
    e_prev = jnp.exp(cum - logw)
    e_last = jnp.exp(c_last - cum)

    lo_full = (lax.broadcasted_iota(jnp.int32, (TT, RWKV_WIDTH), 1) % PAIR) < D
    at_f = -kkn * e_prev
    rt_f = r * e_pos
    at_lo = jnp.where(lo_full, at_f, 0.0).astype(BF16)
    at_hi = jnp.where(lo_full, 0.0, at_f).astype(BF16)
    rt_lo = jnp.where(lo_full, rt_f, 0.0).astype(BF16)
    rt_hi = jnp.where(lo_full, 0.0, rt_f).astype(BF16)
    bt_b = (bb * e_neg).astype(BF16)
    kt_b = (k2 * e_neg).astype(BF16)
    bh_b = (bb * e_last).astype(BF16)
    kh_b = (k2 * e_last).astype(BF16)
    v_b = v.astype(BF16)

    r2 = lax.broadcasted_iota(jnp.int32, (C, PAIR), 0)
    c2 = lax.broadcasted_iota(jnp.int32, (C, PAIR), 1)
    lo = c2 < D
    c2m = jnp.where(lo, c2, c2 - C)
    mask_a = jnp.logical_and(lo, c2 < r2)
    mask_ak = jnp.logical_and(jnp.logical_not(lo), c2m < r2)
    mask_r = c2m <= r2
    eye_hi = jnp.where(jnp.logical_and(jnp.logical_not(lo), c2m == r2), 1.0, 0.0)
    zeros_cp = jnp.zeros((C, PAIR), F32)
    qi = lax.broadcasted_iota(jnp.int32, (PAIR, PAIR), 0) // D
    qj = lax.broadcasted_iota(jnp.int32, (PAIR, PAIR), 1) // D
    diag = qi == qj

    PR = [(c, q) for c in range(NC) for q in range(N_PAIRS)]
    n = len(PR)
    E = range(2)
    win = lambda x, c, q: x[c * C:(c + 1) * C, q * PAIR:(q + 1) * PAIR]
    sc = [_dot_nt(jnp.concatenate([win(at_lo, c, q), win(at_hi, c, q), win(rt_lo, c, q), win(rt_hi, c, q)], axis=0),
                  jnp.concatenate([win(bt_b, c, q), win(kt_b, c, q)], axis=0)) for c, q in PR]
    vr = [pltpu.roll(win(v, c, q), D, axis=1) for c, q in PR]
    vvr = [jnp.concatenate([vr[i], vr[i]], axis=0).astype(BF16) for i in range(n)]
    m_ak = [[jnp.where(mask_ak, sc[i][e * C:(e + 1) * C], 0.0) for e in E] for i in range(n)]
    m_r = [[jnp.where(mask_r, sc[i][(2 + e) * C:(3 + e) * C], 0.0) for e in E] for i in range(n)]

    zf = [[_dot(m_ak[i][e], vvr[i]) for e in E] for i in range(n)]
    W = [[jnp.where(mask_a, sc[i][e * C:(e + 1) * C], eye_hi) for e in E] for i in range(n)]
    for k in range(int(math.log2(C))):
        Wb = [[W[i][e].astype(BF16) for e in E] for i in range(n)]
        AW = [[jnp.dot(Wb[i][e][:, :C], Wb[i][e], preferred_element_type=F32) for e in E] for i in range(n)]
        W = [[jnp.where(lo, 0.0, W[i][e]) + AW[i][e] for e in E] for i in range(n)]
    X = [[_dot(W[i][0], jnp.concatenate([zeros_cp, jnp.where(lo, win(at_f, c, q), zf[i][0])], axis=0)),
          _dot(W[i][1], jnp.concatenate([zeros_cp, jnp.where(lo, zf[i][1], win(at_f, c, q))], axis=0))]
         for i, (c, q) in enumerate(PR)]

    S = [s_scr[q] for q in range(N_PAIRS)]
    ys = []
    for c in range(NC):
        w_last = jnp.exp(cum[(c + 1) * C - 1:(c + 1) * C, :])
        idx = [c * N_PAIRS + q for q in range(N_PAIRS)]
        mkg = [[_dot_tn(X[i][e], win(bh_b, c, q)) for e in E] for q, i in enumerate(idx)]
        vk = [_dot_tn(win(v_b, c, q), win(kh_b, c, q)) for q, i in enumerate(idx)]
        ry = [[_dot(m_r[i][0], jnp.concatenate([X[i][0], jnp.where(lo, 0.0, vr[i])], axis=0)),
               _dot(m_r[i][1], jnp.concatenate([X[i][1], jnp.where(lo, vr[i], 0.0)], axis=0))] for i in idx]
        rp = [win(rt_f, c, q) + jnp.where(lo, ry[q][0], ry[q][1]) for q in range(N_PAIRS)]
        y0 = [pltpu.roll(jnp.where(lo, ry[q][1], ry[q][0]), D, axis=1) for q in range(N_PAIRS)]
        mk = [jnp.where(diag, jnp.concatenate([mkg[q][0][:D], mkg[q][1][D:]], axis=0), 0.0) for q in range(N_PAIRS)]
        g = [jnp.where(diag, vk[q] + jnp.concatenate([mkg[q][0][D:], mkg[q][1][:D]], axis=0), 0.0)
             for q in range(N_PAIRS)]
        y = [_dot_nt(rp[q], S[q]) for q in range(N_PAIRS)]
        dS = [_dot(S[q], mk[q]) for q in range(N_PAIRS)]
        S = [S[q] * w_last[:, q * PAIR:(q + 1) * PAIR] + dS[q] + g[q] for q in range(N_PAIRS)]
        ys.append(jnp.concatenate([y[q] + y0[q] for q in range(N_PAIRS)], axis=1))
    for q in range(N_PAIRS):
        s_scr[q] = S[q]

    bonus = _seg_sum(r * k2 * rk_ref[...], blk) * v
    y_ref[0] = _group_norm_out(jnp.concatenate(ys, axis=0), bonus, gate, lnw_ref[...], lnb_ref[...], blk)

    @pl.when(t == pl.num_programs(1) - 1)
    def _():
        for q in range(N_PAIRS):
            sout_ref[0, 2 * q] = S[q][:D, :D]
            sout_ref[0, 2 * q + 1] = S[q][D:, D:]


def _rwkv_prompt(proj, mu, w0, a0, k_k, k_a, r_k, lnx_w, lnx_b, wupw, wupa, wupg):
    B, T, _ = proj.shape
    vec = lambda n: _full((1, n))
    return pl.pallas_call(
        _rwkv_prompt_kernel,
        grid=(B, T // RWKV_TILE),
        in_specs=[pl.BlockSpec((1, RWKV_TILE, RWKV_PROJ), lambda b, t: (b, t, 0)),
                  vec(RWKV_PROJ), vec(RWKV_WIDTH), vec(RWKV_WIDTH), vec(RWKV_WIDTH), vec(RWKV_WIDTH),
                  vec(RWKV_WIDTH), vec(RWKV_WIDTH), vec(RWKV_WIDTH),
                  _full((DECAY_LORA, RWKV_WIDTH)), _full((AAA_LORA, RWKV_WIDTH)), _full((GATE_LORA, RWKV_WIDTH))],
        out_specs=[pl.BlockSpec((1, RWKV_TILE, RWKV_WIDTH), lambda b, t: (b, t, 0)),
                   pl.BlockSpec((1, N_RWKV_HEADS, HEAD_DIM, HEAD_DIM), lambda b, t: (b, 0, 0, 0))],
        out_shape=[jax.ShapeDtypeStruct((B, T, RWKV_WIDTH), F32),
                   jax.ShapeDtypeStruct((B, N_RWKV_HEADS, HEAD_DIM, HEAD_DIM), F32)],
        scratch_shapes=[pltpu.VMEM((N_PAIRS, PAIR, PAIR), F32),
                        pltpu.VMEM((1, RWKV_PROJ), F32)],
        compiler_params=_params("arbitrary", "arbitrary"),
        name="rwkv_prompt",
    )(proj, mu, w0, a0, k_k, k_a, r_k, lnx_w, lnx_b, wupw, wupa, wupg)


def _rwkv_step_feat_kernel(p_ref, sh_ref, mu_ref, w0_ref, a0_ref, kk_ref, ka_ref, rk_ref,
                           wupw_ref, wupa_ref, wupg_ref, vecs_ref, bonus_ref, gate_ref):
    p = p_ref[...]
    xs = p + (sh_ref[...] - p) * mu_ref[...]
    r, k2, v, kk, a_sig, logw, gate = _rwkv_features(
        xs, w0_ref[...], a0_ref[...], kk_ref[...], ka_ref[...], wupw_ref[...], wupa_ref[...], wupg_ref[...])
    blk = _head_blocks(RWKV_WIDTH // 2)
    kkn = kk / jnp.maximum(jnp.sqrt(_seg_sum(kk * kk, blk)), 1e-12)
    for i, x in enumerate((-kkn, kkn * a_sig, jnp.exp(logw), k2, r, v)):
        vecs_ref[i] = x.T
    bonus_ref[...] = _seg_sum(r * k2 * rk_ref[...], blk) * v
    gate_ref[...] = gate


def _rwkv_step_state_kernel(s_ref, vecs_ref, sout_ref, y_ref):
    S = s_ref[0]
    a, b, w, k, r, v = (vecs_ref[i] for i in range(6))
    sa = jnp.sum(S * a[None], axis=1)
    S = S * w[None] + sa[:, None, :] * b[None] + v[:, None, :] * k[None]
    sout_ref[0] = S
    y_ref[...] = jnp.sum(S * r[None], axis=1)


def _rwkv_step_out_kernel(yt_ref, bonus_ref, gate_ref, lnw_ref, lnb_ref, o_ref):
    o_ref[...] = _group_norm_out(yt_ref[...].T, bonus_ref[...], gate_ref[...], lnw_ref[...], lnb_ref[...],
                                 _head_blocks(RWKV_WIDTH // 2))


def _rwkv_step(proj, shift, state_t, mu, w0, a0, k_k, k_a, r_k, lnx_w, lnx_b, wupw, wupa, wupg):
    n = proj.shape[0]
    vec = lambda m: _full((1, m))
    rows = _full((n, RWKV_PROJ))
    wide = _full((n, RWKV_WIDTH))
    vecs, bonus, gate = pl.pallas_call(
        _rwkv_step_feat_kernel,
        grid=(1,),
        in_specs=[rows, rows, vec(RWKV_PROJ), vec(RWKV_WIDTH), vec(RWKV_WIDTH), vec(RWKV_WIDTH), vec(RWKV_WIDTH),
                  vec(RWKV_WIDTH),
                  _full((DECAY_LORA, RWKV_WIDTH)), _full((AAA_LORA, RWKV_WIDTH)), _full((GATE_LORA, RWKV_WIDTH))],
        out_specs=[_full((6, RWKV_WIDTH, n)), wide, wide],
        out_shape=[jax.ShapeDtypeStruct((6, RWKV_WIDTH, n), F32),
                   jax.ShapeDtypeStruct((n, RWKV_WIDTH), F32), jax.ShapeDtypeStruct((n, RWKV_WIDTH), F32)],
        compiler_params=_params("arbitrary"),
        name="rwkv_step_feat",
    )(proj, shift, mu, w0, a0, k_k, k_a, r_k, wupw, wupa, wupg)
    st_spec = pl.BlockSpec((1, HEAD_DIM, HEAD_DIM, n), lambda h: (h, 0, 0, 0))
    state_new, yt = pl.pallas_call(
        _rwkv_step_state_kernel,
        grid=(N_RWKV_HEADS,),
        in_specs=[st_spec, pl.BlockSpec((6, HEAD_DIM, n), lambda h: (0, h, 0))],
        out_specs=[st_spec, pl.BlockSpec((HEAD_DIM, n), lambda h: (h, 0))],
        out_shape=[jax.ShapeDtypeStruct(state_t.shape, F32), jax.ShapeDtypeStruct((RWKV_WIDTH, n), F32)],
        compiler_params=_params("arbitrary"),
        name="rwkv_step_state",
    )(state_t, vecs)
    y = pl.pallas_call(
        _rwkv_step_out_kernel,
        grid=(1,),
        in_specs=[_full((RWKV_WIDTH, n)), wide, wide, vec(RWKV_WIDTH), vec(RWKV_WIDTH)],
        out_specs=wide,
        out_shape=jax.ShapeDtypeStruct((n, RWKV_WIDTH), F32),
        compiler_params=_params("arbitrary"),
        name="rwkv_step_out",
    )(yt, bonus, gate, lnx_w, lnx_b)
    return y, state_new


def _t5_bucket_np(dist):
    max_exact = N_BUCKETS // 2
    d = np.maximum(dist, 1).astype(np.float32)
    large = max_exact + (np.log(d / np.float32(max_exact)) / np.float32(math.log(MAX_DISTANCE / max_exact))
                         * np.float32(N_BUCKETS - max_exact)).astype(np.int32)
    large = np.minimum(large, N_BUCKETS - 1)
    return np.where(dist < max_exact, dist, large).astype(np.int32)


def _prompt_bucket_table():
    qi = np.arange(BLOCK)[:, None]
    kj = np.arange(2 * BLOCK)[None, :]
    dist = BLOCK + qi - kj
    valid = (dist >= 0) & (dist <= WINDOW)
    return np.where(valid, _t5_bucket_np(np.maximum(dist, 0)), -1).astype(np.int32)


def _decode_bucket_table():
    dist = WINDOW - np.arange(WINDOW)
    return np.broadcast_to(_t5_bucket_np(dist)[None, :], (8, WINDOW)).astype(np.int32).copy()


def _bias_from_buckets(bkt, relb_ref, h, init):
    acc = jnp.full(bkt.shape, init, F32)
    for b in range(N_BUCKETS):
        acc = jnp.where(bkt == b, relb_ref[b, h], acc)
    return acc


ATT_TILE = 4 * BLOCK


def _attn_prompt_kernel(qs_ref, qm_ref, kc_ref, kp_ref, vc_ref, vp_ref, mk_ref, mv_ref, bkt_ref, relb_ref, sink_ref,
                        sqg_ref, skg_ref, mqg_ref, ys_ref, ym_ref, kn_ref, bias_scr, mk_scr, mv_scr):
    i = pl.program_id(1)
    grp = N_SWA_HEADS // N_SWA_KV_HEADS
    ones = jnp.ones((2 * BLOCK, HEAD_DIM), F32)

    @pl.when(i == 0)
    def _():
        bkt = bkt_ref[...]
        for j in range(N_SWA_KV_HEADS):
            for g in range(grp):
                bias_scr[j, g * BLOCK:(g + 1) * BLOCK, :] = _bias_from_buckets(bkt, relb_ref, j * grp + g, NEG) * LOG2E
        mk = mk_ref[0]
        mv = mv_ref[0]
        for h in range(N_MEM_HEADS):
            sl = slice(h * HEAD_DIM, (h + 1) * HEAD_DIM)
            mk_scr[h] = mk[:, sl].astype(BF16)
            mv_scr[h] = jnp.concatenate([mv[:, sl], ones], axis=1).astype(BF16)

    qs = qs_ref[0]
    qm = qm_ref[0]
    kc = kc_ref[0]
    kp = kp_ref[0]
    vc = vc_ref[0]
    vp = vp_ref[0]
    sqg, skg, mqg = sqg_ref[...], skg_ref[...], mqg_ref[...]
    rowi = lax.broadcasted_iota(jnp.int32, (2 * BLOCK, 1), 0)
    col = lax.broadcasted_iota(jnp.int32, (2 * BLOCK, 2 * BLOCK), 1)
    pad_mask = jnp.logical_and(i == 0, col < BLOCK)

    hsl = [slice(h * HEAD_DIM, (h + 1) * HEAD_DIM) for h in range(N_SWA_HEADS)]
    qs_n = _rms_heads(qs, sqg, N_SWA_HEADS) * (ATTN_SCALE * LOG2E)
    qm_n = _rms_heads(qm, mqg, N_MEM_HEADS) * (ATTN_SCALE * LOG2E)
    kc_n = _rms_heads(kc, skg, N_SWA_KV_HEADS)
    kp_n = _rms_heads(kp, skg, N_SWA_KV_HEADS)
    kn_ref[0] = kc_n[ATT_TILE - BLOCK:]
    chains = [(a, j) for a in range(ATT_TILE // BLOCK) for j in range(N_SWA_KV_HEADS)]
    lhs, keys, vals, sinkcol = [], [], [], []
    for a, j in chains:
        rs = slice(a * BLOCK, (a + 1) * BLOCK)
        lhs.append(jnp.concatenate([qs_n[rs, hsl[j * grp + g]] for g in range(grp)], axis=0))
        if a == 0:
            keys.append(jnp.concatenate([kp_n[:, hsl[j]], kc_n[:BLOCK, hsl[j]]], axis=0))
            vv = jnp.concatenate([vp[:, hsl[j]], vc[:BLOCK, hsl[j]]], axis=0)
        else:
            ks = slice((a - 1) * BLOCK, (a + 1) * BLOCK)
            keys.append(kc_n[ks, hsl[j]])
            vv = vc[ks, hsl[j]]
        vals.append(jnp.concatenate([vv, ones], axis=1))
        sinkcol.append(jnp.where(rowi < BLOCK, sink_ref[0, j * grp], sink_ref[0, j * grp + 1]) * LOG2E)
    qmn = [qm_n[:, hsl[h]] for h in range(N_MEM_HEADS)]

    s_w = [_dot_nt(lhs[c], keys[c]) for c in range(len(chains))]
    s_m = [_dot_nt(qmn[h], mk_scr[h]) for h in range(N_MEM_HEADS)]
    e_w, m_w = [], []
    for c, (a, j) in enumerate(chains):
        s = s_w[c] + bias_scr[j]
        if a == 0:
            s = jnp.where(pad_mask, NEG, s)
        m = jnp.maximum(jnp.max(s, -1, keepdims=True), sinkcol[c])
        m_w.append(m)
        e_w.append(jnp.exp2(s - m))
    e_m = [jnp.exp2(s_m[h] - jnp.max(s_m[h], -1, keepdims=True)) for h in range(N_MEM_HEADS)]
    o_w = [_dot(e_w[c], vals[c]) for c in range(len(chains))]
    o_m = [_dot(e_m[h], mv_scr[h]) for h in range(N_MEM_HEADS)]
    for c, (a, j) in enumerate(chains):
        den = o_w[c][:, HEAD_DIM:HEAD_DIM + 1] + jnp.exp2(sinkcol[c] - m_w[c])
        o = o_w[c][:, :HEAD_DIM] / den
        for g in range(grp):
            ys_ref[0, a * BLOCK:(a + 1) * BLOCK, hsl[j * grp + g]] = o[g * BLOCK:(g + 1) * BLOCK]
    for h in range(N_MEM_HEADS):
        ym_ref[0, :, hsl[h]] = o_m[h][:, :HEAD_DIM] / o_m[h][:, HEAD_DIM:HEAD_DIM + 1]


def _attn_prompt(proj, mk, mv, rel_bias, sinks, sqg, skg, mqg):
    B, T, _ = proj.shape
    bkt = jnp.asarray(_prompt_bucket_table())
    smem = pl.BlockSpec(memory_space=pltpu.SMEM)
    kblk, vblk = COL_SK // SWA_KV_WIDTH, COL_SV // SWA_KV_WIDTH
    prev = lambda b, i: (b, jnp.maximum((ATT_TILE // BLOCK) * i - 1, 0))
    memb = pl.BlockSpec((1, N_MEM, MEM_WIDTH), lambda b, i: (b, 0, 0))
    return pl.pallas_call(
        _attn_prompt_kernel,
        grid=(B, T // ATT_TILE),
        in_specs=[pl.BlockSpec((1, ATT_TILE, SWA_WIDTH), lambda b, i: (b, i, COL_SQ // SWA_WIDTH)),
                  pl.BlockSpec((1, ATT_TILE, MEM_WIDTH), lambda b, i: (b, i, COL_MQ // MEM_WIDTH)),
                  pl.BlockSpec((1, ATT_TILE, SWA_KV_WIDTH), lambda b, i: (b, i, kblk)),
                  pl.BlockSpec((1, BLOCK, SWA_KV_WIDTH), lambda b, i: prev(b, i) + (kblk,)),
                  pl.BlockSpec((1, ATT_TILE, SWA_KV_WIDTH), lambda b, i: (b, i, vblk)),
                  pl.BlockSpec((1, BLOCK, SWA_KV_WIDTH), lambda b, i: prev(b, i) + (vblk,)),
                  memb, memb,
                  _full((BLOCK, 2 * BLOCK)), smem, smem,
                  _full((1, HEAD_DIM)), _full((1, HEAD_DIM)), _full((1, HEAD_DIM))],
        out_specs=[pl.BlockSpec((1, ATT_TILE, SWA_WIDTH), lambda b, i: (b, i, 0)),
                   pl.BlockSpec((1, ATT_TILE, MEM_WIDTH), lambda b, i: (b, i, 0)),
                   pl.BlockSpec((1, BLOCK, SWA_KV_WIDTH), lambda b, i: (b, 0, 0))],
        out_shape=[jax.ShapeDtypeStruct((B, T, SWA_WIDTH), F32),
                   jax.ShapeDtypeStruct((B, T, MEM_WIDTH), F32),
                   jax.ShapeDtypeStruct((B, BLOCK, SWA_KV_WIDTH), F32)],
        scratch_shapes=[pltpu.VMEM((N_SWA_KV_HEADS, 2 * BLOCK, 2 * BLOCK), F32),
                        pltpu.VMEM((N_MEM_HEADS, N_MEM, HEAD_DIM), BF16),
                        pltpu.VMEM((N_MEM_HEADS, N_MEM, 2 * HEAD_DIM), BF16)],
        compiler_params=_params("arbitrary", "arbitrary"),
        name="attn_prompt",
    )(proj, proj, proj, proj, proj, proj, mk, mv, bkt, rel_bias, sinks, sqg, skg, mqg)


def _memory_kv_kernel(mem_ref, g_ref, w_ref, kg_ref, mk_ref, mv_ref):
    kv = jnp.dot(_rms(mem_ref[0], g_ref[...]).astype(BF16), w_ref[...], preferred_element_type=F32)
    kg = kg_ref[...]
    for h in range(N_MEM_HEADS):
        sl = slice(h * HEAD_DIM, (h + 1) * HEAD_DIM)
        mk_ref[0, :, sl] = _rms(kv[:, sl], kg)
    mv_ref[0] = kv[:, MEM_WIDTH:]


def _memory_kv(mem, g, w_bf16, kg):
    B = mem.shape[0]
    blk = pl.BlockSpec((1, N_MEM, MEM_WIDTH), lambda b: (b, 0, 0))
    return pl.pallas_call(
        _memory_kv_kernel,
        grid=(B,),
        in_specs=[pl.BlockSpec((1, N_MEM, D_MODEL), lambda b: (b, 0, 0)),
                  _full((1, D_MODEL)), _full((D_MODEL, 2 * MEM_WIDTH)), _full((1, HEAD_DIM))],
        out_specs=[blk, blk],
        out_shape=[jax.ShapeDtypeStruct((B, N_MEM, MEM_WIDTH), F32)] * 2,
        compiler_params=_params("arbitrary"),
        name="memory_kv",
    )(mem, g, w_bf16, kg)


SEQ_TILE = 8


def _rms_heads(x, g, n_heads):
    ms = jnp.dot((x * x).astype(BF16), _head_blocks(n_heads * HEAD_DIM), preferred_element_type=F32)
    return x * lax.rsqrt(ms * (1.0 / HEAD_DIM) + NORM_EPS) * jnp.concatenate([g] * n_heads, axis=1)


def _decode_attn_kernel(p_ref, kbuf_ref, vbuf_ref, mk_ref, mv_ref, bkt_ref, relb_ref, sink_ref,
                        sqg_ref, skg_ref, mqg_ref, ys_ref, ym_ref, kout_ref, vout_ref, tab_scr):
    grp = N_SWA_HEADS // N_SWA_KV_HEADS

    @pl.when(pl.program_id(0) == 0)
    def _():
        hrow = lax.broadcasted_iota(jnp.int32, (8, WINDOW), 0)
        lane = lax.broadcasted_iota(jnp.int32, (8, WINDOW), 1)
        bias_w = jnp.zeros((8, WINDOW), F32)
        cols = jnp.zeros((8, WINDOW), F32)
        for h in range(N_SWA_HEADS):
            bias_w = jnp.where(hrow == h, _bias_from_buckets(bkt_ref[...], relb_ref, h, 0.0), bias_w)
            cols = jnp.where(jnp.logical_and(hrow == h, lane == 0), relb_ref[0, h], cols)
            cols = jnp.where(jnp.logical_and(hrow == h, lane == 1), sink_ref[0, h], cols)
        tab_scr[0] = bias_w
        tab_scr[1] = cols

    bias_w = tab_scr[0][:N_SWA_HEADS]
    bias_new = tab_scr[1][:N_SWA_HEADS, 0:1]
    sink = tab_scr[1][:N_SWA_HEADS, 1:2]
    rowi = lax.broadcasted_iota(jnp.int32, (SWA_KV_WIDTH, WINDOW), 0)
    lanei = lax.broadcasted_iota(jnp.int32, (SWA_KV_WIDTH, WINDOW), 1)
    eye = rowi == lanei
    NH = N_SWA_HEADS
    own = (lax.broadcasted_iota(jnp.int32, (NH, NH * HEAD_DIM), 1) // HEAD_DIM
           == lax.broadcasted_iota(jnp.int32, (NH, NH * HEAD_DIM), 0))

    p = p_ref[...]
    qn = _rms_heads(p[:, COL_SQ:COL_SQ + SWA_WIDTH], sqg_ref[...], N_SWA_HEADS) * ATTN_SCALE
    kn = _rms_heads(p[:, COL_SK:COL_SK + SWA_KV_WIDTH], skg_ref[...], N_SWA_KV_HEADS)
    vn = p[:, COL_SV:COL_SV + SWA_KV_WIDTH]
    qmn = _rms_heads(p[:, COL_MQ:COL_MQ + MEM_WIDTH], mqg_ref[...], N_MEM_HEADS) * ATTN_SCALE
    rep = lambda x: jnp.concatenate([x[:, j * HEAD_DIM:(j + 1) * HEAD_DIM] for j in range(N_SWA_KV_HEADS)
                                     for _ in range(grp)], axis=1)
    kn_rep, vn_rep = rep(kn), rep(vn)

    B = range(SEQ_TILE)
    dup = lambda c: jnp.concatenate([c[j * HEAD_DIM:(j + 1) * HEAD_DIM] for j in range(N_SWA_KV_HEADS)
                                     for _ in range(grp)], axis=0)
    qd = [jnp.where(own, qn[b:b + 1, :], 0.0) for b in B]
    qmd = [jnp.where(own, qmn[b:b + 1, :], 0.0) for b in B]
    kdup = [dup(kbuf_ref[b]) for b in B]
    vdup = [dup(vbuf_ref[b]) for b in B]
    s = [_dot(qd[b], kdup[b]) + bias_w for b in B]
    sm = [_dot(qmd[b], mk_ref[b]) for b in B]
    s_new = [jnp.sum(qd[b] * kn_rep[b:b + 1, :], -1, keepdims=True) + bias_new for b in B]
    m = [jnp.maximum(jnp.maximum(jnp.max(s[b], -1, keepdims=True), s_new[b]), sink) for b in B]
    e = [jnp.exp(s[b] - m[b]) for b in B]
    e_new = [jnp.exp(s_new[b] - m[b]) for b in B]
    den = [jnp.sum(e[b], -1, keepdims=True) + e_new[b] + jnp.exp(sink - m[b]) for b in B]
    em = [jnp.exp(sm[b] - jnp.max(sm[b], -1, keepdims=True)) for b in B]
    ov = [_dot_nt(e[b], vdup[b]) for b in B]
    omf = [_dot_nt(em[b], mv_ref[b]) for b in B]
    ys_rows = [jnp.sum(jnp.where(own, (ov[b] + e_new[b] * vn_rep[b:b + 1, :]) / den[b], 0.0), 0, keepdims=True)
               for b in B]
    ym_rows = [jnp.sum(jnp.where(own, omf[b] / jnp.sum(em[b], -1, keepdims=True), 0.0), 0, keepdims=True)
               for b in B]
    ys_ref[...] = jnp.concatenate(ys_rows, axis=0)
    ym_ref[...] = jnp.concatenate(ym_rows, axis=0)
    for b in B:
        kn_col = jnp.sum(jnp.where(eye, kn[b:b + 1, :], 0.0), -1, keepdims=True)
        vn_col = jnp.sum(jnp.where(eye, vn[b:b + 1, :], 0.0), -1, keepdims=True)
        kout_ref[b] = jnp.where(lanei == WINDOW - 1, kn_col, pltpu.roll(kbuf_ref[b], WINDOW - 1, axis=1))
        vout_ref[b] = jnp.where(lanei == WINDOW - 1, vn_col, pltpu.roll(vbuf_ref[b], WINDOW - 1, axis=1))


def _decode_attn(proj, kbuf, vbuf, mk, mv, rel_bias, sinks, sqg, skg, mqg):
    n = proj.shape[0]
    bkt = jnp.asarray(_decode_bucket_table())
    smem = pl.BlockSpec(memory_space=pltpu.SMEM)
    win = pl.BlockSpec((SEQ_TILE, WINDOW, SWA_KV_WIDTH), lambda i: (i, 0, 0))
    memb = pl.BlockSpec((SEQ_TILE, N_MEM, MEM_WIDTH), lambda i: (i, 0, 0))
    return pl.pallas_call(
        _decode_attn_kernel,
        grid=(n // SEQ_TILE,),
        in_specs=[pl.BlockSpec((SEQ_TILE, IN_PROJ), lambda i: (i, 0)), win, win, memb, memb,
                  _full((8, WINDOW)), smem, smem, _full((1, HEAD_DIM)), _full((1, HEAD_DIM)), _full((1, HEAD_DIM))],
        out_specs=[pl.BlockSpec((SEQ_TILE, SWA_WIDTH), lambda i: (i, 0)),
                   pl.BlockSpec((SEQ_TILE, MEM_WIDTH), lambda i: (i, 0)), win, win],
        out_shape=[jax.ShapeDtypeStruct((n, SWA_WIDTH), F32), jax.ShapeDtypeStruct((n, MEM_WIDTH), F32),
                   jax.ShapeDtypeStruct(kbuf.shape, F32), jax.ShapeDtypeStruct(vbuf.shape, F32)],
        scratch_shapes=[pltpu.VMEM((2, 8, WINDOW), F32)],
        compiler_params=_params("arbitrary"),
        name="decode_attn",
    )(proj, kbuf, vbuf, mk, mv, bkt, rel_bias, sinks, sqg, skg, mqg)


FF_CHUNK = 1024
OUT_FFN_TILE = 512


def _out_ffn_kernel(x_ref, yr_ref, ys_ref, ym_ref, wo_ref, g2_ref, w1_ref, w2_ref, o_ref):
    x1 = (x_ref[...]
          + jnp.dot(yr_ref[...].astype(BF16), wo_ref[0:RWKV_WIDTH, :], preferred_element_type=F32)
          + jnp.dot(ys_ref[...].astype(BF16), wo_ref[RWKV_WIDTH:RWKV_WIDTH + SWA_WIDTH, :],
                    preferred_element_type=F32)
          + jnp.dot(ym_ref[...].astype(BF16), wo_ref[RWKV_WIDTH + SWA_WIDTH:, :], preferred_element_type=F32))
    h2 = _rms(x1, g2_ref[...]).astype(BF16)
    ff = None
    for c in range(D_FF // FF_CHUNK):
        u = jnp.dot(h2, w1_ref[:, c * FF_CHUNK:(c + 1) * FF_CHUNK], preferred_element_type=F32)
        u = jnp.square(jnp.maximum(u, 0.0)).astype(BF16)
        d = jnp.dot(u, w2_ref[c * FF_CHUNK:(c + 1) * FF_CHUNK, :], preferred_element_type=F32)
        ff = d if ff is None else ff + d
    o_ref[...] = x1 + ff


def _out_ffn(x2d, yr, ys, ym, wo, g2, w1, w2, tm):
    n = x2d.shape[0]
    rows = lambda w: pl.BlockSpec((tm, w), lambda i: (i, 0))
    const = lambda shape: pl.BlockSpec(shape, lambda i: (0, 0), pipeline_mode=pl.Buffered(1))
    return pl.pallas_call(
        _out_ffn_kernel,
        grid=(n // tm,),
        in_specs=[rows(D_MODEL), rows(RWKV_WIDTH), rows(SWA_WIDTH), rows(MEM_WIDTH),
                  const((D_MODEL, D_MODEL)), _full((1, D_MODEL)), const((D_MODEL, D_FF)), const((D_FF, D_MODEL))],
        out_specs=rows(D_MODEL),
        out_shape=jax.ShapeDtypeStruct((n, D_MODEL), F32),
        compiler_params=_params("arbitrary"),
        name="out_ffn",
    )(x2d, yr, ys, ym, wo, g2, w1, w2)


def kernel(x_prompt, x_sample, state_rwkv, state_shift, cache_swa_k, cache_swa_v, cache_mem_k, cache_mem_v,
           mem_prompt, rel_bias, norm1_g, w_in, mu_shift, w0, w_up_w, a0, w_up_a, w_up_g, k_k, k_a, r_k,
           lnx_w, lnx_b, q_norm_swa, k_norm_swa, sinks, mem_norm_g, w_mem_kv, q_norm_mem, k_norm_mem,
           w_out, norm2_g, w_ff1, w_ff2):
    B, T, _ = x_prompt.shape
    Bd = x_sample.shape[0]
    l = 0
    w_in_b = w_in[l].astype(BF16)
    w_out_b = w_out[l].astype(BF16)
    w1_b = w_ff1[l].astype(BF16)
    w2_b = w_ff2[l].astype(BF16)
    rwkv_params = (mu_shift[l][None], w0[l][None], a0[l][None], k_k[l][None], k_a[l][None],
                   r_k[l].reshape(1, RWKV_WIDTH), lnx_w[l][None], lnx_b[l][None],
                   w_up_w[l].astype(BF16), w_up_a[l].astype(BF16), w_up_g[l].astype(BF16))
    sqg, skg, mqg, mkg = q_norm_swa[l][None], k_norm_swa[l][None], q_norm_mem[l][None], k_norm_mem[l][None]
    g1, g2 = norm1_g[l][None], norm2_g[l][None]

    xp = x_prompt.reshape(B * T, D_MODEL)
    proj_p = _in_proj(xp, g1, w_in_b, IN_PROJ_TILE).reshape(B, T, IN_PROJ)
    mk, mv = _memory_kv(mem_prompt, mem_norm_g[l][None], w_mem_kv[l].astype(BF16), mkg)
    yr_p, s_p = _rwkv_prompt(proj_p, *rwkv_params)
    ys_p, ym_p, kn_p = _attn_prompt(proj_p, mk, mv, rel_bias, sinks[l][None], sqg, skg, mqg)
    y_p = _out_ffn(xp, yr_p.reshape(B * T, RWKV_WIDTH), ys_p.reshape(B * T, SWA_WIDTH),
                   ym_p.reshape(B * T, MEM_WIDTH), w_out_b, g2, w1_b, w2_b, OUT_FFN_TILE).reshape(B, T, D_MODEL)
    shift_p = proj_p[:, T - 1, :RWKV_PROJ]
    vb_p = proj_p[:, T - WINDOW:, COL_SV:COL_SV + SWA_KV_WIDTH]

    xs = x_sample.reshape(Bd, D_MODEL)
    proj_s = _in_proj(xs, g1, w_in_b, Bd)
    yr_s, st_s = _rwkv_step(proj_s[:, :RWKV_PROJ], state_shift[l], jnp.transpose(state_rwkv[l], (1, 2, 3, 0)),
                            *rwkv_params)
    s_s = jnp.transpose(st_s, (3, 0, 1, 2))
    fmajor = lambda c: jnp.transpose(c, (0, 2, 3, 1)).reshape(Bd, c.shape[2] * HEAD_DIM, c.shape[1])
    ys_s, ym_s, kb_s, vb_s = _decode_attn(
        proj_s, fmajor(cache_swa_k[l]), fmajor(cache_swa_v[l]), fmajor(cache_mem_k[l]), fmajor(cache_mem_v[l]),
        rel_bias, sinks[l][None], sqg, skg, mqg)
    pmajor = lambda c: jnp.transpose(c.reshape(Bd, N_SWA_KV_HEADS, HEAD_DIM, WINDOW), (0, 3, 1, 2))[None]
    y_s = _out_ffn(xs, yr_s, ys_s, ym_s, w_out_b, g2, w1_b, w2_b, Bd).reshape(Bd, 1, D_MODEL)

    return (y_p, y_s,
            s_p[None], shift_p[None],
            kn_p.reshape(1, B, WINDOW, N_SWA_KV_HEADS, HEAD_DIM),
            vb_p.reshape(1, B, WINDOW, N_SWA_KV_HEADS, HEAD_DIM),
            mk.reshape(1, B, N_MEM, N_MEM_HEADS, HEAD_DIM), mv.reshape(1, B, N_MEM, N_MEM_HEADS, HEAD_DIM),
            s_s[None], proj_s[:, :RWKV_PROJ][None],
            pmajor(kb_s), pmajor(vb_s))
```

```python
import math

import numpy as np
import jax
import jax.numpy as jnp
from jax import lax
from jax.experimental import pallas as pl
from jax.experimental.pallas import tpu as pltpu

F32 = jnp.float32
BF16 = jnp.bfloat16

D_MODEL = 1024
HEAD_DIM = 64
RWKV_WIDTH = 512
N_RWKV_HEADS = 8
SWA_WIDTH = 256
N_SWA_HEADS = 4
N_SWA_KV_HEADS = 2
SWA_KV_WIDTH = 128
MEM_WIDTH = 256
N_MEM_HEADS = 4
N_MEM = 256
WINDOW = 128
BLOCK = 128
N_BUCKETS = 32
MAX_DISTANCE = 128
DECAY_LORA = 64
AAA_LORA = 64
GATE_LORA = 128
RWKV_PROJ = 3 * RWKV_WIDTH + DECAY_LORA + AAA_LORA + GATE_LORA
SWA_PROJ = SWA_WIDTH + 2 * SWA_KV_WIDTH
IN_PROJ = RWKV_PROJ + SWA_PROJ + MEM_WIDTH
D_FF = 4 * D_MODEL
NORM_EPS = 1e-6
LNX_EPS = 64e-5
ATTN_SCALE = HEAD_DIM ** -0.5
EXP_M05 = math.exp(-0.5)
LOG2E = math.log2(math.e)
NEG = -1e30

COL_R, COL_K, COL_V = 0, RWKV_WIDTH, 2 * RWKV_WIDTH
COL_WD = 3 * RWKV_WIDTH
COL_AD = COL_WD + DECAY_LORA
COL_GD = COL_AD + AAA_LORA
COL_SQ = RWKV_PROJ
COL_SK = COL_SQ + SWA_WIDTH
COL_SV = COL_SK + SWA_KV_WIDTH
COL_MQ = RWKV_PROJ + SWA_PROJ

CHUNK = 64
VMEM_LIMIT = 56 * 1024 * 1024


def _dot(a, b):
    return jnp.dot(a.astype(BF16), b.astype(BF16), preferred_element_type=F32)


def _dot_nt(a, b):
    return lax.dot_general(a.astype(BF16), b.astype(BF16), (((1,), (1,)), ((), ())),
                           preferred_element_type=F32)


def _dot_tn(a, b):
    return lax.dot_general(a.astype(BF16), b.astype(BF16), (((0,), (0,)), ((), ())),
                           preferred_element_type=F32)


def _rms(x, g):
    return x * lax.rsqrt(jnp.mean(x * x, -1, keepdims=True) + NORM_EPS) * g


def _params(*sem):
    return pltpu.CompilerParams(dimension_semantics=sem, vmem_limit_bytes=VMEM_LIMIT)


def _full(shape):
    n = len(shape)
    return pl.BlockSpec(shape, lambda *_: (0,) * n)


def _head_blocks(width):
    bi = lax.broadcasted_iota(jnp.int32, (width, width), 0) // HEAD_DIM
    bj = lax.broadcasted_iota(jnp.int32, (width, width), 1) // HEAD_DIM
    return jnp.where(bi == bj, 1.0, 0.0).astype(BF16)


IN_PROJ_TILE = 1024
IN_PROJ_SUB = 256


def _in_proj_kernel(x_ref, xs_ref, g_ref, w_ref, o_ref, os_ref):
    tm = x_ref.shape[0]
    for j in range(tm // IN_PROJ_SUB):
        rows = slice(j * IN_PROJ_SUB, (j + 1) * IN_PROJ_SUB)
        h = _rms(x_ref[rows, :], g_ref[...])
        o_ref[rows, :] = jnp.dot(h.astype(BF16), w_ref[...], preferred_element_type=F32)

    @pl.when(pl.program_id(0) == pl.num_programs(0) - 1)
    def _():
        hs = _rms(xs_ref[...], g_ref[...])
        os_ref[...] = jnp.dot(hs.astype(BF16), w_ref[...], preferred_element_type=F32)


def _in_proj(x2d, xs2d, g, w_bf16, tm):
    n, ns = x2d.shape[0], xs2d.shape[0]
    return pl.pallas_call(
        _in_proj_kernel,
        grid=(n // tm,),
        in_specs=[pl.BlockSpec((tm, D_MODEL), lambda i: (i, 0)),
                  _full((ns, D_MODEL)),
                  _full((1, D_MODEL)),
                  _full((D_MODEL, IN_PROJ))],
        out_specs=[pl.BlockSpec((tm, IN_PROJ), lambda i: (i, 0)), _full((ns, IN_PROJ))],
        out_shape=[jax.ShapeDtypeStruct((n, IN_PROJ), F32), jax.ShapeDtypeStruct((ns, IN_PROJ), F32)],
        compiler_params=_params("arbitrary"),
        name="in_proj",
    )(x2d, xs2d, g, w_bf16)


def _rwkv_features(xs, w0, a0, k_k, k_a, wupw, wupa, wupg):
    r = xs[:, COL_R:COL_R + RWKV_WIDTH]
    k = xs[:, COL_K:COL_K + RWKV_WIDTH]
    v = xs[:, COL_V:COL_V + RWKV_WIDTH]
    wd = xs[:, COL_WD:COL_WD + DECAY_LORA]
    ad = xs[:, COL_AD:COL_AD + AAA_LORA]
    gd = xs[:, COL_GD:COL_GD + GATE_LORA]
    logw = -jax.nn.sigmoid(w0 + _dot(jnp.tanh(wd), wupw)) * EXP_M05
    a_sig = jax.nn.sigmoid(a0 + _dot(ad, wupa))
    gate = _dot(jax.nn.sigmoid(gd), wupg)
    kk = k * k_k
    k2 = k * (1.0 + (a_sig - 1.0) * k_a)
    return r, k2, v, kk, a_sig, logw, gate


def _seg_sum(x, blk):
    xb = x.astype(BF16)
    half = RWKV_WIDTH // 2
    return jnp.concatenate([jnp.dot(xb[:, :half], blk, preferred_element_type=F32),
                            jnp.dot(xb[:, half:], blk, preferred_element_type=F32)], axis=1)


def _group_norm_out(y, bonus, gate, lnx_w, lnx_b, blk):
    inv_d = 1.0 / HEAD_DIM
    m = _seg_sum(y, blk) * inv_d
    d = y - m
    var = _seg_sum(d * d, blk) * inv_d
    yn = d * lax.rsqrt(var + LNX_EPS) * lnx_w + lnx_b
    return (yn + bonus) * gate


RWKV_TILE = 4 * CHUNK
PAIR = 2 * HEAD_DIM
N_PAIRS = N_RWKV_HEADS // 2


def _rwkv_prompt_kernel(p_ref, mu_ref, w0_ref, a0_ref, kk_ref, ka_ref, rk_ref, lnw_ref, lnb_ref,
                        wupw_ref, wupa_ref, wupg_ref, y_ref, sout_ref, s_scr, prev_scr):
    C, TT, D = CHUNK, RWKV_TILE, HEAD_DIM
    NC = TT // C
    t = pl.program_id(1)

    @pl.when(t == 0)
    def _():
        s_scr[...] = jnp.zeros_like(s_scr)
        prev_scr[...] = jnp.zeros_like(prev_scr)

    p = p_ref[0]
    row = lax.broadcasted_iota(jnp.int32, p.shape, 0)
    prev = jnp.where(row == 0, prev_scr[...], pltpu.roll(p, 1, axis=0))
    prev_scr[...] = p[TT - 1:TT, :]
    xs = p + (prev - p) * mu_ref[...]
    r, k2, v, kk, a_sig, logw, gate = _rwkv_features(
        xs, w0_ref[...], a0_ref[...], kk_ref[...], ka_ref[...], wupw_ref[...], wupa_ref[...], wupg_ref[...])

    blk = _head_blocks(RWKV_WIDTH // 2)
    kkn = kk / jnp.maximum(jnp.sqrt(_seg_sum(kk * kk, blk)), 1e-12)
    bb = kkn * a_sig

    ri = lax.broadcasted_iota(jnp.int32, (TT, TT), 0)
    ci = lax.broadcasted_iota(jnp.int32, (TT, TT), 1)
    tri = jnp.where(jnp.logical_and(ri >= ci, ri // C == ci // C), 1.0, 0.0).astype(BF16)
    lw2 = logw * LOG2E
    l1 = lw2.astype(BF16)
    rem = lw2 - l1.astype(F32)
    l2 = rem.astype(BF16)
    l3 = (rem - l2.astype(F32)).astype(BF16)
    cum = (jnp.dot(tri, l1, preferred_element_type=F32) + jnp.dot(tri, l2, preferred_element_type=F32)
           + jnp.dot(tri, l3, preferred_element_type=F32))
    c_last = jnp.concatenate([jnp.broadcast_to(cum[(c + 1) * C - 1:(c + 1) * C, :], (C, RWKV_WIDTH))
                              for c in range(NC)], axis=0)
    e_pos = jnp.exp2(cum)
    e_neg = jnp.exp2(-cum)
    e_prev = jnp.exp2(cum - lw2)
    e_last = jnp.exp2(c_last - cum)

    lo_full = (lax.broadcasted_iota(jnp.int32, (TT, RWKV_WIDTH), 1) % PAIR) < D
    at_f = -kkn * e_prev
    rt_f = r * e_pos
    at_lo = jnp.where(lo_full, at_f, 0.0).astype(BF16)
    at_hi = jnp.where(lo_full, 0.0, at_f).astype(BF16)
    rt_lo = jnp.where(lo_full, rt_f, 0.0).astype(BF16)
    rt_hi = jnp.where(lo_full, 0.0, rt_f).astype(BF16)
    bt_b = (bb * e_neg).astype(BF16)
    kt_b = (k2 * e_neg).astype(BF16)
    bh_b = (bb * e_last).astype(BF16)
    kh_b = (k2 * e_last).astype(BF16)
    v_b = v.astype(BF16)

    r2 = lax.broadcasted_iota(jnp.int32, (C, PAIR), 0)
    c2 = lax.broadcasted_iota(jnp.int32, (C, PAIR), 1)
    lo = c2 < D
    c2m = jnp.where(lo, c2, c2 - C)
    mask_a = jnp.logical_and(lo, c2 < r2)
    mask_ak = jnp.logical_and(jnp.logical_not(lo), c2m < r2)
    mask_r = c2m <= r2
    eye_hi = jnp.where(jnp.logical_and(jnp.logical_not(lo), c2m == r2), 1.0, 0.0)
    zeros_cp = jnp.zeros((C, PAIR), F32)
    qi = lax.broadcasted_iota(jnp.int32, (PAIR, PAIR), 0) // D
    qj = lax.broadcasted_iota(jnp.int32, (PAIR, PAIR), 1) // D
    diag = qi == qj

    PR = [(c, q) for c in range(NC) for q in range(N_PAIRS)]
    n = len(PR)
    E = range(2)
    win = lambda x, c, q: x[c * C:(c + 1) * C, q * PAIR:(q + 1) * PAIR]
    sc = [_dot_nt(jnp.concatenate([win(at_lo, c, q), win(at_hi, c, q), win(rt_lo, c, q), win(rt_hi, c, q)], axis=0),
                  jnp.concatenate([win(bt_b, c, q), win(kt_b, c, q)], axis=0)) for c, q in PR]
    vr = [pltpu.roll(win(v, c, q), D, axis=1) for c, q in PR]
    vvr = [jnp.concatenate([vr[i], vr[i]], axis=0).astype(BF16) for i in range(n)]
    m_ak = [[jnp.where(mask_ak, sc[i][e * C:(e + 1) * C], 0.0) for e in E] for i in range(n)]
    m_r = [[jnp.where(mask_r, sc[i][(2 + e) * C:(3 + e) * C], 0.0) for e in E] for i in range(n)]

    zf = [[_dot(m_ak[i][e], vvr[i]) for e in E] for i in range(n)]
    W = [[jnp.where(mask_a, sc[i][e * C:(e + 1) * C], eye_hi) for e in E] for i in range(n)]
    for k in range(int(math.log2(C))):
        Wb = [[W[i][e].astype(BF16) for e in E] for i in range(n)]
        AW = [[jnp.dot(Wb[i][e][:, :C], Wb[i][e], preferred_element_type=F32) for e in E] for i in range(n)]
        W = [[jnp.where(lo, 0.0, W[i][e]) + AW[i][e] for e in E] for i in range(n)]
    X = [[_dot(W[i][0], jnp.concatenate([zeros_cp, jnp.where(lo, win(at_f, c, q), zf[i][0])], axis=0)),
          _dot(W[i][1], jnp.concatenate([zeros_cp, jnp.where(lo, zf[i][1], win(at_f, c, q))], axis=0))]
         for i, (c, q) in enumerate(PR)]

    S = [s_scr[q] for q in range(N_PAIRS)]
    ys = []
    for c in range(NC):
        w_last = jnp.exp2(cum[(c + 1) * C - 1:(c + 1) * C, :])
        idx = [c * N_PAIRS + q for q in range(N_PAIRS)]
        mkg = [[_dot_tn(X[i][e], win(bh_b, c, q)) for e in E] for q, i in enumerate(idx)]
        vk = [_dot_tn(win(v_b, c, q), win(kh_b, c, q)) for q, i in enumerate(idx)]
        ry = [[_dot(m_r[i][0], jnp.concatenate([X[i][0], jnp.where(lo, 0.0, vr[i])], axis=0)),
               _dot(m_r[i][1], jnp.concatenate([X[i][1], jnp.where(lo, vr[i], 0.0)], axis=0))] for i in idx]
        rp = [win(rt_f, c, q) + jnp.where(lo, ry[q][0], ry[q][1]) for q in range(N_PAIRS)]
        y0 = [pltpu.roll(jnp.where(lo, ry[q][1], ry[q][0]), D, axis=1) for q in range(N_PAIRS)]
        mk = [jnp.where(diag, jnp.concatenate([mkg[q][0][:D], mkg[q][1][D:]], axis=0), 0.0) for q in range(N_PAIRS)]
        g = [jnp.where(diag, vk[q] + jnp.concatenate([mkg[q][0][D:], mkg[q][1][:D]], axis=0), 0.0)
             for q in range(N_PAIRS)]
        y = [_dot_nt(rp[q], S[q]) for q in range(N_PAIRS)]
        dS = [_dot(S[q], mk[q]) for q in range(N_PAIRS)]
        S = [S[q] * w_last[:, q * PAIR:(q + 1) * PAIR] + dS[q] + g[q] for q in range(N_PAIRS)]
        ys.append(jnp.concatenate([y[q] + y0[q] for q in range(N_PAIRS)], axis=1))
    for q in range(N_PAIRS):
        s_scr[q] = S[q]

    bonus = _seg_sum(r * k2 * rk_ref[...], blk) * v
    y_ref[0] = _group_norm_out(jnp.concatenate(ys, axis=0), bonus, gate, lnw_ref[...], lnb_ref[...], blk)

    @pl.when(t == pl.num_programs(1) - 1)
    def _():
        for q in range(N_PAIRS):
            sout_ref[0, 2 * q] = S[q][:D, :D]
            sout_ref[0, 2 * q + 1] = S[q][D:, D:]


def _rwkv_prompt(proj, mu, w0, a0, k_k, k_a, r_k, lnx_w, lnx_b, wupw, wupa, wupg):
    B, T, _ = proj.shape
    vec = lambda n: _full((1, n))
    return pl.pallas_call(
        _rwkv_prompt_kernel,
        grid=(B, T // RWKV_TILE),
        in_specs=[pl.BlockSpec((1, RWKV_TILE, RWKV_PROJ), lambda b, t: (b, t, 0)),
                  vec(RWKV_PROJ), vec(RWKV_WIDTH), vec(RWKV_WIDTH), vec(RWKV_WIDTH), vec(RWKV_WIDTH),
                  vec(RWKV_WIDTH), vec(RWKV_WIDTH), vec(RWKV_WIDTH),
                  _full((DECAY_LORA, RWKV_WIDTH)), _full((AAA_LORA, RWKV_WIDTH)), _full((GATE_LORA, RWKV_WIDTH))],
        out_specs=[pl.BlockSpec((1, RWKV_TILE, RWKV_WIDTH), lambda b, t: (b, t, 0)),
                   pl.BlockSpec((1, N_RWKV_HEADS, HEAD_DIM, HEAD_DIM), lambda b, t: (b, 0, 0, 0))],
        out_shape=[jax.ShapeDtypeStruct((B, T, RWKV_WIDTH), F32),
                   jax.ShapeDtypeStruct((B, N_RWKV_HEADS, HEAD_DIM, HEAD_DIM), F32)],
        scratch_shapes=[pltpu.VMEM((N_PAIRS, PAIR, PAIR), F32),
                        pltpu.VMEM((1, RWKV_PROJ), F32)],
        compiler_params=_params("arbitrary", "arbitrary"),
        name="rwkv_prompt",
    )(proj, mu, w0, a0, k_k, k_a, r_k, lnx_w, lnx_b, wupw, wupa, wupg)


def _rwkv_step_feat_kernel(p_ref, sh_ref, mu_ref, w0_ref, a0_ref, kk_ref, ka_ref, rk_ref,
                           wupw_ref, wupa_ref, wupg_ref, vecs_ref, bonus_ref, gate_ref):
    p = p_ref[...]
    xs = p + (sh_ref[...] - p) * mu_ref[...]
    r, k2, v, kk, a_sig, logw, gate = _rwkv_features(
        xs, w0_ref[...], a0_ref[...], kk_ref[...], ka_ref[...], wupw_ref[...], wupa_ref[...], wupg_ref[...])
    blk = _head_blocks(RWKV_WIDTH // 2)
    kkn = kk / jnp.maximum(jnp.sqrt(_seg_sum(kk * kk, blk)), 1e-12)
    for i, x in enumerate((-kkn, kkn * a_sig, jnp.exp(logw), k2, r, v)):
        vecs_ref[i] = x.T
    bonus_ref[...] = _seg_sum(r * k2 * rk_ref[...], blk) * v
    gate_ref[...] = gate


def _rwkv_step_state_kernel(s_ref, vecs_ref, sout_ref, y_ref):
    S = s_ref[0]
    a, b, w, k, r, v = (vecs_ref[i] for i in range(6))
    sa = jnp.sum(S * a[None], axis=1)
    S = S * w[None] + sa[:, None, :] * b[None] + v[:, None, :] * k[None]
    sout_ref[0] = S
    y_ref[...] = jnp.sum(S * r[None], axis=1)


def _rwkv_step_out_kernel(yt_ref, bonus_ref, gate_ref, lnw_ref, lnb_ref, o_ref):
    o_ref[...] = _group_norm_out(yt_ref[...].T, bonus_ref[...], gate_ref[...], lnw_ref[...], lnb_ref[...],
                                 _head_blocks(RWKV_WIDTH // 2))


def _rwkv_step(proj, shift, state_t, mu, w0, a0, k_k, k_a, r_k, lnx_w, lnx_b, wupw, wupa, wupg):
    n = proj.shape[0]
    vec = lambda m: _full((1, m))
    rows = _full((n, RWKV_PROJ))
    wide = _full((n, RWKV_WIDTH))
    vecs, bonus, gate = pl.pallas_call(
        _rwkv_step_feat_kernel,
        grid=(1,),
        in_specs=[rows, rows, vec(RWKV_PROJ), vec(RWKV_WIDTH), vec(RWKV_WIDTH), vec(RWKV_WIDTH), vec(RWKV_WIDTH),
                  vec(RWKV_WIDTH),
                  _full((DECAY_LORA, RWKV_WIDTH)), _full((AAA_LORA, RWKV_WIDTH)), _full((GATE_LORA, RWKV_WIDTH))],
        out_specs=[_full((6, RWKV_WIDTH, n)), wide, wide],
        out_shape=[jax.ShapeDtypeStruct((6, RWKV_WIDTH, n), F32),
                   jax.ShapeDtypeStruct((n, RWKV_WIDTH), F32), jax.ShapeDtypeStruct((n, RWKV_WIDTH), F32)],
        compiler_params=_params("arbitrary"),
        name="rwkv_step_feat",
    )(proj, shift, mu, w0, a0, k_k, k_a, r_k, wupw, wupa, wupg)
    st_spec = pl.BlockSpec((1, HEAD_DIM, HEAD_DIM, n), lambda h: (h, 0, 0, 0))
    state_new, yt = pl.pallas_call(
        _rwkv_step_state_kernel,
        grid=(N_RWKV_HEADS,),
        in_specs=[st_spec, pl.BlockSpec((6, HEAD_DIM, n), lambda h: (0, h, 0))],
        out_specs=[st_spec, pl.BlockSpec((HEAD_DIM, n), lambda h: (h, 0))],
        out_shape=[jax.ShapeDtypeStruct(state_t.shape, F32), jax.ShapeDtypeStruct((RWKV_WIDTH, n), F32)],
        compiler_params=_params("arbitrary"),
        name="rwkv_step_state",
    )(state_t, vecs)
    y = pl.pallas_call(
        _rwkv_step_out_kernel,
        grid=(1,),
        in_specs=[_full((RWKV_WIDTH, n)), wide, wide, vec(RWKV_WIDTH), vec(RWKV_WIDTH)],
        out_specs=wide,
        out_shape=jax.ShapeDtypeStruct((n, RWKV_WIDTH), F32),
        compiler_params=_params("arbitrary"),
        name="rwkv_step_out",
    )(yt, bonus, gate, lnx_w, lnx_b)
    return y, state_new


def _t5_bucket_np(dist):
    max_exact = N_BUCKETS // 2
    d = np.maximum(dist, 1).astype(np.float32)
    large = max_exact + (np.log(d / np.float32(max_exact)) / np.float32(math.log(MAX_DISTANCE / max_exact))
                         * np.float32(N_BUCKETS - max_exact)).astype(np.int32)
    large = np.minimum(large, N_BUCKETS - 1)
    return np.where(dist < max_exact, dist, large).astype(np.int32)


def _prompt_bucket_table():
    qi = np.arange(BLOCK)[:, None]
    kj = np.arange(2 * BLOCK)[None, :]
    dist = BLOCK + qi - kj
    valid = (dist >= 0) & (dist <= WINDOW)
    return np.where(valid, _t5_bucket_np(np.maximum(dist, 0)), -1).astype(np.int32)


def _decode_bucket_table():
    dist = WINDOW - np.arange(WINDOW)
    return np.broadcast_to(_t5_bucket_np(dist)[None, :], (8, WINDOW)).astype(np.int32).copy()


def _bias_from_buckets(bkt, relb_ref, h, init):
    acc = jnp.full(bkt.shape, init, F32)
    for b in range(N_BUCKETS):
        acc = jnp.where(bkt == b, relb_ref[b, h], acc)
    return acc


ATT_TILE = 4 * BLOCK
ATT_GROUP = 8


def _attn_prompt_kernel(qs_ref, qm_ref, kc_ref, kp_ref, vc_ref, vp_ref, mk_ref, mv_ref, bkt_ref, relb_ref, sink_ref,
                        sqg_ref, skg_ref, mqg_ref, ys_ref, ym_ref, kn_ref, bias_scr, mk_scr, mv_scr):
    i = pl.program_id(1)
    grp = N_SWA_HEADS // N_SWA_KV_HEADS
    ones = jnp.ones((2 * BLOCK, HEAD_DIM), F32)

    @pl.when(i == 0)
    def _():
        bkt = bkt_ref[...]
        for j in range(N_SWA_KV_HEADS):
            for g in range(grp):
                bias_scr[j, g * BLOCK:(g + 1) * BLOCK, :] = _bias_from_buckets(bkt, relb_ref, j * grp + g, NEG) * LOG2E
        mk = mk_ref[0]
        mv = mv_ref[0]
        for h in range(N_MEM_HEADS):
            sl = slice(h * HEAD_DIM, (h + 1) * HEAD_DIM)
            mk_scr[h] = mk[:, sl].astype(BF16)
            mv_scr[h] = jnp.concatenate([mv[:, sl], ones], axis=1).astype(BF16)

    qs = qs_ref[0]
    qm = qm_ref[0]
    kc = kc_ref[0]
    kp = kp_ref[0]
    vc = vc_ref[0]
    vp = vp_ref[0]
    sqg, skg, mqg = sqg_ref[...], skg_ref[...], mqg_ref[...]
    rowi = lax.broadcasted_iota(jnp.int32, (2 * BLOCK, 1), 0)
    col = lax.broadcasted_iota(jnp.int32, (2 * BLOCK, 2 * BLOCK), 1)
    pad_mask = jnp.logical_and(i == 0, col < BLOCK)

    hsl = [slice(h * HEAD_DIM, (h + 1) * HEAD_DIM) for h in range(N_SWA_HEADS)]
    qs_n = _rms_heads(qs, sqg, N_SWA_HEADS) * (ATTN_SCALE * LOG2E)
    qm_n = _rms_heads(qm, mqg, N_MEM_HEADS) * (ATTN_SCALE * LOG2E)
    kc_n = _rms_heads(kc, skg, N_SWA_KV_HEADS)
    kp_n = _rms_heads(kp, skg, N_SWA_KV_HEADS)
    kn_ref[0] = kc_n[ATT_TILE - BLOCK:]
    chains = [(a, j) for a in range(ATT_TILE // BLOCK) for j in range(N_SWA_KV_HEADS)]
    lhs, keys, vals, sinkcol = [], [], [], []
    for a, j in chains:
        rs = slice(a * BLOCK, (a + 1) * BLOCK)
        lhs.append(jnp.concatenate([qs_n[rs, hsl[j * grp + g]] for g in range(grp)], axis=0))
        if a == 0:
            keys.append(jnp.concatenate([kp_n[:, hsl[j]], kc_n[:BLOCK, hsl[j]]], axis=0))
            vv = jnp.concatenate([vp[:, hsl[j]], vc[:BLOCK, hsl[j]]], axis=0)
        else:
            ks = slice((a - 1) * BLOCK, (a + 1) * BLOCK)
            keys.append(kc_n[ks, hsl[j]])
            vv = vc[ks, hsl[j]]
        vals.append(jnp.concatenate([vv, ones], axis=1))
        sinkcol.append(jnp.where(rowi < BLOCK, sink_ref[0, j * grp], sink_ref[0, j * grp + 1]) * LOG2E)
    qmn = [qm_n[:, hsl[h]] for h in range(N_MEM_HEADS)]

    bias = [bias_scr[j] for j in range(N_SWA_KV_HEADS)]
    mem_k = [mk_scr[h] for h in range(N_MEM_HEADS)]
    mem_v = [mv_scr[h] for h in range(N_MEM_HEADS)]
    n_w = len(chains)
    n_tasks = n_w + N_MEM_HEADS

    def scores(t):
        return _dot_nt(lhs[t], keys[t]) if t < n_w else _dot_nt(qmn[t - n_w], mem_k[t - n_w])

    def finish(ts, s):
        e, extra = {}, {}
        for t in ts:
            if t < n_w:
                a, j = chains[t]
                st = s[t] + bias[j]
                if a == 0:
                    st = jnp.where(pad_mask, NEG, st)
                m = jnp.maximum(jnp.max(st, -1, keepdims=True), sinkcol[t])
                e[t], extra[t] = jnp.exp2(st - m), jnp.exp2(sinkcol[t] - m)
            else:
                e[t], extra[t] = jnp.exp2(s[t] - jnp.max(s[t], -1, keepdims=True)), 0.0
        o_full = {t: _dot(e[t], vals[t] if t < n_w else mem_v[t - n_w]) for t in ts}
        return {t: o_full[t][:, :HEAD_DIM] / (o_full[t][:, HEAD_DIM:HEAD_DIM + 1] + extra[t]) for t in ts}

    groups = [list(range(g, min(g + ATT_GROUP, n_tasks))) for g in range(0, n_tasks, ATT_GROUP)]
    s, outs = {}, {}
    for gi, ts in enumerate(groups):
        if gi == 0:
            s.update({t: scores(t) for t in ts})
        if gi + 1 < len(groups):
            s.update({t: scores(t) for t in groups[gi + 1]})
        outs.update(finish(ts, s))
    for t, (a, j) in enumerate(chains):
        for g in range(grp):
            ys_ref[0, a * BLOCK:(a + 1) * BLOCK, hsl[j * grp + g]] = outs[t][g * BLOCK:(g + 1) * BLOCK]
    for h in range(N_MEM_HEADS):
        ym_ref[0, :, hsl[h]] = outs[n_w + h]


def _attn_prompt(proj, mk, mv, rel_bias, sinks, sqg, skg, mqg):
    B, T, _ = proj.shape
    bkt = jnp.asarray(_prompt_bucket_table())
    smem = pl.BlockSpec(memory_space=pltpu.SMEM)
    kblk, vblk = COL_SK // SWA_KV_WIDTH, COL_SV // SWA_KV_WIDTH
    prev = lambda b, i: (b, jnp.maximum((ATT_TILE // BLOCK) * i - 1, 0))
    memb = pl.BlockSpec((1, N_MEM, MEM_WIDTH), lambda b, i: (b, 0, 0))
    return pl.pallas_call(
        _attn_prompt_kernel,
        grid=(B, T // ATT_TILE),
        in_specs=[pl.BlockSpec((1, ATT_TILE, SWA_WIDTH), lambda b, i: (b, i, COL_SQ // SWA_WIDTH)),
                  pl.BlockSpec((1, ATT_TILE, MEM_WIDTH), lambda b, i: (b, i, COL_MQ // MEM_WIDTH)),
                  pl.BlockSpec((1, ATT_TILE, SWA_KV_WIDTH), lambda b, i: (b, i, kblk)),
                  pl.BlockSpec((1, BLOCK, SWA_KV_WIDTH), lambda b, i: prev(b, i) + (kblk,)),
                  pl.BlockSpec((1, ATT_TILE, SWA_KV_WIDTH), lambda b, i: (b, i, vblk)),
                  pl.BlockSpec((1, BLOCK, SWA_KV_WIDTH), lambda b, i: prev(b, i) + (vblk,)),
                  memb, memb,
                  _full((BLOCK, 2 * BLOCK)), smem, smem,
                  _full((1, HEAD_DIM)), _full((1, HEAD_DIM)), _full((1, HEAD_DIM))],
        out_specs=[pl.BlockSpec((1, ATT_TILE, SWA_WIDTH), lambda b, i: (b, i, 0)),
                   pl.BlockSpec((1, ATT_TILE, MEM_WIDTH), lambda b, i: (b, i, 0)),
                   pl.BlockSpec((1, BLOCK, SWA_KV_WIDTH), lambda b, i: (b, 0, 0))],
        out_shape=[jax.ShapeDtypeStruct((B, T, SWA_WIDTH), F32),
                   jax.ShapeDtypeStruct((B, T, MEM_WIDTH), F32),
                   jax.ShapeDtypeStruct((B, BLOCK, SWA_KV_WIDTH), F32)],
        scratch_shapes=[pltpu.VMEM((N_SWA_KV_HEADS, 2 * BLOCK, 2 * BLOCK), F32),
                        pltpu.VMEM((N_MEM_HEADS, N_MEM, HEAD_DIM), BF16),
                        pltpu.VMEM((N_MEM_HEADS, N_MEM, 2 * HEAD_DIM), BF16)],
        compiler_params=_params("arbitrary", "arbitrary"),
        name="attn_prompt",
    )(proj, proj, proj, proj, proj, proj, mk, mv, bkt, rel_bias, sinks, sqg, skg, mqg)


def _memory_kv_kernel(mem_ref, g_ref, w_ref, kg_ref, mk_ref, mv_ref):
    kv = jnp.dot(_rms(mem_ref[0], g_ref[...]).astype(BF16), w_ref[...], preferred_element_type=F32)
    kg = kg_ref[...]
    for h in range(N_MEM_HEADS):
        sl = slice(h * HEAD_DIM, (h + 1) * HEAD_DIM)
        mk_ref[0, :, sl] = _rms(kv[:, sl], kg)
    mv_ref[0] = kv[:, MEM_WIDTH:]


def _memory_kv(mem, g, w_bf16, kg):
    B = mem.shape[0]
    blk = pl.BlockSpec((1, N_MEM, MEM_WIDTH), lambda b: (b, 0, 0))
    return pl.pallas_call(
        _memory_kv_kernel,
        grid=(B,),
        in_specs=[pl.BlockSpec((1, N_MEM, D_MODEL), lambda b: (b, 0, 0)),
                  _full((1, D_MODEL)), _full((D_MODEL, 2 * MEM_WIDTH)), _full((1, HEAD_DIM))],
        out_specs=[blk, blk],
        out_shape=[jax.ShapeDtypeStruct((B, N_MEM, MEM_WIDTH), F32)] * 2,
        compiler_params=_params("arbitrary"),
        name="memory_kv",
    )(mem, g, w_bf16, kg)


SEQ_TILE = 8


def _rms_heads(x, g, n_heads):
    ms = jnp.dot((x * x).astype(BF16), _head_blocks(n_heads * HEAD_DIM), preferred_element_type=F32)
    return x * lax.rsqrt(ms * (1.0 / HEAD_DIM) + NORM_EPS) * jnp.concatenate([g] * n_heads, axis=1)


def _decode_attn_kernel(p_ref, kbuf_ref, vbuf_ref, mk_ref, mv_ref, bkt_ref, relb_ref, sink_ref,
                        sqg_ref, skg_ref, mqg_ref, ys_ref, ym_ref, kout_ref, vout_ref, tab_scr):
    grp = N_SWA_HEADS // N_SWA_KV_HEADS

    @pl.when(pl.program_id(0) == 0)
    def _():
        hrow = lax.broadcasted_iota(jnp.int32, (8, WINDOW), 0)
        lane = lax.broadcasted_iota(jnp.int32, (8, WINDOW), 1)
        bias_w = jnp.zeros((8, WINDOW), F32)
        cols = jnp.zeros((8, WINDOW), F32)
        for h in range(N_SWA_HEADS):
            bias_w = jnp.where(hrow == h, _bias_from_buckets(bkt_ref[...], relb_ref, h, 0.0), bias_w)
            cols = jnp.where(jnp.logical_and(hrow == h, lane == 0), relb_ref[0, h], cols)
            cols = jnp.where(jnp.logical_and(hrow == h, lane == 1), sink_ref[0, h], cols)
        tab_scr[0] = bias_w
        tab_scr[1] = cols

    bias_w = tab_scr[0][:N_SWA_HEADS]
    bias_new = tab_scr[1][:N_SWA_HEADS, 0:1]
    sink = tab_scr[1][:N_SWA_HEADS, 1:2]
    rowi = lax.broadcasted_iota(jnp.int32, (SWA_KV_WIDTH, WINDOW), 0)
    lanei = lax.broadcasted_iota(jnp.int32, (SWA_KV_WIDTH, WINDOW), 1)
    eye = rowi == lanei
    NH = N_SWA_HEADS
    own = (lax.broadcasted_iota(jnp.int32, (NH, NH * HEAD_DIM), 1) // HEAD_DIM
           == lax.broadcasted_iota(jnp.int32, (NH, NH * HEAD_DIM), 0))

    p = p_ref[...]
    qn = _rms_heads(p[:, COL_SQ:COL_SQ + SWA_WIDTH], sqg_ref[...], N_SWA_HEADS) * ATTN_SCALE
    kn = _rms_heads(p[:, COL_SK:COL_SK + SWA_KV_WIDTH], skg_ref[...], N_SWA_KV_HEADS)
    vn = p[:, COL_SV:COL_SV + SWA_KV_WIDTH]
    qmn = _rms_heads(p[:, COL_MQ:COL_MQ + MEM_WIDTH], mqg_ref[...], N_MEM_HEADS) * ATTN_SCALE
    rep = lambda x: jnp.concatenate([x[:, j * HEAD_DIM:(j + 1) * HEAD_DIM] for j in range(N_SWA_KV_HEADS)
                                     for _ in range(grp)], axis=1)
    kn_rep, vn_rep = rep(kn), rep(vn)

    B = range(SEQ_TILE)
    dup = lambda c: jnp.concatenate([c[j * HEAD_DIM:(j + 1) * HEAD_DIM] for j in range(N_SWA_KV_HEADS)
                                     for _ in range(grp)], axis=0)
    qd = [jnp.where(own, qn[b:b + 1, :], 0.0) for b in B]
    qmd = [jnp.where(own, qmn[b:b + 1, :], 0.0) for b in B]
    kdup = [dup(kbuf_ref[b]) for b in B]
    vdup = [dup(vbuf_ref[b]) for b in B]
    s = [_dot(qd[b], kdup[b]) + bias_w for b in B]
    sm = [_dot(qmd[b], mk_ref[b]) for b in B]
    s_new = [jnp.sum(qd[b] * kn_rep[b:b + 1, :], -1, keepdims=True) + bias_new for b in B]
    m = [jnp.maximum(jnp.maximum(jnp.max(s[b], -1, keepdims=True), s_new[b]), sink) for b in B]
    e = [jnp.exp(s[b] - m[b]) for b in B]
    e_new = [jnp.exp(s_new[b] - m[b]) for b in B]
    den = [jnp.sum(e[b], -1, keepdims=True) + e_new[b] + jnp.exp(sink - m[b]) for b in B]
    em = [jnp.exp(sm[b] - jnp.max(sm[b], -1, keepdims=True)) for b in B]
    ov = [_dot_nt(e[b], vdup[b]) for b in B]
    omf = [_dot_nt(em[b], mv_ref[b]) for b in B]
    ys_rows = [jnp.sum(jnp.where(own, (ov[b] + e_new[b] * vn_rep[b:b + 1, :]) / den[b], 0.0), 0, keepdims=True)
               for b in B]
    ym_rows = [jnp.sum(jnp.where(own, omf[b] / jnp.sum(em[b], -1, keepdims=True), 0.0), 0, keepdims=True)
               for b in B]
    ys_ref[...] = jnp.concatenate(ys_rows, axis=0)
    ym_ref[...] = jnp.concatenate(ym_rows, axis=0)
    for b in B:
        kn_col = jnp.sum(jnp.where(eye, kn[b:b + 1, :], 0.0), -1, keepdims=True)
        vn_col = jnp.sum(jnp.where(eye, vn[b:b + 1, :], 0.0), -1, keepdims=True)
        kout_ref[b] = jnp.where(lanei == WINDOW - 1, kn_col, pltpu.roll(kbuf_ref[b], WINDOW - 1, axis=1))
        vout_ref[b] = jnp.where(lanei == WINDOW - 1, vn_col, pltpu.roll(vbuf_ref[b], WINDOW - 1, axis=1))


def _decode_attn(proj, kbuf, vbuf, mk, mv, rel_bias, sinks, sqg, skg, mqg):
    n = proj.shape[0]
    bkt = jnp.asarray(_decode_bucket_table())
    smem = pl.BlockSpec(memory_space=pltpu.SMEM)
    win = pl.BlockSpec((SEQ_TILE, WINDOW, SWA_KV_WIDTH), lambda i: (i, 0, 0))
    memb = pl.BlockSpec((SEQ_TILE, N_MEM, MEM_WIDTH), lambda i: (i, 0, 0))
    return pl.pallas_call(
        _decode_attn_kernel,
        grid=(n // SEQ_TILE,),
        in_specs=[pl.BlockSpec((SEQ_TILE, IN_PROJ), lambda i: (i, 0)), win, win, memb, memb,
                  _full((8, WINDOW)), smem, smem, _full((1, HEAD_DIM)), _full((1, HEAD_DIM)), _full((1, HEAD_DIM))],
        out_specs=[pl.BlockSpec((SEQ_TILE, SWA_WIDTH), lambda i: (i, 0)),
                   pl.BlockSpec((SEQ_TILE, MEM_WIDTH), lambda i: (i, 0)), win, win],
        out_shape=[jax.ShapeDtypeStruct((n, SWA_WIDTH), F32), jax.ShapeDtypeStruct((n, MEM_WIDTH), F32),
                   jax.ShapeDtypeStruct(kbuf.shape, F32), jax.ShapeDtypeStruct(vbuf.shape, F32)],
        scratch_shapes=[pltpu.VMEM((2, 8, WINDOW), F32)],
        compiler_params=_params("arbitrary"),
        name="decode_attn",
    )(proj, kbuf, vbuf, mk, mv, bkt, rel_bias, sinks, sqg, skg, mqg)


FF_CHUNK = 1024
OUT_FFN_TILE = 512


def _out_ffn_rows(x, yr, ys, ym, wo_ref, g2_ref, w1_ref, w2_ref):
    x1 = (x
          + jnp.dot(yr.astype(BF16), wo_ref[0:RWKV_WIDTH, :], preferred_element_type=F32)
          + jnp.dot(ys.astype(BF16), wo_ref[RWKV_WIDTH:RWKV_WIDTH + SWA_WIDTH, :], preferred_element_type=F32)
          + jnp.dot(ym.astype(BF16), wo_ref[RWKV_WIDTH + SWA_WIDTH:, :], preferred_element_type=F32))
    h2 = _rms(x1, g2_ref[...]).astype(BF16)
    ff = None
    for c in range(D_FF // FF_CHUNK):
        u = jnp.dot(h2, w1_ref[:, c * FF_CHUNK:(c + 1) * FF_CHUNK], preferred_element_type=F32)
        u = jnp.square(jnp.maximum(u, 0.0)).astype(BF16)
        d = jnp.dot(u, w2_ref[c * FF_CHUNK:(c + 1) * FF_CHUNK, :], preferred_element_type=F32)
        ff = d if ff is None else ff + d
    return x1 + ff


def _out_ffn_kernel(x_ref, yr_ref, ys_ref, ym_ref, xs_ref, yrs_ref, yss_ref, yms_ref,
                    wo_ref, g2_ref, w1_ref, w2_ref, o_ref, os_ref):
    o_ref[...] = _out_ffn_rows(x_ref[...], yr_ref[...], ys_ref[...], ym_ref[...], wo_ref, g2_ref, w1_ref, w2_ref)

    @pl.when(pl.program_id(0) == pl.num_programs(0) - 1)
    def _():
        os_ref[...] = _out_ffn_rows(xs_ref[...], yrs_ref[...], yss_ref[...], yms_ref[...],
                                    wo_ref, g2_ref, w1_ref, w2_ref)


def _out_ffn(prompt, sample, wo, g2, w1, w2, tm):
    n, ns = prompt[0].shape[0], sample[0].shape[0]
    widths = (D_MODEL, RWKV_WIDTH, SWA_WIDTH, MEM_WIDTH)
    const = lambda shape: pl.BlockSpec(shape, lambda i: (0, 0), pipeline_mode=pl.Buffered(1))
    return pl.pallas_call(
        _out_ffn_kernel,
        grid=(n // tm,),
        in_specs=([pl.BlockSpec((tm, w), lambda i: (i, 0)) for w in widths] + [_full((ns, w)) for w in widths]
                  + [const((D_MODEL, D_MODEL)), _full((1, D_MODEL)), const((D_MODEL, D_FF)), const((D_FF, D_MODEL))]),
        out_specs=[pl.BlockSpec((tm, D_MODEL), lambda i: (i, 0)), _full((ns, D_MODEL))],
        out_shape=[jax.ShapeDtypeStruct((n, D_MODEL), F32), jax.ShapeDtypeStruct((ns, D_MODEL), F32)],
        compiler_params=_params("arbitrary"),
        name="out_ffn",
    )(*prompt, *sample, wo, g2, w1, w2)


def kernel(x_prompt, x_sample, state_rwkv, state_shift, cache_swa_k, cache_swa_v, cache_mem_k, cache_mem_v,
           mem_prompt, rel_bias, norm1_g, w_in, mu_shift, w0, w_up_w, a0, w_up_a, w_up_g, k_k, k_a, r_k,
           lnx_w, lnx_b, q_norm_swa, k_norm_swa, sinks, mem_norm_g, w_mem_kv, q_norm_mem, k_norm_mem,
           w_out, norm2_g, w_ff1, w_ff2):
    B, T, _ = x_prompt.shape
    Bd = x_sample.shape[0]
    l = 0
    w_in_b = w_in[l].astype(BF16)
    w_out_b = w_out[l].astype(BF16)
    w1_b = w_ff1[l].astype(BF16)
    w2_b = w_ff2[l].astype(BF16)
    rwkv_params = (mu_shift[l][None], w0[l][None], a0[l][None], k_k[l][None], k_a[l][None],
                   r_k[l].reshape(1, RWKV_WIDTH), lnx_w[l][None], lnx_b[l][None],
                   w_up_w[l].astype(BF16), w_up_a[l].astype(BF16), w_up_g[l].astype(BF16))
    sqg, skg, mqg, mkg = q_norm_swa[l][None], k_norm_swa[l][None], q_norm_mem[l][None], k_norm_mem[l][None]
    g1, g2 = norm1_g[l][None], norm2_g[l][None]

    xp = x_prompt.reshape(B * T, D_MODEL)
    xs = x_sample.reshape(Bd, D_MODEL)
    proj_p, proj_s = _in_proj(xp, xs, g1, w_in_b, IN_PROJ_TILE)
    proj_p = proj_p.reshape(B, T, IN_PROJ)

    mk, mv = _memory_kv(mem_prompt, mem_norm_g[l][None], w_mem_kv[l].astype(BF16), mkg)
    yr_p, s_p = _rwkv_prompt(proj_p, *rwkv_params)
    ys_p, ym_p, kn_p = _attn_prompt(proj_p, mk, mv, rel_bias, sinks[l][None], sqg, skg, mqg)
    shift_p = proj_p[:, T - 1, :RWKV_PROJ]
    vb_p = proj_p[:, T - WINDOW:, COL_SV:COL_SV + SWA_KV_WIDTH]

    yr_s, st_s = _rwkv_step(proj_s[:, :RWKV_PROJ], state_shift[l], jnp.transpose(state_rwkv[l], (1, 2, 3, 0)),
                            *rwkv_params)
    s_s = jnp.transpose(st_s, (3, 0, 1, 2))
    fmajor = lambda c: jnp.transpose(c, (0, 2, 3, 1)).reshape(Bd, c.shape[2] * HEAD_DIM, c.shape[1])
    ys_s, ym_s, kb_s, vb_s = _decode_attn(
        proj_s, fmajor(cache_swa_k[l]), fmajor(cache_swa_v[l]), fmajor(cache_mem_k[l]), fmajor(cache_mem_v[l]),
        rel_bias, sinks[l][None], sqg, skg, mqg)
    pmajor = lambda c: jnp.transpose(c.reshape(Bd, N_SWA_KV_HEADS, HEAD_DIM, WINDOW), (0, 3, 1, 2))[None]

    y_p, y_s = _out_ffn((xp, yr_p.reshape(B * T, RWKV_WIDTH), ys_p.reshape(B * T, SWA_WIDTH),
                         ym_p.reshape(B * T, MEM_WIDTH)), (xs, yr_s, ys_s, ym_s),
                        w_out_b, g2, w1_b, w2_b, OUT_FFN_TILE)
    y_p = y_p.reshape(B, T, D_MODEL)
    y_s = y_s.reshape(Bd, 1, D_MODEL)

    return (y_p, y_s,
            s_p[None], shift_p[None],
            kn_p.reshape(1, B, WINDOW, N_SWA_KV_HEADS, HEAD_DIM),
            vb_p.reshape(1, B, WINDOW, N_SWA_KV_HEADS, HEAD_DIM),
            mk.reshape(1, B, N_MEM, N_MEM_HEADS, HEAD_DIM), mv.reshape(1, B, N_MEM, N_MEM_HEADS, HEAD_DIM),
            s_s[None], proj_s[:, :RWKV_PROJ][None],
            pmajor(kb_s), pmajor(vb_s))
```

```python
import math

import numpy as np
import jax
import jax.numpy as jnp
from jax import lax
from jax.experimental import pallas as pl
from jax.experimental.pallas import tpu as pltpu

F32 = jnp.float32
BF16 = jnp.bfloat16

D_MODEL = 1024
HEAD_DIM = 64
RWKV_WIDTH = 512
N_RWKV_HEADS = 8
SWA_WIDTH = 256
N_SWA_HEADS = 4
N_SWA_KV_HEADS = 2
SWA_KV_WIDTH = 128
MEM_WIDTH = 256
N_MEM_HEADS = 4
N_MEM = 256
WINDOW = 128
BLOCK = 128
N_BUCKETS = 32
MAX_DISTANCE = 128
DECAY_LORA = 64
AAA_LORA = 64
GATE_LORA = 128
RWKV_PROJ = 3 * RWKV_WIDTH + DECAY_LORA + AAA_LORA + GATE_LORA
SWA_PROJ = SWA_WIDTH + 2 * SWA_KV_WIDTH
IN_PROJ = RWKV_PROJ + SWA_PROJ + MEM_WIDTH
D_FF = 4 * D_MODEL
NORM_EPS = 1e-6
LNX_EPS = 64e-5
ATTN_SCALE = HEAD_DIM ** -0.5
EXP_M05 = math.exp(-0.5)
LOG2E = math.log2(math.e)
NEG = -1e30

COL_R, COL_K, COL_V = 0, RWKV_WIDTH, 2 * RWKV_WIDTH
COL_WD = 3 * RWKV_WIDTH
COL_AD = COL_WD + DECAY_LORA
COL_GD = COL_AD + AAA_LORA
COL_SQ = RWKV_PROJ
COL_SK = COL_SQ + SWA_WIDTH
COL_SV = COL_SK + SWA_KV_WIDTH
COL_MQ = RWKV_PROJ + SWA_PROJ

CHUNK = 64
VMEM_LIMIT = 56 * 1024 * 1024


def _dot(a, b):
    return jnp.dot(a.astype(BF16), b.astype(BF16), preferred_element_type=F32)


def _dot_nt(a, b):
    return lax.dot_general(a.astype(BF16), b.astype(BF16), (((1,), (1,)), ((), ())),
                           preferred_element_type=F32)


def _dot_tn(a, b):
    return lax.dot_general(a.astype(BF16), b.astype(BF16), (((0,), (0,)), ((), ())),
                           preferred_element_type=F32)


def _rms(x, g):
    return x * lax.rsqrt(jnp.mean(x * x, -1, keepdims=True) + NORM_EPS) * g


def _params(*sem):
    return pltpu.CompilerParams(dimension_semantics=sem, vmem_limit_bytes=VMEM_LIMIT)


def _full(shape):
    n = len(shape)
    return pl.BlockSpec(shape, lambda *_: (0,) * n)


def _head_blocks(width):
    bi = lax.broadcasted_iota(jnp.int32, (width, width), 0) // HEAD_DIM
    bj = lax.broadcasted_iota(jnp.int32, (width, width), 1) // HEAD_DIM
    return jnp.where(bi == bj, 1.0, 0.0).astype(BF16)


IN_PROJ_TILE = 1024
IN_PROJ_SUB = 256


def _in_proj_kernel(x_ref, xs_ref, g_ref, w_ref, o_ref, os_ref):
    tm = x_ref.shape[0]
    for j in range(tm // IN_PROJ_SUB):
        rows = slice(j * IN_PROJ_SUB, (j + 1) * IN_PROJ_SUB)
        h = _rms(x_ref[rows, :], g_ref[...])
        o_ref[rows, :] = jnp.dot(h.astype(BF16), w_ref[...], preferred_element_type=F32).astype(o_ref.dtype)

    @pl.when(pl.program_id(0) == pl.num_programs(0) - 1)
    def _():
        hs = _rms(xs_ref[...], g_ref[...])
        os_ref[...] = jnp.dot(hs.astype(BF16), w_ref[...], preferred_element_type=F32)


def _in_proj(x2d, xs2d, g, w_bf16, tm):
    n, ns = x2d.shape[0], xs2d.shape[0]
    return pl.pallas_call(
        _in_proj_kernel,
        grid=(n // tm,),
        in_specs=[pl.BlockSpec((tm, D_MODEL), lambda i: (i, 0)),
                  _full((ns, D_MODEL)),
                  _full((1, D_MODEL)),
                  _full((D_MODEL, IN_PROJ))],
        out_specs=[pl.BlockSpec((tm, IN_PROJ), lambda i: (i, 0)), _full((ns, IN_PROJ))],
        out_shape=[jax.ShapeDtypeStruct((n, IN_PROJ), BF16), jax.ShapeDtypeStruct((ns, IN_PROJ), F32)],
        compiler_params=_params("arbitrary"),
        name="in_proj",
    )(x2d, xs2d, g, w_bf16)


def _rwkv_features(xs, w0, a0, k_k, k_a, wupw, wupa, wupg):
    r = xs[:, COL_R:COL_R + RWKV_WIDTH]
    k = xs[:, COL_K:COL_K + RWKV_WIDTH]
    v = xs[:, COL_V:COL_V + RWKV_WIDTH]
    wd = xs[:, COL_WD:COL_WD + DECAY_LORA]
    ad = xs[:, COL_AD:COL_AD + AAA_LORA]
    gd = xs[:, COL_GD:COL_GD + GATE_LORA]
    logw = -jax.nn.sigmoid(w0 + _dot(jnp.tanh(wd), wupw)) * EXP_M05
    a_sig = jax.nn.sigmoid(a0 + _dot(ad, wupa))
    gate = _dot(jax.nn.sigmoid(gd), wupg)
    kk = k * k_k
    k2 = k * (1.0 + (a_sig - 1.0) * k_a)
    return r, k2, v, kk, a_sig, logw, gate


def _seg_sum(x, blk):
    xb = x.astype(BF16)
    half = RWKV_WIDTH // 2
    return jnp.concatenate([jnp.dot(xb[:, :half], blk, preferred_element_type=F32),
                            jnp.dot(xb[:, half:], blk, preferred_element_type=F32)], axis=1)


def _group_norm_out(y, bonus, gate, lnx_w, lnx_b, blk):
    inv_d = 1.0 / HEAD_DIM
    m = _seg_sum(y, blk) * inv_d
    d = y - m
    var = _seg_sum(d * d, blk) * inv_d
    yn = d * lax.rsqrt(var + LNX_EPS) * lnx_w + lnx_b
    return (yn + bonus) * gate


RWKV_TILE = 4 * CHUNK
PAIR = 2 * HEAD_DIM
N_PAIRS = N_RWKV_HEADS // 2


def _rwkv_prompt_kernel(p_ref, mu_ref, w0_ref, a0_ref, kk_ref, ka_ref, rk_ref, lnw_ref, lnb_ref,
                        wupw_ref, wupa_ref, wupg_ref, y_ref, sout_ref, s_scr, prev_scr):
    C, TT, D = CHUNK, RWKV_TILE, HEAD_DIM
    NC = TT // C
    t = pl.program_id(1)

    @pl.when(t == 0)
    def _():
        s_scr[...] = jnp.zeros_like(s_scr)
        prev_scr[...] = jnp.zeros_like(prev_scr)

    p = p_ref[0].astype(F32)
    row = lax.broadcasted_iota(jnp.int32, p.shape, 0)
    prev = jnp.where(row == 0, prev_scr[...], pltpu.roll(p, 1, axis=0))
    prev_scr[...] = p[TT - 1:TT, :]
    xs = p + (prev - p) * mu_ref[...]
    r, k2, v, kk, a_sig, logw, gate = _rwkv_features(
        xs, w0_ref[...], a0_ref[...], kk_ref[...], ka_ref[...], wupw_ref[...], wupa_ref[...], wupg_ref[...])

    blk = _head_blocks(RWKV_WIDTH // 2)
    kkn = kk / jnp.maximum(jnp.sqrt(_seg_sum(kk * kk, blk)), 1e-12)
    bb = kkn * a_sig

    ri = lax.broadcasted_iota(jnp.int32, (TT, TT), 0)
    ci = lax.broadcasted_iota(jnp.int32, (TT, TT), 1)
    tri = jnp.where(jnp.logical_and(ri >= ci, ri // C == ci // C), 1.0, 0.0).astype(BF16)
    lw2 = logw * LOG2E
    l1 = lw2.astype(BF16)
    rem = lw2 - l1.astype(F32)
    l2 = rem.astype(BF16)
    l3 = (rem - l2.astype(F32)).astype(BF16)
    cum = (jnp.dot(tri, l1, preferred_element_type=F32) + jnp.dot(tri, l2, preferred_element_type=F32)
           + jnp.dot(tri, l3, preferred_element_type=F32))
    c_last = jnp.concatenate([jnp.broadcast_to(cum[(c + 1) * C - 1:(c + 1) * C, :], (C, RWKV_WIDTH))
                              for c in range(NC)], axis=0)
    e_pos = jnp.exp2(cum)
    e_neg = jnp.exp2(-cum)
    e_prev = jnp.exp2(cum - lw2)
    e_last = jnp.exp2(c_last - cum)

    lo_full = (lax.broadcasted_iota(jnp.int32, (TT, RWKV_WIDTH), 1) % PAIR) < D
    at_f = -kkn * e_prev
    rt_f = r * e_pos
    at_lo = jnp.where(lo_full, at_f, 0.0).astype(BF16)
    at_hi = jnp.where(lo_full, 0.0, at_f).astype(BF16)
    rt_lo = jnp.where(lo_full, rt_f, 0.0).astype(BF16)
    rt_hi = jnp.where(lo_full, 0.0, rt_f).astype(BF16)
    bt_b = (bb * e_neg).astype(BF16)
    kt_b = (k2 * e_neg).astype(BF16)
    bh_b = (bb * e_last).astype(BF16)
    kh_b = (k2 * e_last).astype(BF16)
    v_b = v.astype(BF16)

    r2 = lax.broadcasted_iota(jnp.int32, (C, PAIR), 0)
    c2 = lax.broadcasted_iota(jnp.int32, (C, PAIR), 1)
    lo = c2 < D
    c2m = jnp.where(lo, c2, c2 - C)
    mask_a = jnp.logical_and(lo, c2 < r2)
    mask_ak = jnp.logical_and(jnp.logical_not(lo), c2m < r2)
    mask_r = c2m <= r2
    eye_hi = jnp.where(jnp.logical_and(jnp.logical_not(lo), c2m == r2), 1.0, 0.0)
    zeros_cp = jnp.zeros((C, PAIR), F32)
    qi = lax.broadcasted_iota(jnp.int32, (PAIR, PAIR), 0) // D
    qj = lax.broadcasted_iota(jnp.int32, (PAIR, PAIR), 1) // D
    diag = qi == qj

    PR = [(c, q) for c in range(NC) for q in range(N_PAIRS)]
    n = len(PR)
    E = range(2)
    win = lambda x, c, q: x[c * C:(c + 1) * C, q * PAIR:(q + 1) * PAIR]
    sc = [_dot_nt(jnp.concatenate([win(at_lo, c, q), win(at_hi, c, q), win(rt_lo, c, q), win(rt_hi, c, q)], axis=0),
                  jnp.concatenate([win(bt_b, c, q), win(kt_b, c, q)], axis=0)) for c, q in PR]
    vr = [pltpu.roll(win(v, c, q), D, axis=1) for c, q in PR]
    vvr = [jnp.concatenate([vr[i], vr[i]], axis=0).astype(BF16) for i in range(n)]
    m_ak = [[jnp.where(mask_ak, sc[i][e * C:(e + 1) * C], 0.0) for e in E] for i in range(n)]
    m_r = [[jnp.where(mask_r, sc[i][(2 + e) * C:(3 + e) * C], 0.0) for e in E] for i in range(n)]

    zf = [[_dot(m_ak[i][e], vvr[i]) for e in E] for i in range(n)]
    W = [[jnp.where(mask_a, sc[i][e * C:(e + 1) * C], eye_hi) for e in E] for i in range(n)]
    for k in range(int(math.log2(C))):
        Wb = [[W[i][e].astype(BF16) for e in E] for i in range(n)]
        AW = [[jnp.dot(Wb[i][e][:, :C], Wb[i][e], preferred_element_type=F32) for e in E] for i in range(n)]
        W = [[jnp.where(lo, 0.0, W[i][e]) + AW[i][e] for e in E] for i in range(n)]
    X = [[_dot(W[i][0], jnp.concatenate([zeros_cp, jnp.where(lo, win(at_f, c, q), zf[i][0])], axis=0)),
          _dot(W[i][1], jnp.concatenate([zeros_cp, jnp.where(lo, zf[i][1], win(at_f, c, q))], axis=0))]
         for i, (c, q) in enumerate(PR)]

    S = [s_scr[q] for q in range(N_PAIRS)]
    ys = []
    for c in range(NC):
        w_last = jnp.exp2(cum[(c + 1) * C - 1:(c + 1) * C, :])
        idx = [c * N_PAIRS + q for q in range(N_PAIRS)]
        mkg = [[_dot_tn(X[i][e], win(bh_b, c, q)) for e in E] for q, i in enumerate(idx)]
        vk = [_dot_tn(win(v_b, c, q), win(kh_b, c, q)) for q, i in enumerate(idx)]
        ry = [[_dot(m_r[i][0], jnp.concatenate([X[i][0], jnp.where(lo, 0.0, vr[i])], axis=0)),
               _dot(m_r[i][1], jnp.concatenate([X[i][1], jnp.where(lo, vr[i], 0.0)], axis=0))] for i in idx]
        rp = [win(rt_f, c, q) + jnp.where(lo, ry[q][0], ry[q][1]) for q in range(N_PAIRS)]
        y0 = [pltpu.roll(jnp.where(lo, ry[q][1], ry[q][0]), D, axis=1) for q in range(N_PAIRS)]
        mk = [jnp.where(diag, jnp.concatenate([mkg[q][0][:D], mkg[q][1][D:]], axis=0), 0.0) for q in range(N_PAIRS)]
        g = [jnp.where(diag, vk[q] + jnp.concatenate([mkg[q][0][D:], mkg[q][1][:D]], axis=0), 0.0)
             for q in range(N_PAIRS)]
        y = [_dot_nt(rp[q], S[q]) for q in range(N_PAIRS)]
        dS = [_dot(S[q], mk[q]) for q in range(N_PAIRS)]
        S = [S[q] * w_last[:, q * PAIR:(q + 1) * PAIR] + dS[q] + g[q] for q in range(N_PAIRS)]
        ys.append(jnp.concatenate([y[q] + y0[q] for q in range(N_PAIRS)], axis=1))
    for q in range(N_PAIRS):
        s_scr[q] = S[q]

    bonus = _seg_sum(r * k2 * rk_ref[...], blk) * v
    y_ref[0] = _group_norm_out(jnp.concatenate(ys, axis=0), bonus, gate, lnw_ref[...], lnb_ref[...], blk)

    @pl.when(t == pl.num_programs(1) - 1)
    def _():
        for q in range(N_PAIRS):
            sout_ref[0, 2 * q] = S[q][:D, :D]
            sout_ref[0, 2 * q + 1] = S[q][D:, D:]


def _rwkv_prompt(proj, mu, w0, a0, k_k, k_a, r_k, lnx_w, lnx_b, wupw, wupa, wupg):
    B, T, _ = proj.shape
    vec = lambda n: _full((1, n))
    return pl.pallas_call(
        _rwkv_prompt_kernel,
        grid=(B, T // RWKV_TILE),
        in_specs=[pl.BlockSpec((1, RWKV_TILE, RWKV_PROJ), lambda b, t: (b, t, 0)),
                  vec(RWKV_PROJ), vec(RWKV_WIDTH), vec(RWKV_WIDTH), vec(RWKV_WIDTH), vec(RWKV_WIDTH),
                  vec(RWKV_WIDTH), vec(RWKV_WIDTH), vec(RWKV_WIDTH),
                  _full((DECAY_LORA, RWKV_WIDTH)), _full((AAA_LORA, RWKV_WIDTH)), _full((GATE_LORA, RWKV_WIDTH))],
        out_specs=[pl.BlockSpec((1, RWKV_TILE, RWKV_WIDTH), lambda b, t: (b, t, 0)),
                   pl.BlockSpec((1, N_RWKV_HEADS, HEAD_DIM, HEAD_DIM), lambda b, t: (b, 0, 0, 0))],
        out_shape=[jax.ShapeDtypeStruct((B, T, RWKV_WIDTH), F32),
                   jax.ShapeDtypeStruct((B, N_RWKV_HEADS, HEAD_DIM, HEAD_DIM), F32)],
        scratch_shapes=[pltpu.VMEM((N_PAIRS, PAIR, PAIR), F32),
                        pltpu.VMEM((1, RWKV_PROJ), F32)],
        compiler_params=_params("arbitrary", "arbitrary"),
        name="rwkv_prompt",
    )(proj, mu, w0, a0, k_k, k_a, r_k, lnx_w, lnx_b, wupw, wupa, wupg)


def _rwkv_step_feat_kernel(p_ref, sh_ref, mu_ref, w0_ref, a0_ref, kk_ref, ka_ref, rk_ref,
                           wupw_ref, wupa_ref, wupg_ref, vecs_ref, bonus_ref, gate_ref):
    p = p_ref[...]
    xs = p + (sh_ref[...] - p) * mu_ref[...]
    r, k2, v, kk, a_sig, logw, gate = _rwkv_features(
        xs, w0_ref[...], a0_ref[...], kk_ref[...], ka_ref[...], wupw_ref[...], wupa_ref[...], wupg_ref[...])
    blk = _head_blocks(RWKV_WIDTH // 2)
    kkn = kk / jnp.maximum(jnp.sqrt(_seg_sum(kk * kk, blk)), 1e-12)
    for i, x in enumerate((-kkn, kkn * a_sig, jnp.exp(logw), k2, r, v)):
        vecs_ref[i] = x.T
    bonus_ref[...] = _seg_sum(r * k2 * rk_ref[...], blk) * v
    gate_ref[...] = gate


def _rwkv_step_state_kernel(s_ref, vecs_ref, sout_ref, y_ref):
    S = s_ref[0]
    a, b, w, k, r, v = (vecs_ref[i] for i in range(6))
    sa = jnp.sum(S * a[None], axis=1)
    S = S * w[None] + sa[:, None, :] * b[None] + v[:, None, :] * k[None]
    sout_ref[0] = S
    y_ref[...] = jnp.sum(S * r[None], axis=1)


def _rwkv_step_out_kernel(yt_ref, bonus_ref, gate_ref, lnw_ref, lnb_ref, o_ref):
    o_ref[...] = _group_norm_out(yt_ref[...].T, bonus_ref[...], gate_ref[...], lnw_ref[...], lnb_ref[...],
                                 _head_blocks(RWKV_WIDTH // 2))


def _rwkv_step(proj, shift, state_t, mu, w0, a0, k_k, k_a, r_k, lnx_w, lnx_b, wupw, wupa, wupg):
    n = proj.shape[0]
    vec = lambda m: _full((1, m))
    rows = _full((n, RWKV_PROJ))
    wide = _full((n, RWKV_WIDTH))
    vecs, bonus, gate = pl.pallas_call(
        _rwkv_step_feat_kernel,
        grid=(1,),
        in_specs=[rows, rows, vec(RWKV_PROJ), vec(RWKV_WIDTH), vec(RWKV_WIDTH), vec(RWKV_WIDTH), vec(RWKV_WIDTH),
                  vec(RWKV_WIDTH),
                  _full((DECAY_LORA, RWKV_WIDTH)), _full((AAA_LORA, RWKV_WIDTH)), _full((GATE_LORA, RWKV_WIDTH))],
        out_specs=[_full((6, RWKV_WIDTH, n)), wide, wide],
        out_shape=[jax.ShapeDtypeStruct((6, RWKV_WIDTH, n), F32),
                   jax.ShapeDtypeStruct((n, RWKV_WIDTH), F32), jax.ShapeDtypeStruct((n, RWKV_WIDTH), F32)],
        compiler_params=_params("arbitrary"),
        name="rwkv_step_feat",
    )(proj, shift, mu, w0, a0, k_k, k_a, r_k, wupw, wupa, wupg)
    st_spec = pl.BlockSpec((1, HEAD_DIM, HEAD_DIM, n), lambda h: (h, 0, 0, 0))
    state_new, yt = pl.pallas_call(
        _rwkv_step_state_kernel,
        grid=(N_RWKV_HEADS,),
        in_specs=[st_spec, pl.BlockSpec((6, HEAD_DIM, n), lambda h: (0, h, 0))],
        out_specs=[st_spec, pl.BlockSpec((HEAD_DIM, n), lambda h: (h, 0))],
        out_shape=[jax.ShapeDtypeStruct(state_t.shape, F32), jax.ShapeDtypeStruct((RWKV_WIDTH, n), F32)],
        compiler_params=_params("arbitrary"),
        name="rwkv_step_state",
    )(state_t, vecs)
    y = pl.pallas_call(
        _rwkv_step_out_kernel,
        grid=(1,),
        in_specs=[_full((RWKV_WIDTH, n)), wide, wide, vec(RWKV_WIDTH), vec(RWKV_WIDTH)],
        out_specs=wide,
        out_shape=jax.ShapeDtypeStruct((n, RWKV_WIDTH), F32),
        compiler_params=_params("arbitrary"),
        name="rwkv_step_out",
    )(yt, bonus, gate, lnx_w, lnx_b)
    return y, state_new


def _t5_bucket_np(dist):
    max_exact = N_BUCKETS // 2
    d = np.maximum(dist, 1).astype(np.float32)
    large = max_exact + (np.log(d / np.float32(max_exact)) / np.float32(math.log(MAX_DISTANCE / max_exact))
                         * np.float32(N_BUCKETS - max_exact)).astype(np.int32)
    large = np.minimum(large, N_BUCKETS - 1)
    return np.where(dist < max_exact, dist, large).astype(np.int32)


def _prompt_bucket_table():
    qi = np.arange(BLOCK)[:, None]
    kj = np.arange(2 * BLOCK)[None, :]
    dist = BLOCK + qi - kj
    valid = (dist >= 0) & (dist <= WINDOW)
    return np.where(valid, _t5_bucket_np(np.maximum(dist, 0)), -1).astype(np.int32)


def _decode_bucket_table():
    dist = WINDOW - np.arange(WINDOW)
    return np.broadcast_to(_t5_bucket_np(dist)[None, :], (8, WINDOW)).astype(np.int32).copy()


def _bias_from_buckets(bkt, relb_ref, h, init):
    acc = jnp.full(bkt.shape, init, F32)
    for b in range(N_BUCKETS):
        acc = jnp.where(bkt == b, relb_ref[b, h], acc)
    return acc


ATT_TILE = 4 * BLOCK
ATT_GROUP = 8


def _attn_prompt_kernel(qs_ref, qm_ref, kc_ref, kp_ref, vc_ref, vp_ref, mk_ref, mv_ref, bkt_ref, relb_ref, sink_ref,
                        sqg_ref, skg_ref, mqg_ref, ys_ref, ym_ref, kn_ref, bias_scr, mk_scr, mv_scr):
    i = pl.program_id(1)
    grp = N_SWA_HEADS // N_SWA_KV_HEADS
    ones = jnp.ones((2 * BLOCK, HEAD_DIM), F32)

    @pl.when(i == 0)
    def _():
        bkt = bkt_ref[...]
        for j in range(N_SWA_KV_HEADS):
            for g in range(grp):
                bias_scr[j, g * BLOCK:(g + 1) * BLOCK, :] = _bias_from_buckets(bkt, relb_ref, j * grp + g, NEG) * LOG2E
        mk = mk_ref[0]
        mv = mv_ref[0]
        for h in range(N_MEM_HEADS):
            sl = slice(h * HEAD_DIM, (h + 1) * HEAD_DIM)
            mk_scr[h] = mk[:, sl].astype(BF16)
            mv_scr[h] = jnp.concatenate([mv[:, sl], ones], axis=1).astype(BF16)

    qs = qs_ref[0].astype(F32)
    qm = qm_ref[0].astype(F32)
    kc = kc_ref[0].astype(F32)
    kp = kp_ref[0].astype(F32)
    vc = vc_ref[0].astype(F32)
    vp = vp_ref[0].astype(F32)
    sqg, skg, mqg = sqg_ref[...], skg_ref[...], mqg_ref[...]
    rowi = lax.broadcasted_iota(jnp.int32, (2 * BLOCK, 1), 0)
    col = lax.broadcasted_iota(jnp.int32, (2 * BLOCK, 2 * BLOCK), 1)
    pad_mask = jnp.logical_and(i == 0, col < BLOCK)

    hsl = [slice(h * HEAD_DIM, (h + 1) * HEAD_DIM) for h in range(N_SWA_HEADS)]
    qs_n = _rms_heads(qs, sqg, N_SWA_HEADS) * (ATTN_SCALE * LOG2E)
    qm_n = _rms_heads(qm, mqg, N_MEM_HEADS) * (ATTN_SCALE * LOG2E)
    kc_n = _rms_heads(kc, skg, N_SWA_KV_HEADS)
    kp_n = _rms_heads(kp, skg, N_SWA_KV_HEADS)
    kn_ref[0] = kc_n[ATT_TILE - BLOCK:]
    chains = [(a, j) for a in range(ATT_TILE // BLOCK) for j in range(N_SWA_KV_HEADS)]
    lhs, keys, vals, sinkcol = [], [], [], []
    for a, j in chains:
        rs = slice(a * BLOCK, (a + 1) * BLOCK)
        lhs.append(jnp.concatenate([qs_n[rs, hsl[j * grp + g]] for g in range(grp)], axis=0))
        if a == 0:
            keys.append(jnp.concatenate([kp_n[:, hsl[j]], kc_n[:BLOCK, hsl[j]]], axis=0))
            vv = jnp.concatenate([vp[:, hsl[j]], vc[:BLOCK, hsl[j]]], axis=0)
        else:
            ks = slice((a - 1) * BLOCK, (a + 1) * BLOCK)
            keys.append(kc_n[ks, hsl[j]])
            vv = vc[ks, hsl[j]]
        vals.append(jnp.concatenate([vv, ones], axis=1))
        sinkcol.append(jnp.where(rowi < BLOCK, sink_ref[0, j * grp], sink_ref[0, j * grp + 1]) * LOG2E)
    qmn = [qm_n[:, hsl[h]] for h in range(N_MEM_HEADS)]

    bias = [bias_scr[j] for j in range(N_SWA_KV_HEADS)]
    mem_k = [mk_scr[h] for h in range(N_MEM_HEADS)]
    mem_v = [mv_scr[h] for h in range(N_MEM_HEADS)]
    n_w = len(chains)
    n_tasks = n_w + N_MEM_HEADS

    def scores(t):
        return _dot_nt(lhs[t], keys[t]) if t < n_w else _dot_nt(qmn[t - n_w], mem_k[t - n_w])

    def finish(ts, s):
        e, extra = {}, {}
        for t in ts:
            if t < n_w:
                a, j = chains[t]
                st = s[t] + bias[j]
                if a == 0:
                    st = jnp.where(pad_mask, NEG, st)
                m = jnp.maximum(jnp.max(st, -1, keepdims=True), sinkcol[t])
                e[t], extra[t] = jnp.exp2(st - m), jnp.exp2(sinkcol[t] - m)
            else:
                e[t], extra[t] = jnp.exp2(s[t] - jnp.max(s[t], -1, keepdims=True)), 0.0
        o_full = {t: _dot(e[t], vals[t] if t < n_w else mem_v[t - n_w]) for t in ts}
        return {t: o_full[t][:, :HEAD_DIM] / (o_full[t][:, HEAD_DIM:HEAD_DIM + 1] + extra[t]) for t in ts}

    groups = [list(range(g, min(g + ATT_GROUP, n_tasks))) for g in range(0, n_tasks, ATT_GROUP)]
    s, outs = {}, {}
    for gi, ts in enumerate(groups):
        if gi == 0:
            s.update({t: scores(t) for t in ts})
        if gi + 1 < len(groups):
            s.update({t: scores(t) for t in groups[gi + 1]})
        outs.update(finish(ts, s))
    for t, (a, j) in enumerate(chains):
        for g in range(grp):
            ys_ref[0, a * BLOCK:(a + 1) * BLOCK, hsl[j * grp + g]] = outs[t][g * BLOCK:(g + 1) * BLOCK]
    for h in range(N_MEM_HEADS):
        ym_ref[0, :, hsl[h]] = outs[n_w + h]


def _attn_prompt(proj, mk, mv, rel_bias, sinks, sqg, skg, mqg):
    B, T, _ = proj.shape
    bkt = jnp.asarray(_prompt_bucket_table())
    smem = pl.BlockSpec(memory_space=pltpu.SMEM)
    kblk, vblk = COL_SK // SWA_KV_WIDTH, COL_SV // SWA_KV_WIDTH
    prev = lambda b, i: (b, jnp.maximum((ATT_TILE // BLOCK) * i - 1, 0))
    memb = pl.BlockSpec((1, N_MEM, MEM_WIDTH), lambda b, i: (b, 0, 0))
    return pl.pallas_call(
        _attn_prompt_kernel,
        grid=(B, T // ATT_TILE),
        in_specs=[pl.BlockSpec((1, ATT_TILE, SWA_WIDTH), lambda b, i: (b, i, COL_SQ // SWA_WIDTH)),
                  pl.BlockSpec((1, ATT_TILE, MEM_WIDTH), lambda b, i: (b, i, COL_MQ // MEM_WIDTH)),
                  pl.BlockSpec((1, ATT_TILE, SWA_KV_WIDTH), lambda b, i: (b, i, kblk)),
                  pl.BlockSpec((1, BLOCK, SWA_KV_WIDTH), lambda b, i: prev(b, i) + (kblk,)),
                  pl.BlockSpec((1, ATT_TILE, SWA_KV_WIDTH), lambda b, i: (b, i, vblk)),
                  pl.BlockSpec((1, BLOCK, SWA_KV_WIDTH), lambda b, i: prev(b, i) + (vblk,)),
                  memb, memb,
                  _full((BLOCK, 2 * BLOCK)), smem, smem,
                  _full((1, HEAD_DIM)), _full((1, HEAD_DIM)), _full((1, HEAD_DIM))],
        out_specs=[pl.BlockSpec((1, ATT_TILE, SWA_WIDTH), lambda b, i: (b, i, 0)),
                   pl.BlockSpec((1, ATT_TILE, MEM_WIDTH), lambda b, i: (b, i, 0)),
                   pl.BlockSpec((1, BLOCK, SWA_KV_WIDTH), lambda b, i: (b, 0, 0))],
        out_shape=[jax.ShapeDtypeStruct((B, T, SWA_WIDTH), F32),
                   jax.ShapeDtypeStruct((B, T, MEM_WIDTH), F32),
                   jax.ShapeDtypeStruct((B, BLOCK, SWA_KV_WIDTH), F32)],
        scratch_shapes=[pltpu.VMEM((N_SWA_KV_HEADS, 2 * BLOCK, 2 * BLOCK), F32),
                        pltpu.VMEM((N_MEM_HEADS, N_MEM, HEAD_DIM), BF16),
                        pltpu.VMEM((N_MEM_HEADS, N_MEM, 2 * HEAD_DIM), BF16)],
        compiler_params=_params("arbitrary", "arbitrary"),
        name="attn_prompt",
    )(proj, proj, proj, proj, proj, proj, mk, mv, bkt, rel_bias, sinks, sqg, skg, mqg)


def _memory_kv_kernel(mem_ref, g_ref, w_ref, kg_ref, mk_ref, mv_ref):
    kv = jnp.dot(_rms(mem_ref[0], g_ref[...]).astype(BF16), w_ref[...], preferred_element_type=F32)
    kg = kg_ref[...]
    for h in range(N_MEM_HEADS):
        sl = slice(h * HEAD_DIM, (h + 1) * HEAD_DIM)
        mk_ref[0, :, sl] = _rms(kv[:, sl], kg)
    mv_ref[0] = kv[:, MEM_WIDTH:]


def _memory_kv(mem, g, w_bf16, kg):
    B = mem.shape[0]
    blk = pl.BlockSpec((1, N_MEM, MEM_WIDTH), lambda b: (b, 0, 0))
    return pl.pallas_call(
        _memory_kv_kernel,
        grid=(B,),
        in_specs=[pl.BlockSpec((1, N_MEM, D_MODEL), lambda b: (b, 0, 0)),
                  _full((1, D_MODEL)), _full((D_MODEL, 2 * MEM_WIDTH)), _full((1, HEAD_DIM))],
        out_specs=[blk, blk],
        out_shape=[jax.ShapeDtypeStruct((B, N_MEM, MEM_WIDTH), F32)] * 2,
        compiler_params=_params("arbitrary"),
        name="memory_kv",
    )(mem, g, w_bf16, kg)


SEQ_TILE = 8


def _rms_heads(x, g, n_heads):
    ms = jnp.dot((x * x).astype(BF16), _head_blocks(n_heads * HEAD_DIM), preferred_element_type=F32)
    return x * lax.rsqrt(ms * (1.0 / HEAD_DIM) + NORM_EPS) * jnp.concatenate([g] * n_heads, axis=1)


def _decode_attn_kernel(p_ref, kbuf_ref, vbuf_ref, mk_ref, mv_ref, bkt_ref, relb_ref, sink_ref,
                        sqg_ref, skg_ref, mqg_ref, ys_ref, ym_ref, kout_ref, vout_ref, tab_scr):
    grp = N_SWA_HEADS // N_SWA_KV_HEADS

    @pl.when(pl.program_id(0) == 0)
    def _():
        hrow = lax.broadcasted_iota(jnp.int32, (8, WINDOW), 0)
        lane = lax.broadcasted_iota(jnp.int32, (8, WINDOW), 1)
        bias_w = jnp.zeros((8, WINDOW), F32)
        cols = jnp.zeros((8, WINDOW), F32)
        for h in range(N_SWA_HEADS):
            bias_w = jnp.where(hrow == h, _bias_from_buckets(bkt_ref[...], relb_ref, h, 0.0), bias_w)
            cols = jnp.where(jnp.logical_and(hrow == h, lane == 0), relb_ref[0, h], cols)
            cols = jnp.where(jnp.logical_and(hrow == h, lane == 1), sink_ref[0, h], cols)
        tab_scr[0] = bias_w
        tab_scr[1] = cols

    bias_w = tab_scr[0][:N_SWA_HEADS]
    bias_new = tab_scr[1][:N_SWA_HEADS, 0:1]
    sink = tab_scr[1][:N_SWA_HEADS, 1:2]
    rowi = lax.broadcasted_iota(jnp.int32, (SWA_KV_WIDTH, WINDOW), 0)
    lanei = lax.broadcasted_iota(jnp.int32, (SWA_KV_WIDTH, WINDOW), 1)
    eye = rowi == lanei
    NH = N_SWA_HEADS
    own = (lax.broadcasted_iota(jnp.int32, (NH, NH * HEAD_DIM), 1) // HEAD_DIM
           == lax.broadcasted_iota(jnp.int32, (NH, NH * HEAD_DIM), 0))

    p = p_ref[...]
    qn = _rms_heads(p[:, COL_SQ:COL_SQ + SWA_WIDTH], sqg_ref[...], N_SWA_HEADS) * ATTN_SCALE
    kn = _rms_heads(p[:, COL_SK:COL_SK + SWA_KV_WIDTH], skg_ref[...], N_SWA_KV_HEADS)
    vn = p[:, COL_SV:COL_SV + SWA_KV_WIDTH]
    qmn = _rms_heads(p[:, COL_MQ:COL_MQ + MEM_WIDTH], mqg_ref[...], N_MEM_HEADS) * ATTN_SCALE
    rep = lambda x: jnp.concatenate([x[:, j * HEAD_DIM:(j + 1) * HEAD_DIM] for j in range(N_SWA_KV_HEADS)
                                     for _ in range(grp)], axis=1)
    kn_rep, vn_rep = rep(kn), rep(vn)

    B = range(SEQ_TILE)
    dup = lambda c: jnp.concatenate([c[j * HEAD_DIM:(j + 1) * HEAD_DIM] for j in range(N_SWA_KV_HEADS)
                                     for _ in range(grp)], axis=0)
    qd = [jnp.where(own, qn[b:b + 1, :], 0.0) for b in B]
    qmd = [jnp.where(own, qmn[b:b + 1, :], 0.0) for b in B]
    kdup = [dup(kbuf_ref[b]) for b in B]
    vdup = [dup(vbuf_ref[b]) for b in B]
    s = [_dot(qd[b], kdup[b]) + bias_w for b in B]
    sm = [_dot(qmd[b], mk_ref[b]) for b in B]
    s_new = [jnp.sum(qd[b] * kn_rep[b:b + 1, :], -1, keepdims=True) + bias_new for b in B]
    m = [jnp.maximum(jnp.maximum(jnp.max(s[b], -1, keepdims=True), s_new[b]), sink) for b in B]
    e = [jnp.exp(s[b] - m[b]) for b in B]
    e_new = [jnp.exp(s_new[b] - m[b]) for b in B]
    den = [jnp.sum(e[b], -1, keepdims=True) + e_new[b] + jnp.exp(sink - m[b]) for b in B]
    em = [jnp.exp(sm[b] - jnp.max(sm[b], -1, keepdims=True)) for b in B]
    ov = [_dot_nt(e[b], vdup[b]) for b in B]
    omf = [_dot_nt(em[b], mv_ref[b]) for b in B]
    ys_rows = [jnp.sum(jnp.where(own, (ov[b] + e_new[b] * vn_rep[b:b + 1, :]) / den[b], 0.0), 0, keepdims=True)
               for b in B]
    ym_rows = [jnp.sum(jnp.where(own, omf[b] / jnp.sum(em[b], -1, keepdims=True), 0.0), 0, keepdims=True)
               for b in B]
    ys_ref[...] = jnp.concatenate(ys_rows, axis=0)
    ym_ref[...] = jnp.concatenate(ym_rows, axis=0)
    for b in B:
        kn_col = jnp.sum(jnp.where(eye, kn[b:b + 1, :], 0.0), -1, keepdims=True)
        vn_col = jnp.sum(jnp.where(eye, vn[b:b + 1, :], 0.0), -1, keepdims=True)
        kout_ref[b] = jnp.where(lanei == WINDOW - 1, kn_col, pltpu.roll(kbuf_ref[b], WINDOW - 1, axis=1))
        vout_ref[b] = jnp.where(lanei == WINDOW - 1, vn_col, pltpu.roll(vbuf_ref[b], WINDOW - 1, axis=1))


def _decode_attn(proj, kbuf, vbuf, mk, mv, rel_bias, sinks, sqg, skg, mqg):
    n = proj.shape[0]
    bkt = jnp.asarray(_decode_bucket_table())
    smem = pl.BlockSpec(memory_space=pltpu.SMEM)
    win = pl.BlockSpec((SEQ_TILE, WINDOW, SWA_KV_WIDTH), lambda i: (i, 0, 0))
    memb = pl.BlockSpec((SEQ_TILE, N_MEM, MEM_WIDTH), lambda i: (i, 0, 0))
    return pl.pallas_call(
        _decode_attn_kernel,
        grid=(n // SEQ_TILE,),
        in_specs=[pl.BlockSpec((SEQ_TILE, IN_PROJ), lambda i: (i, 0)), win, win, memb, memb,
                  _full((8, WINDOW)), smem, smem, _full((1, HEAD_DIM)), _full((1, HEAD_DIM)), _full((1, HEAD_DIM))],
        out_specs=[pl.BlockSpec((SEQ_TILE, SWA_WIDTH), lambda i: (i, 0)),
                   pl.BlockSpec((SEQ_TILE, MEM_WIDTH), lambda i: (i, 0)), win, win],
        out_shape=[jax.ShapeDtypeStruct((n, SWA_WIDTH), F32), jax.ShapeDtypeStruct((n, MEM_WIDTH), F32),
                   jax.ShapeDtypeStruct(kbuf.shape, F32), jax.ShapeDtypeStruct(vbuf.shape, F32)],
        scratch_shapes=[pltpu.VMEM((2, 8, WINDOW), F32)],
        compiler_params=_params("arbitrary"),
        name="decode_attn",
    )(proj, kbuf, vbuf, mk, mv, bkt, rel_bias, sinks, sqg, skg, mqg)


FF_CHUNK = 1024
OUT_FFN_TILE = 512


def _out_ffn_rows(x, yr, ys, ym, wo_ref, g2_ref, w1_ref, w2_ref):
    x1 = (x
          + jnp.dot(yr.astype(BF16), wo_ref[0:RWKV_WIDTH, :], preferred_element_type=F32)
          + jnp.dot(ys.astype(BF16), wo_ref[RWKV_WIDTH:RWKV_WIDTH + SWA_WIDTH, :], preferred_element_type=F32)
          + jnp.dot(ym.astype(BF16), wo_ref[RWKV_WIDTH + SWA_WIDTH:, :], preferred_element_type=F32))
    h2 = _rms(x1, g2_ref[...]).astype(BF16)
    ff = None
    for c in range(D_FF // FF_CHUNK):
        u = jnp.dot(h2, w1_ref[:, c * FF_CHUNK:(c + 1) * FF_CHUNK], preferred_element_type=F32)
        u = jnp.square(jnp.maximum(u, 0.0)).astype(BF16)
        d = jnp.dot(u, w2_ref[c * FF_CHUNK:(c + 1) * FF_CHUNK, :], preferred_element_type=F32)
        ff = d if ff is None else ff + d
    return x1 + ff


def _out_ffn_kernel(x_ref, yr_ref, ys_ref, ym_ref, xs_ref, yrs_ref, yss_ref, yms_ref,
                    wo_ref, g2_ref, w1_ref, w2_ref, o_ref, os_ref):
    o_ref[...] = _out_ffn_rows(x_ref[...], yr_ref[...], ys_ref[...], ym_ref[...], wo_ref, g2_ref, w1_ref, w2_ref)

    @pl.when(pl.program_id(0) == pl.num_programs(0) - 1)
    def _():
        os_ref[...] = _out_ffn_rows(xs_ref[...], yrs_ref[...], yss_ref[...], yms_ref[...],
                                    wo_ref, g2_ref, w1_ref, w2_ref)


def _out_ffn(prompt, sample, wo, g2, w1, w2, tm):
    n, ns = prompt[0].shape[0], sample[0].shape[0]
    widths = (D_MODEL, RWKV_WIDTH, SWA_WIDTH, MEM_WIDTH)
    const = lambda shape: pl.BlockSpec(shape, lambda i: (0, 0), pipeline_mode=pl.Buffered(1))
    return pl.pallas_call(
        _out_ffn_kernel,
        grid=(n // tm,),
        in_specs=([pl.BlockSpec((tm, w), lambda i: (i, 0)) for w in widths] + [_full((ns, w)) for w in widths]
                  + [const((D_MODEL, D_MODEL)), _full((1, D_MODEL)), const((D_MODEL, D_FF)), const((D_FF, D_MODEL))]),
        out_specs=[pl.BlockSpec((tm, D_MODEL), lambda i: (i, 0)), _full((ns, D_MODEL))],
        out_shape=[jax.ShapeDtypeStruct((n, D_MODEL), F32), jax.ShapeDtypeStruct((ns, D_MODEL), F32)],
        compiler_params=_params("arbitrary"),
        name="out_ffn",
    )(*prompt, *sample, wo, g2, w1, w2)


def kernel(x_prompt, x_sample, state_rwkv, state_shift, cache_swa_k, cache_swa_v, cache_mem_k, cache_mem_v,
           mem_prompt, rel_bias, norm1_g, w_in, mu_shift, w0, w_up_w, a0, w_up_a, w_up_g, k_k, k_a, r_k,
           lnx_w, lnx_b, q_norm_swa, k_norm_swa, sinks, mem_norm_g, w_mem_kv, q_norm_mem, k_norm_mem,
           w_out, norm2_g, w_ff1, w_ff2):
    B, T, _ = x_prompt.shape
    Bd = x_sample.shape[0]
    l = 0
    w_in_b = w_in[l].astype(BF16)
    w_out_b = w_out[l].astype(BF16)
    w1_b = w_ff1[l].astype(BF16)
    w2_b = w_ff2[l].astype(BF16)
    rwkv_params = (mu_shift[l][None], w0[l][None], a0[l][None], k_k[l][None], k_a[l][None],
                   r_k[l].reshape(1, RWKV_WIDTH), lnx_w[l][None], lnx_b[l][None],
                   w_up_w[l].astype(BF16), w_up_a[l].astype(BF16), w_up_g[l].astype(BF16))
    sqg, skg, mqg, mkg = q_norm_swa[l][None], k_norm_swa[l][None], q_norm_mem[l][None], k_norm_mem[l][None]
    g1, g2 = norm1_g[l][None], norm2_g[l][None]

    xp = x_prompt.reshape(B * T, D_MODEL)
    xs = x_sample.reshape(Bd, D_MODEL)
    proj_p, proj_s = _in_proj(xp, xs, g1, w_in_b, IN_PROJ_TILE)
    proj_p = proj_p.reshape(B, T, IN_PROJ)

    mk, mv = _memory_kv(mem_prompt, mem_norm_g[l][None], w_mem_kv[l].astype(BF16), mkg)
    yr_p, s_p = _rwkv_prompt(proj_p, *rwkv_params)
    ys_p, ym_p, kn_p = _attn_prompt(proj_p, mk, mv, rel_bias, sinks[l][None], sqg, skg, mqg)
    shift_p = proj_p[:, T - 1, :RWKV_PROJ].astype(F32)
    vb_p = proj_p[:, T - WINDOW:, COL_SV:COL_SV + SWA_KV_WIDTH].astype(F32)

    yr_s, st_s = _rwkv_step(proj_s[:, :RWKV_PROJ], state_shift[l], jnp.transpose(state_rwkv[l], (1, 2, 3, 0)),
                            *rwkv_params)
    s_s = jnp.transpose(st_s, (3, 0, 1, 2))
    fmajor = lambda c: jnp.transpose(c, (0, 2, 3, 1)).reshape(Bd, c.shape[2] * HEAD_DIM, c.shape[1])
    ys_s, ym_s, kb_s, vb_s = _decode_attn(
        proj_s, fmajor(cache_swa_k[l]), fmajor(cache_swa_v[l]), fmajor(cache_mem_k[l]), fmajor(cache_mem_v[l]),
        rel_bias, sinks[l][None], sqg, skg, mqg)
    pmajor = lambda c: jnp.transpose(c.reshape(Bd, N_SWA_KV_HEADS, HEAD_DIM, WINDOW), (0, 3, 1, 2))[None]

    y_p, y_s = _out_ffn((xp, yr_p.reshape(B * T, RWKV_WIDTH), ys_p.reshape(B * T, SWA_WIDTH),
                         ym_p.reshape(B * T, MEM_WIDTH)), (xs, yr_s, ys_s, ym_s),
                        w_out_b, g2, w1_b, w2_b, OUT_FFN_TILE)
    y_p = y_p.reshape(B, T, D_MODEL)
    y_s = y_s.reshape(Bd, 1, D_MODEL)

    return (y_p, y_s,
            s_p[None], shift_p[None],
            kn_p.reshape(1, B, WINDOW, N_SWA_KV_HEADS, HEAD_DIM),
            vb_p.reshape(1, B, WINDOW, N_SWA_KV_HEADS, HEAD_DIM),
            mk.reshape(1, B, N_MEM, N_MEM_HEADS, HEAD_DIM), mv.reshape(1, B, N_MEM, N_MEM_HEADS, HEAD_DIM),
            s_s[None], proj_s[:, :RWKV_PROJ][None],
            pmajor(kb_s), pmajor(vb_s))
```

```python
import math

import numpy as np
import jax
import jax.numpy as jnp
from jax import lax
from jax.experimental import pallas as pl
from jax.experimental.pallas import tpu as pltpu

F32 = jnp.float32
BF16 = jnp.bfloat16

D_MODEL = 1024
HEAD_DIM = 64
RWKV_WIDTH = 512
N_RWKV_HEADS = 8
SWA_WIDTH = 256
N_SWA_HEADS = 4
N_SWA_KV_HEADS = 2
SWA_KV_WIDTH = 128
MEM_WIDTH = 256
N_MEM_HEADS = 4
N_MEM = 256
WINDOW = 128
BLOCK = 128
N_BUCKETS = 32
MAX_DISTANCE = 128
DECAY_LORA = 64
AAA_LORA = 64
GATE_LORA = 128
RWKV_PROJ = 3 * RWKV_WIDTH + DECAY_LORA + AAA_LORA + GATE_LORA
SWA_PROJ = SWA_WIDTH + 2 * SWA_KV_WIDTH
IN_PROJ = RWKV_PROJ + SWA_PROJ + MEM_WIDTH
D_FF = 4 * D_MODEL
NORM_EPS = 1e-6
LNX_EPS = 64e-5
ATTN_SCALE = HEAD_DIM ** -0.5
EXP_M05 = math.exp(-0.5)
LOG2E = math.log2(math.e)
NEG = -1e30

COL_R, COL_K, COL_V = 0, RWKV_WIDTH, 2 * RWKV_WIDTH
COL_WD = 3 * RWKV_WIDTH
COL_AD = COL_WD + DECAY_LORA
COL_GD = COL_AD + AAA_LORA
COL_SQ = RWKV_PROJ
COL_SK = COL_SQ + SWA_WIDTH
COL_SV = COL_SK + SWA_KV_WIDTH
COL_MQ = RWKV_PROJ + SWA_PROJ

CHUNK = 64
VMEM_LIMIT = 56 * 1024 * 1024


def _dot(a, b):
    return jnp.dot(a.astype(BF16), b.astype(BF16), preferred_element_type=F32)


def _dot_nt(a, b):
    return lax.dot_general(a.astype(BF16), b.astype(BF16), (((1,), (1,)), ((), ())),
                           preferred_element_type=F32)


def _dot_tn(a, b):
    return lax.dot_general(a.astype(BF16), b.astype(BF16), (((0,), (0,)), ((), ())),
                           preferred_element_type=F32)


def _rms(x, g):
    return x * lax.rsqrt(jnp.mean(x * x, -1, keepdims=True) + NORM_EPS) * g


def _params(*sem):
    return pltpu.CompilerParams(dimension_semantics=sem, vmem_limit_bytes=VMEM_LIMIT)


def _full(shape):
    n = len(shape)
    return pl.BlockSpec(shape, lambda *_: (0,) * n)


def _head_blocks(width):
    bi = lax.broadcasted_iota(jnp.int32, (width, width), 0) // HEAD_DIM
    bj = lax.broadcasted_iota(jnp.int32, (width, width), 1) // HEAD_DIM
    return jnp.where(bi == bj, 1.0, 0.0).astype(BF16)


IN_PROJ_TILE = 1024
IN_PROJ_SUB = 256


def _in_proj_kernel(x_ref, xs_ref, g_ref, w_ref, o_ref, os_ref):
    tm = x_ref.shape[0]
    for j in range(tm // IN_PROJ_SUB):
        rows = slice(j * IN_PROJ_SUB, (j + 1) * IN_PROJ_SUB)
        h = _rms(x_ref[rows, :], g_ref[...])
        o_ref[rows, :] = jnp.dot(h.astype(BF16), w_ref[...], preferred_element_type=F32)

    @pl.when(pl.program_id(0) == pl.num_programs(0) - 1)
    def _():
        hs = _rms(xs_ref[...], g_ref[...])
        os_ref[...] = jnp.dot(hs.astype(BF16), w_ref[...], preferred_element_type=F32)


def _in_proj(x2d, xs2d, g, w_bf16, tm):
    n, ns = x2d.shape[0], xs2d.shape[0]
    return pl.pallas_call(
        _in_proj_kernel,
        grid=(n // tm,),
        in_specs=[pl.BlockSpec((tm, D_MODEL), lambda i: (i, 0)),
                  _full((ns, D_MODEL)),
                  _full((1, D_MODEL)),
                  _full((D_MODEL, IN_PROJ))],
        out_specs=[pl.BlockSpec((tm, IN_PROJ), lambda i: (i, 0)), _full((ns, IN_PROJ))],
        out_shape=[jax.ShapeDtypeStruct((n, IN_PROJ), F32), jax.ShapeDtypeStruct((ns, IN_PROJ), F32)],
        compiler_params=_params("arbitrary"),
        name="in_proj",
    )(x2d, xs2d, g, w_bf16)


def _rwkv_features(xs, w0, a0, k_k, k_a, wupw, wupa, wupg):
    r = xs[:, COL_R:COL_R + RWKV_WIDTH]
    k = xs[:, COL_K:COL_K + RWKV_WIDTH]
    v = xs[:, COL_V:COL_V + RWKV_WIDTH]
    wd = xs[:, COL_WD:COL_WD + DECAY_LORA]
    ad = xs[:, COL_AD:COL_AD + AAA_LORA]
    gd = xs[:, COL_GD:COL_GD + GATE_LORA]
    logw = -jax.nn.sigmoid(w0 + _dot(jnp.tanh(wd), wupw)) * EXP_M05
    a_sig = jax.nn.sigmoid(a0 + _dot(ad, wupa))
    gate = _dot(jax.nn.sigmoid(gd), wupg)
    kk = k * k_k
    k2 = k * (1.0 + (a_sig - 1.0) * k_a)
    return r, k2, v, kk, a_sig, logw, gate


def _seg_sum(x, blk):
    xb = x.astype(BF16)
    half = RWKV_WIDTH // 2
    return jnp.concatenate([jnp.dot(xb[:, :half], blk, preferred_element_type=F32),
                            jnp.dot(xb[:, half:], blk, preferred_element_type=F32)], axis=1)


def _group_norm_out(y, bonus, gate, lnx_w, lnx_b, blk):
    inv_d = 1.0 / HEAD_DIM
    m = _seg_sum(y, blk) * inv_d
    d = y - m
    var = _seg_sum(d * d, blk) * inv_d
    yn = d * lax.rsqrt(var + LNX_EPS) * lnx_w + lnx_b
    return (yn + bonus) * gate


RWKV_TILE = 4 * CHUNK
PAIR = 2 * HEAD_DIM
N_PAIRS = N_RWKV_HEADS // 2


def _rwkv_prompt_kernel(p_ref, mu_ref, w0_ref, a0_ref, kk_ref, ka_ref, rk_ref, lnw_ref, lnb_ref,
                        wupw_ref, wupa_ref, wupg_ref, y_ref, sout_ref, s_scr, prev_scr):
    C, TT, D = CHUNK, RWKV_TILE, HEAD_DIM
    NC = TT // C
    t = pl.program_id(1)

    @pl.when(t == 0)
    def _():
        s_scr[...] = jnp.zeros_like(s_scr)
        prev_scr[...] = jnp.zeros_like(prev_scr)

    p = p_ref[0]
    row = lax.broadcasted_iota(jnp.int32, p.shape, 0)
    prev = jnp.where(row == 0, prev_scr[...], pltpu.roll(p, 1, axis=0))
    prev_scr[...] = p[TT - 1:TT, :]
    xs = p + (prev - p) * mu_ref[...]
    r, k2, v, kk, a_sig, logw, gate = _rwkv_features(
        xs, w0_ref[...], a0_ref[...], kk_ref[...], ka_ref[...], wupw_ref[...], wupa_ref[...], wupg_ref[...])

    blk = _head_blocks(RWKV_WIDTH // 2)
    kkn = kk / jnp.maximum(jnp.sqrt(_seg_sum(kk * kk, blk)), 1e-12)
    bb = kkn * a_sig

    ri = lax.broadcasted_iota(jnp.int32, (TT, TT), 0)
    ci = lax.broadcasted_iota(jnp.int32, (TT, TT), 1)
    tri = jnp.where(jnp.logical_and(ri >= ci, ri // C == ci // C), 1.0, 0.0).astype(BF16)
    lw2 = logw * LOG2E
    l1 = lw2.astype(BF16)
    l2 = (lw2 - l1.astype(F32)).astype(BF16)
    cum = jnp.dot(tri, l1, preferred_element_type=F32) + jnp.dot(tri, l2, preferred_element_type=F32)
    c_last = jnp.concatenate([jnp.broadcast_to(cum[(c + 1) * C - 1:(c + 1) * C, :], (C, RWKV_WIDTH))
                              for c in range(NC)], axis=0)
    e_pos = jnp.exp2(cum)
    e_neg = jnp.exp2(-cum)
    e_prev = jnp.exp2(cum - lw2)
    e_last = jnp.exp2(c_last - cum)

    lo_full = (lax.broadcasted_iota(jnp.int32, (TT, RWKV_WIDTH), 1) % PAIR) < D
    at_f = -kkn * e_prev
    rt_f = r * e_pos
    at_lo = jnp.where(lo_full, at_f, 0.0).astype(BF16)
    at_hi = jnp.where(lo_full, 0.0, at_f).astype(BF16)
    rt_lo = jnp.where(lo_full, rt_f, 0.0).astype(BF16)
    rt_hi = jnp.where(lo_full, 0.0, rt_f).astype(BF16)
    bt_b = (bb * e_neg).astype(BF16)
    kt_b = (k2 * e_neg).astype(BF16)
    bh_b = (bb * e_last).astype(BF16)
    kh_b = (k2 * e_last).astype(BF16)
    v_b = v.astype(BF16)

    r2 = lax.broadcasted_iota(jnp.int32, (C, PAIR), 0)
    c2 = lax.broadcasted_iota(jnp.int32, (C, PAIR), 1)
    lo = c2 < D
    c2m = jnp.where(lo, c2, c2 - C)
    mask_a = jnp.logical_and(lo, c2 < r2)
    mask_ak = jnp.logical_and(jnp.logical_not(lo), c2m < r2)
    mask_r = c2m <= r2
    eye_hi = jnp.where(jnp.logical_and(jnp.logical_not(lo), c2m == r2), 1.0, 0.0)
    zeros_cp = jnp.zeros((C, PAIR), F32)
    qi = lax.broadcasted_iota(jnp.int32, (PAIR, PAIR), 0) // D
    qj = lax.broadcasted_iota(jnp.int32, (PAIR, PAIR), 1) // D
    diag = qi == qj

    PR = [(c, q) for c in range(NC) for q in range(N_PAIRS)]
    n = len(PR)
    E = range(2)
    win = lambda x, c, q: x[c * C:(c + 1) * C, q * PAIR:(q + 1) * PAIR]
    sc = [_dot_nt(jnp.concatenate([win(at_lo, c, q), win(at_hi, c, q), win(rt_lo, c, q), win(rt_hi, c, q)], axis=0),
                  jnp.concatenate([win(bt_b, c, q), win(kt_b, c, q)], axis=0)) for c, q in PR]
    vr = [pltpu.roll(win(v, c, q), D, axis=1) for c, q in PR]
    vvr = [jnp.concatenate([vr[i], vr[i]], axis=0).astype(BF16) for i in range(n)]
    m_ak = [[jnp.where(mask_ak, sc[i][e * C:(e + 1) * C], 0.0) for e in E] for i in range(n)]
    m_r = [[jnp.where(mask_r, sc[i][(2 + e) * C:(3 + e) * C], 0.0) for e in E] for i in range(n)]

    zf = [[_dot(m_ak[i][e], vvr[i]) for e in E] for i in range(n)]
    W = [[jnp.where(mask_a, sc[i][e * C:(e + 1) * C], eye_hi) for e in E] for i in range(n)]
    for k in range(int(math.log2(C))):
        Wb = [[W[i][e].astype(BF16) for e in E] for i in range(n)]
        AW = [[jnp.dot(Wb[i][e][:, :C], Wb[i][e], preferred_element_type=F32) for e in E] for i in range(n)]
        W = [[jnp.where(lo, 0.0, W[i][e]) + AW[i][e] for e in E] for i in range(n)]
    X = [[_dot(W[i][0], jnp.concatenate([zeros_cp, jnp.where(lo, win(at_f, c, q), zf[i][0])], axis=0)),
          _dot(W[i][1], jnp.concatenate([zeros_cp, jnp.where(lo, zf[i][1], win(at_f, c, q))], axis=0))]
         for i, (c, q) in enumerate(PR)]

    S = [s_scr[q] for q in range(N_PAIRS)]
    ys = []
    for c in range(NC):
        w_last = jnp.exp2(cum[(c + 1) * C - 1:(c + 1) * C, :])
        idx = [c * N_PAIRS + q for q in range(N_PAIRS)]
        mkg = [[_dot_tn(X[i][e], win(bh_b, c, q)) for e in E] for q, i in enumerate(idx)]
        vk = [_dot_tn(win(v_b, c, q), win(kh_b, c, q)) for q, i in enumerate(idx)]
        ry = [[_dot(m_r[i][0], jnp.concatenate([X[i][0], jnp.where(lo, 0.0, vr[i])], axis=0)),
               _dot(m_r[i][1], jnp.concatenate([X[i][1], jnp.where(lo, vr[i], 0.0)], axis=0))] for i in idx]
        rp = [win(rt_f, c, q) + jnp.where(lo, ry[q][0], ry[q][1]) for q in range(N_PAIRS)]
        y0 = [pltpu.roll(jnp.where(lo, ry[q][1], ry[q][0]), D, axis=1) for q in range(N_PAIRS)]
        mk = [jnp.where(diag, jnp.concatenate([mkg[q][0][:D], mkg[q][1][D:]], axis=0), 0.0) for q in range(N_PAIRS)]
        g = [jnp.where(diag, vk[q] + jnp.concatenate([mkg[q][0][D:], mkg[q][1][:D]], axis=0), 0.0)
             for q in range(N_PAIRS)]
        y = [_dot_nt(rp[q], S[q]) for q in range(N_PAIRS)]
        dS = [_dot(S[q], mk[q]) for q in range(N_PAIRS)]
        S = [S[q] * w_last[:, q * PAIR:(q + 1) * PAIR] + dS[q] + g[q] for q in range(N_PAIRS)]
        ys.append(jnp.concatenate([y[q] + y0[q] for q in range(N_PAIRS)], axis=1))
    for q in range(N_PAIRS):
        s_scr[q] = S[q]

    bonus = _seg_sum(r * k2 * rk_ref[...], blk) * v
    y_ref[0] = _group_norm_out(jnp.concatenate(ys, axis=0), bonus, gate, lnw_ref[...], lnb_ref[...], blk)

    @pl.when(t == pl.num_programs(1) - 1)
    def _():
        for q in range(N_PAIRS):
            sout_ref[0, 2 * q] = S[q][:D, :D]
            sout_ref[0, 2 * q + 1] = S[q][D:, D:]


def _rwkv_prompt(proj, mu, w0, a0, k_k, k_a, r_k, lnx_w, lnx_b, wupw, wupa, wupg):
    B, T, _ = proj.shape
    vec = lambda n: _full((1, n))
    return pl.pallas_call(
        _rwkv_prompt_kernel,
        grid=(B, T // RWKV_TILE),
        in_specs=[pl.BlockSpec((1, RWKV_TILE, RWKV_PROJ), lambda b, t: (b, t, 0)),
                  vec(RWKV_PROJ), vec(RWKV_WIDTH), vec(RWKV_WIDTH), vec(RWKV_WIDTH), vec(RWKV_WIDTH),
                  vec(RWKV_WIDTH), vec(RWKV_WIDTH), vec(RWKV_WIDTH),
                  _full((DECAY_LORA, RWKV_WIDTH)), _full((AAA_LORA, RWKV_WIDTH)), _full((GATE_LORA, RWKV_WIDTH))],
        out_specs=[pl.BlockSpec((1, RWKV_TILE, RWKV_WIDTH), lambda b, t: (b, t, 0)),
                   pl.BlockSpec((1, N_RWKV_HEADS, HEAD_DIM, HEAD_DIM), lambda b, t: (b, 0, 0, 0))],
        out_shape=[jax.ShapeDtypeStruct((B, T, RWKV_WIDTH), F32),
                   jax.ShapeDtypeStruct((B, N_RWKV_HEADS, HEAD_DIM, HEAD_DIM), F32)],
        scratch_shapes=[pltpu.VMEM((N_PAIRS, PAIR, PAIR), F32),
                        pltpu.VMEM((1, RWKV_PROJ), F32)],
        compiler_params=_params("arbitrary", "arbitrary"),
        name="rwkv_prompt",
    )(proj, mu, w0, a0, k_k, k_a, r_k, lnx_w, lnx_b, wupw, wupa, wupg)


def _rwkv_step_feat_kernel(p_ref, sh_ref, mu_ref, w0_ref, a0_ref, kk_ref, ka_ref, rk_ref,
                           wupw_ref, wupa_ref, wupg_ref, vecs_ref, bonus_ref, gate_ref):
    p = p_ref[...]
    xs = p + (sh_ref[...] - p) * mu_ref[...]
    r, k2, v, kk, a_sig, logw, gate = _rwkv_features(
        xs, w0_ref[...], a0_ref[...], kk_ref[...], ka_ref[...], wupw_ref[...], wupa_ref[...], wupg_ref[...])
    blk = _head_blocks(RWKV_WIDTH // 2)
    kkn = kk / jnp.maximum(jnp.sqrt(_seg_sum(kk * kk, blk)), 1e-12)
    for i, x in enumerate((-kkn, kkn * a_sig, jnp.exp(logw), k2, r, v)):
        vecs_ref[i] = x.T
    bonus_ref[...] = _seg_sum(r * k2 * rk_ref[...], blk) * v
    gate_ref[...] = gate


def _rwkv_step_state_kernel(s_ref, vecs_ref, sout_ref, y_ref):
    S = s_ref[0]
    a, b, w, k, r, v = (vecs_ref[i] for i in range(6))
    sa = jnp.sum(S * a[None], axis=1)
    S = S * w[None] + sa[:, None, :] * b[None] + v[:, None, :] * k[None]
    sout_ref[0] = S
    y_ref[...] = jnp.sum(S * r[None], axis=1)


def _rwkv_step_out_kernel(yt_ref, bonus_ref, gate_ref, lnw_ref, lnb_ref, o_ref):
    o_ref[...] = _group_norm_out(yt_ref[...].T, bonus_ref[...], gate_ref[...], lnw_ref[...], lnb_ref[...],
                                 _head_blocks(RWKV_WIDTH // 2))


def _rwkv_step(proj, shift, state_t, mu, w0, a0, k_k, k_a, r_k, lnx_w, lnx_b, wupw, wupa, wupg):
    n = proj.shape[0]
    vec = lambda m: _full((1, m))
    rows = _full((n, RWKV_PROJ))
    wide = _full((n, RWKV_WIDTH))
    vecs, bonus, gate = pl.pallas_call(
        _rwkv_step_feat_kernel,
        grid=(1,),
        in_specs=[rows, rows, vec(RWKV_PROJ), vec(RWKV_WIDTH), vec(RWKV_WIDTH), vec(RWKV_WIDTH), vec(RWKV_WIDTH),
                  vec(RWKV_WIDTH),
                  _full((DECAY_LORA, RWKV_WIDTH)), _full((AAA_LORA, RWKV_WIDTH)), _full((GATE_LORA, RWKV_WIDTH))],
        out_specs=[_full((6, RWKV_WIDTH, n)), wide, wide],
        out_shape=[jax.ShapeDtypeStruct((6, RWKV_WIDTH, n), F32),
                   jax.ShapeDtypeStruct((n, RWKV_WIDTH), F32), jax.ShapeDtypeStruct((n, RWKV_WIDTH), F32)],
        compiler_params=_params("arbitrary"),
        name="rwkv_step_feat",
    )(proj, shift, mu, w0, a0, k_k, k_a, r_k, wupw, wupa, wupg)
    st_spec = pl.BlockSpec((1, HEAD_DIM, HEAD_DIM, n), lambda h: (h, 0, 0, 0))
    state_new, yt = pl.pallas_call(
        _rwkv_step_state_kernel,
        grid=(N_RWKV_HEADS,),
        in_specs=[st_spec, pl.BlockSpec((6, HEAD_DIM, n), lambda h: (0, h, 0))],
        out_specs=[st_spec, pl.BlockSpec((HEAD_DIM, n), lambda h: (h, 0))],
        out_shape=[jax.ShapeDtypeStruct(state_t.shape, F32), jax.ShapeDtypeStruct((RWKV_WIDTH, n), F32)],
        compiler_params=_params("arbitrary"),
        name="rwkv_step_state",
    )(state_t, vecs)
    y = pl.pallas_call(
        _rwkv_step_out_kernel,
        grid=(1,),
        in_specs=[_full((RWKV_WIDTH, n)), wide, wide, vec(RWKV_WIDTH), vec(RWKV_WIDTH)],
        out_specs=wide,
        out_shape=jax.ShapeDtypeStruct((n, RWKV_WIDTH), F32),
        compiler_params=_params("arbitrary"),
        name="rwkv_step_out",
    )(yt, bonus, gate, lnx_w, lnx_b)
    return y, state_new


def _t5_bucket_np(dist):
    max_exact = N_BUCKETS // 2
    d = np.maximum(dist, 1).astype(np.float32)
    large = max_exact + (np.log(d / np.float32(max_exact)) / np.float32(math.log(MAX_DISTANCE / max_exact))
                         * np.float32(N_BUCKETS - max_exact)).astype(np.int32)
    large = np.minimum(large, N_BUCKETS - 1)
    return np.where(dist < max_exact, dist, large).astype(np.int32)


def _prompt_bucket_table():
    qi = np.arange(BLOCK)[:, None]
    kj = np.arange(2 * BLOCK)[None, :]
    dist = BLOCK + qi - kj
    valid = (dist >= 0) & (dist <= WINDOW)
    return np.where(valid, _t5_bucket_np(np.maximum(dist, 0)), -1).astype(np.int32)


def _decode_bucket_table():
    dist = WINDOW - np.arange(WINDOW)
    return np.broadcast_to(_t5_bucket_np(dist)[None, :], (8, WINDOW)).astype(np.int32).copy()


def _bias_from_buckets(bkt, relb_ref, h, init):
    acc = jnp.full(bkt.shape, init, F32)
    for b in range(N_BUCKETS):
        acc = jnp.where(bkt == b, relb_ref[b, h], acc)
    return acc


ATT_TILE = 4 * BLOCK
ATT_GROUP = 8


def _attn_prompt_kernel(qs_ref, qm_ref, kc_ref, kp_ref, vc_ref, vp_ref, mk_ref, mv_ref, bkt_ref, relb_ref, sink_ref,
                        sqg_ref, skg_ref, mqg_ref, ys_ref, ym_ref, kn_ref, bias_scr, mk_scr, mv_scr):
    i = pl.program_id(1)
    grp = N_SWA_HEADS // N_SWA_KV_HEADS
    ones = jnp.ones((2 * BLOCK, HEAD_DIM), F32)

    @pl.when(i == 0)
    def _():
        bkt = bkt_ref[...]
        for j in range(N_SWA_KV_HEADS):
            for g in range(grp):
                bias_scr[j, g * BLOCK:(g + 1) * BLOCK, :] = _bias_from_buckets(bkt, relb_ref, j * grp + g, NEG) * LOG2E
        mk = mk_ref[0]
        mv = mv_ref[0]
        for h in range(N_MEM_HEADS):
            sl = slice(h * HEAD_DIM, (h + 1) * HEAD_DIM)
            mk_scr[h] = mk[:, sl].astype(BF16)
            mv_scr[h] = jnp.concatenate([mv[:, sl], ones], axis=1).astype(BF16)

    qs = qs_ref[0]
    qm = qm_ref[0]
    kc = kc_ref[0]
    kp = kp_ref[0]
    vc = vc_ref[0]
    vp = vp_ref[0]
    sqg, skg, mqg = sqg_ref[...], skg_ref[...], mqg_ref[...]
    rowi = lax.broadcasted_iota(jnp.int32, (2 * BLOCK, 1), 0)
    col = lax.broadcasted_iota(jnp.int32, (2 * BLOCK, 2 * BLOCK), 1)
    pad_mask = jnp.logical_and(i == 0, col < BLOCK)

    hsl = [slice(h * HEAD_DIM, (h + 1) * HEAD_DIM) for h in range(N_SWA_HEADS)]
    qs_n = _rms_heads(qs, sqg, N_SWA_HEADS) * (ATTN_SCALE * LOG2E)
    qm_n = _rms_heads(qm, mqg, N_MEM_HEADS) * (ATTN_SCALE * LOG2E)
    kc_n = _rms_heads(kc, skg, N_SWA_KV_HEADS)
    kp_n = _rms_heads(kp, skg, N_SWA_KV_HEADS)
    kn_ref[0] = kc_n[ATT_TILE - BLOCK:]
    chains = [(a, j) for a in range(ATT_TILE // BLOCK) for j in range(N_SWA_KV_HEADS)]
    lhs, keys, vals, sinkcol = [], [], [], []
    for a, j in chains:
        rs = slice(a * BLOCK, (a + 1) * BLOCK)
        lhs.append(jnp.concatenate([qs_n[rs, hsl[j * grp + g]] for g in range(grp)], axis=0))
        if a == 0:
            keys.append(jnp.concatenate([kp_n[:, hsl[j]], kc_n[:BLOCK, hsl[j]]], axis=0))
            vv = jnp.concatenate([vp[:, hsl[j]], vc[:BLOCK, hsl[j]]], axis=0)
        else:
            ks = slice((a - 1) * BLOCK, (a + 1) * BLOCK)
            keys.append(kc_n[ks, hsl[j]])
            vv = vc[ks, hsl[j]]
        vals.append(jnp.concatenate([vv, ones], axis=1))
        sinkcol.append(jnp.where(rowi < BLOCK, sink_ref[0, j * grp], sink_ref[0, j * grp + 1]) * LOG2E)
    qmn = [qm_n[:, hsl[h]] for h in range(N_MEM_HEADS)]

    bias = [bias_scr[j] for j in range(N_SWA_KV_HEADS)]
    mem_k = [mk_scr[h] for h in range(N_MEM_HEADS)]
    mem_v = [mv_scr[h] for h in range(N_MEM_HEADS)]
    n_w = len(chains)
    n_tasks = n_w + N_MEM_HEADS

    def scores(t):
        return _dot_nt(lhs[t], keys[t]) if t < n_w else _dot_nt(qmn[t - n_w], mem_k[t - n_w])

    def finish(ts, s):
        e, extra = {}, {}
        for t in ts:
            if t < n_w:
                a, j = chains[t]
                st = s[t] + bias[j]
                if a == 0:
                    st = jnp.where(pad_mask, NEG, st)
                m = jnp.maximum(jnp.max(st, -1, keepdims=True), sinkcol[t])
                e[t], extra[t] = jnp.exp2(st - m), jnp.exp2(sinkcol[t] - m)
            else:
                e[t], extra[t] = jnp.exp2(s[t] - jnp.max(s[t], -1, keepdims=True)), 0.0
        o_full = {t: _dot(e[t], vals[t] if t < n_w else mem_v[t - n_w]) for t in ts}
        return {t: o_full[t][:, :HEAD_DIM] / (o_full[t][:, HEAD_DIM:HEAD_DIM + 1] + extra[t]) for t in ts}

    groups = [list(range(g, min(g + ATT_GROUP, n_tasks))) for g in range(0, n_tasks, ATT_GROUP)]
    s, outs = {}, {}
    for gi, ts in enumerate(groups):
        if gi == 0:
            s.update({t: scores(t) for t in ts})
        if gi + 1 < len(groups):
            s.update({t: scores(t) for t in groups[gi + 1]})
        outs.update(finish(ts, s))
    for t, (a, j) in enumerate(chains):
        for g in range(grp):
            ys_ref[0, a * BLOCK:(a + 1) * BLOCK, hsl[j * grp + g]] = outs[t][g * BLOCK:(g + 1) * BLOCK]
    for h in range(N_MEM_HEADS):
        ym_ref[0, :, hsl[h]] = outs[n_w + h]


def _attn_prompt(proj, mk, mv, rel_bias, sinks, sqg, skg, mqg):
    B, T, _ = proj.shape
    bkt = jnp.asarray(_prompt_bucket_table())
    smem = pl.BlockSpec(memory_space=pltpu.SMEM)
    kblk, vblk = COL_SK // SWA_KV_WIDTH, COL_SV // SWA_KV_WIDTH
    prev = lambda b, i: (b, jnp.maximum((ATT_TILE // BLOCK) * i - 1, 0))
    memb = pl.BlockSpec((1, N_MEM, MEM_WIDTH), lambda b, i: (b, 0, 0))
    return pl.pallas_call(
        _attn_prompt_kernel,
        grid=(B, T // ATT_TILE),
        in_specs=[pl.BlockSpec((1, ATT_TILE, SWA_WIDTH), lambda b, i: (b, i, COL_SQ // SWA_WIDTH)),
                  pl.BlockSpec((1, ATT_TILE, MEM_WIDTH), lambda b, i: (b, i, COL_MQ // MEM_WIDTH)),
                  pl.BlockSpec((1, ATT_TILE, SWA_KV_WIDTH), lambda b, i: (b, i, kblk)),
                  pl.BlockSpec((1, BLOCK, SWA_KV_WIDTH), lambda b, i: prev(b, i) + (kblk,)),
                  pl.BlockSpec((1, ATT_TILE, SWA_KV_WIDTH), lambda b, i: (b, i, vblk)),
                  pl.BlockSpec((1, BLOCK, SWA_KV_WIDTH), lambda b, i: prev(b, i) + (vblk,)),
                  memb, memb,
                  _full((BLOCK, 2 * BLOCK)), smem, smem,
                  _full((1, HEAD_DIM)), _full((1, HEAD_DIM)), _full((1, HEAD_DIM))],
        out_specs=[pl.BlockSpec((1, ATT_TILE, SWA_WIDTH), lambda b, i: (b, i, 0)),
                   pl.BlockSpec((1, ATT_TILE, MEM_WIDTH), lambda b, i: (b, i, 0)),
                   pl.BlockSpec((1, BLOCK, SWA_KV_WIDTH), lambda b, i: (b, 0, 0))],
        out_shape=[jax.ShapeDtypeStruct((B, T, SWA_WIDTH), F32),
                   jax.ShapeDtypeStruct((B, T, MEM_WIDTH), F32),
                   jax.ShapeDtypeStruct((B, BLOCK, SWA_KV_WIDTH), F32)],
        scratch_shapes=[pltpu.VMEM((N_SWA_KV_HEADS, 2 * BLOCK, 2 * BLOCK), F32),
                        pltpu.VMEM((N_MEM_HEADS, N_MEM, HEAD_DIM), BF16),
                        pltpu.VMEM((N_MEM_HEADS, N_MEM, 2 * HEAD_DIM), BF16)],
        compiler_params=_params("arbitrary", "arbitrary"),
        name="attn_prompt",
    )(proj, proj, proj, proj, proj, proj, mk, mv, bkt, rel_bias, sinks, sqg, skg, mqg)


def _memory_kv_kernel(mem_ref, g_ref, w_ref, kg_ref, mk_ref, mv_ref):
    kv = jnp.dot(_rms(mem_ref[0], g_ref[...]).astype(BF16), w_ref[...], preferred_element_type=F32)
    kg = kg_ref[...]
    for h in range(N_MEM_HEADS):
        sl = slice(h * HEAD_DIM, (h + 1) * HEAD_DIM)
        mk_ref[0, :, sl] = _rms(kv[:, sl], kg)
    mv_ref[0] = kv[:, MEM_WIDTH:]


def _memory_kv(mem, g, w_bf16, kg):
    B = mem.shape[0]
    blk = pl.BlockSpec((1, N_MEM, MEM_WIDTH), lambda b: (b, 0, 0))
    return pl.pallas_call(
        _memory_kv_kernel,
        grid=(B,),
        in_specs=[pl.BlockSpec((1, N_MEM, D_MODEL), lambda b: (b, 0, 0)),
                  _full((1, D_MODEL)), _full((D_MODEL, 2 * MEM_WIDTH)), _full((1, HEAD_DIM))],
        out_specs=[blk, blk],
        out_shape=[jax.ShapeDtypeStruct((B, N_MEM, MEM_WIDTH), F32)] * 2,
        compiler_params=_params("arbitrary"),
        name="memory_kv",
    )(mem, g, w_bf16, kg)


SEQ_TILE = 8


def _rms_heads(x, g, n_heads):
    ms = jnp.dot((x * x).astype(BF16), _head_blocks(n_heads * HEAD_DIM), preferred_element_type=F32)
    return x * lax.rsqrt(ms * (1.0 / HEAD_DIM) + NORM_EPS) * jnp.concatenate([g] * n_heads, axis=1)


def _decode_attn_kernel(p_ref, kbuf_ref, vbuf_ref, mk_ref, mv_ref, bkt_ref, relb_ref, sink_ref,
                        sqg_ref, skg_ref, mqg_ref, ys_ref, ym_ref, kout_ref, vout_ref, tab_scr):
    grp = N_SWA_HEADS // N_SWA_KV_HEADS

    @pl.when(pl.program_id(0) == 0)
    def _():
        hrow = lax.broadcasted_iota(jnp.int32, (8, WINDOW), 0)
        lane = lax.broadcasted_iota(jnp.int32, (8, WINDOW), 1)
        bias_w = jnp.zeros((8, WINDOW), F32)
        cols = jnp.zeros((8, WINDOW), F32)
        for h in range(N_SWA_HEADS):
            bias_w = jnp.where(hrow == h, _bias_from_buckets(bkt_ref[...], relb_ref, h, 0.0), bias_w)
            cols = jnp.where(jnp.logical_and(hrow == h, lane == 0), relb_ref[0, h], cols)
            cols = jnp.where(jnp.logical_and(hrow == h, lane == 1), sink_ref[0, h], cols)
        tab_scr[0] = bias_w
        tab_scr[1] = cols

    bias_w = tab_scr[0][:N_SWA_HEADS]
    bias_new = tab_scr[1][:N_SWA_HEADS, 0:1]
    sink = tab_scr[1][:N_SWA_HEADS, 1:2]
    rowi = lax.broadcasted_iota(jnp.int32, (SWA_KV_WIDTH, WINDOW), 0)
    lanei = lax.broadcasted_iota(jnp.int32, (SWA_KV_WIDTH, WINDOW), 1)
    eye = rowi == lanei
    NH = N_SWA_HEADS
    own = (lax.broadcasted_iota(jnp.int32, (NH, NH * HEAD_DIM), 1) // HEAD_DIM
           == lax.broadcasted_iota(jnp.int32, (NH, NH * HEAD_DIM), 0))

    p = p_ref[...]
    qn = _rms_heads(p[:, COL_SQ:COL_SQ + SWA_WIDTH], sqg_ref[...], N_SWA_HEADS) * ATTN_SCALE
    kn = _rms_heads(p[:, COL_SK:COL_SK + SWA_KV_WIDTH], skg_ref[...], N_SWA_KV_HEADS)
    vn = p[:, COL_SV:COL_SV + SWA_KV_WIDTH]
    qmn = _rms_heads(p[:, COL_MQ:COL_MQ + MEM_WIDTH], mqg_ref[...], N_MEM_HEADS) * ATTN_SCALE
    rep = lambda x: jnp.concatenate([x[:, j * HEAD_DIM:(j + 1) * HEAD_DIM] for j in range(N_SWA_KV_HEADS)
                                     for _ in range(grp)], axis=1)
    kn_rep, vn_rep = rep(kn), rep(vn)

    B = range(SEQ_TILE)
    dup = lambda c: jnp.concatenate([c[j * HEAD_DIM:(j + 1) * HEAD_DIM] for j in range(N_SWA_KV_HEADS)
                                     for _ in range(grp)], axis=0)
    qd = [jnp.where(own, qn[b:b + 1, :], 0.0) for b in B]
    qmd = [jnp.where(own, qmn[b:b + 1, :], 0.0) for b in B]
    kdup = [dup(kbuf_ref[b]) for b in B]
    vdup = [dup(vbuf_ref[b]) for b in B]
    s = [_dot(qd[b], kdup[b]) + bias_w for b in B]
    sm = [_dot(qmd[b], mk_ref[b]) for b in B]
    s_new = [jnp.sum(qd[b] * kn_rep[b:b + 1, :], -1, keepdims=True) + bias_new for b in B]
    m = [jnp.maximum(jnp.maximum(jnp.max(s[b], -1, keepdims=True), s_new[b]), sink) for b in B]
    e = [jnp.exp(s[b] - m[b]) for b in B]
    e_new = [jnp.exp(s_new[b] - m[b]) for b in B]
    den = [jnp.sum(e[b], -1, keepdims=True) + e_new[b] + jnp.exp(sink - m[b]) for b in B]
    em = [jnp.exp(sm[b] - jnp.max(sm[b], -1, keepdims=True)) for b in B]
    ov = [_dot_nt(e[b], vdup[b]) for b in B]
    omf = [_dot_nt(em[b], mv_ref[b]) for b in B]
    ys_rows = [jnp.sum(jnp.where(own, (ov[b] + e_new[b] * vn_rep[b:b + 1, :]) / den[b], 0.0), 0, keepdims=True)
               for b in B]
    ym_rows = [jnp.sum(jnp.where(own, omf[b] / jnp.sum(em[b], -1, keepdims=True), 0.0), 0, keepdims=True)
               for b in B]
    ys_ref[...] = jnp.concatenate(ys_rows, axis=0)
    ym_ref[...] = jnp.concatenate(ym_rows, axis=0)
    for b in B:
        kn_col = jnp.sum(jnp.where(eye, kn[b:b + 1, :], 0.0), -1, keepdims=True)
        vn_col = jnp.sum(jnp.where(eye, vn[b:b + 1, :], 0.0), -1, keepdims=True)
        kout_ref[b] = jnp.where(lanei == WINDOW - 1, kn_col, pltpu.roll(kbuf_ref[b], WINDOW - 1, axis=1))
        vout_ref[b] = jnp.where(lanei == WINDOW - 1, vn_col, pltpu.roll(vbuf_ref[b], WINDOW - 1, axis=1))


def _decode_attn(proj, kbuf, vbuf, mk, mv, rel_bias, sinks, sqg, skg, mqg):
    n = proj.shape[0]
    bkt = jnp.asarray(_decode_bucket_table())
    smem = pl.BlockSpec(memory_space=pltpu.SMEM)
    win = pl.BlockSpec((SEQ_TILE, WINDOW, SWA_KV_WIDTH), lambda i: (i, 0, 0))
    memb = pl.BlockSpec((SEQ_TILE, N_MEM, MEM_WIDTH), lambda i: (i, 0, 0))
    return pl.pallas_call(
        _decode_attn_kernel,
        grid=(n // SEQ_TILE,),
        in_specs=[pl.BlockSpec((SEQ_TILE, IN_PROJ), lambda i: (i, 0)), win, win, memb, memb,
                  _full((8, WINDOW)), smem, smem, _full((1, HEAD_DIM)), _full((1, HEAD_DIM)), _full((1, HEAD_DIM))],
        out_specs=[pl.BlockSpec((SEQ_TILE, SWA_WIDTH), lambda i: (i, 0)),
                   pl.BlockSpec((SEQ_TILE, MEM_WIDTH), lambda i: (i, 0)), win, win],
        out_shape=[jax.ShapeDtypeStruct((n, SWA_WIDTH), F32), jax.ShapeDtypeStruct((n, MEM_WIDTH), F32),
                   jax.ShapeDtypeStruct(kbuf.shape, F32), jax.ShapeDtypeStruct(vbuf.shape, F32)],
        scratch_shapes=[pltpu.VMEM((2, 8, WINDOW), F32)],
        compiler_params=_params("arbitrary"),
        name="decode_attn",
    )(proj, kbuf, vbuf, mk, mv, bkt, rel_bias, sinks, sqg, skg, mqg)


FF_CHUNK = 1024
OUT_FFN_TILE = 512


def _out_ffn_rows(x, yr, ys, ym, wo_ref, g2_ref, w1_ref, w2_ref):
    x1 = (x
          + jnp.dot(yr.astype(BF16), wo_ref[0:RWKV_WIDTH, :], preferred_element_type=F32)
          + jnp.dot(ys.astype(BF16), wo_ref[RWKV_WIDTH:RWKV_WIDTH + SWA_WIDTH, :], preferred_element_type=F32)
          + jnp.dot(ym.astype(BF16), wo_ref[RWKV_WIDTH + SWA_WIDTH:, :], preferred_element_type=F32))
    h2 = _rms(x1, g2_ref[...]).astype(BF16)
    ff = None
    for c in range(D_FF // FF_CHUNK):
        u = jnp.dot(h2, w1_ref[:, c * FF_CHUNK:(c + 1) * FF_CHUNK], preferred_element_type=F32)
        u = jnp.square(jnp.maximum(u, 0.0)).astype(BF16)
        d = jnp.dot(u, w2_ref[c * FF_CHUNK:(c + 1) * FF_CHUNK, :], preferred_element_type=F32)
        ff = d if ff is None else ff + d
    return x1 + ff


def _out_ffn_kernel(x_ref, yr_ref, ys_ref, ym_ref, xs_ref, yrs_ref, yss_ref, yms_ref,
                    wo_ref, g2_ref, w1_ref, w2_ref, o_ref, os_ref):
    o_ref[...] = _out_ffn_rows(x_ref[...], yr_ref[...], ys_ref[...], ym_ref[...], wo_ref, g2_ref, w1_ref, w2_ref)

    @pl.when(pl.program_id(0) == pl.num_programs(0) - 1)
    def _():
        os_ref[...] = _out_ffn_rows(xs_ref[...], yrs_ref[...], yss_ref[...], yms_ref[...],
                                    wo_ref, g2_ref, w1_ref, w2_ref)


def _out_ffn(prompt, sample, wo, g2, w1, w2, tm):
    n, ns = prompt[0].shape[0], sample[0].shape[0]
    widths = (D_MODEL, RWKV_WIDTH, SWA_WIDTH, MEM_WIDTH)
    const = lambda shape: pl.BlockSpec(shape, lambda i: (0, 0), pipeline_mode=pl.Buffered(1))
    return pl.pallas_call(
        _out_ffn_kernel,
        grid=(n // tm,),
        in_specs=([pl.BlockSpec((tm, w), lambda i: (i, 0)) for w in widths] + [_full((ns, w)) for w in widths]
                  + [const((D_MODEL, D_MODEL)), _full((1, D_MODEL)), const((D_MODEL, D_FF)), const((D_FF, D_MODEL))]),
        out_specs=[pl.BlockSpec((tm, D_MODEL), lambda i: (i, 0)), _full((ns, D_MODEL))],
        out_shape=[jax.ShapeDtypeStruct((n, D_MODEL), F32), jax.ShapeDtypeStruct((ns, D_MODEL), F32)],
        compiler_params=_params("arbitrary"),
        name="out_ffn",
    )(*prompt, *sample, wo, g2, w1, w2)


def kernel(x_prompt, x_sample, state_rwkv, state_shift, cache_swa_k, cache_swa_v, cache_mem_k, cache_mem_v,
           mem_prompt, rel_bias, norm1_g, w_in, mu_shift, w0, w_up_w, a0, w_up_a, w_up_g, k_k, k_a, r_k,
           lnx_w, lnx_b, q_norm_swa, k_norm_swa, sinks, mem_norm_g, w_mem_kv, q_norm_mem, k_norm_mem,
           w_out, norm2_g, w_ff1, w_ff2):
    B, T, _ = x_prompt.shape
    Bd = x_sample.shape[0]
    l = 0
    w_in_b = w_in[l].astype(BF16)
    w_out_b = w_out[l].astype(BF16)
    w1_b = w_ff1[l].astype(BF16)
    w2_b = w_ff2[l].astype(BF16)
    rwkv_params = (mu_shift[l][None], w0[l][None], a0[l][None], k_k[l][None], k_a[l][None],
                   r_k[l].reshape(1, RWKV_WIDTH), lnx_w[l][None], lnx_b[l][None],
                   w_up_w[l].astype(BF16), w_up_a[l].astype(BF16), w_up_g[l].astype(BF16))
    sqg, skg, mqg, mkg = q_norm_swa[l][None], k_norm_swa[l][None], q_norm_mem[l][None], k_norm_mem[l][None]
    g1, g2 = norm1_g[l][None], norm2_g[l][None]

    xp = x_prompt.reshape(B * T, D_MODEL)
    xs = x_sample.reshape(Bd, D_MODEL)
    proj_p, proj_s = _in_proj(xp, xs, g1, w_in_b, IN_PROJ_TILE)
    proj_p = proj_p.reshape(B, T, IN_PROJ)

    mk, mv = _memory_kv(mem_prompt, mem_norm_g[l][None], w_mem_kv[l].astype(BF16), mkg)
    yr_p, s_p = _rwkv_prompt(proj_p, *rwkv_params)
    ys_p, ym_p, kn_p = _attn_prompt(proj_p, mk, mv, rel_bias, sinks[l][None], sqg, skg, mqg)
    shift_p = proj_p[:, T - 1, :RWKV_PROJ]
    vb_p = proj_p[:, T - WINDOW:, COL_SV:COL_SV + SWA_KV_WIDTH]

    yr_s, st_s = _rwkv_step(proj_s[:, :RWKV_PROJ], state_shift[l], jnp.transpose(state_rwkv[l], (1, 2, 3, 0)),
                            *rwkv_params)
    s_s = jnp.transpose(st_s, (3, 0, 1, 2))
    fmajor = lambda c: jnp.transpose(c, (0, 2, 3, 1)).reshape(Bd, c.shape[2] * HEAD_DIM, c.shape[1])
    ys_s, ym_s, kb_s, vb_s = _decode_attn(
        proj_s, fmajor(cache_swa_k[l]), fmajor(cache_swa_v[l]), fmajor(cache_mem_k[l]), fmajor(cache_mem_v[l]),
        rel_bias, sinks[l][None], sqg, skg, mqg)
    pmajor = lambda c: jnp.transpose(c.reshape(Bd, N_SWA_KV_HEADS, HEAD_DIM, WINDOW), (0, 3, 1, 2))[None]

    y_p, y_s = _out_ffn((xp, yr_p.reshape(B * T, RWKV_WIDTH), ys_p.reshape(B * T, SWA_WIDTH),
                         ym_p.reshape(B * T, MEM_WIDTH)), (xs, yr_s, ys_s, ym_s),
                        w_out_b, g2, w1_b, w2_b, OUT_FFN_TILE)
    y_p = y_p.reshape(B, T, D_MODEL)
    y_s = y_s.reshape(Bd, 1, D_MODEL)

    return (y_p, y_s,
            s_p[None], shift_p[None],
            kn_p.reshape(1, B, WINDOW, N_SWA_KV_HEADS, HEAD_DIM),
            vb_p.reshape(1, B, WINDOW, N_SWA_KV_HEADS, HEAD_DIM),
            mk.reshape(1, B, N_MEM, N_MEM_HEADS, HEAD_DIM), mv.reshape(1, B, N_MEM, N_MEM_HEADS, HEAD_DIM),
            s_s[None], proj_s[:, :RWKV_PROJ][None],
            pmajor(kb_s), pmajor(vb_s))
```

```python
import functools
import math

import numpy as np
import jax
import jax.numpy as jnp
from jax import lax
from jax.experimental import pallas as pl
from jax.experimental.pallas import tpu as pltpu

F32 = jnp.float32
BF16 = jnp.bfloat16

D_MODEL = 1024
HEAD_DIM = 64
RWKV_WIDTH = 512
N_RWKV_HEADS = 8
SWA_WIDTH = 256
N_SWA_HEADS = 4
N_SWA_KV_HEADS = 2
SWA_KV_WIDTH = 128
MEM_WIDTH = 256
N_MEM_HEADS = 4
N_MEM = 256
WINDOW = 128
BLOCK = 128
N_BUCKETS = 32
MAX_DISTANCE = 128
DECAY_LORA = 64
AAA_LORA = 64
GATE_LORA = 128
RWKV_PROJ = 3 * RWKV_WIDTH + DECAY_LORA + AAA_LORA + GATE_LORA
SWA_PROJ = SWA_WIDTH + 2 * SWA_KV_WIDTH
IN_PROJ = RWKV_PROJ + SWA_PROJ + MEM_WIDTH
D_FF = 4 * D_MODEL
NORM_EPS = 1e-6
LNX_EPS = 64e-5
ATTN_SCALE = HEAD_DIM ** -0.5
EXP_M05 = math.exp(-0.5)
LOG2E = math.log2(math.e)
NEG = -1e30

COL_R, COL_K, COL_V = 0, RWKV_WIDTH, 2 * RWKV_WIDTH
COL_WD = 3 * RWKV_WIDTH
COL_AD = COL_WD + DECAY_LORA
COL_GD = COL_AD + AAA_LORA
COL_SQ = RWKV_PROJ
COL_SK = COL_SQ + SWA_WIDTH
COL_SV = COL_SK + SWA_KV_WIDTH
COL_MQ = RWKV_PROJ + SWA_PROJ

CHUNK = 64
VMEM_LIMIT = 56 * 1024 * 1024


def _dot(a, b):
    return jnp.dot(a.astype(BF16), b.astype(BF16), preferred_element_type=F32)


def _dot_nt(a, b):
    return lax.dot_general(a.astype(BF16), b.astype(BF16), (((1,), (1,)), ((), ())),
                           preferred_element_type=F32)


def _dot_tn(a, b):
    return lax.dot_general(a.astype(BF16), b.astype(BF16), (((0,), (0,)), ((), ())),
                           preferred_element_type=F32)


def _rms(x, g):
    return x * lax.rsqrt(jnp.mean(x * x, -1, keepdims=True) + NORM_EPS) * g


def _params(*sem):
    return pltpu.CompilerParams(dimension_semantics=sem, vmem_limit_bytes=VMEM_LIMIT)


def _full(shape):
    n = len(shape)
    return pl.BlockSpec(shape, lambda *_: (0,) * n)


def _head_blocks(width):
    bi = lax.broadcasted_iota(jnp.int32, (width, width), 0) // HEAD_DIM
    bj = lax.broadcasted_iota(jnp.int32, (width, width), 1) // HEAD_DIM
    return jnp.where(bi == bj, 1.0, 0.0).astype(BF16)


IN_PROJ_TILE = 1024
IN_PROJ_SUB = 256


def _in_proj_kernel(x_ref, xs_ref, g_ref, w_ref, o_ref, os_ref):
    tm = x_ref.shape[0]
    for j in range(tm // IN_PROJ_SUB):
        rows = slice(j * IN_PROJ_SUB, (j + 1) * IN_PROJ_SUB)
        h = _rms(x_ref[rows, :], g_ref[...])
        o_ref[rows, :] = jnp.dot(h.astype(BF16), w_ref[...], preferred_element_type=F32)

    @pl.when(pl.program_id(0) == pl.num_programs(0) - 1)
    def _():
        hs = _rms(xs_ref[...], g_ref[...])
        os_ref[...] = jnp.dot(hs.astype(BF16), w_ref[...], preferred_element_type=F32)


def _in_proj(x2d, xs2d, g, w_bf16, tm):
    n, ns = x2d.shape[0], xs2d.shape[0]
    return pl.pallas_call(
        _in_proj_kernel,
        grid=(n // tm,),
        in_specs=[pl.BlockSpec((tm, D_MODEL), lambda i: (i, 0)),
                  _full((ns, D_MODEL)),
                  _full((1, D_MODEL)),
                  _full((D_MODEL, IN_PROJ))],
        out_specs=[pl.BlockSpec((tm, IN_PROJ), lambda i: (i, 0)), _full((ns, IN_PROJ))],
        out_shape=[jax.ShapeDtypeStruct((n, IN_PROJ), F32), jax.ShapeDtypeStruct((ns, IN_PROJ), F32)],
        compiler_params=_params("arbitrary"),
        name="in_proj",
    )(x2d, xs2d, g, w_bf16)


def _rwkv_features(xs, w0, a0, k_k, k_a, wupw, wupa, wupg):
    r = xs[:, COL_R:COL_R + RWKV_WIDTH]
    k = xs[:, COL_K:COL_K + RWKV_WIDTH]
    v = xs[:, COL_V:COL_V + RWKV_WIDTH]
    wd = xs[:, COL_WD:COL_WD + DECAY_LORA]
    ad = xs[:, COL_AD:COL_AD + AAA_LORA]
    gd = xs[:, COL_GD:COL_GD + GATE_LORA]
    logw = -jax.nn.sigmoid(w0 + _dot(jnp.tanh(wd), wupw)) * EXP_M05
    a_sig = jax.nn.sigmoid(a0 + _dot(ad, wupa))
    gate = _dot(jax.nn.sigmoid(gd), wupg)
    kk = k * k_k
    k2 = k * (1.0 + (a_sig - 1.0) * k_a)
    return r, k2, v, kk, a_sig, logw, gate


def _seg_sum(x, blk):
    xb = x.astype(BF16)
    half = RWKV_WIDTH // 2
    return jnp.concatenate([jnp.dot(xb[:, :half], blk, preferred_element_type=F32),
                            jnp.dot(xb[:, half:], blk, preferred_element_type=F32)], axis=1)


def _group_norm_out(y, bonus, gate, lnx_w, lnx_b, blk):
    inv_d = 1.0 / HEAD_DIM
    m = _seg_sum(y, blk) * inv_d
    d = y - m
    var = _seg_sum(d * d, blk) * inv_d
    yn = d * lax.rsqrt(var + LNX_EPS) * lnx_w + lnx_b
    return (yn + bonus) * gate


RWKV_TILE = 4 * CHUNK
PAIR = 2 * HEAD_DIM
N_PAIRS = N_RWKV_HEADS // 2


def _rwkv_prompt_kernel(p_ref, mu_ref, w0_ref, a0_ref, kk_ref, ka_ref, rk_ref, lnw_ref, lnb_ref,
                        wupw_ref, wupa_ref, wupg_ref, y_ref, sout_ref, s_scr, prev_scr):
    C, TT, D = CHUNK, RWKV_TILE, HEAD_DIM
    NC = TT // C
    t = pl.program_id(1)

    @pl.when(t == 0)
    def _():
        s_scr[...] = jnp.zeros_like(s_scr)
        prev_scr[...] = jnp.zeros_like(prev_scr)

    p = p_ref[0]
    row = lax.broadcasted_iota(jnp.int32, p.shape, 0)
    prev = jnp.where(row == 0, prev_scr[...], pltpu.roll(p, 1, axis=0))
    prev_scr[...] = p[TT - 1:TT, :]
    xs = p + (prev - p) * mu_ref[...]
    r, k2, v, kk, a_sig, logw, gate = _rwkv_features(
        xs, w0_ref[...], a0_ref[...], kk_ref[...], ka_ref[...], wupw_ref[...], wupa_ref[...], wupg_ref[...])

    blk = _head_blocks(RWKV_WIDTH // 2)
    kkn = kk / jnp.maximum(jnp.sqrt(_seg_sum(kk * kk, blk)), 1e-12)
    bb = kkn * a_sig

    ri = lax.broadcasted_iota(jnp.int32, (TT, TT), 0)
    ci = lax.broadcasted_iota(jnp.int32, (TT, TT), 1)
    tri = jnp.where(jnp.logical_and(ri >= ci, ri // C == ci // C), 1.0, 0.0).astype(BF16)
    lw2 = logw * LOG2E
    l1 = lw2.astype(BF16)
    l2 = (lw2 - l1.astype(F32)).astype(BF16)
    cum = jnp.dot(tri, l1, preferred_element_type=F32) + jnp.dot(tri, l2, preferred_element_type=F32)
    c_last = jnp.concatenate([jnp.broadcast_to(cum[(c + 1) * C - 1:(c + 1) * C, :], (C, RWKV_WIDTH))
                              for c in range(NC)], axis=0)
    e_pos = jnp.exp2(cum)
    e_neg = jnp.exp2(-cum)
    e_prev = jnp.exp2(cum - lw2)
    e_last = jnp.exp2(c_last - cum)

    lo_full = (lax.broadcasted_iota(jnp.int32, (TT, RWKV_WIDTH), 1) % PAIR) < D
    at_f = -kkn * e_prev
    rt_f = r * e_pos
    at_lo = jnp.where(lo_full, at_f, 0.0).astype(BF16)
    at_hi = jnp.where(lo_full, 0.0, at_f).astype(BF16)
    rt_lo = jnp.where(lo_full, rt_f, 0.0).astype(BF16)
    rt_hi = jnp.where(lo_full, 0.0, rt_f).astype(BF16)
    bt_b = (bb * e_neg).astype(BF16)
    kt_b = (k2 * e_neg).astype(BF16)
    bh_b = (bb * e_last).astype(BF16)
    kh_b = (k2 * e_last).astype(BF16)
    v_b = v.astype(BF16)

    r2 = lax.broadcasted_iota(jnp.int32, (C, PAIR), 0)
    c2 = lax.broadcasted_iota(jnp.int32, (C, PAIR), 1)
    lo = c2 < D
    c2m = jnp.where(lo, c2, c2 - C)
    mask_a = jnp.logical_and(lo, c2 < r2)
    mask_ak = jnp.logical_and(jnp.logical_not(lo), c2m < r2)
    mask_r = c2m <= r2
    eye_hi = jnp.where(jnp.logical_and(jnp.logical_not(lo), c2m == r2), 1.0, 0.0)
    zeros_cp = jnp.zeros((C, PAIR), F32)
    qi = lax.broadcasted_iota(jnp.int32, (PAIR, PAIR), 0) // D
    qj = lax.broadcasted_iota(jnp.int32, (PAIR, PAIR), 1) // D
    diag = qi == qj

    PR = [(c, q) for c in range(NC) for q in range(N_PAIRS)]
    n = len(PR)
    E = range(2)
    win = lambda x, c, q: x[c * C:(c + 1) * C, q * PAIR:(q + 1) * PAIR]
    sc = [_dot_nt(jnp.concatenate([win(at_lo, c, q), win(at_hi, c, q), win(rt_lo, c, q), win(rt_hi, c, q)], axis=0),
                  jnp.concatenate([win(bt_b, c, q), win(kt_b, c, q)], axis=0)) for c, q in PR]
    vr = [pltpu.roll(win(v, c, q), D, axis=1) for c, q in PR]
    vvr = [jnp.concatenate([vr[i], vr[i]], axis=0).astype(BF16) for i in range(n)]
    m_ak = [[jnp.where(mask_ak, sc[i][e * C:(e + 1) * C], 0.0) for e in E] for i in range(n)]
    m_r = [[jnp.where(mask_r, sc[i][(2 + e) * C:(3 + e) * C], 0.0) for e in E] for i in range(n)]

    zf = [[_dot(m_ak[i][e], vvr[i]) for e in E] for i in range(n)]
    W = [[jnp.where(mask_a, sc[i][e * C:(e + 1) * C], eye_hi) for e in E] for i in range(n)]
    for k in range(int(math.log2(C))):
        Wb = [[W[i][e].astype(BF16) for e in E] for i in range(n)]
        AW = [[jnp.dot(Wb[i][e][:, :C], Wb[i][e], preferred_element_type=F32) for e in E] for i in range(n)]
        W = [[jnp.where(lo, 0.0, W[i][e]) + AW[i][e] for e in E] for i in range(n)]
    X = [[_dot(W[i][0], jnp.concatenate([zeros_cp, jnp.where(lo, win(at_f, c, q), zf[i][0])], axis=0)),
          _dot(W[i][1], jnp.concatenate([zeros_cp, jnp.where(lo, zf[i][1], win(at_f, c, q))], axis=0))]
         for i, (c, q) in enumerate(PR)]

    S = [s_scr[q] for q in range(N_PAIRS)]
    ys = []
    for c in range(NC):
        w_last = jnp.exp2(cum[(c + 1) * C - 1:(c + 1) * C, :])
        idx = [c * N_PAIRS + q for q in range(N_PAIRS)]
        mkg = [[_dot_tn(X[i][e], win(bh_b, c, q)) for e in E] for q, i in enumerate(idx)]
        vk = [_dot_tn(win(v_b, c, q), win(kh_b, c, q)) for q, i in enumerate(idx)]
        ry = [[_dot(m_r[i][0], jnp.concatenate([X[i][0], jnp.where(lo, 0.0, vr[i])], axis=0)),
               _dot(m_r[i][1], jnp.concatenate([X[i][1], jnp.where(lo, vr[i], 0.0)], axis=0))] for i in idx]
        rp = [win(rt_f, c, q) + jnp.where(lo, ry[q][0], ry[q][1]) for q in range(N_PAIRS)]
        y0 = [pltpu.roll(jnp.where(lo, ry[q][1], ry[q][0]), D, axis=1) for q in range(N_PAIRS)]
        mk = [jnp.where(diag, jnp.concatenate([mkg[q][0][:D], mkg[q][1][D:]], axis=0), 0.0) for q in range(N_PAIRS)]
        g = [jnp.where(diag, vk[q] + jnp.concatenate([mkg[q][0][D:], mkg[q][1][:D]], axis=0), 0.0)
             for q in range(N_PAIRS)]
        y = [_dot_nt(rp[q], S[q]) for q in range(N_PAIRS)]
        dS = [_dot(S[q], mk[q]) for q in range(N_PAIRS)]
        S = [S[q] * w_last[:, q * PAIR:(q + 1) * PAIR] + dS[q] + g[q] for q in range(N_PAIRS)]
        ys.append(jnp.concatenate([y[q] + y0[q] for q in range(N_PAIRS)], axis=1))
    for q in range(N_PAIRS):
        s_scr[q] = S[q]

    bonus = _seg_sum(r * k2 * rk_ref[...], blk) * v
    y_ref[0] = _group_norm_out(jnp.concatenate(ys, axis=0), bonus, gate, lnw_ref[...], lnb_ref[...], blk)

    @pl.when(t == pl.num_programs(1) - 1)
    def _():
        for q in range(N_PAIRS):
            sout_ref[0, 2 * q] = S[q][:D, :D]
            sout_ref[0, 2 * q + 1] = S[q][D:, D:]


def _rwkv_prompt(proj, mu, w0, a0, k_k, k_a, r_k, lnx_w, lnx_b, wupw, wupa, wupg):
    B, T, _ = proj.shape
    vec = lambda n: _full((1, n))
    return pl.pallas_call(
        _rwkv_prompt_kernel,
        grid=(B, T // RWKV_TILE),
        in_specs=[pl.BlockSpec((1, RWKV_TILE, RWKV_PROJ), lambda b, t: (b, t, 0)),
                  vec(RWKV_PROJ), vec(RWKV_WIDTH), vec(RWKV_WIDTH), vec(RWKV_WIDTH), vec(RWKV_WIDTH),
                  vec(RWKV_WIDTH), vec(RWKV_WIDTH), vec(RWKV_WIDTH),
                  _full((DECAY_LORA, RWKV_WIDTH)), _full((AAA_LORA, RWKV_WIDTH)), _full((GATE_LORA, RWKV_WIDTH))],
        out_specs=[pl.BlockSpec((1, RWKV_TILE, RWKV_WIDTH), lambda b, t: (b, t, 0)),
                   pl.BlockSpec((1, N_RWKV_HEADS, HEAD_DIM, HEAD_DIM), lambda b, t: (b, 0, 0, 0))],
        out_shape=[jax.ShapeDtypeStruct((B, T, RWKV_WIDTH), F32),
                   jax.ShapeDtypeStruct((B, N_RWKV_HEADS, HEAD_DIM, HEAD_DIM), F32)],
        scratch_shapes=[pltpu.VMEM((N_PAIRS, PAIR, PAIR), F32),
                        pltpu.VMEM((1, RWKV_PROJ), F32)],
        compiler_params=_params("arbitrary", "arbitrary"),
        name="rwkv_prompt",
    )(proj, mu, w0, a0, k_k, k_a, r_k, lnx_w, lnx_b, wupw, wupa, wupg)


def _rwkv_step_feat_kernel(p_ref, sh_ref, mu_ref, w0_ref, a0_ref, kk_ref, ka_ref, rk_ref,
                           wupw_ref, wupa_ref, wupg_ref, vecs_ref, bonus_ref, gate_ref):
    p = p_ref[...]
    xs = p + (sh_ref[...] - p) * mu_ref[...]
    r, k2, v, kk, a_sig, logw, gate = _rwkv_features(
        xs, w0_ref[...], a0_ref[...], kk_ref[...], ka_ref[...], wupw_ref[...], wupa_ref[...], wupg_ref[...])
    blk = _head_blocks(RWKV_WIDTH // 2)
    kkn = kk / jnp.maximum(jnp.sqrt(_seg_sum(kk * kk, blk)), 1e-12)
    for i, x in enumerate((-kkn, kkn * a_sig, jnp.exp(logw), k2, r, v)):
        vecs_ref[i] = x.T
    bonus_ref[...] = _seg_sum(r * k2 * rk_ref[...], blk) * v
    gate_ref[...] = gate


def _rwkv_step_state_kernel(s_ref, vecs_ref, sout_ref, y_ref):
    S = s_ref[0]
    a, b, w, k, r, v = (vecs_ref[i] for i in range(6))
    sa = jnp.sum(S * a[None], axis=1)
    S = S * w[None] + sa[:, None, :] * b[None] + v[:, None, :] * k[None]
    sout_ref[0] = S
    y_ref[...] = jnp.sum(S * r[None], axis=1)


def _rwkv_step_out_kernel(yt_ref, bonus_ref, gate_ref, lnw_ref, lnb_ref, o_ref):
    o_ref[...] = _group_norm_out(yt_ref[...].T, bonus_ref[...], gate_ref[...], lnw_ref[...], lnb_ref[...],
                                 _head_blocks(RWKV_WIDTH // 2))


def _rwkv_step(proj, shift, state_t, mu, w0, a0, k_k, k_a, r_k, lnx_w, lnx_b, wupw, wupa, wupg):
    n = proj.shape[0]
    vec = lambda m: _full((1, m))
    rows = _full((n, RWKV_PROJ))
    wide = _full((n, RWKV_WIDTH))
    vecs, bonus, gate = pl.pallas_call(
        _rwkv_step_feat_kernel,
        grid=(1,),
        in_specs=[rows, rows, vec(RWKV_PROJ), vec(RWKV_WIDTH), vec(RWKV_WIDTH), vec(RWKV_WIDTH), vec(RWKV_WIDTH),
                  vec(RWKV_WIDTH),
                  _full((DECAY_LORA, RWKV_WIDTH)), _full((AAA_LORA, RWKV_WIDTH)), _full((GATE_LORA, RWKV_WIDTH))],
        out_specs=[_full((6, RWKV_WIDTH, n)), wide, wide],
        out_shape=[jax.ShapeDtypeStruct((6, RWKV_WIDTH, n), F32),
                   jax.ShapeDtypeStruct((n, RWKV_WIDTH), F32), jax.ShapeDtypeStruct((n, RWKV_WIDTH), F32)],
        compiler_params=_params("arbitrary"),
        name="rwkv_step_feat",
    )(proj, shift, mu, w0, a0, k_k, k_a, r_k, wupw, wupa, wupg)
    st_spec = pl.BlockSpec((1, HEAD_DIM, HEAD_DIM, n), lambda h: (h, 0, 0, 0))
    state_new, yt = pl.pallas_call(
        _rwkv_step_state_kernel,
        grid=(N_RWKV_HEADS,),
        in_specs=[st_spec, pl.BlockSpec((6, HEAD_DIM, n), lambda h: (0, h, 0))],
        out_specs=[st_spec, pl.BlockSpec((HEAD_DIM, n), lambda h: (h, 0))],
        out_shape=[jax.ShapeDtypeStruct(state_t.shape, F32), jax.ShapeDtypeStruct((RWKV_WIDTH, n), F32)],
        compiler_params=_params("arbitrary"),
        name="rwkv_step_state",
    )(state_t, vecs)
    y = pl.pallas_call(
        _rwkv_step_out_kernel,
        grid=(1,),
        in_specs=[_full((RWKV_WIDTH, n)), wide, wide, vec(RWKV_WIDTH), vec(RWKV_WIDTH)],
        out_specs=wide,
        out_shape=jax.ShapeDtypeStruct((n, RWKV_WIDTH), F32),
        compiler_params=_params("arbitrary"),
        name="rwkv_step_out",
    )(yt, bonus, gate, lnx_w, lnx_b)
    return y, state_new


def _t5_bucket_np(dist):
    max_exact = N_BUCKETS // 2
    d = np.maximum(dist, 1).astype(np.float32)
    large = max_exact + (np.log(d / np.float32(max_exact)) / np.float32(math.log(MAX_DISTANCE / max_exact))
                         * np.float32(N_BUCKETS - max_exact)).astype(np.int32)
    large = np.minimum(large, N_BUCKETS - 1)
    return np.where(dist < max_exact, dist, large).astype(np.int32)


def _prompt_bucket_table():
    qi = np.arange(BLOCK)[:, None]
    kj = np.arange(2 * BLOCK)[None, :]
    dist = BLOCK + qi - kj
    valid = (dist >= 0) & (dist <= WINDOW)
    return np.where(valid, _t5_bucket_np(np.maximum(dist, 0)), -1).astype(np.int32)


def _decode_bucket_table():
    dist = WINDOW - np.arange(WINDOW)
    return np.broadcast_to(_t5_bucket_np(dist)[None, :], (8, WINDOW)).astype(np.int32).copy()


def _bias_from_buckets(bkt, relb_ref, h, init):
    acc = jnp.full(bkt.shape, init, F32)
    for b in range(N_BUCKETS):
        acc = jnp.where(bkt == b, relb_ref[b, h], acc)
    return acc


ATT_TILE = 4 * BLOCK


def _attn_init(bkt_ref, relb_ref, mk_ref, mv_ref, bias_scr, mk_scr, mv_scr):
    grp = N_SWA_HEADS // N_SWA_KV_HEADS
    ones = jnp.ones((N_MEM, HEAD_DIM), F32)
    bkt = bkt_ref[...]
    for j in range(N_SWA_KV_HEADS):
        for g in range(grp):
            bias_scr[j, g * BLOCK:(g + 1) * BLOCK, :] = _bias_from_buckets(bkt, relb_ref, j * grp + g, NEG) * LOG2E
    mk = mk_ref[0]
    mv = mv_ref[0]
    for h in range(N_MEM_HEADS):
        sl = slice(h * HEAD_DIM, (h + 1) * HEAD_DIM)
        mk_scr[h] = mk[:, sl].astype(BF16)
        mv_scr[h] = jnp.concatenate([mv[:, sl], ones], axis=1).astype(BF16)


def _attn_tile(first, qs, qm, kc, kp, vc, vp, sqg, skg, mqg, sink_ref, bias_scr, mk_scr, mv_scr):
    grp = N_SWA_HEADS // N_SWA_KV_HEADS
    ones = jnp.ones((2 * BLOCK, HEAD_DIM), F32)
    rowi = lax.broadcasted_iota(jnp.int32, (2 * BLOCK, 1), 0)
    col = lax.broadcasted_iota(jnp.int32, (2 * BLOCK, 2 * BLOCK), 1)
    pad_mask = jnp.logical_and(first, col < BLOCK)
    hsl = [slice(h * HEAD_DIM, (h + 1) * HEAD_DIM) for h in range(N_SWA_HEADS)]
    qs_n = _rms_heads(qs, sqg, N_SWA_HEADS) * (ATTN_SCALE * LOG2E)
    qm_n = _rms_heads(qm, mqg, N_MEM_HEADS) * (ATTN_SCALE * LOG2E)
    kc_n = _rms_heads(kc, skg, N_SWA_KV_HEADS)
    kp_n = _rms_heads(kp, skg, N_SWA_KV_HEADS)
    chains =[(a, j) for a in range(ATT_TILE // BLOCK) for j in range(N_SWA_KV_HEADS)]
    lhs, keys, vals, sinkcol = [], [], [], []
    for a, j in chains:
        rs = slice(a * BLOCK, (a + 1) * BLOCK)
        lhs.append(jnp.concatenate([qs_n[rs, hsl[j * grp + g]] for g in range(grp)], axis=0))
        if a == 0:
            keys.append(jnp.concatenate([kp_n[:, hsl[j]], kc_n[:BLOCK, hsl[j]]], axis=0))
            vv = jnp.concatenate([vp[:, hsl[j]], vc[:BLOCK, hsl[j]]], axis=0)
        else:
            ks = slice((a - 1) * BLOCK, (a + 1) * BLOCK)
            keys.append(kc_n[ks, hsl[j]])
            vv = vc[ks, hsl[j]]
        vals.append(jnp.concatenate([vv, ones], axis=1))
        sinkcol.append(jnp.where(rowi < BLOCK, sink_ref[0, j * grp], sink_ref[0, j * grp + 1]) * LOG2E)
    qmn = [qm_n[:, hsl[h]] for h in range(N_MEM_HEADS)]

    bias = [bias_scr[j] for j in range(N_SWA_KV_HEADS)]
    mem_k = [mk_scr[h] for h in range(N_MEM_HEADS)]
    mem_v = [mv_scr[h] for h in range(N_MEM_HEADS)]
    n_w = len(chains)

    def scores(t):
        return _dot_nt(lhs[t], keys[t]) if t < n_w else _dot_nt(qmn[t - n_w], mem_k[t - n_w])

    def finish(ts, s):
        e, extra = {}, {}
        for t in ts:
            if t < n_w:
                a, j = chains[t]
                st = s[t] + bias[j]
                if a == 0:
                    st = jnp.where(pad_mask, NEG, st)
                m = jnp.maximum(jnp.max(st, -1, keepdims=True), sinkcol[t])
                e[t], extra[t] = jnp.exp2(st - m), jnp.exp2(sinkcol[t] - m)
            else:
                e[t], extra[t] = jnp.exp2(s[t] - jnp.max(s[t], -1, keepdims=True)), 0.0
        o_full = {t: _dot(e[t], vals[t] if t < n_w else mem_v[t - n_w]) for t in ts}
        return {t: o_full[t][:, :HEAD_DIM] / (o_full[t][:, HEAD_DIM:HEAD_DIM + 1] + extra[t]) for t in ts}

    def place(t):
        if t < n_w:
            a, j = chains[t]
            return [(slice(a * BLOCK, (a + 1) * BLOCK), hsl[j * grp + g], slice(g * BLOCK, (g + 1) * BLOCK))
                    for g in range(grp)]
        h = t - n_w
        return [(slice(0, ATT_TILE), slice(SWA_WIDTH + h * HEAD_DIM, SWA_WIDTH + (h + 1) * HEAD_DIM), slice(0, ATT_TILE))]

    return kc_n, n_w, n_w + N_MEM_HEADS, scores, finish, place


FF_CHUNK = 1024


def _ffn_chunk(h2, w1_ref, w2_ref, c):
    u = jnp.dot(h2, w1_ref[:, c * FF_CHUNK:(c + 1) * FF_CHUNK], preferred_element_type=F32)
    u = jnp.square(jnp.maximum(u, 0.0)).astype(BF16)
    return jnp.dot(u, w2_ref[c * FF_CHUNK:(c + 1) * FF_CHUNK, :], preferred_element_type=F32)


def _mix_rows(x, yr, ys, ym, wo_ref, g2_ref):
    x1 = (x
          + jnp.dot(yr.astype(BF16), wo_ref[0:RWKV_WIDTH, :], preferred_element_type=F32)
          + jnp.dot(ys.astype(BF16), wo_ref[RWKV_WIDTH:RWKV_WIDTH + SWA_WIDTH, :], preferred_element_type=F32)
          + jnp.dot(ym.astype(BF16), wo_ref[RWKV_WIDTH + SWA_WIDTH:, :], preferred_element_type=F32))
    return x1, _rms(x1, g2_ref[...]).astype(BF16)


def _attn_ffn_kernel(tiles_per_seq, qs_ref, qm_ref, kc_ref, kp_ref, vc_ref, vp_ref, mk_ref, mv_ref, bkt_ref, relb_ref,
                     sink_ref, sqg_ref, skg_ref, mqg_ref, x_ref, yr_ref, xs_ref, yrs_ref, yss_ref, yms_ref,
                     wo_ref, g2_ref, w1_ref, w2_ref, o_ref, os_ref, kn_ref, bias_scr, mk_scr, mv_scr, stage_scr):
    s = pl.program_id(0)
    nt = pl.num_programs(0) - 1
    tile = jnp.minimum(s, nt - 1)
    first = lax.rem(tile, tiles_per_seq) == 0
    wslot = lax.rem(s, 2)
    rslot = 1 - wslot

    @pl.when(s == 0)
    def _():
        stage_scr[...] = jnp.zeros_like(stage_scr)

    @pl.when(jnp.logical_and(first, s < nt))
    def _():
        _attn_init(bkt_ref, relb_ref, mk_ref, mv_ref, bias_scr, mk_scr, mv_scr)

    kc_n, n_w, n_tasks, scores, finish, place = _attn_tile(
        first, qs_ref[0], qm_ref[0], kc_ref[0], kp_ref[0], vc_ref[0], vp_ref[0],
        sqg_ref[...], skg_ref[...], mqg_ref[...], sink_ref, bias_scr, mk_scr, mv_scr)
    kn_ref[0] = kc_n[ATT_TILE - BLOCK:]

    staged = stage_scr[rslot]
    x1, h2 = _mix_rows(x_ref[...], yr_ref[...], staged[:, :SWA_WIDTH], staged[:, SWA_WIDTH:], wo_ref, g2_ref)
    sc = {t: scores(t) for t in range(n_tasks)}
    ff = _ffn_chunk(h2, w1_ref, w2_ref, 0)
    outs = finish(range(n_w), sc)
    ff = ff + _ffn_chunk(h2, w1_ref, w2_ref, 1)
    outs.update(finish(range(n_w, n_tasks), sc))
    for c in range(2, D_FF // FF_CHUNK):
        ff = ff + _ffn_chunk(h2, w1_ref, w2_ref, c)
    o_ref[...] = x1 + ff
    for t in range(n_tasks):
        for rows, lanes, src in place(t):
            stage_scr[wslot, rows, lanes] = outs[t][src]

    @pl.when(s == nt)
    def _():
        xs1, hs2 = _mix_rows(xs_ref[...], yrs_ref[...], yss_ref[...], yms_ref[...], wo_ref, g2_ref)
        ffs = None
        for c in range(D_FF // FF_CHUNK):
            d = _ffn_chunk(hs2, w1_ref, w2_ref, c)
            ffs = d if ffs is None else ffs + d
        os_ref[...] = xs1 + ffs


def _attn_ffn(proj, mk, mv, rel_bias, sinks, sqg, skg, mqg, x2d, yr, sample, wo, g2, w1, w2):
    B, T, _ = proj.shape
    tps = T // ATT_TILE
    nt = B * tps
    ns = sample[0].shape[0]
    bkt = jnp.asarray(_prompt_bucket_table())
    smem = pl.BlockSpec(memory_space=pltpu.SMEM)
    kblk, vblk = COL_SK // SWA_KV_WIDTH, COL_SV // SWA_KV_WIDTH
    att = lambda s: jnp.minimum(s, nt - 1)
    cur = lambda s, c: (att(s) // tps, att(s) % tps, c)
    prev = lambda s, c: (att(s) // tps, jnp.maximum((ATT_TILE // BLOCK) * (att(s) % tps) - 1, 0), c)
    ffn = lambda s: (jnp.maximum(s - 1, 0), 0)
    memb = pl.BlockSpec((1, N_MEM, MEM_WIDTH), lambda s: (att(s) // tps, 0, 0))
    const = lambda shape: pl.BlockSpec(shape, lambda s: (0, 0), pipeline_mode=pl.Buffered(1))
    widths = (D_MODEL, RWKV_WIDTH, SWA_WIDTH, MEM_WIDTH)
    return pl.pallas_call(
        functools.partial(_attn_ffn_kernel, tps),
        grid=(nt + 1,),
        in_specs=[pl.BlockSpec((1, ATT_TILE, SWA_WIDTH), lambda s: cur(s, COL_SQ // SWA_WIDTH)),
                  pl.BlockSpec((1, ATT_TILE, MEM_WIDTH), lambda s: cur(s, COL_MQ // MEM_WIDTH)),
                  pl.BlockSpec((1, ATT_TILE, SWA_KV_WIDTH), lambda s: cur(s, kblk)),
                  pl.BlockSpec((1, BLOCK, SWA_KV_WIDTH), lambda s: prev(s, kblk)),
                  pl.BlockSpec((1, ATT_TILE, SWA_KV_WIDTH), lambda s: cur(s, vblk)),
                  pl.BlockSpec((1, BLOCK, SWA_KV_WIDTH), lambda s: prev(s, vblk)),
                  memb, memb,
                  _full((BLOCK, 2 * BLOCK)), smem, smem,
                  _full((1, HEAD_DIM)), _full((1, HEAD_DIM)), _full((1, HEAD_DIM)),
                  pl.BlockSpec((ATT_TILE, D_MODEL), ffn), pl.BlockSpec((ATT_TILE, RWKV_WIDTH), ffn)]
                 + [_full((ns, w)) for w in widths]
                 + [const((D_MODEL, D_MODEL)), _full((1, D_MODEL)), const((D_MODEL, D_FF)), const((D_FF, D_MODEL))],
        out_specs=[pl.BlockSpec((ATT_TILE, D_MODEL), ffn), _full((ns, D_MODEL)),
                   pl.BlockSpec((1, BLOCK, SWA_KV_WIDTH), lambda s: (att(s) // tps, 0, 0))],
        out_shape=[jax.ShapeDtypeStruct((B * T, D_MODEL), F32), jax.ShapeDtypeStruct((ns, D_MODEL), F32),
                   jax.ShapeDtypeStruct((B, BLOCK, SWA_KV_WIDTH), F32)],
        scratch_shapes=[pltpu.VMEM((N_SWA_KV_HEADS, 2 * BLOCK, 2 * BLOCK), F32),
                        pltpu.VMEM((N_MEM_HEADS, N_MEM, HEAD_DIM), BF16),
                        pltpu.VMEM((N_MEM_HEADS, N_MEM, 2 * HEAD_DIM), BF16),
                        pltpu.VMEM((2, ATT_TILE, SWA_WIDTH + MEM_WIDTH), F32)],
        compiler_params=_params("arbitrary"),
        name="attn_ffn",
    )(proj, proj, proj, proj, proj, proj, mk, mv, bkt, rel_bias, sinks, sqg, skg, mqg, x2d, yr, *sample, wo, g2, w1, w2)


def _memory_kv_kernel(mem_ref, g_ref, w_ref, kg_ref, mk_ref, mv_ref):
    kv = _dot(_rms(mem_ref[0], g_ref[...]), w_ref[...])
    kg = kg_ref[...]
    for h in range(N_MEM_HEADS):
        sl = slice(h * HEAD_DIM, (h + 1) * HEAD_DIM)
        mk_ref[0, :, sl] = _rms(kv[:, sl], kg)
    mv_ref[0] = kv[:, MEM_WIDTH:]


def _memory_kv(mem, g, w, kg):
    B = mem.shape[0]
    blk = pl.BlockSpec((1, N_MEM, MEM_WIDTH), lambda b: (b, 0, 0))
    return pl.pallas_call(
        _memory_kv_kernel,
        grid=(B,),
        in_specs=[pl.BlockSpec((1, N_MEM, D_MODEL), lambda b: (b, 0, 0)),
                  _full((1, D_MODEL)), _full((D_MODEL, 2 * MEM_WIDTH)), _full((1, HEAD_DIM))],
        out_specs=[blk, blk],
        out_shape=[jax.ShapeDtypeStruct((B, N_MEM, MEM_WIDTH), F32)] * 2,
        compiler_params=_params("arbitrary"),
        name="memory_kv",
    )(mem, g, w, kg)


SEQ_TILE = 8


def _rms_heads(x, g, n_heads):
    ms = jnp.dot((x * x).astype(BF16), _head_blocks(n_heads * HEAD_DIM), preferred_element_type=F32)
    return x * lax.rsqrt(ms * (1.0 / HEAD_DIM) + NORM_EPS) * jnp.concatenate([g] * n_heads, axis=1)


def _decode_attn_kernel(p_ref, kbuf_ref, vbuf_ref, mk_ref, mv_ref, bkt_ref, relb_ref, sink_ref,
                        sqg_ref, skg_ref, mqg_ref, ys_ref, ym_ref, kout_ref, vout_ref, tab_scr):
    grp = N_SWA_HEADS // N_SWA_KV_HEADS

    @pl.when(pl.program_id(0) == 0)
    def _():
        hrow = lax.broadcasted_iota(jnp.int32, (8, WINDOW), 0)
        lane = lax.broadcasted_iota(jnp.int32, (8, WINDOW), 1)
        bias_w = jnp.zeros((8, WINDOW), F32)
        cols = jnp.zeros((8, WINDOW), F32)
        for h in range(N_SWA_HEADS):
            bias_w = jnp.where(hrow == h, _bias_from_buckets(bkt_ref[...], relb_ref, h, 0.0), bias_w)
            cols = jnp.where(jnp.logical_and(hrow == h, lane == 0), relb_ref[0, h], cols)
            cols = jnp.where(jnp.logical_and(hrow == h, lane == 1), sink_ref[0, h], cols)
        tab_scr[0] = bias_w
        tab_scr[1] = cols

    bias_w = tab_scr[0][:N_SWA_HEADS]
    bias_new = tab_scr[1][:N_SWA_HEADS, 0:1]
    sink = tab_scr[1][:N_SWA_HEADS, 1:2]
    rowi = lax.broadcasted_iota(jnp.int32, (SWA_KV_WIDTH, WINDOW), 0)
    lanei = lax.broadcasted_iota(jnp.int32, (SWA_KV_WIDTH, WINDOW), 1)
    eye = rowi == lanei
    NH = N_SWA_HEADS
    own = (lax.broadcasted_iota(jnp.int32, (NH, NH * HEAD_DIM), 1) // HEAD_DIM
           == lax.broadcasted_iota(jnp.int32, (NH, NH * HEAD_DIM), 0))

    p = p_ref[...]
    qn = _rms_heads(p[:, COL_SQ:COL_SQ + SWA_WIDTH], sqg_ref[...], N_SWA_HEADS) * ATTN_SCALE
    kn = _rms_heads(p[:, COL_SK:COL_SK + SWA_KV_WIDTH], skg_ref[...], N_SWA_KV_HEADS)
    vn = p[:, COL_SV:COL_SV + SWA_KV_WIDTH]
    qmn = _rms_heads(p[:, COL_MQ:COL_MQ + MEM_WIDTH], mqg_ref[...], N_MEM_HEADS) * ATTN_SCALE
    rep = lambda x: jnp.concatenate([x[:, j * HEAD_DIM:(j + 1) * HEAD_DIM] for j in range(N_SWA_KV_HEADS)
                                     for _ in range(grp)], axis=1)
    kn_rep, vn_rep = rep(kn), rep(vn)

    B = range(SEQ_TILE)
    dup = lambda c: jnp.concatenate([c[j * HEAD_DIM:(j + 1) * HEAD_DIM] for j in range(N_SWA_KV_HEADS)
                                     for _ in range(grp)], axis=0)
    qd = [jnp.where(own, qn[b:b + 1, :], 0.0) for b in B]
    qmd = [jnp.where(own, qmn[b:b + 1, :], 0.0) for b in B]
    kdup = [dup(kbuf_ref[b]) for b in B]
    vdup = [dup(vbuf_ref[b]) for b in B]
    s = [_dot(qd[b], kdup[b]) + bias_w for b in B]
    sm = [_dot(qmd[b], mk_ref[b]) for b in B]
    s_new = [jnp.sum(qd[b] * kn_rep[b:b + 1, :], -1, keepdims=True) + bias_new for b in B]
    m = [jnp.maximum(jnp.maximum(jnp.max(s[b], -1, keepdims=True), s_new[b]), sink) for b in B]
    e = [jnp.exp(s[b] - m[b]) for b in B]
    e_new = [jnp.exp(s_new[b] - m[b]) for b in B]
    den = [jnp.sum(e[b], -1, keepdims=True) + e_new[b] + jnp.exp(sink - m[b]) for b in B]
    em = [jnp.exp(sm[b] - jnp.max(sm[b], -1, keepdims=True)) for b in B]
    ov = [_dot_nt(e[b], vdup[b]) for b in B]
    omf = [_dot_nt(em[b], mv_ref[b]) for b in B]
    ys_rows = [jnp.sum(jnp.where(own, (ov[b] + e_new[b] * vn_rep[b:b + 1, :]) / den[b], 0.0), 0, keepdims=True)
               for b in B]
    ym_rows = [jnp.sum(jnp.where(own, omf[b] / jnp.sum(em[b], -1, keepdims=True), 0.0), 0, keepdims=True)
               for b in B]
    ys_ref[...] = jnp.concatenate(ys_rows, axis=0)
    ym_ref[...] = jnp.concatenate(ym_rows, axis=0)
    for b in B:
        kn_col = jnp.sum(jnp.where(eye, kn[b:b + 1, :], 0.0), -1, keepdims=True)
        vn_col = jnp.sum(jnp.where(eye, vn[b:b + 1, :], 0.0), -1, keepdims=True)
        kout_ref[b] = jnp.where(lanei == WINDOW - 1, kn_col, pltpu.roll(kbuf_ref[b], WINDOW - 1, axis=1))
        vout_ref[b] = jnp.where(lanei == WINDOW - 1, vn_col, pltpu.roll(vbuf_ref[b], WINDOW - 1, axis=1))


def _decode_attn(proj, kbuf, vbuf, mk, mv, rel_bias, sinks, sqg, skg, mqg):
    n = proj.shape[0]
    bkt = jnp.asarray(_decode_bucket_table())
    smem = pl.BlockSpec(memory_space=pltpu.SMEM)
    win = pl.BlockSpec((SEQ_TILE, WINDOW, SWA_KV_WIDTH), lambda i: (i, 0, 0))
    memb = pl.BlockSpec((SEQ_TILE, N_MEM, MEM_WIDTH), lambda i: (i, 0, 0))
    return pl.pallas_call(
        _decode_attn_kernel,
        grid=(n // SEQ_TILE,),
        in_specs=[pl.BlockSpec((SEQ_TILE, IN_PROJ), lambda i: (i, 0)), win, win, memb, memb,
                  _full((8, WINDOW)), smem, smem, _full((1, HEAD_DIM)), _full((1, HEAD_DIM)), _full((1, HEAD_DIM))],
        out_specs=[pl.BlockSpec((SEQ_TILE, SWA_WIDTH), lambda i: (i, 0)),
                   pl.BlockSpec((SEQ_TILE, MEM_WIDTH), lambda i: (i, 0)), win, win],
        out_shape=[jax.ShapeDtypeStruct((n, SWA_WIDTH), F32), jax.ShapeDtypeStruct((n, MEM_WIDTH), F32),
                   jax.ShapeDtypeStruct(kbuf.shape, F32), jax.ShapeDtypeStruct(vbuf.shape, F32)],
        scratch_shapes=[pltpu.VMEM((2, 8, WINDOW), F32)],
        compiler_params=_params("arbitrary"),
        name="decode_attn",
    )(proj, kbuf, vbuf, mk, mv, bkt, rel_bias, sinks, sqg, skg, mqg)


def kernel(x_prompt, x_sample, state_rwkv, state_shift, cache_swa_k, cache_swa_v, cache_mem_k, cache_mem_v,
           mem_prompt, rel_bias, norm1_g, w_in, mu_shift, w0, w_up_w, a0, w_up_a, w_up_g, k_k, k_a, r_k,
           lnx_w, lnx_b, q_norm_swa, k_norm_swa, sinks, mem_norm_g, w_mem_kv, q_norm_mem, k_norm_mem,
           w_out, norm2_g, w_ff1, w_ff2):
    B, T, _ = x_prompt.shape
    Bd = x_sample.shape[0]
    l = 0
    w_in_b = w_in[l].astype(BF16)
    w_out_b = w_out[l].astype(BF16)
    w1_b = w_ff1[l].astype(BF16)
    w2_b = w_ff2[l].astype(BF16)
    rwkv_params = (mu_shift[l][None], w0[l][None], a0[l][None], k_k[l][None], k_a[l][None],
                   r_k[l].reshape(1, RWKV_WIDTH), lnx_w[l][None], lnx_b[l][None],
                   w_up_w[l], w_up_a[l], w_up_g[l])
    sqg, skg, mqg, mkg = q_norm_swa[l][None], k_norm_swa[l][None], q_norm_mem[l][None], k_norm_mem[l][None]
    g1, g2 = norm1_g[l][None], norm2_g[l][None]

    xp = x_prompt.reshape(B * T, D_MODEL)
    xs = x_sample.reshape(Bd, D_MODEL)
    proj_p, proj_s = _in_proj(xp, xs, g1, w_in_b, IN_PROJ_TILE)
    proj_p = proj_p.reshape(B, T, IN_PROJ)

    mk, mv = _memory_kv(mem_prompt, mem_norm_g[l][None], w_mem_kv[l], mkg)
    yr_p, s_p = _rwkv_prompt(proj_p, *rwkv_params)
    shift_p = proj_p[:, T - 1, :RWKV_PROJ]
    vb_p = proj_p[:, T - WINDOW:, COL_SV:COL_SV + SWA_KV_WIDTH]

    yr_s, st_s = _rwkv_step(proj_s[:, :RWKV_PROJ], state_shift[l], jnp.transpose(state_rwkv[l], (1, 2, 3, 0)),
                            *rwkv_params)
    s_s = jnp.transpose(st_s, (3, 0, 1, 2))
    fmajor = lambda c: jnp.transpose(c, (0, 2, 3, 1)).reshape(Bd, c.shape[2] * HEAD_DIM, c.shape[1])
    ys_s, ym_s, kb_s, vb_s = _decode_attn(
        proj_s, fmajor(cache_swa_k[l]), fmajor(cache_swa_v[l]), fmajor(cache_mem_k[l]), fmajor(cache_mem_v[l]),
        rel_bias, sinks[l][None], sqg, skg, mqg)
    pmajor = lambda c: jnp.transpose(c.reshape(Bd, N_SWA_KV_HEADS, HEAD_DIM, WINDOW), (0, 3, 1, 2))[None]

    y_p, y_s, kn_p = _attn_ffn(proj_p, mk, mv, rel_bias, sinks[l][None], sqg, skg, mqg,
                               xp, yr_p.reshape(B * T, RWKV_WIDTH), (xs, yr_s, ys_s, ym_s), w_out_b, g2, w1_b, w2_b)
    y_p = y_p.reshape(B, T, D_MODEL)
    y_s = y_s.reshape(Bd, 1, D_MODEL)

    return (y_p, y_s,
            s_p[None], shift_p[None],
            kn_p.reshape(1, B, WINDOW, N_SWA_KV_HEADS, HEAD_DIM),
            vb_p.reshape(1, B, WINDOW, N_SWA_KV_HEADS, HEAD_DIM),
            mk.reshape(1, B, N_MEM, N_MEM_HEADS, HEAD_DIM), mv.reshape(1, B, N_MEM, N_MEM_HEADS, HEAD_DIM),
            s_s[None], proj_s[:, :RWKV_PROJ][None],
            pmajor(kb_s), pmajor(vb_s))
```

```python
import functools
import math

import numpy as np
import jax
import jax.numpy as jnp
from jax import lax
from jax.experimental import pallas as pl
from jax.experimental.pallas import tpu as pltpu

F32 = jnp.float32
BF16 = jnp.bfloat16

D_MODEL = 1024
HEAD_DIM = 64
RWKV_WIDTH = 512
N_RWKV_HEADS = 8
SWA_WIDTH = 256
N_SWA_HEADS = 4
N_SWA_KV_HEADS = 2
SWA_KV_WIDTH = 128
MEM_WIDTH = 256
N_MEM_HEADS = 4
N_MEM = 256
WINDOW = 128
BLOCK = 128
N_BUCKETS = 32
MAX_DISTANCE = 128
DECAY_LORA = 64
AAA_LORA = 64
GATE_LORA = 128
RWKV_PROJ = 3 * RWKV_WIDTH + DECAY_LORA + AAA_LORA + GATE_LORA
SWA_PROJ = SWA_WIDTH + 2 * SWA_KV_WIDTH
IN_PROJ = RWKV_PROJ + SWA_PROJ + MEM_WIDTH
D_FF = 4 * D_MODEL
NORM_EPS = 1e-6
LNX_EPS = 64e-5
ATTN_SCALE = HEAD_DIM ** -0.5
EXP_M05 = math.exp(-0.5)
LOG2E = math.log2(math.e)
NEG = -1e30

COL_R, COL_K, COL_V = 0, RWKV_WIDTH, 2 * RWKV_WIDTH
COL_WD = 3 * RWKV_WIDTH
COL_AD = COL_WD + DECAY_LORA
COL_GD = COL_AD + AAA_LORA
COL_SQ = RWKV_PROJ
COL_SK = COL_SQ + SWA_WIDTH
COL_SV = COL_SK + SWA_KV_WIDTH
COL_MQ = RWKV_PROJ + SWA_PROJ

CHUNK = 64
VMEM_LIMIT = 56 * 1024 * 1024


def _dot(a, b):
    return jnp.dot(a.astype(BF16), b.astype(BF16), preferred_element_type=F32)


def _dot_nt(a, b):
    return lax.dot_general(a.astype(BF16), b.astype(BF16), (((1,), (1,)), ((), ())),
                           preferred_element_type=F32)


def _dot_tn(a, b):
    return lax.dot_general(a.astype(BF16), b.astype(BF16), (((0,), (0,)), ((), ())),
                           preferred_element_type=F32)


def _rms(x, g):
    return x * lax.rsqrt(jnp.mean(x * x, -1, keepdims=True) + NORM_EPS) * g


def _params(*sem):
    return pltpu.CompilerParams(dimension_semantics=sem, vmem_limit_bytes=VMEM_LIMIT)


def _full(shape):
    n = len(shape)
    return pl.BlockSpec(shape, lambda *_: (0,) * n)


def _head_blocks(width):
    bi = lax.broadcasted_iota(jnp.int32, (width, width), 0) // HEAD_DIM
    bj = lax.broadcasted_iota(jnp.int32, (width, width), 1) // HEAD_DIM
    return jnp.where(bi == bj, 1.0, 0.0).astype(BF16)


IN_PROJ_TILE = 1024
IN_PROJ_SUB = 256


def _in_proj_kernel(x_ref, xs_ref, g_ref, w_ref, wo_ref, w1_ref, w2_ref, o_ref, os_ref, wob_ref, w1b_ref, w2b_ref,
                    wb_scr):
    @pl.when(pl.program_id(0) == 0)
    def _():
        wb_scr[...] = w_ref[...].astype(BF16)

    wob_ref[...] = wo_ref[...].astype(BF16)
    w1b_ref[...] = w1_ref[...].astype(BF16)
    w2b_ref[...] = w2_ref[...].astype(BF16)

    tm = x_ref.shape[0]
    for j in range(tm // IN_PROJ_SUB):
        rows = slice(j * IN_PROJ_SUB, (j + 1) * IN_PROJ_SUB)
        h = _rms(x_ref[rows, :], g_ref[...])
        o_ref[rows, :] = jnp.dot(h.astype(BF16), wb_scr[...], preferred_element_type=F32)

    @pl.when(pl.program_id(0) == pl.num_programs(0) - 1)
    def _():
        hs = _rms(xs_ref[...], g_ref[...])
        os_ref[...] = jnp.dot(hs.astype(BF16), wb_scr[...], preferred_element_type=F32)


def _in_proj(x2d, xs2d, g, w, later_weights, tm):
    n, ns = x2d.shape[0], xs2d.shape[0]
    steps = n // tm
    slab = lambda m: pl.BlockSpec((m.shape[0] // steps, m.shape[1]), lambda i: (i, 0))
    proj_p, proj_s, *later_bf16 = pl.pallas_call(
        _in_proj_kernel,
        grid=(steps,),
        in_specs=[pl.BlockSpec((tm, D_MODEL), lambda i: (i, 0)),
                  _full((ns, D_MODEL)),
                  _full((1, D_MODEL)),
                  pl.BlockSpec((D_MODEL, IN_PROJ), lambda i: (0, 0), pipeline_mode=pl.Buffered(1))]
                 + [slab(m) for m in later_weights],
        out_specs=[pl.BlockSpec((tm, IN_PROJ), lambda i: (i, 0)), _full((ns, IN_PROJ))]
                  + [slab(m) for m in later_weights],
        out_shape=[jax.ShapeDtypeStruct((n, IN_PROJ), F32), jax.ShapeDtypeStruct((ns, IN_PROJ), F32)]
                  + [jax.ShapeDtypeStruct(m.shape, BF16) for m in later_weights],
        scratch_shapes=[pltpu.VMEM((D_MODEL, IN_PROJ), BF16)],
        compiler_params=_params("arbitrary"),
        name="in_proj",
    )(x2d, xs2d, g, w, *later_weights)
    return proj_p, proj_s, later_bf16


def _rwkv_features(xs, w0, a0, k_k, k_a, wupw, wupa, wupg):
    r = xs[:, COL_R:COL_R + RWKV_WIDTH]
    k = xs[:, COL_K:COL_K + RWKV_WIDTH]
    v = xs[:, COL_V:COL_V + RWKV_WIDTH]
    wd = xs[:, COL_WD:COL_WD + DECAY_LORA]
    ad = xs[:, COL_AD:COL_AD + AAA_LORA]
    gd = xs[:, COL_GD:COL_GD + GATE_LORA]
    logw = -jax.nn.sigmoid(w0 + _dot(jnp.tanh(wd), wupw)) * EXP_M05
    a_sig = jax.nn.sigmoid(a0 + _dot(ad, wupa))
    gate = _dot(jax.nn.sigmoid(gd), wupg)
    kk = k * k_k
    k2 = k * (1.0 + (a_sig - 1.0) * k_a)
    return r, k2, v, kk, a_sig, logw, gate


def _seg_sum(x, blk):
    xb = x.astype(BF16)
    half = RWKV_WIDTH // 2
    return jnp.concatenate([jnp.dot(xb[:, :half], blk, preferred_element_type=F32),
                            jnp.dot(xb[:, half:], blk, preferred_element_type=F32)], axis=1)


def _group_norm_out(y, bonus, gate, lnx_w, lnx_b, blk):
    inv_d = 1.0 / HEAD_DIM
    m = _seg_sum(y, blk) * inv_d
    d = y - m
    var = _seg_sum(d * d, blk) * inv_d
    yn = d * lax.rsqrt(var + LNX_EPS) * lnx_w + lnx_b
    return (yn + bonus) * gate


RWKV_TILE = 4 * CHUNK
PAIR = 2 * HEAD_DIM
N_PAIRS = N_RWKV_HEADS // 2


def _rwkv_prompt_kernel(p_ref, mu_ref, w0_ref, a0_ref, kk_ref, ka_ref, rk_ref, lnw_ref, lnb_ref,
                        wupw_ref, wupa_ref, wupg_ref, y_ref, sout_ref, s_scr, prev_scr):
    C, TT, D = CHUNK, RWKV_TILE, HEAD_DIM
    NC = TT // C
    t = pl.program_id(1)

    @pl.when(t == 0)
    def _():
        s_scr[...] = jnp.zeros_like(s_scr)
        prev_scr[...] = jnp.zeros_like(prev_scr)

    p = p_ref[0]
    row = lax.broadcasted_iota(jnp.int32, p.shape, 0)
    prev = jnp.where(row == 0, prev_scr[...], pltpu.roll(p, 1, axis=0))
    prev_scr[...] = p[TT - 1:TT, :]
    xs = p + (prev - p) * mu_ref[...]
    r, k2, v, kk, a_sig, logw, gate = _rwkv_features(
        xs, w0_ref[...], a0_ref[...], kk_ref[...], ka_ref[...], wupw_ref[...], wupa_ref[...], wupg_ref[...])

    blk = _head_blocks(RWKV_WIDTH // 2)
    kkn = kk / jnp.maximum(jnp.sqrt(_seg_sum(kk * kk, blk)), 1e-12)
    bb = kkn * a_sig

    ri = lax.broadcasted_iota(jnp.int32, (TT, TT), 0)
    ci = lax.broadcasted_iota(jnp.int32, (TT, TT), 1)
    tri = jnp.where(jnp.logical_and(ri >= ci, ri // C == ci // C), 1.0, 0.0).astype(BF16)
    lw2 = logw * LOG2E
    l1 = lw2.astype(BF16)
    l2 = (lw2 - l1.astype(F32)).astype(BF16)
    cum = jnp.dot(tri, l1, preferred_element_type=F32) + jnp.dot(tri, l2, preferred_element_type=F32)
    c_last = jnp.concatenate([jnp.broadcast_to(cum[(c + 1) * C - 1:(c + 1) * C, :], (C, RWKV_WIDTH))
                              for c in range(NC)], axis=0)
    e_pos = jnp.exp2(cum)
    e_neg = jnp.exp2(-cum)
    e_prev = jnp.exp2(cum - lw2)
    e_last = jnp.exp2(c_last - cum)

    lo_full = (lax.broadcasted_iota(jnp.int32, (TT, RWKV_WIDTH), 1) % PAIR) < D
    at_f = -kkn * e_prev
    rt_f = r * e_pos
    at_lo = jnp.where(lo_full, at_f, 0.0).astype(BF16)
    at_hi = jnp.where(lo_full, 0.0, at_f).astype(BF16)
    rt_lo = jnp.where(lo_full, rt_f, 0.0).astype(BF16)
    rt_hi = jnp.where(lo_full, 0.0, rt_f).astype(BF16)
    bt_b = (bb * e_neg).astype(BF16)
    kt_b = (k2 * e_neg).astype(BF16)
    bh_b = (bb * e_last).astype(BF16)
    kh_b = (k2 * e_last).astype(BF16)
    v_b = v.astype(BF16)

    r2 = lax.broadcasted_iota(jnp.int32, (C, PAIR), 0)
    c2 = lax.broadcasted_iota(jnp.int32, (C, PAIR), 1)
    lo = c2 < D
    c2m = jnp.where(lo, c2, c2 - C)
    mask_a = jnp.logical_and(lo, c2 < r2)
    mask_ak = jnp.logical_and(jnp.logical_not(lo), c2m < r2)
    mask_r = c2m <= r2
    eye_hi = jnp.where(jnp.logical_and(jnp.logical_not(lo), c2m == r2), 1.0, 0.0)
    zeros_cp = jnp.zeros((C, PAIR), F32)
    qi = lax.broadcasted_iota(jnp.int32, (PAIR, PAIR), 0) // D
    qj = lax.broadcasted_iota(jnp.int32, (PAIR, PAIR), 1) // D
    diag = qi == qj

    PR = [(c, q) for c in range(NC) for q in range(N_PAIRS)]
    n = len(PR)
    E = range(2)
    win = lambda x, c, q: x[c * C:(c + 1) * C, q * PAIR:(q + 1) * PAIR]
    sc = [_dot_nt(jnp.concatenate([win(at_lo, c, q), win(at_hi, c, q), win(rt_lo, c, q), win(rt_hi, c, q)], axis=0),
                  jnp.concatenate([win(bt_b, c, q), win(kt_b, c, q)], axis=0)) for c, q in PR]
    vr = [pltpu.roll(win(v, c, q), D, axis=1) for c, q in PR]
    vvr = [jnp.concatenate([vr[i], vr[i]], axis=0).astype(BF16) for i in range(n)]
    m_ak = [[jnp.where(mask_ak, sc[i][e * C:(e + 1) * C], 0.0) for e in E] for i in range(n)]
    m_r = [[jnp.where(mask_r, sc[i][(2 + e) * C:(3 + e) * C], 0.0) for e in E] for i in range(n)]

    zf = [[_dot(m_ak[i][e], vvr[i]) for e in E] for i in range(n)]
    W = [[jnp.where(mask_a, sc[i][e * C:(e + 1) * C], eye_hi) for e in E] for i in range(n)]
    for k in range(int(math.log2(C))):
        Wb = [[W[i][e].astype(BF16) for e in E] for i in range(n)]
        AW = [[jnp.dot(Wb[i][e][:, :C], Wb[i][e], preferred_element_type=F32) for e in E] for i in range(n)]
        W = [[jnp.where(lo, 0.0, W[i][e]) + AW[i][e] for e in E] for i in range(n)]
    X = [[_dot(W[i][0], jnp.concatenate([zeros_cp, jnp.where(lo, win(at_f, c, q), zf[i][0])], axis=0)),
          _dot(W[i][1], jnp.concatenate([zeros_cp, jnp.where(lo, zf[i][1], win(at_f, c, q))], axis=0))]
         for i, (c, q) in enumerate(PR)]

    S = [s_scr[q] for q in range(N_PAIRS)]
    ys = []
    for c in range(NC):
        w_last = jnp.exp2(cum[(c + 1) * C - 1:(c + 1) * C, :])
        idx = [c * N_PAIRS + q for q in range(N_PAIRS)]
        mkg = [[_dot_tn(X[i][e], win(bh_b, c, q)) for e in E] for q, i in enumerate(idx)]
        vk = [_dot_tn(win(v_b, c, q), win(kh_b, c, q)) for q, i in enumerate(idx)]
        ry = [[_dot(m_r[i][0], jnp.concatenate([X[i][0], jnp.where(lo, 0.0, vr[i])], axis=0)),
               _dot(m_r[i][1], jnp.concatenate([X[i][1], jnp.where(lo, vr[i], 0.0)], axis=0))] for i in idx]
        rp = [win(rt_f, c, q) + jnp.where(lo, ry[q][0], ry[q][1]) for q in range(N_PAIRS)]
        y0 = [pltpu.roll(jnp.where(lo, ry[q][1], ry[q][0]), D, axis=1) for q in range(N_PAIRS)]
        mk = [jnp.where(diag, jnp.concatenate([mkg[q][0][:D], mkg[q][1][D:]], axis=0), 0.0) for q in range(N_PAIRS)]
        g = [jnp.where(diag, vk[q] + jnp.concatenate([mkg[q][0][D:], mkg[q][1][:D]], axis=0), 0.0)
             for q in range(N_PAIRS)]
        y = [_dot_nt(rp[q], S[q]) for q in range(N_PAIRS)]
        dS = [_dot(S[q], mk[q]) for q in range(N_PAIRS)]
        S = [S[q] * w_last[:, q * PAIR:(q + 1) * PAIR] + dS[q] + g[q] for q in range(N_PAIRS)]
        ys.append(jnp.concatenate([y[q] + y0[q] for q in range(N_PAIRS)], axis=1))
    for q in range(N_PAIRS):
        s_scr[q] = S[q]

    bonus = _seg_sum(r * k2 * rk_ref[...], blk) * v
    y_ref[0] = _group_norm_out(jnp.concatenate(ys, axis=0), bonus, gate, lnw_ref[...], lnb_ref[...], blk)

    @pl.when(t == pl.num_programs(1) - 1)
    def _():
        for q in range(N_PAIRS):
            sout_ref[0, 2 * q] = S[q][:D, :D]
            sout_ref[0, 2 * q + 1] = S[q][D:, D:]


def _rwkv_prompt(proj, mu, w0, a0, k_k, k_a, r_k, lnx_w, lnx_b, wupw, wupa, wupg):
    B, T, _ = proj.shape
    vec = lambda n: _full((1, n))
    return pl.pallas_call(
        _rwkv_prompt_kernel,
        grid=(B, T // RWKV_TILE),
        in_specs=[pl.BlockSpec((1, RWKV_TILE, RWKV_PROJ), lambda b, t: (b, t, 0)),
                  vec(RWKV_PROJ), vec(RWKV_WIDTH), vec(RWKV_WIDTH), vec(RWKV_WIDTH), vec(RWKV_WIDTH),
                  vec(RWKV_WIDTH), vec(RWKV_WIDTH), vec(RWKV_WIDTH),
                  _full((DECAY_LORA, RWKV_WIDTH)), _full((AAA_LORA, RWKV_WIDTH)), _full((GATE_LORA, RWKV_WIDTH))],
        out_specs=[pl.BlockSpec((1, RWKV_TILE, RWKV_WIDTH), lambda b, t: (b, t, 0)),
                   pl.BlockSpec((1, N_RWKV_HEADS, HEAD_DIM, HEAD_DIM), lambda b, t: (b, 0, 0, 0))],
        out_shape=[jax.ShapeDtypeStruct((B, T, RWKV_WIDTH), F32),
                   jax.ShapeDtypeStruct((B, N_RWKV_HEADS, HEAD_DIM, HEAD_DIM), F32)],
        scratch_shapes=[pltpu.VMEM((N_PAIRS, PAIR, PAIR), F32),
                        pltpu.VMEM((1, RWKV_PROJ), F32)],
        compiler_params=_params("arbitrary", "arbitrary"),
        name="rwkv_prompt",
    )(proj, mu, w0, a0, k_k, k_a, r_k, lnx_w, lnx_b, wupw, wupa, wupg)


def _rwkv_step_feat_kernel(p_ref, sh_ref, mu_ref, w0_ref, a0_ref, kk_ref, ka_ref, rk_ref,
                           wupw_ref, wupa_ref, wupg_ref, vecs_ref, bonus_ref, gate_ref):
    p = p_ref[...]
    xs = p + (sh_ref[...] - p) * mu_ref[...]
    r, k2, v, kk, a_sig, logw, gate = _rwkv_features(
        xs, w0_ref[...], a0_ref[...], kk_ref[...], ka_ref[...], wupw_ref[...], wupa_ref[...], wupg_ref[...])
    blk = _head_blocks(RWKV_WIDTH // 2)
    kkn = kk / jnp.maximum(jnp.sqrt(_seg_sum(kk * kk, blk)), 1e-12)
    for i, x in enumerate((-kkn, kkn * a_sig, jnp.exp(logw), k2, r, v)):
        vecs_ref[i] = x.T
    bonus_ref[...] = _seg_sum(r * k2 * rk_ref[...], blk) * v
    gate_ref[...] = gate


def _rwkv_step_state_kernel(s_ref, vecs_ref, sout_ref, y_ref):
    S = s_ref[0]
    a, b, w, k, r, v = (vecs_ref[i] for i in range(6))
    sa = jnp.sum(S * a[None], axis=1)
    S = S * w[None] + sa[:, None, :] * b[None] + v[:, None, :] * k[None]
    sout_ref[0] = S
    y_ref[...] = jnp.sum(S * r[None], axis=1)


def _rwkv_step_out_kernel(yt_ref, bonus_ref, gate_ref, lnw_ref, lnb_ref, o_ref):
    o_ref[...] = _group_norm_out(yt_ref[...].T, bonus_ref[...], gate_ref[...], lnw_ref[...], lnb_ref[...],
                                 _head_blocks(RWKV_WIDTH // 2))


def _rwkv_step(proj, shift, state_t, mu, w0, a0, k_k, k_a, r_k, lnx_w, lnx_b, wupw, wupa, wupg):
    n = proj.shape[0]
    vec = lambda m: _full((1, m))
    rows = _full((n, RWKV_PROJ))
    wide = _full((n, RWKV_WIDTH))
    vecs, bonus, gate = pl.pallas_call(
        _rwkv_step_feat_kernel,
        grid=(1,),
        in_specs=[rows, rows, vec(RWKV_PROJ), vec(RWKV_WIDTH), vec(RWKV_WIDTH), vec(RWKV_WIDTH), vec(RWKV_WIDTH),
                  vec(RWKV_WIDTH),
                  _full((DECAY_LORA, RWKV_WIDTH)), _full((AAA_LORA, RWKV_WIDTH)), _full((GATE_LORA, RWKV_WIDTH))],
        out_specs=[_full((6, RWKV_WIDTH, n)), wide, wide],
        out_shape=[jax.ShapeDtypeStruct((6, RWKV_WIDTH, n), F32),
                   jax.ShapeDtypeStruct((n, RWKV_WIDTH), F32), jax.ShapeDtypeStruct((n, RWKV_WIDTH), F32)],
        compiler_params=_params("arbitrary"),
        name="rwkv_step_feat",
    )(proj, shift, mu, w0, a0, k_k, k_a, r_k, wupw, wupa, wupg)
    st_spec = pl.BlockSpec((1, HEAD_DIM, HEAD_DIM, n), lambda h: (h, 0, 0, 0))
    state_new, yt = pl.pallas_call(
        _rwkv_step_state_kernel,
        grid=(N_RWKV_HEADS,),
        in_specs=[st_spec, pl.BlockSpec((6, HEAD_DIM, n), lambda h: (0, h, 0))],
        out_specs=[st_spec, pl.BlockSpec((HEAD_DIM, n), lambda h: (h, 0))],
        out_shape=[jax.ShapeDtypeStruct(state_t.shape, F32), jax.ShapeDtypeStruct((RWKV_WIDTH, n), F32)],
        compiler_params=_params("arbitrary"),
        name="rwkv_step_state",
    )(state_t, vecs)
    y = pl.pallas_call(
        _rwkv_step_out_kernel,
        grid=(1,),
        in_specs=[_full((RWKV_WIDTH, n)), wide, wide, vec(RWKV_WIDTH), vec(RWKV_WIDTH)],
        out_specs=wide,
        out_shape=jax.ShapeDtypeStruct((n, RWKV_WIDTH), F32),
        compiler_params=_params("arbitrary"),
        name="rwkv_step_out",
    )(yt, bonus, gate, lnx_w, lnx_b)
    return y, state_new


def _t5_bucket_np(dist):
    max_exact = N_BUCKETS // 2
    d = np.maximum(dist, 1).astype(np.float32)
    large = max_exact + (np.log(d / np.float32(max_exact)) / np.float32(math.log(MAX_DISTANCE / max_exact))
                         * np.float32(N_BUCKETS - max_exact)).astype(np.int32)
    large = np.minimum(large, N_BUCKETS - 1)
    return np.where(dist < max_exact, dist, large).astype(np.int32)


def _prompt_bucket_table():
    qi = np.arange(BLOCK)[:, None]
    kj = np.arange(2 * BLOCK)[None, :]
    dist = BLOCK + qi - kj
    valid = (dist >= 0) & (dist <= WINDOW)
    return np.where(valid, _t5_bucket_np(np.maximum(dist, 0)), -1).astype(np.int32)


def _decode_bucket_table():
    dist = WINDOW - np.arange(WINDOW)
    return np.broadcast_to(_t5_bucket_np(dist)[None, :], (8, WINDOW)).astype(np.int32).copy()


def _bias_from_buckets(bkt, relb_ref, h, init):
    acc = jnp.full(bkt.shape, init, F32)
    for b in range(N_BUCKETS):
        acc = jnp.where(bkt == b, relb_ref[b, h], acc)
    return acc


ATT_TILE = 4 * BLOCK


def _attn_init(bkt_ref, relb_ref, mk_ref, mv_ref, bias_scr, mk_scr, mv_scr):
    grp = N_SWA_HEADS // N_SWA_KV_HEADS
    ones = jnp.ones((N_MEM, HEAD_DIM), F32)
    bkt = bkt_ref[...]
    for j in range(N_SWA_KV_HEADS):
        for g in range(grp):
            bias_scr[j, g * BLOCK:(g + 1) * BLOCK, :] = _bias_from_buckets(bkt, relb_ref, j * grp + g, NEG) * LOG2E
    mk = mk_ref[0]
    mv = mv_ref[0]
    for h in range(N_MEM_HEADS):
        sl = slice(h * HEAD_DIM, (h + 1) * HEAD_DIM)
        mk_scr[h] = mk[:, sl].astype(BF16)
        mv_scr[h] = jnp.concatenate([mv[:, sl], ones], axis=1).astype(BF16)


def _attn_tile(first, qs, qm, kc, kp, vc, vp, sqg, skg, mqg, sink_ref, bias_scr, mk_scr, mv_scr):
    grp = N_SWA_HEADS // N_SWA_KV_HEADS
    ones = jnp.ones((2 * BLOCK, HEAD_DIM), F32)
    rowi = lax.broadcasted_iota(jnp.int32, (2 * BLOCK, 1), 0)
    col = lax.broadcasted_iota(jnp.int32, (2 * BLOCK, 2 * BLOCK), 1)
    pad_mask = jnp.logical_and(first, col < BLOCK)
    hsl = [slice(h * HEAD_DIM, (h + 1) * HEAD_DIM) for h in range(N_SWA_HEADS)]
    qs_n = _rms_heads(qs, sqg, N_SWA_HEADS) * (ATTN_SCALE * LOG2E)
    qm_n = _rms_heads(qm, mqg, N_MEM_HEADS) * (ATTN_SCALE * LOG2E)
    kc_n = _rms_heads(kc, skg, N_SWA_KV_HEADS)
    kp_n = _rms_heads(kp, skg, N_SWA_KV_HEADS)
    chains =[(a, j) for a in range(ATT_TILE // BLOCK) for j in range(N_SWA_KV_HEADS)]
    lhs, keys, vals, sinkcol = [], [], [], []
    for a, j in chains:
        rs = slice(a * BLOCK, (a + 1) * BLOCK)
        lhs.append(jnp.concatenate([qs_n[rs, hsl[j * grp + g]] for g in range(grp)], axis=0))
        if a == 0:
            keys.append(jnp.concatenate([kp_n[:, hsl[j]], kc_n[:BLOCK, hsl[j]]], axis=0))
            vv = jnp.concatenate([vp[:, hsl[j]], vc[:BLOCK, hsl[j]]], axis=0)
        else:
            ks = slice((a - 1) * BLOCK, (a + 1) * BLOCK)
            keys.append(kc_n[ks, hsl[j]])
            vv = vc[ks, hsl[j]]
        vals.append(jnp.concatenate([vv, ones], axis=1))
        sinkcol.append(jnp.where(rowi < BLOCK, sink_ref[0, j * grp], sink_ref[0, j * grp + 1]) * LOG2E)
    qmn = [qm_n[:, hsl[h]] for h in range(N_MEM_HEADS)]

    bias = [bias_scr[j] for j in range(N_SWA_KV_HEADS)]
    mem_k = [mk_scr[h] for h in range(N_MEM_HEADS)]
    mem_v = [mv_scr[h] for h in range(N_MEM_HEADS)]
    n_w = len(chains)

    def scores(t):
        return _dot_nt(lhs[t], keys[t]) if t < n_w else _dot_nt(qmn[t - n_w], mem_k[t - n_w])

    def finish(ts, s):
        e, extra = {}, {}
        for t in ts:
            if t < n_w:
                a, j = chains[t]
                st = s[t] + bias[j]
                if a == 0:
                    st = jnp.where(pad_mask, NEG, st)
                m = jnp.maximum(jnp.max(st, -1, keepdims=True), sinkcol[t])
                e[t], extra[t] = jnp.exp2(st - m), jnp.exp2(sinkcol[t] - m)
            else:
                e[t], extra[t] = jnp.exp2(s[t] - jnp.max(s[t], -1, keepdims=True)), 0.0
        o_full = {t: _dot(e[t], vals[t] if t < n_w else mem_v[t - n_w]) for t in ts}
        return {t: o_full[t][:, :HEAD_DIM] / (o_full[t][:, HEAD_DIM:HEAD_DIM + 1] + extra[t]) for t in ts}

    def place(t):
        if t < n_w:
            a, j = chains[t]
            return [(slice(a * BLOCK, (a + 1) * BLOCK), hsl[j * grp + g], slice(g * BLOCK, (g + 1) * BLOCK))
                    for g in range(grp)]
        h = t - n_w
        return [(slice(0, ATT_TILE), slice(SWA_WIDTH + h * HEAD_DIM, SWA_WIDTH + (h + 1) * HEAD_DIM), slice(0, ATT_TILE))]

    return kc_n, n_w, n_w + N_MEM_HEADS, scores, finish, place


FF_CHUNK = 1024


def _ffn_chunk(h2, w1_ref, w2_ref, c):
    u = jnp.dot(h2, w1_ref[:, c * FF_CHUNK:(c + 1) * FF_CHUNK], preferred_element_type=F32)
    u = jnp.square(jnp.maximum(u, 0.0)).astype(BF16)
    return jnp.dot(u, w2_ref[c * FF_CHUNK:(c + 1) * FF_CHUNK, :], preferred_element_type=F32)


def _mix_rows(x, yr, ys, ym, wo_ref, g2_ref):
    x1 = (x
          + jnp.dot(yr.astype(BF16), wo_ref[0:RWKV_WIDTH, :], preferred_element_type=F32)
          + jnp.dot(ys.astype(BF16), wo_ref[RWKV_WIDTH:RWKV_WIDTH + SWA_WIDTH, :], preferred_element_type=F32)
          + jnp.dot(ym.astype(BF16), wo_ref[RWKV_WIDTH + SWA_WIDTH:, :], preferred_element_type=F32))
    return x1, _rms(x1, g2_ref[...]).astype(BF16)


def _attn_ffn_kernel(tiles_per_seq, qs_ref, qm_ref, kc_ref, kp_ref, vc_ref, vp_ref, mk_ref, mv_ref, bkt_ref, relb_ref,
                     sink_ref, sqg_ref, skg_ref, mqg_ref, x_ref, yr_ref, xs_ref, yrs_ref, yss_ref, yms_ref,
                     wo_ref, g2_ref, w1_ref, w2_ref, o_ref, os_ref, kn_ref, bias_scr, mk_scr, mv_scr, stage_scr):
    s = pl.program_id(0)
    nt = pl.num_programs(0) - 1
    tile = jnp.minimum(s, nt - 1)
    first = lax.rem(tile, tiles_per_seq) == 0
    wslot = lax.rem(s, 2)
    rslot = 1 - wslot

    @pl.when(s == 0)
    def _():
        stage_scr[...] = jnp.zeros_like(stage_scr)

    @pl.when(jnp.logical_and(first, s < nt))
    def _():
        _attn_init(bkt_ref, relb_ref, mk_ref, mv_ref, bias_scr, mk_scr, mv_scr)

    kc_n, n_w, n_tasks, scores, finish, place = _attn_tile(
        first, qs_ref[0], qm_ref[0], kc_ref[0], kp_ref[0], vc_ref[0], vp_ref[0],
        sqg_ref[...], skg_ref[...], mqg_ref[...], sink_ref, bias_scr, mk_scr, mv_scr)
    kn_ref[0] = kc_n[ATT_TILE - BLOCK:]

    staged = stage_scr[rslot]
    x1, h2 = _mix_rows(x_ref[...], yr_ref[...], staged[:, :SWA_WIDTH], staged[:, SWA_WIDTH:], wo_ref, g2_ref)
    sc = {t: scores(t) for t in range(n_tasks)}
    ff = _ffn_chunk(h2, w1_ref, w2_ref, 0)
    outs = finish(range(n_w), sc)
    ff = ff + _ffn_chunk(h2, w1_ref, w2_ref, 1)
    outs.update(finish(range(n_w, n_tasks), sc))
    for c in range(2, D_FF // FF_CHUNK):
        ff = ff + _ffn_chunk(h2, w1_ref, w2_ref, c)
    o_ref[...] = x1 + ff
    for t in range(n_tasks):
        for rows, lanes, src in place(t):
            stage_scr[wslot, rows, lanes] = outs[t][src]

    @pl.when(s == nt)
    def _():
        xs1, hs2 = _mix_rows(xs_ref[...], yrs_ref[...], yss_ref[...], yms_ref[...], wo_ref, g2_ref)
        ffs = None
        for c in range(D_FF // FF_CHUNK):
            d = _ffn_chunk(hs2, w1_ref, w2_ref, c)
            ffs = d if ffs is None else ffs + d
        os_ref[...] = xs1 + ffs


def _attn_ffn(proj, mk, mv, rel_bias, sinks, sqg, skg, mqg, x2d, yr, sample, wo, g2, w1, w2):
    B, T, _ = proj.shape
    tps = T // ATT_TILE
    nt = B * tps
    ns = sample[0].shape[0]
    bkt = jnp.asarray(_prompt_bucket_table())
    smem = pl.BlockSpec(memory_space=pltpu.SMEM)
    kblk, vblk = COL_SK // SWA_KV_WIDTH, COL_SV // SWA_KV_WIDTH
    att = lambda s: jnp.minimum(s, nt - 1)
    cur = lambda s, c: (att(s) // tps, att(s) % tps, c)
    prev = lambda s, c: (att(s) // tps, jnp.maximum((ATT_TILE // BLOCK) * (att(s) % tps) - 1, 0), c)
    ffn = lambda s: (jnp.maximum(s - 1, 0), 0)
    memb = pl.BlockSpec((1, N_MEM, MEM_WIDTH), lambda s: (att(s) // tps, 0, 0))
    const = lambda shape: pl.BlockSpec(shape, lambda s: (0, 0), pipeline_mode=pl.Buffered(1))
    widths = (D_MODEL, RWKV_WIDTH, SWA_WIDTH, MEM_WIDTH)
    return pl.pallas_call(
        functools.partial(_attn_ffn_kernel, tps),
        grid=(nt + 1,),
        in_specs=[pl.BlockSpec((1, ATT_TILE, SWA_WIDTH), lambda s: cur(s, COL_SQ // SWA_WIDTH)),
                  pl.BlockSpec((1, ATT_TILE, MEM_WIDTH), lambda s: cur(s, COL_MQ // MEM_WIDTH)),
                  pl.BlockSpec((1, ATT_TILE, SWA_KV_WIDTH), lambda s: cur(s, kblk)),
                  pl.BlockSpec((1, BLOCK, SWA_KV_WIDTH), lambda s: prev(s, kblk)),
                  pl.BlockSpec((1, ATT_TILE, SWA_KV_WIDTH), lambda s: cur(s, vblk)),
                  pl.BlockSpec((1, BLOCK, SWA_KV_WIDTH), lambda s: prev(s, vblk)),
                  memb, memb,
                  _full((BLOCK, 2 * BLOCK)), smem, smem,
                  _full((1, HEAD_DIM)), _full((1, HEAD_DIM)), _full((1, HEAD_DIM)),
                  pl.BlockSpec((ATT_TILE, D_MODEL), ffn), pl.BlockSpec((ATT_TILE, RWKV_WIDTH), ffn)]
                 + [_full((ns, w)) for w in widths]
                 + [const((D_MODEL, D_MODEL)), _full((1, D_MODEL)), const((D_MODEL, D_FF)), const((D_FF, D_MODEL))],
        out_specs=[pl.BlockSpec((ATT_TILE, D_MODEL), ffn), _full((ns, D_MODEL)),
                   pl.BlockSpec((1, BLOCK, SWA_KV_WIDTH), lambda s: (att(s) // tps, 0, 0))],
        out_shape=[jax.ShapeDtypeStruct((B * T, D_MODEL), F32), jax.ShapeDtypeStruct((ns, D_MODEL), F32),
                   jax.ShapeDtypeStruct((B, BLOCK, SWA_KV_WIDTH), F32)],
        scratch_shapes=[pltpu.VMEM((N_SWA_KV_HEADS, 2 * BLOCK, 2 * BLOCK), F32),
                        pltpu.VMEM((N_MEM_HEADS, N_MEM, HEAD_DIM), BF16),
                        pltpu.VMEM((N_MEM_HEADS, N_MEM, 2 * HEAD_DIM), BF16),
                        pltpu.VMEM((2, ATT_TILE, SWA_WIDTH + MEM_WIDTH), F32)],
        compiler_params=_params("arbitrary"),
        name="attn_ffn",
    )(proj, proj, proj, proj, proj, proj, mk, mv, bkt, rel_bias, sinks, sqg, skg, mqg, x2d, yr, *sample, wo, g2, w1, w2)


def _memory_kv_kernel(mem_ref, g_ref, w_ref, kg_ref, mk_ref, mv_ref):
    kv = _dot(_rms(mem_ref[0], g_ref[...]), w_ref[...])
    kg = kg_ref[...]
    for h in range(N_MEM_HEADS):
        sl = slice(h * HEAD_DIM, (h + 1) * HEAD_DIM)
        mk_ref[0, :, sl] = _rms(kv[:, sl], kg)
    mv_ref[0] = kv[:, MEM_WIDTH:]


def _memory_kv(mem, g, w, kg):
    B = mem.shape[0]
    blk = pl.BlockSpec((1, N_MEM, MEM_WIDTH), lambda b: (b, 0, 0))
    return pl.pallas_call(
        _memory_kv_kernel,
        grid=(B,),
        in_specs=[pl.BlockSpec((1, N_MEM, D_MODEL), lambda b: (b, 0, 0)),
                  _full((1, D_MODEL)), _full((D_MODEL, 2 * MEM_WIDTH)), _full((1, HEAD_DIM))],
        out_specs=[blk, blk],
        out_shape=[jax.ShapeDtypeStruct((B, N_MEM, MEM_WIDTH), F32)] * 2,
        compiler_params=_params("arbitrary"),
        name="memory_kv",
    )(mem, g, w, kg)


SEQ_TILE = 8


def _rms_heads(x, g, n_heads):
    ms = jnp.dot((x * x).astype(BF16), _head_blocks(n_heads * HEAD_DIM), preferred_element_type=F32)
    return x * lax.rsqrt(ms * (1.0 / HEAD_DIM) + NORM_EPS) * jnp.concatenate([g] * n_heads, axis=1)


def _decode_attn_kernel(p_ref, kbuf_ref, vbuf_ref, mk_ref, mv_ref, bkt_ref, relb_ref, sink_ref,
                        sqg_ref, skg_ref, mqg_ref, ys_ref, ym_ref, kout_ref, vout_ref, tab_scr):
    grp = N_SWA_HEADS // N_SWA_KV_HEADS

    @pl.when(pl.program_id(0) == 0)
    def _():
        hrow = lax.broadcasted_iota(jnp.int32, (8, WINDOW), 0)
        lane = lax.broadcasted_iota(jnp.int32, (8, WINDOW), 1)
        bias_w = jnp.zeros((8, WINDOW), F32)
        cols = jnp.zeros((8, WINDOW), F32)
        for h in range(N_SWA_HEADS):
            bias_w = jnp.where(hrow == h, _bias_from_buckets(bkt_ref[...], relb_ref, h, 0.0), bias_w)
            cols = jnp.where(jnp.logical_and(hrow == h, lane == 0), relb_ref[0, h], cols)
            cols = jnp.where(jnp.logical_and(hrow == h, lane == 1), sink_ref[0, h], cols)
        tab_scr[0] = bias_w
        tab_scr[1] = cols

    bias_w = tab_scr[0][:N_SWA_HEADS]
    bias_new = tab_scr[1][:N_SWA_HEADS, 0:1]
    sink = tab_scr[1][:N_SWA_HEADS, 1:2]
    rowi = lax.broadcasted_iota(jnp.int32, (SWA_KV_WIDTH, WINDOW), 0)
    lanei = lax.broadcasted_iota(jnp.int32, (SWA_KV_WIDTH, WINDOW), 1)
    eye = rowi == lanei
    NH = N_SWA_HEADS
    own = (lax.broadcasted_iota(jnp.int32, (NH, NH * HEAD_DIM), 1) // HEAD_DIM
           == lax.broadcasted_iota(jnp.int32, (NH, NH * HEAD_DIM), 0))

    p = p_ref[...]
    qn = _rms_heads(p[:, COL_SQ:COL_SQ + SWA_WIDTH], sqg_ref[...], N_SWA_HEADS) * ATTN_SCALE
    kn = _rms_heads(p[:, COL_SK:COL_SK + SWA_KV_WIDTH], skg_ref[...], N_SWA_KV_HEADS)
    vn = p[:, COL_SV:COL_SV + SWA_KV_WIDTH]
    qmn = _rms_heads(p[:, COL_MQ:COL_MQ + MEM_WIDTH], mqg_ref[...], N_MEM_HEADS) * ATTN_SCALE
    rep = lambda x: jnp.concatenate([x[:, j * HEAD_DIM:(j + 1) * HEAD_DIM] for j in range(N_SWA_KV_HEADS)
                                     for _ in range(grp)], axis=1)
    kn_rep, vn_rep = rep(kn), rep(vn)

    B = range(SEQ_TILE)
    dup = lambda c: jnp.concatenate([c[j * HEAD_DIM:(j + 1) * HEAD_DIM] for j in range(N_SWA_KV_HEADS)
                                     for _ in range(grp)], axis=0)
    qd = [jnp.where(own, qn[b:b + 1, :], 0.0) for b in B]
    qmd = [jnp.where(own, qmn[b:b + 1, :], 0.0) for b in B]
    kdup = [dup(kbuf_ref[b]) for b in B]
    vdup = [dup(vbuf_ref[b]) for b in B]
    s = [_dot(qd[b], kdup[b]) + bias_w for b in B]
    sm = [_dot(qmd[b], mk_ref[b]) for b in B]
    s_new = [jnp.sum(qd[b] * kn_rep[b:b + 1, :], -1, keepdims=True) + bias_new for b in B]
    m = [jnp.maximum(jnp.maximum(jnp.max(s[b], -1, keepdims=True), s_new[b]), sink) for b in B]
    e = [jnp.exp(s[b] - m[b]) for b in B]
    e_new = [jnp.exp(s_new[b] - m[b]) for b in B]
    den = [jnp.sum(e[b], -1, keepdims=True) + e_new[b] + jnp.exp(sink - m[b]) for b in B]
    em = [jnp.exp(sm[b] - jnp.max(sm[b], -1, keepdims=True)) for b in B]
    ov = [_dot_nt(e[b], vdup[b]) for b in B]
    omf = [_dot_nt(em[b], mv_ref[b]) for b in B]
    ys_rows = [jnp.sum(jnp.where(own, (ov[b] + e_new[b] * vn_rep[b:b + 1, :]) / den[b], 0.0), 0, keepdims=True)
               for b in B]
    ym_rows = [jnp.sum(jnp.where(own, omf[b] / jnp.sum(em[b], -1, keepdims=True), 0.0), 0, keepdims=True)
               for b in B]
    ys_ref[...] = jnp.concatenate(ys_rows, axis=0)
    ym_ref[...] = jnp.concatenate(ym_rows, axis=0)
    for b in B:
        kn_col = jnp.sum(jnp.where(eye, kn[b:b + 1, :], 0.0), -1, keepdims=True)
        vn_col = jnp.sum(jnp.where(eye, vn[b:b + 1, :], 0.0), -1, keepdims=True)
        kout_ref[b] = jnp.where(lanei == WINDOW - 1, kn_col, pltpu.roll(kbuf_ref[b], WINDOW - 1, axis=1))
        vout_ref[b] = jnp.where(lanei == WINDOW - 1, vn_col, pltpu.roll(vbuf_ref[b], WINDOW - 1, axis=1))


def _decode_attn(proj, kbuf, vbuf, mk, mv, rel_bias, sinks, sqg, skg, mqg):
    n = proj.shape[0]
    bkt = jnp.asarray(_decode_bucket_table())
    smem = pl.BlockSpec(memory_space=pltpu.SMEM)
    win = pl.BlockSpec((SEQ_TILE, WINDOW, SWA_KV_WIDTH), lambda i: (i, 0, 0))
    memb = pl.BlockSpec((SEQ_TILE, N_MEM, MEM_WIDTH), lambda i: (i, 0, 0))
    return pl.pallas_call(
        _decode_attn_kernel,
        grid=(n // SEQ_TILE,),
        in_specs=[pl.BlockSpec((SEQ_TILE, IN_PROJ), lambda i: (i, 0)), win, win, memb, memb,
                  _full((8, WINDOW)), smem, smem, _full((1, HEAD_DIM)), _full((1, HEAD_DIM)), _full((1, HEAD_DIM))],
        out_specs=[pl.BlockSpec((SEQ_TILE, SWA_WIDTH), lambda i: (i, 0)),
                   pl.BlockSpec((SEQ_TILE, MEM_WIDTH), lambda i: (i, 0)), win, win],
        out_shape=[jax.ShapeDtypeStruct((n, SWA_WIDTH), F32), jax.ShapeDtypeStruct((n, MEM_WIDTH), F32),
                   jax.ShapeDtypeStruct(kbuf.shape, F32), jax.ShapeDtypeStruct(vbuf.shape, F32)],
        scratch_shapes=[pltpu.VMEM((2, 8, WINDOW), F32)],
        compiler_params=_params("arbitrary"),
        name="decode_attn",
    )(proj, kbuf, vbuf, mk, mv, bkt, rel_bias, sinks, sqg, skg, mqg)


def kernel(x_prompt, x_sample, state_rwkv, state_shift, cache_swa_k, cache_swa_v, cache_mem_k, cache_mem_v,
           mem_prompt, rel_bias, norm1_g, w_in, mu_shift, w0, w_up_w, a0, w_up_a, w_up_g, k_k, k_a, r_k,
           lnx_w, lnx_b, q_norm_swa, k_norm_swa, sinks, mem_norm_g, w_mem_kv, q_norm_mem, k_norm_mem,
           w_out, norm2_g, w_ff1, w_ff2):
    B, T, _ = x_prompt.shape
    Bd = x_sample.shape[0]
    l = 0
    rwkv_params = (mu_shift[l][None], w0[l][None], a0[l][None], k_k[l][None], k_a[l][None],
                   r_k[l].reshape(1, RWKV_WIDTH), lnx_w[l][None], lnx_b[l][None],
                   w_up_w[l], w_up_a[l], w_up_g[l])
    sqg, skg, mqg, mkg = q_norm_swa[l][None], k_norm_swa[l][None], q_norm_mem[l][None], k_norm_mem[l][None]
    g1, g2 = norm1_g[l][None], norm2_g[l][None]

    xp = x_prompt.reshape(B * T, D_MODEL)
    xs = x_sample.reshape(Bd, D_MODEL)
    proj_p, proj_s, (w_out_b, w1_b, w2_b) = _in_proj(xp, xs, g1, w_in[l], (w_out[l], w_ff1[l], w_ff2[l]),
                                                     IN_PROJ_TILE)
    proj_p = proj_p.reshape(B, T, IN_PROJ)

    mk, mv = _memory_kv(mem_prompt, mem_norm_g[l][None], w_mem_kv[l], mkg)
    yr_p, s_p = _rwkv_prompt(proj_p, *rwkv_params)
    shift_p = proj_p[:, T - 1, :RWKV_PROJ]
    vb_p = proj_p[:, T - WINDOW:, COL_SV:COL_SV + SWA_KV_WIDTH]

    yr_s, st_s = _rwkv_step(proj_s[:, :RWKV_PROJ], state_shift[l], jnp.transpose(state_rwkv[l], (1, 2, 3, 0)),
                            *rwkv_params)
    s_s = jnp.transpose(st_s, (3, 0, 1, 2))
    fmajor = lambda c: jnp.transpose(c, (0, 2, 3, 1)).reshape(Bd, c.shape[2] * HEAD_DIM, c.shape[1])
    ys_s, ym_s, kb_s, vb_s = _decode_attn(
        proj_s, fmajor(cache_swa_k[l]), fmajor(cache_swa_v[l]), fmajor(cache_mem_k[l]), fmajor(cache_mem_v[l]),
        rel_bias, sinks[l][None], sqg, skg, mqg)
    pmajor = lambda c: jnp.transpose(c.reshape(Bd, N_SWA_KV_HEADS, HEAD_DIM, WINDOW), (0, 3, 1, 2))[None]

    y_p, y_s, kn_p = _attn_ffn(proj_p, mk, mv, rel_bias, sinks[l][None], sqg, skg, mqg,
                               xp, yr_p.reshape(B * T, RWKV_WIDTH), (xs, yr_s, ys_s, ym_s), w_out_b, g2, w1_b, w2_b)
    y_p = y_p.reshape(B, T, D_MODEL)
    y_s = y_s.reshape(Bd, 1, D_MODEL)

    return (y_p, y_s,
            s_p[None], shift_p[None],
            kn_p.reshape(1, B, WINDOW, N_SWA_KV_HEADS, HEAD_DIM),
            vb_p.reshape(1, B, WINDOW, N_SWA_KV_HEADS, HEAD_DIM),
            mk.reshape(1, B, N_MEM, N_MEM_HEADS, HEAD_DIM), mv.reshape(1, B, N_MEM, N_MEM_HEADS, HEAD_DIM),
            s_s[None], proj_s[:, :RWKV_PROJ][None],
            pmajor(kb_s), pmajor(vb_s))
```

```python
import functools
import math

import numpy as np
import jax
import jax.numpy as jnp
from jax import lax
from jax.experimental import pallas as pl
from jax.experimental.pallas import tpu as pltpu

F32 = jnp.float32
BF16 = jnp.bfloat16

D_MODEL = 1024
HEAD_DIM = 64
RWKV_WIDTH = 512
N_RWKV_HEADS = 8
SWA_WIDTH = 256
N_SWA_HEADS = 4
N_SWA_KV_HEADS = 2
SWA_KV_WIDTH = 128
MEM_WIDTH = 256
N_MEM_HEADS = 4
N_MEM = 256
WINDOW = 128
BLOCK = 128
N_BUCKETS = 32
MAX_DISTANCE = 128
DECAY_LORA = 64
AAA_LORA = 64
GATE_LORA = 128
RWKV_PROJ = 3 * RWKV_WIDTH + DECAY_LORA + AAA_LORA + GATE_LORA
SWA_PROJ = SWA_WIDTH + 2 * SWA_KV_WIDTH
IN_PROJ = RWKV_PROJ + SWA_PROJ + MEM_WIDTH
D_FF = 4 * D_MODEL
NORM_EPS = 1e-6
LNX_EPS = 64e-5
ATTN_SCALE = HEAD_DIM ** -0.5
EXP_M05 = math.exp(-0.5)
LOG2E = math.log2(math.e)
NEG = -1e30

COL_R, COL_K, COL_V = 0, RWKV_WIDTH, 2 * RWKV_WIDTH
COL_WD = 3 * RWKV_WIDTH
COL_AD = COL_WD + DECAY_LORA
COL_GD = COL_AD + AAA_LORA
COL_SQ = RWKV_PROJ
COL_SK = COL_SQ + SWA_WIDTH
COL_SV = COL_SK + SWA_KV_WIDTH
COL_MQ = RWKV_PROJ + SWA_PROJ

CHUNK = 64
VMEM_LIMIT = 56 * 1024 * 1024


def _dot(a, b):
    return jnp.dot(a.astype(BF16), b.astype(BF16), preferred_element_type=F32)


def _dot_nt(a, b):
    return lax.dot_general(a.astype(BF16), b.astype(BF16), (((1,), (1,)), ((), ())),
                           preferred_element_type=F32)


def _dot_tn(a, b):
    return lax.dot_general(a.astype(BF16), b.astype(BF16), (((0,), (0,)), ((), ())),
                           preferred_element_type=F32)


def _rms(x, g):
    return x * lax.rsqrt(jnp.mean(x * x, -1, keepdims=True) + NORM_EPS) * g


def _params(*sem):
    return pltpu.CompilerParams(dimension_semantics=sem, vmem_limit_bytes=VMEM_LIMIT)


def _full(shape):
    n = len(shape)
    return pl.BlockSpec(shape, lambda *_: (0,) * n)


def _head_blocks(width):
    bi = lax.broadcasted_iota(jnp.int32, (width, width), 0) // HEAD_DIM
    bj = lax.broadcasted_iota(jnp.int32, (width, width), 1) // HEAD_DIM
    return jnp.where(bi == bj, 1.0, 0.0).astype(BF16)


IN_PROJ_TILE = 1024
IN_PROJ_SUB = 256


def _in_proj_kernel(x_ref, xs_ref, g_ref, w_ref, o_ref, os_ref, wb_scr):
    @pl.when(pl.program_id(0) == 0)
    def _():
        wb_scr[...] = w_ref[...].astype(BF16)

    tm = x_ref.shape[0]
    for j in range(tm // IN_PROJ_SUB):
        rows = slice(j * IN_PROJ_SUB, (j + 1) * IN_PROJ_SUB)
        h = _rms(x_ref[rows, :], g_ref[...])
        o_ref[rows, :] = jnp.dot(h.astype(BF16), wb_scr[...], preferred_element_type=F32)

    @pl.when(pl.program_id(0) == pl.num_programs(0) - 1)
    def _():
        hs = _rms(xs_ref[...], g_ref[...])
        os_ref[...] = jnp.dot(hs.astype(BF16), wb_scr[...], preferred_element_type=F32)


def _in_proj(x2d, xs2d, g, w, tm):
    n, ns = x2d.shape[0], xs2d.shape[0]
    return pl.pallas_call(
        _in_proj_kernel,
        grid=(n // tm,),
        in_specs=[pl.BlockSpec((tm, D_MODEL), lambda i: (i, 0)),
                  _full((ns, D_MODEL)),
                  _full((1, D_MODEL)),
                  pl.BlockSpec((D_MODEL, IN_PROJ), lambda i: (0, 0), pipeline_mode=pl.Buffered(1))],
        out_specs=[pl.BlockSpec((tm, IN_PROJ), lambda i: (i, 0)), _full((ns, IN_PROJ))],
        out_shape=[jax.ShapeDtypeStruct((n, IN_PROJ), F32), jax.ShapeDtypeStruct((ns, IN_PROJ), F32)],
        scratch_shapes=[pltpu.VMEM((D_MODEL, IN_PROJ), BF16)],
        compiler_params=_params("arbitrary"),
        name="in_proj",
    )(x2d, xs2d, g, w)


def _rwkv_features(xs, w0, a0, k_k, k_a, wupw, wupa, wupg):
    r = xs[:, COL_R:COL_R + RWKV_WIDTH]
    k = xs[:, COL_K:COL_K + RWKV_WIDTH]
    v = xs[:, COL_V:COL_V + RWKV_WIDTH]
    wd = xs[:, COL_WD:COL_WD + DECAY_LORA]
    ad = xs[:, COL_AD:COL_AD + AAA_LORA]
    gd = xs[:, COL_GD:COL_GD + GATE_LORA]
    logw = -jax.nn.sigmoid(w0 + _dot(jnp.tanh(wd), wupw)) * EXP_M05
    a_sig = jax.nn.sigmoid(a0 + _dot(ad, wupa))
    gate = _dot(jax.nn.sigmoid(gd), wupg)
    kk = k * k_k
    k2 = k * (1.0 + (a_sig - 1.0) * k_a)
    return r, k2, v, kk, a_sig, logw, gate


def _seg_sum(x, blk):
    xb = x.astype(BF16)
    half = RWKV_WIDTH // 2
    return jnp.concatenate([jnp.dot(xb[:, :half], blk, preferred_element_type=F32),
                            jnp.dot(xb[:, half:], blk, preferred_element_type=F32)], axis=1)


def _group_norm_out(y, bonus, gate, lnx_w, lnx_b, blk):
    inv_d = 1.0 / HEAD_DIM
    m = _seg_sum(y, blk) * inv_d
    d = y - m
    var = _seg_sum(d * d, blk) * inv_d
    yn = d * lax.rsqrt(var + LNX_EPS) * lnx_w + lnx_b
    return (yn + bonus) * gate


RWKV_TILE = 4 * CHUNK
PAIR = 2 * HEAD_DIM
N_PAIRS = N_RWKV_HEADS // 2


def _rwkv_prompt_kernel(p_ref, mu_ref, w0_ref, a0_ref, kk_ref, ka_ref, rk_ref, lnw_ref, lnb_ref,
                        wupw_ref, wupa_ref, wupg_ref, wo_ref, w1_ref, w2_ref,
                        y_ref, sout_ref, wob_ref, w1b_ref, w2b_ref, s_scr, prev_scr):
    C, TT, D = CHUNK, RWKV_TILE, HEAD_DIM
    NC = TT // C
    t = pl.program_id(1)

    wob_ref[...] = wo_ref[...].astype(BF16)
    w1b_ref[...] = w1_ref[...].astype(BF16)
    w2b_ref[...] = w2_ref[...].astype(BF16)

    @pl.when(t == 0)
    def _():
        s_scr[...] = jnp.zeros_like(s_scr)
        prev_scr[...] = jnp.zeros_like(prev_scr)

    p = p_ref[0]
    row = lax.broadcasted_iota(jnp.int32, p.shape, 0)
    prev = jnp.where(row == 0, prev_scr[...], pltpu.roll(p, 1, axis=0))
    prev_scr[...] = p[TT - 1:TT, :]
    xs = p + (prev - p) * mu_ref[...]
    r, k2, v, kk, a_sig, logw, gate = _rwkv_features(
        xs, w0_ref[...], a0_ref[...], kk_ref[...], ka_ref[...], wupw_ref[...], wupa_ref[...], wupg_ref[...])

    blk = _head_blocks(RWKV_WIDTH // 2)
    kkn = kk / jnp.maximum(jnp.sqrt(_seg_sum(kk * kk, blk)), 1e-12)
    bb = kkn * a_sig

    ri = lax.broadcasted_iota(jnp.int32, (TT, TT), 0)
    ci = lax.broadcasted_iota(jnp.int32, (TT, TT), 1)
    tri = jnp.where(jnp.logical_and(ri >= ci, ri // C == ci // C), 1.0, 0.0).astype(BF16)
    lw2 = logw * LOG2E
    l1 = lw2.astype(BF16)
    l2 = (lw2 - l1.astype(F32)).astype(BF16)
    cum = jnp.dot(tri, l1, preferred_element_type=F32) + jnp.dot(tri, l2, preferred_element_type=F32)
    c_last = jnp.concatenate([jnp.broadcast_to(cum[(c + 1) * C - 1:(c + 1) * C, :], (C, RWKV_WIDTH))
                              for c in range(NC)], axis=0)
    e_pos = jnp.exp2(cum)
    e_neg = jnp.exp2(-cum)
    e_prev = jnp.exp2(cum - lw2)
    e_last = jnp.exp2(c_last - cum)

    lo_full = (lax.broadcasted_iota(jnp.int32, (TT, RWKV_WIDTH), 1) % PAIR) < D
    at_f = -kkn * e_prev
    rt_f = r * e_pos
    at_lo = jnp.where(lo_full, at_f, 0.0).astype(BF16)
    at_hi = jnp.where(lo_full, 0.0, at_f).astype(BF16)
    rt_lo = jnp.where(lo_full, rt_f, 0.0).astype(BF16)
    rt_hi = jnp.where(lo_full, 0.0, rt_f).astype(BF16)
    bt_b = (bb * e_neg).astype(BF16)
    kt_b = (k2 * e_neg).astype(BF16)
    bh_b = (bb * e_last).astype(BF16)
    kh_b = (k2 * e_last).astype(BF16)
    v_b = v.astype(BF16)

    r2 = lax.broadcasted_iota(jnp.int32, (C, PAIR), 0)
    c2 = lax.broadcasted_iota(jnp.int32, (C, PAIR), 1)
    lo = c2 < D
    c2m = jnp.where(lo, c2, c2 - C)
    mask_a = jnp.logical_and(lo, c2 < r2)
    mask_ak = jnp.logical_and(jnp.logical_not(lo), c2m < r2)
    mask_r = c2m <= r2
    eye_hi = jnp.where(jnp.logical_and(jnp.logical_not(lo), c2m == r2), 1.0, 0.0)
    zeros_cp = jnp.zeros((C, PAIR), F32)
    qi = lax.broadcasted_iota(jnp.int32, (PAIR, PAIR), 0) // D
    qj = lax.broadcasted_iota(jnp.int32, (PAIR, PAIR), 1) // D
    diag = qi == qj

    PR = [(c, q) for c in range(NC) for q in range(N_PAIRS)]
    n = len(PR)
    E = range(2)
    win = lambda x, c, q: x[c * C:(c + 1) * C, q * PAIR:(q + 1) * PAIR]
    sc = [_dot_nt(jnp.concatenate([win(at_lo, c, q), win(at_hi, c, q), win(rt_lo, c, q), win(rt_hi, c, q)], axis=0),
                  jnp.concatenate([win(bt_b, c, q), win(kt_b, c, q)], axis=0)) for c, q in PR]
    vr = [pltpu.roll(win(v, c, q), D, axis=1) for c, q in PR]
    vvr = [jnp.concatenate([vr[i], vr[i]], axis=0).astype(BF16) for i in range(n)]
    m_ak = [[jnp.where(mask_ak, sc[i][e * C:(e + 1) * C], 0.0) for e in E] for i in range(n)]
    m_r = [[jnp.where(mask_r, sc[i][(2 + e) * C:(3 + e) * C], 0.0) for e in E] for i in range(n)]

    zf = [[_dot(m_ak[i][e], vvr[i]) for e in E] for i in range(n)]
    W = [[jnp.where(mask_a, sc[i][e * C:(e + 1) * C], eye_hi) for e in E] for i in range(n)]
    for k in range(int(math.log2(C))):
        Wb = [[W[i][e].astype(BF16) for e in E] for i in range(n)]
        AW = [[jnp.dot(Wb[i][e][:, :C], Wb[i][e], preferred_element_type=F32) for e in E] for i in range(n)]
        W = [[jnp.where(lo, 0.0, W[i][e]) + AW[i][e] for e in E] for i in range(n)]
    X = [[_dot(W[i][0], jnp.concatenate([zeros_cp, jnp.where(lo, win(at_f, c, q), zf[i][0])], axis=0)),
          _dot(W[i][1], jnp.concatenate([zeros_cp, jnp.where(lo, zf[i][1], win(at_f, c, q))], axis=0))]
         for i, (c, q) in enumerate(PR)]

    S = [s_scr[q] for q in range(N_PAIRS)]
    ys = []
    for c in range(NC):
        w_last = jnp.exp2(cum[(c + 1) * C - 1:(c + 1) * C, :])
        idx = [c * N_PAIRS + q for q in range(N_PAIRS)]
        mkg = [[_dot_tn(X[i][e], win(bh_b, c, q)) for e in E] for q, i in enumerate(idx)]
        vk = [_dot_tn(win(v_b, c, q), win(kh_b, c, q)) for q, i in enumerate(idx)]
        ry = [[_dot(m_r[i][0], jnp.concatenate([X[i][0], jnp.where(lo, 0.0, vr[i])], axis=0)),
               _dot(m_r[i][1], jnp.concatenate([X[i][1], jnp.where(lo, vr[i], 0.0)], axis=0))] for i in idx]
        rp = [win(rt_f, c, q) + jnp.where(lo, ry[q][0], ry[q][1]) for q in range(N_PAIRS)]
        y0 = [pltpu.roll(jnp.where(lo, ry[q][1], ry[q][0]), D, axis=1) for q in range(N_PAIRS)]
        mk = [jnp.where(diag, jnp.concatenate([mkg[q][0][:D], mkg[q][1][D:]], axis=0), 0.0) for q in range(N_PAIRS)]
        g = [jnp.where(diag, vk[q] + jnp.concatenate([mkg[q][0][D:], mkg[q][1][:D]], axis=0), 0.0)
             for q in range(N_PAIRS)]
        y = [_dot_nt(rp[q], S[q]) for q in range(N_PAIRS)]
        dS = [_dot(S[q], mk[q]) for q in range(N_PAIRS)]
        S = [S[q] * w_last[:, q * PAIR:(q + 1) * PAIR] + dS[q] + g[q] for q in range(N_PAIRS)]
        ys.append(jnp.concatenate([y[q] + y0[q] for q in range(N_PAIRS)], axis=1))
    for q in range(N_PAIRS):
        s_scr[q] = S[q]

    bonus = _seg_sum(r * k2 * rk_ref[...], blk) * v
    y_ref[0] = _group_norm_out(jnp.concatenate(ys, axis=0), bonus, gate, lnw_ref[...], lnb_ref[...], blk)

    @pl.when(t == pl.num_programs(1) - 1)
    def _():
        for q in range(N_PAIRS):
            sout_ref[0, 2 * q] = S[q][:D, :D]
            sout_ref[0, 2 * q + 1] = S[q][D:, D:]


def _rwkv_prompt(proj, mu, w0, a0, k_k, k_a, r_k, lnx_w, lnx_b, wupw, wupa, wupg, later_weights):
    B, T, _ = proj.shape
    nt = T // RWKV_TILE
    vec = lambda n: _full((1, n))
    slab = lambda m: pl.BlockSpec((m.shape[0] // (B * nt), m.shape[1]), lambda b, t: (b * nt + t, 0))
    y, s_out, *later_bf16 = pl.pallas_call(
        _rwkv_prompt_kernel,
        grid=(B, nt),
        in_specs=[pl.BlockSpec((1, RWKV_TILE, RWKV_PROJ), lambda b, t: (b, t, 0)),
                  vec(RWKV_PROJ), vec(RWKV_WIDTH), vec(RWKV_WIDTH), vec(RWKV_WIDTH), vec(RWKV_WIDTH),
                  vec(RWKV_WIDTH), vec(RWKV_WIDTH), vec(RWKV_WIDTH),
                  _full((DECAY_LORA, RWKV_WIDTH)), _full((AAA_LORA, RWKV_WIDTH)), _full((GATE_LORA, RWKV_WIDTH))]
                 + [slab(m) for m in later_weights],
        out_specs=[pl.BlockSpec((1, RWKV_TILE, RWKV_WIDTH), lambda b, t: (b, t, 0)),
                   pl.BlockSpec((1, N_RWKV_HEADS, HEAD_DIM, HEAD_DIM), lambda b, t: (b, 0, 0, 0))]
                  + [slab(m) for m in later_weights],
        out_shape=[jax.ShapeDtypeStruct((B, T, RWKV_WIDTH), F32),
                   jax.ShapeDtypeStruct((B, N_RWKV_HEADS, HEAD_DIM, HEAD_DIM), F32)]
                  + [jax.ShapeDtypeStruct(m.shape, BF16) for m in later_weights],
        scratch_shapes=[pltpu.VMEM((N_PAIRS, PAIR, PAIR), F32),
                        pltpu.VMEM((1, RWKV_PROJ), F32)],
        compiler_params=_params("arbitrary", "arbitrary"),
        name="rwkv_prompt",
    )(proj, mu, w0, a0, k_k, k_a, r_k, lnx_w, lnx_b, wupw, wupa, wupg, *later_weights)
    return y, s_out, later_bf16


def _rwkv_step_feat_kernel(p_ref, sh_ref, mu_ref, w0_ref, a0_ref, kk_ref, ka_ref, rk_ref,
                           wupw_ref, wupa_ref, wupg_ref, vecs_ref, bonus_ref, gate_ref):
    p = p_ref[...]
    xs = p + (sh_ref[...] - p) * mu_ref[...]
    r, k2, v, kk, a_sig, logw, gate = _rwkv_features(
        xs, w0_ref[...], a0_ref[...], kk_ref[...], ka_ref[...], wupw_ref[...], wupa_ref[...], wupg_ref[...])
    blk = _head_blocks(RWKV_WIDTH // 2)
    kkn = kk / jnp.maximum(jnp.sqrt(_seg_sum(kk * kk, blk)), 1e-12)
    for i, x in enumerate((-kkn, kkn * a_sig, jnp.exp(logw), k2, r, v)):
        vecs_ref[i] = x.T
    bonus_ref[...] = _seg_sum(r * k2 * rk_ref[...], blk) * v
    gate_ref[...] = gate


def _rwkv_step_state_kernel(s_ref, vecs_ref, sout_ref, y_ref):
    S = s_ref[0]
    a, b, w, k, r, v = (vecs_ref[i] for i in range(6))
    sa = jnp.sum(S * a[None], axis=1)
    S = S * w[None] + sa[:, None, :] * b[None] + v[:, None, :] * k[None]
    sout_ref[0] = S
    y_ref[...] = jnp.sum(S * r[None], axis=1)


def _rwkv_step_out_kernel(yt_ref, bonus_ref, gate_ref, lnw_ref, lnb_ref, o_ref):
    o_ref[...] = _group_norm_out(yt_ref[...].T, bonus_ref[...], gate_ref[...], lnw_ref[...], lnb_ref[...],
                                 _head_blocks(RWKV_WIDTH // 2))


def _rwkv_step(proj, shift, state_t, mu, w0, a0, k_k, k_a, r_k, lnx_w, lnx_b, wupw, wupa, wupg):
    n = proj.shape[0]
    vec = lambda m: _full((1, m))
    rows = _full((n, RWKV_PROJ))
    wide = _full((n, RWKV_WIDTH))
    vecs, bonus, gate = pl.pallas_call(
        _rwkv_step_feat_kernel,
        grid=(1,),
        in_specs=[rows, rows, vec(RWKV_PROJ), vec(RWKV_WIDTH), vec(RWKV_WIDTH), vec(RWKV_WIDTH), vec(RWKV_WIDTH),
                  vec(RWKV_WIDTH),
                  _full((DECAY_LORA, RWKV_WIDTH)), _full((AAA_LORA, RWKV_WIDTH)), _full((GATE_LORA, RWKV_WIDTH))],
        out_specs=[_full((6, RWKV_WIDTH, n)), wide, wide],
        out_shape=[jax.ShapeDtypeStruct((6, RWKV_WIDTH, n), F32),
                   jax.ShapeDtypeStruct((n, RWKV_WIDTH), F32), jax.ShapeDtypeStruct((n, RWKV_WIDTH), F32)],
        compiler_params=_params("arbitrary"),
        name="rwkv_step_feat",
    )(proj, shift, mu, w0, a0, k_k, k_a, r_k, wupw, wupa, wupg)
    st_spec = pl.BlockSpec((1, HEAD_DIM, HEAD_DIM, n), lambda h: (h, 0, 0, 0))
    state_new, yt = pl.pallas_call(
        _rwkv_step_state_kernel,
        grid=(N_RWKV_HEADS,),
        in_specs=[st_spec, pl.BlockSpec((6, HEAD_DIM, n), lambda h: (0, h, 0))],
        out_specs=[st_spec, pl.BlockSpec((HEAD_DIM, n), lambda h: (h, 0))],
        out_shape=[jax.ShapeDtypeStruct(state_t.shape, F32), jax.ShapeDtypeStruct((RWKV_WIDTH, n), F32)],
        compiler_params=_params("arbitrary"),
        name="rwkv_step_state",
    )(state_t, vecs)
    y = pl.pallas_call(
        _rwkv_step_out_kernel,
        grid=(1,),
        in_specs=[_full((RWKV_WIDTH, n)), wide, wide, vec(RWKV_WIDTH), vec(RWKV_WIDTH)],
        out_specs=wide,
        out_shape=jax.ShapeDtypeStruct((n, RWKV_WIDTH), F32),
        compiler_params=_params("arbitrary"),
        name="rwkv_step_out",
    )(yt, bonus, gate, lnx_w, lnx_b)
    return y, state_new


def _t5_bucket_np(dist):
    max_exact = N_BUCKETS // 2
    d = np.maximum(dist, 1).astype(np.float32)
    large = max_exact + (np.log(d / np.float32(max_exact)) / np.float32(math.log(MAX_DISTANCE / max_exact))
                         * np.float32(N_BUCKETS - max_exact)).astype(np.int32)
    large = np.minimum(large, N_BUCKETS - 1)
    return np.where(dist < max_exact, dist, large).astype(np.int32)


def _prompt_bucket_table():
    qi = np.arange(BLOCK)[:, None]
    kj = np.arange(2 * BLOCK)[None, :]
    dist = BLOCK + qi - kj
    valid = (dist >= 0) & (dist <= WINDOW)
    return np.where(valid, _t5_bucket_np(np.maximum(dist, 0)), -1).astype(np.int32)


def _decode_bucket_table():
    dist = WINDOW - np.arange(WINDOW)
    return np.broadcast_to(_t5_bucket_np(dist)[None, :], (8, WINDOW)).astype(np.int32).copy()


def _bias_from_buckets(bkt, relb_ref, h, init):
    acc = jnp.full(bkt.shape, init, F32)
    for b in range(N_BUCKETS):
        acc = jnp.where(bkt == b, relb_ref[b, h], acc)
    return acc


ATT_TILE = 4 * BLOCK


def _attn_init(bkt_ref, relb_ref, mk_ref, mv_ref, bias_scr, mk_scr, mv_scr):
    grp = N_SWA_HEADS // N_SWA_KV_HEADS
    ones = jnp.ones((N_MEM, HEAD_DIM), F32)
    bkt = bkt_ref[...]
    for j in range(N_SWA_KV_HEADS):
        for g in range(grp):
            bias_scr[j, g * BLOCK:(g + 1) * BLOCK, :] = _bias_from_buckets(bkt, relb_ref, j * grp + g, NEG) * LOG2E
    mk = mk_ref[0]
    mv = mv_ref[0]
    for h in range(N_MEM_HEADS):
        sl = slice(h * HEAD_DIM, (h + 1) * HEAD_DIM)
        mk_scr[h] = mk[:, sl].astype(BF16)
        mv_scr[h] = jnp.concatenate([mv[:, sl], ones], axis=1).astype(BF16)


def _attn_tile(first, qs, qm, kc, kp, vc, vp, sqg, skg, mqg, sink_ref, bias_scr, mk_scr, mv_scr):
    grp = N_SWA_HEADS // N_SWA_KV_HEADS
    ones = jnp.ones((2 * BLOCK, HEAD_DIM), F32)
    rowi = lax.broadcasted_iota(jnp.int32, (2 * BLOCK, 1), 0)
    col = lax.broadcasted_iota(jnp.int32, (2 * BLOCK, 2 * BLOCK), 1)
    pad_mask = jnp.logical_and(first, col < BLOCK)
    hsl = [slice(h * HEAD_DIM, (h + 1) * HEAD_DIM) for h in range(N_SWA_HEADS)]
    qs_n = _rms_heads(qs, sqg, N_SWA_HEADS) * (ATTN_SCALE * LOG2E)
    qm_n = _rms_heads(qm, mqg, N_MEM_HEADS) * (ATTN_SCALE * LOG2E)
    kc_n = _rms_heads(kc, skg, N_SWA_KV_HEADS)
    kp_n = _rms_heads(kp, skg, N_SWA_KV_HEADS)
    chains =[(a, j) for a in range(ATT_TILE // BLOCK) for j in range(N_SWA_KV_HEADS)]
    lhs, keys, vals, sinkcol = [], [], [], []
    for a, j in chains:
        rs = slice(a * BLOCK, (a + 1) * BLOCK)
        lhs.append(jnp.concatenate([qs_n[rs, hsl[j * grp + g]] for g in range(grp)], axis=0))
        if a == 0:
            keys.append(jnp.concatenate([kp_n[:, hsl[j]], kc_n[:BLOCK, hsl[j]]], axis=0))
            vv = jnp.concatenate([vp[:, hsl[j]], vc[:BLOCK, hsl[j]]], axis=0)
        else:
            ks = slice((a - 1) * BLOCK, (a + 1) * BLOCK)
            keys.append(kc_n[ks, hsl[j]])
            vv = vc[ks, hsl[j]]
        vals.append(jnp.concatenate([vv, ones], axis=1))
        sinkcol.append(jnp.where(rowi < BLOCK, sink_ref[0, j * grp], sink_ref[0, j * grp + 1]) * LOG2E)
    qmn = [qm_n[:, hsl[h]] for h in range(N_MEM_HEADS)]

    bias = [bias_scr[j] for j in range(N_SWA_KV_HEADS)]
    mem_k = [mk_scr[h] for h in range(N_MEM_HEADS)]
    mem_v = [mv_scr[h] for h in range(N_MEM_HEADS)]
    n_w = len(chains)

    def scores(t):
        return _dot_nt(lhs[t], keys[t]) if t < n_w else _dot_nt(qmn[t - n_w], mem_k[t - n_w])

    def finish(ts, s):
        e, extra = {}, {}
        for t in ts:
            if t < n_w:
                a, j = chains[t]
                st = s[t] + bias[j]
                if a == 0:
                    st = jnp.where(pad_mask, NEG, st)
                m = jnp.maximum(jnp.max(st, -1, keepdims=True), sinkcol[t])
                e[t], extra[t] = jnp.exp2(st - m), jnp.exp2(sinkcol[t] - m)
            else:
                e[t], extra[t] = jnp.exp2(s[t] - jnp.max(s[t], -1, keepdims=True)), 0.0
        o_full = {t: _dot(e[t], vals[t] if t < n_w else mem_v[t - n_w]) for t in ts}
        return {t: o_full[t][:, :HEAD_DIM] / (o_full[t][:, HEAD_DIM:HEAD_DIM + 1] + extra[t]) for t in ts}

    def place(t):
        if t < n_w:
            a, j = chains[t]
            return [(slice(a * BLOCK, (a + 1) * BLOCK), hsl[j * grp + g], slice(g * BLOCK, (g + 1) * BLOCK))
                    for g in range(grp)]
        h = t - n_w
        return [(slice(0, ATT_TILE), slice(SWA_WIDTH + h * HEAD_DIM, SWA_WIDTH + (h + 1) * HEAD_DIM), slice(0, ATT_TILE))]

    return kc_n, n_w, n_w + N_MEM_HEADS, scores, finish, place


FF_CHUNK = 1024


def _ffn_chunk(h2, w1_ref, w2_ref, c):
    u = jnp.dot(h2, w1_ref[:, c * FF_CHUNK:(c + 1) * FF_CHUNK], preferred_element_type=F32)
    u = jnp.square(jnp.maximum(u, 0.0)).astype(BF16)
    return jnp.dot(u, w2_ref[c * FF_CHUNK:(c + 1) * FF_CHUNK, :], preferred_element_type=F32)


def _mix_rows(x, yr, ys, ym, wo_ref, g2_ref):
    x1 = (x
          + jnp.dot(yr.astype(BF16), wo_ref[0:RWKV_WIDTH, :], preferred_element_type=F32)
          + jnp.dot(ys.astype(BF16), wo_ref[RWKV_WIDTH:RWKV_WIDTH + SWA_WIDTH, :], preferred_element_type=F32)
          + jnp.dot(ym.astype(BF16), wo_ref[RWKV_WIDTH + SWA_WIDTH:, :], preferred_element_type=F32))
    return x1, _rms(x1, g2_ref[...]).astype(BF16)


def _attn_ffn_kernel(tiles_per_seq, qs_ref, qm_ref, kc_ref, kp_ref, vc_ref, vp_ref, mk_ref, mv_ref, bkt_ref, relb_ref,
                     sink_ref, sqg_ref, skg_ref, mqg_ref, x_ref, yr_ref, xs_ref, yrs_ref, yss_ref, yms_ref,
                     wo_ref, g2_ref, w1_ref, w2_ref, o_ref, os_ref, kn_ref, bias_scr, mk_scr, mv_scr, stage_scr):
    s = pl.program_id(0)
    nt = pl.num_programs(0) - 1
    tile = jnp.minimum(s, nt - 1)
    first = lax.rem(tile, tiles_per_seq) == 0
    wslot = lax.rem(s, 2)
    rslot = 1 - wslot

    @pl.when(s == 0)
    def _():
        stage_scr[...] = jnp.zeros_like(stage_scr)

    @pl.when(jnp.logical_and(first, s < nt))
    def _():
        _attn_init(bkt_ref, relb_ref, mk_ref, mv_ref, bias_scr, mk_scr, mv_scr)

    kc_n, n_w, n_tasks, scores, finish, place = _attn_tile(
        first, qs_ref[0], qm_ref[0], kc_ref[0], kp_ref[0], vc_ref[0], vp_ref[0],
        sqg_ref[...], skg_ref[...], mqg_ref[...], sink_ref, bias_scr, mk_scr, mv_scr)
    kn_ref[0] = kc_n[ATT_TILE - BLOCK:]

    staged = stage_scr[rslot]
    x1, h2 = _mix_rows(x_ref[...], yr_ref[...], staged[:, :SWA_WIDTH], staged[:, SWA_WIDTH:], wo_ref, g2_ref)
    sc = {t: scores(t) for t in range(n_tasks)}
    ff = _ffn_chunk(h2, w1_ref, w2_ref, 0)
    outs = finish(range(n_w), sc)
    ff = ff + _ffn_chunk(h2, w1_ref, w2_ref, 1)
    outs.update(finish(range(n_w, n_tasks), sc))
    for c in range(2, D_FF // FF_CHUNK):
        ff = ff + _ffn_chunk(h2, w1_ref, w2_ref, c)
    o_ref[...] = x1 + ff
    for t in range(n_tasks):
        for rows, lanes, src in place(t):
            stage_scr[wslot, rows, lanes] = outs[t][src]

    @pl.when(s == nt)
    def _():
        xs1, hs2 = _mix_rows(xs_ref[...], yrs_ref[...], yss_ref[...], yms_ref[...], wo_ref, g2_ref)
        ffs = None
        for c in range(D_FF // FF_CHUNK):
            d = _ffn_chunk(hs2, w1_ref, w2_ref, c)
            ffs = d if ffs is None else ffs + d
        os_ref[...] = xs1 + ffs


def _attn_ffn(proj, mk, mv, rel_bias, sinks, sqg, skg, mqg, x2d, yr, sample, wo, g2, w1, w2):
    B, T, _ = proj.shape
    tps = T // ATT_TILE
    nt = B * tps
    ns = sample[0].shape[0]
    bkt = jnp.asarray(_prompt_bucket_table())
    smem = pl.BlockSpec(memory_space=pltpu.SMEM)
    kblk, vblk = COL_SK // SWA_KV_WIDTH, COL_SV // SWA_KV_WIDTH
    att = lambda s: jnp.minimum(s, nt - 1)
    cur = lambda s, c: (att(s) // tps, att(s) % tps, c)
    prev = lambda s, c: (att(s) // tps, jnp.maximum((ATT_TILE // BLOCK) * (att(s) % tps) - 1, 0), c)
    ffn = lambda s: (jnp.maximum(s - 1, 0), 0)
    memb = pl.BlockSpec((1, N_MEM, MEM_WIDTH), lambda s: (att(s) // tps, 0, 0))
    const = lambda shape: pl.BlockSpec(shape, lambda s: (0, 0), pipeline_mode=pl.Buffered(1))
    widths = (D_MODEL, RWKV_WIDTH, SWA_WIDTH, MEM_WIDTH)
    return pl.pallas_call(
        functools.partial(_attn_ffn_kernel, tps),
        grid=(nt + 1,),
        in_specs=[pl.BlockSpec((1, ATT_TILE, SWA_WIDTH), lambda s: cur(s, COL_SQ // SWA_WIDTH)),
                  pl.BlockSpec((1, ATT_TILE, MEM_WIDTH), lambda s: cur(s, COL_MQ // MEM_WIDTH)),
                  pl.BlockSpec((1, ATT_TILE, SWA_KV_WIDTH), lambda s: cur(s, kblk)),
                  pl.BlockSpec((1, BLOCK, SWA_KV_WIDTH), lambda s: prev(s, kblk)),
                  pl.BlockSpec((1, ATT_TILE, SWA_KV_WIDTH), lambda s: cur(s, vblk)),
                  pl.BlockSpec((1, BLOCK, SWA_KV_WIDTH), lambda s: prev(s, vblk)),
                  memb, memb,
                  _full((BLOCK, 2 * BLOCK)), smem, smem,
                  _full((1, HEAD_DIM)), _full((1, HEAD_DIM)), _full((1, HEAD_DIM)),
                  pl.BlockSpec((ATT_TILE, D_MODEL), ffn), pl.BlockSpec((ATT_TILE, RWKV_WIDTH), ffn)]
                 + [_full((ns, w)) for w in widths]
                 + [const((D_MODEL, D_MODEL)), _full((1, D_MODEL)), const((D_MODEL, D_FF)), const((D_FF, D_MODEL))],
        out_specs=[pl.BlockSpec((ATT_TILE, D_MODEL), ffn), _full((ns, D_MODEL)),
                   pl.BlockSpec((1, BLOCK, SWA_KV_WIDTH), lambda s: (att(s) // tps, 0, 0))],
        out_shape=[jax.ShapeDtypeStruct((B * T, D_MODEL), F32), jax.ShapeDtypeStruct((ns, D_MODEL), F32),
                   jax.ShapeDtypeStruct((B, BLOCK, SWA_KV_WIDTH), F32)],
        scratch_shapes=[pltpu.VMEM((N_SWA_KV_HEADS, 2 * BLOCK, 2 * BLOCK), F32),
                        pltpu.VMEM((N_MEM_HEADS, N_MEM, HEAD_DIM), BF16),
                        pltpu.VMEM((N_MEM_HEADS, N_MEM, 2 * HEAD_DIM), BF16),
                        pltpu.VMEM((2, ATT_TILE, SWA_WIDTH + MEM_WIDTH), F32)],
        compiler_params=_params("arbitrary"),
        name="attn_ffn",
    )(proj, proj, proj, proj, proj, proj, mk, mv, bkt, rel_bias, sinks, sqg, skg, mqg, x2d, yr, *sample, wo, g2, w1, w2)


def _memory_kv_kernel(mem_ref, g_ref, w_ref, kg_ref, mk_ref, mv_ref):
    kv = _dot(_rms(mem_ref[0], g_ref[...]), w_ref[...])
    kg = kg_ref[...]
    for h in range(N_MEM_HEADS):
        sl = slice(h * HEAD_DIM, (h + 1) * HEAD_DIM)
        mk_ref[0, :, sl] = _rms(kv[:, sl], kg)
    mv_ref[0] = kv[:, MEM_WIDTH:]


def _memory_kv(mem, g, w, kg):
    B = mem.shape[0]
    blk = pl.BlockSpec((1, N_MEM, MEM_WIDTH), lambda b: (b, 0, 0))
    return pl.pallas_call(
        _memory_kv_kernel,
        grid=(B,),
        in_specs=[pl.BlockSpec((1, N_MEM, D_MODEL), lambda b: (b, 0, 0)),
                  _full((1, D_MODEL)), _full((D_MODEL, 2 * MEM_WIDTH)), _full((1, HEAD_DIM))],
        out_specs=[blk, blk],
        out_shape=[jax.ShapeDtypeStruct((B, N_MEM, MEM_WIDTH), F32)] * 2,
        compiler_params=_params("arbitrary"),
        name="memory_kv",
    )(mem, g, w, kg)


SEQ_TILE = 8


def _rms_heads(x, g, n_heads):
    ms = jnp.dot((x * x).astype(BF16), _head_blocks(n_heads * HEAD_DIM), preferred_element_type=F32)
    return x * lax.rsqrt(ms * (1.0 / HEAD_DIM) + NORM_EPS) * jnp.concatenate([g] * n_heads, axis=1)


def _decode_attn_kernel(p_ref, kbuf_ref, vbuf_ref, mk_ref, mv_ref, bkt_ref, relb_ref, sink_ref,
                        sqg_ref, skg_ref, mqg_ref, ys_ref, ym_ref, kout_ref, vout_ref, tab_scr):
    grp = N_SWA_HEADS // N_SWA_KV_HEADS

    @pl.when(pl.program_id(0) == 0)
    def _():
        hrow = lax.broadcasted_iota(jnp.int32, (8, WINDOW), 0)
        lane = lax.broadcasted_iota(jnp.int32, (8, WINDOW), 1)
        bias_w = jnp.zeros((8, WINDOW), F32)
        cols = jnp.zeros((8, WINDOW), F32)
        for h in range(N_SWA_HEADS):
            bias_w = jnp.where(hrow == h, _bias_from_buckets(bkt_ref[...], relb_ref, h, 0.0), bias_w)
            cols = jnp.where(jnp.logical_and(hrow == h, lane == 0), relb_ref[0, h], cols)
            cols = jnp.where(jnp.logical_and(hrow == h, lane == 1), sink_ref[0, h], cols)
        tab_scr[0] = bias_w
        tab_scr[1] = cols

    bias_w = tab_scr[0][:N_SWA_HEADS]
    bias_new = tab_scr[1][:N_SWA_HEADS, 0:1]
    sink = tab_scr[1][:N_SWA_HEADS, 1:2]
    rowi = lax.broadcasted_iota(jnp.int32, (SWA_KV_WIDTH, WINDOW), 0)
    lanei = lax.broadcasted_iota(jnp.int32, (SWA_KV_WIDTH, WINDOW), 1)
    eye = rowi == lanei
    NH = N_SWA_HEADS
    own = (lax.broadcasted_iota(jnp.int32, (NH, NH * HEAD_DIM), 1) // HEAD_DIM
           == lax.broadcasted_iota(jnp.int32, (NH, NH * HEAD_DIM), 0))

    p = p_ref[...]
    qn = _rms_heads(p[:, COL_SQ:COL_SQ + SWA_WIDTH], sqg_ref[...], N_SWA_HEADS) * ATTN_SCALE
    kn = _rms_heads(p[:, COL_SK:COL_SK + SWA_KV_WIDTH], skg_ref[...], N_SWA_KV_HEADS)
    vn = p[:, COL_SV:COL_SV + SWA_KV_WIDTH]
    qmn = _rms_heads(p[:, COL_MQ:COL_MQ + MEM_WIDTH], mqg_ref[...], N_MEM_HEADS) * ATTN_SCALE
    rep = lambda x: jnp.concatenate([x[:, j * HEAD_DIM:(j + 1) * HEAD_DIM] for j in range(N_SWA_KV_HEADS)
                                     for _ in range(grp)], axis=1)
    kn_rep, vn_rep = rep(kn), rep(vn)

    B = range(SEQ_TILE)
    dup = lambda c: jnp.concatenate([c[j * HEAD_DIM:(j + 1) * HEAD_DIM] for j in range(N_SWA_KV_HEADS)
                                     for _ in range(grp)], axis=0)
    qd = [jnp.where(own, qn[b:b + 1, :], 0.0) for b in B]
    qmd = [jnp.where(own, qmn[b:b + 1, :], 0.0) for b in B]
    kdup = [dup(kbuf_ref[b]) for b in B]
    vdup = [dup(vbuf_ref[b]) for b in B]
    s = [_dot(qd[b], kdup[b]) + bias_w for b in B]
    sm = [_dot(qmd[b], mk_ref[b]) for b in B]
    s_new = [jnp.sum(qd[b] * kn_rep[b:b + 1, :], -1, keepdims=True) + bias_new for b in B]
    m = [jnp.maximum(jnp.maximum(jnp.max(s[b], -1, keepdims=True), s_new[b]), sink) for b in B]
    e = [jnp.exp(s[b] - m[b]) for b in B]
    e_new = [jnp.exp(s_new[b] - m[b]) for b in B]
    den = [jnp.sum(e[b], -1, keepdims=True) + e_new[b] + jnp.exp(sink - m[b]) for b in B]
    em = [jnp.exp(sm[b] - jnp.max(sm[b], -1, keepdims=True)) for b in B]
    ov = [_dot_nt(e[b], vdup[b]) for b in B]
    omf = [_dot_nt(em[b], mv_ref[b]) for b in B]
    ys_rows = [jnp.sum(jnp.where(own, (ov[b] + e_new[b] * vn_rep[b:b + 1, :]) / den[b], 0.0), 0, keepdims=True)
               for b in B]
    ym_rows = [jnp.sum(jnp.where(own, omf[b] / jnp.sum(em[b], -1, keepdims=True), 0.0), 0, keepdims=True)
               for b in B]
    ys_ref[...] = jnp.concatenate(ys_rows, axis=0)
    ym_ref[...] = jnp.concatenate(ym_rows, axis=0)
    for b in B:
        kn_col = jnp.sum(jnp.where(eye, kn[b:b + 1, :], 0.0), -1, keepdims=True)
        vn_col = jnp.sum(jnp.where(eye, vn[b:b + 1, :], 0.0), -1, keepdims=True)
        kout_ref[b] = jnp.where(lanei == WINDOW - 1, kn_col, pltpu.roll(kbuf_ref[b], WINDOW - 1, axis=1))
        vout_ref[b] = jnp.where(lanei == WINDOW - 1, vn_col, pltpu.roll(vbuf_ref[b], WINDOW - 1, axis=1))


def _decode_attn(proj, kbuf, vbuf, mk, mv, rel_bias, sinks, sqg, skg, mqg):
    n = proj.shape[0]
    bkt = jnp.asarray(_decode_bucket_table())
    smem = pl.BlockSpec(memory_space=pltpu.SMEM)
    win = pl.BlockSpec((SEQ_TILE, WINDOW, SWA_KV_WIDTH), lambda i: (i, 0, 0))
    memb = pl.BlockSpec((SEQ_TILE, N_MEM, MEM_WIDTH), lambda i: (i, 0, 0))
    return pl.pallas_call(
        _decode_attn_kernel,
        grid=(n // SEQ_TILE,),
        in_specs=[pl.BlockSpec((SEQ_TILE, IN_PROJ), lambda i: (i, 0)), win, win, memb, memb,
                  _full((8, WINDOW)), smem, smem, _full((1, HEAD_DIM)), _full((1, HEAD_DIM)), _full((1, HEAD_DIM))],
        out_specs=[pl.BlockSpec((SEQ_TILE, SWA_WIDTH), lambda i: (i, 0)),
                   pl.BlockSpec((SEQ_TILE, MEM_WIDTH), lambda i: (i, 0)), win, win],
        out_shape=[jax.ShapeDtypeStruct((n, SWA_WIDTH), F32), jax.ShapeDtypeStruct((n, MEM_WIDTH), F32),
                   jax.ShapeDtypeStruct(kbuf.shape, F32), jax.ShapeDtypeStruct(vbuf.shape, F32)],
        scratch_shapes=[pltpu.VMEM((2, 8, WINDOW), F32)],
        compiler_params=_params("arbitrary"),
        name="decode_attn",
    )(proj, kbuf, vbuf, mk, mv, bkt, rel_bias, sinks, sqg, skg, mqg)


def kernel(x_prompt, x_sample, state_rwkv, state_shift, cache_swa_k, cache_swa_v, cache_mem_k, cache_mem_v,
           mem_prompt, rel_bias, norm1_g, w_in, mu_shift, w0, w_up_w, a0, w_up_a, w_up_g, k_k, k_a, r_k,
           lnx_w, lnx_b, q_norm_swa, k_norm_swa, sinks, mem_norm_g, w_mem_kv, q_norm_mem, k_norm_mem,
           w_out, norm2_g, w_ff1, w_ff2):
    B, T, _ = x_prompt.shape
    Bd = x_sample.shape[0]
    l = 0
    rwkv_params = (mu_shift[l][None], w0[l][None], a0[l][None], k_k[l][None], k_a[l][None],
                   r_k[l].reshape(1, RWKV_WIDTH), lnx_w[l][None], lnx_b[l][None],
                   w_up_w[l], w_up_a[l], w_up_g[l])
    sqg, skg, mqg, mkg = q_norm_swa[l][None], k_norm_swa[l][None], q_norm_mem[l][None], k_norm_mem[l][None]
    g1, g2 = norm1_g[l][None], norm2_g[l][None]

    xp = x_prompt.reshape(B * T, D_MODEL)
    xs = x_sample.reshape(Bd, D_MODEL)
    proj_p, proj_s = _in_proj(xp, xs, g1, w_in[l], IN_PROJ_TILE)
    proj_p = proj_p.reshape(B, T, IN_PROJ)

    mk, mv = _memory_kv(mem_prompt, mem_norm_g[l][None], w_mem_kv[l], mkg)
    yr_p, s_p, (w_out_b, w1_b, w2_b) = _rwkv_prompt(proj_p, *rwkv_params, (w_out[l], w_ff1[l], w_ff2[l]))
    shift_p = proj_p[:, T - 1, :RWKV_PROJ]
    vb_p = proj_p[:, T - WINDOW:, COL_SV:COL_SV + SWA_KV_WIDTH]

    yr_s, st_s = _rwkv_step(proj_s[:, :RWKV_PROJ], state_shift[l], jnp.transpose(state_rwkv[l], (1, 2, 3, 0)),
                            *rwkv_params)
    s_s = jnp.transpose(st_s, (3, 0, 1, 2))
    fmajor = lambda c: jnp.transpose(c, (0, 2, 3, 1)).reshape(Bd, c.shape[2] * HEAD_DIM, c.shape[1])
    ys_s, ym_s, kb_s, vb_s = _decode_attn(
        proj_s, fmajor(cache_swa_k[l]), fmajor(cache_swa_v[l]), fmajor(cache_mem_k[l]), fmajor(cache_mem_v[l]),
        rel_bias, sinks[l][None], sqg, skg, mqg)
    pmajor = lambda c: jnp.transpose(c.reshape(Bd, N_SWA_KV_HEADS, HEAD_DIM, WINDOW), (0, 3, 1, 2))[None]

    y_p, y_s, kn_p = _attn_ffn(proj_p, mk, mv, rel_bias, sinks[l][None], sqg, skg, mqg,
                               xp, yr_p.reshape(B * T, RWKV_WIDTH), (xs, yr_s, ys_s, ym_s), w_out_b, g2, w1_b, w2_b)
    y_p = y_p.reshape(B, T, D_MODEL)
    y_s = y_s.reshape(Bd, 1, D_MODEL)

    return (y_p, y_s,
            s_p[None], shift_p[None],
            kn_p.reshape(1, B, WINDOW, N_SWA_KV_HEADS, HEAD_DIM),
            vb_p.reshape(1, B, WINDOW, N_SWA_KV_HEADS, HEAD_DIM),
            mk.reshape(1, B, N_MEM, N_MEM_HEADS, HEAD_DIM), mv.reshape(1, B, N_MEM, N_MEM_HEADS, HEAD_DIM),
            s_s[None], proj_s[:, :RWKV_PROJ][None],
            pmajor(kb_s), pmajor(vb_s))
```

```python
import functools
import math

import numpy as np
import jax
import jax.numpy as jnp
from jax import lax
from jax.experimental import pallas as pl
from jax.experimental.pallas import tpu as pltpu

F32 = jnp.float32
BF16 = jnp.bfloat16

D_MODEL = 1024
HEAD_DIM = 64
RWKV_WIDTH = 512
N_RWKV_HEADS = 8
SWA_WIDTH = 256
N_SWA_HEADS = 4
N_SWA_KV_HEADS = 2
SWA_KV_WIDTH = 128
MEM_WIDTH = 256
N_MEM_HEADS = 4
N_MEM = 256
WINDOW = 128
BLOCK = 128
N_BUCKETS = 32
MAX_DISTANCE = 128
DECAY_LORA = 64
AAA_LORA = 64
GATE_LORA = 128
RWKV_PROJ = 3 * RWKV_WIDTH + DECAY_LORA + AAA_LORA + GATE_LORA
SWA_PROJ = SWA_WIDTH + 2 * SWA_KV_WIDTH
IN_PROJ = RWKV_PROJ + SWA_PROJ + MEM_WIDTH
D_FF = 4 * D_MODEL
NORM_EPS = 1e-6
LNX_EPS = 64e-5
ATTN_SCALE = HEAD_DIM ** -0.5
EXP_M05 = math.exp(-0.5)
LOG2E = math.log2(math.e)
NEG = -1e30

COL_R, COL_K, COL_V = 0, RWKV_WIDTH, 2 * RWKV_WIDTH
COL_WD = 3 * RWKV_WIDTH
COL_AD = COL_WD + DECAY_LORA
COL_GD = COL_AD + AAA_LORA
COL_SQ = RWKV_PROJ
COL_SK = COL_SQ + SWA_WIDTH
COL_SV = COL_SK + SWA_KV_WIDTH
COL_MQ = RWKV_PROJ + SWA_PROJ

CHUNK = 64
VMEM_LIMIT = 56 * 1024 * 1024


def _dot(a, b):
    return jnp.dot(a.astype(BF16), b.astype(BF16), preferred_element_type=F32)


def _dot_nt(a, b):
    return lax.dot_general(a.astype(BF16), b.astype(BF16), (((1,), (1,)), ((), ())),
                           preferred_element_type=F32)


def _dot_tn(a, b):
    return lax.dot_general(a.astype(BF16), b.astype(BF16), (((0,), (0,)), ((), ())),
                           preferred_element_type=F32)


def _rms(x, g):
    return x * lax.rsqrt(jnp.mean(x * x, -1, keepdims=True) + NORM_EPS) * g


def _params(*sem):
    return pltpu.CompilerParams(dimension_semantics=sem, vmem_limit_bytes=VMEM_LIMIT)


def _full(shape):
    n = len(shape)
    return pl.BlockSpec(shape, lambda *_: (0,) * n)


def _head_blocks(width):
    bi = lax.broadcasted_iota(jnp.int32, (width, width), 0) // HEAD_DIM
    bj = lax.broadcasted_iota(jnp.int32, (width, width), 1) // HEAD_DIM
    return jnp.where(bi == bj, 1.0, 0.0).astype(BF16)


IN_PROJ_TILE = 1024
IN_PROJ_SUB = 256


def _in_proj_kernel(x_ref, xs_ref, g_ref, w_ref, wo_ref, w1_ref, w2_ref, o_ref, os_ref, wob_ref, w1b_ref, w2b_ref,
                    wb_scr):
    @pl.when(pl.program_id(0) == 0)
    def _():
        wb_scr[...] = w_ref[...].astype(BF16)

    wob_ref[...] = wo_ref[...].astype(BF16)
    w1b_ref[...] = w1_ref[...].astype(BF16)
    w2b_ref[...] = w2_ref[...].astype(BF16)

    tm = x_ref.shape[0]
    for j in range(tm // IN_PROJ_SUB):
        rows = slice(j * IN_PROJ_SUB, (j + 1) * IN_PROJ_SUB)
        h = _rms(x_ref[rows, :], g_ref[...])
        o_ref[rows, :] = jnp.dot(h.astype(BF16), wb_scr[...], preferred_element_type=F32)

    @pl.when(pl.program_id(0) == pl.num_programs(0) - 1)
    def _():
        hs = _rms(xs_ref[...], g_ref[...])
        os_ref[...] = jnp.dot(hs.astype(BF16), wb_scr[...], preferred_element_type=F32)


def _in_proj(x2d, xs2d, g, w, later_weights, tm):
    n, ns = x2d.shape[0], xs2d.shape[0]
    steps = n // tm
    slab = lambda m: pl.BlockSpec((m.shape[0] // steps, m.shape[1]), lambda i: (i, 0))
    proj_p, proj_s, *later_bf16 = pl.pallas_call(
        _in_proj_kernel,
        grid=(steps,),
        in_specs=[pl.BlockSpec((tm, D_MODEL), lambda i: (i, 0)),
                  _full((ns, D_MODEL)),
                  _full((1, D_MODEL)),
                  pl.BlockSpec((D_MODEL, IN_PROJ), lambda i: (0, 0), pipeline_mode=pl.Buffered(1))]
                 + [slab(m) for m in later_weights],
        out_specs=[pl.BlockSpec((tm, IN_PROJ), lambda i: (i, 0)), _full((ns, IN_PROJ))]
                  + [slab(m) for m in later_weights],
        out_shape=[jax.ShapeDtypeStruct((n, IN_PROJ), F32), jax.ShapeDtypeStruct((ns, IN_PROJ), F32)]
                  + [jax.ShapeDtypeStruct(m.shape, BF16) for m in later_weights],
        scratch_shapes=[pltpu.VMEM((D_MODEL, IN_PROJ), BF16)],
        compiler_params=_params("arbitrary"),
        name="in_proj",
    )(x2d, xs2d, g, w, *later_weights)
    return proj_p, proj_s, later_bf16


def _rwkv_features(xs, w0, a0, k_k, k_a, wupw, wupa, wupg):
    r = xs[:, COL_R:COL_R + RWKV_WIDTH]
    k = xs[:, COL_K:COL_K + RWKV_WIDTH]
    v = xs[:, COL_V:COL_V + RWKV_WIDTH]
    wd = xs[:, COL_WD:COL_WD + DECAY_LORA]
    ad = xs[:, COL_AD:COL_AD + AAA_LORA]
    gd = xs[:, COL_GD:COL_GD + GATE_LORA]
    logw = -jax.nn.sigmoid(w0 + _dot(jnp.tanh(wd), wupw)) * EXP_M05
    a_sig = jax.nn.sigmoid(a0 + _dot(ad, wupa))
    gate = _dot(jax.nn.sigmoid(gd), wupg)
    kk = k * k_k
    k2 = k * (1.0 + (a_sig - 1.0) * k_a)
    return r, k2, v, kk, a_sig, logw, gate


def _seg_sum(x, blk):
    xb = x.astype(BF16)
    half = RWKV_WIDTH // 2
    return jnp.concatenate([jnp.dot(xb[:, :half], blk, preferred_element_type=F32),
                            jnp.dot(xb[:, half:], blk, preferred_element_type=F32)], axis=1)


def _group_norm_out(y, bonus, gate, lnx_w, lnx_b, blk):
    inv_d = 1.0 / HEAD_DIM
    m = _seg_sum(y, blk) * inv_d
    d = y - m
    var = _seg_sum(d * d, blk) * inv_d
    yn = d * lax.rsqrt(var + LNX_EPS) * lnx_w + lnx_b
    return (yn + bonus) * gate


RWKV_TILE = 4 * CHUNK
PAIR = 2 * HEAD_DIM
N_PAIRS = N_RWKV_HEADS // 2


def _rwkv_prompt_kernel(p_ref, mu_ref, w0_ref, a0_ref, kk_ref, ka_ref, rk_ref, lnw_ref, lnb_ref,
                        wupw_ref, wupa_ref, wupg_ref, y_ref, sout_ref, s_scr, prev_scr):
    C, TT, D = CHUNK, RWKV_TILE, HEAD_DIM
    NC = TT // C
    t = pl.program_id(1)

    @pl.when(t == 0)
    def _():
        s_scr[...] = jnp.zeros_like(s_scr)
        prev_scr[...] = jnp.zeros_like(prev_scr)

    p = p_ref[0]
    row = lax.broadcasted_iota(jnp.int32, p.shape, 0)
    prev = jnp.where(row == 0, prev_scr[...], pltpu.roll(p, 1, axis=0))
    prev_scr[...] = p[TT - 1:TT, :]
    xs = p + (prev - p) * mu_ref[...]
    r, k2, v, kk, a_sig, logw, gate = _rwkv_features(
        xs, w0_ref[...], a0_ref[...], kk_ref[...], ka_ref[...], wupw_ref[...], wupa_ref[...], wupg_ref[...])

    blk = _head_blocks(RWKV_WIDTH // 2)
    kkn = kk / jnp.maximum(jnp.sqrt(_seg_sum(kk * kk, blk)), 1e-12)
    bb = kkn * a_sig

    ri = lax.broadcasted_iota(jnp.int32, (TT, TT), 0)
    ci = lax.broadcasted_iota(jnp.int32, (TT, TT), 1)
    tri = jnp.where(jnp.logical_and(ri >= ci, ri // C == ci // C), 1.0, 0.0).astype(BF16)
    lw2 = logw * LOG2E
    l1 = lw2.astype(BF16)
    l2 = (lw2 - l1.astype(F32)).astype(BF16)
    cum = jnp.dot(tri, l1, preferred_element_type=F32) + jnp.dot(tri, l2, preferred_element_type=F32)
    c_last = jnp.concatenate([jnp.broadcast_to(cum[(c + 1) * C - 1:(c + 1) * C, :], (C, RWKV_WIDTH))
                              for c in range(NC)], axis=0)
    e_pos = jnp.exp2(cum)
    e_neg = jnp.exp2(-cum)
    e_prev = jnp.exp2(cum - lw2)
    e_last = jnp.exp2(c_last - cum)

    lo_full = (lax.broadcasted_iota(jnp.int32, (TT, RWKV_WIDTH), 1) % PAIR) < D
    at_f = -kkn * e_prev
    rt_f = r * e_pos
    at_lo = jnp.where(lo_full, at_f, 0.0).astype(BF16)
    at_hi = jnp.where(lo_full, 0.0, at_f).astype(BF16)
    rt_lo = jnp.where(lo_full, rt_f, 0.0).astype(BF16)
    rt_hi = jnp.where(lo_full, 0.0, rt_f).astype(BF16)
    bt_b = (bb * e_neg).astype(BF16)
    kt_b = (k2 * e_neg).astype(BF16)
    bh_b = (bb * e_last).astype(BF16)
    kh_b = (k2 * e_last).astype(BF16)
    v_b = v.astype(BF16)

    r2 = lax.broadcasted_iota(jnp.int32, (C, PAIR), 0)
    c2 = lax.broadcasted_iota(jnp.int32, (C, PAIR), 1)
    lo = c2 < D
    c2m = jnp.where(lo, c2, c2 - C)
    mask_a = jnp.logical_and(lo, c2 < r2)
    mask_ak = jnp.logical_and(jnp.logical_not(lo), c2m < r2)
    mask_r = c2m <= r2
    eye_hi = jnp.where(jnp.logical_and(jnp.logical_not(lo), c2m == r2), 1.0, 0.0)
    zeros_cp = jnp.zeros((C, PAIR), F32)
    qi = lax.broadcasted_iota(jnp.int32, (PAIR, PAIR), 0) // D
    qj = lax.broadcasted_iota(jnp.int32, (PAIR, PAIR), 1) // D
    diag = qi == qj

    PR = [(c, q) for c in range(NC) for q in range(N_PAIRS)]
    n = len(PR)
    E = range(2)
    win = lambda x, c, q: x[c * C:(c + 1) * C, q * PAIR:(q + 1) * PAIR]
    sc = [_dot_nt(jnp.concatenate([win(at_lo, c, q), win(at_hi, c, q), win(rt_lo, c, q), win(rt_hi, c, q)], axis=0),
                  jnp.concatenate([win(bt_b, c, q), win(kt_b, c, q)], axis=0)) for c, q in PR]
    vr = [pltpu.roll(win(v, c, q), D, axis=1) for c, q in PR]
    vvr = [jnp.concatenate([vr[i], vr[i]], axis=0).astype(BF16) for i in range(n)]
    m_ak = [[jnp.where(mask_ak, sc[i][e * C:(e + 1) * C], 0.0) for e in E] for i in range(n)]
    m_r = [[jnp.where(mask_r, sc[i][(2 + e) * C:(3 + e) * C], 0.0) for e in E] for i in range(n)]

    zf = [[_dot(m_ak[i][e], vvr[i]) for e in E] for i in range(n)]
    W = [[jnp.where(mask_a, sc[i][e * C:(e + 1) * C], eye_hi) for e in E] for i in range(n)]
    for k in range(int(math.log2(C))):
        Wb = [[W[i][e].astype(BF16) for e in E] for i in range(n)]
        AW = [[jnp.dot(Wb[i][e][:, :C], Wb[i][e], preferred_element_type=F32) for e in E] for i in range(n)]
        W = [[jnp.where(lo, 0.0, W[i][e]) + AW[i][e] for e in E] for i in range(n)]
    X = [[_dot(W[i][0], jnp.concatenate([zeros_cp, jnp.where(lo, win(at_f, c, q), zf[i][0])], axis=0)),
          _dot(W[i][1], jnp.concatenate([zeros_cp, jnp.where(lo, zf[i][1], win(at_f, c, q))], axis=0))]
         for i, (c, q) in enumerate(PR)]

    S = [s_scr[q] for q in range(N_PAIRS)]
    ys = []
    for c in range(NC):
        w_last = jnp.exp2(cum[(c + 1) * C - 1:(c + 1) * C, :])
        idx = [c * N_PAIRS + q for q in range(N_PAIRS)]
        mkg = [[_dot_tn(X[i][e], win(bh_b, c, q)) for e in E] for q, i in enumerate(idx)]
        vk = [_dot_tn(win(v_b, c, q), win(kh_b, c, q)) for q, i in enumerate(idx)]
        ry = [[_dot(m_r[i][0], jnp.concatenate([X[i][0], jnp.where(lo, 0.0, vr[i])], axis=0)),
               _dot(m_r[i][1], jnp.concatenate([X[i][1], jnp.where(lo, vr[i], 0.0)], axis=0))] for i in idx]
        rp = [win(rt_f, c, q) + jnp.where(lo, ry[q][0], ry[q][1]) for q in range(N_PAIRS)]
        y0 = [pltpu.roll(jnp.where(lo, ry[q][1], ry[q][0]), D, axis=1) for q in range(N_PAIRS)]
        mk = [jnp.where(diag, jnp.concatenate([mkg[q][0][:D], mkg[q][1][D:]], axis=0), 0.0) for q in range(N_PAIRS)]
        g = [jnp.where(diag, vk[q] + jnp.concatenate([mkg[q][0][D:], mkg[q][1][:D]], axis=0), 0.0)
             for q in range(N_PAIRS)]
        y = [_dot_nt(rp[q], S[q]) for q in range(N_PAIRS)]
        dS = [_dot(S[q], mk[q]) for q in range(N_PAIRS)]
        S = [S[q] * w_last[:, q * PAIR:(q + 1) * PAIR] + dS[q] + g[q] for q in range(N_PAIRS)]
        ys.append(jnp.concatenate([y[q] + y0[q] for q in range(N_PAIRS)], axis=1))
    for q in range(N_PAIRS):
        s_scr[q] = S[q]

    bonus = _seg_sum(r * k2 * rk_ref[...], blk) * v
    y_ref[0] = _group_norm_out(jnp.concatenate(ys, axis=0), bonus, gate, lnw_ref[...], lnb_ref[...], blk)

    @pl.when(t == pl.num_programs(1) - 1)
    def _():
        for q in range(N_PAIRS):
            sout_ref[0, 2 * q] = S[q][:D, :D]
            sout_ref[0, 2 * q + 1] = S[q][D:, D:]


def _rwkv_prompt(proj, mu, w0, a0, k_k, k_a, r_k, lnx_w, lnx_b, wupw, wupa, wupg):
    B, T, _ = proj.shape
    vec = lambda n: _full((1, n))
    return pl.pallas_call(
        _rwkv_prompt_kernel,
        grid=(B, T // RWKV_TILE),
        in_specs=[pl.BlockSpec((1, RWKV_TILE, RWKV_PROJ), lambda b, t: (b, t, 0)),
                  vec(RWKV_PROJ), vec(RWKV_WIDTH), vec(RWKV_WIDTH), vec(RWKV_WIDTH), vec(RWKV_WIDTH),
                  vec(RWKV_WIDTH), vec(RWKV_WIDTH), vec(RWKV_WIDTH),
                  _full((DECAY_LORA, RWKV_WIDTH)), _full((AAA_LORA, RWKV_WIDTH)), _full((GATE_LORA, RWKV_WIDTH))],
        out_specs=[pl.BlockSpec((1, RWKV_TILE, RWKV_WIDTH), lambda b, t: (b, t, 0)),
                   pl.BlockSpec((1, N_RWKV_HEADS, HEAD_DIM, HEAD_DIM), lambda b, t: (b, 0, 0, 0))],
        out_shape=[jax.ShapeDtypeStruct((B, T, RWKV_WIDTH), F32),
                   jax.ShapeDtypeStruct((B, N_RWKV_HEADS, HEAD_DIM, HEAD_DIM), F32)],
        scratch_shapes=[pltpu.VMEM((N_PAIRS, PAIR, PAIR), F32),
                        pltpu.VMEM((1, RWKV_PROJ), F32)],
        compiler_params=_params("arbitrary", "arbitrary"),
        name="rwkv_prompt",
    )(proj, mu, w0, a0, k_k, k_a, r_k, lnx_w, lnx_b, wupw, wupa, wupg)


def _rwkv_step_feat_kernel(p_ref, sh_ref, mu_ref, w0_ref, a0_ref, kk_ref, ka_ref, rk_ref,
                           wupw_ref, wupa_ref, wupg_ref, vecs_ref, bonus_ref, gate_ref):
    p = p_ref[...]
    xs = p + (sh_ref[...] - p) * mu_ref[...]
    r, k2, v, kk, a_sig, logw, gate = _rwkv_features(
        xs, w0_ref[...], a0_ref[...], kk_ref[...], ka_ref[...], wupw_ref[...], wupa_ref[...], wupg_ref[...])
    blk = _head_blocks(RWKV_WIDTH // 2)
    kkn = kk / jnp.maximum(jnp.sqrt(_seg_sum(kk * kk, blk)), 1e-12)
    for i, x in enumerate((-kkn, kkn * a_sig, jnp.exp(logw), k2, r, v)):
        vecs_ref[i] = x.T
    bonus_ref[...] = _seg_sum(r * k2 * rk_ref[...], blk) * v
    gate_ref[...] = gate


def _rwkv_step_state_kernel(s_ref, vecs_ref, sout_ref, y_ref):
    S = s_ref[0]
    a, b, w, k, r, v = (vecs_ref[i] for i in range(6))
    sa = jnp.sum(S * a[None], axis=1)
    S = S * w[None] + sa[:, None, :] * b[None] + v[:, None, :] * k[None]
    sout_ref[0] = S
    y_ref[...] = jnp.sum(S * r[None], axis=1)


def _rwkv_step_out_kernel(yt_ref, bonus_ref, gate_ref, lnw_ref, lnb_ref, o_ref):
    o_ref[...] = _group_norm_out(yt_ref[...].T, bonus_ref[...], gate_ref[...], lnw_ref[...], lnb_ref[...],
                                 _head_blocks(RWKV_WIDTH // 2))


def _rwkv_step(proj, shift, state_t, mu, w0, a0, k_k, k_a, r_k, lnx_w, lnx_b, wupw, wupa, wupg):
    n = proj.shape[0]
    vec = lambda m: _full((1, m))
    rows = _full((n, RWKV_PROJ))
    wide = _full((n, RWKV_WIDTH))
    vecs, bonus, gate = pl.pallas_call(
        _rwkv_step_feat_kernel,
        grid=(1,),
        in_specs=[rows, rows, vec(RWKV_PROJ), vec(RWKV_WIDTH), vec(RWKV_WIDTH), vec(RWKV_WIDTH), vec(RWKV_WIDTH),
                  vec(RWKV_WIDTH),
                  _full((DECAY_LORA, RWKV_WIDTH)), _full((AAA_LORA, RWKV_WIDTH)), _full((GATE_LORA, RWKV_WIDTH))],
        out_specs=[_full((6, RWKV_WIDTH, n)), wide, wide],
        out_shape=[jax.ShapeDtypeStruct((6, RWKV_WIDTH, n), F32),
                   jax.ShapeDtypeStruct((n, RWKV_WIDTH), F32), jax.ShapeDtypeStruct((n, RWKV_WIDTH), F32)],
        compiler_params=_params("arbitrary"),
        name="rwkv_step_feat",
    )(proj, shift, mu, w0, a0, k_k, k_a, r_k, wupw, wupa, wupg)
    st_spec = pl.BlockSpec((1, HEAD_DIM, HEAD_DIM, n), lambda h: (h, 0, 0, 0))
    state_new, yt = pl.pallas_call(
        _rwkv_step_state_kernel,
        grid=(N_RWKV_HEADS,),
        in_specs=[st_spec, pl.BlockSpec((6, HEAD_DIM, n), lambda h: (0, h, 0))],
        out_specs=[st_spec, pl.BlockSpec((HEAD_DIM, n), lambda h: (h, 0))],
        out_shape=[jax.ShapeDtypeStruct(state_t.shape, F32), jax.ShapeDtypeStruct((RWKV_WIDTH, n), F32)],
        compiler_params=_params("arbitrary"),
        name="rwkv_step_state",
    )(state_t, vecs)
    y = pl.pallas_call(
        _rwkv_step_out_kernel,
        grid=(1,),
        in_specs=[_full((RWKV_WIDTH, n)), wide, wide, vec(RWKV_WIDTH), vec(RWKV_WIDTH)],
        out_specs=wide,
        out_shape=jax.ShapeDtypeStruct((n, RWKV_WIDTH), F32),
        compiler_params=_params("arbitrary"),
        name="rwkv_step_out",
    )(yt, bonus, gate, lnx_w, lnx_b)
    return y, state_new


def _t5_bucket_np(dist):
    max_exact = N_BUCKETS // 2
    d = np.maximum(dist, 1).astype(np.float32)
    large = max_exact + (np.log(d / np.float32(max_exact)) / np.float32(math.log(MAX_DISTANCE / max_exact))
                         * np.float32(N_BUCKETS - max_exact)).astype(np.int32)
    large = np.minimum(large, N_BUCKETS - 1)
    return np.where(dist < max_exact, dist, large).astype(np.int32)


def _prompt_bucket_table():
    qi = np.arange(BLOCK)[:, None]
    kj = np.arange(2 * BLOCK)[None, :]
    dist = BLOCK + qi - kj
    valid = (dist >= 0) & (dist <= WINDOW)
    return np.where(valid, _t5_bucket_np(np.maximum(dist, 0)), -1).astype(np.int32)


def _decode_bucket_table():
    dist = WINDOW - np.arange(WINDOW)
    return np.broadcast_to(_t5_bucket_np(dist)[None, :], (8, WINDOW)).astype(np.int32).copy()


def _bias_from_buckets(bkt, relb_ref, h, init):
    acc = jnp.full(bkt.shape, init, F32)
    for b in range(N_BUCKETS):
        acc = jnp.where(bkt == b, relb_ref[b, h], acc)
    return acc


ATT_TILE = 4 * BLOCK


def _attn_init(bkt_ref, relb_ref, mk_ref, mv_ref, bias_scr, mk_scr, mv_scr):
    grp = N_SWA_HEADS // N_SWA_KV_HEADS
    ones = jnp.ones((N_MEM, HEAD_DIM), F32)
    bkt = bkt_ref[...]
    for j in range(N_SWA_KV_HEADS):
        for g in range(grp):
            bias_scr[j, g * BLOCK:(g + 1) * BLOCK, :] = _bias_from_buckets(bkt, relb_ref, j * grp + g, NEG) * LOG2E
    mk = mk_ref[0]
    mv = mv_ref[0]
    for h in range(N_MEM_HEADS):
        sl = slice(h * HEAD_DIM, (h + 1) * HEAD_DIM)
        mk_scr[h] = mk[:, sl].astype(BF16)
        mv_scr[h] = jnp.concatenate([mv[:, sl], ones], axis=1).astype(BF16)


def _attn_tile(first, qs, qm, kc, kp, vc, vp, sqg, skg, mqg, sink_ref, bias_scr, mk_scr, mv_scr):
    grp = N_SWA_HEADS // N_SWA_KV_HEADS
    ones = jnp.ones((2 * BLOCK, HEAD_DIM), F32)
    rowi = lax.broadcasted_iota(jnp.int32, (2 * BLOCK, 1), 0)
    col = lax.broadcasted_iota(jnp.int32, (2 * BLOCK, 2 * BLOCK), 1)
    pad_mask = jnp.logical_and(first, col < BLOCK)
    hsl = [slice(h * HEAD_DIM, (h + 1) * HEAD_DIM) for h in range(N_SWA_HEADS)]
    qs_n = _rms_heads(qs, sqg, N_SWA_HEADS) * (ATTN_SCALE * LOG2E)
    qm_n = _rms_heads(qm, mqg, N_MEM_HEADS) * (ATTN_SCALE * LOG2E)
    kc_n = _rms_heads(kc, skg, N_SWA_KV_HEADS)
    kp_n = _rms_heads(kp, skg, N_SWA_KV_HEADS)
    chains =[(a, j) for a in range(ATT_TILE // BLOCK) for j in range(N_SWA_KV_HEADS)]
    lhs, keys, vals, sinkcol = [], [], [], []
    for a, j in chains:
        rs = slice(a * BLOCK, (a + 1) * BLOCK)
        lhs.append(jnp.concatenate([qs_n[rs, hsl[j * grp + g]] for g in range(grp)], axis=0))
        if a == 0:
            keys.append(jnp.concatenate([kp_n[:, hsl[j]], kc_n[:BLOCK, hsl[j]]], axis=0))
            vv = jnp.concatenate([vp[:, hsl[j]], vc[:BLOCK, hsl[j]]], axis=0)
        else:
            ks = slice((a - 1) * BLOCK, (a + 1) * BLOCK)
            keys.append(kc_n[ks, hsl[j]])
            vv = vc[ks, hsl[j]]
        vals.append(jnp.concatenate([vv, ones], axis=1))
        sinkcol.append(jnp.where(rowi < BLOCK, sink_ref[0, j * grp], sink_ref[0, j * grp + 1]) * LOG2E)
    qmn = [qm_n[:, hsl[h]] for h in range(N_MEM_HEADS)]

    bias = [bias_scr[j] for j in range(N_SWA_KV_HEADS)]
    mem_k = [mk_scr[h] for h in range(N_MEM_HEADS)]
    mem_v = [mv_scr[h] for h in range(N_MEM_HEADS)]
    n_w = len(chains)

    def scores(t):
        return _dot_nt(lhs[t], keys[t]) if t < n_w else _dot_nt(qmn[t - n_w], mem_k[t - n_w])

    def finish(ts, s):
        e, extra = {}, {}
        for t in ts:
            if t < n_w:
                a, j = chains[t]
                st = s[t] + bias[j]
                if a == 0:
                    st = jnp.where(pad_mask, NEG, st)
                m = jnp.maximum(jnp.max(st, -1, keepdims=True), sinkcol[t])
                e[t], extra[t] = jnp.exp2(st - m), jnp.exp2(sinkcol[t] - m)
            else:
                e[t], extra[t] = jnp.exp2(s[t] - jnp.max(s[t], -1, keepdims=True)), 0.0
        o_full = {t: _dot(e[t], vals[t] if t < n_w else mem_v[t - n_w]) for t in ts}
        return {t: o_full[t][:, :HEAD_DIM] / (o_full[t][:, HEAD_DIM:HEAD_DIM + 1] + extra[t]) for t in ts}

    def place(t):
        if t < n_w:
            a, j = chains[t]
            return [(slice(a * BLOCK, (a + 1) * BLOCK), hsl[j * grp + g], slice(g * BLOCK, (g + 1) * BLOCK))
                    for g in range(grp)]
        h = t - n_w
        return [(slice(0, ATT_TILE), slice(SWA_WIDTH + h * HEAD_DIM, SWA_WIDTH + (h + 1) * HEAD_DIM), slice(0, ATT_TILE))]

    return kc_n, n_w, n_w + N_MEM_HEADS, scores, finish, place


FF_CHUNK = 1024


def _ffn_chunk(h2, w1_ref, w2_ref, c):
    u = jnp.dot(h2, w1_ref[:, c * FF_CHUNK:(c + 1) * FF_CHUNK], preferred_element_type=F32)
    u = jnp.square(jnp.maximum(u, 0.0)).astype(BF16)
    return jnp.dot(u, w2_ref[c * FF_CHUNK:(c + 1) * FF_CHUNK, :], preferred_element_type=F32)


def _mix_rows(x, yr, ys, ym, wo_ref, g2_ref):
    x1 = (x
          + jnp.dot(yr.astype(BF16), wo_ref[0:RWKV_WIDTH, :], preferred_element_type=F32)
          + jnp.dot(ys.astype(BF16), wo_ref[RWKV_WIDTH:RWKV_WIDTH + SWA_WIDTH, :], preferred_element_type=F32)
          + jnp.dot(ym.astype(BF16), wo_ref[RWKV_WIDTH + SWA_WIDTH:, :], preferred_element_type=F32))
    return x1, _rms(x1, g2_ref[...]).astype(BF16)


def _attn_ffn_kernel(tiles_per_seq, qs_ref, qm_ref, kc_ref, kp_ref, vc_ref, vp_ref, mk_ref, mv_ref, bkt_ref, relb_ref,
                     sink_ref, sqg_ref, skg_ref, mqg_ref, x_ref, yr_ref, xs_ref, yrs_ref, yss_ref, yms_ref,
                     wo_ref, g2_ref, w1_ref, w2_ref, o_ref, os_ref, kn_ref, bias_scr, mk_scr, mv_scr, stage_scr):
    s = pl.program_id(0)
    nt = pl.num_programs(0) - 1
    tile = jnp.minimum(s, nt - 1)
    first = lax.rem(tile, tiles_per_seq) == 0
    wslot = lax.rem(s, 2)
    rslot = 1 - wslot

    @pl.when(jnp.logical_and(first, s < nt))
    def _():
        _attn_init(bkt_ref, relb_ref, mk_ref, mv_ref, bias_scr, mk_scr, mv_scr)

    def step(mlp_rows, out_ref):
        kc_n, n_w, n_tasks, scores, finish, place = _attn_tile(
            first, qs_ref[0], qm_ref[0], kc_ref[0], kp_ref[0], vc_ref[0], vp_ref[0],
            sqg_ref[...], skg_ref[...], mqg_ref[...], sink_ref, bias_scr, mk_scr, mv_scr)
        kn_ref[0] = kc_n[ATT_TILE - BLOCK:]

        x1, h2 = _mix_rows(*mlp_rows(), wo_ref, g2_ref)
        sc = {t: scores(t) for t in range(n_tasks)}
        ff = _ffn_chunk(h2, w1_ref, w2_ref, 0)
        outs = finish(range(n_w), sc)
        ff = ff + _ffn_chunk(h2, w1_ref, w2_ref, 1)
        outs.update(finish(range(n_w, n_tasks), sc))
        for c in range(2, D_FF // FF_CHUNK):
            ff = ff + _ffn_chunk(h2, w1_ref, w2_ref, c)
        out_ref[...] = x1 + ff
        for t in range(n_tasks):
            for rows, lanes, src in place(t):
                stage_scr[wslot, rows, lanes] = outs[t][src]

    def sample_rows():
        return xs_ref[...], yrs_ref[...], yss_ref[...], yms_ref[...]

    def tile_rows():
        staged = stage_scr[rslot]
        return x_ref[...], yr_ref[...], staged[:, :SWA_WIDTH], staged[:, SWA_WIDTH:]

    pl.when(s == 0)(functools.partial(step, sample_rows, os_ref))
    pl.when(s > 0)(functools.partial(step, tile_rows, o_ref))


def _attn_ffn(proj, mk, mv, rel_bias, sinks, sqg, skg, mqg, x2d, yr, sample, wo, g2, w1, w2):
    B, T, _ = proj.shape
    tps = T // ATT_TILE
    nt = B * tps
    ns = sample[0].shape[0]
    bkt = jnp.asarray(_prompt_bucket_table())
    smem = pl.BlockSpec(memory_space=pltpu.SMEM)
    kblk, vblk = COL_SK // SWA_KV_WIDTH, COL_SV // SWA_KV_WIDTH
    att = lambda s: jnp.minimum(s, nt - 1)
    cur = lambda s, c: (att(s) // tps, att(s) % tps, c)
    prev = lambda s, c: (att(s) // tps, jnp.maximum((ATT_TILE // BLOCK) * (att(s) % tps) - 1, 0), c)
    ffn = lambda s: (jnp.maximum(s - 1, 0), 0)
    memb = pl.BlockSpec((1, N_MEM, MEM_WIDTH), lambda s: (att(s) // tps, 0, 0))
    const = lambda shape: pl.BlockSpec(shape, lambda s: (0, 0), pipeline_mode=pl.Buffered(1))
    widths = (D_MODEL, RWKV_WIDTH, SWA_WIDTH, MEM_WIDTH)
    return pl.pallas_call(
        functools.partial(_attn_ffn_kernel, tps),
        grid=(nt + 1,),
        in_specs=[pl.BlockSpec((1, ATT_TILE, SWA_WIDTH), lambda s: cur(s, COL_SQ // SWA_WIDTH)),
                  pl.BlockSpec((1, ATT_TILE, MEM_WIDTH), lambda s: cur(s, COL_MQ // MEM_WIDTH)),
                  pl.BlockSpec((1, ATT_TILE, SWA_KV_WIDTH), lambda s: cur(s, kblk)),
                  pl.BlockSpec((1, BLOCK, SWA_KV_WIDTH), lambda s: prev(s, kblk)),
                  pl.BlockSpec((1, ATT_TILE, SWA_KV_WIDTH), lambda s: cur(s, vblk)),
                  pl.BlockSpec((1, BLOCK, SWA_KV_WIDTH), lambda s: prev(s, vblk)),
                  memb, memb,
                  _full((BLOCK, 2 * BLOCK)), smem, smem,
                  _full((1, HEAD_DIM)), _full((1, HEAD_DIM)), _full((1, HEAD_DIM)),
                  pl.BlockSpec((ATT_TILE, D_MODEL), ffn), pl.BlockSpec((ATT_TILE, RWKV_WIDTH), ffn)]
                 + [_full((ns, w)) for w in widths]
                 + [const((D_MODEL, D_MODEL)), _full((1, D_MODEL)), const((D_MODEL, D_FF)), const((D_FF, D_MODEL))],
        out_specs=[pl.BlockSpec((ATT_TILE, D_MODEL), ffn), _full((ns, D_MODEL)),
                   pl.BlockSpec((1, BLOCK, SWA_KV_WIDTH), lambda s: (att(s) // tps, 0, 0))],
        out_shape=[jax.ShapeDtypeStruct((B * T, D_MODEL), F32), jax.ShapeDtypeStruct((ns, D_MODEL), F32),
                   jax.ShapeDtypeStruct((B, BLOCK, SWA_KV_WIDTH), F32)],
        scratch_shapes=[pltpu.VMEM((N_SWA_KV_HEADS, 2 * BLOCK, 2 * BLOCK), F32),
                        pltpu.VMEM((N_MEM_HEADS, N_MEM, HEAD_DIM), BF16),
                        pltpu.VMEM((N_MEM_HEADS, N_MEM, 2 * HEAD_DIM), BF16),
                        pltpu.VMEM((2, ATT_TILE, SWA_WIDTH + MEM_WIDTH), F32)],
        compiler_params=_params("arbitrary"),
        name="attn_ffn",
    )(proj, proj, proj, proj, proj, proj, mk, mv, bkt, rel_bias, sinks, sqg, skg, mqg, x2d, yr, *sample, wo, g2, w1, w2)


def _memory_kv_kernel(mem_ref, g_ref, w_ref, kg_ref, mk_ref, mv_ref):
    kv = _dot(_rms(mem_ref[0], g_ref[...]), w_ref[...])
    kg = kg_ref[...]
    for h in range(N_MEM_HEADS):
        sl = slice(h * HEAD_DIM, (h + 1) * HEAD_DIM)
        mk_ref[0, :, sl] = _rms(kv[:, sl], kg)
    mv_ref[0] = kv[:, MEM_WIDTH:]


def _memory_kv(mem, g, w, kg):
    B = mem.shape[0]
    blk = pl.BlockSpec((1, N_MEM, MEM_WIDTH), lambda b: (b, 0, 0))
    return pl.pallas_call(
        _memory_kv_kernel,
        grid=(B,),
        in_specs=[pl.BlockSpec((1, N_MEM, D_MODEL), lambda b: (b, 0, 0)),
                  _full((1, D_MODEL)), _full((D_MODEL, 2 * MEM_WIDTH)), _full((1, HEAD_DIM))],
        out_specs=[blk, blk],
        out_shape=[jax.ShapeDtypeStruct((B, N_MEM, MEM_WIDTH), F32)] * 2,
        compiler_params=_params("arbitrary"),
        name="memory_kv",
    )(mem, g, w, kg)


SEQ_TILE = 8


def _rms_heads(x, g, n_heads):
    ms = jnp.dot((x * x).astype(BF16), _head_blocks(n_heads * HEAD_DIM), preferred_element_type=F32)
    return x * lax.rsqrt(ms * (1.0 / HEAD_DIM) + NORM_EPS) * jnp.concatenate([g] * n_heads, axis=1)


def _decode_attn_kernel(p_ref, kbuf_ref, vbuf_ref, mk_ref, mv_ref, bkt_ref, relb_ref, sink_ref,
                        sqg_ref, skg_ref, mqg_ref, ys_ref, ym_ref, kout_ref, vout_ref, tab_scr):
    grp = N_SWA_HEADS // N_SWA_KV_HEADS

    @pl.when(pl.program_id(0) == 0)
    def _():
        hrow = lax.broadcasted_iota(jnp.int32, (8, WINDOW), 0)
        lane = lax.broadcasted_iota(jnp.int32, (8, WINDOW), 1)
        bias_w = jnp.zeros((8, WINDOW), F32)
        cols = jnp.zeros((8, WINDOW), F32)
        for h in range(N_SWA_HEADS):
            bias_w = jnp.where(hrow == h, _bias_from_buckets(bkt_ref[...], relb_ref, h, 0.0), bias_w)
            cols = jnp.where(jnp.logical_and(hrow == h, lane == 0), relb_ref[0, h], cols)
            cols = jnp.where(jnp.logical_and(hrow == h, lane == 1), sink_ref[0, h], cols)
        tab_scr[0] = bias_w
        tab_scr[1] = cols

    bias_w = tab_scr[0][:N_SWA_HEADS]
    bias_new = tab_scr[1][:N_SWA_HEADS, 0:1]
    sink = tab_scr[1][:N_SWA_HEADS, 1:2]
    rowi = lax.broadcasted_iota(jnp.int32, (SWA_KV_WIDTH, WINDOW), 0)
    lanei = lax.broadcasted_iota(jnp.int32, (SWA_KV_WIDTH, WINDOW), 1)
    eye = rowi == lanei
    NH = N_SWA_HEADS
    own = (lax.broadcasted_iota(jnp.int32, (NH, NH * HEAD_DIM), 1) // HEAD_DIM
           == lax.broadcasted_iota(jnp.int32, (NH, NH * HEAD_DIM), 0))

    p = p_ref[...]
    qn = _rms_heads(p[:, COL_SQ:COL_SQ + SWA_WIDTH], sqg_ref[...], N_SWA_HEADS) * ATTN_SCALE
    kn = _rms_heads(p[:, COL_SK:COL_SK + SWA_KV_WIDTH], skg_ref[...], N_SWA_KV_HEADS)
    vn = p[:, COL_SV:COL_SV + SWA_KV_WIDTH]
    qmn = _rms_heads(p[:, COL_MQ:COL_MQ + MEM_WIDTH], mqg_ref[...], N_MEM_HEADS) * ATTN_SCALE
    rep = lambda x: jnp.concatenate([x[:, j * HEAD_DIM:(j + 1) * HEAD_DIM] for j in range(N_SWA_KV_HEADS)
                                     for _ in range(grp)], axis=1)
    kn_rep, vn_rep = rep(kn), rep(vn)

    B = range(SEQ_TILE)
    dup = lambda c: jnp.concatenate([c[j * HEAD_DIM:(j + 1) * HEAD_DIM] for j in range(N_SWA_KV_HEADS)
                                     for _ in range(grp)], axis=0)
    qd = [jnp.where(own, qn[b:b + 1, :], 0.0) for b in B]
    qmd = [jnp.where(own, qmn[b:b + 1, :], 0.0) for b in B]
    kdup = [dup(kbuf_ref[b]) for b in B]
    vdup = [dup(vbuf_ref[b]) for b in B]
    s = [_dot(qd[b], kdup[b]) + bias_w for b in B]
    sm = [_dot(qmd[b], mk_ref[b]) for b in B]
    s_new = [jnp.sum(qd[b] * kn_rep[b:b + 1, :], -1, keepdims=True) + bias_new for b in B]
    m = [jnp.maximum(jnp.maximum(jnp.max(s[b], -1, keepdims=True), s_new[b]), sink) for b in B]
    e = [jnp.exp(s[b] - m[b]) for b in B]
    e_new = [jnp.exp(s_new[b] - m[b]) for b in B]
    den = [jnp.sum(e[b], -1, keepdims=True) + e_new[b] + jnp.exp(sink - m[b]) for b in B]
    em = [jnp.exp(sm[b] - jnp.max(sm[b], -1, keepdims=True)) for b in B]
    ov = [_dot_nt(e[b], vdup[b]) for b in B]
    omf = [_dot_nt(em[b], mv_ref[b]) for b in B]
    ys_rows = [jnp.sum(jnp.where(own, (ov[b] + e_new[b] * vn_rep[b:b + 1, :]) / den[b], 0.0), 0, keepdims=True)
               for b in B]
    ym_rows = [jnp.sum(jnp.where(own, omf[b] / jnp.sum(em[b], -1, keepdims=True), 0.0), 0, keepdims=True)
               for b in B]
    ys_ref[...] = jnp.concatenate(ys_rows, axis=0)
    ym_ref[...] = jnp.concatenate(ym_rows, axis=0)
    for b in B:
        kn_col = jnp.sum(jnp.where(eye, kn[b:b + 1, :], 0.0), -1, keepdims=True)
        vn_col = jnp.sum(jnp.where(eye, vn[b:b + 1, :], 0.0), -1, keepdims=True)
        kout_ref[b] = jnp.where(lanei == WINDOW - 1, kn_col, pltpu.roll(kbuf_ref[b], WINDOW - 1, axis=1))
        vout_ref[b] = jnp.where(lanei == WINDOW - 1, vn_col, pltpu.roll(vbuf_ref[b], WINDOW - 1, axis=1))


def _decode_attn(proj, kbuf, vbuf, mk, mv, rel_bias, sinks, sqg, skg, mqg):
    n = proj.shape[0]
    bkt = jnp.asarray(_decode_bucket_table())
    smem = pl.BlockSpec(memory_space=pltpu.SMEM)
    win = pl.BlockSpec((SEQ_TILE, WINDOW, SWA_KV_WIDTH), lambda i: (i, 0, 0))
    memb = pl.BlockSpec((SEQ_TILE, N_MEM, MEM_WIDTH), lambda i: (i, 0, 0))
    return pl.pallas_call(
        _decode_attn_kernel,
        grid=(n // SEQ_TILE,),
        in_specs=[pl.BlockSpec((SEQ_TILE, IN_PROJ), lambda i: (i, 0)), win, win, memb, memb,
                  _full((8, WINDOW)), smem, smem, _full((1, HEAD_DIM)), _full((1, HEAD_DIM)), _full((1, HEAD_DIM))],
        out_specs=[pl.BlockSpec((SEQ_TILE, SWA_WIDTH), lambda i: (i, 0)),
                   pl.BlockSpec((SEQ_TILE, MEM_WIDTH), lambda i: (i, 0)), win, win],
        out_shape=[jax.ShapeDtypeStruct((n, SWA_WIDTH), F32), jax.ShapeDtypeStruct((n, MEM_WIDTH), F32),
                   jax.ShapeDtypeStruct(kbuf.shape, F32), jax.ShapeDtypeStruct(vbuf.shape, F32)],
        scratch_shapes=[pltpu.VMEM((2, 8, WINDOW), F32)],
        compiler_params=_params("arbitrary"),
        name="decode_attn",
    )(proj, kbuf, vbuf, mk, mv, bkt, rel_bias, sinks, sqg, skg, mqg)


def kernel(x_prompt, x_sample, state_rwkv, state_shift, cache_swa_k, cache_swa_v, cache_mem_k, cache_mem_v,
           mem_prompt, rel_bias, norm1_g, w_in, mu_shift, w0, w_up_w, a0, w_up_a, w_up_g, k_k, k_a, r_k,
           lnx_w, lnx_b, q_norm_swa, k_norm_swa, sinks, mem_norm_g, w_mem_kv, q_norm_mem, k_norm_mem,
           w_out, norm2_g, w_ff1, w_ff2):
    B, T, _ = x_prompt.shape
    Bd = x_sample.shape[0]
    l = 0
    rwkv_params = (mu_shift[l][None], w0[l][None], a0[l][None], k_k[l][None], k_a[l][None],
                   r_k[l].reshape(1, RWKV_WIDTH), lnx_w[l][None], lnx_b[l][None],
                   w_up_w[l], w_up_a[l], w_up_g[l])
    sqg, skg, mqg, mkg = q_norm_swa[l][None], k_norm_swa[l][None], q_norm_mem[l][None], k_norm_mem[l][None]
    g1, g2 = norm1_g[l][None], norm2_g[l][None]

    xp = x_prompt.reshape(B * T, D_MODEL)
    xs = x_sample.reshape(Bd, D_MODEL)
    proj_p, proj_s, (w_out_b, w1_b, w2_b) = _in_proj(xp, xs, g1, w_in[l], (w_out[l], w_ff1[l], w_ff2[l]),
                                                     IN_PROJ_TILE)
    proj_p = proj_p.reshape(B, T, IN_PROJ)

    mk, mv = _memory_kv(mem_prompt, mem_norm_g[l][None], w_mem_kv[l], mkg)
    yr_p, s_p = _rwkv_prompt(proj_p, *rwkv_params)
    shift_p = proj_p[:, T - 1, :RWKV_PROJ]
    vb_p = proj_p[:, T - WINDOW:, COL_SV:COL_SV + SWA_KV_WIDTH]

    yr_s, st_s = _rwkv_step(proj_s[:, :RWKV_PROJ], state_shift[l], jnp.transpose(state_rwkv[l], (1, 2, 3, 0)),
                            *rwkv_params)
    s_s = jnp.transpose(st_s, (3, 0, 1, 2))
    fmajor = lambda c: jnp.transpose(c, (0, 2, 3, 1)).reshape(Bd, c.shape[2] * HEAD_DIM, c.shape[1])
    ys_s, ym_s, kb_s, vb_s = _decode_attn(
        proj_s, fmajor(cache_swa_k[l]), fmajor(cache_swa_v[l]), fmajor(cache_mem_k[l]), fmajor(cache_mem_v[l]),
        rel_bias, sinks[l][None], sqg, skg, mqg)
    pmajor = lambda c: jnp.transpose(c.reshape(Bd, N_SWA_KV_HEADS, HEAD_DIM, WINDOW), (0, 3, 1, 2))[None]

    y_p, y_s, kn_p = _attn_ffn(proj_p, mk, mv, rel_bias, sinks[l][None], sqg, skg, mqg,
                               xp, yr_p.reshape(B * T, RWKV_WIDTH), (xs, yr_s, ys_s, ym_s), w_out_b, g2, w1_b, w2_b)
    y_p = y_p.reshape(B, T, D_MODEL)
    y_s = y_s.reshape(Bd, 1, D_MODEL)

    return (y_p, y_s,
            s_p[None], shift_p[None],
            kn_p.reshape(1, B, WINDOW, N_SWA_KV_HEADS, HEAD_DIM),
            vb_p.reshape(1, B, WINDOW, N_SWA_KV_HEADS, HEAD_DIM),
            mk.reshape(1, B, N_MEM, N_MEM_HEADS, HEAD_DIM), mv.reshape(1, B, N_MEM, N_MEM_HEADS, HEAD_DIM),
            s_s[None], proj_s[:, :RWKV_PROJ][None],
            pmajor(kb_s), pmajor(vb_s))
```

```python
import functools
import math

import numpy as np
import jax
import jax.numpy as jnp
from jax import lax
from jax.experimental import pallas as pl
from jax.experimental.pallas import tpu as pltpu

F32 = jnp.float32
BF16 = jnp.bfloat16

D_MODEL = 1024
HEAD_DIM = 64
RWKV_WIDTH = 512
N_RWKV_HEADS = 8
SWA_WIDTH = 256
N_SWA_HEADS = 4
N_SWA_KV_HEADS = 2
SWA_KV_WIDTH = 128
MEM_WIDTH = 256
N_MEM_HEADS = 4
N_MEM = 256
WINDOW = 128
BLOCK = 128
N_BUCKETS = 32
MAX_DISTANCE = 128
DECAY_LORA = 64
AAA_LORA = 64
GATE_LORA = 128
RWKV_PROJ = 3 * RWKV_WIDTH + DECAY_LORA + AAA_LORA + GATE_LORA
SWA_PROJ = SWA_WIDTH + 2 * SWA_KV_WIDTH
IN_PROJ = RWKV_PROJ + SWA_PROJ + MEM_WIDTH
D_FF = 4 * D_MODEL
NORM_EPS = 1e-6
LNX_EPS = 64e-5
ATTN_SCALE = HEAD_DIM ** -0.5
EXP_M05 = math.exp(-0.5)
LOG2E = math.log2(math.e)
NEG = -1e30

COL_R, COL_K, COL_V = 0, RWKV_WIDTH, 2 * RWKV_WIDTH
COL_WD = 3 * RWKV_WIDTH
COL_AD = COL_WD + DECAY_LORA
COL_GD = COL_AD + AAA_LORA
COL_SQ = RWKV_PROJ
COL_SK = COL_SQ + SWA_WIDTH
COL_SV = COL_SK + SWA_KV_WIDTH
COL_MQ = RWKV_PROJ + SWA_PROJ

CHUNK = 64
VMEM_LIMIT = 56 * 1024 * 1024


def _dot(a, b):
    return jnp.dot(a.astype(BF16), b.astype(BF16), preferred_element_type=F32)


def _dot_nt(a, b):
    return lax.dot_general(a.astype(BF16), b.astype(BF16), (((1,), (1,)), ((), ())),
                           preferred_element_type=F32)


def _dot_tn(a, b):
    return lax.dot_general(a.astype(BF16), b.astype(BF16), (((0,), (0,)), ((), ())),
                           preferred_element_type=F32)


def _rms(x, g):
    return x * lax.rsqrt(jnp.mean(x * x, -1, keepdims=True) + NORM_EPS) * g


def _params(*sem):
    return pltpu.CompilerParams(dimension_semantics=sem, vmem_limit_bytes=VMEM_LIMIT)


def _full(shape):
    n = len(shape)
    return pl.BlockSpec(shape, lambda *_: (0,) * n)


def _head_blocks(width):
    bi = lax.broadcasted_iota(jnp.int32, (width, width), 0) // HEAD_DIM
    bj = lax.broadcasted_iota(jnp.int32, (width, width), 1) // HEAD_DIM
    return jnp.where(bi == bj, 1.0, 0.0).astype(BF16)


IN_PROJ_TILE = 1024
IN_PROJ_SUB = 256


def _in_proj_kernel(x_ref, xs_ref, g_ref, w_ref, wo_ref, w1_ref, w2_ref, o_ref, os_ref, wob_ref, w1b_ref, w2b_ref,
                    wb_scr):
    @pl.when(pl.program_id(0) == 0)
    def _():
        wb_scr[...] = w_ref[...].astype(BF16)

    wob_ref[...] = wo_ref[...].astype(BF16)
    w1b_ref[...] = w1_ref[...].astype(BF16)
    w2b_ref[...] = w2_ref[...].astype(BF16)

    tm = x_ref.shape[0]
    for j in range(tm // IN_PROJ_SUB):
        rows = slice(j * IN_PROJ_SUB, (j + 1) * IN_PROJ_SUB)
        h = _rms(x_ref[rows, :], g_ref[...])
        o_ref[rows, :] = jnp.dot(h.astype(BF16), wb_scr[...], preferred_element_type=F32).astype(o_ref.dtype)

    @pl.when(pl.program_id(0) == pl.num_programs(0) - 1)
    def _():
        hs = _rms(xs_ref[...], g_ref[...])
        os_ref[...] = jnp.dot(hs.astype(BF16), wb_scr[...], preferred_element_type=F32)


def _in_proj(x2d, xs2d, g, w, later_weights, tm):
    n, ns = x2d.shape[0], xs2d.shape[0]
    steps = n // tm
    slab = lambda m: pl.BlockSpec((m.shape[0] // steps, m.shape[1]), lambda i: (i, 0))
    proj_p, proj_s, *later_bf16 = pl.pallas_call(
        _in_proj_kernel,
        grid=(steps,),
        in_specs=[pl.BlockSpec((tm, D_MODEL), lambda i: (i, 0)),
                  _full((ns, D_MODEL)),
                  _full((1, D_MODEL)),
                  pl.BlockSpec((D_MODEL, IN_PROJ), lambda i: (0, 0), pipeline_mode=pl.Buffered(1))]
                 + [slab(m) for m in later_weights],
        out_specs=[pl.BlockSpec((tm, IN_PROJ), lambda i: (i, 0)), _full((ns, IN_PROJ))]
                  + [slab(m) for m in later_weights],
        out_shape=[jax.ShapeDtypeStruct((n, IN_PROJ), BF16), jax.ShapeDtypeStruct((ns, IN_PROJ), F32)]
                  + [jax.ShapeDtypeStruct(m.shape, BF16) for m in later_weights],
        scratch_shapes=[pltpu.VMEM((D_MODEL, IN_PROJ), BF16)],
        compiler_params=_params("arbitrary"),
        name="in_proj",
    )(x2d, xs2d, g, w, *later_weights)
    return proj_p, proj_s, later_bf16


def _rwkv_features(xs, w0, a0, k_k, k_a, wupw, wupa, wupg):
    r = xs[:, COL_R:COL_R + RWKV_WIDTH]
    k = xs[:, COL_K:COL_K + RWKV_WIDTH]
    v = xs[:, COL_V:COL_V + RWKV_WIDTH]
    wd = xs[:, COL_WD:COL_WD + DECAY_LORA]
    ad = xs[:, COL_AD:COL_AD + AAA_LORA]
    gd = xs[:, COL_GD:COL_GD + GATE_LORA]
    logw = -jax.nn.sigmoid(w0 + _dot(jnp.tanh(wd), wupw)) * EXP_M05
    a_sig = jax.nn.sigmoid(a0 + _dot(ad, wupa))
    gate = _dot(jax.nn.sigmoid(gd), wupg)
    kk = k * k_k
    k2 = k * (1.0 + (a_sig - 1.0) * k_a)
    return r, k2, v, kk, a_sig, logw, gate


def _seg_sum(x, blk):
    xb = x.astype(BF16)
    half = RWKV_WIDTH // 2
    return jnp.concatenate([jnp.dot(xb[:, :half], blk, preferred_element_type=F32),
                            jnp.dot(xb[:, half:], blk, preferred_element_type=F32)], axis=1)


def _group_norm_out(y, bonus, gate, lnx_w, lnx_b, blk):
    inv_d = 1.0 / HEAD_DIM
    m = _seg_sum(y, blk) * inv_d
    d = y - m
    var = _seg_sum(d * d, blk) * inv_d
    yn = d * lax.rsqrt(var + LNX_EPS) * lnx_w + lnx_b
    return (yn + bonus) * gate


RWKV_TILE = 4 * CHUNK
PAIR = 2 * HEAD_DIM
N_PAIRS = N_RWKV_HEADS // 2


def _rwkv_prompt_kernel(p_ref, mu_ref, w0_ref, a0_ref, kk_ref, ka_ref, rk_ref, lnw_ref, lnb_ref,
                        wupw_ref, wupa_ref, wupg_ref, y_ref, sout_ref, s_scr, prev_scr):
    C, TT, D = CHUNK, RWKV_TILE, HEAD_DIM
    NC = TT // C
    t = pl.program_id(1)

    @pl.when(t == 0)
    def _():
        s_scr[...] = jnp.zeros_like(s_scr)
        prev_scr[...] = jnp.zeros_like(prev_scr)

    p = p_ref[0].astype(F32)
    row = lax.broadcasted_iota(jnp.int32, p.shape, 0)
    prev = jnp.where(row == 0, prev_scr[...], pltpu.roll(p, 1, axis=0))
    prev_scr[...] = p[TT - 1:TT, :]
    xs = p + (prev - p) * mu_ref[...]
    r, k2, v, kk, a_sig, logw, gate = _rwkv_features(
        xs, w0_ref[...], a0_ref[...], kk_ref[...], ka_ref[...], wupw_ref[...], wupa_ref[...], wupg_ref[...])

    blk = _head_blocks(RWKV_WIDTH // 2)
    kkn = kk / jnp.maximum(jnp.sqrt(_seg_sum(kk * kk, blk)), 1e-12)
    bb = kkn * a_sig

    ri = lax.broadcasted_iota(jnp.int32, (TT, TT), 0)
    ci = lax.broadcasted_iota(jnp.int32, (TT, TT), 1)
    tri = jnp.where(jnp.logical_and(ri >= ci, ri // C == ci // C), 1.0, 0.0).astype(BF16)
    lw2 = logw * LOG2E
    l1 = lw2.astype(BF16)
    l2 = (lw2 - l1.astype(F32)).astype(BF16)
    cum = jnp.dot(tri, l1, preferred_element_type=F32) + jnp.dot(tri, l2, preferred_element_type=F32)
    c_last = jnp.concatenate([jnp.broadcast_to(cum[(c + 1) * C - 1:(c + 1) * C, :], (C, RWKV_WIDTH))
                              for c in range(NC)], axis=0)
    e_pos = jnp.exp2(cum)
    e_neg = jnp.exp2(-cum)
    e_prev = jnp.exp2(cum - lw2)
    e_last = jnp.exp2(c_last - cum)

    lo_full = (lax.broadcasted_iota(jnp.int32, (TT, RWKV_WIDTH), 1) % PAIR) < D
    at_f = -kkn * e_prev
    rt_f = r * e_pos
    at_lo = jnp.where(lo_full, at_f, 0.0).astype(BF16)
    at_hi = jnp.where(lo_full, 0.0, at_f).astype(BF16)
    rt_lo = jnp.where(lo_full, rt_f, 0.0).astype(BF16)
    rt_hi = jnp.where(lo_full, 0.0, rt_f).astype(BF16)
    bt_b = (bb * e_neg).astype(BF16)
    kt_b = (k2 * e_neg).astype(BF16)
    bh_b = (bb * e_last).astype(BF16)
    kh_b = (k2 * e_last).astype(BF16)
    v_b = v.astype(BF16)

    r2 = lax.broadcasted_iota(jnp.int32, (C, PAIR), 0)
    c2 = lax.broadcasted_iota(jnp.int32, (C, PAIR), 1)
    lo = c2 < D
    c2m = jnp.where(lo, c2, c2 - C)
    mask_a = jnp.logical_and(lo, c2 < r2)
    mask_ak = jnp.logical_and(jnp.logical_not(lo), c2m < r2)
    mask_r = c2m <= r2
    eye_hi = jnp.where(jnp.logical_and(jnp.logical_not(lo), c2m == r2), 1.0, 0.0)
    zeros_cp = jnp.zeros((C, PAIR), F32)
    qi = lax.broadcasted_iota(jnp.int32, (PAIR, PAIR), 0) // D
    qj = lax.broadcasted_iota(jnp.int32, (PAIR, PAIR), 1) // D
    diag = qi == qj

    PR = [(c, q) for c in range(NC) for q in range(N_PAIRS)]
    n = len(PR)
    E = range(2)
    win = lambda x, c, q: x[c * C:(c + 1) * C, q * PAIR:(q + 1) * PAIR]
    sc = [_dot_nt(jnp.concatenate([win(at_lo, c, q), win(at_hi, c, q), win(rt_lo, c, q), win(rt_hi, c, q)], axis=0),
                  jnp.concatenate([win(bt_b, c, q), win(kt_b, c, q)], axis=0)) for c, q in PR]
    vr = [pltpu.roll(win(v, c, q), D, axis=1) for c, q in PR]
    vvr = [jnp.concatenate([vr[i], vr[i]], axis=0).astype(BF16) for i in range(n)]
    m_ak = [[jnp.where(mask_ak, sc[i][e * C:(e + 1) * C], 0.0) for e in E] for i in range(n)]
    m_r = [[jnp.where(mask_r, sc[i][(2 + e) * C:(3 + e) * C], 0.0) for e in E] for i in range(n)]

    zf = [[_dot(m_ak[i][e], vvr[i]) for e in E] for i in range(n)]
    W = [[jnp.where(mask_a, sc[i][e * C:(e + 1) * C], eye_hi) for e in E] for i in range(n)]
    for k in range(int(math.log2(C))):
        Wb = [[W[i][e].astype(BF16) for e in E] for i in range(n)]
        AW = [[jnp.dot(Wb[i][e][:, :C], Wb[i][e], preferred_element_type=F32) for e in E] for i in range(n)]
        W = [[jnp.where(lo, 0.0, W[i][e]) + AW[i][e] for e in E] for i in range(n)]
    X = [[_dot(W[i][0], jnp.concatenate([zeros_cp, jnp.where(lo, win(at_f, c, q), zf[i][0])], axis=0)),
          _dot(W[i][1], jnp.concatenate([zeros_cp, jnp.where(lo, zf[i][1], win(at_f, c, q))], axis=0))]
         for i, (c, q) in enumerate(PR)]

    S = [s_scr[q] for q in range(N_PAIRS)]
    ys = []
    for c in range(NC):
        w_last = jnp.exp2(cum[(c + 1) * C - 1:(c + 1) * C, :])
        idx = [c * N_PAIRS + q for q in range(N_PAIRS)]
        mkg = [[_dot_tn(X[i][e], win(bh_b, c, q)) for e in E] for q, i in enumerate(idx)]
        vk = [_dot_tn(win(v_b, c, q), win(kh_b, c, q)) for q, i in enumerate(idx)]
        ry = [[_dot(m_r[i][0], jnp.concatenate([X[i][0], jnp.where(lo, 0.0, vr[i])], axis=0)),
               _dot(m_r[i][1], jnp.concatenate([X[i][1], jnp.where(lo, vr[i], 0.0)], axis=0))] for i in idx]
        rp = [win(rt_f, c, q) + jnp.where(lo, ry[q][0], ry[q][1]) for q in range(N_PAIRS)]
        y0 = [pltpu.roll(jnp.where(lo, ry[q][1], ry[q][0]), D, axis=1) for q in range(N_PAIRS)]
        mk = [jnp.where(diag, jnp.concatenate([mkg[q][0][:D], mkg[q][1][D:]], axis=0), 0.0) for q in range(N_PAIRS)]
        g = [jnp.where(diag, vk[q] + jnp.concatenate([mkg[q][0][D:], mkg[q][1][:D]], axis=0), 0.0)
             for q in range(N_PAIRS)]
        y = [_dot_nt(rp[q], S[q]) for q in range(N_PAIRS)]
        dS = [_dot(S[q], mk[q]) for q in range(N_PAIRS)]
        S = [S[q] * w_last[:, q * PAIR:(q + 1) * PAIR] + dS[q] + g[q] for q in range(N_PAIRS)]
        ys.append(jnp.concatenate([y[q] + y0[q] for q in range(N_PAIRS)], axis=1))
    for q in range(N_PAIRS):
        s_scr[q] = S[q]

    bonus = _seg_sum(r * k2 * rk_ref[...], blk) * v
    y_ref[0] = _group_norm_out(jnp.concatenate(ys, axis=0), bonus, gate, lnw_ref[...], lnb_ref[...], blk)

    @pl.when(t == pl.num_programs(1) - 1)
    def _():
        for q in range(N_PAIRS):
            sout_ref[0, 2 * q] = S[q][:D, :D]
            sout_ref[0, 2 * q + 1] = S[q][D:, D:]


def _rwkv_prompt(proj, mu, w0, a0, k_k, k_a, r_k, lnx_w, lnx_b, wupw, wupa, wupg):
    B, T, _ = proj.shape
    vec = lambda n: _full((1, n))
    return pl.pallas_call(
        _rwkv_prompt_kernel,
        grid=(B, T // RWKV_TILE),
        in_specs=[pl.BlockSpec((1, RWKV_TILE, RWKV_PROJ), lambda b, t: (b, t, 0)),
                  vec(RWKV_PROJ), vec(RWKV_WIDTH), vec(RWKV_WIDTH), vec(RWKV_WIDTH), vec(RWKV_WIDTH),
                  vec(RWKV_WIDTH), vec(RWKV_WIDTH), vec(RWKV_WIDTH),
                  _full((DECAY_LORA, RWKV_WIDTH)), _full((AAA_LORA, RWKV_WIDTH)), _full((GATE_LORA, RWKV_WIDTH))],
        out_specs=[pl.BlockSpec((1, RWKV_TILE, RWKV_WIDTH), lambda b, t: (b, t, 0)),
                   pl.BlockSpec((1, N_RWKV_HEADS, HEAD_DIM, HEAD_DIM), lambda b, t: (b, 0, 0, 0))],
        out_shape=[jax.ShapeDtypeStruct((B, T, RWKV_WIDTH), F32),
                   jax.ShapeDtypeStruct((B, N_RWKV_HEADS, HEAD_DIM, HEAD_DIM), F32)],
        scratch_shapes=[pltpu.VMEM((N_PAIRS, PAIR, PAIR), F32),
                        pltpu.VMEM((1, RWKV_PROJ), F32)],
        compiler_params=_params("arbitrary", "arbitrary"),
        name="rwkv_prompt",
    )(proj, mu, w0, a0, k_k, k_a, r_k, lnx_w, lnx_b, wupw, wupa, wupg)


def _rwkv_step_feat_kernel(p_ref, sh_ref, mu_ref, w0_ref, a0_ref, kk_ref, ka_ref, rk_ref,
                           wupw_ref, wupa_ref, wupg_ref, vecs_ref, bonus_ref, gate_ref):
    p = p_ref[...]
    xs = p + (sh_ref[...] - p) * mu_ref[...]
    r, k2, v, kk, a_sig, logw, gate = _rwkv_features(
        xs, w0_ref[...], a0_ref[...], kk_ref[...], ka_ref[...], wupw_ref[...], wupa_ref[...], wupg_ref[...])
    blk = _head_blocks(RWKV_WIDTH // 2)
    kkn = kk / jnp.maximum(jnp.sqrt(_seg_sum(kk * kk, blk)), 1e-12)
    for i, x in enumerate((-kkn, kkn * a_sig, jnp.exp(logw), k2, r, v)):
        vecs_ref[i] = x.T
    bonus_ref[...] = _seg_sum(r * k2 * rk_ref[...], blk) * v
    gate_ref[...] = gate


def _rwkv_step_state_kernel(s_ref, vecs_ref, sout_ref, y_ref):
    S = s_ref[0]
    a, b, w, k, r, v = (vecs_ref[i] for i in range(6))
    sa = jnp.sum(S * a[None], axis=1)
    S = S * w[None] + sa[:, None, :] * b[None] + v[:, None, :] * k[None]
    sout_ref[0] = S
    y_ref[...] = jnp.sum(S * r[None], axis=1)


def _rwkv_step_out_kernel(yt_ref, bonus_ref, gate_ref, lnw_ref, lnb_ref, o_ref):
    o_ref[...] = _group_norm_out(yt_ref[...].T, bonus_ref[...], gate_ref[...], lnw_ref[...], lnb_ref[...],
                                 _head_blocks(RWKV_WIDTH // 2))


def _rwkv_step(proj, shift, state_t, mu, w0, a0, k_k, k_a, r_k, lnx_w, lnx_b, wupw, wupa, wupg):
    n = proj.shape[0]
    vec = lambda m: _full((1, m))
    rows = _full((n, RWKV_PROJ))
    wide = _full((n, RWKV_WIDTH))
    vecs, bonus, gate = pl.pallas_call(
        _rwkv_step_feat_kernel,
        grid=(1,),
        in_specs=[rows, rows, vec(RWKV_PROJ), vec(RWKV_WIDTH), vec(RWKV_WIDTH), vec(RWKV_WIDTH), vec(RWKV_WIDTH),
                  vec(RWKV_WIDTH),
                  _full((DECAY_LORA, RWKV_WIDTH)), _full((AAA_LORA, RWKV_WIDTH)), _full((GATE_LORA, RWKV_WIDTH))],
        out_specs=[_full((6, RWKV_WIDTH, n)), wide, wide],
        out_shape=[jax.ShapeDtypeStruct((6, RWKV_WIDTH, n), F32),
                   jax.ShapeDtypeStruct((n, RWKV_WIDTH), F32), jax.ShapeDtypeStruct((n, RWKV_WIDTH), F32)],
        compiler_params=_params("arbitrary"),
        name="rwkv_step_feat",
    )(proj, shift, mu, w0, a0, k_k, k_a, r_k, wupw, wupa, wupg)
    st_spec = pl.BlockSpec((1, HEAD_DIM, HEAD_DIM, n), lambda h: (h, 0, 0, 0))
    state_new, yt = pl.pallas_call(
        _rwkv_step_state_kernel,
        grid=(N_RWKV_HEADS,),
        in_specs=[st_spec, pl.BlockSpec((6, HEAD_DIM, n), lambda h: (0, h, 0))],
        out_specs=[st_spec, pl.BlockSpec((HEAD_DIM, n), lambda h: (h, 0))],
        out_shape=[jax.ShapeDtypeStruct(state_t.shape, F32), jax.ShapeDtypeStruct((RWKV_WIDTH, n), F32)],
        compiler_params=_params("arbitrary"),
        name="rwkv_step_state",
    )(state_t, vecs)
    y = pl.pallas_call(
        _rwkv_step_out_kernel,
        grid=(1,),
        in_specs=[_full((RWKV_WIDTH, n)), wide, wide, vec(RWKV_WIDTH), vec(RWKV_WIDTH)],
        out_specs=wide,
        out_shape=jax.ShapeDtypeStruct((n, RWKV_WIDTH), F32),
        compiler_params=_params("arbitrary"),
        name="rwkv_step_out",
    )(yt, bonus, gate, lnx_w, lnx_b)
    return y, state_new


def _t5_bucket_np(dist):
    max_exact = N_BUCKETS // 2
    d = np.maximum(dist, 1).astype(np.float32)
    large = max_exact + (np.log(d / np.float32(max_exact)) / np.float32(math.log(MAX_DISTANCE / max_exact))
                         * np.float32(N_BUCKETS - max_exact)).astype(np.int32)
    large = np.minimum(large, N_BUCKETS - 1)
    return np.where(dist < max_exact, dist, large).astype(np.int32)


def _prompt_bucket_table():
    qi = np.arange(BLOCK)[:, None]
    kj = np.arange(2 * BLOCK)[None, :]
    dist = BLOCK + qi - kj
    valid = (dist >= 0) & (dist <= WINDOW)
    return np.where(valid, _t5_bucket_np(np.maximum(dist, 0)), -1).astype(np.int32)


def _decode_bucket_table():
    dist = WINDOW - np.arange(WINDOW)
    return np.broadcast_to(_t5_bucket_np(dist)[None, :], (8, WINDOW)).astype(np.int32).copy()


def _bias_from_buckets(bkt, relb_ref, h, init):
    acc = jnp.full(bkt.shape, init, F32)
    for b in range(N_BUCKETS):
        acc = jnp.where(bkt == b, relb_ref[b, h], acc)
    return acc


ATT_TILE = 4 * BLOCK


def _attn_init(bkt_ref, relb_ref, mk_ref, mv_ref, bias_scr, mk_scr, mv_scr):
    grp = N_SWA_HEADS // N_SWA_KV_HEADS
    ones = jnp.ones((N_MEM, HEAD_DIM), F32)
    bkt = bkt_ref[...]
    for j in range(N_SWA_KV_HEADS):
        for g in range(grp):
            bias_scr[j, g * BLOCK:(g + 1) * BLOCK, :] = _bias_from_buckets(bkt, relb_ref, j * grp + g, NEG) * LOG2E
    mk = mk_ref[0]
    mv = mv_ref[0]
    for h in range(N_MEM_HEADS):
        sl = slice(h * HEAD_DIM, (h + 1) * HEAD_DIM)
        mk_scr[h] = mk[:, sl].astype(BF16)
        mv_scr[h] = jnp.concatenate([mv[:, sl], ones], axis=1).astype(BF16)


def _attn_tile(first, qs, qm, kc, kp, vc, vp, sqg, skg, mqg, sink_ref, bias_scr, mk_scr, mv_scr):
    grp = N_SWA_HEADS // N_SWA_KV_HEADS
    ones = jnp.ones((2 * BLOCK, HEAD_DIM), F32)
    rowi = lax.broadcasted_iota(jnp.int32, (2 * BLOCK, 1), 0)
    col = lax.broadcasted_iota(jnp.int32, (2 * BLOCK, 2 * BLOCK), 1)
    pad_mask = jnp.logical_and(first, col < BLOCK)
    hsl = [slice(h * HEAD_DIM, (h + 1) * HEAD_DIM) for h in range(N_SWA_HEADS)]
    qs_n = _rms_heads(qs, sqg, N_SWA_HEADS) * (ATTN_SCALE * LOG2E)
    qm_n = _rms_heads(qm, mqg, N_MEM_HEADS) * (ATTN_SCALE * LOG2E)
    kc_n = _rms_heads(kc, skg, N_SWA_KV_HEADS)
    kp_n = _rms_heads(kp, skg, N_SWA_KV_HEADS)
    chains =[(a, j) for a in range(ATT_TILE // BLOCK) for j in range(N_SWA_KV_HEADS)]
    lhs, keys, vals, sinkcol = [], [], [], []
    for a, j in chains:
        rs = slice(a * BLOCK, (a + 1) * BLOCK)
        lhs.append(jnp.concatenate([qs_n[rs, hsl[j * grp + g]] for g in range(grp)], axis=0))
        if a == 0:
            keys.append(jnp.concatenate([kp_n[:, hsl[j]], kc_n[:BLOCK, hsl[j]]], axis=0))
            vv = jnp.concatenate([vp[:, hsl[j]], vc[:BLOCK, hsl[j]]], axis=0)
        else:
            ks = slice((a - 1) * BLOCK, (a + 1) * BLOCK)
            keys.append(kc_n[ks, hsl[j]])
            vv = vc[ks, hsl[j]]
        vals.append(jnp.concatenate([vv, ones], axis=1))
        sinkcol.append(jnp.where(rowi < BLOCK, sink_ref[0, j * grp], sink_ref[0, j * grp + 1]) * LOG2E)
    qmn = [qm_n[:, hsl[h]] for h in range(N_MEM_HEADS)]

    bias = [bias_scr[j] for j in range(N_SWA_KV_HEADS)]
    mem_k = [mk_scr[h] for h in range(N_MEM_HEADS)]
    mem_v = [mv_scr[h] for h in range(N_MEM_HEADS)]
    n_w = len(chains)

    def scores(t):
        return _dot_nt(lhs[t], keys[t]) if t < n_w else _dot_nt(qmn[t - n_w], mem_k[t - n_w])

    def finish(ts, s):
        e, extra = {}, {}
        for t in ts:
            if t < n_w:
                a, j = chains[t]
                st = s[t] + bias[j]
                if a == 0:
                    st = jnp.where(pad_mask, NEG, st)
                m = jnp.maximum(jnp.max(st, -1, keepdims=True), sinkcol[t])
                e[t], extra[t] = jnp.exp2(st - m), jnp.exp2(sinkcol[t] - m)
            else:
                e[t], extra[t] = jnp.exp2(s[t] - jnp.max(s[t], -1, keepdims=True)), 0.0
        o_full = {t: _dot(e[t], vals[t] if t < n_w else mem_v[t - n_w]) for t in ts}
        return {t: o_full[t][:, :HEAD_DIM] / (o_full[t][:, HEAD_DIM:HEAD_DIM + 1] + extra[t]) for t in ts}

    def place(t):
        if t < n_w:
            a, j = chains[t]
            return [(slice(a * BLOCK, (a + 1) * BLOCK), hsl[j * grp + g], slice(g * BLOCK, (g + 1) * BLOCK))
                    for g in range(grp)]
        h = t - n_w
        return [(slice(0, ATT_TILE), slice(SWA_WIDTH + h * HEAD_DIM, SWA_WIDTH + (h + 1) * HEAD_DIM), slice(0, ATT_TILE))]

    return kc_n, n_w, n_w + N_MEM_HEADS, scores, finish, place


FF_CHUNK = 1024


def _ffn_chunk(h2, w1_ref, w2_ref, c):
    u = jnp.dot(h2, w1_ref[:, c * FF_CHUNK:(c + 1) * FF_CHUNK], preferred_element_type=F32)
    u = jnp.square(jnp.maximum(u, 0.0)).astype(BF16)
    return jnp.dot(u, w2_ref[c * FF_CHUNK:(c + 1) * FF_CHUNK, :], preferred_element_type=F32)


def _mix_rows(x, yr, ys, ym, wo_ref, g2_ref):
    x1 = (x
          + jnp.dot(yr.astype(BF16), wo_ref[0:RWKV_WIDTH, :], preferred_element_type=F32)
          + jnp.dot(ys.astype(BF16), wo_ref[RWKV_WIDTH:RWKV_WIDTH + SWA_WIDTH, :], preferred_element_type=F32)
          + jnp.dot(ym.astype(BF16), wo_ref[RWKV_WIDTH + SWA_WIDTH:, :], preferred_element_type=F32))
    return x1, _rms(x1, g2_ref[...]).astype(BF16)


def _attn_ffn_kernel(tiles_per_seq, qs_ref, qm_ref, kc_ref, kp_ref, vc_ref, vp_ref, mk_ref, mv_ref, bkt_ref, relb_ref,
                     sink_ref, sqg_ref, skg_ref, mqg_ref, x_ref, yr_ref, xs_ref, yrs_ref, yss_ref, yms_ref,
                     wo_ref, g2_ref, w1_ref, w2_ref, o_ref, os_ref, kn_ref, bias_scr, mk_scr, mv_scr, stage_scr):
    s = pl.program_id(0)
    nt = pl.num_programs(0) - 1
    tile = jnp.minimum(s, nt - 1)
    first = lax.rem(tile, tiles_per_seq) == 0
    wslot = lax.rem(s, 2)
    rslot = 1 - wslot

    @pl.when(jnp.logical_and(first, s < nt))
    def _():
        _attn_init(bkt_ref, relb_ref, mk_ref, mv_ref, bias_scr, mk_scr, mv_scr)

    def step(mlp_rows, out_ref):
        kc_n, n_w, n_tasks, scores, finish, place = _attn_tile(
            first, *(ref[0].astype(F32) for ref in (qs_ref, qm_ref, kc_ref, kp_ref, vc_ref, vp_ref)),
            sqg_ref[...], skg_ref[...], mqg_ref[...], sink_ref, bias_scr, mk_scr, mv_scr)
        kn_ref[0] = kc_n[ATT_TILE - BLOCK:]

        x1, h2 = _mix_rows(*mlp_rows(), wo_ref, g2_ref)
        sc = {t: scores(t) for t in range(n_tasks)}
        ff = _ffn_chunk(h2, w1_ref, w2_ref, 0)
        outs = finish(range(n_w), sc)
        ff = ff + _ffn_chunk(h2, w1_ref, w2_ref, 1)
        outs.update(finish(range(n_w, n_tasks), sc))
        for c in range(2, D_FF // FF_CHUNK):
            ff = ff + _ffn_chunk(h2, w1_ref, w2_ref, c)
        out_ref[...] = x1 + ff
        for t in range(n_tasks):
            for rows, lanes, src in place(t):
                stage_scr[wslot, rows, lanes] = outs[t][src]

    def sample_rows():
        return xs_ref[...], yrs_ref[...], yss_ref[...], yms_ref[...]

    def tile_rows():
        staged = stage_scr[rslot]
        return x_ref[...], yr_ref[...], staged[:, :SWA_WIDTH], staged[:, SWA_WIDTH:]

    pl.when(s == 0)(functools.partial(step, sample_rows, os_ref))
    pl.when(s > 0)(functools.partial(step, tile_rows, o_ref))


def _attn_ffn(proj, mk, mv, rel_bias, sinks, sqg, skg, mqg, x2d, yr, sample, wo, g2, w1, w2):
    B, T, _ = proj.shape
    tps = T // ATT_TILE
    nt = B * tps
    ns = sample[0].shape[0]
    bkt = jnp.asarray(_prompt_bucket_table())
    smem = pl.BlockSpec(memory_space=pltpu.SMEM)
    kblk, vblk = COL_SK // SWA_KV_WIDTH, COL_SV // SWA_KV_WIDTH
    att = lambda s: jnp.minimum(s, nt - 1)
    cur = lambda s, c: (att(s) // tps, att(s) % tps, c)
    prev = lambda s, c: (att(s) // tps, jnp.maximum((ATT_TILE // BLOCK) * (att(s) % tps) - 1, 0), c)
    ffn = lambda s: (jnp.maximum(s - 1, 0), 0)
    memb = pl.BlockSpec((1, N_MEM, MEM_WIDTH), lambda s: (att(s) // tps, 0, 0))
    const = lambda shape: pl.BlockSpec(shape, lambda s: (0, 0), pipeline_mode=pl.Buffered(1))
    widths = (D_MODEL, RWKV_WIDTH, SWA_WIDTH, MEM_WIDTH)
    return pl.pallas_call(
        functools.partial(_attn_ffn_kernel, tps),
        grid=(nt + 1,),
        in_specs=[pl.BlockSpec((1, ATT_TILE, SWA_WIDTH), lambda s: cur(s, COL_SQ // SWA_WIDTH)),
                  pl.BlockSpec((1, ATT_TILE, MEM_WIDTH), lambda s: cur(s, COL_MQ // MEM_WIDTH)),
                  pl.BlockSpec((1, ATT_TILE, SWA_KV_WIDTH), lambda s: cur(s, kblk)),
                  pl.BlockSpec((1, BLOCK, SWA_KV_WIDTH), lambda s: prev(s, kblk)),
                  pl.BlockSpec((1, ATT_TILE, SWA_KV_WIDTH), lambda s: cur(s, vblk)),
                  pl.BlockSpec((1, BLOCK, SWA_KV_WIDTH), lambda s: prev(s, vblk)),
                  memb, memb,
                  _full((BLOCK, 2 * BLOCK)), smem, smem,
                  _full((1, HEAD_DIM)), _full((1, HEAD_DIM)), _full((1, HEAD_DIM)),
                  pl.BlockSpec((ATT_TILE, D_MODEL), ffn), pl.BlockSpec((ATT_TILE, RWKV_WIDTH), ffn)]
                 + [_full((ns, w)) for w in widths]
                 + [const((D_MODEL, D_MODEL)), _full((1, D_MODEL)), const((D_MODEL, D_FF)), const((D_FF, D_MODEL))],
        out_specs=[pl.BlockSpec((ATT_TILE, D_MODEL), ffn), _full((ns, D_MODEL)),
                   pl.BlockSpec((1, BLOCK, SWA_KV_WIDTH), lambda s: (att(s) // tps, 0, 0))],
        out_shape=[jax.ShapeDtypeStruct((B * T, D_MODEL), F32), jax.ShapeDtypeStruct((ns, D_MODEL), F32),
                   jax.ShapeDtypeStruct((B, BLOCK, SWA_KV_WIDTH), F32)],
        scratch_shapes=[pltpu.VMEM((N_SWA_KV_HEADS, 2 * BLOCK, 2 * BLOCK), F32),
                        pltpu.VMEM((N_MEM_HEADS, N_MEM, HEAD_DIM), BF16),
                        pltpu.VMEM((N_MEM_HEADS, N_MEM, 2 * HEAD_DIM), BF16),
                        pltpu.VMEM((2, ATT_TILE, SWA_WIDTH + MEM_WIDTH), F32)],
        compiler_params=_params("arbitrary"),
        name="attn_ffn",
    )(proj, proj, proj, proj, proj, proj, mk, mv, bkt, rel_bias, sinks, sqg, skg, mqg, x2d, yr, *sample, wo, g2, w1, w2)


def _memory_kv_kernel(mem_ref, g_ref, w_ref, kg_ref, mk_ref, mv_ref):
    kv = _dot(_rms(mem_ref[0], g_ref[...]), w_ref[...])
    kg = kg_ref[...]
    for h in range(N_MEM_HEADS):
        sl = slice(h * HEAD_DIM, (h + 1) * HEAD_DIM)
        mk_ref[0, :, sl] = _rms(kv[:, sl], kg)
    mv_ref[0] = kv[:, MEM_WIDTH:]


def _memory_kv(mem, g, w, kg):
    B = mem.shape[0]
    blk = pl.BlockSpec((1, N_MEM, MEM_WIDTH), lambda b: (b, 0, 0))
    return pl.pallas_call(
        _memory_kv_kernel,
        grid=(B,),
        in_specs=[pl.BlockSpec((1, N_MEM, D_MODEL), lambda b: (b, 0, 0)),
                  _full((1, D_MODEL)), _full((D_MODEL, 2 * MEM_WIDTH)), _full((1, HEAD_DIM))],
        out_specs=[blk, blk],
        out_shape=[jax.ShapeDtypeStruct((B, N_MEM, MEM_WIDTH), F32)] * 2,
        compiler_params=_params("arbitrary"),
        name="memory_kv",
    )(mem, g, w, kg)


SEQ_TILE = 8


def _rms_heads(x, g, n_heads):
    ms = jnp.dot((x * x).astype(BF16), _head_blocks(n_heads * HEAD_DIM), preferred_element_type=F32)
    return x * lax.rsqrt(ms * (1.0 / HEAD_DIM) + NORM_EPS) * jnp.concatenate([g] * n_heads, axis=1)


def _decode_attn_kernel(p_ref, kbuf_ref, vbuf_ref, mk_ref, mv_ref, bkt_ref, relb_ref, sink_ref,
                        sqg_ref, skg_ref, mqg_ref, ys_ref, ym_ref, kout_ref, vout_ref, tab_scr):
    grp = N_SWA_HEADS // N_SWA_KV_HEADS

    @pl.when(pl.program_id(0) == 0)
    def _():
        hrow = lax.broadcasted_iota(jnp.int32, (8, WINDOW), 0)
        lane = lax.broadcasted_iota(jnp.int32, (8, WINDOW), 1)
        bias_w = jnp.zeros((8, WINDOW), F32)
        cols = jnp.zeros((8, WINDOW), F32)
        for h in range(N_SWA_HEADS):
            bias_w = jnp.where(hrow == h, _bias_from_buckets(bkt_ref[...], relb_ref, h, 0.0), bias_w)
            cols = jnp.where(jnp.logical_and(hrow == h, lane == 0), relb_ref[0, h], cols)
            cols = jnp.where(jnp.logical_and(hrow == h, lane == 1), sink_ref[0, h], cols)
        tab_scr[0] = bias_w
        tab_scr[1] = cols

    bias_w = tab_scr[0][:N_SWA_HEADS]
    bias_new = tab_scr[1][:N_SWA_HEADS, 0:1]
    sink = tab_scr[1][:N_SWA_HEADS, 1:2]
    rowi = lax.broadcasted_iota(jnp.int32, (SWA_KV_WIDTH, WINDOW), 0)
    lanei = lax.broadcasted_iota(jnp.int32, (SWA_KV_WIDTH, WINDOW), 1)
    eye = rowi == lanei
    NH = N_SWA_HEADS
    own = (lax.broadcasted_iota(jnp.int32, (NH, NH * HEAD_DIM), 1) // HEAD_DIM
           == lax.broadcasted_iota(jnp.int32, (NH, NH * HEAD_DIM), 0))

    p = p_ref[...]
    qn = _rms_heads(p[:, COL_SQ:COL_SQ + SWA_WIDTH], sqg_ref[...], N_SWA_HEADS) * ATTN_SCALE
    kn = _rms_heads(p[:, COL_SK:COL_SK + SWA_KV_WIDTH], skg_ref[...], N_SWA_KV_HEADS)
    vn = p[:, COL_SV:COL_SV + SWA_KV_WIDTH]
    qmn = _rms_heads(p[:, COL_MQ:COL_MQ + MEM_WIDTH], mqg_ref[...], N_MEM_HEADS) * ATTN_SCALE
    rep = lambda x: jnp.concatenate([x[:, j * HEAD_DIM:(j + 1) * HEAD_DIM] for j in range(N_SWA_KV_HEADS)
                                     for _ in range(grp)], axis=1)
    kn_rep, vn_rep = rep(kn), rep(vn)

    B = range(SEQ_TILE)
    dup = lambda c: jnp.concatenate([c[j * HEAD_DIM:(j + 1) * HEAD_DIM] for j in range(N_SWA_KV_HEADS)
                                     for _ in range(grp)], axis=0)
    qd = [jnp.where(own, qn[b:b + 1, :], 0.0) for b in B]
    qmd = [jnp.where(own, qmn[b:b + 1, :], 0.0) for b in B]
    kdup = [dup(kbuf_ref[b]) for b in B]
    vdup = [dup(vbuf_ref[b]) for b in B]
    s = [_dot(qd[b], kdup[b]) + bias_w for b in B]
    sm = [_dot(qmd[b], mk_ref[b]) for b in B]
    s_new = [jnp.sum(qd[b] * kn_rep[b:b + 1, :], -1, keepdims=True) + bias_new for b in B]
    m = [jnp.maximum(jnp.maximum(jnp.max(s[b], -1, keepdims=True), s_new[b]), sink) for b in B]
    e = [jnp.exp(s[b] - m[b]) for b in B]
    e_new = [jnp.exp(s_new[b] - m[b]) for b in B]
    den = [jnp.sum(e[b], -1, keepdims=True) + e_new[b] + jnp.exp(sink - m[b]) for b in B]
    em = [jnp.exp(sm[b] - jnp.max(sm[b], -1, keepdims=True)) for b in B]
    ov = [_dot_nt(e[b], vdup[b]) for b in B]
    omf = [_dot_nt(em[b], mv_ref[b]) for b in B]
    ys_rows = [jnp.sum(jnp.where(own, (ov[b] + e_new[b] * vn_rep[b:b + 1, :]) / den[b], 0.0), 0, keepdims=True)
               for b in B]
    ym_rows = [jnp.sum(jnp.where(own, omf[b] / jnp.sum(em[b], -1, keepdims=True), 0.0), 0, keepdims=True)
               for b in B]
    ys_ref[...] = jnp.concatenate(ys_rows, axis=0)
    ym_ref[...] = jnp.concatenate(ym_rows, axis=0)
    for b in B:
        kn_col = jnp.sum(jnp.where(eye, kn[b:b + 1, :], 0.0), -1, keepdims=True)
        vn_col = jnp.sum(jnp.where(eye, vn[b:b + 1, :], 0.0), -1, keepdims=True)
        kout_ref[b] = jnp.where(lanei == WINDOW - 1, kn_col, pltpu.roll(kbuf_ref[b], WINDOW - 1, axis=1))
        vout_ref[b] = jnp.where(lanei == WINDOW - 1, vn_col, pltpu.roll(vbuf_ref[b], WINDOW - 1, axis=1))


def _decode_attn(proj, kbuf, vbuf, mk, mv, rel_bias, sinks, sqg, skg, mqg):
    n = proj.shape[0]
    bkt = jnp.asarray(_decode_bucket_table())
    smem = pl.BlockSpec(memory_space=pltpu.SMEM)
    win = pl.BlockSpec((SEQ_TILE, WINDOW, SWA_KV_WIDTH), lambda i: (i, 0, 0))
    memb = pl.BlockSpec((SEQ_TILE, N_MEM, MEM_WIDTH), lambda i: (i, 0, 0))
    return pl.pallas_call(
        _decode_attn_kernel,
        grid=(n // SEQ_TILE,),
        in_specs=[pl.BlockSpec((SEQ_TILE, IN_PROJ), lambda i: (i, 0)), win, win, memb, memb,
                  _full((8, WINDOW)), smem, smem, _full((1, HEAD_DIM)), _full((1, HEAD_DIM)), _full((1, HEAD_DIM))],
        out_specs=[pl.BlockSpec((SEQ_TILE, SWA_WIDTH), lambda i: (i, 0)),
                   pl.BlockSpec((SEQ_TILE, MEM_WIDTH), lambda i: (i, 0)), win, win],
        out_shape=[jax.ShapeDtypeStruct((n, SWA_WIDTH), F32), jax.ShapeDtypeStruct((n, MEM_WIDTH), F32),
                   jax.ShapeDtypeStruct(kbuf.shape, F32), jax.ShapeDtypeStruct(vbuf.shape, F32)],
        scratch_shapes=[pltpu.VMEM((2, 8, WINDOW), F32)],
        compiler_params=_params("arbitrary"),
        name="decode_attn",
    )(proj, kbuf, vbuf, mk, mv, bkt, rel_bias, sinks, sqg, skg, mqg)


def kernel(x_prompt, x_sample, state_rwkv, state_shift, cache_swa_k, cache_swa_v, cache_mem_k, cache_mem_v,
           mem_prompt, rel_bias, norm1_g, w_in, mu_shift, w0, w_up_w, a0, w_up_a, w_up_g, k_k, k_a, r_k,
           lnx_w, lnx_b, q_norm_swa, k_norm_swa, sinks, mem_norm_g, w_mem_kv, q_norm_mem, k_norm_mem,
           w_out, norm2_g, w_ff1, w_ff2):
    B, T, _ = x_prompt.shape
    Bd = x_sample.shape[0]
    l = 0
    rwkv_params = (mu_shift[l][None], w0[l][None], a0[l][None], k_k[l][None], k_a[l][None],
                   r_k[l].reshape(1, RWKV_WIDTH), lnx_w[l][None], lnx_b[l][None],
                   w_up_w[l], w_up_a[l], w_up_g[l])
    sqg, skg, mqg, mkg = q_norm_swa[l][None], k_norm_swa[l][None], q_norm_mem[l][None], k_norm_mem[l][None]
    g1, g2 = norm1_g[l][None], norm2_g[l][None]

    xp = x_prompt.reshape(B * T, D_MODEL)
    xs = x_sample.reshape(Bd, D_MODEL)
    proj_p, proj_s, (w_out_b, w1_b, w2_b) = _in_proj(xp, xs, g1, w_in[l], (w_out[l], w_ff1[l], w_ff2[l]),
                                                     IN_PROJ_TILE)
    proj_p = proj_p.reshape(B, T, IN_PROJ)

    mk, mv = _memory_kv(mem_prompt, mem_norm_g[l][None], w_mem_kv[l], mkg)
    yr_p, s_p = _rwkv_prompt(proj_p, *rwkv_params)
    shift_p = proj_p[:, T - 1, :RWKV_PROJ].astype(F32)
    vb_p = proj_p[:, T - WINDOW:, COL_SV:COL_SV + SWA_KV_WIDTH].astype(F32)

    yr_s, st_s = _rwkv_step(proj_s[:, :RWKV_PROJ], state_shift[l], jnp.transpose(state_rwkv[l], (1, 2, 3, 0)),
                            *rwkv_params)
    s_s = jnp.transpose(st_s, (3, 0, 1, 2))
    fmajor = lambda c: jnp.transpose(c, (0, 2, 3, 1)).reshape(Bd, c.shape[2] * HEAD_DIM, c.shape[1])
    ys_s, ym_s, kb_s, vb_s = _decode_attn(
        proj_s, fmajor(cache_swa_k[l]), fmajor(cache_swa_v[l]), fmajor(cache_mem_k[l]), fmajor(cache_mem_v[l]),
        rel_bias, sinks[l][None], sqg, skg, mqg)
    pmajor = lambda c: jnp.transpose(c.reshape(Bd, N_SWA_KV_HEADS, HEAD_DIM, WINDOW), (0, 3, 1, 2))[None]

    y_p, y_s, kn_p = _attn_ffn(proj_p, mk, mv, rel_bias, sinks[l][None], sqg, skg, mqg,
                               xp, yr_p.reshape(B * T, RWKV_WIDTH), (xs, yr_s, ys_s, ym_s), w_out_b, g2, w1_b, w2_b)
    y_p = y_p.reshape(B, T, D_MODEL)
    y_s = y_s.reshape(Bd, 1, D_MODEL)

    return (y_p, y_s,
            s_p[None], shift_p[None],
            kn_p.reshape(1, B, WINDOW, N_SWA_KV_HEADS, HEAD_DIM),
            vb_p.reshape(1, B, WINDOW, N_SWA_KV_HEADS, HEAD_DIM),
            mk.reshape(1, B, N_MEM, N_MEM_HEADS, HEAD_DIM), mv.reshape(1, B, N_MEM, N_MEM_HEADS, HEAD_DIM),
            s_s[None], proj_s[:, :RWKV_PROJ][None],
            pmajor(kb_s), pmajor(vb_s))
```

```python
import functools
import math

import numpy as np
import jax
import jax.numpy as jnp
from jax import lax
from jax.experimental import pallas as pl
from jax.experimental.pallas import tpu as pltpu

F32 = jnp.float32
BF16 = jnp.bfloat16

D_MODEL = 1024
HEAD_DIM = 64
RWKV_WIDTH = 512
N_RWKV_HEADS = 8
SWA_WIDTH = 256
N_SWA_HEADS = 4
N_SWA_KV_HEADS = 2
SWA_KV_WIDTH = 128
MEM_WIDTH = 256
N_MEM_HEADS = 4
N_MEM = 256
WINDOW = 128
BLOCK = 128
N_BUCKETS = 32
MAX_DISTANCE = 128
DECAY_LORA = 64
AAA_LORA = 64
GATE_LORA = 128
RWKV_PROJ = 3 * RWKV_WIDTH + DECAY_LORA + AAA_LORA + GATE_LORA
SWA_PROJ = SWA_WIDTH + 2 * SWA_KV_WIDTH
IN_PROJ = RWKV_PROJ + SWA_PROJ + MEM_WIDTH
D_FF = 4 * D_MODEL
NORM_EPS = 1e-6
LNX_EPS = 64e-5
ATTN_SCALE = HEAD_DIM ** -0.5
EXP_M05 = math.exp(-0.5)
LOG2E = math.log2(math.e)
NEG = -1e30

COL_R, COL_K, COL_V = 0, RWKV_WIDTH, 2 * RWKV_WIDTH
COL_WD = 3 * RWKV_WIDTH
COL_AD = COL_WD + DECAY_LORA
COL_GD = COL_AD + AAA_LORA
COL_SQ = RWKV_PROJ
COL_SK = COL_SQ + SWA_WIDTH
COL_SV = COL_SK + SWA_KV_WIDTH
COL_MQ = RWKV_PROJ + SWA_PROJ

CHUNK = 64
VMEM_LIMIT = 56 * 1024 * 1024


def _dot(a, b):
    return jnp.dot(a.astype(BF16), b.astype(BF16), preferred_element_type=F32)


def _dot_nt(a, b):
    return lax.dot_general(a.astype(BF16), b.astype(BF16), (((1,), (1,)), ((), ())),
                           preferred_element_type=F32)


def _dot_tn(a, b):
    return lax.dot_general(a.astype(BF16), b.astype(BF16), (((0,), (0,)), ((), ())),
                           preferred_element_type=F32)


def _rms(x, g):
    return x * lax.rsqrt(jnp.mean(x * x, -1, keepdims=True) + NORM_EPS) * g


def _params(*sem):
    return pltpu.CompilerParams(dimension_semantics=sem, vmem_limit_bytes=VMEM_LIMIT)


def _full(shape):
    n = len(shape)
    return pl.BlockSpec(shape, lambda *_: (0,) * n)


def _head_blocks(width):
    bi = lax.broadcasted_iota(jnp.int32, (width, width), 0) // HEAD_DIM
    bj = lax.broadcasted_iota(jnp.int32, (width, width), 1) // HEAD_DIM
    return jnp.where(bi == bj, 1.0, 0.0).astype(BF16)


IN_PROJ_TILE = 1024
IN_PROJ_SUB = 256


def _in_proj_kernel(x_ref, xs_ref, g_ref, w_ref, wo_ref, w1_ref, w2_ref, o_ref, os_ref, osr_ref, wob_ref, w1b_ref,
                    w2b_ref, wb_scr):
    @pl.when(pl.program_id(0) == 0)
    def _():
        wb_scr[...] = w_ref[...].astype(BF16)

    wob_ref[...] = wo_ref[...].astype(BF16)
    w1b_ref[...] = w1_ref[...].astype(BF16)
    w2b_ref[...] = w2_ref[...].astype(BF16)

    tm = x_ref.shape[0]
    for j in range(tm // IN_PROJ_SUB):
        rows = slice(j * IN_PROJ_SUB, (j + 1) * IN_PROJ_SUB)
        h = _rms(x_ref[rows, :], g_ref[...])
        o_ref[rows, :] = jnp.dot(h.astype(BF16), wb_scr[...], preferred_element_type=F32)

    @pl.when(pl.program_id(0) == pl.num_programs(0) - 1)
    def _():
        hs = _rms(xs_ref[...], g_ref[...])
        ps = jnp.dot(hs.astype(BF16), wb_scr[...], preferred_element_type=F32)
        os_ref[...] = ps
        osr_ref[...] = ps[:, :RWKV_PROJ]


def _in_proj(x2d, xs2d, g, w, later_weights, tm):
    n, ns = x2d.shape[0], xs2d.shape[0]
    steps = n // tm
    slab = lambda m: pl.BlockSpec((m.shape[0] // steps, m.shape[1]), lambda i: (i, 0))
    proj_p, proj_s, proj_s_rwkv, *later_bf16 = pl.pallas_call(
        _in_proj_kernel,
        grid=(steps,),
        in_specs=[pl.BlockSpec((tm, D_MODEL), lambda i: (i, 0)),
                  _full((ns, D_MODEL)),
                  _full((1, D_MODEL)),
                  pl.BlockSpec((D_MODEL, IN_PROJ), lambda i: (0, 0), pipeline_mode=pl.Buffered(1))]
                 + [slab(m) for m in later_weights],
        out_specs=[pl.BlockSpec((tm, IN_PROJ), lambda i: (i, 0)), _full((ns, IN_PROJ)), _full((ns, RWKV_PROJ))]
                  + [slab(m) for m in later_weights],
        out_shape=[jax.ShapeDtypeStruct((n, IN_PROJ), F32), jax.ShapeDtypeStruct((ns, IN_PROJ), F32),
                   jax.ShapeDtypeStruct((ns, RWKV_PROJ), F32)]
                  + [jax.ShapeDtypeStruct(m.shape, BF16) for m in later_weights],
        scratch_shapes=[pltpu.VMEM((D_MODEL, IN_PROJ), BF16)],
        compiler_params=_params("arbitrary"),
        name="in_proj",
    )(x2d, xs2d, g, w, *later_weights)
    return proj_p, proj_s, proj_s_rwkv, later_bf16


def _rwkv_features(xs, w0, a0, k_k, k_a, wupw, wupa, wupg):
    r = xs[:, COL_R:COL_R + RWKV_WIDTH]
    k = xs[:, COL_K:COL_K + RWKV_WIDTH]
    v = xs[:, COL_V:COL_V + RWKV_WIDTH]
    wd = xs[:, COL_WD:COL_WD + DECAY_LORA]
    ad = xs[:, COL_AD:COL_AD + AAA_LORA]
    gd = xs[:, COL_GD:COL_GD + GATE_LORA]
    logw = -jax.nn.sigmoid(w0 + _dot(jnp.tanh(wd), wupw)) * EXP_M05
    a_sig = jax.nn.sigmoid(a0 + _dot(ad, wupa))
    gate = _dot(jax.nn.sigmoid(gd), wupg)
    kk = k * k_k
    k2 = k * (1.0 + (a_sig - 1.0) * k_a)
    return r, k2, v, kk, a_sig, logw, gate


def _seg_sum(x, blk):
    xb = x.astype(BF16)
    half = RWKV_WIDTH // 2
    return jnp.concatenate([jnp.dot(xb[:, :half], blk, preferred_element_type=F32),
                            jnp.dot(xb[:, half:], blk, preferred_element_type=F32)], axis=1)


def _group_norm_out(y, bonus, gate, lnx_w, lnx_b, blk):
    inv_d = 1.0 / HEAD_DIM
    m = _seg_sum(y, blk) * inv_d
    d = y - m
    var = _seg_sum(d * d, blk) * inv_d
    yn = d * lax.rsqrt(var + LNX_EPS) * lnx_w + lnx_b
    return (yn + bonus) * gate


RWKV_TILE = 4 * CHUNK
PAIR = 2 * HEAD_DIM
N_PAIRS = N_RWKV_HEADS // 2


def _rwkv_prompt_kernel(p_ref, mu_ref, w0_ref, a0_ref, kk_ref, ka_ref, rk_ref, lnw_ref, lnb_ref,
                        wupw_ref, wupa_ref, wupg_ref, y_ref, sout_ref, s_scr, prev_scr):
    C, TT, D = CHUNK, RWKV_TILE, HEAD_DIM
    NC = TT // C
    t = pl.program_id(1)

    @pl.when(t == 0)
    def _():
        s_scr[...] = jnp.zeros_like(s_scr)
        prev_scr[...] = jnp.zeros_like(prev_scr)

    p = p_ref[0]
    row = lax.broadcasted_iota(jnp.int32, p.shape, 0)
    prev = jnp.where(row == 0, prev_scr[...], pltpu.roll(p, 1, axis=0))
    prev_scr[...] = p[TT - 1:TT, :]
    xs = p + (prev - p) * mu_ref[...]
    r, k2, v, kk, a_sig, logw, gate = _rwkv_features(
        xs, w0_ref[...], a0_ref[...], kk_ref[...], ka_ref[...], wupw_ref[...], wupa_ref[...], wupg_ref[...])

    blk = _head_blocks(RWKV_WIDTH // 2)
    kkn = kk / jnp.maximum(jnp.sqrt(_seg_sum(kk * kk, blk)), 1e-12)
    bb = kkn * a_sig

    ri = lax.broadcasted_iota(jnp.int32, (TT, TT), 0)
    ci = lax.broadcasted_iota(jnp.int32, (TT, TT), 1)
    tri = jnp.where(jnp.logical_and(ri >= ci, ri // C == ci // C), 1.0, 0.0).astype(BF16)
    lw2 = logw * LOG2E
    l1 = lw2.astype(BF16)
    l2 = (lw2 - l1.astype(F32)).astype(BF16)
    cum = jnp.dot(tri, l1, preferred_element_type=F32) + jnp.dot(tri, l2, preferred_element_type=F32)
    c_last = jnp.concatenate([jnp.broadcast_to(cum[(c + 1) * C - 1:(c + 1) * C, :], (C, RWKV_WIDTH))
                              for c in range(NC)], axis=0)
    e_pos = jnp.exp2(cum)
    e_neg = jnp.exp2(-cum)
    e_prev = jnp.exp2(cum - lw2)
    e_last = jnp.exp2(c_last - cum)

    lo_full = (lax.broadcasted_iota(jnp.int32, (TT, RWKV_WIDTH), 1) % PAIR) < D
    at_f = -kkn * e_prev
    rt_f = r * e_pos
    at_lo = jnp.where(lo_full, at_f, 0.0).astype(BF16)
    at_hi = jnp.where(lo_full, 0.0, at_f).astype(BF16)
    rt_lo = jnp.where(lo_full, rt_f, 0.0).astype(BF16)
    rt_hi = jnp.where(lo_full, 0.0, rt_f).astype(BF16)
    bt_b = (bb * e_neg).astype(BF16)
    kt_b = (k2 * e_neg).astype(BF16)
    bh_b = (bb * e_last).astype(BF16)
    kh_b = (k2 * e_last).astype(BF16)
    v_b = v.astype(BF16)

    r2 = lax.broadcasted_iota(jnp.int32, (C, PAIR), 0)
    c2 = lax.broadcasted_iota(jnp.int32, (C, PAIR), 1)
    lo = c2 < D
    c2m = jnp.where(lo, c2, c2 - C)
    mask_a = jnp.logical_and(lo, c2 < r2)
    mask_ak = jnp.logical_and(jnp.logical_not(lo), c2m < r2)
    mask_r = c2m <= r2
    eye_hi = jnp.where(jnp.logical_and(jnp.logical_not(lo), c2m == r2), 1.0, 0.0)
    zeros_cp = jnp.zeros((C, PAIR), F32)
    qi = lax.broadcasted_iota(jnp.int32, (PAIR, PAIR), 0) // D
    qj = lax.broadcasted_iota(jnp.int32, (PAIR, PAIR), 1) // D
    diag = qi == qj

    PR = [(c, q) for c in range(NC) for q in range(N_PAIRS)]
    n = len(PR)
    E = range(2)
    win = lambda x, c, q: x[c * C:(c + 1) * C, q * PAIR:(q + 1) * PAIR]
    sc = [_dot_nt(jnp.concatenate([win(at_lo, c, q), win(at_hi, c, q), win(rt_lo, c, q), win(rt_hi, c, q)], axis=0),
                  jnp.concatenate([win(bt_b, c, q), win(kt_b, c, q)], axis=0)) for c, q in PR]
    vr = [pltpu.roll(win(v, c, q), D, axis=1) for c, q in PR]
    vvr = [jnp.concatenate([vr[i], vr[i]], axis=0).astype(BF16) for i in range(n)]
    m_ak = [[jnp.where(mask_ak, sc[i][e * C:(e + 1) * C], 0.0) for e in E] for i in range(n)]
    m_r = [[jnp.where(mask_r, sc[i][(2 + e) * C:(3 + e) * C], 0.0) for e in E] for i in range(n)]

    zf = [[_dot(m_ak[i][e], vvr[i]) for e in E] for i in range(n)]
    W = [[jnp.where(mask_a, sc[i][e * C:(e + 1) * C], eye_hi) for e in E] for i in range(n)]
    for k in range(int(math.log2(C))):
        Wb = [[W[i][e].astype(BF16) for e in E] for i in range(n)]
        AW = [[jnp.dot(Wb[i][e][:, :C], Wb[i][e], preferred_element_type=F32) for e in E] for i in range(n)]
        W = [[jnp.where(lo, 0.0, W[i][e]) + AW[i][e] for e in E] for i in range(n)]
    X = [[_dot(W[i][0], jnp.concatenate([zeros_cp, jnp.where(lo, win(at_f, c, q), zf[i][0])], axis=0)),
          _dot(W[i][1], jnp.concatenate([zeros_cp, jnp.where(lo, zf[i][1], win(at_f, c, q))], axis=0))]
         for i, (c, q) in enumerate(PR)]

    S = [s_scr[q] for q in range(N_PAIRS)]
    ys = []
    for c in range(NC):
        w_last = jnp.exp2(cum[(c + 1) * C - 1:(c + 1) * C, :])
        idx = [c * N_PAIRS + q for q in range(N_PAIRS)]
        mkg = [[_dot_tn(X[i][e], win(bh_b, c, q)) for e in E] for q, i in enumerate(idx)]
        vk = [_dot_tn(win(v_b, c, q), win(kh_b, c, q)) for q, i in enumerate(idx)]
        ry = [[_dot(m_r[i][0], jnp.concatenate([X[i][0], jnp.where(lo, 0.0, vr[i])], axis=0)),
               _dot(m_r[i][1], jnp.concatenate([X[i][1], jnp.where(lo, vr[i], 0.0)], axis=0))] for i in idx]
        rp = [win(rt_f, c, q) + jnp.where(lo, ry[q][0], ry[q][1]) for q in range(N_PAIRS)]
        y0 = [pltpu.roll(jnp.where(lo, ry[q][1], ry[q][0]), D, axis=1) for q in range(N_PAIRS)]
        mk = [jnp.where(diag, jnp.concatenate([mkg[q][0][:D], mkg[q][1][D:]], axis=0), 0.0) for q in range(N_PAIRS)]
        g = [jnp.where(diag, vk[q] + jnp.concatenate([mkg[q][0][D:], mkg[q][1][:D]], axis=0), 0.0)
             for q in range(N_PAIRS)]
        y = [_dot_nt(rp[q], S[q]) for q in range(N_PAIRS)]
        dS = [_dot(S[q], mk[q]) for q in range(N_PAIRS)]
        S = [S[q] * w_last[:, q * PAIR:(q + 1) * PAIR] + dS[q] + g[q] for q in range(N_PAIRS)]
        ys.append(jnp.concatenate([y[q] + y0[q] for q in range(N_PAIRS)], axis=1))
    for q in range(N_PAIRS):
        s_scr[q] = S[q]

    bonus = _seg_sum(r * k2 * rk_ref[...], blk) * v
    y_ref[0] = _group_norm_out(jnp.concatenate(ys, axis=0), bonus, gate, lnw_ref[...], lnb_ref[...], blk)

    @pl.when(t == pl.num_programs(1) - 1)
    def _():
        for q in range(N_PAIRS):
            sout_ref[0, 2 * q] = S[q][:D, :D]
            sout_ref[0, 2 * q + 1] = S[q][D:, D:]


def _rwkv_prompt(proj, mu, w0, a0, k_k, k_a, r_k, lnx_w, lnx_b, wupw, wupa, wupg):
    B, T, _ = proj.shape
    vec = lambda n: _full((1, n))
    return pl.pallas_call(
        _rwkv_prompt_kernel,
        grid=(B, T // RWKV_TILE),
        in_specs=[pl.BlockSpec((1, RWKV_TILE, RWKV_PROJ), lambda b, t: (b, t, 0)),
                  vec(RWKV_PROJ), vec(RWKV_WIDTH), vec(RWKV_WIDTH), vec(RWKV_WIDTH), vec(RWKV_WIDTH),
                  vec(RWKV_WIDTH), vec(RWKV_WIDTH), vec(RWKV_WIDTH),
                  _full((DECAY_LORA, RWKV_WIDTH)), _full((AAA_LORA, RWKV_WIDTH)), _full((GATE_LORA, RWKV_WIDTH))],
        out_specs=[pl.BlockSpec((1, RWKV_TILE, RWKV_WIDTH), lambda b, t: (b, t, 0)),
                   pl.BlockSpec((1, N_RWKV_HEADS, HEAD_DIM, HEAD_DIM), lambda b, t: (b, 0, 0, 0))],
        out_shape=[jax.ShapeDtypeStruct((B, T, RWKV_WIDTH), F32),
                   jax.ShapeDtypeStruct((B, N_RWKV_HEADS, HEAD_DIM, HEAD_DIM), F32)],
        scratch_shapes=[pltpu.VMEM((N_PAIRS, PAIR, PAIR), F32),
                        pltpu.VMEM((1, RWKV_PROJ), F32)],
        compiler_params=_params("arbitrary", "arbitrary"),
        name="rwkv_prompt",
    )(proj, mu, w0, a0, k_k, k_a, r_k, lnx_w, lnx_b, wupw, wupa, wupg)


def _rwkv_step_feat_kernel(p_ref, sh_ref, mu_ref, w0_ref, a0_ref, kk_ref, ka_ref, rk_ref,
                           wupw_ref, wupa_ref, wupg_ref, vecs_ref, bonus_ref, gate_ref):
    p = p_ref[...]
    xs = p + (sh_ref[...] - p) * mu_ref[...]
    r, k2, v, kk, a_sig, logw, gate = _rwkv_features(
        xs, w0_ref[...], a0_ref[...], kk_ref[...], ka_ref[...], wupw_ref[...], wupa_ref[...], wupg_ref[...])
    blk = _head_blocks(RWKV_WIDTH // 2)
    kkn = kk / jnp.maximum(jnp.sqrt(_seg_sum(kk * kk, blk)), 1e-12)
    for i, x in enumerate((-kkn, kkn * a_sig, jnp.exp(logw), k2, r, v)):
        vecs_ref[i] = x.T
    bonus_ref[...] = _seg_sum(r * k2 * rk_ref[...], blk) * v
    gate_ref[...] = gate


def _rwkv_step_state_kernel(s_ref, vecs_ref, sout_ref, y_ref):
    S = s_ref[0]
    a, b, w, k, r, v = (vecs_ref[i] for i in range(6))
    sa = jnp.sum(S * a[None], axis=1)
    S = S * w[None] + sa[:, None, :] * b[None] + v[:, None, :] * k[None]
    sout_ref[0] = S
    y_ref[...] = jnp.sum(S * r[None], axis=1)


def _rwkv_step_out_kernel(yt_ref, bonus_ref, gate_ref, lnw_ref, lnb_ref, o_ref):
    o_ref[...] = _group_norm_out(yt_ref[...].T, bonus_ref[...], gate_ref[...], lnw_ref[...], lnb_ref[...],
                                 _head_blocks(RWKV_WIDTH // 2))


def _rwkv_step(proj, shift, state_t, mu, w0, a0, k_k, k_a, r_k, lnx_w, lnx_b, wupw, wupa, wupg):
    n = proj.shape[0]
    vec = lambda m: _full((1, m))
    rows = _full((n, RWKV_PROJ))
    wide = _full((n, RWKV_WIDTH))
    vecs, bonus, gate = pl.pallas_call(
        _rwkv_step_feat_kernel,
        grid=(1,),
        in_specs=[rows, rows, vec(RWKV_PROJ), vec(RWKV_WIDTH), vec(RWKV_WIDTH), vec(RWKV_WIDTH), vec(RWKV_WIDTH),
                  vec(RWKV_WIDTH),
                  _full((DECAY_LORA, RWKV_WIDTH)), _full((AAA_LORA, RWKV_WIDTH)), _full((GATE_LORA, RWKV_WIDTH))],
        out_specs=[_full((6, RWKV_WIDTH, n)), wide, wide],
        out_shape=[jax.ShapeDtypeStruct((6, RWKV_WIDTH, n), F32),
                   jax.ShapeDtypeStruct((n, RWKV_WIDTH), F32), jax.ShapeDtypeStruct((n, RWKV_WIDTH), F32)],
        compiler_params=_params("arbitrary"),
        name="rwkv_step_feat",
    )(proj, shift, mu, w0, a0, k_k, k_a, r_k, wupw, wupa, wupg)
    st_spec = pl.BlockSpec((1, HEAD_DIM, HEAD_DIM, n), lambda h: (h, 0, 0, 0))
    state_new, yt = pl.pallas_call(
        _rwkv_step_state_kernel,
        grid=(N_RWKV_HEADS,),
        in_specs=[st_spec, pl.BlockSpec((6, HEAD_DIM, n), lambda h: (0, h, 0))],
        out_specs=[st_spec, pl.BlockSpec((HEAD_DIM, n), lambda h: (h, 0))],
        out_shape=[jax.ShapeDtypeStruct(state_t.shape, F32), jax.ShapeDtypeStruct((RWKV_WIDTH, n), F32)],
        compiler_params=_params("arbitrary"),
        name="rwkv_step_state",
    )(state_t, vecs)
    y = pl.pallas_call(
        _rwkv_step_out_kernel,
        grid=(1,),
        in_specs=[_full((RWKV_WIDTH, n)), wide, wide, vec(RWKV_WIDTH), vec(RWKV_WIDTH)],
        out_specs=wide,
        out_shape=jax.ShapeDtypeStruct((n, RWKV_WIDTH), F32),
        compiler_params=_params("arbitrary"),
        name="rwkv_step_out",
    )(yt, bonus, gate, lnx_w, lnx_b)
    return y, state_new


def _t5_bucket_np(dist):
    max_exact = N_BUCKETS // 2
    d = np.maximum(dist, 1).astype(np.float32)
    large = max_exact + (np.log(d / np.float32(max_exact)) / np.float32(math.log(MAX_DISTANCE / max_exact))
                         * np.float32(N_BUCKETS - max_exact)).astype(np.int32)
    large = np.minimum(large, N_BUCKETS - 1)
    return np.where(dist < max_exact, dist, large).astype(np.int32)


def _prompt_bucket_table():
    qi = np.arange(BLOCK)[:, None]
    kj = np.arange(2 * BLOCK)[None, :]
    dist = BLOCK + qi - kj
    valid = (dist >= 0) & (dist <= WINDOW)
    return np.where(valid, _t5_bucket_np(np.maximum(dist, 0)), -1).astype(np.int32)


def _decode_bucket_table():
    dist = WINDOW - np.arange(WINDOW)
    return np.broadcast_to(_t5_bucket_np(dist)[None, :], (8, WINDOW)).astype(np.int32).copy()


def _bias_from_buckets(bkt, relb_ref, h, init):
    acc = jnp.full(bkt.shape, init, F32)
    for b in range(N_BUCKETS):
        acc = jnp.where(bkt == b, relb_ref[b, h], acc)
    return acc


ATT_TILE = 4 * BLOCK


def _attn_init(bkt_ref, relb_ref, mk_ref, mv_ref, bias_scr, mk_scr, mv_scr):
    grp = N_SWA_HEADS // N_SWA_KV_HEADS
    ones = jnp.ones((N_MEM, HEAD_DIM), F32)
    bkt = bkt_ref[...]
    for j in range(N_SWA_KV_HEADS):
        for g in range(grp):
            bias_scr[j, g * BLOCK:(g + 1) * BLOCK, :] = _bias_from_buckets(bkt, relb_ref, j * grp + g, NEG) * LOG2E
    mk = mk_ref[0]
    mv = mv_ref[0]
    for h in range(N_MEM_HEADS):
        sl = slice(h * HEAD_DIM, (h + 1) * HEAD_DIM)
        mk_scr[h] = mk[:, sl].astype(BF16)
        mv_scr[h] = jnp.concatenate([mv[:, sl], ones], axis=1).astype(BF16)


def _attn_tile(first, qs, qm, kc, kp, vc, vp, sqg, skg, mqg, sink_ref, bias_scr, mk_scr, mv_scr):
    grp = N_SWA_HEADS // N_SWA_KV_HEADS
    ones = jnp.ones((2 * BLOCK, HEAD_DIM), F32)
    rowi = lax.broadcasted_iota(jnp.int32, (2 * BLOCK, 1), 0)
    col = lax.broadcasted_iota(jnp.int32, (2 * BLOCK, 2 * BLOCK), 1)
    pad_mask = jnp.logical_and(first, col < BLOCK)
    hsl = [slice(h * HEAD_DIM, (h + 1) * HEAD_DIM) for h in range(N_SWA_HEADS)]
    qs_n = _rms_heads(qs, sqg, N_SWA_HEADS) * (ATTN_SCALE * LOG2E)
    qm_n = _rms_heads(qm, mqg, N_MEM_HEADS) * (ATTN_SCALE * LOG2E)
    kc_n = _rms_heads(kc, skg, N_SWA_KV_HEADS)
    kp_n = _rms_heads(kp, skg, N_SWA_KV_HEADS)
    chains =[(a, j) for a in range(ATT_TILE // BLOCK) for j in range(N_SWA_KV_HEADS)]
    lhs, keys, vals, sinkcol = [], [], [], []
    for a, j in chains:
        rs = slice(a * BLOCK, (a + 1) * BLOCK)
        lhs.append(jnp.concatenate([qs_n[rs, hsl[j * grp + g]] for g in range(grp)], axis=0))
        if a == 0:
            keys.append(jnp.concatenate([kp_n[:, hsl[j]], kc_n[:BLOCK, hsl[j]]], axis=0))
            vv = jnp.concatenate([vp[:, hsl[j]], vc[:BLOCK, hsl[j]]], axis=0)
        else:
            ks = slice((a - 1) * BLOCK, (a + 1) * BLOCK)
            keys.append(kc_n[ks, hsl[j]])
            vv = vc[ks, hsl[j]]
        vals.append(jnp.concatenate([vv, ones], axis=1))
        sinkcol.append(jnp.where(rowi < BLOCK, sink_ref[0, j * grp], sink_ref[0, j * grp + 1]) * LOG2E)
    qmn = [qm_n[:, hsl[h]] for h in range(N_MEM_HEADS)]

    bias = [bias_scr[j] for j in range(N_SWA_KV_HEADS)]
    mem_k = [mk_scr[h] for h in range(N_MEM_HEADS)]
    mem_v = [mv_scr[h] for h in range(N_MEM_HEADS)]
    n_w = len(chains)

    def scores(t):
        return _dot_nt(lhs[t], keys[t]) if t < n_w else _dot_nt(qmn[t - n_w], mem_k[t - n_w])

    def finish(ts, s):
        e, extra = {}, {}
        for t in ts:
            if t < n_w:
                a, j = chains[t]
                st = s[t] + bias[j]
                if a == 0:
                    st = jnp.where(pad_mask, NEG, st)
                m = jnp.maximum(jnp.max(st, -1, keepdims=True), sinkcol[t])
                e[t], extra[t] = jnp.exp2(st - m), jnp.exp2(sinkcol[t] - m)
            else:
                e[t], extra[t] = jnp.exp2(s[t] - jnp.max(s[t], -1, keepdims=True)), 0.0
        o_full = {t: _dot(e[t], vals[t] if t < n_w else mem_v[t - n_w]) for t in ts}
        return {t: o_full[t][:, :HEAD_DIM] / (o_full[t][:, HEAD_DIM:HEAD_DIM + 1] + extra[t]) for t in ts}

    def place(t):
        if t < n_w:
            a, j = chains[t]
            return [(slice(a * BLOCK, (a + 1) * BLOCK), hsl[j * grp + g], slice(g * BLOCK, (g + 1) * BLOCK))
                    for g in range(grp)]
        h = t - n_w
        return [(slice(0, ATT_TILE), slice(SWA_WIDTH + h * HEAD_DIM, SWA_WIDTH + (h + 1) * HEAD_DIM), slice(0, ATT_TILE))]

    return kc_n, n_w, n_w + N_MEM_HEADS, scores, finish, place


FF_CHUNK = 1024


def _ffn_chunk(h2, w1_ref, w2_ref, c):
    u = jnp.dot(h2, w1_ref[:, c * FF_CHUNK:(c + 1) * FF_CHUNK], preferred_element_type=F32)
    u = jnp.square(jnp.maximum(u, 0.0)).astype(BF16)
    return jnp.dot(u, w2_ref[c * FF_CHUNK:(c + 1) * FF_CHUNK, :], preferred_element_type=F32)


def _mix_rows(x, yr, ys, ym, wo_ref, g2_ref):
    x1 = (x
          + jnp.dot(yr.astype(BF16), wo_ref[0:RWKV_WIDTH, :], preferred_element_type=F32)
          + jnp.dot(ys.astype(BF16), wo_ref[RWKV_WIDTH:RWKV_WIDTH + SWA_WIDTH, :], preferred_element_type=F32)
          + jnp.dot(ym.astype(BF16), wo_ref[RWKV_WIDTH + SWA_WIDTH:, :], preferred_element_type=F32))
    return x1, _rms(x1, g2_ref[...]).astype(BF16)


def _attn_ffn_kernel(tiles_per_seq, qs_ref, qm_ref, kc_ref, kp_ref, vc_ref, vp_ref, mk_ref, mv_ref, bkt_ref, relb_ref,
                     sink_ref, sqg_ref, skg_ref, mqg_ref, x_ref, yr_ref, xs_ref, yrs_ref, yss_ref, yms_ref,
                     wo_ref, g2_ref, w1_ref, w2_ref, o_ref, os_ref, kn_ref, vn_ref, bias_scr, mk_scr, mv_scr, stage_scr):
    s = pl.program_id(0)
    nt = pl.num_programs(0) - 1
    tile = jnp.minimum(s, nt - 1)
    first = lax.rem(tile, tiles_per_seq) == 0
    wslot = lax.rem(s, 2)
    rslot = 1 - wslot

    @pl.when(jnp.logical_and(first, s < nt))
    def _():
        _attn_init(bkt_ref, relb_ref, mk_ref, mv_ref, bias_scr, mk_scr, mv_scr)

    def step(mlp_rows, out_ref):
        kc_n, n_w, n_tasks, scores, finish, place = _attn_tile(
            first, qs_ref[0], qm_ref[0], kc_ref[0], kp_ref[0], vc_ref[0], vp_ref[0],
            sqg_ref[...], skg_ref[...], mqg_ref[...], sink_ref, bias_scr, mk_scr, mv_scr)
        kn_ref[0] = kc_n[ATT_TILE - BLOCK:].T
        vn_ref[0] = vc_ref[0, ATT_TILE - BLOCK:, :].T

        x1, h2 = _mix_rows(*mlp_rows(), wo_ref, g2_ref)
        sc = {t: scores(t) for t in range(n_tasks)}
        ff = _ffn_chunk(h2, w1_ref, w2_ref, 0)
        outs = finish(range(n_w), sc)
        ff = ff + _ffn_chunk(h2, w1_ref, w2_ref, 1)
        outs.update(finish(range(n_w, n_tasks), sc))
        for c in range(2, D_FF // FF_CHUNK):
            ff = ff + _ffn_chunk(h2, w1_ref, w2_ref, c)
        out_ref[...] = x1 + ff
        for t in range(n_tasks):
            for rows, lanes, src in place(t):
                stage_scr[wslot, rows, lanes] = outs[t][src]

    def sample_rows():
        return xs_ref[...], yrs_ref[...], yss_ref[...], yms_ref[...]

    def tile_rows():
        staged = stage_scr[rslot]
        return x_ref[...], yr_ref[...], staged[:, :SWA_WIDTH], staged[:, SWA_WIDTH:]

    pl.when(s == 0)(functools.partial(step, sample_rows, os_ref))
    pl.when(s > 0)(functools.partial(step, tile_rows, o_ref))


def _attn_ffn(proj, mk, mv, rel_bias, sinks, sqg, skg, mqg, x2d, yr, sample, wo, g2, w1, w2):
    B, T, _ = proj.shape
    tps = T // ATT_TILE
    nt = B * tps
    ns = sample[0].shape[0]
    bkt = jnp.asarray(_prompt_bucket_table())
    smem = pl.BlockSpec(memory_space=pltpu.SMEM)
    kblk, vblk = COL_SK // SWA_KV_WIDTH, COL_SV // SWA_KV_WIDTH
    att = lambda s: jnp.minimum(s, nt - 1)
    cur = lambda s, c: (att(s) // tps, att(s) % tps, c)
    prev = lambda s, c: (att(s) // tps, jnp.maximum((ATT_TILE // BLOCK) * (att(s) % tps) - 1, 0), c)
    ffn = lambda s: (jnp.maximum(s - 1, 0), 0)
    memb = pl.BlockSpec((1, N_MEM, MEM_WIDTH), lambda s: (att(s) // tps, 0, 0))
    const = lambda shape: pl.BlockSpec(shape, lambda s: (0, 0), pipeline_mode=pl.Buffered(1))
    widths = (D_MODEL, RWKV_WIDTH, SWA_WIDTH, MEM_WIDTH)
    return pl.pallas_call(
        functools.partial(_attn_ffn_kernel, tps),
        grid=(nt + 1,),
        in_specs=[pl.BlockSpec((1, ATT_TILE, SWA_WIDTH), lambda s: cur(s, COL_SQ // SWA_WIDTH)),
                  pl.BlockSpec((1, ATT_TILE, MEM_WIDTH), lambda s: cur(s, COL_MQ // MEM_WIDTH)),
                  pl.BlockSpec((1, ATT_TILE, SWA_KV_WIDTH), lambda s: cur(s, kblk)),
                  pl.BlockSpec((1, BLOCK, SWA_KV_WIDTH), lambda s: prev(s, kblk)),
                  pl.BlockSpec((1, ATT_TILE, SWA_KV_WIDTH), lambda s: cur(s, vblk)),
                  pl.BlockSpec((1, BLOCK, SWA_KV_WIDTH), lambda s: prev(s, vblk)),
                  memb, memb,
                  _full((BLOCK, 2 * BLOCK)), smem, smem,
                  _full((1, HEAD_DIM)), _full((1, HEAD_DIM)), _full((1, HEAD_DIM)),
                  pl.BlockSpec((ATT_TILE, D_MODEL), ffn), pl.BlockSpec((ATT_TILE, RWKV_WIDTH), ffn)]
                 + [_full((ns, w)) for w in widths]
                 + [const((D_MODEL, D_MODEL)), _full((1, D_MODEL)), const((D_MODEL, D_FF)), const((D_FF, D_MODEL))],
        out_specs=[pl.BlockSpec((ATT_TILE, D_MODEL), ffn), _full((ns, D_MODEL))]
                  + [pl.BlockSpec((1, SWA_KV_WIDTH, BLOCK), lambda s: (att(s) // tps, 0, 0))] * 2,
        out_shape=[jax.ShapeDtypeStruct((B * T, D_MODEL), F32), jax.ShapeDtypeStruct((ns, D_MODEL), F32)]
                  + [jax.ShapeDtypeStruct((B, SWA_KV_WIDTH, BLOCK), F32)] * 2,
        scratch_shapes=[pltpu.VMEM((N_SWA_KV_HEADS, 2 * BLOCK, 2 * BLOCK), F32),
                        pltpu.VMEM((N_MEM_HEADS, N_MEM, HEAD_DIM), BF16),
                        pltpu.VMEM((N_MEM_HEADS, N_MEM, 2 * HEAD_DIM), BF16),
                        pltpu.VMEM((2, ATT_TILE, SWA_WIDTH + MEM_WIDTH), F32)],
        compiler_params=_params("arbitrary"),
        name="attn_ffn",
    )(proj, proj, proj, proj, proj, proj, mk, mv, bkt, rel_bias, sinks, sqg, skg, mqg, x2d, yr, *sample, wo, g2, w1, w2)


def _memory_kv_kernel(mem_ref, g_ref, w_ref, kg_ref, mk_ref, mv_ref, mkt_ref, mvt_ref):
    kv = _dot(_rms(mem_ref[0], g_ref[...]), w_ref[...])
    kg = kg_ref[...]
    mk = jnp.concatenate([_rms(kv[:, h * HEAD_DIM:(h + 1) * HEAD_DIM], kg) for h in range(N_MEM_HEADS)], axis=1)
    mv = kv[:, MEM_WIDTH:]
    mk_ref[0] = mk
    mv_ref[0] = mv
    mkt_ref[0] = mk.T
    mvt_ref[0] = mv.T


def _memory_kv(mem, g, w, kg):
    B = mem.shape[0]
    blk = pl.BlockSpec((1, N_MEM, MEM_WIDTH), lambda b: (b, 0, 0))
    blk_t = pl.BlockSpec((1, MEM_WIDTH, N_MEM), lambda b: (b, 0, 0))
    return pl.pallas_call(
        _memory_kv_kernel,
        grid=(B,),
        in_specs=[pl.BlockSpec((1, N_MEM, D_MODEL), lambda b: (b, 0, 0)),
                  _full((1, D_MODEL)), _full((D_MODEL, 2 * MEM_WIDTH)), _full((1, HEAD_DIM))],
        out_specs=[blk, blk, blk_t, blk_t],
        out_shape=[jax.ShapeDtypeStruct((B, N_MEM, MEM_WIDTH), F32)] * 2
                  + [jax.ShapeDtypeStruct((B, MEM_WIDTH, N_MEM), F32)] * 2,
        compiler_params=_params("arbitrary"),
        name="memory_kv",
    )(mem, g, w, kg)


SEQ_TILE = 8


def _rms_heads(x, g, n_heads):
    ms = jnp.dot((x * x).astype(BF16), _head_blocks(n_heads * HEAD_DIM), preferred_element_type=F32)
    return x * lax.rsqrt(ms * (1.0 / HEAD_DIM) + NORM_EPS) * jnp.concatenate([g] * n_heads, axis=1)


def _decode_attn_kernel(p_ref, kbuf_ref, vbuf_ref, mk_ref, mv_ref, bkt_ref, relb_ref, sink_ref,
                        sqg_ref, skg_ref, mqg_ref, ys_ref, ym_ref, kout_ref, vout_ref, tab_scr):
    grp = N_SWA_HEADS // N_SWA_KV_HEADS

    @pl.when(pl.program_id(0) == 0)
    def _():
        hrow = lax.broadcasted_iota(jnp.int32, (8, WINDOW), 0)
        lane = lax.broadcasted_iota(jnp.int32, (8, WINDOW), 1)
        bias_w = jnp.zeros((8, WINDOW), F32)
        cols = jnp.zeros((8, WINDOW), F32)
        for h in range(N_SWA_HEADS):
            bias_w = jnp.where(hrow == h, _bias_from_buckets(bkt_ref[...], relb_ref, h, 0.0), bias_w)
            cols = jnp.where(jnp.logical_and(hrow == h, lane == 0), relb_ref[0, h], cols)
            cols = jnp.where(jnp.logical_and(hrow == h, lane == 1), sink_ref[0, h], cols)
        tab_scr[0] = bias_w
        tab_scr[1] = cols

    bias_w = tab_scr[0][:N_SWA_HEADS]
    bias_new = tab_scr[1][:N_SWA_HEADS, 0:1]
    sink = tab_scr[1][:N_SWA_HEADS, 1:2]
    rowi = lax.broadcasted_iota(jnp.int32, (SWA_KV_WIDTH, WINDOW), 0)
    lanei = lax.broadcasted_iota(jnp.int32, (SWA_KV_WIDTH, WINDOW), 1)
    eye = rowi == lanei
    NH = N_SWA_HEADS
    own = (lax.broadcasted_iota(jnp.int32, (NH, NH * HEAD_DIM), 1) // HEAD_DIM
           == lax.broadcasted_iota(jnp.int32, (NH, NH * HEAD_DIM), 0))

    p = p_ref[...]
    qn = _rms_heads(p[:, COL_SQ:COL_SQ + SWA_WIDTH], sqg_ref[...], N_SWA_HEADS) * ATTN_SCALE
    kn = _rms_heads(p[:, COL_SK:COL_SK + SWA_KV_WIDTH], skg_ref[...], N_SWA_KV_HEADS)
    vn = p[:, COL_SV:COL_SV + SWA_KV_WIDTH]
    qmn = _rms_heads(p[:, COL_MQ:COL_MQ + MEM_WIDTH], mqg_ref[...], N_MEM_HEADS) * ATTN_SCALE
    rep = lambda x: jnp.concatenate([x[:, j * HEAD_DIM:(j + 1) * HEAD_DIM] for j in range(N_SWA_KV_HEADS)
                                     for _ in range(grp)], axis=1)
    kn_rep, vn_rep = rep(kn), rep(vn)

    B = range(SEQ_TILE)
    dup = lambda c: jnp.concatenate([c[j * HEAD_DIM:(j + 1) * HEAD_DIM] for j in range(N_SWA_KV_HEADS)
                                     for _ in range(grp)], axis=0)
    qd = [jnp.where(own, qn[b:b + 1, :], 0.0) for b in B]
    qmd = [jnp.where(own, qmn[b:b + 1, :], 0.0) for b in B]
    kdup = [dup(kbuf_ref[b]) for b in B]
    vdup = [dup(vbuf_ref[b]) for b in B]
    s = [_dot(qd[b], kdup[b]) + bias_w for b in B]
    sm = [_dot(qmd[b], mk_ref[b]) for b in B]
    s_new = [jnp.sum(qd[b] * kn_rep[b:b + 1, :], -1, keepdims=True) + bias_new for b in B]
    m = [jnp.maximum(jnp.maximum(jnp.max(s[b], -1, keepdims=True), s_new[b]), sink) for b in B]
    e = [jnp.exp(s[b] - m[b]) for b in B]
    e_new = [jnp.exp(s_new[b] - m[b]) for b in B]
    den = [jnp.sum(e[b], -1, keepdims=True) + e_new[b] + jnp.exp(sink - m[b]) for b in B]
    em = [jnp.exp(sm[b] - jnp.max(sm[b], -1, keepdims=True)) for b in B]
    ov = [_dot_nt(e[b], vdup[b]) for b in B]
    omf = [_dot_nt(em[b], mv_ref[b]) for b in B]
    ys_rows = [jnp.sum(jnp.where(own, (ov[b] + e_new[b] * vn_rep[b:b + 1, :]) / den[b], 0.0), 0, keepdims=True)
               for b in B]
    ym_rows = [jnp.sum(jnp.where(own, omf[b] / jnp.sum(em[b], -1, keepdims=True), 0.0), 0, keepdims=True)
               for b in B]
    ys_ref[...] = jnp.concatenate(ys_rows, axis=0)
    ym_ref[...] = jnp.concatenate(ym_rows, axis=0)
    for b in B:
        kn_col = jnp.sum(jnp.where(eye, kn[b:b + 1, :], 0.0), -1, keepdims=True)
        vn_col = jnp.sum(jnp.where(eye, vn[b:b + 1, :], 0.0), -1, keepdims=True)
        kout_ref[b] = jnp.where(lanei == WINDOW - 1, kn_col, pltpu.roll(kbuf_ref[b], WINDOW - 1, axis=1))
        vout_ref[b] = jnp.where(lanei == WINDOW - 1, vn_col, pltpu.roll(vbuf_ref[b], WINDOW - 1, axis=1))


def _decode_attn(proj, kbuf, vbuf, mk, mv, rel_bias, sinks, sqg, skg, mqg):
    n = proj.shape[0]
    bkt = jnp.asarray(_decode_bucket_table())
    smem = pl.BlockSpec(memory_space=pltpu.SMEM)
    win = pl.BlockSpec((SEQ_TILE, WINDOW, SWA_KV_WIDTH), lambda i: (i, 0, 0))
    memb = pl.BlockSpec((SEQ_TILE, N_MEM, MEM_WIDTH), lambda i: (i, 0, 0))
    return pl.pallas_call(
        _decode_attn_kernel,
        grid=(n // SEQ_TILE,),
        in_specs=[pl.BlockSpec((SEQ_TILE, IN_PROJ), lambda i: (i, 0)), win, win, memb, memb,
                  _full((8, WINDOW)), smem, smem, _full((1, HEAD_DIM)), _full((1, HEAD_DIM)), _full((1, HEAD_DIM))],
        out_specs=[pl.BlockSpec((SEQ_TILE, SWA_WIDTH), lambda i: (i, 0)),
                   pl.BlockSpec((SEQ_TILE, MEM_WIDTH), lambda i: (i, 0)), win, win],
        out_shape=[jax.ShapeDtypeStruct((n, SWA_WIDTH), F32), jax.ShapeDtypeStruct((n, MEM_WIDTH), F32),
                   jax.ShapeDtypeStruct(kbuf.shape, F32), jax.ShapeDtypeStruct(vbuf.shape, F32)],
        scratch_shapes=[pltpu.VMEM((2, 8, WINDOW), F32)],
        compiler_params=_params("arbitrary"),
        name="decode_attn",
    )(proj, kbuf, vbuf, mk, mv, bkt, rel_bias, sinks, sqg, skg, mqg)


def kernel(x_prompt, x_sample, state_rwkv, state_shift, cache_swa_k, cache_swa_v, cache_mem_k, cache_mem_v,
           mem_prompt, rel_bias, norm1_g, w_in, mu_shift, w0, w_up_w, a0, w_up_a, w_up_g, k_k, k_a, r_k,
           lnx_w, lnx_b, q_norm_swa, k_norm_swa, sinks, mem_norm_g, w_mem_kv, q_norm_mem, k_norm_mem,
           w_out, norm2_g, w_ff1, w_ff2):
    B, T, _ = x_prompt.shape
    Bd = x_sample.shape[0]
    l = 0
    rwkv_params = (mu_shift[l][None], w0[l][None], a0[l][None], k_k[l][None], k_a[l][None],
                   r_k[l].reshape(1, RWKV_WIDTH), lnx_w[l][None], lnx_b[l][None],
                   w_up_w[l], w_up_a[l], w_up_g[l])
    sqg, skg, mqg, mkg = q_norm_swa[l][None], k_norm_swa[l][None], q_norm_mem[l][None], k_norm_mem[l][None]
    g1, g2 = norm1_g[l][None], norm2_g[l][None]

    xp = x_prompt.reshape(B * T, D_MODEL)
    xs = x_sample.reshape(Bd, D_MODEL)
    proj_p, proj_s, proj_s_rwkv, (w_out_b, w1_b, w2_b) = _in_proj(
        xp, xs, g1, w_in[l], (w_out[l], w_ff1[l], w_ff2[l]), IN_PROJ_TILE)
    proj_p = proj_p.reshape(B, T, IN_PROJ)

    mk, mv, mk_t, mv_t = _memory_kv(mem_prompt, mem_norm_g[l][None], w_mem_kv[l], mkg)
    yr_p, s_p = _rwkv_prompt(proj_p, *rwkv_params)
    shift_p = proj_p[:, T - 1, :RWKV_PROJ]

    yr_s, st_s = _rwkv_step(proj_s_rwkv, state_shift[l], jnp.transpose(state_rwkv[l], (1, 2, 3, 0)), *rwkv_params)
    s_s = jnp.transpose(st_s, (3, 0, 1, 2))
    fmajor = lambda c: jnp.transpose(c, (0, 2, 3, 1)).reshape(Bd, c.shape[2] * HEAD_DIM, c.shape[1])
    ys_s, ym_s, kb_s, vb_s = _decode_attn(
        proj_s, fmajor(cache_swa_k[l]), fmajor(cache_swa_v[l]), fmajor(cache_mem_k[l]), fmajor(cache_mem_v[l]),
        rel_bias, sinks[l][None], sqg, skg, mqg)
    pmajor = lambda c: jnp.transpose(c.reshape(c.shape[0], -1, HEAD_DIM, c.shape[2]), (0, 3, 1, 2))[None]

    y_p, y_s, kn_p, vn_p = _attn_ffn(proj_p, mk, mv, rel_bias, sinks[l][None], sqg, skg, mqg,
                               xp, yr_p.reshape(B * T, RWKV_WIDTH), (xs, yr_s, ys_s, ym_s), w_out_b, g2, w1_b, w2_b)
    y_p = y_p.reshape(B, T, D_MODEL)
    y_s = y_s.reshape(Bd, 1, D_MODEL)

    return (y_p, y_s,
            s_p[None], shift_p[None],
            pmajor(kn_p), pmajor(vn_p), pmajor(mk_t), pmajor(mv_t),
            s_s[None], proj_s_rwkv[None],
            pmajor(kb_s), pmajor(vb_s))
```

```python
import functools
import math

import numpy as np
import jax
import jax.numpy as jnp
from jax import lax
from jax.experimental import pallas as pl
from jax.experimental.pallas import tpu as pltpu

F32 = jnp.float32
BF16 = jnp.bfloat16

D_MODEL = 1024
HEAD_DIM = 64
RWKV_WIDTH = 512
N_RWKV_HEADS = 8
SWA_WIDTH = 256
N_SWA_HEADS = 4
N_SWA_KV_HEADS = 2
SWA_KV_WIDTH = 128
MEM_WIDTH = 256
N_MEM_HEADS = 4
N_MEM = 256
WINDOW = 128
BLOCK = 128
N_BUCKETS = 32
MAX_DISTANCE = 128
DECAY_LORA = 64
AAA_LORA = 64
GATE_LORA = 128
RWKV_PROJ = 3 * RWKV_WIDTH + DECAY_LORA + AAA_LORA + GATE_LORA
SWA_PROJ = SWA_WIDTH + 2 * SWA_KV_WIDTH
IN_PROJ = RWKV_PROJ + SWA_PROJ + MEM_WIDTH
D_FF = 4 * D_MODEL
NORM_EPS = 1e-6
LNX_EPS = 64e-5
ATTN_SCALE = HEAD_DIM ** -0.5
EXP_M05 = math.exp(-0.5)
LOG2E = math.log2(math.e)
NEG = -1e30

COL_R, COL_K, COL_V = 0, RWKV_WIDTH, 2 * RWKV_WIDTH
COL_WD = 3 * RWKV_WIDTH
COL_AD = COL_WD + DECAY_LORA
COL_GD = COL_AD + AAA_LORA
COL_SQ = RWKV_PROJ
COL_SK = COL_SQ + SWA_WIDTH
COL_SV = COL_SK + SWA_KV_WIDTH
COL_MQ = RWKV_PROJ + SWA_PROJ

CHUNK = 64
VMEM_LIMIT = 56 * 1024 * 1024


def _dot(a, b):
    return jnp.dot(a.astype(BF16), b.astype(BF16), preferred_element_type=F32)


def _dot_nt(a, b):
    return lax.dot_general(a.astype(BF16), b.astype(BF16), (((1,), (1,)), ((), ())),
                           preferred_element_type=F32)


def _dot_tn(a, b):
    return lax.dot_general(a.astype(BF16), b.astype(BF16), (((0,), (0,)), ((), ())),
                           preferred_element_type=F32)


def _rms(x, g):
    return x * lax.rsqrt(jnp.mean(x * x, -1, keepdims=True) + NORM_EPS) * g


def _params(*sem):
    return pltpu.CompilerParams(dimension_semantics=sem, vmem_limit_bytes=VMEM_LIMIT)


def _full(shape):
    n = len(shape)
    return pl.BlockSpec(shape, lambda *_: (0,) * n)


def _head_blocks(width):
    bi = lax.broadcasted_iota(jnp.int32, (width, width), 0) // HEAD_DIM
    bj = lax.broadcasted_iota(jnp.int32, (width, width), 1) // HEAD_DIM
    return jnp.where(bi == bj, 1.0, 0.0).astype(BF16)


IN_PROJ_TILE = 1024
IN_PROJ_SUB = 256


def _in_proj_kernel(x_ref, xs_ref, g_ref, w_ref, wo_ref, w1_ref, w2_ref, o_ref, os_ref, osr_ref, wob_ref, w1b_ref,
                    w2b_ref, wb_scr):
    @pl.when(pl.program_id(0) == 0)
    def _():
        wb_scr[...] = w_ref[...].astype(BF16)

    wob_ref[...] = wo_ref[...].astype(BF16)
    w1b_ref[...] = w1_ref[...].astype(BF16)
    w2b_ref[...] = w2_ref[...].astype(BF16)

    tm = x_ref.shape[0]
    for j in range(tm // IN_PROJ_SUB):
        rows = slice(j * IN_PROJ_SUB, (j + 1) * IN_PROJ_SUB)
        h = _rms(x_ref[rows, :], g_ref[...])
        o_ref[rows, :] = jnp.dot(h.astype(BF16), wb_scr[...], preferred_element_type=F32)

    @pl.when(pl.program_id(0) == pl.num_programs(0) - 1)
    def _():
        hs = _rms(xs_ref[:, 0, :], g_ref[...])
        ps = jnp.dot(hs.astype(BF16), wb_scr[...], preferred_element_type=F32)
        os_ref[...] = ps
        osr_ref[...] = ps[:, :RWKV_PROJ]


def _in_proj(x2d, xs, g, w, later_weights, tm):
    n, ns = x2d.shape[0], xs.shape[0]
    steps = n // tm
    slab = lambda m: pl.BlockSpec((m.shape[0] // steps, m.shape[1]), lambda i: (i, 0))
    proj_p, proj_s, proj_s_rwkv, *later_bf16 = pl.pallas_call(
        _in_proj_kernel,
        grid=(steps,),
        in_specs=[pl.BlockSpec((tm, D_MODEL), lambda i: (i, 0)),
                  _full((ns, 1, D_MODEL)),
                  _full((1, D_MODEL)),
                  pl.BlockSpec((D_MODEL, IN_PROJ), lambda i: (0, 0), pipeline_mode=pl.Buffered(1))]
                 + [slab(m) for m in later_weights],
        out_specs=[pl.BlockSpec((tm, IN_PROJ), lambda i: (i, 0)), _full((ns, IN_PROJ)), _full((ns, RWKV_PROJ))]
                  + [slab(m) for m in later_weights],
        out_shape=[jax.ShapeDtypeStruct((n, IN_PROJ), F32), jax.ShapeDtypeStruct((ns, IN_PROJ), F32),
                   jax.ShapeDtypeStruct((ns, RWKV_PROJ), F32)]
                  + [jax.ShapeDtypeStruct(m.shape, BF16) for m in later_weights],
        scratch_shapes=[pltpu.VMEM((D_MODEL, IN_PROJ), BF16)],
        compiler_params=_params("arbitrary"),
        name="in_proj",
    )(x2d, xs, g, w, *later_weights)
    return proj_p, proj_s, proj_s_rwkv, later_bf16


def _rwkv_features(xs, w0, a0, k_k, k_a, wupw, wupa, wupg):
    r = xs[:, COL_R:COL_R + RWKV_WIDTH]
    k = xs[:, COL_K:COL_K + RWKV_WIDTH]
    v = xs[:, COL_V:COL_V + RWKV_WIDTH]
    wd = xs[:, COL_WD:COL_WD + DECAY_LORA]
    ad = xs[:, COL_AD:COL_AD + AAA_LORA]
    gd = xs[:, COL_GD:COL_GD + GATE_LORA]
    logw = -jax.nn.sigmoid(w0 + _dot(jnp.tanh(wd), wupw)) * EXP_M05
    a_sig = jax.nn.sigmoid(a0 + _dot(ad, wupa))
    gate = _dot(jax.nn.sigmoid(gd), wupg)
    kk = k * k_k
    k2 = k * (1.0 + (a_sig - 1.0) * k_a)
    return r, k2, v, kk, a_sig, logw, gate


def _seg_sum(x, blk):
    xb = x.astype(BF16)
    half = RWKV_WIDTH // 2
    return jnp.concatenate([jnp.dot(xb[:, :half], blk, preferred_element_type=F32),
                            jnp.dot(xb[:, half:], blk, preferred_element_type=F32)], axis=1)


def _group_norm_out(y, bonus, gate, lnx_w, lnx_b, blk):
    inv_d = 1.0 / HEAD_DIM
    m = _seg_sum(y, blk) * inv_d
    d = y - m
    var = _seg_sum(d * d, blk) * inv_d
    yn = d * lax.rsqrt(var + LNX_EPS) * lnx_w + lnx_b
    return (yn + bonus) * gate


RWKV_TILE = 4 * CHUNK
PAIR = 2 * HEAD_DIM
N_PAIRS = N_RWKV_HEADS // 2


def _rwkv_prompt_kernel(p_ref, mu_ref, w0_ref, a0_ref, kk_ref, ka_ref, rk_ref, lnw_ref, lnb_ref,
                        wupw_ref, wupa_ref, wupg_ref, y_ref, sout_ref, s_scr, prev_scr):
    C, TT, D = CHUNK, RWKV_TILE, HEAD_DIM
    NC = TT // C
    t = pl.program_id(1)

    @pl.when(t == 0)
    def _():
        s_scr[...] = jnp.zeros_like(s_scr)
        prev_scr[...] = jnp.zeros_like(prev_scr)

    p = p_ref[0]
    row = lax.broadcasted_iota(jnp.int32, p.shape, 0)
    prev = jnp.where(row == 0, prev_scr[...], pltpu.roll(p, 1, axis=0))
    prev_scr[...] = p[TT - 1:TT, :]
    xs = p + (prev - p) * mu_ref[...]
    r, k2, v, kk, a_sig, logw, gate = _rwkv_features(
        xs, w0_ref[...], a0_ref[...], kk_ref[...], ka_ref[...], wupw_ref[...], wupa_ref[...], wupg_ref[...])

    blk = _head_blocks(RWKV_WIDTH // 2)
    kkn = kk / jnp.maximum(jnp.sqrt(_seg_sum(kk * kk, blk)), 1e-12)
    bb = kkn * a_sig

    ri = lax.broadcasted_iota(jnp.int32, (TT, TT), 0)
    ci = lax.broadcasted_iota(jnp.int32, (TT, TT), 1)
    tri = jnp.where(jnp.logical_and(ri >= ci, ri // C == ci // C), 1.0, 0.0).astype(BF16)
    lw2 = logw * LOG2E
    l1 = lw2.astype(BF16)
    l2 = (lw2 - l1.astype(F32)).astype(BF16)
    cum = jnp.dot(tri, l1, preferred_element_type=F32) + jnp.dot(tri, l2, preferred_element_type=F32)
    c_last = jnp.concatenate([jnp.broadcast_to(cum[(c + 1) * C - 1:(c + 1) * C, :], (C, RWKV_WIDTH))
                              for c in range(NC)], axis=0)
    e_pos = jnp.exp2(cum)
    e_neg = jnp.exp2(-cum)
    e_prev = jnp.exp2(cum - lw2)
    e_last = jnp.exp2(c_last - cum)

    lo_full = (lax.broadcasted_iota(jnp.int32, (TT, RWKV_WIDTH), 1) % PAIR) < D
    at_f = -kkn * e_prev
    rt_f = r * e_pos
    at_lo = jnp.where(lo_full, at_f, 0.0).astype(BF16)
    at_hi = jnp.where(lo_full, 0.0, at_f).astype(BF16)
    rt_lo = jnp.where(lo_full, rt_f, 0.0).astype(BF16)
    rt_hi = jnp.where(lo_full, 0.0, rt_f).astype(BF16)
    bt_b = (bb * e_neg).astype(BF16)
    kt_b = (k2 * e_neg).astype(BF16)
    bh_b = (bb * e_last).astype(BF16)
    kh_b = (k2 * e_last).astype(BF16)
    v_b = v.astype(BF16)

    r2 = lax.broadcasted_iota(jnp.int32, (C, PAIR), 0)
    c2 = lax.broadcasted_iota(jnp.int32, (C, PAIR), 1)
    lo = c2 < D
    c2m = jnp.where(lo, c2, c2 - C)
    mask_a = jnp.logical_and(lo, c2 < r2)
    mask_ak = jnp.logical_and(jnp.logical_not(lo), c2m < r2)
    mask_r = c2m <= r2
    eye_hi = jnp.where(jnp.logical_and(jnp.logical_not(lo), c2m == r2), 1.0, 0.0)
    zeros_cp = jnp.zeros((C, PAIR), F32)
    qi = lax.broadcasted_iota(jnp.int32, (PAIR, PAIR), 0) // D
    qj = lax.broadcasted_iota(jnp.int32, (PAIR, PAIR), 1) // D
    diag = qi == qj

    PR = [(c, q) for c in range(NC) for q in range(N_PAIRS)]
    n = len(PR)
    E = range(2)
    win = lambda x, c, q: x[c * C:(c + 1) * C, q * PAIR:(q + 1) * PAIR]
    sc = [_dot_nt(jnp.concatenate([win(at_lo, c, q), win(at_hi, c, q), win(rt_lo, c, q), win(rt_hi, c, q)], axis=0),
                  jnp.concatenate([win(bt_b, c, q), win(kt_b, c, q)], axis=0)) for c, q in PR]
    vr = [pltpu.roll(win(v, c, q), D, axis=1) for c, q in PR]
    vvr = [jnp.concatenate([vr[i], vr[i]], axis=0).astype(BF16) for i in range(n)]
    m_ak = [[jnp.where(mask_ak, sc[i][e * C:(e + 1) * C], 0.0) for e in E] for i in range(n)]
    m_r = [[jnp.where(mask_r, sc[i][(2 + e) * C:(3 + e) * C], 0.0) for e in E] for i in range(n)]

    zf = [[_dot(m_ak[i][e], vvr[i]) for e in E] for i in range(n)]
    W = [[jnp.where(mask_a, sc[i][e * C:(e + 1) * C], eye_hi) for e in E] for i in range(n)]
    for k in range(int(math.log2(C))):
        Wb = [[W[i][e].astype(BF16) for e in E] for i in range(n)]
        AW = [[jnp.dot(Wb[i][e][:, :C], Wb[i][e], preferred_element_type=F32) for e in E] for i in range(n)]
        W = [[jnp.where(lo, 0.0, W[i][e]) + AW[i][e] for e in E] for i in range(n)]
    X = [[_dot(W[i][0], jnp.concatenate([zeros_cp, jnp.where(lo, win(at_f, c, q), zf[i][0])], axis=0)),
          _dot(W[i][1], jnp.concatenate([zeros_cp, jnp.where(lo, zf[i][1], win(at_f, c, q))], axis=0))]
         for i, (c, q) in enumerate(PR)]

    S = [s_scr[q] for q in range(N_PAIRS)]
    ys = []
    for c in range(NC):
        w_last = jnp.exp2(cum[(c + 1) * C - 1:(c + 1) * C, :])
        idx = [c * N_PAIRS + q for q in range(N_PAIRS)]
        mkg = [[_dot_tn(X[i][e], win(bh_b, c, q)) for e in E] for q, i in enumerate(idx)]
        vk = [_dot_tn(win(v_b, c, q), win(kh_b, c, q)) for q, i in enumerate(idx)]
        ry = [[_dot(m_r[i][0], jnp.concatenate([X[i][0], jnp.where(lo, 0.0, vr[i])], axis=0)),
               _dot(m_r[i][1], jnp.concatenate([X[i][1], jnp.where(lo, vr[i], 0.0)], axis=0))] for i in idx]
        rp = [win(rt_f, c, q) + jnp.where(lo, ry[q][0], ry[q][1]) for q in range(N_PAIRS)]
        y0 = [pltpu.roll(jnp.where(lo, ry[q][1], ry[q][0]), D, axis=1) for q in range(N_PAIRS)]
        mk = [jnp.where(diag, jnp.concatenate([mkg[q][0][:D], mkg[q][1][D:]], axis=0), 0.0) for q in range(N_PAIRS)]
        g = [jnp.where(diag, vk[q] + jnp.concatenate([mkg[q][0][D:], mkg[q][1][:D]], axis=0), 0.0)
             for q in range(N_PAIRS)]
        y = [_dot_nt(rp[q], S[q]) for q in range(N_PAIRS)]
        dS = [_dot(S[q], mk[q]) for q in range(N_PAIRS)]
        S = [S[q] * w_last[:, q * PAIR:(q + 1) * PAIR] + dS[q] + g[q] for q in range(N_PAIRS)]
        ys.append(jnp.concatenate([y[q] + y0[q] for q in range(N_PAIRS)], axis=1))
    for q in range(N_PAIRS):
        s_scr[q] = S[q]

    bonus = _seg_sum(r * k2 * rk_ref[...], blk) * v
    y_ref[0] = _group_norm_out(jnp.concatenate(ys, axis=0), bonus, gate, lnw_ref[...], lnb_ref[...], blk)

    @pl.when(t == pl.num_programs(1) - 1)
    def _():
        for q in range(N_PAIRS):
            sout_ref[0, 2 * q] = S[q][:D, :D]
            sout_ref[0, 2 * q + 1] = S[q][D:, D:]


def _rwkv_prompt(proj, mu, w0, a0, k_k, k_a, r_k, lnx_w, lnx_b, wupw, wupa, wupg):
    B, T, _ = proj.shape
    vec = lambda n: _full((1, n))
    return pl.pallas_call(
        _rwkv_prompt_kernel,
        grid=(B, T // RWKV_TILE),
        in_specs=[pl.BlockSpec((1, RWKV_TILE, RWKV_PROJ), lambda b, t: (b, t, 0)),
                  vec(RWKV_PROJ), vec(RWKV_WIDTH), vec(RWKV_WIDTH), vec(RWKV_WIDTH), vec(RWKV_WIDTH),
                  vec(RWKV_WIDTH), vec(RWKV_WIDTH), vec(RWKV_WIDTH),
                  _full((DECAY_LORA, RWKV_WIDTH)), _full((AAA_LORA, RWKV_WIDTH)), _full((GATE_LORA, RWKV_WIDTH))],
        out_specs=[pl.BlockSpec((1, RWKV_TILE, RWKV_WIDTH), lambda b, t: (b, t, 0)),
                   pl.BlockSpec((1, N_RWKV_HEADS, HEAD_DIM, HEAD_DIM), lambda b, t: (b, 0, 0, 0))],
        out_shape=[jax.ShapeDtypeStruct((B, T, RWKV_WIDTH), F32),
                   jax.ShapeDtypeStruct((B, N_RWKV_HEADS, HEAD_DIM, HEAD_DIM), F32)],
        scratch_shapes=[pltpu.VMEM((N_PAIRS, PAIR, PAIR), F32),
                        pltpu.VMEM((1, RWKV_PROJ), F32)],
        compiler_params=_params("arbitrary", "arbitrary"),
        name="rwkv_prompt",
    )(proj, mu, w0, a0, k_k, k_a, r_k, lnx_w, lnx_b, wupw, wupa, wupg)


def _rwkv_step_feat_kernel(p_ref, sh_ref, mu_ref, w0_ref, a0_ref, kk_ref, ka_ref, rk_ref,
                           wupw_ref, wupa_ref, wupg_ref, vecs_ref, bonus_ref, gate_ref):
    p = p_ref[...]
    xs = p + (sh_ref[...] - p) * mu_ref[...]
    r, k2, v, kk, a_sig, logw, gate = _rwkv_features(
        xs, w0_ref[...], a0_ref[...], kk_ref[...], ka_ref[...], wupw_ref[...], wupa_ref[...], wupg_ref[...])
    blk = _head_blocks(RWKV_WIDTH // 2)
    kkn = kk / jnp.maximum(jnp.sqrt(_seg_sum(kk * kk, blk)), 1e-12)
    for i, x in enumerate((-kkn, kkn * a_sig, jnp.exp(logw), k2, r, v)):
        vecs_ref[i] = x.T
    bonus_ref[...] = _seg_sum(r * k2 * rk_ref[...], blk) * v
    gate_ref[...] = gate


def _rwkv_step_state_kernel(s_ref, vecs_ref, sout_ref, y_ref):
    S = s_ref[0]
    a, b, w, k, r, v = (vecs_ref[i] for i in range(6))
    sa = jnp.sum(S * a[None], axis=1)
    S = S * w[None] + sa[:, None, :] * b[None] + v[:, None, :] * k[None]
    sout_ref[0] = S
    y_ref[...] = jnp.sum(S * r[None], axis=1)


def _rwkv_step_out_kernel(yt_ref, bonus_ref, gate_ref, lnw_ref, lnb_ref, o_ref):
    o_ref[...] = _group_norm_out(yt_ref[...].T, bonus_ref[...], gate_ref[...], lnw_ref[...], lnb_ref[...],
                                 _head_blocks(RWKV_WIDTH // 2))


def _rwkv_step(proj, shift, state_t, mu, w0, a0, k_k, k_a, r_k, lnx_w, lnx_b, wupw, wupa, wupg):
    n = proj.shape[0]
    vec = lambda m: _full((1, m))
    rows = _full((n, RWKV_PROJ))
    wide = _full((n, RWKV_WIDTH))
    vecs, bonus, gate = pl.pallas_call(
        _rwkv_step_feat_kernel,
        grid=(1,),
        in_specs=[rows, rows, vec(RWKV_PROJ), vec(RWKV_WIDTH), vec(RWKV_WIDTH), vec(RWKV_WIDTH), vec(RWKV_WIDTH),
                  vec(RWKV_WIDTH),
                  _full((DECAY_LORA, RWKV_WIDTH)), _full((AAA_LORA, RWKV_WIDTH)), _full((GATE_LORA, RWKV_WIDTH))],
        out_specs=[_full((6, RWKV_WIDTH, n)), wide, wide],
        out_shape=[jax.ShapeDtypeStruct((6, RWKV_WIDTH, n), F32),
                   jax.ShapeDtypeStruct((n, RWKV_WIDTH), F32), jax.ShapeDtypeStruct((n, RWKV_WIDTH), F32)],
        compiler_params=_params("arbitrary"),
        name="rwkv_step_feat",
    )(proj, shift, mu, w0, a0, k_k, k_a, r_k, wupw, wupa, wupg)
    st_spec = pl.BlockSpec((1, HEAD_DIM, HEAD_DIM, n), lambda h: (h, 0, 0, 0))
    state_new, yt = pl.pallas_call(
        _rwkv_step_state_kernel,
        grid=(N_RWKV_HEADS,),
        in_specs=[st_spec, pl.BlockSpec((6, HEAD_DIM, n), lambda h: (0, h, 0))],
        out_specs=[st_spec, pl.BlockSpec((HEAD_DIM, n), lambda h: (h, 0))],
        out_shape=[jax.ShapeDtypeStruct(state_t.shape, F32), jax.ShapeDtypeStruct((RWKV_WIDTH, n), F32)],
        compiler_params=_params("arbitrary"),
        name="rwkv_step_state",
    )(state_t, vecs)
    y = pl.pallas_call(
        _rwkv_step_out_kernel,
        grid=(1,),
        in_specs=[_full((RWKV_WIDTH, n)), wide, wide, vec(RWKV_WIDTH), vec(RWKV_WIDTH)],
        out_specs=wide,
        out_shape=jax.ShapeDtypeStruct((n, RWKV_WIDTH), F32),
        compiler_params=_params("arbitrary"),
        name="rwkv_step_out",
    )(yt, bonus, gate, lnx_w, lnx_b)
    return y, state_new


def _t5_bucket_np(dist):
    max_exact = N_BUCKETS // 2
    d = np.maximum(dist, 1).astype(np.float32)
    large = max_exact + (np.log(d / np.float32(max_exact)) / np.float32(math.log(MAX_DISTANCE / max_exact))
                         * np.float32(N_BUCKETS - max_exact)).astype(np.int32)
    large = np.minimum(large, N_BUCKETS - 1)
    return np.where(dist < max_exact, dist, large).astype(np.int32)


def _prompt_bucket_table():
    qi = np.arange(BLOCK)[:, None]
    kj = np.arange(2 * BLOCK)[None, :]
    dist = BLOCK + qi - kj
    valid = (dist >= 0) & (dist <= WINDOW)
    return np.where(valid, _t5_bucket_np(np.maximum(dist, 0)), -1).astype(np.int32)


def _prompt_bucket_row():
    table = _prompt_bucket_table()
    assert all(np.array_equal(table[q], np.roll(table[0], q)) for q in range(BLOCK))
    return np.broadcast_to(table[:1], (8, 2 * BLOCK)).copy()


def _decode_bucket_table():
    dist = WINDOW - np.arange(WINDOW)
    return np.broadcast_to(_t5_bucket_np(dist)[None, :], (8, WINDOW)).astype(np.int32).copy()


def _bias_from_buckets(bkt, relb_ref, h, init):
    acc = jnp.full(bkt.shape, init, F32)
    for b in range(N_BUCKETS):
        acc = jnp.where(bkt == b, relb_ref[b, h], acc)
    return acc


ATT_TILE = 4 * BLOCK


def _attn_init(bkt_ref, relb_ref, mk_ref, mv_ref, bias_scr, mk_scr, mv_scr):
    grp = N_SWA_HEADS // N_SWA_KV_HEADS
    ones = jnp.ones((N_MEM, HEAD_DIM), F32)
    bkt = bkt_ref[...]
    for j in range(N_SWA_KV_HEADS):
        for g in range(grp):
            row = _bias_from_buckets(bkt, relb_ref, j * grp + g, NEG)[0:1] * LOG2E
            bias_scr[j, g * BLOCK:(g + 1) * BLOCK, :] = pltpu.roll(
                jnp.broadcast_to(row, (BLOCK, 2 * BLOCK)), 0, 1, stride=1, stride_axis=0)
    mk = mk_ref[0]
    mv = mv_ref[0]
    for h in range(N_MEM_HEADS):
        sl = slice(h * HEAD_DIM, (h + 1) * HEAD_DIM)
        mk_scr[h] = mk[:, sl].astype(BF16)
        mv_scr[h] = jnp.concatenate([mv[:, sl], ones], axis=1).astype(BF16)


def _attn_tile(first, qs, qm, kc, kp, vc, vp, sqg, skg, mqg, sink_ref, bias_scr, mk_scr, mv_scr):
    grp = N_SWA_HEADS // N_SWA_KV_HEADS
    ones = jnp.ones((2 * BLOCK, HEAD_DIM), F32)
    rowi = lax.broadcasted_iota(jnp.int32, (2 * BLOCK, 1), 0)
    col = lax.broadcasted_iota(jnp.int32, (2 * BLOCK, 2 * BLOCK), 1)
    pad_mask = jnp.logical_and(first, col < BLOCK)
    hsl = [slice(h * HEAD_DIM, (h + 1) * HEAD_DIM) for h in range(N_SWA_HEADS)]
    qs_n = _rms_heads(qs, sqg, N_SWA_HEADS) * (ATTN_SCALE * LOG2E)
    qm_n = _rms_heads(qm, mqg, N_MEM_HEADS) * (ATTN_SCALE * LOG2E)
    kc_n = _rms_heads(kc, skg, N_SWA_KV_HEADS)
    kp_n = _rms_heads(kp, skg, N_SWA_KV_HEADS)
    chains =[(a, j) for a in range(ATT_TILE // BLOCK) for j in range(N_SWA_KV_HEADS)]
    lhs, keys, vals, sinkcol = [], [], [], []
    for a, j in chains:
        rs = slice(a * BLOCK, (a + 1) * BLOCK)
        lhs.append(jnp.concatenate([qs_n[rs, hsl[j * grp + g]] for g in range(grp)], axis=0))
        if a == 0:
            keys.append(jnp.concatenate([kp_n[:, hsl[j]], kc_n[:BLOCK, hsl[j]]], axis=0))
            vv = jnp.concatenate([vp[:, hsl[j]], vc[:BLOCK, hsl[j]]], axis=0)
        else:
            ks = slice((a - 1) * BLOCK, (a + 1) * BLOCK)
            keys.append(kc_n[ks, hsl[j]])
            vv = vc[ks, hsl[j]]
        vals.append(jnp.concatenate([vv, ones], axis=1))
        sinkcol.append(jnp.where(rowi < BLOCK, sink_ref[0, j * grp], sink_ref[0, j * grp + 1]) * LOG2E)
    qmn = [qm_n[:, hsl[h]] for h in range(N_MEM_HEADS)]

    bias = [bias_scr[j] for j in range(N_SWA_KV_HEADS)]
    mem_k = [mk_scr[h] for h in range(N_MEM_HEADS)]
    mem_v = [mv_scr[h] for h in range(N_MEM_HEADS)]
    n_w = len(chains)

    def scores(t):
        return _dot_nt(lhs[t], keys[t]) if t < n_w else _dot_nt(qmn[t - n_w], mem_k[t - n_w])

    def finish(ts, s):
        e, extra = {}, {}
        for t in ts:
            if t < n_w:
                a, j = chains[t]
                st = s[t] + bias[j]
                if a == 0:
                    st = jnp.where(pad_mask, NEG, st)
                m = jnp.maximum(jnp.max(st, -1, keepdims=True), sinkcol[t])
                e[t], extra[t] = jnp.exp2(st - m), jnp.exp2(sinkcol[t] - m)
            else:
                e[t], extra[t] = jnp.exp2(s[t] - jnp.max(s[t], -1, keepdims=True)), 0.0
        o_full = {t: _dot(e[t], vals[t] if t < n_w else mem_v[t - n_w]) for t in ts}
        return {t: o_full[t][:, :HEAD_DIM] / (o_full[t][:, HEAD_DIM:HEAD_DIM + 1] + extra[t]) for t in ts}

    def place(t):
        if t < n_w:
            a, j = chains[t]
            return [(slice(a * BLOCK, (a + 1) * BLOCK), hsl[j * grp + g], slice(g * BLOCK, (g + 1) * BLOCK))
                    for g in range(grp)]
        h = t - n_w
        return [(slice(0, ATT_TILE), slice(SWA_WIDTH + h * HEAD_DIM, SWA_WIDTH + (h + 1) * HEAD_DIM), slice(0, ATT_TILE))]

    return kc_n, n_w, n_w + N_MEM_HEADS, scores, finish, place


FF_CHUNK = 1024


def _ffn_chunk(h2, w1_ref, w2_ref, c):
    u = jnp.dot(h2, w1_ref[:, c * FF_CHUNK:(c + 1) * FF_CHUNK], preferred_element_type=F32)
    u = jnp.square(jnp.maximum(u, 0.0)).astype(BF16)
    return jnp.dot(u, w2_ref[c * FF_CHUNK:(c + 1) * FF_CHUNK, :], preferred_element_type=F32)


def _mix_rows(x, yr, ys, ym, wo_ref, g2_ref):
    x1 = (x
          + jnp.dot(yr.astype(BF16), wo_ref[0:RWKV_WIDTH, :], preferred_element_type=F32)
          + jnp.dot(ys.astype(BF16), wo_ref[RWKV_WIDTH:RWKV_WIDTH + SWA_WIDTH, :], preferred_element_type=F32)
          + jnp.dot(ym.astype(BF16), wo_ref[RWKV_WIDTH + SWA_WIDTH:, :], preferred_element_type=F32))
    return x1, _rms(x1, g2_ref[...]).astype(BF16)


def _attn_ffn_kernel(tiles_per_seq, qs_ref, qm_ref, kc_ref, kp_ref, vc_ref, vp_ref, mk_ref, mv_ref, bkt_ref, relb_ref,
                     sink_ref, sqg_ref, skg_ref, mqg_ref, x_ref, yr_ref, xs_ref, yrs_ref, yss_ref, yms_ref,
                     wo_ref, g2_ref, w1_ref, w2_ref, o_ref, os_ref, kn_ref, vn_ref, bias_scr, mk_scr, mv_scr, stage_scr):
    s = pl.program_id(0)
    nt = pl.num_programs(0) - 1
    tile = jnp.minimum(s, nt - 1)
    first = lax.rem(tile, tiles_per_seq) == 0
    wslot = lax.rem(s, 2)
    rslot = 1 - wslot

    @pl.when(jnp.logical_and(first, s < nt))
    def _():
        _attn_init(bkt_ref, relb_ref, mk_ref, mv_ref, bias_scr, mk_scr, mv_scr)

    def step(mlp_rows, store):
        kc_n, n_w, n_tasks, scores, finish, place = _attn_tile(
            first, qs_ref[0], qm_ref[0], kc_ref[0], kp_ref[0], vc_ref[0], vp_ref[0],
            sqg_ref[...], skg_ref[...], mqg_ref[...], sink_ref, bias_scr, mk_scr, mv_scr)
        kn_ref[0] = kc_n[ATT_TILE - BLOCK:].T
        vn_ref[0] = vc_ref[0, ATT_TILE - BLOCK:, :].T

        x1, h2 = _mix_rows(*mlp_rows(), wo_ref, g2_ref)
        sc = {t: scores(t) for t in range(n_tasks)}
        ff = _ffn_chunk(h2, w1_ref, w2_ref, 0)
        outs = finish(range(n_w), sc)
        ff = ff + _ffn_chunk(h2, w1_ref, w2_ref, 1)
        outs.update(finish(range(n_w, n_tasks), sc))
        for c in range(2, D_FF // FF_CHUNK):
            ff = ff + _ffn_chunk(h2, w1_ref, w2_ref, c)
        store(x1 + ff)
        for t in range(n_tasks):
            for rows, lanes, src in place(t):
                stage_scr[wslot, rows, lanes] = outs[t][src]

    def sample_rows():
        return xs_ref[:, 0, :], yrs_ref[...], yss_ref[...], yms_ref[...]

    def store_sample(v):
        os_ref[:, 0, :] = v

    def store_tile(v):
        o_ref[...] = v

    def tile_rows():
        staged = stage_scr[rslot]
        return x_ref[...], yr_ref[...], staged[:, :SWA_WIDTH], staged[:, SWA_WIDTH:]

    pl.when(s == 0)(functools.partial(step, sample_rows, store_sample))
    pl.when(s > 0)(functools.partial(step, tile_rows, store_tile))


def _attn_ffn(proj, mk, mv, rel_bias, sinks, sqg, skg, mqg, x2d, yr, sample, wo, g2, w1, w2):
    B, T, _ = proj.shape
    tps = T // ATT_TILE
    nt = B * tps
    bkt = jnp.asarray(_prompt_bucket_row())
    smem = pl.BlockSpec(memory_space=pltpu.SMEM)
    kblk, vblk = COL_SK // SWA_KV_WIDTH, COL_SV // SWA_KV_WIDTH
    att = lambda s: jnp.minimum(s, nt - 1)
    cur = lambda s, c: (att(s) // tps, att(s) % tps, c)
    prev = lambda s, c: (att(s) // tps, jnp.maximum((ATT_TILE // BLOCK) * (att(s) % tps) - 1, 0), c)
    ffn = lambda s: (jnp.maximum(s - 1, 0), 0)
    memb = pl.BlockSpec((1, N_MEM, MEM_WIDTH), lambda s: (att(s) // tps, 0, 0))
    const = lambda shape: pl.BlockSpec(shape, lambda s: (0, 0), pipeline_mode=pl.Buffered(1))
    return pl.pallas_call(
        functools.partial(_attn_ffn_kernel, tps),
        grid=(nt + 1,),
        in_specs=[pl.BlockSpec((1, ATT_TILE, SWA_WIDTH), lambda s: cur(s, COL_SQ // SWA_WIDTH)),
                  pl.BlockSpec((1, ATT_TILE, MEM_WIDTH), lambda s: cur(s, COL_MQ // MEM_WIDTH)),
                  pl.BlockSpec((1, ATT_TILE, SWA_KV_WIDTH), lambda s: cur(s, kblk)),
                  pl.BlockSpec((1, BLOCK, SWA_KV_WIDTH), lambda s: prev(s, kblk)),
                  pl.BlockSpec((1, ATT_TILE, SWA_KV_WIDTH), lambda s: cur(s, vblk)),
                  pl.BlockSpec((1, BLOCK, SWA_KV_WIDTH), lambda s: prev(s, vblk)),
                  memb, memb,
                  _full((8, 2 * BLOCK)), smem, smem,
                  _full((1, HEAD_DIM)), _full((1, HEAD_DIM)), _full((1, HEAD_DIM)),
                  pl.BlockSpec((ATT_TILE, D_MODEL), ffn), pl.BlockSpec((ATT_TILE, RWKV_WIDTH), ffn)]
                 + [_full(a.shape) for a in sample]
                 + [const((D_MODEL, D_MODEL)), _full((1, D_MODEL)), const((D_MODEL, D_FF)), const((D_FF, D_MODEL))],
        out_specs=[pl.BlockSpec((ATT_TILE, D_MODEL), ffn), _full(sample[0].shape)]
                  + [pl.BlockSpec((1, SWA_KV_WIDTH, BLOCK), lambda s: (att(s) // tps, 0, 0))] * 2,
        out_shape=[jax.ShapeDtypeStruct((B * T, D_MODEL), F32), jax.ShapeDtypeStruct(sample[0].shape, F32)]
                  + [jax.ShapeDtypeStruct((B, SWA_KV_WIDTH, BLOCK), F32)] * 2,
        scratch_shapes=[pltpu.VMEM((N_SWA_KV_HEADS, 2 * BLOCK, 2 * BLOCK), F32),
                        pltpu.VMEM((N_MEM_HEADS, N_MEM, HEAD_DIM), BF16),
                        pltpu.VMEM((N_MEM_HEADS, N_MEM, 2 * HEAD_DIM), BF16),
                        pltpu.VMEM((2, ATT_TILE, SWA_WIDTH + MEM_WIDTH), F32)],
        compiler_params=_params("arbitrary"),
        name="attn_ffn",
    )(proj, proj, proj, proj, proj, proj, mk, mv, bkt, rel_bias, sinks, sqg, skg, mqg, x2d, yr, *sample, wo, g2, w1, w2)


def _memory_kv_kernel(mem_ref, g_ref, w_ref, kg_ref, mk_ref, mv_ref, mkt_ref, mvt_ref):
    kv = _dot(_rms(mem_ref[0], g_ref[...]), w_ref[...])
    kg = kg_ref[...]
    mk = jnp.concatenate([_rms(kv[:, h * HEAD_DIM:(h + 1) * HEAD_DIM], kg) for h in range(N_MEM_HEADS)], axis=1)
    mv = kv[:, MEM_WIDTH:]
    mk_ref[0] = mk
    mv_ref[0] = mv
    mkt_ref[0] = mk.T
    mvt_ref[0] = mv.T


def _memory_kv(mem, g, w, kg):
    B = mem.shape[0]
    blk = pl.BlockSpec((1, N_MEM, MEM_WIDTH), lambda b: (b, 0, 0))
    blk_t = pl.BlockSpec((1, MEM_WIDTH, N_MEM), lambda b: (b, 0, 0))
    return pl.pallas_call(
        _memory_kv_kernel,
        grid=(B,),
        in_specs=[pl.BlockSpec((1, N_MEM, D_MODEL), lambda b: (b, 0, 0)),
                  _full((1, D_MODEL)), _full((D_MODEL, 2 * MEM_WIDTH)), _full((1, HEAD_DIM))],
        out_specs=[blk, blk, blk_t, blk_t],
        out_shape=[jax.ShapeDtypeStruct((B, N_MEM, MEM_WIDTH), F32)] * 2
                  + [jax.ShapeDtypeStruct((B, MEM_WIDTH, N_MEM), F32)] * 2,
        compiler_params=_params("arbitrary"),
        name="memory_kv",
    )(mem, g, w, kg)


SEQ_TILE = 8


def _rms_heads(x, g, n_heads):
    ms = jnp.dot((x * x).astype(BF16), _head_blocks(n_heads * HEAD_DIM), preferred_element_type=F32)
    return x * lax.rsqrt(ms * (1.0 / HEAD_DIM) + NORM_EPS) * jnp.concatenate([g] * n_heads, axis=1)


def _decode_attn_kernel(p_ref, kbuf_ref, vbuf_ref, mk_ref, mv_ref, bkt_ref, relb_ref, sink_ref,
                        sqg_ref, skg_ref, mqg_ref, ys_ref, ym_ref, kout_ref, vout_ref, tab_scr):
    grp = N_SWA_HEADS // N_SWA_KV_HEADS

    @pl.when(pl.program_id(0) == 0)
    def _():
        hrow = lax.broadcasted_iota(jnp.int32, (8, WINDOW), 0)
        lane = lax.broadcasted_iota(jnp.int32, (8, WINDOW), 1)
        bias_w = jnp.zeros((8, WINDOW), F32)
        cols = jnp.zeros((8, WINDOW), F32)
        for h in range(N_SWA_HEADS):
            bias_w = jnp.where(hrow == h, _bias_from_buckets(bkt_ref[...], relb_ref, h, 0.0), bias_w)
            cols = jnp.where(jnp.logical_and(hrow == h, lane == 0), relb_ref[0, h], cols)
            cols = jnp.where(jnp.logical_and(hrow == h, lane == 1), sink_ref[0, h], cols)
        tab_scr[0] = bias_w
        tab_scr[1] = cols

    bias_w = tab_scr[0][:N_SWA_HEADS]
    bias_new = tab_scr[1][:N_SWA_HEADS, 0:1]
    sink = tab_scr[1][:N_SWA_HEADS, 1:2]
    rowi = lax.broadcasted_iota(jnp.int32, (SWA_KV_WIDTH, WINDOW), 0)
    lanei = lax.broadcasted_iota(jnp.int32, (SWA_KV_WIDTH, WINDOW), 1)
    eye = rowi == lanei
    NH = N_SWA_HEADS
    own = (lax.broadcasted_iota(jnp.int32, (NH, NH * HEAD_DIM), 1) // HEAD_DIM
           == lax.broadcasted_iota(jnp.int32, (NH, NH * HEAD_DIM), 0))

    p = p_ref[...]
    qn = _rms_heads(p[:, COL_SQ:COL_SQ + SWA_WIDTH], sqg_ref[...], N_SWA_HEADS) * ATTN_SCALE
    kn = _rms_heads(p[:, COL_SK:COL_SK + SWA_KV_WIDTH], skg_ref[...], N_SWA_KV_HEADS)
    vn = p[:, COL_SV:COL_SV + SWA_KV_WIDTH]
    qmn = _rms_heads(p[:, COL_MQ:COL_MQ + MEM_WIDTH], mqg_ref[...], N_MEM_HEADS) * ATTN_SCALE
    rep = lambda x: jnp.concatenate([x[:, j * HEAD_DIM:(j + 1) * HEAD_DIM] for j in range(N_SWA_KV_HEADS)
                                     for _ in range(grp)], axis=1)
    kn_rep, vn_rep = rep(kn), rep(vn)

    B = range(SEQ_TILE)
    dup = lambda c: jnp.concatenate([c[j * HEAD_DIM:(j + 1) * HEAD_DIM] for j in range(N_SWA_KV_HEADS)
                                     for _ in range(grp)], axis=0)
    qd = [jnp.where(own, qn[b:b + 1, :], 0.0) for b in B]
    qmd = [jnp.where(own, qmn[b:b + 1, :], 0.0) for b in B]
    kdup = [dup(kbuf_ref[b]) for b in B]
    vdup = [dup(vbuf_ref[b]) for b in B]
    s = [_dot(qd[b], kdup[b]) + bias_w for b in B]
    sm = [_dot(qmd[b], mk_ref[b]) for b in B]
    s_new = [jnp.sum(qd[b] * kn_rep[b:b + 1, :], -1, keepdims=True) + bias_new for b in B]
    m = [jnp.maximum(jnp.maximum(jnp.max(s[b], -1, keepdims=True), s_new[b]), sink) for b in B]
    e = [jnp.exp(s[b] - m[b]) for b in B]
    e_new = [jnp.exp(s_new[b] - m[b]) for b in B]
    den = [jnp.sum(e[b], -1, keepdims=True) + e_new[b] + jnp.exp(sink - m[b]) for b in B]
    em = [jnp.exp(sm[b] - jnp.max(sm[b], -1, keepdims=True)) for b in B]
    ov = [_dot_nt(e[b], vdup[b]) for b in B]
    omf = [_dot_nt(em[b], mv_ref[b]) for b in B]
    ys_rows = [jnp.sum(jnp.where(own, (ov[b] + e_new[b] * vn_rep[b:b + 1, :]) / den[b], 0.0), 0, keepdims=True)
               for b in B]
    ym_rows = [jnp.sum(jnp.where(own, omf[b] / jnp.sum(em[b], -1, keepdims=True), 0.0), 0, keepdims=True)
               for b in B]
    ys_ref[...] = jnp.concatenate(ys_rows, axis=0)
    ym_ref[...] = jnp.concatenate(ym_rows, axis=0)
    for b in B:
        kn_col = jnp.sum(jnp.where(eye, kn[b:b + 1, :], 0.0), -1, keepdims=True)
        vn_col = jnp.sum(jnp.where(eye, vn[b:b + 1, :], 0.0), -1, keepdims=True)
        kout_ref[b] = jnp.where(lanei == WINDOW - 1, kn_col, pltpu.roll(kbuf_ref[b], WINDOW - 1, axis=1))
        vout_ref[b] = jnp.where(lanei == WINDOW - 1, vn_col, pltpu.roll(vbuf_ref[b], WINDOW - 1, axis=1))


def _decode_attn(proj, kbuf, vbuf, mk, mv, rel_bias, sinks, sqg, skg, mqg):
    n = proj.shape[0]
    bkt = jnp.asarray(_decode_bucket_table())
    smem = pl.BlockSpec(memory_space=pltpu.SMEM)
    win = pl.BlockSpec((SEQ_TILE, WINDOW, SWA_KV_WIDTH), lambda i: (i, 0, 0))
    memb = pl.BlockSpec((SEQ_TILE, N_MEM, MEM_WIDTH), lambda i: (i, 0, 0))
    return pl.pallas_call(
        _decode_attn_kernel,
        grid=(n // SEQ_TILE,),
        in_specs=[pl.BlockSpec((SEQ_TILE, IN_PROJ), lambda i: (i, 0)), win, win, memb, memb,
                  _full((8, WINDOW)), smem, smem, _full((1, HEAD_DIM)), _full((1, HEAD_DIM)), _full((1, HEAD_DIM))],
        out_specs=[pl.BlockSpec((SEQ_TILE, SWA_WIDTH), lambda i: (i, 0)),
                   pl.BlockSpec((SEQ_TILE, MEM_WIDTH), lambda i: (i, 0)), win, win],
        out_shape=[jax.ShapeDtypeStruct((n, SWA_WIDTH), F32), jax.ShapeDtypeStruct((n, MEM_WIDTH), F32),
                   jax.ShapeDtypeStruct(kbuf.shape, F32), jax.ShapeDtypeStruct(vbuf.shape, F32)],
        scratch_shapes=[pltpu.VMEM((2, 8, WINDOW), F32)],
        compiler_params=_params("arbitrary"),
        name="decode_attn",
    )(proj, kbuf, vbuf, mk, mv, bkt, rel_bias, sinks, sqg, skg, mqg)


def kernel(x_prompt, x_sample, state_rwkv, state_shift, cache_swa_k, cache_swa_v, cache_mem_k, cache_mem_v,
           mem_prompt, rel_bias, norm1_g, w_in, mu_shift, w0, w_up_w, a0, w_up_a, w_up_g, k_k, k_a, r_k,
           lnx_w, lnx_b, q_norm_swa, k_norm_swa, sinks, mem_norm_g, w_mem_kv, q_norm_mem, k_norm_mem,
           w_out, norm2_g, w_ff1, w_ff2):
    B, T, _ = x_prompt.shape
    Bd = x_sample.shape[0]
    l = 0
    rwkv_params = (mu_shift[l][None], w0[l][None], a0[l][None], k_k[l][None], k_a[l][None],
                   r_k[l].reshape(1, RWKV_WIDTH), lnx_w[l][None], lnx_b[l][None],
                   w_up_w[l], w_up_a[l], w_up_g[l])
    sqg, skg, mqg, mkg = q_norm_swa[l][None], k_norm_swa[l][None], q_norm_mem[l][None], k_norm_mem[l][None]
    g1, g2 = norm1_g[l][None], norm2_g[l][None]

    xp = x_prompt.reshape(B * T, D_MODEL)
    xs = x_sample
    proj_p, proj_s, proj_s_rwkv, (w_out_b, w1_b, w2_b) = _in_proj(
        xp, xs, g1, w_in[l], (w_out[l], w_ff1[l], w_ff2[l]), IN_PROJ_TILE)
    proj_p = proj_p.reshape(B, T, IN_PROJ)

    mk, mv, mk_t, mv_t = _memory_kv(mem_prompt, mem_norm_g[l][None], w_mem_kv[l], mkg)
    yr_p, s_p = _rwkv_prompt(proj_p, *rwkv_params)
    shift_p = proj_p[:, T - 1, :RWKV_PROJ]

    yr_s, st_s = _rwkv_step(proj_s_rwkv, state_shift[l], jnp.transpose(state_rwkv[l], (1, 2, 3, 0)), *rwkv_params)
    s_s = jnp.transpose(st_s, (3, 0, 1, 2))
    fmajor = lambda c: jnp.transpose(c, (0, 2, 3, 1)).reshape(Bd, c.shape[2] * HEAD_DIM, c.shape[1])
    ys_s, ym_s, kb_s, vb_s = _decode_attn(
        proj_s, fmajor(cache_swa_k[l]), fmajor(cache_swa_v[l]), fmajor(cache_mem_k[l]), fmajor(cache_mem_v[l]),
        rel_bias, sinks[l][None], sqg, skg, mqg)
    pmajor = lambda c: jnp.transpose(c.reshape(c.shape[0], -1, HEAD_DIM, c.shape[2]), (0, 3, 1, 2))[None]

    y_p, y_s, kn_p, vn_p = _attn_ffn(proj_p, mk, mv, rel_bias, sinks[l][None], sqg, skg, mqg,
                               xp, yr_p.reshape(B * T, RWKV_WIDTH), (xs, yr_s, ys_s, ym_s), w_out_b, g2, w1_b, w2_b)
    y_p = y_p.reshape(B, T, D_MODEL)

    return (y_p, y_s,
            s_p[None], shift_p[None],
            pmajor(kn_p), pmajor(vn_p), pmajor(mk_t), pmajor(mv_t),
            s_s[None], proj_s_rwkv[None],
            pmajor(kb_s), pmajor(vb_s))
```

```python
import functools
import math

import numpy as np
import jax
import jax.numpy as jnp
from jax import lax
from jax.experimental import pallas as pl
from jax.experimental.pallas import tpu as pltpu

F32 = jnp.float32
BF16 = jnp.bfloat16

D_MODEL = 1024
HEAD_DIM = 64
RWKV_WIDTH = 512
N_RWKV_HEADS = 8
SWA_WIDTH = 256
N_SWA_HEADS = 4
N_SWA_KV_HEADS = 2
SWA_KV_WIDTH = 128
MEM_WIDTH = 256
N_MEM_HEADS = 4
N_MEM = 256
WINDOW = 128
BLOCK = 128
N_BUCKETS = 32
MAX_DISTANCE = 128
DECAY_LORA = 64
AAA_LORA = 64
GATE_LORA = 128
RWKV_PROJ = 3 * RWKV_WIDTH + DECAY_LORA + AAA_LORA + GATE_LORA
SWA_PROJ = SWA_WIDTH + 2 * SWA_KV_WIDTH
IN_PROJ = RWKV_PROJ + SWA_PROJ + MEM_WIDTH
D_FF = 4 * D_MODEL
NORM_EPS = 1e-6
LNX_EPS = 64e-5
ATTN_SCALE = HEAD_DIM ** -0.5
EXP_M05 = math.exp(-0.5)
LOG2E = math.log2(math.e)
NEG = -1e30

COL_R, COL_K, COL_V = 0, RWKV_WIDTH, 2 * RWKV_WIDTH
COL_WD = 3 * RWKV_WIDTH
COL_AD = COL_WD + DECAY_LORA
COL_GD = COL_AD + AAA_LORA
COL_SQ = RWKV_PROJ
COL_SK = COL_SQ + SWA_WIDTH
COL_SV = COL_SK + SWA_KV_WIDTH
COL_MQ = RWKV_PROJ + SWA_PROJ

CHUNK = 64
VMEM_LIMIT = 56 * 1024 * 1024


def _dot(a, b):
    return jnp.dot(a.astype(BF16), b.astype(BF16), preferred_element_type=F32)


def _dot_nt(a, b):
    return lax.dot_general(a.astype(BF16), b.astype(BF16), (((1,), (1,)), ((), ())),
                           preferred_element_type=F32)


def _dot_tn(a, b):
    return lax.dot_general(a.astype(BF16), b.astype(BF16), (((0,), (0,)), ((), ())),
                           preferred_element_type=F32)


def _rms(x, g):
    return x * lax.rsqrt(jnp.mean(x * x, -1, keepdims=True) + NORM_EPS) * g


def _params(*sem):
    return pltpu.CompilerParams(dimension_semantics=sem, vmem_limit_bytes=VMEM_LIMIT)


def _full(shape):
    n = len(shape)
    return pl.BlockSpec(shape, lambda *_: (0,) * n)


def _head_blocks(width):
    bi = lax.broadcasted_iota(jnp.int32, (width, width), 0) // HEAD_DIM
    bj = lax.broadcasted_iota(jnp.int32, (width, width), 1) // HEAD_DIM
    return jnp.where(bi == bj, 1.0, 0.0).astype(BF16)


IN_PROJ_TILE = 1024
IN_PROJ_SUB = 256


def _in_proj_kernel(x_ref, xs_ref, g_ref, w_ref, wo_ref, w1_ref, w2_ref, o_ref, os_ref, osr_ref, wob_ref, w1b_ref,
                    w2b_ref, wb_scr):
    @pl.when(pl.program_id(0) == 0)
    def _():
        wb_scr[...] = w_ref[...].astype(BF16)

    wob_ref[...] = wo_ref[...].astype(BF16)
    w1b_ref[...] = w1_ref[...].astype(BF16)
    w2b_ref[...] = w2_ref[...].astype(BF16)

    tm = x_ref.shape[0]
    for j in range(tm // IN_PROJ_SUB):
        rows = slice(j * IN_PROJ_SUB, (j + 1) * IN_PROJ_SUB)
        h = _rms(x_ref[rows, :], g_ref[...])
        o_ref[rows, :] = jnp.dot(h.astype(BF16), wb_scr[...], preferred_element_type=F32)

    @pl.when(pl.program_id(0) == pl.num_programs(0) - 1)
    def _():
        hs = _rms(xs_ref[:, 0, :], g_ref[...])
        ps = jnp.dot(hs.astype(BF16), wb_scr[...], preferred_element_type=F32)
        os_ref[...] = ps
        osr_ref[...] = ps[:, :RWKV_PROJ]


def _in_proj(x2d, xs, g, w, later_weights, tm):
    n, ns = x2d.shape[0], xs.shape[0]
    steps = n // tm
    slab = lambda m: pl.BlockSpec((m.shape[0] // steps, m.shape[1]), lambda i: (i, 0))
    proj_p, proj_s, proj_s_rwkv, *later_bf16 = pl.pallas_call(
        _in_proj_kernel,
        grid=(steps,),
        in_specs=[pl.BlockSpec((tm, D_MODEL), lambda i: (i, 0)),
                  _full((ns, 1, D_MODEL)),
                  _full((1, D_MODEL)),
                  pl.BlockSpec((D_MODEL, IN_PROJ), lambda i: (0, 0), pipeline_mode=pl.Buffered(1))]
                 + [slab(m) for m in later_weights],
        out_specs=[pl.BlockSpec((tm, IN_PROJ), lambda i: (i, 0)), _full((ns, IN_PROJ)), _full((ns, RWKV_PROJ))]
                  + [slab(m) for m in later_weights],
        out_shape=[jax.ShapeDtypeStruct((n, IN_PROJ), F32), jax.ShapeDtypeStruct((ns, IN_PROJ), F32),
                   jax.ShapeDtypeStruct((ns, RWKV_PROJ), F32)]
                  + [jax.ShapeDtypeStruct(m.shape, BF16) for m in later_weights],
        scratch_shapes=[pltpu.VMEM((D_MODEL, IN_PROJ), BF16)],
        compiler_params=_params("arbitrary"),
        name="in_proj",
    )(x2d, xs, g, w, *later_weights)
    return proj_p, proj_s, proj_s_rwkv, later_bf16


def _rwkv_features(xs, w0, a0, k_k, k_a, wupw, wupa, wupg):
    r = xs[:, COL_R:COL_R + RWKV_WIDTH]
    k = xs[:, COL_K:COL_K + RWKV_WIDTH]
    v = xs[:, COL_V:COL_V + RWKV_WIDTH]
    wd = xs[:, COL_WD:COL_WD + DECAY_LORA]
    ad = xs[:, COL_AD:COL_AD + AAA_LORA]
    gd = xs[:, COL_GD:COL_GD + GATE_LORA]
    logw = -jax.nn.sigmoid(w0 + _dot(jnp.tanh(wd), wupw)) * EXP_M05
    a_sig = jax.nn.sigmoid(a0 + _dot(ad, wupa))
    gate = _dot(jax.nn.sigmoid(gd), wupg)
    kk = k * k_k
    k2 = k * (1.0 + (a_sig - 1.0) * k_a)
    return r, k2, v, kk, a_sig, logw, gate


def _seg_sum(x, blk):
    xb = x.astype(BF16)
    half = RWKV_WIDTH // 2
    return jnp.concatenate([jnp.dot(xb[:, :half], blk, preferred_element_type=F32),
                            jnp.dot(xb[:, half:], blk, preferred_element_type=F32)], axis=1)


def _group_norm_out(y, bonus, gate, lnx_w, lnx_b, blk):
    inv_d = 1.0 / HEAD_DIM
    m = _seg_sum(y, blk) * inv_d
    d = y - m
    var = _seg_sum(d * d, blk) * inv_d
    yn = d * lax.rsqrt(var + LNX_EPS) * lnx_w + lnx_b
    return (yn + bonus) * gate


RWKV_TILE = 4 * CHUNK
PAIR = 2 * HEAD_DIM
N_PAIRS = N_RWKV_HEADS // 2


def _rwkv_prompt_kernel(p_ref, mu_ref, w0_ref, a0_ref, kk_ref, ka_ref, rk_ref, lnw_ref, lnb_ref,
                        wupw_ref, wupa_ref, wupg_ref, y_ref, sout_ref, s_scr, prev_scr):
    C, TT, D = CHUNK, RWKV_TILE, HEAD_DIM
    NC = TT // C
    t = pl.program_id(1)

    @pl.when(t == 0)
    def _():
        s_scr[...] = jnp.zeros_like(s_scr)
        prev_scr[...] = jnp.zeros_like(prev_scr)

    p = p_ref[0]
    row = lax.broadcasted_iota(jnp.int32, p.shape, 0)
    prev = jnp.where(row == 0, prev_scr[...], pltpu.roll(p, 1, axis=0))
    prev_scr[...] = p[TT - 1:TT, :]
    xs = p + (prev - p) * mu_ref[...]
    r, k2, v, kk, a_sig, logw, gate = _rwkv_features(
        xs, w0_ref[...], a0_ref[...], kk_ref[...], ka_ref[...], wupw_ref[...], wupa_ref[...], wupg_ref[...])

    blk = _head_blocks(RWKV_WIDTH // 2)
    kkn = kk / jnp.maximum(jnp.sqrt(_seg_sum(kk * kk, blk)), 1e-12)
    bb = kkn * a_sig

    ri = lax.broadcasted_iota(jnp.int32, (TT, TT), 0)
    ci = lax.broadcasted_iota(jnp.int32, (TT, TT), 1)
    tri = jnp.where(jnp.logical_and(ri >= ci, ri // C == ci // C), 1.0, 0.0).astype(BF16)
    lw2 = logw * LOG2E
    l1 = lw2.astype(BF16)
    l2 = (lw2 - l1.astype(F32)).astype(BF16)
    cum = jnp.dot(tri, l1, preferred_element_type=F32) + jnp.dot(tri, l2, preferred_element_type=F32)
    c_last = jnp.concatenate([jnp.broadcast_to(cum[(c + 1) * C - 1:(c + 1) * C, :], (C, RWKV_WIDTH))
                              for c in range(NC)], axis=0)
    e_pos = jnp.exp2(cum)
    e_neg = jnp.exp2(-cum)
    e_prev = jnp.exp2(cum - lw2)
    e_last = jnp.exp2(c_last - cum)

    lo_full = (lax.broadcasted_iota(jnp.int32, (TT, RWKV_WIDTH), 1) % PAIR) < D
    at_f = -kkn * e_prev
    rt_f = r * e_pos
    at_lo = jnp.where(lo_full, at_f, 0.0).astype(BF16)
    at_hi = jnp.where(lo_full, 0.0, at_f).astype(BF16)
    rt_lo = jnp.where(lo_full, rt_f, 0.0).astype(BF16)
    rt_hi = jnp.where(lo_full, 0.0, rt_f).astype(BF16)
    bt_b = (bb * e_neg).astype(BF16)
    kt_b = (k2 * e_neg).astype(BF16)
    bh_b = (bb * e_last).astype(BF16)
    kh_b = (k2 * e_last).astype(BF16)
    v_b = v.astype(BF16)

    r2 = lax.broadcasted_iota(jnp.int32, (C, PAIR), 0)
    c2 = lax.broadcasted_iota(jnp.int32, (C, PAIR), 1)
    lo = c2 < D
    c2m = jnp.where(lo, c2, c2 - C)
    mask_a = jnp.logical_and(lo, c2 < r2)
    mask_ak = jnp.logical_and(jnp.logical_not(lo), c2m < r2)
    mask_r = c2m <= r2
    eye_hi = jnp.where(jnp.logical_and(jnp.logical_not(lo), c2m == r2), 1.0, 0.0)
    zeros_cp = jnp.zeros((C, PAIR), F32)
    qi = lax.broadcasted_iota(jnp.int32, (PAIR, PAIR), 0) // D
    qj = lax.broadcasted_iota(jnp.int32, (PAIR, PAIR), 1) // D
    diag = qi == qj

    PR = [(c, q) for c in range(NC) for q in range(N_PAIRS)]
    n = len(PR)
    E = range(2)
    win = lambda x, c, q: x[c * C:(c + 1) * C, q * PAIR:(q + 1) * PAIR]
    sc = [_dot_nt(jnp.concatenate([win(at_lo, c, q), win(at_hi, c, q), win(rt_lo, c, q), win(rt_hi, c, q)], axis=0),
                  jnp.concatenate([win(bt_b, c, q), win(kt_b, c, q)], axis=0)) for c, q in PR]
    vr = [pltpu.roll(win(v, c, q), D, axis=1) for c, q in PR]
    vvr = [jnp.concatenate([vr[i], vr[i]], axis=0).astype(BF16) for i in range(n)]
    m_ak = [[jnp.where(mask_ak, sc[i][e * C:(e + 1) * C], 0.0) for e in E] for i in range(n)]
    m_r = [[jnp.where(mask_r, sc[i][(2 + e) * C:(3 + e) * C], 0.0) for e in E] for i in range(n)]

    zf = [[_dot(m_ak[i][e], vvr[i]) for e in E] for i in range(n)]
    W = [[jnp.where(mask_a, sc[i][e * C:(e + 1) * C], eye_hi) for e in E] for i in range(n)]
    for k in range(int(math.log2(C))):
        Wb = [[W[i][e].astype(BF16) for e in E] for i in range(n)]
        AW = [[jnp.dot(Wb[i][e][:, :C], Wb[i][e], preferred_element_type=F32) for e in E] for i in range(n)]
        W = [[jnp.where(lo, 0.0, W[i][e]) + AW[i][e] for e in E] for i in range(n)]
    X = [[_dot(W[i][0], jnp.concatenate([zeros_cp, jnp.where(lo, win(at_f, c, q), zf[i][0])], axis=0)),
          _dot(W[i][1], jnp.concatenate([zeros_cp, jnp.where(lo, zf[i][1], win(at_f, c, q))], axis=0))]
         for i, (c, q) in enumerate(PR)]

    S = [s_scr[q] for q in range(N_PAIRS)]
    ys = []
    for c in range(NC):
        w_last = jnp.exp2(cum[(c + 1) * C - 1:(c + 1) * C, :])
        idx = [c * N_PAIRS + q for q in range(N_PAIRS)]
        mkg = [[_dot_tn(X[i][e], win(bh_b, c, q)) for e in E] for q, i in enumerate(idx)]
        vk = [_dot_tn(win(v_b, c, q), win(kh_b, c, q)) for q, i in enumerate(idx)]
        ry = [[_dot(m_r[i][0], jnp.concatenate([X[i][0], jnp.where(lo, 0.0, vr[i])], axis=0)),
               _dot(m_r[i][1], jnp.concatenate([X[i][1], jnp.where(lo, vr[i], 0.0)], axis=0))] for i in idx]
        rp = [win(rt_f, c, q) + jnp.where(lo, ry[q][0], ry[q][1]) for q in range(N_PAIRS)]
        y0 = [pltpu.roll(jnp.where(lo, ry[q][1], ry[q][0]), D, axis=1) for q in range(N_PAIRS)]
        mk = [jnp.where(diag, jnp.concatenate([mkg[q][0][:D], mkg[q][1][D:]], axis=0), 0.0) for q in range(N_PAIRS)]
        g = [jnp.where(diag, vk[q] + jnp.concatenate([mkg[q][0][D:], mkg[q][1][:D]], axis=0), 0.0)
             for q in range(N_PAIRS)]
        y = [_dot_nt(rp[q], S[q]) for q in range(N_PAIRS)]
        dS = [_dot(S[q], mk[q]) for q in range(N_PAIRS)]
        S = [S[q] * w_last[:, q * PAIR:(q + 1) * PAIR] + dS[q] + g[q] for q in range(N_PAIRS)]
        ys.append(jnp.concatenate([y[q] + y0[q] for q in range(N_PAIRS)], axis=1))
    for q in range(N_PAIRS):
        s_scr[q] = S[q]

    bonus = _seg_sum(r * k2 * rk_ref[...], blk) * v
    y_ref[0] = _group_norm_out(jnp.concatenate(ys, axis=0), bonus, gate, lnw_ref[...], lnb_ref[...], blk)

    @pl.when(t == pl.num_programs(1) - 1)
    def _():
        for q in range(N_PAIRS):
            sout_ref[0, 2 * q] = S[q][:D, :D]
            sout_ref[0, 2 * q + 1] = S[q][D:, D:]


def _rwkv_prompt(proj, mu, w0, a0, k_k, k_a, r_k, lnx_w, lnx_b, wupw, wupa, wupg):
    B, T, _ = proj.shape
    vec = lambda n: _full((1, n))
    return pl.pallas_call(
        _rwkv_prompt_kernel,
        grid=(B, T // RWKV_TILE),
        in_specs=[pl.BlockSpec((1, RWKV_TILE, RWKV_PROJ), lambda b, t: (b, t, 0)),
                  vec(RWKV_PROJ), vec(RWKV_WIDTH), vec(RWKV_WIDTH), vec(RWKV_WIDTH), vec(RWKV_WIDTH),
                  vec(RWKV_WIDTH), vec(RWKV_WIDTH), vec(RWKV_WIDTH),
                  _full((DECAY_LORA, RWKV_WIDTH)), _full((AAA_LORA, RWKV_WIDTH)), _full((GATE_LORA, RWKV_WIDTH))],
        out_specs=[pl.BlockSpec((1, RWKV_TILE, RWKV_WIDTH), lambda b, t: (b, t, 0)),
                   pl.BlockSpec((1, N_RWKV_HEADS, HEAD_DIM, HEAD_DIM), lambda b, t: (b, 0, 0, 0))],
        out_shape=[jax.ShapeDtypeStruct((B, T, RWKV_WIDTH), F32),
                   jax.ShapeDtypeStruct((B, N_RWKV_HEADS, HEAD_DIM, HEAD_DIM), F32)],
        scratch_shapes=[pltpu.VMEM((N_PAIRS, PAIR, PAIR), F32),
                        pltpu.VMEM((1, RWKV_PROJ), F32)],
        compiler_params=_params("arbitrary", "arbitrary"),
        name="rwkv_prompt",
    )(proj, mu, w0, a0, k_k, k_a, r_k, lnx_w, lnx_b, wupw, wupa, wupg)


def _rwkv_step_kernel(p_ref, sh_ref, mu_ref, w0_ref, a0_ref, kk_ref, ka_ref, rk_ref, lnw_ref, lnb_ref,
                      wupw_ref, wupa_ref, wupg_ref, s_ref, sout_ref, y_ref, vecs_scr, bonus_scr, gate_scr, yt_scr):
    h = pl.program_id(0)
    blk = _head_blocks(RWKV_WIDTH // 2)

    @pl.when(h == 0)
    def _():
        p = p_ref[...]
        xs = p + (sh_ref[...] - p) * mu_ref[...]
        r, k2, v, kk, a_sig, logw, gate = _rwkv_features(
            xs, w0_ref[...], a0_ref[...], kk_ref[...], ka_ref[...], wupw_ref[...], wupa_ref[...], wupg_ref[...])
        kkn = kk / jnp.maximum(jnp.sqrt(_seg_sum(kk * kk, blk)), 1e-12)
        for i, x in enumerate((-kkn, kkn * a_sig, jnp.exp(logw), k2, r, v)):
            vecs_scr[i] = x.T
        bonus_scr[...] = _seg_sum(r * k2 * rk_ref[...], blk) * v
        gate_scr[...] = gate

    rows = pl.ds(pl.multiple_of(h * HEAD_DIM, HEAD_DIM), HEAD_DIM)
    S = s_ref[0]
    a, b, w, k, r, v = (vecs_scr[i, rows, :] for i in range(6))
    sa = jnp.sum(S * a[None], axis=1)
    S = S * w[None] + sa[:, None, :] * b[None] + v[:, None, :] * k[None]
    sout_ref[0] = S
    yt_scr[rows, :] = jnp.sum(S * r[None], axis=1)

    @pl.when(h == pl.num_programs(0) - 1)
    def _():
        y_ref[...] = _group_norm_out(yt_scr[...].T, bonus_scr[...], gate_scr[...], lnw_ref[...], lnb_ref[...], blk)


def _rwkv_step(proj, shift, state_t, mu, w0, a0, k_k, k_a, r_k, lnx_w, lnx_b, wupw, wupa, wupg):
    n = proj.shape[0]
    vec = lambda m: _full((1, m))
    rows = _full((n, RWKV_PROJ))
    wide = _full((n, RWKV_WIDTH))
    st_spec = pl.BlockSpec((1, HEAD_DIM, HEAD_DIM, n), lambda h: (h, 0, 0, 0))
    state_new, y = pl.pallas_call(
        _rwkv_step_kernel,
        grid=(N_RWKV_HEADS,),
        in_specs=[rows, rows, vec(RWKV_PROJ)] + [vec(RWKV_WIDTH)] * 7
                 + [_full((DECAY_LORA, RWKV_WIDTH)), _full((AAA_LORA, RWKV_WIDTH)), _full((GATE_LORA, RWKV_WIDTH)),
                    st_spec],
        out_specs=[st_spec, wide],
        out_shape=[jax.ShapeDtypeStruct(state_t.shape, F32), jax.ShapeDtypeStruct((n, RWKV_WIDTH), F32)],
        scratch_shapes=[pltpu.VMEM((6, RWKV_WIDTH, n), F32), pltpu.VMEM((n, RWKV_WIDTH), F32),
                        pltpu.VMEM((n, RWKV_WIDTH), F32), pltpu.VMEM((RWKV_WIDTH, n), F32)],
        compiler_params=_params("arbitrary"),
        name="rwkv_step",
    )(proj, shift, mu, w0, a0, k_k, k_a, r_k, lnx_w, lnx_b, wupw, wupa, wupg, state_t)
    return y, state_new


def _t5_bucket_np(dist):
    max_exact = N_BUCKETS // 2
    d = np.maximum(dist, 1).astype(np.float32)
    large = max_exact + (np.log(d / np.float32(max_exact)) / np.float32(math.log(MAX_DISTANCE / max_exact))
                         * np.float32(N_BUCKETS - max_exact)).astype(np.int32)
    large = np.minimum(large, N_BUCKETS - 1)
    return np.where(dist < max_exact, dist, large).astype(np.int32)


def _prompt_bucket_table():
    qi = np.arange(BLOCK)[:, None]
    kj = np.arange(2 * BLOCK)[None, :]
    dist = BLOCK + qi - kj
    valid = (dist >= 0) & (dist <= WINDOW)
    return np.where(valid, _t5_bucket_np(np.maximum(dist, 0)), -1).astype(np.int32)


def _prompt_bucket_row():
    table = _prompt_bucket_table()
    assert all(np.array_equal(table[q], np.roll(table[0], q)) for q in range(BLOCK))
    return np.broadcast_to(table[:1], (8, 2 * BLOCK)).copy()


def _decode_bucket_table():
    dist = WINDOW - np.arange(WINDOW)
    return np.broadcast_to(_t5_bucket_np(dist)[None, :], (8, WINDOW)).astype(np.int32).copy()


def _bias_from_buckets(bkt, relb_ref, h, init):
    acc = jnp.full(bkt.shape, init, F32)
    for b in range(N_BUCKETS):
        acc = jnp.where(bkt == b, relb_ref[b, h], acc)
    return acc


ATT_TILE = 4 * BLOCK


def _attn_init(bkt_ref, relb_ref, mk_ref, mv_ref, bias_scr, mk_scr, mv_scr):
    grp = N_SWA_HEADS // N_SWA_KV_HEADS
    ones = jnp.ones((N_MEM, HEAD_DIM), F32)
    bkt = bkt_ref[...]
    for j in range(N_SWA_KV_HEADS):
        for g in range(grp):
            row = _bias_from_buckets(bkt, relb_ref, j * grp + g, NEG)[0:1] * LOG2E
            bias_scr[j, g * BLOCK:(g + 1) * BLOCK, :] = pltpu.roll(
                jnp.broadcast_to(row, (BLOCK, 2 * BLOCK)), 0, 1, stride=1, stride_axis=0)
    mk = mk_ref[0]
    mv = mv_ref[0]
    for h in range(N_MEM_HEADS):
        sl = slice(h * HEAD_DIM, (h + 1) * HEAD_DIM)
        mk_scr[h] = mk[:, sl].astype(BF16)
        mv_scr[h] = jnp.concatenate([mv[:, sl], ones], axis=1).astype(BF16)


def _attn_tile(first, qs, qm, kc, kp, vc, vp, sqg, skg, mqg, sink_ref, bias_scr, mk_scr, mv_scr):
    grp = N_SWA_HEADS // N_SWA_KV_HEADS
    ones = jnp.ones((2 * BLOCK, HEAD_DIM), F32)
    rowi = lax.broadcasted_iota(jnp.int32, (2 * BLOCK, 1), 0)
    col = lax.broadcasted_iota(jnp.int32, (2 * BLOCK, 2 * BLOCK), 1)
    pad_mask = jnp.logical_and(first, col < BLOCK)
    hsl = [slice(h * HEAD_DIM, (h + 1) * HEAD_DIM) for h in range(N_SWA_HEADS)]
    qs_n = _rms_heads(qs, sqg, N_SWA_HEADS) * (ATTN_SCALE * LOG2E)
    qm_n = _rms_heads(qm, mqg, N_MEM_HEADS) * (ATTN_SCALE * LOG2E)
    kc_n = _rms_heads(kc, skg, N_SWA_KV_HEADS)
    kp_n = _rms_heads(kp, skg, N_SWA_KV_HEADS)
    chains =[(a, j) for a in range(ATT_TILE // BLOCK) for j in range(N_SWA_KV_HEADS)]
    lhs, keys, vals, sinkcol = [], [], [], []
    for a, j in chains:
        rs = slice(a * BLOCK, (a + 1) * BLOCK)
        lhs.append(jnp.concatenate([qs_n[rs, hsl[j * grp + g]] for g in range(grp)], axis=0))
        if a == 0:
            keys.append(jnp.concatenate([kp_n[:, hsl[j]], kc_n[:BLOCK, hsl[j]]], axis=0))
            vv = jnp.concatenate([vp[:, hsl[j]], vc[:BLOCK, hsl[j]]], axis=0)
        else:
            ks = slice((a - 1) * BLOCK, (a + 1) * BLOCK)
            keys.append(kc_n[ks, hsl[j]])
            vv = vc[ks, hsl[j]]
        vals.append(jnp.concatenate([vv, ones], axis=1))
        sinkcol.append(jnp.where(rowi < BLOCK, sink_ref[0, j * grp], sink_ref[0, j * grp + 1]) * LOG2E)
    qmn = [qm_n[:, hsl[h]] for h in range(N_MEM_HEADS)]

    bias = [bias_scr[j] for j in range(N_SWA_KV_HEADS)]
    mem_k = [mk_scr[h] for h in range(N_MEM_HEADS)]
    mem_v = [mv_scr[h] for h in range(N_MEM_HEADS)]
    n_w = len(chains)

    def scores(t):
        return _dot_nt(lhs[t], keys[t]) if t < n_w else _dot_nt(qmn[t - n_w], mem_k[t - n_w])

    def finish(ts, s):
        e, extra = {}, {}
        for t in ts:
            if t < n_w:
                a, j = chains[t]
                st = s[t] + bias[j]
                if a == 0:
                    st = jnp.where(pad_mask, NEG, st)
                m = jnp.maximum(jnp.max(st, -1, keepdims=True), sinkcol[t])
                e[t], extra[t] = jnp.exp2(st - m), jnp.exp2(sinkcol[t] - m)
            else:
                e[t], extra[t] = jnp.exp2(s[t] - jnp.max(s[t], -1, keepdims=True)), 0.0
        o_full = {t: _dot(e[t], vals[t] if t < n_w else mem_v[t - n_w]) for t in ts}
        return {t: o_full[t][:, :HEAD_DIM] / (o_full[t][:, HEAD_DIM:HEAD_DIM + 1] + extra[t]) for t in ts}

    def place(t):
        if t < n_w:
            a, j = chains[t]
            return [(slice(a * BLOCK, (a + 1) * BLOCK), hsl[j * grp + g], slice(g * BLOCK, (g + 1) * BLOCK))
                    for g in range(grp)]
        h = t - n_w
        return [(slice(0, ATT_TILE), slice(SWA_WIDTH + h * HEAD_DIM, SWA_WIDTH + (h + 1) * HEAD_DIM), slice(0, ATT_TILE))]

    return kc_n, n_w, n_w + N_MEM_HEADS, scores, finish, place


FF_CHUNK = 1024


def _ffn_chunk(h2, w1_ref, w2_ref, c):
    u = jnp.dot(h2, w1_ref[:, c * FF_CHUNK:(c + 1) * FF_CHUNK], preferred_element_type=F32)
    u = jnp.square(jnp.maximum(u, 0.0)).astype(BF16)
    return jnp.dot(u, w2_ref[c * FF_CHUNK:(c + 1) * FF_CHUNK, :], preferred_element_type=F32)


def _mix_rows(x, yr, ys, ym, wo_ref, g2_ref):
    x1 = (x
          + jnp.dot(yr.astype(BF16), wo_ref[0:RWKV_WIDTH, :], preferred_element_type=F32)
          + jnp.dot(ys.astype(BF16), wo_ref[RWKV_WIDTH:RWKV_WIDTH + SWA_WIDTH, :], preferred_element_type=F32)
          + jnp.dot(ym.astype(BF16), wo_ref[RWKV_WIDTH + SWA_WIDTH:, :], preferred_element_type=F32))
    return x1, _rms(x1, g2_ref[...]).astype(BF16)


def _attn_ffn_kernel(tiles_per_seq, qs_ref, qm_ref, kc_ref, kp_ref, vc_ref, vp_ref, mk_ref, mv_ref, bkt_ref, relb_ref,
                     sink_ref, sqg_ref, skg_ref, mqg_ref, x_ref, yr_ref, xs_ref, yrs_ref, yss_ref, yms_ref,
                     wo_ref, g2_ref, w1_ref, w2_ref, o_ref, os_ref, kn_ref, vn_ref, bias_scr, mk_scr, mv_scr, stage_scr):
    s = pl.program_id(0)
    nt = pl.num_programs(0) - 1
    tile = jnp.minimum(s, nt - 1)
    first = lax.rem(tile, tiles_per_seq) == 0
    wslot = lax.rem(s, 2)
    rslot = 1 - wslot

    @pl.when(jnp.logical_and(first, s < nt))
    def _():
        _attn_init(bkt_ref, relb_ref, mk_ref, mv_ref, bias_scr, mk_scr, mv_scr)

    def step(mlp_rows, store):
        kc_n, n_w, n_tasks, scores, finish, place = _attn_tile(
            first, qs_ref[0], qm_ref[0], kc_ref[0], kp_ref[0], vc_ref[0], vp_ref[0],
            sqg_ref[...], skg_ref[...], mqg_ref[...], sink_ref, bias_scr, mk_scr, mv_scr)
        kn_ref[0] = kc_n[ATT_TILE - BLOCK:].T
        vn_ref[0] = vc_ref[0, ATT_TILE - BLOCK:, :].T

        x1, h2 = _mix_rows(*mlp_rows(), wo_ref, g2_ref)
        sc = {t: scores(t) for t in range(n_tasks)}
        ff = _ffn_chunk(h2, w1_ref, w2_ref, 0)
        outs = finish(range(n_w), sc)
        ff = ff + _ffn_chunk(h2, w1_ref, w2_ref, 1)
        outs.update(finish(range(n_w, n_tasks), sc))
        for c in range(2, D_FF // FF_CHUNK):
            ff = ff + _ffn_chunk(h2, w1_ref, w2_ref, c)
        store(x1 + ff)
        for t in range(n_tasks):
            for rows, lanes, src in place(t):
                stage_scr[wslot, rows, lanes] = outs[t][src]

    def sample_rows():
        return xs_ref[:, 0, :], yrs_ref[...], yss_ref[...], yms_ref[...]

    def store_sample(v):
        os_ref[:, 0, :] = v

    def store_tile(v):
        o_ref[...] = v

    def tile_rows():
        staged = stage_scr[rslot]
        return x_ref[...], yr_ref[...], staged[:, :SWA_WIDTH], staged[:, SWA_WIDTH:]

    pl.when(s == 0)(functools.partial(step, sample_rows, store_sample))
    pl.when(s > 0)(functools.partial(step, tile_rows, store_tile))


def _attn_ffn(proj, mk, mv, rel_bias, sinks, sqg, skg, mqg, x2d, yr, sample, wo, g2, w1, w2):
    B, T, _ = proj.shape
    tps = T // ATT_TILE
    nt = B * tps
    bkt = jnp.asarray(_prompt_bucket_row())
    smem = pl.BlockSpec(memory_space=pltpu.SMEM)
    kblk, vblk = COL_SK // SWA_KV_WIDTH, COL_SV // SWA_KV_WIDTH
    att = lambda s: jnp.minimum(s, nt - 1)
    cur = lambda s, c: (att(s) // tps, att(s) % tps, c)
    prev = lambda s, c: (att(s) // tps, jnp.maximum((ATT_TILE // BLOCK) * (att(s) % tps) - 1, 0), c)
    ffn = lambda s: (jnp.maximum(s - 1, 0), 0)
    memb = pl.BlockSpec((1, N_MEM, MEM_WIDTH), lambda s: (att(s) // tps, 0, 0))
    const = lambda shape: pl.BlockSpec(shape, lambda s: (0, 0), pipeline_mode=pl.Buffered(1))
    return pl.pallas_call(
        functools.partial(_attn_ffn_kernel, tps),
        grid=(nt + 1,),
        in_specs=[pl.BlockSpec((1, ATT_TILE, SWA_WIDTH), lambda s: cur(s, COL_SQ // SWA_WIDTH)),
                  pl.BlockSpec((1, ATT_TILE, MEM_WIDTH), lambda s: cur(s, COL_MQ // MEM_WIDTH)),
                  pl.BlockSpec((1, ATT_TILE, SWA_KV_WIDTH), lambda s: cur(s, kblk)),
                  pl.BlockSpec((1, BLOCK, SWA_KV_WIDTH), lambda s: prev(s, kblk)),
                  pl.BlockSpec((1, ATT_TILE, SWA_KV_WIDTH), lambda s: cur(s, vblk)),
                  pl.BlockSpec((1, BLOCK, SWA_KV_WIDTH), lambda s: prev(s, vblk)),
                  memb, memb,
                  _full((8, 2 * BLOCK)), smem, smem,
                  _full((1, HEAD_DIM)), _full((1, HEAD_DIM)), _full((1, HEAD_DIM)),
                  pl.BlockSpec((ATT_TILE, D_MODEL), ffn), pl.BlockSpec((ATT_TILE, RWKV_WIDTH), ffn)]
                 + [_full(a.shape) for a in sample]
                 + [const((D_MODEL, D_MODEL)), _full((1, D_MODEL)), const((D_MODEL, D_FF)), const((D_FF, D_MODEL))],
        out_specs=[pl.BlockSpec((ATT_TILE, D_MODEL), ffn), _full(sample[0].shape)]
                  + [pl.BlockSpec((1, SWA_KV_WIDTH, BLOCK), lambda s: (att(s) // tps, 0, 0))] * 2,
        out_shape=[jax.ShapeDtypeStruct((B * T, D_MODEL), F32), jax.ShapeDtypeStruct(sample[0].shape, F32)]
                  + [jax.ShapeDtypeStruct((B, SWA_KV_WIDTH, BLOCK), F32)] * 2,
        scratch_shapes=[pltpu.VMEM((N_SWA_KV_HEADS, 2 * BLOCK, 2 * BLOCK), F32),
                        pltpu.VMEM((N_MEM_HEADS, N_MEM, HEAD_DIM), BF16),
                        pltpu.VMEM((N_MEM_HEADS, N_MEM, 2 * HEAD_DIM), BF16),
                        pltpu.VMEM((2, ATT_TILE, SWA_WIDTH + MEM_WIDTH), F32)],
        compiler_params=_params("arbitrary"),
        name="attn_ffn",
    )(proj, proj, proj, proj, proj, proj, mk, mv, bkt, rel_bias, sinks, sqg, skg, mqg, x2d, yr, *sample, wo, g2, w1, w2)


def _memory_kv_kernel(mem_ref, g_ref, w_ref, kg_ref, mk_ref, mv_ref, mkt_ref, mvt_ref):
    kv = _dot(_rms(mem_ref[0], g_ref[...]), w_ref[...])
    kg = kg_ref[...]
    mk = jnp.concatenate([_rms(kv[:, h * HEAD_DIM:(h + 1) * HEAD_DIM], kg) for h in range(N_MEM_HEADS)], axis=1)
    mv = kv[:, MEM_WIDTH:]
    mk_ref[0] = mk
    mv_ref[0] = mv
    mkt_ref[0] = mk.T
    mvt_ref[0] = mv.T


def _memory_kv(mem, g, w, kg):
    B = mem.shape[0]
    blk = pl.BlockSpec((1, N_MEM, MEM_WIDTH), lambda b: (b, 0, 0))
    blk_t = pl.BlockSpec((1, MEM_WIDTH, N_MEM), lambda b: (b, 0, 0))
    return pl.pallas_call(
        _memory_kv_kernel,
        grid=(B,),
        in_specs=[pl.BlockSpec((1, N_MEM, D_MODEL), lambda b: (b, 0, 0)),
                  _full((1, D_MODEL)), _full((D_MODEL, 2 * MEM_WIDTH)), _full((1, HEAD_DIM))],
        out_specs=[blk, blk, blk_t, blk_t],
        out_shape=[jax.ShapeDtypeStruct((B, N_MEM, MEM_WIDTH), F32)] * 2
                  + [jax.ShapeDtypeStruct((B, MEM_WIDTH, N_MEM), F32)] * 2,
        compiler_params=_params("arbitrary"),
        name="memory_kv",
    )(mem, g, w, kg)


SEQ_TILE = 8


def _rms_heads(x, g, n_heads):
    ms = jnp.dot((x * x).astype(BF16), _head_blocks(n_heads * HEAD_DIM), preferred_element_type=F32)
    return x * lax.rsqrt(ms * (1.0 / HEAD_DIM) + NORM_EPS) * jnp.concatenate([g] * n_heads, axis=1)


def _decode_attn_kernel(p_ref, kbuf_ref, vbuf_ref, mk_ref, mv_ref, bkt_ref, relb_ref, sink_ref,
                        sqg_ref, skg_ref, mqg_ref, ys_ref, ym_ref, kout_ref, vout_ref, tab_scr):
    grp = N_SWA_HEADS // N_SWA_KV_HEADS

    @pl.when(pl.program_id(0) == 0)
    def _():
        hrow = lax.broadcasted_iota(jnp.int32, (8, WINDOW), 0)
        lane = lax.broadcasted_iota(jnp.int32, (8, WINDOW), 1)
        bias_w = jnp.zeros((8, WINDOW), F32)
        cols = jnp.zeros((8, WINDOW), F32)
        for h in range(N_SWA_HEADS):
            bias_w = jnp.where(hrow == h, _bias_from_buckets(bkt_ref[...], relb_ref, h, 0.0), bias_w)
            cols = jnp.where(jnp.logical_and(hrow == h, lane == 0), relb_ref[0, h], cols)
            cols = jnp.where(jnp.logical_and(hrow == h, lane == 1), sink_ref[0, h], cols)
        tab_scr[0] = bias_w
        tab_scr[1] = cols

    bias_w = tab_scr[0][:N_SWA_HEADS]
    bias_new = tab_scr[1][:N_SWA_HEADS, 0:1]
    sink = tab_scr[1][:N_SWA_HEADS, 1:2]
    rowi = lax.broadcasted_iota(jnp.int32, (SWA_KV_WIDTH, WINDOW), 0)
    lanei = lax.broadcasted_iota(jnp.int32, (SWA_KV_WIDTH, WINDOW), 1)
    eye = rowi == lanei
    NH = N_SWA_HEADS
    own = (lax.broadcasted_iota(jnp.int32, (NH, NH * HEAD_DIM), 1) // HEAD_DIM
           == lax.broadcasted_iota(jnp.int32, (NH, NH * HEAD_DIM), 0))

    p = p_ref[...]
    qn = _rms_heads(p[:, COL_SQ:COL_SQ + SWA_WIDTH], sqg_ref[...], N_SWA_HEADS) * ATTN_SCALE
    kn = _rms_heads(p[:, COL_SK:COL_SK + SWA_KV_WIDTH], skg_ref[...], N_SWA_KV_HEADS)
    vn = p[:, COL_SV:COL_SV + SWA_KV_WIDTH]
    qmn = _rms_heads(p[:, COL_MQ:COL_MQ + MEM_WIDTH], mqg_ref[...], N_MEM_HEADS) * ATTN_SCALE
    rep = lambda x: jnp.concatenate([x[:, j * HEAD_DIM:(j + 1) * HEAD_DIM] for j in range(N_SWA_KV_HEADS)
                                     for _ in range(grp)], axis=1)
    kn_rep, vn_rep = rep(kn), rep(vn)

    B = range(SEQ_TILE)
    dup = lambda c: jnp.concatenate([c[j * HEAD_DIM:(j + 1) * HEAD_DIM] for j in range(N_SWA_KV_HEADS)
                                     for _ in range(grp)], axis=0)
    qd = [jnp.where(own, qn[b:b + 1, :], 0.0) for b in B]
    qmd = [jnp.where(own, qmn[b:b + 1, :], 0.0) for b in B]
    kdup = [dup(kbuf_ref[b]) for b in B]
    vdup = [dup(vbuf_ref[b]) for b in B]
    s = [_dot(qd[b], kdup[b]) + bias_w for b in B]
    sm = [_dot(qmd[b], mk_ref[b]) for b in B]
    s_new = [jnp.sum(qd[b] * kn_rep[b:b + 1, :], -1, keepdims=True) + bias_new for b in B]
    m = [jnp.maximum(jnp.maximum(jnp.max(s[b], -1, keepdims=True), s_new[b]), sink) for b in B]
    e = [jnp.exp(s[b] - m[b]) for b in B]
    e_new = [jnp.exp(s_new[b] - m[b]) for b in B]
    den = [jnp.sum(e[b], -1, keepdims=True) + e_new[b] + jnp.exp(sink - m[b]) for b in B]
    em = [jnp.exp(sm[b] - jnp.max(sm[b], -1, keepdims=True)) for b in B]
    ov = [_dot_nt(e[b], vdup[b]) for b in B]
    omf = [_dot_nt(em[b], mv_ref[b]) for b in B]
    ys_rows = [jnp.sum(jnp.where(own, (ov[b] + e_new[b] * vn_rep[b:b + 1, :]) / den[b], 0.0), 0, keepdims=True)
               for b in B]
    ym_rows = [jnp.sum(jnp.where(own, omf[b] / jnp.sum(em[b], -1, keepdims=True), 0.0), 0, keepdims=True)
               for b in B]
    ys_ref[...] = jnp.concatenate(ys_rows, axis=0)
    ym_ref[...] = jnp.concatenate(ym_rows, axis=0)
    for b in B:
        kn_col = jnp.sum(jnp.where(eye, kn[b:b + 1, :], 0.0), -1, keepdims=True)
        vn_col = jnp.sum(jnp.where(eye, vn[b:b + 1, :], 0.0), -1, keepdims=True)
        kout_ref[b] = jnp.where(lanei == WINDOW - 1, kn_col, pltpu.roll(kbuf_ref[b], WINDOW - 1, axis=1))
        vout_ref[b] = jnp.where(lanei == WINDOW - 1, vn_col, pltpu.roll(vbuf_ref[b], WINDOW - 1, axis=1))


def _decode_attn(proj, kbuf, vbuf, mk, mv, rel_bias, sinks, sqg, skg, mqg):
    n = proj.shape[0]
    bkt = jnp.asarray(_decode_bucket_table())
    smem = pl.BlockSpec(memory_space=pltpu.SMEM)
    win = pl.BlockSpec((SEQ_TILE, WINDOW, SWA_KV_WIDTH), lambda i: (i, 0, 0))
    memb = pl.BlockSpec((SEQ_TILE, N_MEM, MEM_WIDTH), lambda i: (i, 0, 0))
    return pl.pallas_call(
        _decode_attn_kernel,
        grid=(n // SEQ_TILE,),
        in_specs=[pl.BlockSpec((SEQ_TILE, IN_PROJ), lambda i: (i, 0)), win, win, memb, memb,
                  _full((8, WINDOW)), smem, smem, _full((1, HEAD_DIM)), _full((1, HEAD_DIM)), _full((1, HEAD_DIM))],
        out_specs=[pl.BlockSpec((SEQ_TILE, SWA_WIDTH), lambda i: (i, 0)),
                   pl.BlockSpec((SEQ_TILE, MEM_WIDTH), lambda i: (i, 0)), win, win],
        out_shape=[jax.ShapeDtypeStruct((n, SWA_WIDTH), F32), jax.ShapeDtypeStruct((n, MEM_WIDTH), F32),
                   jax.ShapeDtypeStruct(kbuf.shape, F32), jax.ShapeDtypeStruct(vbuf.shape, F32)],
        scratch_shapes=[pltpu.VMEM((2, 8, WINDOW), F32)],
        compiler_params=_params("arbitrary"),
        name="decode_attn",
    )(proj, kbuf, vbuf, mk, mv, bkt, rel_bias, sinks, sqg, skg, mqg)


def kernel(x_prompt, x_sample, state_rwkv, state_shift, cache_swa_k, cache_swa_v, cache_mem_k, cache_mem_v,
           mem_prompt, rel_bias, norm1_g, w_in, mu_shift, w0, w_up_w, a0, w_up_a, w_up_g, k_k, k_a, r_k,
           lnx_w, lnx_b, q_norm_swa, k_norm_swa, sinks, mem_norm_g, w_mem_kv, q_norm_mem, k_norm_mem,
           w_out, norm2_g, w_ff1, w_ff2):
    B, T, _ = x_prompt.shape
    Bd = x_sample.shape[0]
    l = 0
    rwkv_params = (mu_shift[l][None], w0[l][None], a0[l][None], k_k[l][None], k_a[l][None],
                   r_k[l].reshape(1, RWKV_WIDTH), lnx_w[l][None], lnx_b[l][None],
                   w_up_w[l], w_up_a[l], w_up_g[l])
    sqg, skg, mqg, mkg = q_norm_swa[l][None], k_norm_swa[l][None], q_norm_mem[l][None], k_norm_mem[l][None]
    g1, g2 = norm1_g[l][None], norm2_g[l][None]

    xp = x_prompt.reshape(B * T, D_MODEL)
    xs = x_sample
    proj_p, proj_s, proj_s_rwkv, (w_out_b, w1_b, w2_b) = _in_proj(
        xp, xs, g1, w_in[l], (w_out[l], w_ff1[l], w_ff2[l]), IN_PROJ_TILE)
    proj_p = proj_p.reshape(B, T, IN_PROJ)

    mk, mv, mk_t, mv_t = _memory_kv(mem_prompt, mem_norm_g[l][None], w_mem_kv[l], mkg)
    yr_p, s_p = _rwkv_prompt(proj_p, *rwkv_params)
    shift_p = proj_p[:, T - 1, :RWKV_PROJ]

    yr_s, st_s = _rwkv_step(proj_s_rwkv, state_shift[l], jnp.transpose(state_rwkv[l], (1, 2, 3, 0)), *rwkv_params)
    s_s = jnp.transpose(st_s, (3, 0, 1, 2))
    fmajor = lambda c: jnp.transpose(c, (0, 2, 3, 1)).reshape(Bd, c.shape[2] * HEAD_DIM, c.shape[1])
    ys_s, ym_s, kb_s, vb_s = _decode_attn(
        proj_s, fmajor(cache_swa_k[l]), fmajor(cache_swa_v[l]), fmajor(cache_mem_k[l]), fmajor(cache_mem_v[l]),
        rel_bias, sinks[l][None], sqg, skg, mqg)
    pmajor = lambda c: jnp.transpose(c.reshape(c.shape[0], -1, HEAD_DIM, c.shape[2]), (0, 3, 1, 2))[None]

    y_p, y_s, kn_p, vn_p = _attn_ffn(proj_p, mk, mv, rel_bias, sinks[l][None], sqg, skg, mqg,
                               xp, yr_p.reshape(B * T, RWKV_WIDTH), (xs, yr_s, ys_s, ym_s), w_out_b, g2, w1_b, w2_b)
    y_p = y_p.reshape(B, T, D_MODEL)

    return (y_p, y_s,
            s_p[None], shift_p[None],
            pmajor(kn_p), pmajor(vn_p), pmajor(mk_t), pmajor(mv_t),
            s_s[None], proj_s_rwkv[None],
            pmajor(kb_s), pmajor(vb_s))
```

```python
import functools
import math

import numpy as np
import jax
import jax.numpy as jnp
from jax import lax
from jax.experimental import pallas as pl
from jax.experimental.pallas import tpu as pltpu

F32 = jnp.float32
BF16 = jnp.bfloat16

D_MODEL = 1024
HEAD_DIM = 64
RWKV_WIDTH = 512
N_RWKV_HEADS = 8
SWA_WIDTH = 256
N_SWA_HEADS = 4
N_SWA_KV_HEADS = 2
SWA_KV_WIDTH = 128
MEM_WIDTH = 256
N_MEM_HEADS = 4
N_MEM = 256
WINDOW = 128
BLOCK = 128
N_BUCKETS = 32
MAX_DISTANCE = 128
DECAY_LORA = 64
AAA_LORA = 64
GATE_LORA = 128
RWKV_PROJ = 3 * RWKV_WIDTH + DECAY_LORA + AAA_LORA + GATE_LORA
SWA_PROJ = SWA_WIDTH + 2 * SWA_KV_WIDTH
IN_PROJ = RWKV_PROJ + SWA_PROJ + MEM_WIDTH
D_FF = 4 * D_MODEL
NORM_EPS = 1e-6
LNX_EPS = 64e-5
ATTN_SCALE = HEAD_DIM ** -0.5
EXP_M05 = math.exp(-0.5)
LOG2E = math.log2(math.e)
NEG = -1e30

COL_R, COL_K, COL_V = 0, RWKV_WIDTH, 2 * RWKV_WIDTH
COL_WD = 3 * RWKV_WIDTH
COL_AD = COL_WD + DECAY_LORA
COL_GD = COL_AD + AAA_LORA
COL_SQ = RWKV_PROJ
COL_SK = COL_SQ + SWA_WIDTH
COL_SV = COL_SK + SWA_KV_WIDTH
COL_MQ = RWKV_PROJ + SWA_PROJ

CHUNK = 64
VMEM_LIMIT = 56 * 1024 * 1024


def _dot(a, b):
    return jnp.dot(a.astype(BF16), b.astype(BF16), preferred_element_type=F32)


def _dot_nt(a, b):
    return lax.dot_general(a.astype(BF16), b.astype(BF16), (((1,), (1,)), ((), ())),
                           preferred_element_type=F32)


def _dot_tn(a, b):
    return lax.dot_general(a.astype(BF16), b.astype(BF16), (((0,), (0,)), ((), ())),
                           preferred_element_type=F32)


def _rms(x, g):
    return x * lax.rsqrt(jnp.mean(x * x, -1, keepdims=True) + NORM_EPS) * g


def _params(*sem):
    return pltpu.CompilerParams(dimension_semantics=sem, vmem_limit_bytes=VMEM_LIMIT)


def _full(shape):
    n = len(shape)
    return pl.BlockSpec(shape, lambda *_: (0,) * n)


def _head_blocks(width):
    bi = lax.broadcasted_iota(jnp.int32, (width, width), 0) // HEAD_DIM
    bj = lax.broadcasted_iota(jnp.int32, (width, width), 1) // HEAD_DIM
    return jnp.where(bi == bj, 1.0, 0.0).astype(BF16)


IN_PROJ_TILE = 1024
IN_PROJ_SUB = 256


def _in_proj_kernel(x_ref, xs_ref, g_ref, w_ref, wo_ref, w1_ref, w2_ref, o_ref, os_ref, osr_ref, wob_ref, w1b_ref,
                    w2b_ref, wb_scr):
    @pl.when(pl.program_id(0) == 0)
    def _():
        wb_scr[...] = w_ref[...].astype(BF16)

    wob_ref[...] = wo_ref[...].astype(BF16)
    w1b_ref[...] = w1_ref[...].astype(BF16)
    w2b_ref[...] = w2_ref[...].astype(BF16)

    tm = x_ref.shape[0]
    for j in range(tm // IN_PROJ_SUB):
        rows = slice(j * IN_PROJ_SUB, (j + 1) * IN_PROJ_SUB)
        h = _rms(x_ref[rows, :], g_ref[...])
        o_ref[rows, :] = jnp.dot(h.astype(BF16), wb_scr[...], preferred_element_type=F32)

    @pl.when(pl.program_id(0) == pl.num_programs(0) - 1)
    def _():
        hs = _rms(xs_ref[:, 0, :], g_ref[...])
        ps = jnp.dot(hs.astype(BF16), wb_scr[...], preferred_element_type=F32)
        os_ref[...] = ps
        osr_ref[...] = ps[:, :RWKV_PROJ]


def _in_proj(x2d, xs, g, w, later_weights, tm):
    n, ns = x2d.shape[0], xs.shape[0]
    steps = n // tm
    slab = lambda m: pl.BlockSpec((m.shape[0] // steps, m.shape[1]), lambda i: (i, 0))
    proj_p, proj_s, proj_s_rwkv, *later_bf16 = pl.pallas_call(
        _in_proj_kernel,
        grid=(steps,),
        in_specs=[pl.BlockSpec((tm, D_MODEL), lambda i: (i, 0)),
                  _full((ns, 1, D_MODEL)),
                  _full((1, D_MODEL)),
                  pl.BlockSpec((D_MODEL, IN_PROJ), lambda i: (0, 0), pipeline_mode=pl.Buffered(1))]
                 + [slab(m) for m in later_weights],
        out_specs=[pl.BlockSpec((tm, IN_PROJ), lambda i: (i, 0)), _full((ns, IN_PROJ)), _full((ns, RWKV_PROJ))]
                  + [slab(m) for m in later_weights],
        out_shape=[jax.ShapeDtypeStruct((n, IN_PROJ), F32), jax.ShapeDtypeStruct((ns, IN_PROJ), F32),
                   jax.ShapeDtypeStruct((ns, RWKV_PROJ), F32)]
                  + [jax.ShapeDtypeStruct(m.shape, BF16) for m in later_weights],
        scratch_shapes=[pltpu.VMEM((D_MODEL, IN_PROJ), BF16)],
        compiler_params=_params("arbitrary"),
        name="in_proj",
    )(x2d, xs, g, w, *later_weights)
    return proj_p, proj_s, proj_s_rwkv, later_bf16


def _rwkv_features(xs, w0, a0, k_k, k_a, wupw, wupa, wupg):
    r = xs[:, COL_R:COL_R + RWKV_WIDTH]
    k = xs[:, COL_K:COL_K + RWKV_WIDTH]
    v = xs[:, COL_V:COL_V + RWKV_WIDTH]
    wd = xs[:, COL_WD:COL_WD + DECAY_LORA]
    ad = xs[:, COL_AD:COL_AD + AAA_LORA]
    gd = xs[:, COL_GD:COL_GD + GATE_LORA]
    logw = -jax.nn.sigmoid(w0 + _dot(jnp.tanh(wd), wupw)) * EXP_M05
    a_sig = jax.nn.sigmoid(a0 + _dot(ad, wupa))
    gate = _dot(jax.nn.sigmoid(gd), wupg)
    kk = k * k_k
    k2 = k * (1.0 + (a_sig - 1.0) * k_a)
    return r, k2, v, kk, a_sig, logw, gate


def _seg_sum(x, blk):
    xb = x.astype(BF16)
    half = RWKV_WIDTH // 2
    return jnp.concatenate([jnp.dot(xb[:, :half], blk, preferred_element_type=F32),
                            jnp.dot(xb[:, half:], blk, preferred_element_type=F32)], axis=1)


def _group_norm_out(y, bonus, gate, lnx_w, lnx_b, blk):
    inv_d = 1.0 / HEAD_DIM
    m = _seg_sum(y, blk) * inv_d
    d = y - m
    var = _seg_sum(d * d, blk) * inv_d
    yn = d * lax.rsqrt(var + LNX_EPS) * lnx_w + lnx_b
    return (yn + bonus) * gate


RWKV_TILE = 4 * CHUNK
PAIR = 2 * HEAD_DIM
N_PAIRS = N_RWKV_HEADS // 2


def _rwkv_prompt_kernel(p_ref, mu_ref, w0_ref, a0_ref, kk_ref, ka_ref, rk_ref, lnw_ref, lnb_ref,
                        wupw_ref, wupa_ref, wupg_ref, y_ref, sout_ref, shift_ref, s_scr, prev_scr):
    C, TT, D = CHUNK, RWKV_TILE, HEAD_DIM
    NC = TT // C
    t = pl.program_id(1)

    @pl.when(t == 0)
    def _():
        s_scr[...] = jnp.zeros_like(s_scr)
        prev_scr[...] = jnp.zeros_like(prev_scr)

    p = p_ref[0]
    row = lax.broadcasted_iota(jnp.int32, p.shape, 0)
    prev = jnp.where(row == 0, prev_scr[...], pltpu.roll(p, 1, axis=0))
    prev_scr[...] = p[TT - 1:TT, :]
    xs = p + (prev - p) * mu_ref[...]
    r, k2, v, kk, a_sig, logw, gate = _rwkv_features(
        xs, w0_ref[...], a0_ref[...], kk_ref[...], ka_ref[...], wupw_ref[...], wupa_ref[...], wupg_ref[...])

    blk = _head_blocks(RWKV_WIDTH // 2)
    kkn = kk / jnp.maximum(jnp.sqrt(_seg_sum(kk * kk, blk)), 1e-12)
    bb = kkn * a_sig

    ri = lax.broadcasted_iota(jnp.int32, (TT, TT), 0)
    ci = lax.broadcasted_iota(jnp.int32, (TT, TT), 1)
    tri = jnp.where(jnp.logical_and(ri >= ci, ri // C == ci // C), 1.0, 0.0).astype(BF16)
    lw2 = logw * LOG2E
    l1 = lw2.astype(BF16)
    l2 = (lw2 - l1.astype(F32)).astype(BF16)
    cum = jnp.dot(tri, l1, preferred_element_type=F32) + jnp.dot(tri, l2, preferred_element_type=F32)
    c_last = jnp.concatenate([jnp.broadcast_to(cum[(c + 1) * C - 1:(c + 1) * C, :], (C, RWKV_WIDTH))
                              for c in range(NC)], axis=0)
    e_pos = jnp.exp2(cum)
    e_neg = jnp.exp2(-cum)
    e_prev = jnp.exp2(cum - lw2)
    e_last = jnp.exp2(c_last - cum)

    lo_full = (lax.broadcasted_iota(jnp.int32, (TT, RWKV_WIDTH), 1) % PAIR) < D
    at_f = -kkn * e_prev
    rt_f = r * e_pos
    at_lo = jnp.where(lo_full, at_f, 0.0).astype(BF16)
    at_hi = jnp.where(lo_full, 0.0, at_f).astype(BF16)
    rt_lo = jnp.where(lo_full, rt_f, 0.0).astype(BF16)
    rt_hi = jnp.where(lo_full, 0.0, rt_f).astype(BF16)
    bt_b = (bb * e_neg).astype(BF16)
    kt_b = (k2 * e_neg).astype(BF16)
    bh_b = (bb * e_last).astype(BF16)
    kh_b = (k2 * e_last).astype(BF16)
    v_b = v.astype(BF16)

    r2 = lax.broadcasted_iota(jnp.int32, (C, PAIR), 0)
    c2 = lax.broadcasted_iota(jnp.int32, (C, PAIR), 1)
    lo = c2 < D
    c2m = jnp.where(lo, c2, c2 - C)
    mask_a = jnp.logical_and(lo, c2 < r2)
    mask_ak = jnp.logical_and(jnp.logical_not(lo), c2m < r2)
    mask_r = c2m <= r2
    eye_hi = jnp.where(jnp.logical_and(jnp.logical_not(lo), c2m == r2), 1.0, 0.0)
    zeros_cp = jnp.zeros((C, PAIR), F32)
    qi = lax.broadcasted_iota(jnp.int32, (PAIR, PAIR), 0) // D
    qj = lax.broadcasted_iota(jnp.int32, (PAIR, PAIR), 1) // D
    diag = qi == qj

    PR = [(c, q) for c in range(NC) for q in range(N_PAIRS)]
    n = len(PR)
    E = range(2)
    win = lambda x, c, q: x[c * C:(c + 1) * C, q * PAIR:(q + 1) * PAIR]
    sc = [_dot_nt(jnp.concatenate([win(at_lo, c, q), win(at_hi, c, q), win(rt_lo, c, q), win(rt_hi, c, q)], axis=0),
                  jnp.concatenate([win(bt_b, c, q), win(kt_b, c, q)], axis=0)) for c, q in PR]
    vr = [pltpu.roll(win(v, c, q), D, axis=1) for c, q in PR]
    vvr = [jnp.concatenate([vr[i], vr[i]], axis=0).astype(BF16) for i in range(n)]
    m_ak = [[jnp.where(mask_ak, sc[i][e * C:(e + 1) * C], 0.0) for e in E] for i in range(n)]
    m_r = [[jnp.where(mask_r, sc[i][(2 + e) * C:(3 + e) * C], 0.0) for e in E] for i in range(n)]

    zf = [[_dot(m_ak[i][e], vvr[i]) for e in E] for i in range(n)]
    W = [[jnp.where(mask_a, sc[i][e * C:(e + 1) * C], eye_hi) for e in E] for i in range(n)]
    for k in range(int(math.log2(C))):
        Wb = [[W[i][e].astype(BF16) for e in E] for i in range(n)]
        AW = [[jnp.dot(Wb[i][e][:, :C], Wb[i][e], preferred_element_type=F32) for e in E] for i in range(n)]
        W = [[jnp.where(lo, 0.0, W[i][e]) + AW[i][e] for e in E] for i in range(n)]
    X = [[_dot(W[i][0], jnp.concatenate([zeros_cp, jnp.where(lo, win(at_f, c, q), zf[i][0])], axis=0)),
          _dot(W[i][1], jnp.concatenate([zeros_cp, jnp.where(lo, zf[i][1], win(at_f, c, q))], axis=0))]
         for i, (c, q) in enumerate(PR)]

    S = [s_scr[q] for q in range(N_PAIRS)]
    ys = []
    for c in range(NC):
        w_last = jnp.exp2(cum[(c + 1) * C - 1:(c + 1) * C, :])
        idx = [c * N_PAIRS + q for q in range(N_PAIRS)]
        mkg = [[_dot_tn(X[i][e], win(bh_b, c, q)) for e in E] for q, i in enumerate(idx)]
        vk = [_dot_tn(win(v_b, c, q), win(kh_b, c, q)) for q, i in enumerate(idx)]
        ry = [[_dot(m_r[i][0], jnp.concatenate([X[i][0], jnp.where(lo, 0.0, vr[i])], axis=0)),
               _dot(m_r[i][1], jnp.concatenate([X[i][1], jnp.where(lo, vr[i], 0.0)], axis=0))] for i in idx]
        rp = [win(rt_f, c, q) + jnp.where(lo, ry[q][0], ry[q][1]) for q in range(N_PAIRS)]
        y0 = [pltpu.roll(jnp.where(lo, ry[q][1], ry[q][0]), D, axis=1) for q in range(N_PAIRS)]
        mk = [jnp.where(diag, jnp.concatenate([mkg[q][0][:D], mkg[q][1][D:]], axis=0), 0.0) for q in range(N_PAIRS)]
        g = [jnp.where(diag, vk[q] + jnp.concatenate([mkg[q][0][D:], mkg[q][1][:D]], axis=0), 0.0)
             for q in range(N_PAIRS)]
        y = [_dot_nt(rp[q], S[q]) for q in range(N_PAIRS)]
        dS = [_dot(S[q], mk[q]) for q in range(N_PAIRS)]
        S = [S[q] * w_last[:, q * PAIR:(q + 1) * PAIR] + dS[q] + g[q] for q in range(N_PAIRS)]
        ys.append(jnp.concatenate([y[q] + y0[q] for q in range(N_PAIRS)], axis=1))
    for q in range(N_PAIRS):
        s_scr[q] = S[q]

    bonus = _seg_sum(r * k2 * rk_ref[...], blk) * v
    y_ref[0] = _group_norm_out(jnp.concatenate(ys, axis=0), bonus, gate, lnw_ref[...], lnb_ref[...], blk)

    @pl.when(t == pl.num_programs(1) - 1)
    def _():
        for q in range(N_PAIRS):
            sout_ref[0, 2 * q] = S[q][:D, :D]
            sout_ref[0, 2 * q + 1] = S[q][D:, D:]
        shift_ref[0] = p[TT - 1:TT, :]


def _rwkv_prompt(proj, mu, w0, a0, k_k, k_a, r_k, lnx_w, lnx_b, wupw, wupa, wupg):
    B, T, _ = proj.shape
    vec = lambda n: _full((1, n))
    return pl.pallas_call(
        _rwkv_prompt_kernel,
        grid=(B, T // RWKV_TILE),
        in_specs=[pl.BlockSpec((1, RWKV_TILE, RWKV_PROJ), lambda b, t: (b, t, 0)),
                  vec(RWKV_PROJ), vec(RWKV_WIDTH), vec(RWKV_WIDTH), vec(RWKV_WIDTH), vec(RWKV_WIDTH),
                  vec(RWKV_WIDTH), vec(RWKV_WIDTH), vec(RWKV_WIDTH),
                  _full((DECAY_LORA, RWKV_WIDTH)), _full((AAA_LORA, RWKV_WIDTH)), _full((GATE_LORA, RWKV_WIDTH))],
        out_specs=[pl.BlockSpec((1, RWKV_TILE, RWKV_WIDTH), lambda b, t: (b, t, 0)),
                   pl.BlockSpec((1, N_RWKV_HEADS, HEAD_DIM, HEAD_DIM), lambda b, t: (b, 0, 0, 0)),
                   pl.BlockSpec((1, 1, RWKV_PROJ), lambda b, t: (b, 0, 0))],
        out_shape=[jax.ShapeDtypeStruct((B, T, RWKV_WIDTH), F32),
                   jax.ShapeDtypeStruct((B, N_RWKV_HEADS, HEAD_DIM, HEAD_DIM), F32),
                   jax.ShapeDtypeStruct((B, 1, RWKV_PROJ), F32)],
        scratch_shapes=[pltpu.VMEM((N_PAIRS, PAIR, PAIR), F32),
                        pltpu.VMEM((1, RWKV_PROJ), F32)],
        compiler_params=_params("arbitrary", "arbitrary"),
        name="rwkv_prompt",
    )(proj, mu, w0, a0, k_k, k_a, r_k, lnx_w, lnx_b, wupw, wupa, wupg)


def _rwkv_step_kernel(p_ref, sh_ref, mu_ref, w0_ref, a0_ref, kk_ref, ka_ref, rk_ref, lnw_ref, lnb_ref,
                      wupw_ref, wupa_ref, wupg_ref, s_ref, sout_ref, y_ref, vecs_scr, bonus_scr, gate_scr, yt_scr):
    h = pl.program_id(0)
    blk = _head_blocks(RWKV_WIDTH // 2)

    @pl.when(h == 0)
    def _():
        p = p_ref[...]
        xs = p + (sh_ref[...] - p) * mu_ref[...]
        r, k2, v, kk, a_sig, logw, gate = _rwkv_features(
            xs, w0_ref[...], a0_ref[...], kk_ref[...], ka_ref[...], wupw_ref[...], wupa_ref[...], wupg_ref[...])
        kkn = kk / jnp.maximum(jnp.sqrt(_seg_sum(kk * kk, blk)), 1e-12)
        for i, x in enumerate((-kkn, kkn * a_sig, jnp.exp(logw), k2, r, v)):
            vecs_scr[i] = x.T
        bonus_scr[...] = _seg_sum(r * k2 * rk_ref[...], blk) * v
        gate_scr[...] = gate

    rows = pl.ds(pl.multiple_of(h * HEAD_DIM, HEAD_DIM), HEAD_DIM)
    S = s_ref[0]
    a, b, w, k, r, v = (vecs_scr[i, rows, :] for i in range(6))
    sa = jnp.sum(S * a[None], axis=1)
    S = S * w[None] + sa[:, None, :] * b[None] + v[:, None, :] * k[None]
    sout_ref[0] = S
    yt_scr[rows, :] = jnp.sum(S * r[None], axis=1)

    @pl.when(h == pl.num_programs(0) - 1)
    def _():
        y_ref[...] = _group_norm_out(yt_scr[...].T, bonus_scr[...], gate_scr[...], lnw_ref[...], lnb_ref[...], blk)


def _rwkv_step(proj, shift, state_t, mu, w0, a0, k_k, k_a, r_k, lnx_w, lnx_b, wupw, wupa, wupg):
    n = proj.shape[0]
    vec = lambda m: _full((1, m))
    rows = _full((n, RWKV_PROJ))
    wide = _full((n, RWKV_WIDTH))
    st_spec = pl.BlockSpec((1, HEAD_DIM, HEAD_DIM, n), lambda h: (h, 0, 0, 0))
    state_new, y = pl.pallas_call(
        _rwkv_step_kernel,
        grid=(N_RWKV_HEADS,),
        in_specs=[rows, rows, vec(RWKV_PROJ)] + [vec(RWKV_WIDTH)] * 7
                 + [_full((DECAY_LORA, RWKV_WIDTH)), _full((AAA_LORA, RWKV_WIDTH)), _full((GATE_LORA, RWKV_WIDTH)),
                    st_spec],
        out_specs=[st_spec, wide],
        out_shape=[jax.ShapeDtypeStruct(state_t.shape, F32), jax.ShapeDtypeStruct((n, RWKV_WIDTH), F32)],
        scratch_shapes=[pltpu.VMEM((6, RWKV_WIDTH, n), F32), pltpu.VMEM((n, RWKV_WIDTH), F32),
                        pltpu.VMEM((n, RWKV_WIDTH), F32), pltpu.VMEM((RWKV_WIDTH, n), F32)],
        compiler_params=_params("arbitrary"),
        name="rwkv_step",
    )(proj, shift, mu, w0, a0, k_k, k_a, r_k, lnx_w, lnx_b, wupw, wupa, wupg, state_t)
    return y, state_new


def _t5_bucket_np(dist):
    max_exact = N_BUCKETS // 2
    d = np.maximum(dist, 1).astype(np.float32)
    large = max_exact + (np.log(d / np.float32(max_exact)) / np.float32(math.log(MAX_DISTANCE / max_exact))
                         * np.float32(N_BUCKETS - max_exact)).astype(np.int32)
    large = np.minimum(large, N_BUCKETS - 1)
    return np.where(dist < max_exact, dist, large).astype(np.int32)


def _prompt_bucket_table():
    qi = np.arange(BLOCK)[:, None]
    kj = np.arange(2 * BLOCK)[None, :]
    dist = BLOCK + qi - kj
    valid = (dist >= 0) & (dist <= WINDOW)
    return np.where(valid, _t5_bucket_np(np.maximum(dist, 0)), -1).astype(np.int32)


def _prompt_bucket_row():
    table = _prompt_bucket_table()
    assert all(np.array_equal(table[q], np.roll(table[0], q)) for q in range(BLOCK))
    return np.broadcast_to(table[:1], (8, 2 * BLOCK)).copy()


def _decode_bucket_table():
    dist = WINDOW - np.arange(WINDOW)
    return np.broadcast_to(_t5_bucket_np(dist)[None, :], (8, WINDOW)).astype(np.int32).copy()


def _bias_from_buckets(bkt, relb_ref, h, init):
    acc = jnp.full(bkt.shape, init, F32)
    for b in range(N_BUCKETS):
        acc = jnp.where(bkt == b, relb_ref[b, h], acc)
    return acc


ATT_TILE = 4 * BLOCK


def _attn_init(bkt_ref, relb_ref, mk_ref, mv_ref, bias_scr, mk_scr, mv_scr):
    grp = N_SWA_HEADS // N_SWA_KV_HEADS
    ones = jnp.ones((N_MEM, HEAD_DIM), F32)
    bkt = bkt_ref[...]
    for j in range(N_SWA_KV_HEADS):
        for g in range(grp):
            row = _bias_from_buckets(bkt, relb_ref, j * grp + g, NEG)[0:1] * LOG2E
            bias_scr[j, g * BLOCK:(g + 1) * BLOCK, :] = pltpu.roll(
                jnp.broadcast_to(row, (BLOCK, 2 * BLOCK)), 0, 1, stride=1, stride_axis=0)
    mk = mk_ref[0]
    mv = mv_ref[0]
    for h in range(N_MEM_HEADS):
        sl = slice(h * HEAD_DIM, (h + 1) * HEAD_DIM)
        mk_scr[h] = mk[:, sl].astype(BF16)
        mv_scr[h] = jnp.concatenate([mv[:, sl], ones], axis=1).astype(BF16)


def _attn_tile(first, qs, qm, kc, kp, vc, vp, sqg, skg, mqg, sink_ref, bias_scr, mk_scr, mv_scr):
    grp = N_SWA_HEADS // N_SWA_KV_HEADS
    ones = jnp.ones((2 * BLOCK, HEAD_DIM), F32)
    rowi = lax.broadcasted_iota(jnp.int32, (2 * BLOCK, 1), 0)
    col = lax.broadcasted_iota(jnp.int32, (2 * BLOCK, 2 * BLOCK), 1)
    pad_mask = jnp.logical_and(first, col < BLOCK)
    hsl = [slice(h * HEAD_DIM, (h + 1) * HEAD_DIM) for h in range(N_SWA_HEADS)]
    qs_n = _rms_heads(qs, sqg, N_SWA_HEADS) * (ATTN_SCALE * LOG2E)
    qm_n = _rms_heads(qm, mqg, N_MEM_HEADS) * (ATTN_SCALE * LOG2E)
    kc_n = _rms_heads(kc, skg, N_SWA_KV_HEADS)
    kp_n = _rms_heads(kp, skg, N_SWA_KV_HEADS)
    chains =[(a, j) for a in range(ATT_TILE // BLOCK) for j in range(N_SWA_KV_HEADS)]
    lhs, keys, vals, sinkcol = [], [], [], []
    for a, j in chains:
        rs = slice(a * BLOCK, (a + 1) * BLOCK)
        lhs.append(jnp.concatenate([qs_n[rs, hsl[j * grp + g]] for g in range(grp)], axis=0))
        if a == 0:
            keys.append(jnp.concatenate([kp_n[:, hsl[j]], kc_n[:BLOCK, hsl[j]]], axis=0))
            vv = jnp.concatenate([vp[:, hsl[j]], vc[:BLOCK, hsl[j]]], axis=0)
        else:
            ks = slice((a - 1) * BLOCK, (a + 1) * BLOCK)
            keys.append(kc_n[ks, hsl[j]])
            vv = vc[ks, hsl[j]]
        vals.append(jnp.concatenate([vv, ones], axis=1))
        sinkcol.append(jnp.where(rowi < BLOCK, sink_ref[0, j * grp], sink_ref[0, j * grp + 1]) * LOG2E)
    qmn = [qm_n[:, hsl[h]] for h in range(N_MEM_HEADS)]

    bias = [bias_scr[j] for j in range(N_SWA_KV_HEADS)]
    mem_k = [mk_scr[h] for h in range(N_MEM_HEADS)]
    mem_v = [mv_scr[h] for h in range(N_MEM_HEADS)]
    n_w = len(chains)

    def scores(t):
        return _dot_nt(lhs[t], keys[t]) if t < n_w else _dot_nt(qmn[t - n_w], mem_k[t - n_w])

    def finish(ts, s):
        e, extra = {}, {}
        for t in ts:
            if t < n_w:
                a, j = chains[t]
                st = s[t] + bias[j]
                if a == 0:
                    st = jnp.where(pad_mask, NEG, st)
                m = jnp.maximum(jnp.max(st, -1, keepdims=True), sinkcol[t])
                e[t], extra[t] = jnp.exp2(st - m), jnp.exp2(sinkcol[t] - m)
            else:
                e[t], extra[t] = jnp.exp2(s[t] - jnp.max(s[t], -1, keepdims=True)), 0.0
        o_full = {t: _dot(e[t], vals[t] if t < n_w else mem_v[t - n_w]) for t in ts}
        return {t: o_full[t][:, :HEAD_DIM] / (o_full[t][:, HEAD_DIM:HEAD_DIM + 1] + extra[t]) for t in ts}

    def place(t):
        if t < n_w:
            a, j = chains[t]
            return [(slice(a * BLOCK, (a + 1) * BLOCK), hsl[j * grp + g], slice(g * BLOCK, (g + 1) * BLOCK))
                    for g in range(grp)]
        h = t - n_w
        return [(slice(0, ATT_TILE), slice(SWA_WIDTH + h * HEAD_DIM, SWA_WIDTH + (h + 1) * HEAD_DIM), slice(0, ATT_TILE))]

    return kc_n, n_w, n_w + N_MEM_HEADS, scores, finish, place


FF_CHUNK = 1024


def _ffn_chunk(h2, w1_ref, w2_ref, c):
    u = jnp.dot(h2, w1_ref[:, c * FF_CHUNK:(c + 1) * FF_CHUNK], preferred_element_type=F32)
    u = jnp.square(jnp.maximum(u, 0.0)).astype(BF16)
    return jnp.dot(u, w2_ref[c * FF_CHUNK:(c + 1) * FF_CHUNK, :], preferred_element_type=F32)


def _mix_rows(x, yr, ys, ym, wo_ref, g2_ref):
    x1 = (x
          + jnp.dot(yr.astype(BF16), wo_ref[0:RWKV_WIDTH, :], preferred_element_type=F32)
          + jnp.dot(ys.astype(BF16), wo_ref[RWKV_WIDTH:RWKV_WIDTH + SWA_WIDTH, :], preferred_element_type=F32)
          + jnp.dot(ym.astype(BF16), wo_ref[RWKV_WIDTH + SWA_WIDTH:, :], preferred_element_type=F32))
    return x1, _rms(x1, g2_ref[...]).astype(BF16)


def _attn_ffn_kernel(tiles_per_seq, qs_ref, qm_ref, kc_ref, kp_ref, vc_ref, vp_ref, mk_ref, mv_ref, bkt_ref, relb_ref,
                     sink_ref, sqg_ref, skg_ref, mqg_ref, x_ref, yr_ref, xs_ref, yrs_ref, yss_ref, yms_ref,
                     wo_ref, g2_ref, w1_ref, w2_ref, o_ref, os_ref, kn_ref, vn_ref, bias_scr, mk_scr, mv_scr, stage_scr):
    s = pl.program_id(0)
    nt = pl.num_programs(0) - 1
    tile = jnp.minimum(s, nt - 1)
    first = lax.rem(tile, tiles_per_seq) == 0
    wslot = lax.rem(s, 2)
    rslot = 1 - wslot

    @pl.when(jnp.logical_and(first, s < nt))
    def _():
        _attn_init(bkt_ref, relb_ref, mk_ref, mv_ref, bias_scr, mk_scr, mv_scr)

    def step(mlp_rows, store):
        kc_n, n_w, n_tasks, scores, finish, place = _attn_tile(
            first, qs_ref[0], qm_ref[0], kc_ref[0], kp_ref[0], vc_ref[0], vp_ref[0],
            sqg_ref[...], skg_ref[...], mqg_ref[...], sink_ref, bias_scr, mk_scr, mv_scr)
        kn_ref[0] = kc_n[ATT_TILE - BLOCK:].T
        vn_ref[0] = vc_ref[0, ATT_TILE - BLOCK:, :].T

        x1, h2 = _mix_rows(*mlp_rows(), wo_ref, g2_ref)
        sc = {t: scores(t) for t in range(n_tasks)}
        ff = _ffn_chunk(h2, w1_ref, w2_ref, 0)
        outs = finish(range(n_w), sc)
        ff = ff + _ffn_chunk(h2, w1_ref, w2_ref, 1)
        outs.update(finish(range(n_w, n_tasks), sc))
        for c in range(2, D_FF // FF_CHUNK):
            ff = ff + _ffn_chunk(h2, w1_ref, w2_ref, c)
        store(x1 + ff)
        for t in range(n_tasks):
            for rows, lanes, src in place(t):
                stage_scr[wslot, rows, lanes] = outs[t][src]

    def sample_rows():
        return xs_ref[:, 0, :], yrs_ref[...], yss_ref[...], yms_ref[...]

    def store_sample(v):
        os_ref[:, 0, :] = v

    def store_tile(v):
        o_ref[...] = v

    def tile_rows():
        staged = stage_scr[rslot]
        return x_ref[...], yr_ref[...], staged[:, :SWA_WIDTH], staged[:, SWA_WIDTH:]

    pl.when(s == 0)(functools.partial(step, sample_rows, store_sample))
    pl.when(s > 0)(functools.partial(step, tile_rows, store_tile))


def _attn_ffn(proj, mk, mv, rel_bias, sinks, sqg, skg, mqg, x2d, yr, sample, wo, g2, w1, w2):
    B, T, _ = proj.shape
    tps = T // ATT_TILE
    nt = B * tps
    bkt = jnp.asarray(_prompt_bucket_row())
    smem = pl.BlockSpec(memory_space=pltpu.SMEM)
    kblk, vblk = COL_SK // SWA_KV_WIDTH, COL_SV // SWA_KV_WIDTH
    att = lambda s: jnp.minimum(s, nt - 1)
    cur = lambda s, c: (att(s) // tps, att(s) % tps, c)
    prev = lambda s, c: (att(s) // tps, jnp.maximum((ATT_TILE // BLOCK) * (att(s) % tps) - 1, 0), c)
    ffn = lambda s: (jnp.maximum(s - 1, 0), 0)
    memb = pl.BlockSpec((1, N_MEM, MEM_WIDTH), lambda s: (att(s) // tps, 0, 0))
    const = lambda shape: pl.BlockSpec(shape, lambda s: (0, 0), pipeline_mode=pl.Buffered(1))
    return pl.pallas_call(
        functools.partial(_attn_ffn_kernel, tps),
        grid=(nt + 1,),
        in_specs=[pl.BlockSpec((1, ATT_TILE, SWA_WIDTH), lambda s: cur(s, COL_SQ // SWA_WIDTH)),
                  pl.BlockSpec((1, ATT_TILE, MEM_WIDTH), lambda s: cur(s, COL_MQ // MEM_WIDTH)),
                  pl.BlockSpec((1, ATT_TILE, SWA_KV_WIDTH), lambda s: cur(s, kblk)),
                  pl.BlockSpec((1, BLOCK, SWA_KV_WIDTH), lambda s: prev(s, kblk)),
                  pl.BlockSpec((1, ATT_TILE, SWA_KV_WIDTH), lambda s: cur(s, vblk)),
                  pl.BlockSpec((1, BLOCK, SWA_KV_WIDTH), lambda s: prev(s, vblk)),
                  memb, memb,
                  _full((8, 2 * BLOCK)), smem, smem,
                  _full((1, HEAD_DIM)), _full((1, HEAD_DIM)), _full((1, HEAD_DIM)),
                  pl.BlockSpec((ATT_TILE, D_MODEL), ffn), pl.BlockSpec((ATT_TILE, RWKV_WIDTH), ffn)]
                 + [_full(a.shape) for a in sample]
                 + [const((D_MODEL, D_MODEL)), _full((1, D_MODEL)), const((D_MODEL, D_FF)), const((D_FF, D_MODEL))],
        out_specs=[pl.BlockSpec((ATT_TILE, D_MODEL), ffn), _full(sample[0].shape)]
                  + [pl.BlockSpec((1, SWA_KV_WIDTH, BLOCK), lambda s: (att(s) // tps, 0, 0))] * 2,
        out_shape=[jax.ShapeDtypeStruct((B * T, D_MODEL), F32), jax.ShapeDtypeStruct(sample[0].shape, F32)]
                  + [jax.ShapeDtypeStruct((B, SWA_KV_WIDTH, BLOCK), F32)] * 2,
        scratch_shapes=[pltpu.VMEM((N_SWA_KV_HEADS, 2 * BLOCK, 2 * BLOCK), F32),
                        pltpu.VMEM((N_MEM_HEADS, N_MEM, HEAD_DIM), BF16),
                        pltpu.VMEM((N_MEM_HEADS, N_MEM, 2 * HEAD_DIM), BF16),
                        pltpu.VMEM((2, ATT_TILE, SWA_WIDTH + MEM_WIDTH), F32)],
        compiler_params=_params("arbitrary"),
        name="attn_ffn",
    )(proj, proj, proj, proj, proj, proj, mk, mv, bkt, rel_bias, sinks, sqg, skg, mqg, x2d, yr, *sample, wo, g2, w1, w2)


def _memory_kv_kernel(mem_ref, g_ref, w_ref, kg_ref, mk_ref, mv_ref, mkt_ref, mvt_ref):
    kv = _dot(_rms(mem_ref[0], g_ref[...]), w_ref[...])
    kg = kg_ref[...]
    mk = jnp.concatenate([_rms(kv[:, h * HEAD_DIM:(h + 1) * HEAD_DIM], kg) for h in range(N_MEM_HEADS)], axis=1)
    mv = kv[:, MEM_WIDTH:]
    mk_ref[0] = mk
    mv_ref[0] = mv
    mkt_ref[0] = mk.T
    mvt_ref[0] = mv.T


def _memory_kv(mem, g, w, kg):
    B = mem.shape[0]
    blk = pl.BlockSpec((1, N_MEM, MEM_WIDTH), lambda b: (b, 0, 0))
    blk_t = pl.BlockSpec((1, MEM_WIDTH, N_MEM), lambda b: (b, 0, 0))
    return pl.pallas_call(
        _memory_kv_kernel,
        grid=(B,),
        in_specs=[pl.BlockSpec((1, N_MEM, D_MODEL), lambda b: (b, 0, 0)),
                  _full((1, D_MODEL)), _full((D_MODEL, 2 * MEM_WIDTH)), _full((1, HEAD_DIM))],
        out_specs=[blk, blk, blk_t, blk_t],
        out_shape=[jax.ShapeDtypeStruct((B, N_MEM, MEM_WIDTH), F32)] * 2
                  + [jax.ShapeDtypeStruct((B, MEM_WIDTH, N_MEM), F32)] * 2,
        compiler_params=_params("arbitrary"),
        name="memory_kv",
    )(mem, g, w, kg)


SEQ_TILE = 8


def _rms_heads(x, g, n_heads):
    ms = jnp.dot((x * x).astype(BF16), _head_blocks(n_heads * HEAD_DIM), preferred_element_type=F32)
    return x * lax.rsqrt(ms * (1.0 / HEAD_DIM) + NORM_EPS) * jnp.concatenate([g] * n_heads, axis=1)


def _decode_attn_kernel(p_ref, kbuf_ref, vbuf_ref, mk_ref, mv_ref, bkt_ref, relb_ref, sink_ref,
                        sqg_ref, skg_ref, mqg_ref, ys_ref, ym_ref, kout_ref, vout_ref, tab_scr):
    grp = N_SWA_HEADS // N_SWA_KV_HEADS

    @pl.when(pl.program_id(0) == 0)
    def _():
        hrow = lax.broadcasted_iota(jnp.int32, (8, WINDOW), 0)
        lane = lax.broadcasted_iota(jnp.int32, (8, WINDOW), 1)
        bias_w = jnp.zeros((8, WINDOW), F32)
        cols = jnp.zeros((8, WINDOW), F32)
        for h in range(N_SWA_HEADS):
            bias_w = jnp.where(hrow == h, _bias_from_buckets(bkt_ref[...], relb_ref, h, 0.0), bias_w)
            cols = jnp.where(jnp.logical_and(hrow == h, lane == 0), relb_ref[0, h], cols)
            cols = jnp.where(jnp.logical_and(hrow == h, lane == 1), sink_ref[0, h], cols)
        tab_scr[0] = bias_w
        tab_scr[1] = cols

    bias_w = tab_scr[0][:N_SWA_HEADS]
    bias_new = tab_scr[1][:N_SWA_HEADS, 0:1]
    sink = tab_scr[1][:N_SWA_HEADS, 1:2]
    rowi = lax.broadcasted_iota(jnp.int32, (SWA_KV_WIDTH, WINDOW), 0)
    lanei = lax.broadcasted_iota(jnp.int32, (SWA_KV_WIDTH, WINDOW), 1)
    eye = rowi == lanei
    NH = N_SWA_HEADS
    own = (lax.broadcasted_iota(jnp.int32, (NH, NH * HEAD_DIM), 1) // HEAD_DIM
           == lax.broadcasted_iota(jnp.int32, (NH, NH * HEAD_DIM), 0))

    p = p_ref[...]
    qn = _rms_heads(p[:, COL_SQ:COL_SQ + SWA_WIDTH], sqg_ref[...], N_SWA_HEADS) * ATTN_SCALE
    kn = _rms_heads(p[:, COL_SK:COL_SK + SWA_KV_WIDTH], skg_ref[...], N_SWA_KV_HEADS)
    vn = p[:, COL_SV:COL_SV + SWA_KV_WIDTH]
    qmn = _rms_heads(p[:, COL_MQ:COL_MQ + MEM_WIDTH], mqg_ref[...], N_MEM_HEADS) * ATTN_SCALE
    rep = lambda x: jnp.concatenate([x[:, j * HEAD_DIM:(j + 1) * HEAD_DIM] for j in range(N_SWA_KV_HEADS)
                                     for _ in range(grp)], axis=1)
    kn_rep, vn_rep = rep(kn), rep(vn)

    B = range(SEQ_TILE)
    dup = lambda c: jnp.concatenate([c[j * HEAD_DIM:(j + 1) * HEAD_DIM] for j in range(N_SWA_KV_HEADS)
                                     for _ in range(grp)], axis=0)
    qd = [jnp.where(own, qn[b:b + 1, :], 0.0) for b in B]
    qmd = [jnp.where(own, qmn[b:b + 1, :], 0.0) for b in B]
    kdup = [dup(kbuf_ref[b]) for b in B]
    vdup = [dup(vbuf_ref[b]) for b in B]
    s = [_dot(qd[b], kdup[b]) + bias_w for b in B]
    sm = [_dot(qmd[b], mk_ref[b]) for b in B]
    s_new = [jnp.sum(qd[b] * kn_rep[b:b + 1, :], -1, keepdims=True) + bias_new for b in B]
    m = [jnp.maximum(jnp.maximum(jnp.max(s[b], -1, keepdims=True), s_new[b]), sink) for b in B]
    e = [jnp.exp(s[b] - m[b]) for b in B]
    e_new = [jnp.exp(s_new[b] - m[b]) for b in B]
    den = [jnp.sum(e[b], -1, keepdims=True) + e_new[b] + jnp.exp(sink - m[b]) for b in B]
    em = [jnp.exp(sm[b] - jnp.max(sm[b], -1, keepdims=True)) for b in B]
    ov = [_dot_nt(e[b], vdup[b]) for b in B]
    omf = [_dot_nt(em[b], mv_ref[b]) for b in B]
    ys_rows = [jnp.sum(jnp.where(own, (ov[b] + e_new[b] * vn_rep[b:b + 1, :]) / den[b], 0.0), 0, keepdims=True)
               for b in B]
    ym_rows = [jnp.sum(jnp.where(own, omf[b] / jnp.sum(em[b], -1, keepdims=True), 0.0), 0, keepdims=True)
               for b in B]
    ys_ref[...] = jnp.concatenate(ys_rows, axis=0)
    ym_ref[...] = jnp.concatenate(ym_rows, axis=0)
    for b in B:
        kn_col = jnp.sum(jnp.where(eye, kn[b:b + 1, :], 0.0), -1, keepdims=True)
        vn_col = jnp.sum(jnp.where(eye, vn[b:b + 1, :], 0.0), -1, keepdims=True)
        kout_ref[b] = jnp.where(lanei == WINDOW - 1, kn_col, pltpu.roll(kbuf_ref[b], WINDOW - 1, axis=1))
        vout_ref[b] = jnp.where(lanei == WINDOW - 1, vn_col, pltpu.roll(vbuf_ref[b], WINDOW - 1, axis=1))


def _decode_attn(proj, kbuf, vbuf, mk, mv, rel_bias, sinks, sqg, skg, mqg):
    n = proj.shape[0]
    bkt = jnp.asarray(_decode_bucket_table())
    smem = pl.BlockSpec(memory_space=pltpu.SMEM)
    win = pl.BlockSpec((SEQ_TILE, WINDOW, SWA_KV_WIDTH), lambda i: (i, 0, 0))
    memb = pl.BlockSpec((SEQ_TILE, N_MEM, MEM_WIDTH), lambda i: (i, 0, 0))
    return pl.pallas_call(
        _decode_attn_kernel,
        grid=(n // SEQ_TILE,),
        in_specs=[pl.BlockSpec((SEQ_TILE, IN_PROJ), lambda i: (i, 0)), win, win, memb, memb,
                  _full((8, WINDOW)), smem, smem, _full((1, HEAD_DIM)), _full((1, HEAD_DIM)), _full((1, HEAD_DIM))],
        out_specs=[pl.BlockSpec((SEQ_TILE, SWA_WIDTH), lambda i: (i, 0)),
                   pl.BlockSpec((SEQ_TILE, MEM_WIDTH), lambda i: (i, 0)), win, win],
        out_shape=[jax.ShapeDtypeStruct((n, SWA_WIDTH), F32), jax.ShapeDtypeStruct((n, MEM_WIDTH), F32),
                   jax.ShapeDtypeStruct(kbuf.shape, F32), jax.ShapeDtypeStruct(vbuf.shape, F32)],
        scratch_shapes=[pltpu.VMEM((2, 8, WINDOW), F32)],
        compiler_params=_params("arbitrary"),
        name="decode_attn",
    )(proj, kbuf, vbuf, mk, mv, bkt, rel_bias, sinks, sqg, skg, mqg)


def kernel(x_prompt, x_sample, state_rwkv, state_shift, cache_swa_k, cache_swa_v, cache_mem_k, cache_mem_v,
           mem_prompt, rel_bias, norm1_g, w_in, mu_shift, w0, w_up_w, a0, w_up_a, w_up_g, k_k, k_a, r_k,
           lnx_w, lnx_b, q_norm_swa, k_norm_swa, sinks, mem_norm_g, w_mem_kv, q_norm_mem, k_norm_mem,
           w_out, norm2_g, w_ff1, w_ff2):
    B, T, _ = x_prompt.shape
    Bd = x_sample.shape[0]
    l = 0
    rwkv_params = (mu_shift[l][None], w0[l][None], a0[l][None], k_k[l][None], k_a[l][None],
                   r_k[l].reshape(1, RWKV_WIDTH), lnx_w[l][None], lnx_b[l][None],
                   w_up_w[l], w_up_a[l], w_up_g[l])
    sqg, skg, mqg, mkg = q_norm_swa[l][None], k_norm_swa[l][None], q_norm_mem[l][None], k_norm_mem[l][None]
    g1, g2 = norm1_g[l][None], norm2_g[l][None]

    xp = x_prompt.reshape(B * T, D_MODEL)
    xs = x_sample
    proj_p, proj_s, proj_s_rwkv, (w_out_b, w1_b, w2_b) = _in_proj(
        xp, xs, g1, w_in[l], (w_out[l], w_ff1[l], w_ff2[l]), IN_PROJ_TILE)
    proj_p = proj_p.reshape(B, T, IN_PROJ)

    mk, mv, mk_t, mv_t = _memory_kv(mem_prompt, mem_norm_g[l][None], w_mem_kv[l], mkg)
    yr_p, s_p, shift_p = _rwkv_prompt(proj_p, *rwkv_params)

    yr_s, st_s = _rwkv_step(proj_s_rwkv, state_shift[l], jnp.transpose(state_rwkv[l], (1, 2, 3, 0)), *rwkv_params)
    s_s = jnp.transpose(st_s, (3, 0, 1, 2))
    fmajor = lambda c: jnp.transpose(c, (0, 2, 3, 1)).reshape(Bd, c.shape[2] * HEAD_DIM, c.shape[1])
    ys_s, ym_s, kb_s, vb_s = _decode_attn(
        proj_s, fmajor(cache_swa_k[l]), fmajor(cache_swa_v[l]), fmajor(cache_mem_k[l]), fmajor(cache_mem_v[l]),
        rel_bias, sinks[l][None], sqg, skg, mqg)
    pmajor = lambda c: jnp.transpose(c.reshape(c.shape[0], -1, HEAD_DIM, c.shape[2]), (0, 3, 1, 2))[None]

    y_p, y_s, kn_p, vn_p = _attn_ffn(proj_p, mk, mv, rel_bias, sinks[l][None], sqg, skg, mqg,
                               xp, yr_p.reshape(B * T, RWKV_WIDTH), (xs, yr_s, ys_s, ym_s), w_out_b, g2, w1_b, w2_b)
    y_p = y_p.reshape(B, T, D_MODEL)

    return (y_p, y_s,
            s_p[None], shift_p.reshape(1, B, RWKV_PROJ),
            pmajor(kn_p), pmajor(vn_p), pmajor(mk_t), pmajor(mv_t),
            s_s[None], proj_s_rwkv[None],
            pmajor(kb_s), pmajor(vb_s))
```
